```python
import math
import jax, jax.numpy as jnp
from jax import lax
import numpy as np

D_MODEL = 1024
BATCH = 4
SEQ = 4096
DEPTH = 2
DEC_BATCH = 128
DEC_SEQ = 1
PAST_LEN = 16384
PAGE_SIZE = 128

HEAD_DIM = 64
M_WIDTH = D_MODEL // 4
M_HEADS = M_WIDTH // HEAD_DIM
M_CHUNK = 64
FORGET_BIAS = 3.0
C_WIDTH = D_MODEL // 4
C_KERNEL = 31
S_WIDTH = D_MODEL // 2
S_Q_HEADS = S_WIDTH // HEAD_DIM
S_KV_HEADS = S_Q_HEADS // 4
S_GROUP = S_Q_HEADS // S_KV_HEADS
S_KV_WIDTH = S_KV_HEADS * HEAD_DIM
WINDOW = 128
N_BUCKETS = 32
MAX_DISTANCE = 128
MIX_WIDTH = M_WIDTH + C_WIDTH + S_WIDTH
IN_SIZES = (M_WIDTH, M_WIDTH, M_WIDTH, M_WIDTH, M_HEADS, M_HEADS,
            C_WIDTH, C_WIDTH, S_WIDTH, S_KV_WIDTH, S_KV_WIDTH)
IN_WIDTH = sum(IN_SIZES)
MEM_LEN = 256
X_HEADS = 4
X_HEAD_DIM = 64
X_WIDTH = X_HEADS * X_HEAD_DIM
D_FF = 256 * ((8 * D_MODEL // 3 + 255) // 256)
FFN_KERNEL = 3
EPS = 1e-6

kernel_name = 'hybrid_mlstm_conv_swa_decode_step'


def in_offsets():
    offs, acc = [], 0
    for s in IN_SIZES[:-1]:
        acc += s
        offs.append(acc)
    return offs


def rmsnorm(x, g):
    xf = x.astype(jnp.float32)
    y = xf * lax.rsqrt(jnp.mean(xf * xf, axis=-1, keepdims=True) + EPS)
    return (y * g.astype(jnp.float32)).astype(x.dtype)


def layernorm(x, g, b):
    xf = x.astype(jnp.float32)
    xc = xf - jnp.mean(xf, axis=-1, keepdims=True)
    y = xc * lax.rsqrt(jnp.mean(xc * xc, axis=-1, keepdims=True) + EPS)
    return (y * g.astype(jnp.float32) + b.astype(jnp.float32)).astype(x.dtype)


def causal_dwconv(hist, x, w, b):
    xe = jnp.concatenate([hist.astype(x.dtype), x], axis=1)
    y = lax.conv_general_dilated(xe, w[:, None, :].astype(x.dtype), window_strides=(1,), padding='VALID',
                                 dimension_numbers=('NWC', 'WIO', 'NWC'), feature_group_count=x.shape[-1])
    return y + b.astype(x.dtype), xe[:, xe.shape[1] - hist.shape[1]:]


def t5_bucket(dist):
    n = jnp.maximum(dist, 0)
    max_exact = N_BUCKETS // 2
    nf = jnp.maximum(n, max_exact).astype(jnp.float32)
    large = max_exact + (jnp.log(nf / max_exact) / math.log(MAX_DISTANCE / max_exact)
                         * (N_BUCKETS - max_exact)).astype(jnp.int32)
    return jnp.where(n < max_exact, n, jnp.minimum(large, N_BUCKETS - 1))


def rel_bias_grid(rel_bias, dist):
    b = rel_bias[t5_bucket(dist)].astype(jnp.float32)
    return jnp.transpose(b, (2, 0, 1)).reshape(S_KV_HEADS, S_GROUP, dist.shape[0], dist.shape[1])


def sink_softmax(s, sinks):
    sk = jnp.broadcast_to(sinks.astype(jnp.float32).reshape(S_KV_HEADS, S_GROUP, 1, 1), s.shape[:-1] + (1,))
    return jax.nn.softmax(jnp.concatenate([s, sk], axis=-1), axis=-1)[..., :-1]


def mlstm_scan(q, k, v, i_pre, f_pre, C0, n0, m0):
    B, H, T, d = q.shape
    L = M_CHUNK if T % M_CHUNK == 0 else T
    nc = T // L

    def to_chunks(a):
        return jnp.moveaxis(a.reshape(a.shape[:2] + (nc, L) + a.shape[3:]), 2, 0)

    logf = jax.nn.log_sigmoid(f_pre)
    xs = (to_chunks(q), to_chunks(k), to_chunks(v), to_chunks(i_pre), to_chunks(logf))
    causal = jnp.tril(jnp.ones((L, L), dtype=bool))

    def step(carry, inp):
        C, n, m = carry
        qc, kc, vc, ic, fc = inp
        b = jnp.cumsum(fc, axis=-1)
        a = b + m[..., None]
        Dm = jnp.where(causal, b[..., :, None] - b[..., None, :] + ic[..., None, :], -jnp.inf)
        m_t = jnp.maximum(a, jnp.max(Dm, axis=-1))
        w_inter = jnp.exp(a - m_t)
        S = jnp.einsum('bhtd,bhsd->bhts', qc, kc) * jnp.exp(Dm - m_t[..., None])
        num = w_inter[..., None] * jnp.einsum('bhtd,bhde->bhte', qc, C) + jnp.einsum('bhts,bhse->bhte', S, vc)
        den = w_inter * jnp.einsum('bhtd,bhd->bht', qc, n) + jnp.sum(S, axis=-1)
        h = num / jnp.maximum(jnp.abs(den), jnp.exp(-m_t))[..., None]
        bL = b[..., -1]
        g = bL[..., None] - b + ic
        m_new = jnp.maximum(bL + m, jnp.max(g, axis=-1))
        decay = jnp.exp(bL + m - m_new)
        wk = jnp.exp(g - m_new[..., None])
        C_new = decay[..., None, None] * C + jnp.einsum('bhs,bhsd,bhse->bhde', wk, kc, vc)
        n_new = decay[..., None] * n + jnp.einsum('bhs,bhsd->bhd', wk, kc)
        return (C_new, n_new, m_new), h

    (C, n, m), hs = lax.scan(step, (C0, n0, m0), xs)
    return jnp.moveaxis(hs, 0, 2).reshape(B, H, T, d), C, n, m


def mlstm_group(q, k, v, o, i_pre, f_pre, norm_g, C0, n0, m0):
    B, T, _ = q.shape
    f32 = jnp.float32

    def heads(a):
        return a.astype(f32).reshape(B, T, M_HEADS, HEAD_DIM).transpose(0, 2, 1, 3)

    h, C, n, m = mlstm_scan(heads(q), heads(k) * (HEAD_DIM ** -0.5), heads(v),
                            i_pre.astype(f32).transpose(0, 2, 1), f_pre.astype(f32).transpose(0, 2, 1),
                            C0, n0, m0)
    h = h * lax.rsqrt(jnp.mean(h * h, axis=-1, keepdims=True) + EPS) * norm_g.astype(f32).reshape(M_HEADS, 1, HEAD_DIM)
    h = h.transpose(0, 2, 1, 3).reshape(B, T, M_WIDTH)
    return (h * jax.nn.sigmoid(o.astype(f32))).astype(q.dtype), C, n, m


def conv_group(a, g, hist, w, b, ln_g, ln_b):
    u = a * jax.nn.sigmoid(g)
    y, new_hist = causal_dwconv(hist, u, w, b)
    return jax.nn.silu(layernorm(y, ln_g, ln_b)), new_hist


def swa_prompt(q, k, v, sinks, rel_bias):
    B, S, _ = q.shape
    nb = S // WINDOW
    k4 = k.reshape(B, S, S_KV_HEADS, HEAD_DIM)
    v4 = v.reshape(B, S, S_KV_HEADS, HEAD_DIM)
    qb = q.reshape(B, nb, WINDOW, S_KV_HEADS, S_GROUP, HEAD_DIM)
    kb = k4.reshape(B, nb, WINDOW, S_KV_HEADS, HEAD_DIM)
    vb = v4.reshape(B, nb, WINDOW, S_KV_HEADS, HEAD_DIM)
    pad = ((0, 0), (1, 0), (0, 0), (0, 0), (0, 0))
    kk = jnp.concatenate([jnp.pad(kb, pad)[:, :-1], kb], axis=2)
    vv = jnp.concatenate([jnp.pad(vb, pad)[:, :-1], vb], axis=2)
    qi = jnp.arange(WINDOW)[:, None]
    ki = jnp.arange(2 * WINDOW)[None, :]
    dist = qi + WINDOW - ki
    band = (dist >= 0) & (dist < WINDOW)
    valid = band[None] & ((jnp.arange(nb) > 0)[:, None, None] | (ki >= WINDOW)[None])
    s = (jnp.einsum('bnqhgd,bnkhd->bnhgqk', qb, kk).astype(jnp.float32) * (HEAD_DIM ** -0.5)
         + rel_bias_grid(rel_bias, dist))
    s = jnp.where(valid[None, :, None, None], s, -jnp.inf)
    p = sink_softmax(s, sinks)
    o = jnp.einsum('bnhgqk,bnkhd->bnqhgd', p.astype(v.dtype), vv).reshape(B, S, S_WIDTH)
    return o, k4[:, S - WINDOW:], v4[:, S - WINDOW:]


def swa_sample(q, k, v, k_cache, v_cache, sinks, rel_bias):
    B, T, _ = q.shape
    Lc = k_cache.shape[1]
    qh = q.reshape(B, T, S_KV_HEADS, S_GROUP, HEAD_DIM)
    kk = jnp.concatenate([k_cache.astype(k.dtype), k.reshape(B, T, S_KV_HEADS, HEAD_DIM)], axis=1)
    vv = jnp.concatenate([v_cache.astype(v.dtype), v.reshape(B, T, S_KV_HEADS, HEAD_DIM)], axis=1)
    dist = (Lc + jnp.arange(T))[:, None] - jnp.arange(Lc + T)[None, :]
    valid = (dist >= 0) & (dist < WINDOW)
    s = (jnp.einsum('bqhgd,bkhd->bhgqk', qh, kk).astype(jnp.float32) * (HEAD_DIM ** -0.5)
         + rel_bias_grid(rel_bias, dist))
    s = jnp.where(valid, s, -jnp.inf)
    p = sink_softmax(s, sinks)
    o = jnp.einsum('bhgqk,bkhd->bqhgd', p.astype(v.dtype), vv).reshape(B, T, S_WIDTH)
    return o, kk[:, T:], vv[:, T:]


def cross_attend(h, mem_k, mem_v, w_q, w_o):
    B, T, _ = h.shape
    q = (h @ w_q).reshape(B, T, X_HEADS, X_HEAD_DIM)
    s = jnp.einsum('bqhd,bkhd->bhqk', q, mem_k.astype(q.dtype)).astype(jnp.float32) * (X_HEAD_DIM ** -0.5)
    p = jax.nn.softmax(s, axis=-1)
    o = jnp.einsum('bhqk,bkhd->bqhd', p.astype(h.dtype), mem_v.astype(h.dtype)).reshape(B, T, X_WIDTH)
    return o @ w_o


def conv_ffn(h, hist, w_up, cw, cb, w_down):
    a, g = jnp.split(h @ w_up, 2, axis=-1)
    g, new_hist = causal_dwconv(hist, g, cw, cb)
    return (jax.nn.silu(g) * a) @ w_down, new_hist


def mixing_block(h, p, C0, n0, m0, conv_hist, swa_cache, rel_bias):
    z = h @ p['w_in']
    mq, mk, mv, mo, mi, mf, ca, cg, sq, sk, sv = jnp.split(z, in_offsets(), axis=-1)
    h_m, C, n, m = mlstm_group(mq, mk, mv, mo, mi + p['b_i'], mf + p['b_f'], p['mlstm_norm_g'], C0, n0, m0)
    h_c, conv_new = conv_group(ca, cg, conv_hist, p['conv_w'], p['conv_b'], p['conv_ln_g'], p['conv_ln_b'])
    if swa_cache is None:
        h_s, k_keep, v_keep = swa_prompt(sq, sk, sv, p['swa_sinks'], rel_bias)
    else:
        h_s, k_keep, v_keep = swa_sample(sq, sk, sv, swa_cache[0], swa_cache[1], p['swa_sinks'], rel_bias)
    out = jnp.concatenate([h_m, h_c, h_s], axis=-1) @ p['w_out']
    return out, (C, n, m, conv_new, k_keep, v_keep)


def decoder_layer(x, p, mem_k, mem_v, C0, n0, m0, conv_hist, ffn_hist, swa_cache, rel_bias):
    mix, st = mixing_block(rmsnorm(x, p['norm1_g']), p, C0, n0, m0, conv_hist, swa_cache, rel_bias)
    x = x + mix
    x = x + cross_attend(rmsnorm(x, p['norm2_g']), mem_k, mem_v, p['w_xq'], p['w_xo'])
    f, ffn_new = conv_ffn(rmsnorm(x, p['norm3_g']), ffn_hist, p['w_up'], p['ffn_conv_w'], p['ffn_conv_b'], p['w_down'])
    return x + f, st + (ffn_new,)


def setup_inputs(seed: int = 0) -> dict:
    key = jax.random.key(seed)
    ks = iter(jax.random.split(key, 48))

    def nrm(shape, scale=1.0):
        return jax.random.normal(next(ks), shape, jnp.float32) * scale

    def gain(shape):
        return 1.0 + nrm(shape, 0.02)

    swa_buf = min(WINDOW, PAST_LEN)
    return {
        'x_prompt': nrm((BATCH, SEQ, D_MODEL)),
        'x_sample': nrm((DEC_BATCH, DEC_SEQ, D_MODEL)),
        'mem_prompt': nrm((BATCH, MEM_LEN, D_MODEL)),
        'state_mlstm_C': nrm((DEPTH, DEC_BATCH, M_HEADS, HEAD_DIM, HEAD_DIM), 0.3),
        'state_mlstm_n': nrm((DEPTH, DEC_BATCH, M_HEADS, HEAD_DIM), 0.3),
        'state_mlstm_m': nrm((DEPTH, DEC_BATCH, M_HEADS), 0.5),
        'state_conv': nrm((DEPTH, DEC_BATCH, C_KERNEL - 1, C_WIDTH), 0.5),
        'cache_swa_k': nrm((DEPTH, DEC_BATCH, swa_buf, S_KV_HEADS, HEAD_DIM)),
        'cache_swa_v': nrm((DEPTH, DEC_BATCH, swa_buf, S_KV_HEADS, HEAD_DIM)),
        'cache_mem_k': nrm((DEPTH, DEC_BATCH, MEM_LEN, X_HEADS, X_HEAD_DIM)),
        'cache_mem_v': nrm((DEPTH, DEC_BATCH, MEM_LEN, X_HEADS, X_HEAD_DIM)),
        'state_ffn_conv': nrm((DEPTH, DEC_BATCH, FFN_KERNEL - 1, D_FF), 0.5),
        'rel_bias': nrm((N_BUCKETS, S_Q_HEADS), 0.5),
        'norm1_g': gain((DEPTH, D_MODEL)),
        'w_in': nrm((DEPTH, D_MODEL, IN_WIDTH), D_MODEL ** -0.5),
        'b_i': nrm((DEPTH, M_HEADS), 0.1),
        'b_f': FORGET_BIAS + nrm((DEPTH, M_HEADS), 0.5),
        'mlstm_norm_g': gain((DEPTH, M_WIDTH)),
        'conv_w': nrm((DEPTH, C_KERNEL, C_WIDTH), C_KERNEL ** -0.5),
        'conv_b': nrm((DEPTH, C_WIDTH), 0.02),
        'conv_ln_g': gain((DEPTH, C_WIDTH)),
        'conv_ln_b': nrm((DEPTH, C_WIDTH), 0.02),
        'swa_sinks': nrm((DEPTH, S_Q_HEADS), 0.5),
        'w_out': nrm((DEPTH, MIX_WIDTH, D_MODEL), MIX_WIDTH ** -0.5),
        'norm2_g': gain((DEPTH, D_MODEL)),
        'w_xq': nrm((DEPTH, D_MODEL, X_WIDTH), D_MODEL ** -0.5),
        'w_xk': nrm((DEPTH, D_MODEL, X_WIDTH), D_MODEL ** -0.5),
        'w_xv': nrm((DEPTH, D_MODEL, X_WIDTH), D_MODEL ** -0.5),
        'w_xo': nrm((DEPTH, X_WIDTH, D_MODEL), X_WIDTH ** -0.5),
        'norm3_g': gain((DEPTH, D_MODEL)),
        'w_up': nrm((DEPTH, D_MODEL, 2 * D_FF), D_MODEL ** -0.5),
        'ffn_conv_w': nrm((DEPTH, FFN_KERNEL, D_FF), FFN_KERNEL ** -0.5),
        'ffn_conv_b': nrm((DEPTH, D_FF), 0.02),
        'w_down': nrm((DEPTH, D_FF, D_MODEL), D_FF ** -0.5),
        'final_norm_g': gain((D_MODEL,)),
    }


def reference(x_prompt, x_sample, mem_prompt, state_mlstm_C, state_mlstm_n, state_mlstm_m, state_conv,
              cache_swa_k, cache_swa_v, cache_mem_k, cache_mem_v, state_ffn_conv, rel_bias,
              norm1_g, w_in, b_i, b_f, mlstm_norm_g, conv_w, conv_b, conv_ln_g, conv_ln_b, swa_sinks, w_out,
              norm2_g, w_xq, w_xk, w_xv, w_xo, norm3_g, w_up, ffn_conv_w, ffn_conv_b, w_down, final_norm_g):
    f32 = jnp.float32
    dt = x_prompt.dtype

    def layer_params(l):
        return {'norm1_g': norm1_g[l], 'w_in': w_in[l], 'b_i': b_i[l], 'b_f': b_f[l],
                'mlstm_norm_g': mlstm_norm_g[l], 'conv_w': conv_w[l], 'conv_b': conv_b[l],
                'conv_ln_g': conv_ln_g[l], 'conv_ln_b': conv_ln_b[l], 'swa_sinks': swa_sinks[l],
                'w_out': w_out[l], 'norm2_g': norm2_g[l], 'w_xq': w_xq[l], 'w_xo': w_xo[l],
                'norm3_g': norm3_g[l], 'w_up': w_up[l], 'ffn_conv_w': ffn_conv_w[l],
                'ffn_conv_b': ffn_conv_b[l], 'w_down': w_down[l]}

    def stack(states, i):
        return jnp.stack([s[i] for s in states]).astype(dt)

    bp = x_prompt.shape[0]
    xp = x_prompt
    prompt_st = []
    for l in range(DEPTH):
        mk = (mem_prompt @ w_xk[l]).reshape(bp, mem_prompt.shape[1], X_HEADS, X_HEAD_DIM)
        mv = (mem_prompt @ w_xv[l]).reshape(bp, mem_prompt.shape[1], X_HEADS, X_HEAD_DIM)
        xp, st = decoder_layer(xp, layer_params(l), mk, mv,
                               jnp.zeros((bp, M_HEADS, HEAD_DIM, HEAD_DIM), f32),
                               jnp.zeros((bp, M_HEADS, HEAD_DIM), f32),
                               jnp.zeros((bp, M_HEADS), f32),
                               jnp.zeros((bp, C_KERNEL - 1, C_WIDTH), dt),
                               jnp.zeros((bp, FFN_KERNEL - 1, D_FF), dt),
                               None, rel_bias)
        prompt_st.append(st + (mk, mv))
    y_prompt = rmsnorm(xp, final_norm_g)

    xs = x_sample
    sample_st = []
    for l in range(DEPTH):
        xs, st = decoder_layer(xs, layer_params(l), cache_mem_k[l], cache_mem_v[l],
                               state_mlstm_C[l].astype(f32), state_mlstm_n[l].astype(f32),
                               state_mlstm_m[l].astype(f32), state_conv[l], state_ffn_conv[l],
                               (cache_swa_k[l], cache_swa_v[l]), rel_bias)
        sample_st.append(st)
    y_sample = rmsnorm(xs, final_norm_g)

    return (y_prompt, y_sample,
            stack(prompt_st, 0), stack(prompt_st, 1), stack(prompt_st, 2), stack(prompt_st, 3),
            stack(prompt_st, 4), stack(prompt_st, 5), stack(prompt_st, 7), stack(prompt_st, 8),
            stack(prompt_st, 6),
            stack(sample_st, 0), stack(sample_st, 1), stack(sample_st, 2), stack(sample_st, 3),
            stack(sample_st, 4), stack(sample_st, 5), stack(sample_st, 6))
```

```python
import functools
import math

import numpy as np
import jax
import jax.numpy as jnp
from jax import lax
from jax.experimental import pallas as pl
from jax.experimental.pallas import tpu as pltpu

F32 = jnp.float32
BF16 = jnp.bfloat16
EPS = 1e-6
NEG_INF = float("-inf")

HEAD_DIM = 64
M_HEADS = 4
M_WIDTH = M_HEADS * HEAD_DIM
C_WIDTH = 256
C_KERNEL = 31
S_Q_HEADS = 8
S_KV_HEADS = 2
S_WIDTH = S_Q_HEADS * HEAD_DIM
S_KV_WIDTH = S_KV_HEADS * HEAD_DIM
WINDOW = 128
N_BUCKETS = 32
MAX_DISTANCE = 128
X_HEADS = 4
X_WIDTH = X_HEADS * HEAD_DIM
FFN_KERNEL = 3
QK_SCALE = HEAD_DIM ** -0.5

Z_M = 0
Z_C = 4 * M_WIDTH
Z_SQ = Z_C + 2 * C_WIDTH
Z_SKV = Z_SQ + S_WIDTH
Z_G = Z_SKV + 2 * S_KV_WIDTH
Z_WIDTH = Z_G + 128

LANES = 128
SUBLANES = 8
VMEM_LIMIT = 56 * 1024 * 1024

M_CHUNK = 128
FFN_CHUNK = 256


def _params(*sem):
    return pltpu.CompilerParams(dimension_semantics=sem, vmem_limit_bytes=VMEM_LIMIT)


def _const_spec(shape):
    nd = len(shape)
    return pl.BlockSpec(shape, lambda *_: (0,) * nd, pipeline_mode=pl.Buffered(1))


def _rms(x, g):
    return x * lax.rsqrt(jnp.mean(x * x, axis=-1, keepdims=True) + EPS) * g


def _sigmoid(x):
    return 1.0 / (1.0 + jnp.exp(-x))


def _log_sigmoid(x):
    return jnp.minimum(x, 0.0) - jnp.log1p(jnp.exp(-jnp.abs(x)))


def _dot(a, b):
    return jnp.dot(a, b, preferred_element_type=F32)


def _dot_nt(a, b):
    return lax.dot_general(a, b, (((1,), (1,)), ((), ())), preferred_element_type=F32)


def _dot_tn(a, b):
    return lax.dot_general(a, b, (((0,), (0,)), ((), ())), preferred_element_type=F32)


def _norm_matmul_kernel(x_ref, g_ref, w_ref, o_ref, *, norm, col_chunk):
    x = x_ref[...]
    h = (_rms(x, g_ref[...]) if norm else x).astype(BF16)
    n = o_ref.shape[1]
    for c in range(0, n, col_chunk):
        w = min(col_chunk, n - c)
        o_ref[:, c:c + w] = _dot(h, w_ref[:, c:c + w])


def _norm_matmul(x, g, w, *, norm, tm):
    m, d = x.shape
    n = w.shape[1]
    tm = min(tm, m)
    return pl.pallas_call(
        functools.partial(_norm_matmul_kernel, norm=norm, col_chunk=512),
        grid=(m // tm,),
        in_specs=[pl.BlockSpec((tm, d), lambda i: (i, 0)), _const_spec((1, d)), _const_spec((d, n))],
        out_specs=pl.BlockSpec((tm, n), lambda i: (i, 0)),
        out_shape=jax.ShapeDtypeStruct((m, n), F32),
        compiler_params=_params("arbitrary"),
        name="norm_matmul" if norm else "matmul",
    )(x, g, w)


def _mlstm_prompt_kernel(z_ref, g_ref, gb_ref, ng_ref, h_ref, cn_ref, m_ref, cn_scr, m_scr):
    L = z_ref.shape[0]

    @pl.when(pl.program_id(1) == 0)
    def _():
        cn_scr[...] = jnp.zeros(cn_scr.shape, F32)
        m_scr[...] = jnp.zeros(m_scr.shape, F32)

    gates = g_ref[...] + gb_ref[...]
    gates_t = gates.T
    row = lax.broadcasted_iota(jnp.int32, (L, L), 0)
    col = lax.broadcasted_iota(jnp.int32, (L, L), 1)
    causal = row >= col
    one_col = jnp.where(lax.broadcasted_iota(jnp.int32, (L, HEAD_DIM), 1) == 0, 1.0, 0.0)

    for h in range(M_HEADS):
        lo = h * HEAD_DIM
        i_col = gates[:, h:h + 1]
        i_row = gates_t[h:h + 1, :]
        f_col = _log_sigmoid(gates[:, M_HEADS + h:M_HEADS + h + 1])
        f_row = _log_sigmoid(gates_t[M_HEADS + h:M_HEADS + h + 1, :])
        b_col = jnp.sum(jnp.where(causal, f_row, 0.0), axis=1, keepdims=True)
        b_row = jnp.sum(jnp.where(causal, 0.0, f_col), axis=0, keepdims=True) + f_row
        m_prev = m_scr[h, 0:1, 0:1]
        a_col = b_col + m_prev
        dm = jnp.where(causal, b_col - b_row + i_row, NEG_INF)
        m_t = jnp.maximum(a_col, jnp.max(dm, axis=1, keepdims=True))
        w_inter = jnp.exp(a_col - m_t)
        q = z_ref[:, lo:lo + HEAD_DIM].astype(BF16)
        k = z_ref[:, M_WIDTH + lo:M_WIDTH + lo + HEAD_DIM] * QK_SCALE
        v = z_ref[:, 2 * M_WIDTH + lo:2 * M_WIDTH + lo + HEAD_DIM]
        o = z_ref[:, 3 * M_WIDTH + lo:3 * M_WIDTH + lo + HEAD_DIM]
        s = _dot_nt(q, k.astype(BF16)) * jnp.exp(dm - m_t)
        v_ext = jnp.concatenate([v, one_col], axis=1).astype(BF16)
        cn = cn_scr[h]
        nd = w_inter * _dot(q, cn.astype(BF16)) + _dot(s.astype(BF16), v_ext)
        num = nd[:, 0:HEAD_DIM]
        den = nd[:, HEAD_DIM:HEAD_DIM + 1]
        hh = num / jnp.maximum(jnp.abs(den), jnp.exp(-m_t))
        hh = _rms(hh, ng_ref[:, lo:lo + HEAD_DIM])
        h_ref[:, lo:lo + HEAD_DIM] = hh * _sigmoid(o)
        b_last = b_col[L - 1:L, :]
        g_col = b_last - b_col + i_col
        m_new = jnp.maximum(b_last + m_prev, jnp.max(g_col, axis=0, keepdims=True))
        decay = jnp.exp(b_last + m_prev - m_new)
        wk = jnp.exp(g_col - m_new)
        cn_scr[h] = decay * cn + _dot_tn((k * wk).astype(BF16), v_ext)
        m_scr[h] = jnp.broadcast_to(m_new, m_scr.shape[1:])

    cn_ref[...] = cn_scr[...]
    m_ref[...] = m_scr[...]


def _mlstm_prompt(z3, gbias, ng):
    b, s, _ = z3.shape
    L = M_CHUNK
    return pl.pallas_call(
        _mlstm_prompt_kernel,
        grid=(b, s // L),
        in_specs=[
            pl.BlockSpec((None, L, 4 * M_WIDTH), lambda i, c: (i, c, Z_M // (4 * M_WIDTH))),
            pl.BlockSpec((None, L, 128), lambda i, c: (i, c, Z_G // 128)),
            _const_spec((1, 128)),
            _const_spec((1, M_WIDTH)),
        ],
        out_specs=[
            pl.BlockSpec((None, L, M_WIDTH), lambda i, c: (i, c, 0)),
            pl.BlockSpec((None, M_HEADS, HEAD_DIM, 128), lambda i, c: (i, 0, 0, 0)),
            pl.BlockSpec((None, M_HEADS, SUBLANES, 128), lambda i, c: (i, 0, 0, 0)),
        ],
        out_shape=[
            jax.ShapeDtypeStruct((b, s, M_WIDTH), F32),
            jax.ShapeDtypeStruct((b, M_HEADS, HEAD_DIM, 128), F32),
            jax.ShapeDtypeStruct((b, M_HEADS, SUBLANES, 128), F32),
        ],
        scratch_shapes=[pltpu.VMEM((M_HEADS, HEAD_DIM, 128), F32), pltpu.VMEM((M_HEADS, SUBLANES, 128), F32)],
        compiler_params=_params("arbitrary", "arbitrary"),
        name="mlstm_prompt",
    )(z3, z3, gbias, ng)


CONV_PAD = 32


def _conv_prompt_kernel(z_ref, w_ref, cb_ref, lg_ref, lb_ref, h_ref, tail_ref, buf):
    tc = z_ref.shape[0]
    t = pl.program_id(1)

    @pl.when(t == 0)
    def _():
        buf[0:CONV_PAD, :] = jnp.zeros((CONV_PAD, C_WIDTH), F32)

    @pl.when(t > 0)
    def _():
        buf[0:CONV_PAD, :] = buf[tc:tc + CONV_PAD, :]

    u = z_ref[:, 0:C_WIDTH] * _sigmoid(z_ref[:, C_WIDTH:2 * C_WIDTH])
    buf[CONV_PAD:CONV_PAD + tc, :] = u
    off = CONV_PAD - (C_KERNEL - 1)
    acc = jnp.broadcast_to(cb_ref[...], (tc, C_WIDTH))
    for j in range(C_KERNEL):
        acc = acc + w_ref[j:j + 1, :] * buf[off + j:off + j + tc, :]
    mu = jnp.mean(acc, axis=-1, keepdims=True)
    xc = acc - mu
    y = xc * lax.rsqrt(jnp.mean(xc * xc, axis=-1, keepdims=True) + EPS) * lg_ref[...] + lb_ref[...]
    h_ref[...] = y * _sigmoid(y)
    tail_ref[...] = buf[tc:tc + CONV_PAD, :]


def _conv_prompt(z3, w, cb, lg, lb, *, tc):
    b, s, _ = z3.shape
    return pl.pallas_call(
        _conv_prompt_kernel,
        grid=(b, s // tc),
        in_specs=[
            pl.BlockSpec((None, tc, 2 * C_WIDTH), lambda i, t: (i, t, Z_C // (2 * C_WIDTH))),
            _const_spec((CONV_PAD, C_WIDTH)),
            _const_spec((1, C_WIDTH)), _const_spec((1, C_WIDTH)), _const_spec((1, C_WIDTH)),
        ],
        out_specs=[
            pl.BlockSpec((None, tc, C_WIDTH), lambda i, t: (i, t, 0)),
            pl.BlockSpec((None, CONV_PAD, C_WIDTH), lambda i, t: (i, 0, 0)),
        ],
        out_shape=[
            jax.ShapeDtypeStruct((b, s, C_WIDTH), F32),
            jax.ShapeDtypeStruct((b, CONV_PAD, C_WIDTH), F32),
        ],
        scratch_shapes=[pltpu.VMEM((CONV_PAD + tc, C_WIDTH), F32)],
        compiler_params=_params("arbitrary", "arbitrary"),
        name="conv_prompt",
    )(z3, w, cb, lg, lb)


def _swa_prompt_kernel(q_ref, kv_ref, kvp_ref, bias_ref, sink_ref, o_ref):
    W = WINDOW
    n = pl.program_id(1)
    kk = jnp.concatenate([kvp_ref[:, 0:S_KV_WIDTH], kv_ref[:, 0:S_KV_WIDTH]], axis=0)
    vv = jnp.concatenate([kvp_ref[:, S_KV_WIDTH:2 * S_KV_WIDTH], kv_ref[:, S_KV_WIDTH:2 * S_KV_WIDTH]], axis=0)
    kk_r = pltpu.roll(kk, HEAD_DIM, axis=1)
    vv_r = pltpu.roll(vv, HEAD_DIM, axis=1)
    lo_lane = lax.broadcasted_iota(jnp.int32, (2 * W, S_KV_WIDTH), 1) < HEAD_DIM
    k_a = [jnp.where(lo_lane, kk, 0.0).astype(BF16), jnp.where(lo_lane, kk_r, 0.0).astype(BF16)]
    k_b = [jnp.where(lo_lane, 0.0, kk_r).astype(BF16), jnp.where(lo_lane, 0.0, kk).astype(BF16)]
    v_a = [jnp.where(lo_lane, vv, 0.0).astype(BF16), jnp.where(lo_lane, vv_r, 0.0).astype(BF16)]
    v_b = [jnp.where(lo_lane, 0.0, vv_r).astype(BF16), jnp.where(lo_lane, 0.0, vv).astype(BF16)]
    key_ok = jnp.logical_or(lax.broadcasted_iota(jnp.int32, (W, 2 * W), 1) >= W, n > 0)
    out_lo = lax.broadcasted_iota(jnp.int32, (W, 128), 1) < HEAD_DIM

    for j in range(S_Q_HEADS // 2):
        hk = (2 * j) // (S_Q_HEADS // S_KV_HEADS)
        qs = q_ref[:, 128 * j:128 * (j + 1)].astype(BF16)
        probs, inv = [], []
        for p, kmat in enumerate((k_a[hk], k_b[hk])):
            head = 2 * j + p
            s = _dot_nt(qs, kmat) * QK_SCALE + bias_ref[head]
            s = jnp.where(key_ok, s, NEG_INF)
            sink = sink_ref[head:head + 1, 0:1]
            m = jnp.maximum(jnp.max(s, axis=1, keepdims=True), sink)
            e = jnp.exp(s - m)
            inv.append(1.0 / (jnp.sum(e, axis=1, keepdims=True) + jnp.exp(sink - m)))
            probs.append(e.astype(BF16))
        pv = _dot(jnp.concatenate(probs, axis=1), jnp.concatenate([v_a[hk], v_b[hk]], axis=0))
        o_ref[:, 128 * j:128 * (j + 1)] = pv * jnp.where(out_lo, inv[0], inv[1])


def _swa_prompt(z3, bias, sinks):
    b, s, _ = z3.shape
    W = WINDOW
    return pl.pallas_call(
        _swa_prompt_kernel,
        grid=(b, s // W),
        in_specs=[
            pl.BlockSpec((None, W, S_WIDTH), lambda i, n: (i, n, Z_SQ // S_WIDTH)),
            pl.BlockSpec((None, W, 2 * S_KV_WIDTH), lambda i, n: (i, n, Z_SKV // (2 * S_KV_WIDTH))),
            pl.BlockSpec((None, W, 2 * S_KV_WIDTH), lambda i, n: (i, jnp.maximum(n - 1, 0), Z_SKV // (2 * S_KV_WIDTH))),
            _const_spec((S_Q_HEADS, W, 2 * W)),
            _const_spec((S_Q_HEADS, 128)),
        ],
        out_specs=pl.BlockSpec((None, W, S_WIDTH), lambda i, n: (i, n, 0)),
        out_shape=jax.ShapeDtypeStruct((b, s, S_WIDTH), F32),
        compiler_params=_params("arbitrary", "arbitrary"),
        name="swa_prompt",
    )(z3, z3, z3, bias, sinks)


def _cross_attend_rows(x1, g2_ref, wq_ref, mk, mv, wo_ref):
    mem = mk.shape[0]
    qx = _dot(_rms(x1, g2_ref[...]).astype(BF16), wq_ref[...]).astype(BF16)
    lane_head = lax.broadcasted_iota(jnp.int32, (mem, X_WIDTH), 1) // HEAD_DIM
    probs, vms = [], []
    for h in range(X_HEADS):
        km = jnp.where(lane_head == h, mk, 0.0).astype(BF16)
        s = _dot_nt(qx, km) * QK_SCALE
        e = jnp.exp(s - jnp.max(s, axis=1, keepdims=True))
        probs.append((e / jnp.sum(e, axis=1, keepdims=True)).astype(BF16))
        vms.append(jnp.where(lane_head == h, mv, 0.0).astype(BF16))
    o = _dot(jnp.concatenate(probs, axis=1), jnp.concatenate(vms, axis=0))
    return x1 + _dot(o.astype(BF16), wo_ref[...])


def _mix_out_prompt_kernel(x_ref, hm_ref, hc_ref, hs_ref, wout_ref, g2_ref, wq_ref, mk_ref, mv_ref, wo_ref, o_ref):
    cat = jnp.concatenate([hm_ref[...].astype(BF16), hc_ref[...].astype(BF16), hs_ref[...].astype(BF16)], axis=1)
    x1 = x_ref[...] + _dot(cat, wout_ref[...])
    o_ref[...] = _cross_attend_rows(x1, g2_ref, wq_ref, mk_ref[...], mv_ref[...], wo_ref)


def _mix_out_prompt(x3, hm, hc, hs, wout, g2, wq, mkv, wo, *, tm):
    b, s, d = x3.shape
    mem = mkv.shape[1]
    row = lambda w: pl.BlockSpec((None, tm, w), lambda i, t: (i, t, 0))
    return pl.pallas_call(
        _mix_out_prompt_kernel,
        grid=(b, s // tm),
        in_specs=[
            row(d), row(M_WIDTH), row(C_WIDTH), row(S_WIDTH),
            _const_spec(wout.shape), _const_spec((1, d)), _const_spec(wq.shape),
            pl.BlockSpec((None, mem, X_WIDTH), lambda i, t: (i, 0, 0)),
            pl.BlockSpec((None, mem, X_WIDTH), lambda i, t: (i, 0, 1)),
            _const_spec(wo.shape),
        ],
        out_specs=row(d),
        out_shape=jax.ShapeDtypeStruct((b, s, d), F32),
        compiler_params=_params("arbitrary", "arbitrary"),
        name="mix_out_prompt",
    )(x3, hm, hc, hs, wout, g2, wq, mkv, mkv, wo)


FFN_PAD = 8


def _ffn_prompt_kernel(x_ref, g3_ref, wup_ref, cw_ref, cb_ref, wdn_ref, gf_ref, o_ref, tail_ref,
                       gbuf, carry, act, *, final):
    tm = x_ref.shape[0]
    dff = wdn_ref.shape[0]

    @pl.when(pl.program_id(1) == 0)
    def _():
        carry[...] = jnp.zeros(carry.shape, F32)

    x = x_ref[...]
    h = _rms(x, g3_ref[...]).astype(BF16)
    for c in range(0, dff, FFN_CHUNK):
        a = _dot(h, wup_ref[:, c:c + FFN_CHUNK])
        g = _dot(h, wup_ref[:, dff + c:dff + c + FFN_CHUNK])
        gbuf[0:FFN_PAD, :] = carry[:, c:c + FFN_CHUNK]
        gbuf[FFN_PAD:FFN_PAD + tm, :] = g
        carry[:, c:c + FFN_CHUNK] = g[tm - FFN_PAD:tm, :]
        gc = (cw_ref[0:1, c:c + FFN_CHUNK] * gbuf[FFN_PAD - 2:FFN_PAD - 2 + tm, :]
              + cw_ref[1:2, c:c + FFN_CHUNK] * gbuf[FFN_PAD - 1:FFN_PAD - 1 + tm, :]
              + cw_ref[2:3, c:c + FFN_CHUNK] * g + cb_ref[:, c:c + FFN_CHUNK])
        act[:, c:c + FFN_CHUNK] = (gc * _sigmoid(gc) * a).astype(BF16)
    y = x + _dot(act[...], wdn_ref[...])
    o_ref[...] = _rms(y, gf_ref[...]) if final else y
    tail_ref[...] = carry[...]


def _ffn_prompt(x3, g3, wup, cw, cb, wdn, gf, *, tm, final):
    b, s, d = x3.shape
    dff = wdn.shape[0]
    return pl.pallas_call(
        functools.partial(_ffn_prompt_kernel, final=final),
        grid=(b, s // tm),
        in_specs=[
            pl.BlockSpec((None, tm, d), lambda i, t: (i, t, 0)),
            _const_spec((1, d)), _const_spec(wup.shape), _const_spec(cw.shape), _const_spec((1, dff)),
            _const_spec(wdn.shape), _const_spec((1, d)),
        ],
        out_specs=[
            pl.BlockSpec((None, tm, d), lambda i, t: (i, t, 0)),
            pl.BlockSpec((None, FFN_PAD, dff), lambda i, t: (i, 0, 0)),
        ],
        out_shape=[jax.ShapeDtypeStruct((b, s, d), F32), jax.ShapeDtypeStruct((b, FFN_PAD, dff), F32)],
        scratch_shapes=[
            pltpu.VMEM((FFN_PAD + tm, FFN_CHUNK), F32),
            pltpu.VMEM((FFN_PAD, dff), F32),
            pltpu.VMEM((tm, dff), BF16),
        ],
        compiler_params=_params("arbitrary", "arbitrary"),
        name="ffn_prompt",
    )(x3, g3, wup, cw, cb, wdn, gf)


MLSTM_SAMPLE_BLOCK = 8
SWA_SAMPLE_BLOCK = 16
XATTN_SAMPLE_BLOCK = 8
CONV_SAMPLE_BLOCK = 32


def _expand_heads(x, width):
    lane_head = lax.broadcasted_iota(jnp.int32, (x.shape[0], width), 1) // HEAD_DIM
    out = jnp.zeros((x.shape[0], width), F32)
    for h in range(width // HEAD_DIM):
        out = jnp.where(lane_head == h, x[:, h:h + 1], out)
    return out


def _head_sums(x):
    lane_head = lax.broadcasted_iota(jnp.int32, x.shape, 1) // HEAD_DIM
    out = jnp.zeros(x.shape, F32)
    for h in range(x.shape[1] // HEAD_DIM):
        sel = lane_head == h
        out = jnp.where(sel, jnp.sum(jnp.where(sel, x, 0.0), axis=1, keepdims=True), out)
    return out


def _mlstm_sample_kernel(z_ref, g_ref, gb_ref, ng_ref, c_ref, n_ref, m_ref,
                         h_ref, co_ref, no_ref, mo_ref, qc_scr):
    R = z_ref.shape[0]
    q = z_ref[:, 0:M_WIDTH]
    k = z_ref[:, M_WIDTH:2 * M_WIDTH] * QK_SCALE
    v = z_ref[:, 2 * M_WIDTH:3 * M_WIDTH]
    o = z_ref[:, 3 * M_WIDTH:4 * M_WIDTH]
    gates = g_ref[...] + gb_ref[...]
    i_pre = gates[:, 0:M_HEADS]
    f_log = _log_sigmoid(gates[:, M_HEADS:2 * M_HEADS])
    a = f_log + m_ref[...]
    m_t = jnp.maximum(a, i_pre)
    w_inter = _expand_heads(jnp.exp(a - m_t), M_WIDTH)
    w_new = _expand_heads(jnp.exp(i_pre - m_t), M_WIDTH)
    floor = _expand_heads(jnp.exp(-m_t), M_WIDTH)
    n_old = n_ref[...]

    eye = (lax.broadcasted_iota(jnp.int32, (HEAD_DIM, M_WIDTH), 1) % HEAD_DIM
           == lax.broadcasted_iota(jnp.int32, (HEAD_DIM, M_WIDTH), 0))
    lane_head = lax.broadcasted_iota(jnp.int32, (HEAD_DIM, M_WIDTH), 1) // HEAD_DIM
    for r in range(R):
        for h in range(M_HEADS):
            lo = h * HEAD_DIM
            sel = jnp.logical_and(eye, lane_head == h)
            q_col = jnp.sum(jnp.where(sel, q[r:r + 1, :], 0.0), axis=1, keepdims=True)
            k_col = jnp.sum(jnp.where(sel, k[r:r + 1, :], 0.0), axis=1, keepdims=True)
            c_old = c_ref[r, lo:lo + HEAD_DIM, :]
            qc_scr[r:r + 1, lo:lo + HEAD_DIM] = jnp.sum(q_col * c_old, axis=0, keepdims=True)
            co_ref[r, lo:lo + HEAD_DIM, :] = (w_inter[r:r + 1, lo:lo + HEAD_DIM] * c_old
                                              + k_col * (w_new[r:r + 1, lo:lo + HEAD_DIM] * v[r:r + 1, lo:lo + HEAD_DIM]))

    s = _head_sums(q * k) * w_new
    num = w_inter * qc_scr[...] + s * v
    den = w_inter * _head_sums(q * n_old) + s
    hh = num / jnp.maximum(jnp.abs(den), floor)
    hh = hh * lax.rsqrt(_head_sums(hh * hh) * (1.0 / HEAD_DIM) + EPS) * ng_ref[...]
    h_ref[...] = hh * _sigmoid(o)
    no_ref[...] = w_inter * n_old + w_new * k
    mo_ref[...] = m_t


def _mlstm_sample(z, gbias, ng, c, n, m):
    bsz = z.shape[0]
    R = MLSTM_SAMPLE_BLOCK
    return pl.pallas_call(
        _mlstm_sample_kernel,
        grid=(bsz // R,),
        in_specs=[
            pl.BlockSpec((R, 4 * M_WIDTH), lambda i: (i, Z_M // (4 * M_WIDTH))),
            pl.BlockSpec((R, 128), lambda i: (i, Z_G // 128)),
            _const_spec((1, 128)), _const_spec((1, M_WIDTH)),
            pl.BlockSpec((R, M_WIDTH, HEAD_DIM), lambda i: (i, 0, 0)),
            pl.BlockSpec((R, M_WIDTH), lambda i: (i, 0)),
            pl.BlockSpec((R, M_HEADS), lambda i: (i, 0)),
        ],
        out_specs=[
            pl.BlockSpec((R, M_WIDTH), lambda i: (i, 0)),
            pl.BlockSpec((R, M_WIDTH, HEAD_DIM), lambda i: (i, 0, 0)),
            pl.BlockSpec((R, M_WIDTH), lambda i: (i, 0)),
            pl.BlockSpec((R, M_HEADS), lambda i: (i, 0)),
        ],
        out_shape=[
            jax.ShapeDtypeStruct((bsz, M_WIDTH), F32),
            jax.ShapeDtypeStruct((bsz, M_WIDTH, HEAD_DIM), F32),
            jax.ShapeDtypeStruct((bsz, M_WIDTH), F32),
            jax.ShapeDtypeStruct((bsz, M_HEADS), F32),
        ],
        scratch_shapes=[pltpu.VMEM((R, M_WIDTH), F32)],
        compiler_params=_params("arbitrary"),
        name="mlstm_sample",
    )(z, z, gbias, ng, c, n, m)


def _conv_sample_kernel(z_ref, hist_ref, w_ref, cb_ref, lg_ref, lb_ref, h_ref, hist_out_ref):
    nh = C_KERNEL - 1
    u = z_ref[:, 0:C_WIDTH] * _sigmoid(z_ref[:, C_WIDTH:2 * C_WIDTH])
    acc = cb_ref[...] + w_ref[nh:nh + 1, :] * u
    for j in range(nh):
        acc = acc + w_ref[j:j + 1, :] * hist_ref[:, j * C_WIDTH:(j + 1) * C_WIDTH]
    mu = jnp.mean(acc, axis=-1, keepdims=True)
    xc = acc - mu
    y = xc * lax.rsqrt(jnp.mean(xc * xc, axis=-1, keepdims=True) + EPS) * lg_ref[...] + lb_ref[...]
    h_ref[...] = y * _sigmoid(y)
    hist_out_ref[:, 0:(nh - 1) * C_WIDTH] = hist_ref[:, C_WIDTH:nh * C_WIDTH]
    hist_out_ref[:, (nh - 1) * C_WIDTH:nh * C_WIDTH] = u


def _conv_sample(z, hist, w, cb, lg, lb):
    bsz = z.shape[0]
    R = min(CONV_SAMPLE_BLOCK, bsz)
    hw = hist.shape[1]
    return pl.pallas_call(
        _conv_sample_kernel,
        grid=(bsz // R,),
        in_specs=[
            pl.BlockSpec((R, 2 * C_WIDTH), lambda i: (i, Z_C // (2 * C_WIDTH))),
            pl.BlockSpec((R, hw), lambda i: (i, 0)),
            _const_spec((CONV_PAD, C_WIDTH)),
            _const_spec((1, C_WIDTH)), _const_spec((1, C_WIDTH)), _const_spec((1, C_WIDTH)),
        ],
        out_specs=[pl.BlockSpec((R, C_WIDTH), lambda i: (i, 0)), pl.BlockSpec((R, hw), lambda i: (i, 0))],
        out_shape=[jax.ShapeDtypeStruct((bsz, C_WIDTH), F32), jax.ShapeDtypeStruct((bsz, hw), F32)],
        compiler_params=_params("arbitrary"),
        name="conv_sample",
    )(z, hist, w, cb, lg, lb)


def _swa_sample_kernel(q_ref, kvn_ref, kc_ref, vc_ref, bias_ref, aux_ref, o_ref, ko_ref, vo_ref):
    R = q_ref.shape[0]
    W = kc_ref.shape[1]
    H = S_Q_HEADS
    shape = (R, H, 128)
    row = lax.broadcasted_iota(jnp.int32, shape, 1)
    lane_half = lax.broadcasted_iota(jnp.int32, shape, 2) // HEAD_DIM
    q_half = row % 2
    kv_head = row // (H // S_KV_HEADS)
    qs = jnp.zeros(shape, F32)
    for j in range(H // 2):
        qs = jnp.where(row // 2 == j, q_ref[:, :, 128 * j:128 * (j + 1)], qs)
    q8 = jnp.where(lane_half == kv_head, jnp.where(q_half == kv_head, qs, pltpu.roll(qs, HEAD_DIM, axis=2)), 0.0)
    k_new = kvn_ref[:, :, 0:S_KV_WIDTH]
    v_new = kvn_ref[:, :, S_KV_WIDTH:2 * S_KV_WIDTH]
    kc = kc_ref[...]
    vc = vc_ref[...]
    s = jnp.einsum("bqd,bkd->bqk", q8.astype(BF16), kc.astype(BF16), preferred_element_type=F32) * QK_SCALE
    s = s + bias_ref[...][None]
    s_new = jnp.sum(q8 * k_new, axis=2, keepdims=True) * QK_SCALE + aux_ref[:, 0:1][None]
    sink = aux_ref[:, 1:2][None]
    m = jnp.maximum(jnp.maximum(jnp.max(s, axis=2, keepdims=True), s_new), sink)
    e = jnp.exp(s - m)
    e_new = jnp.exp(s_new - m)
    inv = 1.0 / (jnp.sum(e, axis=2, keepdims=True) + e_new + jnp.exp(sink - m))
    o8 = jnp.einsum("bqk,bkd->bqd", e.astype(BF16), vc.astype(BF16), preferred_element_type=F32)
    o8 = (o8 + e_new * v_new) * inv
    o8 = jnp.where(lane_half == q_half, jnp.where(q_half == kv_head, o8, pltpu.roll(o8, HEAD_DIM, axis=2)), 0.0)
    for j in range(H // 2):
        o_ref[:, :, 128 * j:128 * (j + 1)] = jnp.sum(jnp.where(row // 2 == j, o8, 0.0), axis=1, keepdims=True)
    ko_ref[:, 0:W - 1, :] = kc_ref[:, 1:W, :]
    ko_ref[:, W - 1:W, :] = k_new
    vo_ref[:, 0:W - 1, :] = vc_ref[:, 1:W, :]
    vo_ref[:, W - 1:W, :] = v_new


def _swa_sample(z3, kc, vc, bias, aux):
    bsz = z3.shape[0]
    R = min(SWA_SAMPLE_BLOCK, bsz)
    W = kc.shape[1]
    cache = pl.BlockSpec((R, W, S_KV_WIDTH), lambda i: (i, 0, 0))
    return pl.pallas_call(
        _swa_sample_kernel,
        grid=(bsz // R,),
        in_specs=[
            pl.BlockSpec((R, 1, S_WIDTH), lambda i: (i, 0, Z_SQ // S_WIDTH)),
            pl.BlockSpec((R, 1, 2 * S_KV_WIDTH), lambda i: (i, 0, Z_SKV // (2 * S_KV_WIDTH))),
            cache, cache,
            _const_spec((S_Q_HEADS, 128)), _const_spec((S_Q_HEADS, 128)),
        ],
        out_specs=[pl.BlockSpec((R, 1, S_WIDTH), lambda i: (i, 0, 0)), cache, cache],
        out_shape=[
            jax.ShapeDtypeStruct((bsz, 1, S_WIDTH), F32),
            jax.ShapeDtypeStruct(kc.shape, F32),
            jax.ShapeDtypeStruct(vc.shape, F32),
        ],
        compiler_params=_params("arbitrary"),
        name="swa_sample",
    )(z3, z3, kc, vc, bias, aux)


def _mix_out_sample_kernel(x_ref, hm_ref, hc_ref, hs_ref, wout_ref, g2_ref, wq_ref, x1_ref, q_ref):
    cat = jnp.concatenate([hm_ref[...].astype(BF16), hc_ref[...].astype(BF16), hs_ref[...].astype(BF16)], axis=1)
    x1 = x_ref[...] + _dot(cat, wout_ref[...])
    x1_ref[...] = x1
    q_ref[...] = _dot(_rms(x1, g2_ref[...]).astype(BF16), wq_ref[...])


def _mix_out_sample(x, hm, hc, hs, wout, g2, wq):
    bsz, d = x.shape
    full = lambda a: pl.BlockSpec(a.shape, lambda i: (0,) * a.ndim)
    args = (x, hm, hc, hs, wout, g2, wq)
    return pl.pallas_call(
        _mix_out_sample_kernel,
        grid=(1,),
        in_specs=[full(a) for a in args],
        out_specs=[pl.BlockSpec((bsz, d), lambda i: (0, 0)), pl.BlockSpec((bsz, X_WIDTH), lambda i: (0, 0))],
        out_shape=[jax.ShapeDtypeStruct((bsz, d), F32), jax.ShapeDtypeStruct((bsz, X_WIDTH), F32)],
        compiler_params=_params("arbitrary"),
        name="mix_out_sample",
    )(*args)


def _xattn_sample_kernel(q_ref, k_ref, v_ref, o_ref):
    R, mem, _ = k_ref.shape
    shape = (R, SUBLANES, X_WIDTH)
    row = lax.broadcasted_iota(jnp.int32, shape, 1)
    lane_head = lax.broadcasted_iota(jnp.int32, shape, 2) // HEAD_DIM
    own = row == lane_head
    q8 = jnp.where(own, jnp.broadcast_to(q_ref[...], shape), 0.0).astype(BF16)
    s = jnp.einsum("bqd,bkd->bqk", q8, k_ref[...].astype(BF16), preferred_element_type=F32) * QK_SCALE
    e = jnp.exp(s - jnp.max(s, axis=2, keepdims=True))
    p = (e / jnp.sum(e, axis=2, keepdims=True)).astype(BF16)
    o8 = jnp.einsum("bqk,bkd->bqd", p, v_ref[...].astype(BF16), preferred_element_type=F32)
    o_ref[...] = jnp.sum(jnp.where(own, o8, 0.0), axis=1, keepdims=True)


def _xattn_sample(q3, k, v):
    bsz, mem, w = k.shape
    R = min(XATTN_SAMPLE_BLOCK, bsz)
    kv = pl.BlockSpec((R, mem, w), lambda i: (i, 0, 0))
    qo = pl.BlockSpec((R, 1, w), lambda i: (i, 0, 0))
    return pl.pallas_call(
        _xattn_sample_kernel,
        grid=(bsz // R,),
        in_specs=[qo, kv, kv],
        out_specs=qo,
        out_shape=jax.ShapeDtypeStruct((bsz, 1, w), F32),
        compiler_params=_params("arbitrary"),
        name="xattn_sample",
    )(q3, k, v)


def _ffn_sample_kernel(x1_ref, ox_ref, wo_ref, g3_ref, wup_ref, cw_ref, cb_ref, wdn_ref, gf_ref, hist_ref,
                       o_ref, hist_out_ref, act, *, final):
    dff = wdn_ref.shape[0]
    x = x1_ref[...] + _dot(ox_ref[...].astype(BF16), wo_ref[...])
    h = _rms(x, g3_ref[...]).astype(BF16)
    for c in range(0, dff, FFN_CHUNK):
        a = _dot(h, wup_ref[:, c:c + FFN_CHUNK])
        g = _dot(h, wup_ref[:, dff + c:dff + c + FFN_CHUNK])
        h1 = hist_ref[:, dff + c:dff + c + FFN_CHUNK]
        gc = (cw_ref[0:1, c:c + FFN_CHUNK] * hist_ref[:, c:c + FFN_CHUNK]
              + cw_ref[1:2, c:c + FFN_CHUNK] * h1
              + cw_ref[2:3, c:c + FFN_CHUNK] * g + cb_ref[:, c:c + FFN_CHUNK])
        act[:, c:c + FFN_CHUNK] = (gc * _sigmoid(gc) * a).astype(BF16)
        hist_out_ref[:, c:c + FFN_CHUNK] = h1
        hist_out_ref[:, dff + c:dff + c + FFN_CHUNK] = g
    y = x + _dot(act[...], wdn_ref[...])
    o_ref[...] = _rms(y, gf_ref[...]) if final else y


def _ffn_sample(x1, ox, wo, g3, wup, cw, cb, wdn, gf, hist, *, final):
    bsz, d = x1.shape
    dff = wdn.shape[0]
    full = lambda a: pl.BlockSpec(a.shape, lambda i: (0,) * a.ndim)
    args = (x1, ox, wo, g3, wup, cw, cb, wdn, gf, hist)
    return pl.pallas_call(
        functools.partial(_ffn_sample_kernel, final=final),
        grid=(1,),
        in_specs=[full(a) for a in args],
        out_specs=[pl.BlockSpec((bsz, d), lambda i: (0, 0)), pl.BlockSpec(hist.shape, lambda i: (0, 0))],
        out_shape=[jax.ShapeDtypeStruct((bsz, d), F32), jax.ShapeDtypeStruct(hist.shape, F32)],
        scratch_shapes=[pltpu.VMEM((bsz, dff), BF16)],
        compiler_params=_params("arbitrary"),
        name="ffn_sample",
    )(*args)


def _t5_buckets(dist):
    n = np.maximum(dist, 0)
    max_exact = N_BUCKETS // 2
    nf = np.maximum(n, max_exact).astype(np.float32)
    large = max_exact + (np.log(nf / np.float32(max_exact)) / np.float32(math.log(MAX_DISTANCE / max_exact))
                         * np.float32(N_BUCKETS - max_exact)).astype(np.int32)
    return np.where(n < max_exact, n, np.minimum(large, N_BUCKETS - 1))


def _swa_tables(rel_bias, sinks):
    W = WINDOW
    dist = np.arange(W)[:, None] + W - np.arange(2 * W)[None, :]
    band = (dist >= 0) & (dist < W)
    grid = jnp.transpose(rel_bias[_t5_buckets(dist)], (2, 0, 1))
    prompt_bias = jnp.where(band[None], grid, NEG_INF)
    dist_c = W - np.arange(W)
    tab = jnp.transpose(rel_bias[_t5_buckets(dist_c)], (1, 0))
    cache_bias = jnp.where((dist_c < W)[None], tab, NEG_INF)
    return prompt_bias, cache_bias, rel_bias[0]


def _pad_rows(a, rows):
    return jnp.concatenate([a, jnp.zeros((rows - a.shape[0],) + a.shape[1:], a.dtype)], axis=0)


def _row(a):
    return a.reshape(1, -1)


def kernel(x_prompt, x_sample, mem_prompt, state_mlstm_C, state_mlstm_n, state_mlstm_m, state_conv, cache_swa_k, cache_swa_v, cache_mem_k, cache_mem_v, state_ffn_conv, rel_bias, norm1_g, w_in, b_i, b_f, mlstm_norm_g, conv_w, conv_b, conv_ln_g, conv_ln_b, swa_sinks, w_out, norm2_g, w_xq, w_xk, w_xv, w_xo, norm3_g, w_up, ffn_conv_w, ffn_conv_b, w_down, final_norm_g):
    depth = w_in.shape[0]
    bp, seq, d = x_prompt.shape
    bs = x_sample.shape[0]
    mem = mem_prompt.shape[1]
    dff = w_down.shape[1]
    W = WINDOW
    nh = C_KERNEL - 1

    xp = x_prompt
    xs = x_sample.reshape(bs, d)
    mem2 = mem_prompt.reshape(bp * mem, d)
    ones_d = jnp.ones((1, d), F32)
    gf = _row(final_norm_g)
    pm_c, pm_n, pm_m, p_conv, p_k, p_v, p_mk, p_mv, p_ffn = ([] for _ in range(9))
    s_c, s_n, s_m, s_conv, s_k, s_v, s_ffn = ([] for _ in range(7))

    for l in range(depth):
        last = l == depth - 1
        wi = w_in[l]
        gate_cols = wi[:, 4 * M_WIDTH:4 * M_WIDTH + 2 * M_HEADS]
        win = jnp.concatenate(
            [wi[:, 0:4 * M_WIDTH], wi[:, 4 * M_WIDTH + 2 * M_HEADS:], gate_cols,
             jnp.zeros((d, 128 - 2 * M_HEADS), F32)], axis=1).astype(BF16)
        wout = w_out[l].astype(BF16)
        wxq = w_xq[l].astype(BF16)
        wxkv = jnp.concatenate([w_xk[l], w_xv[l]], axis=1).astype(BF16)
        wxo = w_xo[l].astype(BF16)
        wup = w_up[l].astype(BF16)
        wdn = w_down[l].astype(BF16)
        g1, g2, g3 = _row(norm1_g[l]), _row(norm2_g[l]), _row(norm3_g[l])
        gbias = jnp.concatenate([b_i[l], b_f[l], jnp.zeros((128 - 2 * M_HEADS,), F32)]).reshape(1, 128)
        ng = _row(mlstm_norm_g[l])
        cw = _pad_rows(conv_w[l], CONV_PAD)
        cb, lg, lb = _row(conv_b[l]), _row(conv_ln_g[l]), _row(conv_ln_b[l])
        fcw = _pad_rows(ffn_conv_w[l], SUBLANES)
        fcb = _row(ffn_conv_b[l])
        prompt_bias, cache_bias, bias0 = _swa_tables(rel_bias, swa_sinks[l])
        sink_rows = jnp.broadcast_to(swa_sinks[l][:, None], (S_Q_HEADS, 128))
        aux = jnp.concatenate([bias0[:, None], swa_sinks[l][:, None], jnp.zeros((S_Q_HEADS, 126), F32)], axis=1)

        mkv = _norm_matmul(mem2, ones_d, wxkv, norm=False, tm=512).reshape(bp, mem, 2 * X_WIDTH)
        z = _norm_matmul(xp.reshape(bp * seq, d), g1, win, norm=True, tm=512).reshape(bp, seq, Z_WIDTH)
        hm, cn, mm = _mlstm_prompt(z, gbias, ng)
        hc, ctail = _conv_prompt(z, cw, cb, lg, lb, tc=512)
        hs = _swa_prompt(z, prompt_bias, sink_rows)
        xp = _mix_out_prompt(xp, hm, hc, hs, wout, g2, wxq, mkv, wxo, tm=512)
        xp, ftail = _ffn_prompt(xp, g3, wup, fcw, fcb, wdn, gf, tm=512, final=last)
        pm_c.append(cn[..., 0:HEAD_DIM])
        pm_n.append(cn[..., HEAD_DIM])
        pm_m.append(mm[:, :, 0, 0])
        p_conv.append(ctail[:, CONV_PAD - nh:, :])
        p_k.append(z[:, seq - W:, Z_SKV:Z_SKV + S_KV_WIDTH].reshape(bp, W, S_KV_HEADS, HEAD_DIM))
        p_v.append(z[:, seq - W:, Z_SKV + S_KV_WIDTH:Z_SKV + 2 * S_KV_WIDTH].reshape(bp, W, S_KV_HEADS, HEAD_DIM))
        p_mk.append(mkv[..., 0:X_WIDTH].reshape(bp, mem, X_HEADS, HEAD_DIM))
        p_mv.append(mkv[..., X_WIDTH:].reshape(bp, mem, X_HEADS, HEAD_DIM))
        p_ffn.append(ftail[:, FFN_PAD - (FFN_KERNEL - 1):, :])

        zs = _norm_matmul(xs, g1, win, norm=True, tm=bs)
        hm_s, c_new, n_new, m_new = _mlstm_sample(
            zs, gbias, ng, state_mlstm_C[l].reshape(bs, M_WIDTH, HEAD_DIM),
            state_mlstm_n[l].reshape(bs, M_WIDTH), state_mlstm_m[l])
        hc_s, conv_new = _conv_sample(zs, state_conv[l].reshape(bs, nh * C_WIDTH), cw, cb, lg, lb)
        hs_s, k_new, v_new = _swa_sample(
            zs.reshape(bs, 1, Z_WIDTH), cache_swa_k[l].reshape(bs, W, S_KV_WIDTH),
            cache_swa_v[l].reshape(bs, W, S_KV_WIDTH), cache_bias, aux)
        x1, qx = _mix_out_sample(xs, hm_s, hc_s, hs_s.reshape(bs, S_WIDTH), wout, g2, wxq)
        ox = _xattn_sample(qx.reshape(bs, 1, X_WIDTH), cache_mem_k[l].reshape(bs, mem, X_WIDTH),
                           cache_mem_v[l].reshape(bs, mem, X_WIDTH))
        xs, ffn_new = _ffn_sample(x1, ox.reshape(bs, X_WIDTH), wxo, g3, wup, fcw, fcb, wdn, gf,
                                  state_ffn_conv[l].reshape(bs, (FFN_KERNEL - 1) * dff), final=last)
        s_c.append(c_new.reshape(bs, M_HEADS, HEAD_DIM, HEAD_DIM))
        s_n.append(n_new.reshape(bs, M_HEADS, HEAD_DIM))
        s_m.append(m_new)
        s_conv.append(conv_new.reshape(bs, nh, C_WIDTH))
        s_k.append(k_new.reshape(bs, W, S_KV_HEADS, HEAD_DIM))
        s_v.append(v_new.reshape(bs, W, S_KV_HEADS, HEAD_DIM))
        s_ffn.append(ffn_new.reshape(bs, FFN_KERNEL - 1, dff))

    st = jnp.stack
    return (xp, xs.reshape(bs, 1, d),
            st(pm_c), st(pm_n), st(pm_m), st(p_conv), st(p_k), st(p_v), st(p_mk), st(p_mv), st(p_ffn),
            st(s_c), st(s_n), st(s_m), st(s_conv), st(s_k), st(s_v), st(s_ffn))
```

```python
import functools
import math

import numpy as np
import jax
import jax.numpy as jnp
from jax import lax
from jax.experimental import pallas as pl
from jax.experimental.pallas import tpu as pltpu

F32 = jnp.float32
BF16 = jnp.bfloat16
EPS = 1e-6
NEG_INF = float("-inf")

HEAD_DIM = 64
M_HEADS = 4
M_WIDTH = M_HEADS * HEAD_DIM
C_WIDTH = 256
C_KERNEL = 31
S_Q_HEADS = 8
S_KV_HEADS = 2
S_WIDTH = S_Q_HEADS * HEAD_DIM
S_KV_WIDTH = S_KV_HEADS * HEAD_DIM
WINDOW = 128
N_BUCKETS = 32
MAX_DISTANCE = 128
X_HEADS = 4
X_WIDTH = X_HEADS * HEAD_DIM
FFN_KERNEL = 3
QK_SCALE = HEAD_DIM ** -0.5
LOG2E = math.log2(math.e)

Z_M = 0
Z_C = 4 * M_WIDTH
Z_SQ = Z_C + 2 * C_WIDTH
Z_SKV = Z_SQ + S_WIDTH
Z_G = Z_SKV + 2 * S_KV_WIDTH
Z_WIDTH = Z_G + 128

LANES = 128
SUBLANES = 8
VMEM_LIMIT = 56 * 1024 * 1024

M_CHUNK = 128
FFN_CHUNK = 256


def _params(*sem):
    return pltpu.CompilerParams(dimension_semantics=sem, vmem_limit_bytes=VMEM_LIMIT)


def _const_spec(shape):
    nd = len(shape)
    return pl.BlockSpec(shape, lambda *_: (0,) * nd, pipeline_mode=pl.Buffered(1))


def _rms(x, g):
    return x * lax.rsqrt(jnp.mean(x * x, axis=-1, keepdims=True) + EPS) * g


def _sigmoid(x):
    return 1.0 / (1.0 + jnp.exp(-x))


def _log_sigmoid(x):
    return jnp.minimum(x, 0.0) - jnp.log1p(jnp.exp(-jnp.abs(x)))


def _dot(a, b):
    return jnp.dot(a, b, preferred_element_type=F32)


def _dot_nt(a, b):
    return lax.dot_general(a, b, (((1,), (1,)), ((), ())), preferred_element_type=F32)


def _dot_tn(a, b):
    return lax.dot_general(a, b, (((0,), (0,)), ((), ())), preferred_element_type=F32)


def _norm_matmul_kernel(x_ref, g_ref, w_ref, o_ref, *, norm, col_chunk):
    x = x_ref[...]
    h = (_rms(x, g_ref[...]) if norm else x).astype(BF16)
    n = o_ref.shape[1]
    for c in range(0, n, col_chunk):
        w = min(col_chunk, n - c)
        o_ref[:, c:c + w] = _dot(h, w_ref[:, c:c + w])


def _norm_matmul(x, g, w, *, norm, tm):
    m, d = x.shape
    n = w.shape[1]
    tm = min(tm, m)
    return pl.pallas_call(
        functools.partial(_norm_matmul_kernel, norm=norm, col_chunk=512),
        grid=(m // tm,),
        in_specs=[pl.BlockSpec((tm, d), lambda i: (i, 0)), _const_spec((1, d)), _const_spec((d, n))],
        out_specs=pl.BlockSpec((tm, n), lambda i: (i, 0)),
        out_shape=jax.ShapeDtypeStruct((m, n), F32),
        compiler_params=_params("arbitrary"),
        name="norm_matmul" if norm else "matmul",
    )(x, g, w)


def _mlstm_prompt_kernel(z_ref, g_ref, gb_ref, ng_ref, h_ref, cn_ref, m_ref, cn_scr, m_scr, ht_scr):
    NB, L = z_ref.shape[0], z_ref.shape[1]
    hi = lax.Precision.HIGHEST

    @pl.when(pl.program_id(0) == 0)
    def _():
        cn_scr[...] = jnp.zeros(cn_scr.shape, F32)
        m_scr[...] = jnp.zeros(m_scr.shape, F32)

    src = lax.broadcasted_iota(jnp.int32, (L, L), 0)
    qry = lax.broadcasted_iota(jnp.int32, (L, L), 1)
    causal_t = src <= qry
    upper = jnp.where(causal_t, 1.0, 0.0)
    lower = jnp.where(src >= qry, 1.0, 0.0)
    lane_half = lax.broadcasted_iota(jnp.int32, (L, 128), 1) // HEAD_DIM
    row128 = lax.broadcasted_iota(jnp.int32, (128, 128), 0)

    for b in range(NB):
        gates = g_ref[b] + gb_ref[...]
        gates_t = gates.T[0:SUBLANES, :]
        b_rows = jnp.dot(_log_sigmoid(gates_t), upper, precision=hi, preferred_element_type=F32)
        b_cols = jnp.dot(lower, _log_sigmoid(gates), precision=hi, preferred_element_type=F32)
        for h in range(M_HEADS):
            par = h % 2
            slab = 128 * (h // 2)
            own = lane_half == par
            b_row = b_rows[M_HEADS + h:M_HEADS + h + 1, :]
            ci_col = gates[:, h:h + 1] - b_cols[:, M_HEADS + h:M_HEADS + h + 1]
            m_prev = m_scr[b, h, 0:1, 0:1]
            a_row = b_row + m_prev
            dm = jnp.where(causal_t, b_row + ci_col, NEG_INF)
            m_row = jnp.maximum(a_row, jnp.max(dm, axis=0, keepdims=True))
            w_inter = jnp.exp(a_row - m_row)
            q2 = z_ref[b, :, slab:slab + 128].astype(BF16)
            k2 = jnp.where(own, z_ref[b, :, M_WIDTH + slab:M_WIDTH + slab + 128] * QK_SCALE, 0.0)
            v2 = z_ref[b, :, 2 * M_WIDTH + slab:2 * M_WIDTH + slab + 128].astype(BF16)
            s_t = _dot_nt(k2.astype(BF16), q2) * jnp.exp(dm - m_row)
            cn = cn_scr[b, h]
            nd = w_inter * _dot_nt(cn.astype(BF16), q2)
            pv = _dot_tn(v2, s_t.astype(BF16))
            num = nd[0:HEAD_DIM, :] + pv[HEAD_DIM * par:HEAD_DIM * (par + 1), :]
            den = nd[HEAD_DIM:HEAD_DIM + 1, :] + jnp.sum(s_t, axis=0, keepdims=True)
            hh = num * (1.0 / jnp.maximum(jnp.abs(den), jnp.exp(-m_row)))
            hh = hh * lax.rsqrt(jnp.mean(hh * hh, axis=0, keepdims=True) + EPS)
            ht_scr[b, HEAD_DIM * h:HEAD_DIM * (h + 1), :] = hh
            b_last = b_row[:, L - 1:L]
            g_col = b_last + ci_col
            m_new = jnp.maximum(b_last + m_prev, jnp.max(g_col, axis=0, keepdims=True))
            decay = jnp.exp(b_last + m_prev - m_new)
            kw = k2 * jnp.exp(g_col - m_new)
            upd = _dot_tn(v2, kw.astype(BF16))
            c_new = decay * cn[0:HEAD_DIM, :] + upd[HEAD_DIM * par:HEAD_DIM * (par + 1), :]
            n_new = decay * cn[HEAD_DIM:HEAD_DIM + 1, :] + jnp.sum(kw, axis=0, keepdims=True)
            cn_scr[b, h, 0:HEAD_DIM, :] = c_new
            cn_scr[b, h, HEAD_DIM:128, :] = jnp.where(row128[HEAD_DIM:128, :] == HEAD_DIM, n_new, 0.0)
            m_scr[b, h] = jnp.broadcast_to(m_new, m_scr.shape[2:])
        o_gate = _sigmoid(z_ref[b, :, 3 * M_WIDTH:4 * M_WIDTH])
        h_ref[b] = ht_scr[b].T * ng_ref[...] * o_gate

    cn_ref[...] = cn_scr[...]
    m_ref[...] = m_scr[...]


def _mlstm_prompt(z3, gbias, ng):
    b, s, _ = z3.shape
    L = M_CHUNK
    return pl.pallas_call(
        _mlstm_prompt_kernel,
        grid=(s // L,),
        in_specs=[
            pl.BlockSpec((b, L, 4 * M_WIDTH), lambda c: (0, c, Z_M // (4 * M_WIDTH))),
            pl.BlockSpec((b, L, 128), lambda c: (0, c, Z_G // 128)),
            _const_spec((1, 128)),
            _const_spec((1, M_WIDTH)),
        ],
        out_specs=[
            pl.BlockSpec((b, L, M_WIDTH), lambda c: (0, c, 0)),
            pl.BlockSpec((b, M_HEADS, 128, 128), lambda c: (0, 0, 0, 0)),
            pl.BlockSpec((b, M_HEADS, SUBLANES, 128), lambda c: (0, 0, 0, 0)),
        ],
        out_shape=[
            jax.ShapeDtypeStruct((b, s, M_WIDTH), F32),
            jax.ShapeDtypeStruct((b, M_HEADS, 128, 128), F32),
            jax.ShapeDtypeStruct((b, M_HEADS, SUBLANES, 128), F32),
        ],
        scratch_shapes=[
            pltpu.VMEM((b, M_HEADS, 128, 128), F32),
            pltpu.VMEM((b, M_HEADS, SUBLANES, 128), F32),
            pltpu.VMEM((b, M_WIDTH, L), F32),
        ],
        compiler_params=_params("arbitrary"),
        name="mlstm_prompt",
    )(z3, z3, gbias, ng)


CONV_PAD = 32


def _conv_prompt_kernel(z_ref, w_ref, cb_ref, lg_ref, lb_ref, h_ref, tail_ref, buf):
    tc = z_ref.shape[0]
    t = pl.program_id(1)

    @pl.when(t == 0)
    def _():
        buf[0:CONV_PAD, :] = jnp.zeros((CONV_PAD, C_WIDTH), F32)

    @pl.when(t > 0)
    def _():
        buf[0:CONV_PAD, :] = buf[tc:tc + CONV_PAD, :]

    u = z_ref[:, 0:C_WIDTH] * _sigmoid(z_ref[:, C_WIDTH:2 * C_WIDTH])
    buf[CONV_PAD:CONV_PAD + tc, :] = u
    off = CONV_PAD - (C_KERNEL - 1)
    acc = jnp.broadcast_to(cb_ref[...], (tc, C_WIDTH))
    for j in range(C_KERNEL):
        acc = acc + w_ref[j:j + 1, :] * buf[off + j:off + j + tc, :]
    mu = jnp.mean(acc, axis=-1, keepdims=True)
    xc = acc - mu
    y = xc * lax.rsqrt(jnp.mean(xc * xc, axis=-1, keepdims=True) + EPS) * lg_ref[...] + lb_ref[...]
    h_ref[...] = y * _sigmoid(y)
    tail_ref[...] = buf[tc:tc + CONV_PAD, :]


def _conv_prompt(z3, w, cb, lg, lb, *, tc):
    b, s, _ = z3.shape
    return pl.pallas_call(
        _conv_prompt_kernel,
        grid=(b, s // tc),
        in_specs=[
            pl.BlockSpec((None, tc, 2 * C_WIDTH), lambda i, t: (i, t, Z_C // (2 * C_WIDTH))),
            _const_spec((CONV_PAD, C_WIDTH)),
            _const_spec((1, C_WIDTH)), _const_spec((1, C_WIDTH)), _const_spec((1, C_WIDTH)),
        ],
        out_specs=[
            pl.BlockSpec((None, tc, C_WIDTH), lambda i, t: (i, t, 0)),
            pl.BlockSpec((None, CONV_PAD, C_WIDTH), lambda i, t: (i, 0, 0)),
        ],
        out_shape=[
            jax.ShapeDtypeStruct((b, s, C_WIDTH), F32),
            jax.ShapeDtypeStruct((b, CONV_PAD, C_WIDTH), F32),
        ],
        scratch_shapes=[pltpu.VMEM((CONV_PAD + tc, C_WIDTH), F32)],
        compiler_params=_params("arbitrary", "arbitrary"),
        name="conv_prompt",
    )(z3, w, cb, lg, lb)


def _swa_prompt_kernel(rb_ref, sink_ref, bucket_ref, q_ref, kv_ref, kvp_ref, o_ref, kt_ref, vt_ref,
                       bias_scr, s_scr, p_scr, ot_scr):
    W = WINDOW
    H = S_Q_HEADS
    G = H // S_KV_HEADS
    i = pl.program_id(0)
    n = pl.program_id(1)

    @pl.when(jnp.logical_and(i == 0, n == 0))
    def _():
        bucket = bucket_ref[...]
        prev_key = lax.broadcasted_iota(jnp.int32, (2 * W, W), 0) < W
        for h in range(H):
            acc = jnp.full((2 * W, W), NEG_INF, F32)
            for b in range(N_BUCKETS):
                acc = jnp.where(bucket == b, rb_ref[b, h], acc)
            acc = acc * LOG2E
            bias_scr[0, h] = acc
            bias_scr[1, h] = jnp.where(prev_key, NEG_INF, acc)

    first = jnp.where(n == 0, 1, 0)
    kk = jnp.concatenate([kvp_ref[:, 0:S_KV_WIDTH], kv_ref[:, 0:S_KV_WIDTH]], axis=0) * (QK_SCALE * LOG2E)
    vv = jnp.concatenate([kvp_ref[:, S_KV_WIDTH:2 * S_KV_WIDTH], kv_ref[:, S_KV_WIDTH:2 * S_KV_WIDTH]], axis=0)
    kk_r = pltpu.roll(kk, HEAD_DIM, axis=1)
    lo_lane = lax.broadcasted_iota(jnp.int32, (2 * W, S_KV_WIDTH), 1) < HEAD_DIM
    k_var = [[jnp.where(lo_lane, kk, 0.0).astype(BF16), jnp.where(lo_lane, 0.0, kk_r).astype(BF16)],
             [jnp.where(lo_lane, kk_r, 0.0).astype(BF16), jnp.where(lo_lane, 0.0, kk).astype(BF16)]]
    v_t = vv.T.astype(BF16)

    m_rows = [None] * H
    for hk in range(S_KV_HEADS):
        c0 = 2 * 128 * hk
        q_st = jnp.concatenate([q_ref[:, c0:c0 + 128], q_ref[:, c0 + 128:c0 + 256]], axis=0).astype(BF16)
        for half in range(2):
            s_t = _dot_nt(k_var[hk][half], q_st)
            for slab in range(2):
                head = G * hk + 2 * slab + half
                sb = s_t[:, 128 * slab:128 * (slab + 1)] + bias_scr[first, head]
                s_scr[head] = sb
                m_rows[head] = jnp.maximum(jnp.max(sb, axis=0, keepdims=True), sink_ref[head] * LOG2E)

    inv = [None] * H
    for head in range(H):
        e = jnp.exp2(s_scr[head] - m_rows[head])
        den = jnp.sum(e, axis=0, keepdims=True) + jnp.exp2(sink_ref[head] * LOG2E - m_rows[head])
        inv[head] = 1.0 / den
        p_scr[:, 128 * head:128 * (head + 1)] = e.astype(BF16)

    for hk in range(S_KV_HEADS):
        o_t = _dot(v_t[HEAD_DIM * hk:HEAD_DIM * (hk + 1), :], p_scr[:, 128 * G * hk:128 * G * (hk + 1)])
        for g in range(G):
            head = G * hk + g
            ot_scr[HEAD_DIM * head:HEAD_DIM * (head + 1), :] = o_t[:, 128 * g:128 * (g + 1)] * inv[head]
    o_ref[...] = ot_scr[...].T

    @pl.when(n == pl.num_programs(1) - 1)
    def _():
        kt_ref[...] = kv_ref[:, 0:S_KV_WIDTH].T
        vt_ref[...] = kv_ref[:, S_KV_WIDTH:2 * S_KV_WIDTH].T


def _swa_prompt(z3, rel_bias, sinks, bucket):
    b, s, _ = z3.shape
    W = WINDOW
    smem = pl.BlockSpec(memory_space=pltpu.SMEM)
    return pl.pallas_call(
        _swa_prompt_kernel,
        grid=(b, s // W),
        in_specs=[
            smem, smem, _const_spec((2 * W, W)),
            pl.BlockSpec((None, W, S_WIDTH), lambda i, n: (i, n, Z_SQ // S_WIDTH)),
            pl.BlockSpec((None, W, 2 * S_KV_WIDTH), lambda i, n: (i, n, Z_SKV // (2 * S_KV_WIDTH))),
            pl.BlockSpec((None, W, 2 * S_KV_WIDTH), lambda i, n: (i, jnp.maximum(n - 1, 0), Z_SKV // (2 * S_KV_WIDTH))),
        ],
        out_specs=[
            pl.BlockSpec((None, W, S_WIDTH), lambda i, n: (i, n, 0)),
            pl.BlockSpec((None, S_KV_WIDTH, W), lambda i, n: (i, 0, 0)),
            pl.BlockSpec((None, S_KV_WIDTH, W), lambda i, n: (i, 0, 0)),
        ],
        out_shape=[
            jax.ShapeDtypeStruct((b, s, S_WIDTH), F32),
            jax.ShapeDtypeStruct((b, S_KV_WIDTH, W), F32),
            jax.ShapeDtypeStruct((b, S_KV_WIDTH, W), F32),
        ],
        scratch_shapes=[
            pltpu.VMEM((2, S_Q_HEADS, 2 * W, W), F32),
            pltpu.VMEM((S_Q_HEADS, 2 * W, W), F32),
            pltpu.VMEM((2 * W, S_Q_HEADS * W), BF16),
            pltpu.VMEM((S_WIDTH, W), F32),
        ],
        compiler_params=_params("arbitrary", "arbitrary"),
        name="swa_prompt",
    )(rel_bias, sinks, bucket, z3, z3, z3)


def _cross_attend_rows(x1, g2_ref, wq_ref, mk, mv, wo_ref):
    mem = mk.shape[0]
    qx = _dot(_rms(x1, g2_ref[...]).astype(BF16), wq_ref[...]).astype(BF16)
    lane_head = lax.broadcasted_iota(jnp.int32, (mem, X_WIDTH), 1) // HEAD_DIM
    probs, vms = [], []
    for h in range(X_HEADS):
        km = jnp.where(lane_head == h, mk, 0.0).astype(BF16)
        s = _dot_nt(qx, km) * QK_SCALE
        e = jnp.exp(s - jnp.max(s, axis=1, keepdims=True))
        probs.append((e / jnp.sum(e, axis=1, keepdims=True)).astype(BF16))
        vms.append(jnp.where(lane_head == h, mv, 0.0).astype(BF16))
    o = _dot(jnp.concatenate(probs, axis=1), jnp.concatenate(vms, axis=0))
    return x1 + _dot(o.astype(BF16), wo_ref[...])


def _mix_out_prompt_kernel(x_ref, hm_ref, hc_ref, hs_ref, wout_ref, g2_ref, wq_ref, mk_ref, mv_ref, wo_ref, o_ref):
    cat = jnp.concatenate([hm_ref[...].astype(BF16), hc_ref[...].astype(BF16), hs_ref[...].astype(BF16)], axis=1)
    x1 = x_ref[...] + _dot(cat, wout_ref[...])
    o_ref[...] = _cross_attend_rows(x1, g2_ref, wq_ref, mk_ref[...], mv_ref[...], wo_ref)


def _mix_out_prompt(x3, hm, hc, hs, wout, g2, wq, mkv, wo, *, tm):
    b, s, d = x3.shape
    mem = mkv.shape[1]
    row = lambda w: pl.BlockSpec((None, tm, w), lambda i, t: (i, t, 0))
    return pl.pallas_call(
        _mix_out_prompt_kernel,
        grid=(b, s // tm),
        in_specs=[
            row(d), row(M_WIDTH), row(C_WIDTH), row(S_WIDTH),
            _const_spec(wout.shape), _const_spec((1, d)), _const_spec(wq.shape),
            pl.BlockSpec((None, mem, X_WIDTH), lambda i, t: (i, 0, 0)),
            pl.BlockSpec((None, mem, X_WIDTH), lambda i, t: (i, 0, 1)),
            _const_spec(wo.shape),
        ],
        out_specs=row(d),
        out_shape=jax.ShapeDtypeStruct((b, s, d), F32),
        compiler_params=_params("arbitrary", "arbitrary"),
        name="mix_out_prompt",
    )(x3, hm, hc, hs, wout, g2, wq, mkv, mkv, wo)


FFN_PAD = 8


def _ffn_prompt_kernel(x_ref, g3_ref, wup_ref, cw_ref, cb_ref, wdn_ref, gf_ref, o_ref, tail_ref,
                       gbuf, carry, act, *, final):
    tm = x_ref.shape[0]
    dff = wdn_ref.shape[0]

    @pl.when(pl.program_id(1) == 0)
    def _():
        carry[...] = jnp.zeros(carry.shape, F32)

    x = x_ref[...]
    h = _rms(x, g3_ref[...]).astype(BF16)
    for c in range(0, dff, FFN_CHUNK):
        a = _dot(h, wup_ref[:, c:c + FFN_CHUNK])
        g = _dot(h, wup_ref[:, dff + c:dff + c + FFN_CHUNK])
        gbuf[0:FFN_PAD, :] = carry[:, c:c + FFN_CHUNK]
        gbuf[FFN_PAD:FFN_PAD + tm, :] = g
        carry[:, c:c + FFN_CHUNK] = g[tm - FFN_PAD:tm, :]
        gc = (cw_ref[0:1, c:c + FFN_CHUNK] * gbuf[FFN_PAD - 2:FFN_PAD - 2 + tm, :]
              + cw_ref[1:2, c:c + FFN_CHUNK] * gbuf[FFN_PAD - 1:FFN_PAD - 1 + tm, :]
              + cw_ref[2:3, c:c + FFN_CHUNK] * g + cb_ref[:, c:c + FFN_CHUNK])
        act[:, c:c + FFN_CHUNK] = (gc * _sigmoid(gc) * a).astype(BF16)
    y = x + _dot(act[...], wdn_ref[...])
    o_ref[...] = _rms(y, gf_ref[...]) if final else y
    tail_ref[...] = carry[...]


def _ffn_prompt(x3, g3, wup, cw, cb, wdn, gf, *, tm, final):
    b, s, d = x3.shape
    dff = wdn.shape[0]
    return pl.pallas_call(
        functools.partial(_ffn_prompt_kernel, final=final),
        grid=(b, s // tm),
        in_specs=[
            pl.BlockSpec((None, tm, d), lambda i, t: (i, t, 0)),
            _const_spec((1, d)), _const_spec(wup.shape), _const_spec(cw.shape), _const_spec((1, dff)),
            _const_spec(wdn.shape), _const_spec((1, d)),
        ],
        out_specs=[
            pl.BlockSpec((None, tm, d), lambda i, t: (i, t, 0)),
            pl.BlockSpec((None, FFN_PAD, dff), lambda i, t: (i, 0, 0)),
        ],
        out_shape=[jax.ShapeDtypeStruct((b, s, d), F32), jax.ShapeDtypeStruct((b, FFN_PAD, dff), F32)],
        scratch_shapes=[
            pltpu.VMEM((FFN_PAD + tm, FFN_CHUNK), F32),
            pltpu.VMEM((FFN_PAD, dff), F32),
            pltpu.VMEM((tm, dff), BF16),
        ],
        compiler_params=_params("arbitrary", "arbitrary"),
        name="ffn_prompt",
    )(x3, g3, wup, cw, cb, wdn, gf)


MLSTM_SAMPLE_BLOCK = 8
SWA_SAMPLE_BLOCK = 16
XATTN_SAMPLE_BLOCK = 8
CONV_SAMPLE_BLOCK = 32


def _expand_heads(x, width):
    lane_head = lax.broadcasted_iota(jnp.int32, (x.shape[0], width), 1) // HEAD_DIM
    out = jnp.zeros((x.shape[0], width), F32)
    for h in range(width // HEAD_DIM):
        out = jnp.where(lane_head == h, x[:, h:h + 1], out)
    return out


def _head_sums(x):
    lane_head = lax.broadcasted_iota(jnp.int32, x.shape, 1) // HEAD_DIM
    out = jnp.zeros(x.shape, F32)
    for h in range(x.shape[1] // HEAD_DIM):
        sel = lane_head == h
        out = jnp.where(sel, jnp.sum(jnp.where(sel, x, 0.0), axis=1, keepdims=True), out)
    return out


def _mlstm_sample_kernel(z_ref, g_ref, gb_ref, ng_ref, c_ref, n_ref, m_ref,
                         h_ref, co_ref, no_ref, mo_ref, qc_scr):
    R = z_ref.shape[0]
    q = z_ref[:, 0:M_WIDTH]
    k = z_ref[:, M_WIDTH:2 * M_WIDTH] * QK_SCALE
    v = z_ref[:, 2 * M_WIDTH:3 * M_WIDTH]
    o = z_ref[:, 3 * M_WIDTH:4 * M_WIDTH]
    gates = g_ref[...] + gb_ref[...]
    i_pre = gates[:, 0:M_HEADS]
    f_log = _log_sigmoid(gates[:, M_HEADS:2 * M_HEADS])
    a = f_log + m_ref[...]
    m_t = jnp.maximum(a, i_pre)
    w_inter = _expand_heads(jnp.exp(a - m_t), M_WIDTH)
    w_new = _expand_heads(jnp.exp(i_pre - m_t), M_WIDTH)
    floor = _expand_heads(jnp.exp(-m_t), M_WIDTH)
    n_old = n_ref[...]

    eye = (lax.broadcasted_iota(jnp.int32, (HEAD_DIM, M_WIDTH), 1) % HEAD_DIM
           == lax.broadcasted_iota(jnp.int32, (HEAD_DIM, M_WIDTH), 0))
    lane_head = lax.broadcasted_iota(jnp.int32, (HEAD_DIM, M_WIDTH), 1) // HEAD_DIM
    for r in range(R):
        for h in range(M_HEADS):
            lo = h * HEAD_DIM
            sel = jnp.logical_and(eye, lane_head == h)
            q_col = jnp.sum(jnp.where(sel, q[r:r + 1, :], 0.0), axis=1, keepdims=True)
            k_col = jnp.sum(jnp.where(sel, k[r:r + 1, :], 0.0), axis=1, keepdims=True)
            c_old = c_ref[r, lo:lo + HEAD_DIM, :]
            qc_scr[r:r + 1, lo:lo + HEAD_DIM] = jnp.sum(q_col * c_old, axis=0, keepdims=True)
            co_ref[r, lo:lo + HEAD_DIM, :] = (w_inter[r:r + 1, lo:lo + HEAD_DIM] * c_old
                                              + k_col * (w_new[r:r + 1, lo:lo + HEAD_DIM] * v[r:r + 1, lo:lo + HEAD_DIM]))

    s = _head_sums(q * k) * w_new
    num = w_inter * qc_scr[...] + s * v
    den = w_inter * _head_sums(q * n_old) + s
    hh = num / jnp.maximum(jnp.abs(den), floor)
    hh = hh * lax.rsqrt(_head_sums(hh * hh) * (1.0 / HEAD_DIM) + EPS) * ng_ref[...]
    h_ref[...] = hh * _sigmoid(o)
    no_ref[...] = w_inter * n_old + w_new * k
    mo_ref[...] = m_t


def _mlstm_sample(z, gbias, ng, c, n, m):
    bsz = z.shape[0]
    R = MLSTM_SAMPLE_BLOCK
    return pl.pallas_call(
        _mlstm_sample_kernel,
        grid=(bsz // R,),
        in_specs=[
            pl.BlockSpec((R, 4 * M_WIDTH), lambda i: (i, Z_M // (4 * M_WIDTH))),
            pl.BlockSpec((R, 128), lambda i: (i, Z_G // 128)),
            _const_spec((1, 128)), _const_spec((1, M_WIDTH)),
            pl.BlockSpec((R, M_WIDTH, HEAD_DIM), lambda i: (i, 0, 0)),
            pl.BlockSpec((R, M_WIDTH), lambda i: (i, 0)),
            pl.BlockSpec((R, M_HEADS), lambda i: (i, 0)),
        ],
        out_specs=[
            pl.BlockSpec((R, M_WIDTH), lambda i: (i, 0)),
            pl.BlockSpec((R, M_WIDTH, HEAD_DIM), lambda i: (i, 0, 0)),
            pl.BlockSpec((R, M_WIDTH), lambda i: (i, 0)),
            pl.BlockSpec((R, M_HEADS), lambda i: (i, 0)),
        ],
        out_shape=[
            jax.ShapeDtypeStruct((bsz, M_WIDTH), F32),
            jax.ShapeDtypeStruct((bsz, M_WIDTH, HEAD_DIM), F32),
            jax.ShapeDtypeStruct((bsz, M_WIDTH), F32),
            jax.ShapeDtypeStruct((bsz, M_HEADS), F32),
        ],
        scratch_shapes=[pltpu.VMEM((R, M_WIDTH), F32)],
        compiler_params=_params("arbitrary"),
        name="mlstm_sample",
    )(z, z, gbias, ng, c, n, m)


def _conv_sample_kernel(z_ref, hist_ref, w_ref, cb_ref, lg_ref, lb_ref, h_ref, hist_out_ref):
    nh = C_KERNEL - 1
    u = z_ref[:, 0:C_WIDTH] * _sigmoid(z_ref[:, C_WIDTH:2 * C_WIDTH])
    acc = cb_ref[...] + w_ref[nh:nh + 1, :] * u
    for j in range(nh):
        acc = acc + w_ref[j:j + 1, :] * hist_ref[:, j * C_WIDTH:(j + 1) * C_WIDTH]
    mu = jnp.mean(acc, axis=-1, keepdims=True)
    xc = acc - mu
    y = xc * lax.rsqrt(jnp.mean(xc * xc, axis=-1, keepdims=True) + EPS) * lg_ref[...] + lb_ref[...]
    h_ref[...] = y * _sigmoid(y)
    hist_out_ref[:, 0:(nh - 1) * C_WIDTH] = hist_ref[:, C_WIDTH:nh * C_WIDTH]
    hist_out_ref[:, (nh - 1) * C_WIDTH:nh * C_WIDTH] = u


def _conv_sample(z, hist, w, cb, lg, lb):
    bsz = z.shape[0]
    R = min(CONV_SAMPLE_BLOCK, bsz)
    hw = hist.shape[1]
    return pl.pallas_call(
        _conv_sample_kernel,
        grid=(bsz // R,),
        in_specs=[
            pl.BlockSpec((R, 2 * C_WIDTH), lambda i: (i, Z_C // (2 * C_WIDTH))),
            pl.BlockSpec((R, hw), lambda i: (i, 0)),
            _const_spec((CONV_PAD, C_WIDTH)),
            _const_spec((1, C_WIDTH)), _const_spec((1, C_WIDTH)), _const_spec((1, C_WIDTH)),
        ],
        out_specs=[pl.BlockSpec((R, C_WIDTH), lambda i: (i, 0)), pl.BlockSpec((R, hw), lambda i: (i, 0))],
        out_shape=[jax.ShapeDtypeStruct((bsz, C_WIDTH), F32), jax.ShapeDtypeStruct((bsz, hw), F32)],
        compiler_params=_params("arbitrary"),
        name="conv_sample",
    )(z, hist, w, cb, lg, lb)


def _swa_sample_kernel(q_ref, kvn_ref, kc_ref, vc_ref, bias_ref, aux_ref, o_ref, ko_ref, vo_ref):
    R = q_ref.shape[0]
    W = kc_ref.shape[1]
    H = S_Q_HEADS
    shape = (R, H, 128)
    row = lax.broadcasted_iota(jnp.int32, shape, 1)
    lane_half = lax.broadcasted_iota(jnp.int32, shape, 2) // HEAD_DIM
    q_half = row % 2
    kv_head = row // (H // S_KV_HEADS)
    qs = jnp.zeros(shape, F32)
    for j in range(H // 2):
        qs = jnp.where(row // 2 == j, q_ref[:, :, 128 * j:128 * (j + 1)], qs)
    q8 = jnp.where(lane_half == kv_head, jnp.where(q_half == kv_head, qs, pltpu.roll(qs, HEAD_DIM, axis=2)), 0.0)
    k_new = kvn_ref[:, :, 0:S_KV_WIDTH]
    v_new = kvn_ref[:, :, S_KV_WIDTH:2 * S_KV_WIDTH]
    kc = kc_ref[...]
    vc = vc_ref[...]
    s = jnp.einsum("bqd,bkd->bqk", q8.astype(BF16), kc.astype(BF16), preferred_element_type=F32) * QK_SCALE
    s = s + bias_ref[...][None]
    s_new = jnp.sum(q8 * k_new, axis=2, keepdims=True) * QK_SCALE + aux_ref[:, 0:1][None]
    sink = aux_ref[:, 1:2][None]
    m = jnp.maximum(jnp.maximum(jnp.max(s, axis=2, keepdims=True), s_new), sink)
    e = jnp.exp(s - m)
    e_new = jnp.exp(s_new - m)
    inv = 1.0 / (jnp.sum(e, axis=2, keepdims=True) + e_new + jnp.exp(sink - m))
    o8 = jnp.einsum("bqk,bkd->bqd", e.astype(BF16), vc.astype(BF16), preferred_element_type=F32)
    o8 = (o8 + e_new * v_new) * inv
    o8 = jnp.where(lane_half == q_half, jnp.where(q_half == kv_head, o8, pltpu.roll(o8, HEAD_DIM, axis=2)), 0.0)
    for j in range(H // 2):
        o_ref[:, :, 128 * j:128 * (j + 1)] = jnp.sum(jnp.where(row // 2 == j, o8, 0.0), axis=1, keepdims=True)
    ko_ref[:, 0:W - 1, :] = kc_ref[:, 1:W, :]
    ko_ref[:, W - 1:W, :] = k_new
    vo_ref[:, 0:W - 1, :] = vc_ref[:, 1:W, :]
    vo_ref[:, W - 1:W, :] = v_new


def _swa_sample(z3, kc, vc, bias, aux):
    bsz = z3.shape[0]
    R = min(SWA_SAMPLE_BLOCK, bsz)
    W = kc.shape[1]
    cache = pl.BlockSpec((R, W, S_KV_WIDTH), lambda i: (i, 0, 0))
    return pl.pallas_call(
        _swa_sample_kernel,
        grid=(bsz // R,),
        in_specs=[
            pl.BlockSpec((R, 1, S_WIDTH), lambda i: (i, 0, Z_SQ // S_WIDTH)),
            pl.BlockSpec((R, 1, 2 * S_KV_WIDTH), lambda i: (i, 0, Z_SKV // (2 * S_KV_WIDTH))),
            cache, cache,
            _const_spec((S_Q_HEADS, 128)), _const_spec((S_Q_HEADS, 128)),
        ],
        out_specs=[pl.BlockSpec((R, 1, S_WIDTH), lambda i: (i, 0, 0)), cache, cache],
        out_shape=[
            jax.ShapeDtypeStruct((bsz, 1, S_WIDTH), F32),
            jax.ShapeDtypeStruct(kc.shape, F32),
            jax.ShapeDtypeStruct(vc.shape, F32),
        ],
        compiler_params=_params("arbitrary"),
        name="swa_sample",
    )(z3, z3, kc, vc, bias, aux)


def _mix_out_sample_kernel(x_ref, hm_ref, hc_ref, hs_ref, wout_ref, g2_ref, wq_ref, x1_ref, q_ref):
    cat = jnp.concatenate([hm_ref[...].astype(BF16), hc_ref[...].astype(BF16), hs_ref[...].astype(BF16)], axis=1)
    x1 = x_ref[...] + _dot(cat, wout_ref[...])
    x1_ref[...] = x1
    q_ref[...] = _dot(_rms(x1, g2_ref[...]).astype(BF16), wq_ref[...])


def _mix_out_sample(x, hm, hc, hs, wout, g2, wq):
    bsz, d = x.shape
    full = lambda a: pl.BlockSpec(a.shape, lambda i: (0,) * a.ndim)
    args = (x, hm, hc, hs, wout, g2, wq)
    return pl.pallas_call(
        _mix_out_sample_kernel,
        grid=(1,),
        in_specs=[full(a) for a in args],
        out_specs=[pl.BlockSpec((bsz, d), lambda i: (0, 0)), pl.BlockSpec((bsz, X_WIDTH), lambda i: (0, 0))],
        out_shape=[jax.ShapeDtypeStruct((bsz, d), F32), jax.ShapeDtypeStruct((bsz, X_WIDTH), F32)],
        compiler_params=_params("arbitrary"),
        name="mix_out_sample",
    )(*args)


def _xattn_sample_kernel(q_ref, k_ref, v_ref, o_ref):
    R, mem, _ = k_ref.shape
    shape = (R, SUBLANES, X_WIDTH)
    row = lax.broadcasted_iota(jnp.int32, shape, 1)
    lane_head = lax.broadcasted_iota(jnp.int32, shape, 2) // HEAD_DIM
    own = row == lane_head
    q8 = jnp.where(own, jnp.broadcast_to(q_ref[...], shape), 0.0).astype(BF16)
    s = jnp.einsum("bqd,bkd->bqk", q8, k_ref[...].astype(BF16), preferred_element_type=F32) * QK_SCALE
    e = jnp.exp(s - jnp.max(s, axis=2, keepdims=True))
    p = (e / jnp.sum(e, axis=2, keepdims=True)).astype(BF16)
    o8 = jnp.einsum("bqk,bkd->bqd", p, v_ref[...].astype(BF16), preferred_element_type=F32)
    o_ref[...] = jnp.sum(jnp.where(own, o8, 0.0), axis=1, keepdims=True)


def _xattn_sample(q3, k, v):
    bsz, mem, w = k.shape
    R = min(XATTN_SAMPLE_BLOCK, bsz)
    kv = pl.BlockSpec((R, mem, w), lambda i: (i, 0, 0))
    qo = pl.BlockSpec((R, 1, w), lambda i: (i, 0, 0))
    return pl.pallas_call(
        _xattn_sample_kernel,
        grid=(bsz // R,),
        in_specs=[qo, kv, kv],
        out_specs=qo,
        out_shape=jax.ShapeDtypeStruct((bsz, 1, w), F32),
        compiler_params=_params("arbitrary"),
        name="xattn_sample",
    )(q3, k, v)


def _ffn_sample_kernel(x1_ref, ox_ref, wo_ref, g3_ref, wup_ref, cw_ref, cb_ref, wdn_ref, gf_ref, hist_ref,
                       o_ref, hist_out_ref, act, *, final):
    dff = wdn_ref.shape[0]
    x = x1_ref[...] + _dot(ox_ref[...].astype(BF16), wo_ref[...])
    h = _rms(x, g3_ref[...]).astype(BF16)
    for c in range(0, dff, FFN_CHUNK):
        a = _dot(h, wup_ref[:, c:c + FFN_CHUNK])
        g = _dot(h, wup_ref[:, dff + c:dff + c + FFN_CHUNK])
        h1 = hist_ref[:, dff + c:dff + c + FFN_CHUNK]
        gc = (cw_ref[0:1, c:c + FFN_CHUNK] * hist_ref[:, c:c + FFN_CHUNK]
              + cw_ref[1:2, c:c + FFN_CHUNK] * h1
              + cw_ref[2:3, c:c + FFN_CHUNK] * g + cb_ref[:, c:c + FFN_CHUNK])
        act[:, c:c + FFN_CHUNK] = (gc * _sigmoid(gc) * a).astype(BF16)
        hist_out_ref[:, c:c + FFN_CHUNK] = h1
        hist_out_ref[:, dff + c:dff + c + FFN_CHUNK] = g
    y = x + _dot(act[...], wdn_ref[...])
    o_ref[...] = _rms(y, gf_ref[...]) if final else y


def _ffn_sample(x1, ox, wo, g3, wup, cw, cb, wdn, gf, hist, *, final):
    bsz, d = x1.shape
    dff = wdn.shape[0]
    full = lambda a: pl.BlockSpec(a.shape, lambda i: (0,) * a.ndim)
    args = (x1, ox, wo, g3, wup, cw, cb, wdn, gf, hist)
    return pl.pallas_call(
        functools.partial(_ffn_sample_kernel, final=final),
        grid=(1,),
        in_specs=[full(a) for a in args],
        out_specs=[pl.BlockSpec((bsz, d), lambda i: (0, 0)), pl.BlockSpec(hist.shape, lambda i: (0, 0))],
        out_shape=[jax.ShapeDtypeStruct((bsz, d), F32), jax.ShapeDtypeStruct(hist.shape, F32)],
        scratch_shapes=[pltpu.VMEM((bsz, dff), BF16)],
        compiler_params=_params("arbitrary"),
        name="ffn_sample",
    )(*args)


def _t5_buckets(dist):
    n = np.maximum(dist, 0)
    max_exact = N_BUCKETS // 2
    nf = np.maximum(n, max_exact).astype(np.float32)
    large = max_exact + (np.log(nf / np.float32(max_exact)) / np.float32(math.log(MAX_DISTANCE / max_exact))
                         * np.float32(N_BUCKETS - max_exact)).astype(np.int32)
    return np.where(n < max_exact, n, np.minimum(large, N_BUCKETS - 1))


def _prompt_buckets():
    W = WINDOW
    dist = np.arange(W)[None, :] + W - np.arange(2 * W)[:, None]
    band = (dist >= 0) & (dist < W)
    return np.where(band, _t5_buckets(dist), -1).astype(np.int32)


def _swa_tables(rel_bias):
    W = WINDOW
    dist_c = W - np.arange(W)
    tab = jnp.transpose(rel_bias[_t5_buckets(dist_c)], (1, 0))
    cache_bias = jnp.where((dist_c < W)[None], tab, NEG_INF)
    return cache_bias, rel_bias[0]


def _pad_rows(a, rows):
    return jnp.concatenate([a, jnp.zeros((rows - a.shape[0],) + a.shape[1:], a.dtype)], axis=0)


def _row(a):
    return a.reshape(1, -1)


def kernel(x_prompt, x_sample, mem_prompt, state_mlstm_C, state_mlstm_n, state_mlstm_m, state_conv, cache_swa_k, cache_swa_v, cache_mem_k, cache_mem_v, state_ffn_conv, rel_bias, norm1_g, w_in, b_i, b_f, mlstm_norm_g, conv_w, conv_b, conv_ln_g, conv_ln_b, swa_sinks, w_out, norm2_g, w_xq, w_xk, w_xv, w_xo, norm3_g, w_up, ffn_conv_w, ffn_conv_b, w_down, final_norm_g):
    depth = w_in.shape[0]
    bp, seq, d = x_prompt.shape
    bs = x_sample.shape[0]
    mem = mem_prompt.shape[1]
    dff = w_down.shape[1]
    W = WINDOW
    nh = C_KERNEL - 1

    xp = x_prompt
    xs = x_sample.reshape(bs, d)
    mem2 = mem_prompt.reshape(bp * mem, d)
    ones_d = jnp.ones((1, d), F32)
    gf = _row(final_norm_g)
    pm_c, pm_n, pm_m, p_conv, p_k, p_v, p_mk, p_mv, p_ffn = ([] for _ in range(9))
    s_c, s_n, s_m, s_conv, s_k, s_v, s_ffn = ([] for _ in range(7))

    for l in range(depth):
        last = l == depth - 1
        wi = w_in[l]
        gate_cols = wi[:, 4 * M_WIDTH:4 * M_WIDTH + 2 * M_HEADS]
        win = jnp.concatenate(
            [wi[:, 0:4 * M_WIDTH], wi[:, 4 * M_WIDTH + 2 * M_HEADS:], gate_cols,
             jnp.zeros((d, 128 - 2 * M_HEADS), F32)], axis=1).astype(BF16)
        wout = w_out[l].astype(BF16)
        wxq = w_xq[l].astype(BF16)
        wxkv = jnp.concatenate([w_xk[l], w_xv[l]], axis=1).astype(BF16)
        wxo = w_xo[l].astype(BF16)
        wup = w_up[l].astype(BF16)
        wdn = w_down[l].astype(BF16)
        g1, g2, g3 = _row(norm1_g[l]), _row(norm2_g[l]), _row(norm3_g[l])
        gbias = jnp.concatenate([b_i[l], b_f[l], jnp.zeros((128 - 2 * M_HEADS,), F32)]).reshape(1, 128)
        ng = _row(mlstm_norm_g[l])
        cw = _pad_rows(conv_w[l], CONV_PAD)
        cb, lg, lb = _row(conv_b[l]), _row(conv_ln_g[l]), _row(conv_ln_b[l])
        fcw = _pad_rows(ffn_conv_w[l], SUBLANES)
        fcb = _row(ffn_conv_b[l])
        cache_bias, bias0 = _swa_tables(rel_bias)
        aux = jnp.concatenate([bias0[:, None], swa_sinks[l][:, None], jnp.zeros((S_Q_HEADS, 126), F32)], axis=1)

        mkv = _norm_matmul(mem2, ones_d, wxkv, norm=False, tm=512).reshape(bp, mem, 2 * X_WIDTH)
        z = _norm_matmul(xp.reshape(bp * seq, d), g1, win, norm=True, tm=512).reshape(bp, seq, Z_WIDTH)
        hm, cn, mm = _mlstm_prompt(z, gbias, ng)
        hc, ctail = _conv_prompt(z, cw, cb, lg, lb, tc=512)
        hs, kt, vt = _swa_prompt(z, rel_bias, swa_sinks[l], jnp.asarray(_prompt_buckets()))
        xp = _mix_out_prompt(xp, hm, hc, hs, wout, g2, wxq, mkv, wxo, tm=512)
        xp, ftail = _ffn_prompt(xp, g3, wup, fcw, fcb, wdn, gf, tm=512, final=last)
        own = [cn[:, h, :, HEAD_DIM * (h % 2):HEAD_DIM * (h % 2 + 1)] for h in range(M_HEADS)]
        pm_c.append(jnp.stack([jnp.swapaxes(t[:, 0:HEAD_DIM, :], 1, 2) for t in own], axis=1))
        pm_n.append(jnp.stack([t[:, HEAD_DIM, :] for t in own], axis=1))
        pm_m.append(mm[:, :, 0, 0])
        p_conv.append(ctail[:, CONV_PAD - nh:, :])
        p_k.append(jnp.transpose(kt.reshape(bp, S_KV_HEADS, HEAD_DIM, W), (0, 3, 1, 2)))
        p_v.append(jnp.transpose(vt.reshape(bp, S_KV_HEADS, HEAD_DIM, W), (0, 3, 1, 2)))
        p_mk.append(mkv[..., 0:X_WIDTH].reshape(bp, mem, X_HEADS, HEAD_DIM))
        p_mv.append(mkv[..., X_WIDTH:].reshape(bp, mem, X_HEADS, HEAD_DIM))
        p_ffn.append(ftail[:, FFN_PAD - (FFN_KERNEL - 1):, :])

        zs = _norm_matmul(xs, g1, win, norm=True, tm=bs)
        hm_s, c_new, n_new, m_new = _mlstm_sample(
            zs, gbias, ng, state_mlstm_C[l].reshape(bs, M_WIDTH, HEAD_DIM),
            state_mlstm_n[l].reshape(bs, M_WIDTH), state_mlstm_m[l])
        hc_s, conv_new = _conv_sample(zs, state_conv[l].reshape(bs, nh * C_WIDTH), cw, cb, lg, lb)
        hs_s, k_new, v_new = _swa_sample(
            zs.reshape(bs, 1, Z_WIDTH), cache_swa_k[l].reshape(bs, W, S_KV_WIDTH),
            cache_swa_v[l].reshape(bs, W, S_KV_WIDTH), cache_bias, aux)
        x1, qx = _mix_out_sample(xs, hm_s, hc_s, hs_s.reshape(bs, S_WIDTH), wout, g2, wxq)
        ox = _xattn_sample(qx.reshape(bs, 1, X_WIDTH), cache_mem_k[l].reshape(bs, mem, X_WIDTH),
                           cache_mem_v[l].reshape(bs, mem, X_WIDTH))
        xs, ffn_new = _ffn_sample(x1, ox.reshape(bs, X_WIDTH), wxo, g3, wup, fcw, fcb, wdn, gf,
                                  state_ffn_conv[l].reshape(bs, (FFN_KERNEL - 1) * dff), final=last)
        s_c.append(c_new.reshape(bs, M_HEADS, HEAD_DIM, HEAD_DIM))
        s_n.append(n_new.reshape(bs, M_HEADS, HEAD_DIM))
        s_m.append(m_new)
        s_conv.append(conv_new.reshape(bs, nh, C_WIDTH))
        s_k.append(k_new.reshape(bs, W, S_KV_HEADS, HEAD_DIM))
        s_v.append(v_new.reshape(bs, W, S_KV_HEADS, HEAD_DIM))
        s_ffn.append(ffn_new.reshape(bs, FFN_KERNEL - 1, dff))

    st = jnp.stack
    return (xp, xs.reshape(bs, 1, d),
            st(pm_c), st(pm_n), st(pm_m), st(p_conv), st(p_k), st(p_v), st(p_mk), st(p_mv), st(p_ffn),
            st(s_c), st(s_n), st(s_m), st(s_conv), st(s_k), st(s_v), st(s_ffn))
```

```python
import functools
import math

import numpy as np
import jax
import jax.numpy as jnp
from jax import lax
from jax.experimental import pallas as pl
from jax.experimental.pallas import tpu as pltpu

F32 = jnp.float32
BF16 = jnp.bfloat16
EPS = 1e-6
NEG_INF = float("-inf")

HEAD_DIM = 64
M_HEADS = 4
M_WIDTH = M_HEADS * HEAD_DIM
C_WIDTH = 256
C_KERNEL = 31
S_Q_HEADS = 8
S_KV_HEADS = 2
S_WIDTH = S_Q_HEADS * HEAD_DIM
S_KV_WIDTH = S_KV_HEADS * HEAD_DIM
WINDOW = 128
N_BUCKETS = 32
MAX_DISTANCE = 128
X_HEADS = 4
X_WIDTH = X_HEADS * HEAD_DIM
FFN_KERNEL = 3
QK_SCALE = HEAD_DIM ** -0.5
LOG2E = math.log2(math.e)

Z_M = 0
Z_C = 4 * M_WIDTH
Z_SQ = Z_C + 2 * C_WIDTH
Z_SKV = Z_SQ + S_WIDTH
Z_G = Z_SKV + 2 * S_KV_WIDTH
Z_WIDTH = Z_G + 128

LANES = 128
SUBLANES = 8
VMEM_LIMIT = 56 * 1024 * 1024

M_CHUNK = 128
FFN_CHUNK = 256


def _params(*sem):
    return pltpu.CompilerParams(dimension_semantics=sem, vmem_limit_bytes=VMEM_LIMIT)


def _const_spec(shape):
    nd = len(shape)
    return pl.BlockSpec(shape, lambda *_: (0,) * nd, pipeline_mode=pl.Buffered(1))


def _rms(x, g):
    return x * lax.rsqrt(jnp.mean(x * x, axis=-1, keepdims=True) + EPS) * g


def _sigmoid(x):
    return 1.0 / (1.0 + jnp.exp(-x))


def _log_sigmoid(x):
    return jnp.minimum(x, 0.0) - jnp.log1p(jnp.exp(-jnp.abs(x)))


def _dot(a, b):
    return jnp.dot(a, b, preferred_element_type=F32)


def _dot_nt(a, b):
    return lax.dot_general(a, b, (((1,), (1,)), ((), ())), preferred_element_type=F32)


def _dot_tn(a, b):
    return lax.dot_general(a, b, (((0,), (0,)), ((), ())), preferred_element_type=F32)


def _norm_matmul_kernel(x_ref, g_ref, w_ref, o_ref, *, norm, col_chunk):
    x = x_ref[...]
    h = (_rms(x, g_ref[...]) if norm else x).astype(BF16)
    n = o_ref.shape[1]
    for c in range(0, n, col_chunk):
        w = min(col_chunk, n - c)
        o_ref[:, c:c + w] = _dot(h, w_ref[:, c:c + w])


def _norm_matmul(x, g, w, *, norm, tm):
    m, d = x.shape
    n = w.shape[1]
    tm = min(tm, m)
    return pl.pallas_call(
        functools.partial(_norm_matmul_kernel, norm=norm, col_chunk=512),
        grid=(m // tm,),
        in_specs=[pl.BlockSpec((tm, d), lambda i: (i, 0)), _const_spec((1, d)), _const_spec((d, n))],
        out_specs=pl.BlockSpec((tm, n), lambda i: (i, 0)),
        out_shape=jax.ShapeDtypeStruct((m, n), F32),
        compiler_params=_params("arbitrary"),
        name="norm_matmul" if norm else "matmul",
    )(x, g, w)


def _mem_kv_kernel(x_ref, w_ref, o_ref, acc):
    acc[...] = _dot(x_ref[...].astype(BF16), w_ref[...])
    o_ref[...] = acc[...].T


def _mem_kv(mem3, w):
    b, mem, d = mem3.shape
    n = w.shape[1]
    return pl.pallas_call(
        _mem_kv_kernel,
        grid=(b,),
        in_specs=[pl.BlockSpec((None, mem, d), lambda i: (i, 0, 0)), _const_spec((d, n))],
        out_specs=pl.BlockSpec((None, n, mem), lambda i: (i, 0, 0)),
        out_shape=jax.ShapeDtypeStruct((b, n, mem), F32),
        scratch_shapes=[pltpu.VMEM((mem, n), F32)],
        compiler_params=_params("arbitrary"),
        name="mem_kv",
    )(mem3, w)


def _sample_in_kernel(x_ref, g_ref, w_ref, z_ref, zt_ref):
    h = _rms(x_ref[...], g_ref[...]).astype(BF16)
    n = z_ref.shape[1]
    for c in range(0, n, 512):
        w = min(512, n - c)
        zc = _dot(h, w_ref[:, c:c + w])
        z_ref[:, c:c + w] = zc
        zt_ref[c:c + w, :] = zc.T


def _sample_in(x, g, w):
    m, d = x.shape
    n = w.shape[1]
    full = lambda shape: pl.BlockSpec(shape, lambda i: (0,) * len(shape))
    return pl.pallas_call(
        _sample_in_kernel,
        grid=(1,),
        in_specs=[full((m, d)), full((1, d)), full((d, n))],
        out_specs=[full((m, n)), full((n, m))],
        out_shape=[jax.ShapeDtypeStruct((m, n), F32), jax.ShapeDtypeStruct((n, m), F32)],
        compiler_params=_params("arbitrary"),
        name="sample_in",
    )(x, g, w)


def _mlstm_prompt_kernel(z_ref, g_ref, gb_ref, ng_ref, h_ref, cn_ref, m_ref, cn_scr, m_scr, ht_scr):
    NB, L = z_ref.shape[0], z_ref.shape[1]
    hi = lax.Precision.HIGHEST

    @pl.when(pl.program_id(0) == 0)
    def _():
        cn_scr[...] = jnp.zeros(cn_scr.shape, F32)
        m_scr[...] = jnp.zeros(m_scr.shape, F32)

    src = lax.broadcasted_iota(jnp.int32, (L, L), 0)
    qry = lax.broadcasted_iota(jnp.int32, (L, L), 1)
    causal_t = src <= qry
    upper = jnp.where(causal_t, 1.0, 0.0)
    lower = jnp.where(src >= qry, 1.0, 0.0)
    lane_half = lax.broadcasted_iota(jnp.int32, (L, 128), 1) // HEAD_DIM
    row128 = lax.broadcasted_iota(jnp.int32, (128, 128), 0)

    for b in range(NB):
        gates = g_ref[b] + gb_ref[...]
        gates_t = gates.T[0:SUBLANES, :]
        b_rows = jnp.dot(_log_sigmoid(gates_t), upper, precision=hi, preferred_element_type=F32)
        b_cols = jnp.dot(lower, _log_sigmoid(gates), precision=hi, preferred_element_type=F32)
        for h in range(M_HEADS):
            par = h % 2
            slab = 128 * (h // 2)
            own = lane_half == par
            b_row = b_rows[M_HEADS + h:M_HEADS + h + 1, :]
            ci_col = gates[:, h:h + 1] - b_cols[:, M_HEADS + h:M_HEADS + h + 1]
            m_prev = m_scr[b, h, 0:1, 0:1]
            a_row = b_row + m_prev
            dm = jnp.where(causal_t, b_row + ci_col, NEG_INF)
            m_row = jnp.maximum(a_row, jnp.max(dm, axis=0, keepdims=True))
            w_inter = jnp.exp(a_row - m_row)
            q2 = z_ref[b, :, slab:slab + 128].astype(BF16)
            k2 = jnp.where(own, z_ref[b, :, M_WIDTH + slab:M_WIDTH + slab + 128] * QK_SCALE, 0.0)
            v2 = z_ref[b, :, 2 * M_WIDTH + slab:2 * M_WIDTH + slab + 128].astype(BF16)
            s_t = _dot_nt(k2.astype(BF16), q2) * jnp.exp(dm - m_row)
            cn = cn_scr[b, h]
            nd = w_inter * _dot_nt(cn.astype(BF16), q2)
            pv = _dot_tn(v2, s_t.astype(BF16))
            num = nd[0:HEAD_DIM, :] + pv[HEAD_DIM * par:HEAD_DIM * (par + 1), :]
            den = nd[HEAD_DIM:HEAD_DIM + 1, :] + jnp.sum(s_t, axis=0, keepdims=True)
            hh = num * (1.0 / jnp.maximum(jnp.abs(den), jnp.exp(-m_row)))
            hh = hh * lax.rsqrt(jnp.mean(hh * hh, axis=0, keepdims=True) + EPS)
            ht_scr[b, HEAD_DIM * h:HEAD_DIM * (h + 1), :] = hh
            b_last = b_row[:, L - 1:L]
            g_col = b_last + ci_col
            m_new = jnp.maximum(b_last + m_prev, jnp.max(g_col, axis=0, keepdims=True))
            decay = jnp.exp(b_last + m_prev - m_new)
            kw = k2 * jnp.exp(g_col - m_new)
            upd = _dot_tn(v2, kw.astype(BF16))
            c_new = decay * cn[0:HEAD_DIM, :] + upd[HEAD_DIM * par:HEAD_DIM * (par + 1), :]
            n_new = decay * cn[HEAD_DIM:HEAD_DIM + 1, :] + jnp.sum(kw, axis=0, keepdims=True)
            cn_scr[b, h, 0:HEAD_DIM, :] = c_new
            cn_scr[b, h, HEAD_DIM:128, :] = jnp.where(row128[HEAD_DIM:128, :] == HEAD_DIM, n_new, 0.0)
            m_scr[b, h] = jnp.broadcast_to(m_new, m_scr.shape[2:])
        o_gate = _sigmoid(z_ref[b, :, 3 * M_WIDTH:4 * M_WIDTH])
        h_ref[b] = ht_scr[b].T * ng_ref[...] * o_gate

    cn_ref[...] = cn_scr[...]
    m_ref[...] = m_scr[...]


def _mlstm_prompt(z3, gbias, ng):
    b, s, _ = z3.shape
    L = M_CHUNK
    return pl.pallas_call(
        _mlstm_prompt_kernel,
        grid=(s // L,),
        in_specs=[
            pl.BlockSpec((b, L, 4 * M_WIDTH), lambda c: (0, c, Z_M // (4 * M_WIDTH))),
            pl.BlockSpec((b, L, 128), lambda c: (0, c, Z_G // 128)),
            _const_spec((1, 128)),
            _const_spec((1, M_WIDTH)),
        ],
        out_specs=[
            pl.BlockSpec((b, L, M_WIDTH), lambda c: (0, c, 0)),
            pl.BlockSpec((b, M_HEADS, 128, 128), lambda c: (0, 0, 0, 0)),
            pl.BlockSpec((b, M_HEADS, SUBLANES, 128), lambda c: (0, 0, 0, 0)),
        ],
        out_shape=[
            jax.ShapeDtypeStruct((b, s, M_WIDTH), F32),
            jax.ShapeDtypeStruct((b, M_HEADS, 128, 128), F32),
            jax.ShapeDtypeStruct((b, M_HEADS, SUBLANES, 128), F32),
        ],
        scratch_shapes=[
            pltpu.VMEM((b, M_HEADS, 128, 128), F32),
            pltpu.VMEM((b, M_HEADS, SUBLANES, 128), F32),
            pltpu.VMEM((b, M_WIDTH, L), F32),
        ],
        compiler_params=_params("arbitrary"),
        name="mlstm_prompt",
    )(z3, z3, gbias, ng)


CONV_PAD = 32


def _conv_prompt_kernel(z_ref, w_ref, cb_ref, lg_ref, lb_ref, h_ref, tail_ref, buf):
    tc = z_ref.shape[0]
    t = pl.program_id(1)

    @pl.when(t == 0)
    def _():
        buf[0:CONV_PAD, :] = jnp.zeros((CONV_PAD, C_WIDTH), F32)

    @pl.when(t > 0)
    def _():
        buf[0:CONV_PAD, :] = buf[tc:tc + CONV_PAD, :]

    u = z_ref[:, 0:C_WIDTH] * _sigmoid(z_ref[:, C_WIDTH:2 * C_WIDTH])
    buf[CONV_PAD:CONV_PAD + tc, :] = u
    off = CONV_PAD - (C_KERNEL - 1)
    acc = jnp.broadcast_to(cb_ref[...], (tc, C_WIDTH))
    for j in range(C_KERNEL):
        acc = acc + w_ref[j:j + 1, :] * buf[off + j:off + j + tc, :]
    mu = jnp.mean(acc, axis=-1, keepdims=True)
    xc = acc - mu
    y = xc * lax.rsqrt(jnp.mean(xc * xc, axis=-1, keepdims=True) + EPS) * lg_ref[...] + lb_ref[...]
    h_ref[...] = y * _sigmoid(y)
    tail_ref[...] = buf[tc:tc + CONV_PAD, :]


def _conv_prompt(z3, w, cb, lg, lb, *, tc):
    b, s, _ = z3.shape
    return pl.pallas_call(
        _conv_prompt_kernel,
        grid=(b, s // tc),
        in_specs=[
            pl.BlockSpec((None, tc, 2 * C_WIDTH), lambda i, t: (i, t, Z_C // (2 * C_WIDTH))),
            _const_spec((CONV_PAD, C_WIDTH)),
            _const_spec((1, C_WIDTH)), _const_spec((1, C_WIDTH)), _const_spec((1, C_WIDTH)),
        ],
        out_specs=[
            pl.BlockSpec((None, tc, C_WIDTH), lambda i, t: (i, t, 0)),
            pl.BlockSpec((None, CONV_PAD, C_WIDTH), lambda i, t: (i, 0, 0)),
        ],
        out_shape=[
            jax.ShapeDtypeStruct((b, s, C_WIDTH), F32),
            jax.ShapeDtypeStruct((b, CONV_PAD, C_WIDTH), F32),
        ],
        scratch_shapes=[pltpu.VMEM((CONV_PAD + tc, C_WIDTH), F32)],
        compiler_params=_params("arbitrary", "arbitrary"),
        name="conv_prompt",
    )(z3, w, cb, lg, lb)


def _swa_prompt_kernel(rb_ref, sink_ref, bucket_ref, q_ref, kv_ref, kvp_ref, o_ref, kt_ref, vt_ref,
                       bias_scr, s_scr, p_scr, ot_scr):
    W = WINDOW
    H = S_Q_HEADS
    G = H // S_KV_HEADS
    i = pl.program_id(0)
    n = pl.program_id(1)

    @pl.when(jnp.logical_and(i == 0, n == 0))
    def _():
        bucket = bucket_ref[...]
        prev_key = lax.broadcasted_iota(jnp.int32, (2 * W, W), 0) < W
        for h in range(H):
            acc = jnp.full((2 * W, W), NEG_INF, F32)
            for b in range(N_BUCKETS):
                acc = jnp.where(bucket == b, rb_ref[b, h], acc)
            acc = acc * LOG2E
            bias_scr[0, h] = acc
            bias_scr[1, h] = jnp.where(prev_key, NEG_INF, acc)

    first = jnp.where(n == 0, 1, 0)
    kk = jnp.concatenate([kvp_ref[:, 0:S_KV_WIDTH], kv_ref[:, 0:S_KV_WIDTH]], axis=0) * (QK_SCALE * LOG2E)
    vv = jnp.concatenate([kvp_ref[:, S_KV_WIDTH:2 * S_KV_WIDTH], kv_ref[:, S_KV_WIDTH:2 * S_KV_WIDTH]], axis=0)
    kk_r = pltpu.roll(kk, HEAD_DIM, axis=1)
    lo_lane = lax.broadcasted_iota(jnp.int32, (2 * W, S_KV_WIDTH), 1) < HEAD_DIM
    k_var = [[jnp.where(lo_lane, kk, 0.0).astype(BF16), jnp.where(lo_lane, 0.0, kk_r).astype(BF16)],
             [jnp.where(lo_lane, kk_r, 0.0).astype(BF16), jnp.where(lo_lane, 0.0, kk).astype(BF16)]]
    v_t = vv.T.astype(BF16)

    m_rows = [None] * H
    for hk in range(S_KV_HEADS):
        c0 = 2 * 128 * hk
        q_st = jnp.concatenate([q_ref[:, c0:c0 + 128], q_ref[:, c0 + 128:c0 + 256]], axis=0).astype(BF16)
        for half in range(2):
            s_t = _dot_nt(k_var[hk][half], q_st)
            for slab in range(2):
                head = G * hk + 2 * slab + half
                sb = s_t[:, 128 * slab:128 * (slab + 1)] + bias_scr[first, head]
                s_scr[head] = sb
                m_rows[head] = jnp.maximum(jnp.max(sb, axis=0, keepdims=True), sink_ref[head] * LOG2E)

    inv = [None] * H
    for head in range(H):
        e = jnp.exp2(s_scr[head] - m_rows[head])
        den = jnp.sum(e, axis=0, keepdims=True) + jnp.exp2(sink_ref[head] * LOG2E - m_rows[head])
        inv[head] = 1.0 / den
        p_scr[:, 128 * head:128 * (head + 1)] = e.astype(BF16)

    for hk in range(S_KV_HEADS):
        o_t = _dot(v_t[HEAD_DIM * hk:HEAD_DIM * (hk + 1), :], p_scr[:, 128 * G * hk:128 * G * (hk + 1)])
        for g in range(G):
            head = G * hk + g
            ot_scr[HEAD_DIM * head:HEAD_DIM * (head + 1), :] = o_t[:, 128 * g:128 * (g + 1)] * inv[head]
    o_ref[...] = ot_scr[...].T

    @pl.when(n == pl.num_programs(1) - 1)
    def _():
        kt_ref[...] = kv_ref[:, 0:S_KV_WIDTH].T
        vt_ref[...] = kv_ref[:, S_KV_WIDTH:2 * S_KV_WIDTH].T


def _swa_prompt(z3, rel_bias, sinks, bucket):
    b, s, _ = z3.shape
    W = WINDOW
    smem = pl.BlockSpec(memory_space=pltpu.SMEM)
    return pl.pallas_call(
        _swa_prompt_kernel,
        grid=(b, s // W),
        in_specs=[
            smem, smem, _const_spec((2 * W, W)),
            pl.BlockSpec((None, W, S_WIDTH), lambda i, n: (i, n, Z_SQ // S_WIDTH)),
            pl.BlockSpec((None, W, 2 * S_KV_WIDTH), lambda i, n: (i, n, Z_SKV // (2 * S_KV_WIDTH))),
            pl.BlockSpec((None, W, 2 * S_KV_WIDTH), lambda i, n: (i, jnp.maximum(n - 1, 0), Z_SKV // (2 * S_KV_WIDTH))),
        ],
        out_specs=[
            pl.BlockSpec((None, W, S_WIDTH), lambda i, n: (i, n, 0)),
            pl.BlockSpec((None, S_KV_WIDTH, W), lambda i, n: (i, 0, 0)),
            pl.BlockSpec((None, S_KV_WIDTH, W), lambda i, n: (i, 0, 0)),
        ],
        out_shape=[
            jax.ShapeDtypeStruct((b, s, S_WIDTH), F32),
            jax.ShapeDtypeStruct((b, S_KV_WIDTH, W), F32),
            jax.ShapeDtypeStruct((b, S_KV_WIDTH, W), F32),
        ],
        scratch_shapes=[
            pltpu.VMEM((2, S_Q_HEADS, 2 * W, W), F32),
            pltpu.VMEM((S_Q_HEADS, 2 * W, W), F32),
            pltpu.VMEM((2 * W, S_Q_HEADS * W), BF16),
            pltpu.VMEM((S_WIDTH, W), F32),
        ],
        compiler_params=_params("arbitrary", "arbitrary"),
        name="swa_prompt",
    )(rel_bias, sinks, bucket, z3, z3, z3)


def _cross_attend_rows(x1, g2_ref, wq_ref, mk_t, mv_t, wo_ref):
    qx = _dot(_rms(x1, g2_ref[...]).astype(BF16), wq_ref[...]).astype(BF16)
    row_head = lax.broadcasted_iota(jnp.int32, mk_t.shape, 0) // HEAD_DIM
    probs, vms = [], []
    for h in range(X_HEADS):
        km = jnp.where(row_head == h, mk_t, 0.0).astype(BF16)
        s = _dot(qx, km) * QK_SCALE
        e = jnp.exp(s - jnp.max(s, axis=1, keepdims=True))
        probs.append((e / jnp.sum(e, axis=1, keepdims=True)).astype(BF16))
        vms.append(jnp.where(row_head == h, mv_t, 0.0).astype(BF16))
    o = _dot_nt(jnp.concatenate(probs, axis=1), jnp.concatenate(vms, axis=1))
    return x1 + _dot(o.astype(BF16), wo_ref[...])


def _mix_out_prompt_kernel(x_ref, hm_ref, hc_ref, hs_ref, wout_ref, g2_ref, wq_ref, mk_ref, mv_ref, wo_ref, o_ref):
    cat = jnp.concatenate([hm_ref[...].astype(BF16), hc_ref[...].astype(BF16), hs_ref[...].astype(BF16)], axis=1)
    x1 = x_ref[...] + _dot(cat, wout_ref[...])
    o_ref[...] = _cross_attend_rows(x1, g2_ref, wq_ref, mk_ref[...], mv_ref[...], wo_ref)


def _mix_out_prompt(x3, hm, hc, hs, wout, g2, wq, mkv, wo, *, tm):
    b, s, d = x3.shape
    mem = mkv.shape[2]
    row = lambda w: pl.BlockSpec((None, tm, w), lambda i, t: (i, t, 0))
    return pl.pallas_call(
        _mix_out_prompt_kernel,
        grid=(b, s // tm),
        in_specs=[
            row(d), row(M_WIDTH), row(C_WIDTH), row(S_WIDTH),
            _const_spec(wout.shape), _const_spec((1, d)), _const_spec(wq.shape),
            pl.BlockSpec((None, X_WIDTH, mem), lambda i, t: (i, 0, 0)),
            pl.BlockSpec((None, X_WIDTH, mem), lambda i, t: (i, 1, 0)),
            _const_spec(wo.shape),
        ],
        out_specs=row(d),
        out_shape=jax.ShapeDtypeStruct((b, s, d), F32),
        compiler_params=_params("arbitrary", "arbitrary"),
        name="mix_out_prompt",
    )(x3, hm, hc, hs, wout, g2, wq, mkv, mkv, wo)


FFN_PAD = 8


def _ffn_prompt_kernel(x_ref, g3_ref, wup_ref, cw_ref, cb_ref, wdn_ref, gf_ref, o_ref, tail_ref,
                       gbuf, carry, act, *, final):
    tm = x_ref.shape[0]
    dff = wdn_ref.shape[0]

    @pl.when(pl.program_id(1) == 0)
    def _():
        carry[...] = jnp.zeros(carry.shape, F32)

    x = x_ref[...]
    h = _rms(x, g3_ref[...]).astype(BF16)
    for c in range(0, dff, FFN_CHUNK):
        a = _dot(h, wup_ref[:, c:c + FFN_CHUNK])
        g = _dot(h, wup_ref[:, dff + c:dff + c + FFN_CHUNK])
        gbuf[0:FFN_PAD, :] = carry[:, c:c + FFN_CHUNK]
        gbuf[FFN_PAD:FFN_PAD + tm, :] = g
        carry[:, c:c + FFN_CHUNK] = g[tm - FFN_PAD:tm, :]
        gc = (cw_ref[0:1, c:c + FFN_CHUNK] * gbuf[FFN_PAD - 2:FFN_PAD - 2 + tm, :]
              + cw_ref[1:2, c:c + FFN_CHUNK] * gbuf[FFN_PAD - 1:FFN_PAD - 1 + tm, :]
              + cw_ref[2:3, c:c + FFN_CHUNK] * g + cb_ref[:, c:c + FFN_CHUNK])
        act[:, c:c + FFN_CHUNK] = (gc * _sigmoid(gc) * a).astype(BF16)
    y = x + _dot(act[...], wdn_ref[...])
    o_ref[...] = _rms(y, gf_ref[...]) if final else y
    tail_ref[...] = carry[...]


def _ffn_prompt(x3, g3, wup, cw, cb, wdn, gf, *, tm, final):
    b, s, d = x3.shape
    dff = wdn.shape[0]
    return pl.pallas_call(
        functools.partial(_ffn_prompt_kernel, final=final),
        grid=(b, s // tm),
        in_specs=[
            pl.BlockSpec((None, tm, d), lambda i, t: (i, t, 0)),
            _const_spec((1, d)), _const_spec(wup.shape), _const_spec(cw.shape), _const_spec((1, dff)),
            _const_spec(wdn.shape), _const_spec((1, d)),
        ],
        out_specs=[
            pl.BlockSpec((None, tm, d), lambda i, t: (i, t, 0)),
            pl.BlockSpec((None, FFN_PAD, dff), lambda i, t: (i, 0, 0)),
        ],
        out_shape=[jax.ShapeDtypeStruct((b, s, d), F32), jax.ShapeDtypeStruct((b, FFN_PAD, dff), F32)],
        scratch_shapes=[
            pltpu.VMEM((FFN_PAD + tm, FFN_CHUNK), F32),
            pltpu.VMEM((FFN_PAD, dff), F32),
            pltpu.VMEM((tm, dff), BF16),
        ],
        compiler_params=_params("arbitrary", "arbitrary"),
        name="ffn_prompt",
    )(x3, g3, wup, cw, cb, wdn, gf)


SWA_SAMPLE_BLOCK = 16
XATTN_SAMPLE_BLOCK = 8
CONV_SAMPLE_BLOCK = 32


def _mlstm_sample_kernel(gb_ref, q_ref, k_ref, v_ref, o_ref, g_ref, ng_ref, c_ref, n_ref, m_ref,
                         h_ref, co_ref, no_ref, mo_ref, kw_scr):
    h = pl.program_id(0)
    i_pre = g_ref[pl.ds(h, 1), :] + gb_ref[h]
    f_pre = g_ref[pl.ds(M_HEADS + h, 1), :] + gb_ref[M_HEADS + h]
    a = _log_sigmoid(f_pre) + m_ref[pl.ds(h, 1), :]
    m_t = jnp.maximum(a, i_pre)
    w_old = jnp.exp(a - m_t)
    w_new = jnp.exp(i_pre - m_t)
    q = q_ref[...]
    k = k_ref[...] * QK_SCALE
    v = v_ref[...]
    n_old = n_ref[...]
    kw_scr[...] = k * w_new

    def body(d, acc):
        c_old = c_ref[d]
        co_ref[d] = w_old * c_old + kw_scr[pl.ds(d, 1), :] * v
        return acc + q_ref[pl.ds(d, 1), :] * c_old

    qc = lax.fori_loop(0, HEAD_DIM, body, jnp.zeros(v.shape, F32), unroll=8)
    s = jnp.sum(q * k, axis=0, keepdims=True) * w_new
    num = w_old * qc + s * v
    den = w_old * jnp.sum(q * n_old, axis=0, keepdims=True) + s
    hh = num / jnp.maximum(jnp.abs(den), jnp.exp(-m_t))
    hh = hh * lax.rsqrt(jnp.mean(hh * hh, axis=0, keepdims=True) + EPS) * ng_ref[...]
    h_ref[...] = hh * _sigmoid(o_ref[...])
    no_ref[...] = w_old * n_old + kw_scr[...]
    mo_ref[pl.ds(h, 1), :] = m_t


def _mlstm_sample(l, zt, gb, ngt, c_all, n_all, m_all):
    bsz = zt.shape[1]
    D = HEAD_DIM
    feat = lambda off: pl.BlockSpec((D, bsz), lambda h: (off // D + h, 0))
    return pl.pallas_call(
        _mlstm_sample_kernel,
        grid=(M_HEADS,),
        in_specs=[
            pl.BlockSpec(memory_space=pltpu.SMEM),
            feat(Z_M), feat(Z_M + M_WIDTH), feat(Z_M + 2 * M_WIDTH), feat(Z_M + 3 * M_WIDTH),
            pl.BlockSpec((SUBLANES, bsz), lambda h: (Z_G // SUBLANES, 0)),
            pl.BlockSpec((D, bsz), lambda h: (h, 0)),
            pl.BlockSpec((None, None, D, D, bsz), lambda h: (l, h, 0, 0, 0)),
            pl.BlockSpec((None, None, D, bsz), lambda h: (l, h, 0, 0)),
            pl.BlockSpec((None, M_HEADS, bsz), lambda h: (l, 0, 0)),
        ],
        out_specs=[
            pl.BlockSpec((D, bsz), lambda h: (h, 0)),
            pl.BlockSpec((None, D, D, bsz), lambda h: (h, 0, 0, 0)),
            pl.BlockSpec((None, D, bsz), lambda h: (h, 0, 0)),
            pl.BlockSpec((M_HEADS, bsz), lambda h: (0, 0)),
        ],
        out_shape=[
            jax.ShapeDtypeStruct((M_WIDTH, bsz), F32),
            jax.ShapeDtypeStruct((M_HEADS, D, D, bsz), F32),
            jax.ShapeDtypeStruct((M_HEADS, D, bsz), F32),
            jax.ShapeDtypeStruct((M_HEADS, bsz), F32),
        ],
        scratch_shapes=[pltpu.VMEM((D, bsz), F32)],
        compiler_params=_params("arbitrary"),
        name="mlstm_sample",
    )(gb, zt, zt, zt, zt, zt, ngt, c_all, n_all, m_all)


def _conv_sample_kernel(z_ref, hist_ref, w_ref, cb_ref, lg_ref, lb_ref, h_ref, hist_out_ref):
    nh = C_KERNEL - 1
    u = z_ref[:, 0:C_WIDTH] * _sigmoid(z_ref[:, C_WIDTH:2 * C_WIDTH])
    acc = cb_ref[...] + w_ref[nh:nh + 1, :] * u
    for j in range(nh):
        acc = acc + w_ref[j:j + 1, :] * hist_ref[j]
    mu = jnp.mean(acc, axis=-1, keepdims=True)
    xc = acc - mu
    y = xc * lax.rsqrt(jnp.mean(xc * xc, axis=-1, keepdims=True) + EPS) * lg_ref[...] + lb_ref[...]
    h_ref[...] = y * _sigmoid(y)
    for j in range(nh - 1):
        hist_out_ref[j] = hist_ref[j + 1]
    hist_out_ref[nh - 1] = u


def _conv_sample(l, z, hist_all, w, cb, lg, lb):
    bsz = z.shape[0]
    nh = hist_all.shape[1]
    R = min(CONV_SAMPLE_BLOCK, bsz)
    return pl.pallas_call(
        _conv_sample_kernel,
        grid=(bsz // R,),
        in_specs=[
            pl.BlockSpec((R, 2 * C_WIDTH), lambda i: (i, Z_C // (2 * C_WIDTH))),
            pl.BlockSpec((None, nh, R, C_WIDTH), lambda i: (l, 0, i, 0)),
            _const_spec((CONV_PAD, C_WIDTH)),
            _const_spec((1, C_WIDTH)), _const_spec((1, C_WIDTH)), _const_spec((1, C_WIDTH)),
        ],
        out_specs=[pl.BlockSpec((R, C_WIDTH), lambda i: (i, 0)), pl.BlockSpec((nh, R, C_WIDTH), lambda i: (0, i, 0))],
        out_shape=[jax.ShapeDtypeStruct((bsz, C_WIDTH), F32), jax.ShapeDtypeStruct((nh, bsz, C_WIDTH), F32)],
        compiler_params=_params("arbitrary"),
        name="conv_sample",
    )(z, hist_all, w, cb, lg, lb)


def _swa_sample_kernel(q_ref, kvn_ref, kv2_ref, kc_ref, vc_ref, bias_ref, aux_ref, o_ref, ko_ref, vo_ref):
    R = q_ref.shape[0]
    W = kc_ref.shape[2]
    H = S_Q_HEADS
    shape = (R, H, 128)
    row = lax.broadcasted_iota(jnp.int32, shape, 1)
    lane_half = lax.broadcasted_iota(jnp.int32, shape, 2) // HEAD_DIM
    q_half = row % 2
    kv_head = row // (H // S_KV_HEADS)
    qs = jnp.zeros(shape, F32)
    for j in range(H // 2):
        qs = jnp.where(row // 2 == j, q_ref[:, :, 128 * j:128 * (j + 1)], qs)
    q8 = jnp.where(lane_half == kv_head, jnp.where(q_half == kv_head, qs, pltpu.roll(qs, HEAD_DIM, axis=2)), 0.0)
    k_new = kvn_ref[:, :, 0:S_KV_WIDTH]
    v_new = kvn_ref[:, :, S_KV_WIDTH:2 * S_KV_WIDTH]
    s = jnp.einsum("bqd,bdk->bqk", q8.astype(BF16), kc_ref[...].astype(BF16), preferred_element_type=F32) * QK_SCALE
    s = s + bias_ref[...][None]
    s_new = jnp.sum(q8 * k_new, axis=2, keepdims=True) * QK_SCALE + aux_ref[:, 0:1][None]
    sink = aux_ref[:, 1:2][None]
    m = jnp.maximum(jnp.maximum(jnp.max(s, axis=2, keepdims=True), s_new), sink)
    e = jnp.exp(s - m)
    e_new = jnp.exp(s_new - m)
    inv = 1.0 / (jnp.sum(e, axis=2, keepdims=True) + e_new + jnp.exp(sink - m))
    o8 = jnp.einsum("bqk,bdk->bqd", e.astype(BF16), vc_ref[...].astype(BF16), preferred_element_type=F32)
    o8 = (o8 + e_new * v_new) * inv
    o8 = jnp.where(lane_half == q_half, jnp.where(q_half == kv_head, o8, pltpu.roll(o8, HEAD_DIM, axis=2)), 0.0)
    for j in range(H // 2):
        o_ref[:, :, 128 * j:128 * (j + 1)] = jnp.sum(jnp.where(row // 2 == j, o8, 0.0), axis=1, keepdims=True)
    k_cols = kv2_ref[:, 0:S_KV_WIDTH].T
    v_cols = kv2_ref[:, S_KV_WIDTH:2 * S_KV_WIDTH].T
    last = lax.broadcasted_iota(jnp.int32, (S_KV_WIDTH, W), 1) == W - 1
    for r in range(R):
        ko_ref[r] = jnp.where(last, k_cols[:, r:r + 1], pltpu.roll(kc_ref[r], W - 1, axis=1))
        vo_ref[r] = jnp.where(last, v_cols[:, r:r + 1], pltpu.roll(vc_ref[r], W - 1, axis=1))


def _swa_sample(l, z, z3, kc_all, vc_all, bias, aux):
    bsz = z.shape[0]
    R = min(SWA_SAMPLE_BLOCK, bsz)
    W = kc_all.shape[3]
    cache_in = pl.BlockSpec((None, R, S_KV_WIDTH, W), lambda i: (l, i, 0, 0))
    cache_out = pl.BlockSpec((R, S_KV_WIDTH, W), lambda i: (i, 0, 0))
    return pl.pallas_call(
        _swa_sample_kernel,
        grid=(bsz // R,),
        in_specs=[
            pl.BlockSpec((R, 1, S_WIDTH), lambda i: (i, 0, Z_SQ // S_WIDTH)),
            pl.BlockSpec((R, 1, 2 * S_KV_WIDTH), lambda i: (i, 0, Z_SKV // (2 * S_KV_WIDTH))),
            pl.BlockSpec((R, 2 * S_KV_WIDTH), lambda i: (i, Z_SKV // (2 * S_KV_WIDTH))),
            cache_in, cache_in,
            _const_spec((S_Q_HEADS, 128)), _const_spec((S_Q_HEADS, 128)),
        ],
        out_specs=[pl.BlockSpec((R, 1, S_WIDTH), lambda i: (i, 0, 0)), cache_out, cache_out],
        out_shape=[
            jax.ShapeDtypeStruct((bsz, 1, S_WIDTH), F32),
            jax.ShapeDtypeStruct((bsz, S_KV_WIDTH, W), F32),
            jax.ShapeDtypeStruct((bsz, S_KV_WIDTH, W), F32),
        ],
        compiler_params=_params("arbitrary"),
        name="swa_sample",
    )(z3, z3, z, kc_all, vc_all, bias, aux)


def _mix_out_sample_kernel(x_ref, hmt_ref, hc_ref, hs_ref, wout_ref, g2_ref, wq_ref, x1_ref, q_ref):
    cat = jnp.concatenate([hmt_ref[...].T.astype(BF16), hc_ref[...].astype(BF16), hs_ref[...].astype(BF16)], axis=1)
    x1 = x_ref[...] + _dot(cat, wout_ref[...])
    x1_ref[...] = x1
    q_ref[...] = _dot(_rms(x1, g2_ref[...]).astype(BF16), wq_ref[...])


def _mix_out_sample(x, hm, hc, hs, wout, g2, wq):
    bsz, d = x.shape
    full = lambda a: pl.BlockSpec(a.shape, lambda i: (0,) * a.ndim)
    args = (x, hm, hc, hs, wout, g2, wq)
    return pl.pallas_call(
        _mix_out_sample_kernel,
        grid=(1,),
        in_specs=[full(a) for a in args],
        out_specs=[pl.BlockSpec((bsz, d), lambda i: (0, 0)), pl.BlockSpec((bsz, X_WIDTH), lambda i: (0, 0))],
        out_shape=[jax.ShapeDtypeStruct((bsz, d), F32), jax.ShapeDtypeStruct((bsz, X_WIDTH), F32)],
        compiler_params=_params("arbitrary"),
        name="mix_out_sample",
    )(*args)


def _xattn_sample_kernel(q_ref, k_ref, v_ref, o_ref):
    R = k_ref.shape[0]
    shape = (R, SUBLANES, X_WIDTH)
    row = lax.broadcasted_iota(jnp.int32, shape, 1)
    lane_head = lax.broadcasted_iota(jnp.int32, shape, 2) // HEAD_DIM
    own = row == lane_head
    q8 = jnp.where(own, jnp.broadcast_to(q_ref[...], shape), 0.0).astype(BF16)
    s = jnp.einsum("bqd,bdk->bqk", q8, k_ref[...].astype(BF16), preferred_element_type=F32) * QK_SCALE
    e = jnp.exp(s - jnp.max(s, axis=2, keepdims=True))
    p = (e / jnp.sum(e, axis=2, keepdims=True)).astype(BF16)
    o8 = jnp.einsum("bqk,bdk->bqd", p, v_ref[...].astype(BF16), preferred_element_type=F32)
    o_ref[...] = jnp.sum(jnp.where(own, o8, 0.0), axis=1, keepdims=True)


def _xattn_sample(l, q3, k_all, v_all):
    _, bsz, w, mem = k_all.shape
    R = min(XATTN_SAMPLE_BLOCK, bsz)
    kv = pl.BlockSpec((None, R, w, mem), lambda i: (l, i, 0, 0))
    qo = pl.BlockSpec((R, 1, w), lambda i: (i, 0, 0))
    return pl.pallas_call(
        _xattn_sample_kernel,
        grid=(bsz // R,),
        in_specs=[qo, kv, kv],
        out_specs=qo,
        out_shape=jax.ShapeDtypeStruct((bsz, 1, w), F32),
        compiler_params=_params("arbitrary"),
        name="xattn_sample",
    )(q3, k_all, v_all)


def _ffn_sample_kernel(x1_ref, ox_ref, wo_ref, g3_ref, wup_ref, cw_ref, cb_ref, wdn_ref, gf_ref, hist_ref,
                       o_ref, hist_out_ref, act, *, final):
    dff = wdn_ref.shape[0]
    x = x1_ref[...] + _dot(ox_ref[...].astype(BF16), wo_ref[...])
    h = _rms(x, g3_ref[...]).astype(BF16)
    for c in range(0, dff, FFN_CHUNK):
        a = _dot(h, wup_ref[:, c:c + FFN_CHUNK])
        g = _dot(h, wup_ref[:, dff + c:dff + c + FFN_CHUNK])
        h1 = hist_ref[:, dff + c:dff + c + FFN_CHUNK]
        gc = (cw_ref[0:1, c:c + FFN_CHUNK] * hist_ref[:, c:c + FFN_CHUNK]
              + cw_ref[1:2, c:c + FFN_CHUNK] * h1
              + cw_ref[2:3, c:c + FFN_CHUNK] * g + cb_ref[:, c:c + FFN_CHUNK])
        act[:, c:c + FFN_CHUNK] = (gc * _sigmoid(gc) * a).astype(BF16)
        hist_out_ref[:, c:c + FFN_CHUNK] = h1
        hist_out_ref[:, dff + c:dff + c + FFN_CHUNK] = g
    y = x + _dot(act[...], wdn_ref[...])
    o_ref[...] = _rms(y, gf_ref[...]) if final else y


def _ffn_sample(x1, ox, wo, g3, wup, cw, cb, wdn, gf, hist, *, final):
    bsz, d = x1.shape
    dff = wdn.shape[0]
    full = lambda a: pl.BlockSpec(a.shape, lambda i: (0,) * a.ndim)
    args = (x1, ox, wo, g3, wup, cw, cb, wdn, gf, hist)
    return pl.pallas_call(
        functools.partial(_ffn_sample_kernel, final=final),
        grid=(1,),
        in_specs=[full(a) for a in args],
        out_specs=[pl.BlockSpec((bsz, d), lambda i: (0, 0)), pl.BlockSpec(hist.shape, lambda i: (0, 0))],
        out_shape=[jax.ShapeDtypeStruct((bsz, d), F32), jax.ShapeDtypeStruct(hist.shape, F32)],
        scratch_shapes=[pltpu.VMEM((bsz, dff), BF16)],
        compiler_params=_params("arbitrary"),
        name="ffn_sample",
    )(*args)


def _t5_buckets(dist):
    n = np.maximum(dist, 0)
    max_exact = N_BUCKETS // 2
    nf = np.maximum(n, max_exact).astype(np.float32)
    large = max_exact + (np.log(nf / np.float32(max_exact)) / np.float32(math.log(MAX_DISTANCE / max_exact))
                         * np.float32(N_BUCKETS - max_exact)).astype(np.int32)
    return np.where(n < max_exact, n, np.minimum(large, N_BUCKETS - 1))


def _prompt_buckets():
    W = WINDOW
    dist = np.arange(W)[None, :] + W - np.arange(2 * W)[:, None]
    band = (dist >= 0) & (dist < W)
    return np.where(band, _t5_buckets(dist), -1).astype(np.int32)


def _swa_tables(rel_bias):
    W = WINDOW
    dist_c = W - np.arange(W)
    tab = jnp.transpose(rel_bias[_t5_buckets(dist_c)], (1, 0))
    cache_bias = jnp.where((dist_c < W)[None], tab, NEG_INF)
    return cache_bias, rel_bias[0]


def _pad_rows(a, rows):
    return jnp.concatenate([a, jnp.zeros((rows - a.shape[0],) + a.shape[1:], a.dtype)], axis=0)


def _row(a):
    return a.reshape(1, -1)


def kernel(x_prompt, x_sample, mem_prompt, state_mlstm_C, state_mlstm_n, state_mlstm_m, state_conv, cache_swa_k, cache_swa_v, cache_mem_k, cache_mem_v, state_ffn_conv, rel_bias, norm1_g, w_in, b_i, b_f, mlstm_norm_g, conv_w, conv_b, conv_ln_g, conv_ln_b, swa_sinks, w_out, norm2_g, w_xq, w_xk, w_xv, w_xo, norm3_g, w_up, ffn_conv_w, ffn_conv_b, w_down, final_norm_g):
    depth = w_in.shape[0]
    bp, seq, d = x_prompt.shape
    bs = x_sample.shape[0]
    mem = mem_prompt.shape[1]
    dff = w_down.shape[1]
    W = WINDOW
    nh = C_KERNEL - 1

    xp = x_prompt
    xs = x_sample.reshape(bs, d)
    gf = _row(final_norm_g)
    c_all = jnp.transpose(state_mlstm_C, (0, 2, 3, 4, 1))
    n_all = jnp.transpose(state_mlstm_n, (0, 2, 3, 1))
    m_all = jnp.transpose(state_mlstm_m, (0, 2, 1))
    hist_all = jnp.transpose(state_conv, (0, 2, 1, 3))
    kc_all = jnp.transpose(cache_swa_k, (0, 1, 3, 4, 2)).reshape(depth, bs, S_KV_WIDTH, W)
    vc_all = jnp.transpose(cache_swa_v, (0, 1, 3, 4, 2)).reshape(depth, bs, S_KV_WIDTH, W)
    mk_all = jnp.transpose(cache_mem_k, (0, 1, 3, 4, 2)).reshape(depth, bs, X_WIDTH, mem)
    mv_all = jnp.transpose(cache_mem_v, (0, 1, 3, 4, 2)).reshape(depth, bs, X_WIDTH, mem)
    pm_c, pm_n, pm_m, p_conv, p_k, p_v, p_mk, p_mv, p_ffn = ([] for _ in range(9))
    s_c, s_n, s_m, s_conv, s_k, s_v, s_ffn = ([] for _ in range(7))

    for l in range(depth):
        last = l == depth - 1
        wi = w_in[l]
        gate_cols = wi[:, 4 * M_WIDTH:4 * M_WIDTH + 2 * M_HEADS]
        win = jnp.concatenate(
            [wi[:, 0:4 * M_WIDTH], wi[:, 4 * M_WIDTH + 2 * M_HEADS:], gate_cols,
             jnp.zeros((d, 128 - 2 * M_HEADS), F32)], axis=1).astype(BF16)
        wout = w_out[l].astype(BF16)
        wxq = w_xq[l].astype(BF16)
        wxkv = jnp.concatenate([w_xk[l], w_xv[l]], axis=1).astype(BF16)
        wxo = w_xo[l].astype(BF16)
        wup = w_up[l].astype(BF16)
        wdn = w_down[l].astype(BF16)
        g1, g2, g3 = _row(norm1_g[l]), _row(norm2_g[l]), _row(norm3_g[l])
        gbias = jnp.concatenate([b_i[l], b_f[l], jnp.zeros((128 - 2 * M_HEADS,), F32)]).reshape(1, 128)
        ng = _row(mlstm_norm_g[l])
        cw = _pad_rows(conv_w[l], CONV_PAD)
        cb, lg, lb = _row(conv_b[l]), _row(conv_ln_g[l]), _row(conv_ln_b[l])
        fcw = _pad_rows(ffn_conv_w[l], SUBLANES)
        fcb = _row(ffn_conv_b[l])
        cache_bias, bias0 = _swa_tables(rel_bias)
        aux = jnp.concatenate([bias0[:, None], swa_sinks[l][:, None], jnp.zeros((S_Q_HEADS, 126), F32)], axis=1)

        mkv = _mem_kv(mem_prompt, wxkv)
        z =_norm_matmul(xp.reshape(bp * seq, d), g1, win, norm=True, tm=512).reshape(bp, seq, Z_WIDTH)
        hm, cn, mm = _mlstm_prompt(z, gbias, ng)
        hc, ctail = _conv_prompt(z, cw, cb, lg, lb, tc=512)
        hs, kt, vt = _swa_prompt(z, rel_bias, swa_sinks[l], jnp.asarray(_prompt_buckets()))
        xp = _mix_out_prompt(xp, hm, hc, hs, wout, g2, wxq, mkv, wxo, tm=512)
        xp, ftail = _ffn_prompt(xp, g3, wup, fcw, fcb, wdn, gf, tm=512, final=last)
        own = [cn[:, h, :, HEAD_DIM * (h % 2):HEAD_DIM * (h % 2 + 1)] for h in range(M_HEADS)]
        pm_c.append(jnp.stack([jnp.swapaxes(t[:, 0:HEAD_DIM, :], 1, 2) for t in own], axis=1))
        pm_n.append(jnp.stack([t[:, HEAD_DIM, :] for t in own], axis=1))
        pm_m.append(mm[:, :, 0, 0])
        p_conv.append(ctail[:, CONV_PAD - nh:, :])
        p_k.append(jnp.transpose(kt.reshape(bp, S_KV_HEADS, HEAD_DIM, W), (0, 3, 1, 2)))
        p_v.append(jnp.transpose(vt.reshape(bp, S_KV_HEADS, HEAD_DIM, W), (0, 3, 1, 2)))
        p_mk.append(jnp.transpose(mkv[:, 0:X_WIDTH, :].reshape(bp, X_HEADS, HEAD_DIM, mem), (0, 3, 1, 2)))
        p_mv.append(jnp.transpose(mkv[:, X_WIDTH:, :].reshape(bp, X_HEADS, HEAD_DIM, mem), (0, 3, 1, 2)))
        p_ffn.append(ftail[:, FFN_PAD - (FFN_KERNEL - 1):, :])

        zs, zst = _sample_in(xs, g1, win)
        gb8 = jnp.concatenate([b_i[l], b_f[l]])
        ngt = jnp.broadcast_to(mlstm_norm_g[l][:, None], (M_WIDTH, bs))
        hmt_s, c_new, n_new, m_new = _mlstm_sample(l, zst, gb8, ngt, c_all, n_all, m_all)
        hc_s, conv_new = _conv_sample(l, zs, hist_all, cw, cb, lg, lb)
        hs_s, k_new, v_new = _swa_sample(l, zs, zs.reshape(bs, 1, Z_WIDTH), kc_all, vc_all, cache_bias, aux)
        x1, qx = _mix_out_sample(xs, hmt_s, hc_s, hs_s.reshape(bs, S_WIDTH), wout, g2, wxq)
        ox = _xattn_sample(l, qx.reshape(bs, 1, X_WIDTH), mk_all, mv_all)
        xs, ffn_new = _ffn_sample(x1, ox.reshape(bs, X_WIDTH), wxo, g3, wup, fcw, fcb, wdn, gf,
                                  state_ffn_conv[l].reshape(bs, (FFN_KERNEL - 1) * dff), final=last)
        s_c.append(c_new)
        s_n.append(n_new)
        s_m.append(m_new)
        s_conv.append(conv_new)
        s_k.append(k_new.reshape(bs, S_KV_HEADS, HEAD_DIM, W))
        s_v.append(v_new.reshape(bs, S_KV_HEADS, HEAD_DIM, W))
        s_ffn.append(ffn_new.reshape(bs, FFN_KERNEL - 1, dff))

    st = jnp.stack
    tr = jnp.transpose
    return (xp, xs.reshape(bs, 1, d),
            st(pm_c), st(pm_n), st(pm_m), st(p_conv), st(p_k), st(p_v), st(p_mk), st(p_mv), st(p_ffn),
            tr(st(s_c), (0, 4, 1, 2, 3)), tr(st(s_n), (0, 3, 1, 2)), tr(st(s_m), (0, 2, 1)),
            tr(st(s_conv), (0, 2, 1, 3)), tr(st(s_k), (0, 1, 4, 2, 3)), tr(st(s_v), (0, 1, 4, 2, 3)),
            st(s_ffn))
```

```python
import functools
import math

import numpy as np
import jax
import jax.numpy as jnp
from jax import lax
from jax.experimental import pallas as pl
from jax.experimental.pallas import tpu as pltpu

F32 = jnp.float32
BF16 = jnp.bfloat16
EPS = 1e-6
NEG_INF = float("-inf")

HEAD_DIM = 64
M_HEADS = 4
M_WIDTH = M_HEADS * HEAD_DIM
C_WIDTH = 256
C_KERNEL = 31
S_Q_HEADS = 8
S_KV_HEADS = 2
S_WIDTH = S_Q_HEADS * HEAD_DIM
S_KV_WIDTH = S_KV_HEADS * HEAD_DIM
WINDOW = 128
N_BUCKETS = 32
MAX_DISTANCE = 128
X_HEADS = 4
X_WIDTH = X_HEADS * HEAD_DIM
FFN_KERNEL = 3
QK_SCALE = HEAD_DIM ** -0.5
LOG2E = math.log2(math.e)

Z_M = 0
Z_C = 4 * M_WIDTH
Z_SQ = Z_C + 2 * C_WIDTH
Z_SKV = Z_SQ + S_WIDTH
Z_G = Z_SKV + 2 * S_KV_WIDTH
Z_WIDTH = Z_G + 128

LANES = 128
SUBLANES = 8
VMEM_LIMIT = 56 * 1024 * 1024

M_CHUNK = 128
FFN_CHUNK = 256


def _params(*sem):
    return pltpu.CompilerParams(dimension_semantics=sem, vmem_limit_bytes=VMEM_LIMIT)


def _const_spec(shape):
    nd = len(shape)
    return pl.BlockSpec(shape, lambda *_: (0,) * nd, pipeline_mode=pl.Buffered(1))


def _weight_spec(w, l):
    nd = w.ndim - 1
    return pl.BlockSpec((None,) + w.shape[1:], lambda *_: (l,) + (0,) * nd, pipeline_mode=pl.Buffered(1))


def _rms(x, g):
    return x * lax.rsqrt(jnp.mean(x * x, axis=-1, keepdims=True) + EPS) * g


def _sigmoid(x):
    return 1.0 / (1.0 + jnp.exp(-x))


def _log_sigmoid(x):
    return jnp.minimum(x, 0.0) - jnp.log1p(jnp.exp(-jnp.abs(x)))


def _dot(a, b):
    return jnp.dot(a, b, preferred_element_type=F32)


def _dot_nt(a, b):
    return lax.dot_general(a, b, (((1,), (1,)), ((), ())), preferred_element_type=F32)


def _dot_tn(a, b):
    return lax.dot_general(a, b, (((0,), (0,)), ((), ())), preferred_element_type=F32)


NORM_SUB = 512


def _norm_matmul_kernel(x_ref, g_ref, w_ref, o_ref, *, norm, col_chunk):
    tm, n = o_ref.shape
    sub = min(NORM_SUB, tm)
    for r0 in range(0, tm, sub):
        x = x_ref[r0:r0 + sub, :]
        h = (_rms(x, g_ref[...]) if norm else x).astype(BF16)
        for c in range(0, n, col_chunk):
            w = min(col_chunk, n - c)
            o_ref[r0:r0 + sub, c:c + w] = _dot(h, w_ref[:, c:c + w])


def _norm_matmul(l, x, g, w, *, norm, tm):
    m, d = x.shape
    n = w.shape[2]
    tm = min(tm, m)
    return pl.pallas_call(
        functools.partial(_norm_matmul_kernel, norm=norm, col_chunk=512),
        grid=(m // tm,),
        in_specs=[pl.BlockSpec((tm, d), lambda i: (i, 0)), _const_spec((1, d)), _weight_spec(w, l)],
        out_specs=pl.BlockSpec((tm, n), lambda i: (i, 0)),
        out_shape=jax.ShapeDtypeStruct((m, n), F32),
        compiler_params=_params("arbitrary"),
        name="norm_matmul" if norm else "matmul",
    )(x, g, w)


def _mem_kv_kernel(x_ref, w_ref, o_ref, acc):
    acc[...] = _dot(x_ref[...].astype(BF16), w_ref[...])
    o_ref[...] = acc[...].T


def _mem_kv(l, mem3, w):
    b, mem, d = mem3.shape
    n = w.shape[2]
    return pl.pallas_call(
        _mem_kv_kernel,
        grid=(b,),
        in_specs=[pl.BlockSpec((None, mem, d), lambda i: (i, 0, 0)), _weight_spec(w, l)],
        out_specs=pl.BlockSpec((None, n, mem), lambda i: (i, 0, 0)),
        out_shape=jax.ShapeDtypeStruct((b, n, mem), F32),
        scratch_shapes=[pltpu.VMEM((mem, n), F32)],
        compiler_params=_params("arbitrary"),
        name="mem_kv",
    )(mem3, w)


def _sample_in_kernel(x_ref, g_ref, w_ref, z_ref, zt_ref):
    h = _rms(x_ref[...], g_ref[...]).astype(BF16)
    n = z_ref.shape[1]
    for c in range(0, n, 512):
        w = min(512, n - c)
        zc = _dot(h, w_ref[:, c:c + w])
        z_ref[:, c:c + w] = zc
        zt_ref[c:c + w, :] = zc.T


def _sample_in(l, x, g, w):
    m, d = x.shape
    n = w.shape[2]
    full = lambda shape: pl.BlockSpec(shape, lambda i: (0,) * len(shape))
    return pl.pallas_call(
        _sample_in_kernel,
        grid=(1,),
        in_specs=[full((m, d)), full((1, d)), _weight_spec(w, l)],
        out_specs=[full((m, n)), full((n, m))],
        out_shape=[jax.ShapeDtypeStruct((m, n), F32), jax.ShapeDtypeStruct((n, m), F32)],
        compiler_params=_params("arbitrary"),
        name="sample_in",
    )(x, g, w)


def _mlstm_prompt_kernel(z_ref, g_ref, gb_ref, ng_ref, h_ref, cn_ref, m_ref, cn_scr, m_scr, ht_scr):
    NB, L = z_ref.shape[0], z_ref.shape[1]
    hi = lax.Precision.HIGHEST

    @pl.when(pl.program_id(0) == 0)
    def _():
        cn_scr[...] = jnp.zeros(cn_scr.shape, F32)
        m_scr[...] = jnp.zeros(m_scr.shape, F32)

    src = lax.broadcasted_iota(jnp.int32, (L, L), 0)
    qry = lax.broadcasted_iota(jnp.int32, (L, L), 1)
    causal_t = src <= qry
    upper = jnp.where(causal_t, 1.0, 0.0)
    lower = jnp.where(src >= qry, 1.0, 0.0)
    lane_half = lax.broadcasted_iota(jnp.int32, (L, 128), 1) // HEAD_DIM
    row128 = lax.broadcasted_iota(jnp.int32, (128, 128), 0)

    for b in range(NB):
        gates = g_ref[b] + gb_ref[...]
        gates_t = gates.T[0:SUBLANES, :]
        b_rows = jnp.dot(_log_sigmoid(gates_t), upper, precision=hi, preferred_element_type=F32)
        b_cols = jnp.dot(lower, _log_sigmoid(gates), precision=hi, preferred_element_type=F32)
        for h in range(M_HEADS):
            par = h % 2
            slab = 128 * (h // 2)
            own = lane_half == par
            b_row = b_rows[M_HEADS + h:M_HEADS + h + 1, :]
            ci_col = gates[:, h:h + 1] - b_cols[:, M_HEADS + h:M_HEADS + h + 1]
            m_prev = m_scr[b, h, 0:1, 0:1]
            a_row = b_row + m_prev
            dm = jnp.where(causal_t, b_row + ci_col, NEG_INF)
            m_row = jnp.maximum(a_row, jnp.max(dm, axis=0, keepdims=True))
            w_inter = jnp.exp(a_row - m_row)
            q2 = z_ref[b, :, slab:slab + 128].astype(BF16)
            k2 = jnp.where(own, z_ref[b, :, M_WIDTH + slab:M_WIDTH + slab + 128] * QK_SCALE, 0.0)
            v2 = z_ref[b, :, 2 * M_WIDTH + slab:2 * M_WIDTH + slab + 128].astype(BF16)
            s_t = _dot_nt(k2.astype(BF16), q2) * jnp.exp(dm - m_row)
            cn = cn_scr[b, h]
            nd = w_inter * _dot_nt(cn.astype(BF16), q2)
            pv = _dot_tn(v2, s_t.astype(BF16))
            num = nd[0:HEAD_DIM, :] + pv[HEAD_DIM * par:HEAD_DIM * (par + 1), :]
            den = nd[HEAD_DIM:HEAD_DIM + 1, :] + jnp.sum(s_t, axis=0, keepdims=True)
            hh = num * (1.0 / jnp.maximum(jnp.abs(den), jnp.exp(-m_row)))
            hh = hh * lax.rsqrt(jnp.mean(hh * hh, axis=0, keepdims=True) + EPS)
            ht_scr[b, HEAD_DIM * h:HEAD_DIM * (h + 1), :] = hh
            b_last = b_row[:, L - 1:L]
            g_col = b_last + ci_col
            m_new = jnp.maximum(b_last + m_prev, jnp.max(g_col, axis=0, keepdims=True))
            decay = jnp.exp(b_last + m_prev - m_new)
            kw = k2 * jnp.exp(g_col - m_new)
            upd = _dot_tn(v2, kw.astype(BF16))
            c_new = decay * cn[0:HEAD_DIM, :] + upd[HEAD_DIM * par:HEAD_DIM * (par + 1), :]
            n_new = decay * cn[HEAD_DIM:HEAD_DIM + 1, :] + jnp.sum(kw, axis=0, keepdims=True)
            cn_scr[b, h, 0:HEAD_DIM, :] = c_new
            cn_scr[b, h, HEAD_DIM:128, :] = jnp.where(row128[HEAD_DIM:128, :] == HEAD_DIM, n_new, 0.0)
            m_scr[b, h] = jnp.broadcast_to(m_new, m_scr.shape[2:])
        o_gate = _sigmoid(z_ref[b, :, 3 * M_WIDTH:4 * M_WIDTH])
        h_ref[b] = ht_scr[b].T * ng_ref[...] * o_gate

    cn_ref[...] = cn_scr[...]
    m_ref[...] = m_scr[...]


def _mlstm_prompt(z3, gbias, ng):
    b, s, _ = z3.shape
    L = M_CHUNK
    return pl.pallas_call(
        _mlstm_prompt_kernel,
        grid=(s // L,),
        in_specs=[
            pl.BlockSpec((b, L, 4 * M_WIDTH), lambda c: (0, c, Z_M // (4 * M_WIDTH))),
            pl.BlockSpec((b, L, 128), lambda c: (0, c, Z_G // 128)),
            _const_spec((1, 128)),
            _const_spec((1, M_WIDTH)),
        ],
        out_specs=[
            pl.BlockSpec((b, L, M_WIDTH), lambda c: (0, c, 0)),
            pl.BlockSpec((b, M_HEADS, 128, 128), lambda c: (0, 0, 0, 0)),
            pl.BlockSpec((b, M_HEADS, SUBLANES, 128), lambda c: (0, 0, 0, 0)),
        ],
        out_shape=[
            jax.ShapeDtypeStruct((b, s, M_WIDTH), F32),
            jax.ShapeDtypeStruct((b, M_HEADS, 128, 128), F32),
            jax.ShapeDtypeStruct((b, M_HEADS, SUBLANES, 128), F32),
        ],
        scratch_shapes=[
            pltpu.VMEM((b, M_HEADS, 128, 128), F32),
            pltpu.VMEM((b, M_HEADS, SUBLANES, 128), F32),
            pltpu.VMEM((b, M_WIDTH, L), F32),
        ],
        compiler_params=_params("arbitrary"),
        name="mlstm_prompt",
    )(z3, z3, gbias, ng)


CONV_PAD = 32


def _conv_prompt_kernel(z_ref, w_ref, cb_ref, lg_ref, lb_ref, h_ref, tail_ref, buf, shifted):
    tc = z_ref.shape[0]
    t = pl.program_id(1)

    @pl.when(t == 0)
    def _():
        buf[0:CONV_PAD, :] = jnp.zeros((CONV_PAD, C_WIDTH), F32)

    @pl.when(t > 0)
    def _():
        buf[0:CONV_PAD, :] = buf[tc:tc + CONV_PAD, :]

    u = z_ref[:, 0:C_WIDTH] * _sigmoid(z_ref[:, C_WIDTH:2 * C_WIDTH])
    buf[CONV_PAD:CONV_PAD + tc, :] = u
    n_sh = shifted.shape[1]
    for r in range(1, SUBLANES):
        shifted[r - 1] = buf[r:r + n_sh, :]
    off = CONV_PAD - (C_KERNEL - 1)
    acc = jnp.broadcast_to(cb_ref[...], (tc, C_WIDTH))
    for j in range(C_KERNEL):
        r = (off + j) % SUBLANES
        base = off + j - r
        win = buf[base:base + tc, :] if r == 0 else shifted[r - 1, base:base + tc, :]
        acc = acc + w_ref[j:j + 1, :] * win
    mu = jnp.mean(acc, axis=-1, keepdims=True)
    xc = acc - mu
    y = xc * lax.rsqrt(jnp.mean(xc * xc, axis=-1, keepdims=True) + EPS) * lg_ref[...] + lb_ref[...]
    h_ref[...] = y * _sigmoid(y)
    tail_ref[...] = buf[tc:tc + CONV_PAD, :]


def _conv_prompt(z3, w, cb, lg, lb, *, tc):
    b, s, _ = z3.shape
    return pl.pallas_call(
        _conv_prompt_kernel,
        grid=(b, s // tc),
        in_specs=[
            pl.BlockSpec((None, tc, 2 * C_WIDTH), lambda i, t: (i, t, Z_C // (2 * C_WIDTH))),
            _const_spec((CONV_PAD, C_WIDTH)),
            _const_spec((1, C_WIDTH)), _const_spec((1, C_WIDTH)), _const_spec((1, C_WIDTH)),
        ],
        out_specs=[
            pl.BlockSpec((None, tc, C_WIDTH), lambda i, t: (i, t, 0)),
            pl.BlockSpec((None, CONV_PAD, C_WIDTH), lambda i, t: (i, 0, 0)),
        ],
        out_shape=[
            jax.ShapeDtypeStruct((b, s, C_WIDTH), F32),
            jax.ShapeDtypeStruct((b, CONV_PAD, C_WIDTH), F32),
        ],
        scratch_shapes=[pltpu.VMEM((CONV_PAD + tc, C_WIDTH), F32),
                        pltpu.VMEM((SUBLANES - 1, CONV_PAD + tc - SUBLANES, C_WIDTH), F32)],
        compiler_params=_params("arbitrary", "arbitrary"),
        name="conv_prompt",
    )(z3, w, cb, lg, lb)


def _swa_prompt_kernel(rb_ref, sink_ref, bucket_ref, q_ref, kv_ref, kvp_ref, o_ref, kt_ref, vt_ref,
                       bias_scr, s_scr, p_scr, ot_scr):
    W = WINDOW
    H = S_Q_HEADS
    G = H // S_KV_HEADS
    i = pl.program_id(0)
    n = pl.program_id(1)

    @pl.when(jnp.logical_and(i == 0, n == 0))
    def _():
        bucket = bucket_ref[...]
        prev_key = lax.broadcasted_iota(jnp.int32, (2 * W, W), 0) < W
        for h in range(H):
            acc = jnp.full((2 * W, W), NEG_INF, F32)
            for b in range(N_BUCKETS):
                acc = jnp.where(bucket == b, rb_ref[b, h], acc)
            acc = acc * LOG2E
            bias_scr[0, h] = acc
            bias_scr[1, h] = jnp.where(prev_key, NEG_INF, acc)

    QB = q_ref.shape[0] // W
    NK = (QB + 1) * W
    first = jnp.where(n == 0, 1, 0)
    kk = jnp.concatenate([kvp_ref[:, 0:S_KV_WIDTH], kv_ref[:, 0:S_KV_WIDTH]], axis=0) * (QK_SCALE * LOG2E)
    vv = jnp.concatenate([kvp_ref[:, S_KV_WIDTH:2 * S_KV_WIDTH], kv_ref[:, S_KV_WIDTH:2 * S_KV_WIDTH]], axis=0)
    kk_r = pltpu.roll(kk, HEAD_DIM, axis=1)
    lo_lane = lax.broadcasted_iota(jnp.int32, (NK, S_KV_WIDTH), 1) < HEAD_DIM
    k_var = [[jnp.where(lo_lane, kk, 0.0).astype(BF16), jnp.where(lo_lane, 0.0, kk_r).astype(BF16)],
             [jnp.where(lo_lane, kk_r, 0.0).astype(BF16), jnp.where(lo_lane, 0.0, kk).astype(BF16)]]
    v_t = vv.T.astype(BF16)

    for j in range(QB):
        k0 = j * W
        masked = first if j == 0 else 0
        m_rows = [None] * H
        for hk in range(S_KV_HEADS):
            c0 = 2 * 128 * hk
            q_st = jnp.concatenate([q_ref[k0:k0 + W, c0:c0 + 128], q_ref[k0:k0 + W, c0 + 128:c0 + 256]],
                                   axis=0).astype(BF16)
            for half in range(2):
                s_t = _dot_nt(k_var[hk][half][k0:k0 + 2 * W, :], q_st)
                for slab in range(2):
                    head = G * hk + 2 * slab + half
                    sb = s_t[:, 128 * slab:128 * (slab + 1)] + bias_scr[masked, head]
                    s_scr[j, head] = sb
                    m_rows[head] = jnp.maximum(jnp.max(sb, axis=0, keepdims=True), sink_ref[head] * LOG2E)

        inv = [None] * H
        for head in range(H):
            e = jnp.exp2(s_scr[j, head] - m_rows[head])
            den = jnp.sum(e, axis=0, keepdims=True) + jnp.exp2(sink_ref[head] * LOG2E - m_rows[head])
            inv[head] = 1.0 / den
            p_scr[j, :, 128 * head:128 * (head + 1)] = e.astype(BF16)

        for hk in range(S_KV_HEADS):
            o_t = _dot(v_t[HEAD_DIM * hk:HEAD_DIM * (hk + 1), k0:k0 + 2 * W],
                       p_scr[j, :, 128 * G * hk:128 * G * (hk + 1)])
            for g in range(G):
                head = G * hk + g
                ot_scr[j, HEAD_DIM * head:HEAD_DIM * (head + 1), :] = o_t[:, 128 * g:128 * (g + 1)] * inv[head]
        o_ref[k0:k0 + W, :] = ot_scr[j].T

    @pl.when(n == pl.num_programs(1) - 1)
    def _():
        kt_ref[...] = kv_ref[(QB - 1) * W:QB * W, 0:S_KV_WIDTH].T
        vt_ref[...] = kv_ref[(QB - 1) * W:QB * W, S_KV_WIDTH:2 * S_KV_WIDTH].T


def _swa_prompt(z3, rel_bias, sinks, bucket, *, qb):
    b, s, _ = z3.shape
    W = WINDOW
    smem = pl.BlockSpec(memory_space=pltpu.SMEM)
    return pl.pallas_call(
        _swa_prompt_kernel,
        grid=(b, s // (qb * W)),
        in_specs=[
            smem, smem, _const_spec((2 * W, W)),
            pl.BlockSpec((None, qb * W, S_WIDTH), lambda i, n: (i, n, Z_SQ // S_WIDTH)),
            pl.BlockSpec((None, qb * W, 2 * S_KV_WIDTH), lambda i, n: (i, n, Z_SKV // (2 * S_KV_WIDTH))),
            pl.BlockSpec((None, W, 2 * S_KV_WIDTH),
                         lambda i, n: (i, jnp.maximum(n * qb - 1, 0), Z_SKV // (2 * S_KV_WIDTH))),
        ],
        out_specs=[
            pl.BlockSpec((None, qb * W, S_WIDTH), lambda i, n: (i, n, 0)),
            pl.BlockSpec((None, S_KV_WIDTH, W), lambda i, n: (i, 0, 0)),
            pl.BlockSpec((None, S_KV_WIDTH, W), lambda i, n: (i, 0, 0)),
        ],
        out_shape=[
            jax.ShapeDtypeStruct((b, s, S_WIDTH), F32),
            jax.ShapeDtypeStruct((b, S_KV_WIDTH, W), F32),
            jax.ShapeDtypeStruct((b, S_KV_WIDTH, W), F32),
        ],
        scratch_shapes=[
            pltpu.VMEM((2, S_Q_HEADS, 2 * W, W), F32),
            pltpu.VMEM((qb, S_Q_HEADS, 2 * W, W), F32),
            pltpu.VMEM((qb, 2 * W, S_Q_HEADS * W), BF16),
            pltpu.VMEM((qb, S_WIDTH, W), F32),
        ],
        compiler_params=_params("arbitrary", "arbitrary"),
        name="swa_prompt",
    )(rel_bias, sinks, bucket, z3, z3, z3)


MIX_SUB = 1024


def _cross_attend_rows(x1, g2_ref, wq_ref, k_heads, v_cat, wo_ref):
    qx = _dot(_rms(x1, g2_ref[...]).astype(BF16), wq_ref[...]).astype(BF16)
    probs = []
    for h in range(X_HEADS):
        s = _dot(qx, k_heads[h]) * QK_SCALE
        e = jnp.exp(s - jnp.max(s, axis=1, keepdims=True))
        probs.append((e / jnp.sum(e, axis=1, keepdims=True)).astype(BF16))
    o = _dot_nt(jnp.concatenate(probs, axis=1), v_cat)
    return x1 + _dot(o.astype(BF16), wo_ref[...])


def _mix_out_prompt_kernel(x_ref, hm_ref, hc_ref, hs_ref, wout_ref, g2_ref, wq_ref, mk_ref, mv_ref, wo_ref, o_ref):
    tm = x_ref.shape[0]
    mk_t = mk_ref[...]
    mv_t = mv_ref[...]
    row_head = lax.broadcasted_iota(jnp.int32, mk_t.shape, 0) // HEAD_DIM
    k_heads = [jnp.where(row_head == h, mk_t, 0.0).astype(BF16) for h in range(X_HEADS)]
    v_cat = jnp.concatenate([jnp.where(row_head == h, mv_t, 0.0).astype(BF16) for h in range(X_HEADS)], axis=1)
    for r0 in range(0, tm, MIX_SUB):
        rows = slice(r0, r0 + MIX_SUB)
        cat = jnp.concatenate([hm_ref[rows, :].astype(BF16), hc_ref[rows, :].astype(BF16),
                               hs_ref[rows, :].astype(BF16)], axis=1)
        x1 = x_ref[rows, :] + _dot(cat, wout_ref[...])
        o_ref[rows, :] = _cross_attend_rows(x1, g2_ref, wq_ref, k_heads, v_cat, wo_ref)


def _mix_out_prompt(l, x3, hm, hc, hs, wout, g2, wq, mkv, wo, *, tm):
    b, s, d = x3.shape
    tm = min(tm, s)
    mem = mkv.shape[2]
    row = lambda w: pl.BlockSpec((None, tm, w), lambda i, t: (i, t, 0))
    return pl.pallas_call(
        _mix_out_prompt_kernel,
        grid=(b, s // tm),
        in_specs=[
            row(d), row(M_WIDTH), row(C_WIDTH), row(S_WIDTH),
            _weight_spec(wout, l), _const_spec((1, d)), _weight_spec(wq, l),
            pl.BlockSpec((None, X_WIDTH, mem), lambda i, t: (i, 0, 0)),
            pl.BlockSpec((None, X_WIDTH, mem), lambda i, t: (i, 1, 0)),
            _weight_spec(wo, l),
        ],
        out_specs=row(d),
        out_shape=jax.ShapeDtypeStruct((b, s, d), F32),
        compiler_params=_params("arbitrary", "arbitrary"),
        name="mix_out_prompt",
    )(x3, hm, hc, hs, wout, g2, wq, mkv, mkv, wo)


FFN_PAD = 8


FFN_SUB = 512


def _ffn_prompt_kernel(x_ref, g3_ref, wup_ref, cw_ref, cb_ref, wdn_ref, gf_ref, o_ref, tail_ref,
                       gbuf, carry, act, *, final):
    tm = x_ref.shape[0]
    dff = wdn_ref.shape[0]
    sub = min(FFN_SUB, tm)

    @pl.when(pl.program_id(1) == 0)
    def _():
        carry[...] = jnp.zeros(carry.shape, F32)

    for s in range(tm // sub):
        rows = slice(s * sub, (s + 1) * sub)
        x = x_ref[rows, :]
        h = _rms(x, g3_ref[...]).astype(BF16)
        for c in range(0, dff, FFN_CHUNK):
            a = _dot(h, wup_ref[:, c:c + FFN_CHUNK])
            g = _dot(h, wup_ref[:, dff + c:dff + c + FFN_CHUNK])
            gbuf[s, 0:FFN_PAD, :] = carry[:, c:c + FFN_CHUNK]
            gbuf[s, FFN_PAD:FFN_PAD + sub, :] = g
            carry[:, c:c + FFN_CHUNK] = g[sub - FFN_PAD:sub, :]
            gc = (cw_ref[0:1, c:c + FFN_CHUNK] * gbuf[s, FFN_PAD - 2:FFN_PAD - 2 + sub, :]
                  + cw_ref[1:2, c:c + FFN_CHUNK] * gbuf[s, FFN_PAD - 1:FFN_PAD - 1 + sub, :]
                  + cw_ref[2:3, c:c + FFN_CHUNK] * g + cb_ref[:, c:c + FFN_CHUNK])
            act[s, :, c:c + FFN_CHUNK] = (gc * _sigmoid(gc) * a).astype(BF16)
        y = x + _dot(act[s], wdn_ref[...])
        o_ref[rows, :] = _rms(y, gf_ref[...]) if final else y
    tail_ref[...] = carry[...]


def _ffn_prompt(l, x3, g3, wup, cw, cb, wdn, gf, *, tm, final):
    b, s, d = x3.shape
    tm = min(tm, s)
    dff = wdn.shape[1]
    sub = min(FFN_SUB, tm)
    return pl.pallas_call(
        functools.partial(_ffn_prompt_kernel, final=final),
        grid=(b, s // tm),
        in_specs=[
            pl.BlockSpec((None, tm, d), lambda i, t: (i, t, 0)),
            _const_spec((1, d)), _weight_spec(wup, l), _const_spec(cw.shape), _const_spec((1, dff)),
            _weight_spec(wdn, l), _const_spec((1, d)),
        ],
        out_specs=[
            pl.BlockSpec((None, tm, d), lambda i, t: (i, t, 0)),
            pl.BlockSpec((None, FFN_PAD, dff), lambda i, t: (i, 0, 0)),
        ],
        out_shape=[jax.ShapeDtypeStruct((b, s, d), F32), jax.ShapeDtypeStruct((b, FFN_PAD, dff), F32)],
        scratch_shapes=[
            pltpu.VMEM((tm // sub, FFN_PAD + sub, FFN_CHUNK), F32),
            pltpu.VMEM((FFN_PAD, dff), F32),
            pltpu.VMEM((tm // sub, sub, dff), BF16),
        ],
        compiler_params=_params("arbitrary", "arbitrary"),
        name="ffn_prompt",
    )(x3, g3, wup, cw, cb, wdn, gf)


SWA_SAMPLE_BLOCK = 16
XATTN_SAMPLE_BLOCK = 8
CONV_SAMPLE_BLOCK = 32


def _mlstm_sample_kernel(gb_ref, q_ref, k_ref, v_ref, o_ref, g_ref, ng_ref, c_ref, n_ref, m_ref,
                         h_ref, co_ref, no_ref, mo_ref, kw_scr):
    h = pl.program_id(0)
    i_pre = g_ref[pl.ds(h, 1), :] + gb_ref[h]
    f_pre = g_ref[pl.ds(M_HEADS + h, 1), :] + gb_ref[M_HEADS + h]
    a = _log_sigmoid(f_pre) + m_ref[pl.ds(h, 1), :]
    m_t = jnp.maximum(a, i_pre)
    w_old = jnp.exp(a - m_t)
    w_new = jnp.exp(i_pre - m_t)
    q = q_ref[...]
    k = k_ref[...] * QK_SCALE
    v = v_ref[...]
    n_old = n_ref[...]
    kw_scr[...] = k * w_new

    def body(d, acc):
        c_old = c_ref[d]
        co_ref[d] = w_old * c_old + kw_scr[pl.ds(d, 1), :] * v
        return acc + q_ref[pl.ds(d, 1), :] * c_old

    qc = lax.fori_loop(0, HEAD_DIM, body, jnp.zeros(v.shape, F32), unroll=8)
    s = jnp.sum(q * k, axis=0, keepdims=True) * w_new
    num = w_old * qc + s * v
    den = w_old * jnp.sum(q * n_old, axis=0, keepdims=True) + s
    hh = num / jnp.maximum(jnp.abs(den), jnp.exp(-m_t))
    hh = hh * lax.rsqrt(jnp.mean(hh * hh, axis=0, keepdims=True) + EPS) * ng_ref[...]
    h_ref[...] = hh * _sigmoid(o_ref[...])
    no_ref[...] = w_old * n_old + kw_scr[...]
    mo_ref[pl.ds(h, 1), :] = m_t


def _mlstm_sample(l, zt, gb, ngt, c_all, n_all, m_all):
    bsz = zt.shape[1]
    D = HEAD_DIM
    feat = lambda off: pl.BlockSpec((D, bsz), lambda h: (off // D + h, 0))
    return pl.pallas_call(
        _mlstm_sample_kernel,
        grid=(M_HEADS,),
        in_specs=[
            pl.BlockSpec(memory_space=pltpu.SMEM),
            feat(Z_M), feat(Z_M + M_WIDTH), feat(Z_M + 2 * M_WIDTH), feat(Z_M + 3 * M_WIDTH),
            pl.BlockSpec((SUBLANES, bsz), lambda h: (Z_G // SUBLANES, 0)),
            pl.BlockSpec((D, bsz), lambda h: (h, 0)),
            pl.BlockSpec((None, None, D, D, bsz), lambda h: (l, h, 0, 0, 0)),
            pl.BlockSpec((None, None, D, bsz), lambda h: (l, h, 0, 0)),
            pl.BlockSpec((None, M_HEADS, bsz), lambda h: (l, 0, 0)),
        ],
        out_specs=[
            pl.BlockSpec((D, bsz), lambda h: (h, 0)),
            pl.BlockSpec((None, D, D, bsz), lambda h: (h, 0, 0, 0)),
            pl.BlockSpec((None, D, bsz), lambda h: (h, 0, 0)),
            pl.BlockSpec((M_HEADS, bsz), lambda h: (0, 0)),
        ],
        out_shape=[
            jax.ShapeDtypeStruct((M_WIDTH, bsz), F32),
            jax.ShapeDtypeStruct((M_HEADS, D, D, bsz), F32),
            jax.ShapeDtypeStruct((M_HEADS, D, bsz), F32),
            jax.ShapeDtypeStruct((M_HEADS, bsz), F32),
        ],
        scratch_shapes=[pltpu.VMEM((D, bsz), F32)],
        compiler_params=_params("arbitrary"),
        name="mlstm_sample",
    )(gb, zt, zt, zt, zt, zt, ngt, c_all, n_all, m_all)


def _conv_sample_kernel(z_ref, hist_ref, w_ref, cb_ref, lg_ref, lb_ref, h_ref, hist_out_ref):
    nh = C_KERNEL - 1
    u = z_ref[:, 0:C_WIDTH] * _sigmoid(z_ref[:, C_WIDTH:2 * C_WIDTH])
    acc = cb_ref[...] + w_ref[nh:nh + 1, :] * u
    for j in range(nh):
        acc = acc + w_ref[j:j + 1, :] * hist_ref[j]
    mu = jnp.mean(acc, axis=-1, keepdims=True)
    xc = acc - mu
    y = xc * lax.rsqrt(jnp.mean(xc * xc, axis=-1, keepdims=True) + EPS) * lg_ref[...] + lb_ref[...]
    h_ref[...] = y * _sigmoid(y)
    for j in range(nh - 1):
        hist_out_ref[j] = hist_ref[j + 1]
    hist_out_ref[nh - 1] = u


def _conv_sample(l, z, hist_all, w, cb, lg, lb):
    bsz = z.shape[0]
    nh = hist_all.shape[1]
    R = min(CONV_SAMPLE_BLOCK, bsz)
    return pl.pallas_call(
        _conv_sample_kernel,
        grid=(bsz // R,),
        in_specs=[
            pl.BlockSpec((R, 2 * C_WIDTH), lambda i: (i, Z_C // (2 * C_WIDTH))),
            pl.BlockSpec((None, nh, R, C_WIDTH), lambda i: (l, 0, i, 0)),
            _const_spec((CONV_PAD, C_WIDTH)),
            _const_spec((1, C_WIDTH)), _const_spec((1, C_WIDTH)), _const_spec((1, C_WIDTH)),
        ],
        out_specs=[pl.BlockSpec((R, C_WIDTH), lambda i: (i, 0)), pl.BlockSpec((nh, R, C_WIDTH), lambda i: (0, i, 0))],
        out_shape=[jax.ShapeDtypeStruct((bsz, C_WIDTH), F32), jax.ShapeDtypeStruct((nh, bsz, C_WIDTH), F32)],
        compiler_params=_params("arbitrary"),
        name="conv_sample",
    )(z, hist_all, w, cb, lg, lb)


def _swa_sample_kernel(q_ref, kvn_ref, kv2_ref, kc_ref, vc_ref, bias_ref, aux_ref, o_ref, ko_ref, vo_ref):
    R = q_ref.shape[0]
    W = kc_ref.shape[2]
    H = S_Q_HEADS
    shape = (R, H, 128)
    row = lax.broadcasted_iota(jnp.int32, shape, 1)
    lane_half = lax.broadcasted_iota(jnp.int32, shape, 2) // HEAD_DIM
    q_half = row % 2
    kv_head = row // (H // S_KV_HEADS)
    qs = jnp.zeros(shape, F32)
    for j in range(H // 2):
        qs = jnp.where(row // 2 == j, q_ref[:, :, 128 * j:128 * (j + 1)], qs)
    q8 = jnp.where(lane_half == kv_head, jnp.where(q_half == kv_head, qs, pltpu.roll(qs, HEAD_DIM, axis=2)), 0.0)
    k_new = kvn_ref[:, :, 0:S_KV_WIDTH]
    v_new = kvn_ref[:, :, S_KV_WIDTH:2 * S_KV_WIDTH]
    s = jnp.einsum("bqd,bdk->bqk", q8.astype(BF16), kc_ref[...].astype(BF16), preferred_element_type=F32) * QK_SCALE
    s = s + bias_ref[...][None]
    s_new = jnp.sum(q8 * k_new, axis=2, keepdims=True) * QK_SCALE + aux_ref[:, 0:1][None]
    sink = aux_ref[:, 1:2][None]
    m = jnp.maximum(jnp.maximum(jnp.max(s, axis=2, keepdims=True), s_new), sink)
    e = jnp.exp(s - m)
    e_new = jnp.exp(s_new - m)
    inv = 1.0 / (jnp.sum(e, axis=2, keepdims=True) + e_new + jnp.exp(sink - m))
    o8 = jnp.einsum("bqk,bdk->bqd", e.astype(BF16), vc_ref[...].astype(BF16), preferred_element_type=F32)
    o8 = (o8 + e_new * v_new) * inv
    o8 = jnp.where(lane_half == q_half, jnp.where(q_half == kv_head, o8, pltpu.roll(o8, HEAD_DIM, axis=2)), 0.0)
    for j in range(H // 2):
        o_ref[:, :, 128 * j:128 * (j + 1)] = jnp.sum(jnp.where(row // 2 == j, o8, 0.0), axis=1, keepdims=True)
    k_cols = kv2_ref[:, 0:S_KV_WIDTH].T
    v_cols = kv2_ref[:, S_KV_WIDTH:2 * S_KV_WIDTH].T
    last = lax.broadcasted_iota(jnp.int32, (S_KV_WIDTH, W), 1) == W - 1
    for r in range(R):
        ko_ref[r] = jnp.where(last, k_cols[:, r:r + 1], pltpu.roll(kc_ref[r], W - 1, axis=1))
        vo_ref[r] = jnp.where(last, v_cols[:, r:r + 1], pltpu.roll(vc_ref[r], W - 1, axis=1))


def _swa_sample(l, z, z3, kc_all, vc_all, bias, aux):
    bsz = z.shape[0]
    R = min(SWA_SAMPLE_BLOCK, bsz)
    W = kc_all.shape[3]
    cache_in = pl.BlockSpec((None, R, S_KV_WIDTH, W), lambda i: (l, i, 0, 0))
    cache_out = pl.BlockSpec((R, S_KV_WIDTH, W), lambda i: (i, 0, 0))
    return pl.pallas_call(
        _swa_sample_kernel,
        grid=(bsz // R,),
        in_specs=[
            pl.BlockSpec((R, 1, S_WIDTH), lambda i: (i, 0, Z_SQ // S_WIDTH)),
            pl.BlockSpec((R, 1, 2 * S_KV_WIDTH), lambda i: (i, 0, Z_SKV // (2 * S_KV_WIDTH))),
            pl.BlockSpec((R, 2 * S_KV_WIDTH), lambda i: (i, Z_SKV // (2 * S_KV_WIDTH))),
            cache_in, cache_in,
            _const_spec((S_Q_HEADS, 128)), _const_spec((S_Q_HEADS, 128)),
        ],
        out_specs=[pl.BlockSpec((R, 1, S_WIDTH), lambda i: (i, 0, 0)), cache_out, cache_out],
        out_shape=[
            jax.ShapeDtypeStruct((bsz, 1, S_WIDTH), F32),
            jax.ShapeDtypeStruct((bsz, S_KV_WIDTH, W), F32),
            jax.ShapeDtypeStruct((bsz, S_KV_WIDTH, W), F32),
        ],
        compiler_params=_params("arbitrary"),
        name="swa_sample",
    )(z3, z3, z, kc_all, vc_all, bias, aux)


def _mix_out_sample_kernel(x_ref, hmt_ref, hc_ref, hs_ref, wout_ref, g2_ref, wq_ref, x1_ref, q_ref):
    cat = jnp.concatenate([hmt_ref[...].T.astype(BF16), hc_ref[...].astype(BF16), hs_ref[...].astype(BF16)], axis=1)
    x1 = x_ref[...] + _dot(cat, wout_ref[...])
    x1_ref[...] = x1
    q_ref[...] = _dot(_rms(x1, g2_ref[...]).astype(BF16), wq_ref[...])


def _mix_out_sample(l, x, hm, hc, hs, wout, g2, wq):
    bsz, d = x.shape
    full = lambda a: pl.BlockSpec(a.shape, lambda i: (0,) * a.ndim)
    args = (x, hm, hc, hs, wout, g2, wq)
    return pl.pallas_call(
        _mix_out_sample_kernel,
        grid=(1,),
        in_specs=[full(x), full(hm), full(hc), full(hs), _weight_spec(wout, l), full(g2), _weight_spec(wq, l)],
        out_specs=[pl.BlockSpec((bsz, d), lambda i: (0, 0)), pl.BlockSpec((bsz, X_WIDTH), lambda i: (0, 0))],
        out_shape=[jax.ShapeDtypeStruct((bsz, d), F32), jax.ShapeDtypeStruct((bsz, X_WIDTH), F32)],
        compiler_params=_params("arbitrary"),
        name="mix_out_sample",
    )(*args)


def _xattn_sample_kernel(q_ref, k_ref, v_ref, o_ref):
    R = k_ref.shape[0]
    shape = (R, SUBLANES, X_WIDTH)
    row = lax.broadcasted_iota(jnp.int32, shape, 1)
    lane_head = lax.broadcasted_iota(jnp.int32, shape, 2) // HEAD_DIM
    own = row == lane_head
    q8 = jnp.where(own, jnp.broadcast_to(q_ref[...], shape), 0.0).astype(BF16)
    s = jnp.einsum("bqd,bdk->bqk", q8, k_ref[...].astype(BF16), preferred_element_type=F32) * QK_SCALE
    e = jnp.exp(s - jnp.max(s, axis=2, keepdims=True))
    p = (e / jnp.sum(e, axis=2, keepdims=True)).astype(BF16)
    o8 = jnp.einsum("bqk,bdk->bqd", p, v_ref[...].astype(BF16), preferred_element_type=F32)
    o_ref[...] = jnp.sum(jnp.where(own, o8, 0.0), axis=1, keepdims=True)


def _xattn_sample(l, q3, k_all, v_all):
    _, bsz, w, mem = k_all.shape
    R = min(XATTN_SAMPLE_BLOCK, bsz)
    kv = pl.BlockSpec((None, R, w, mem), lambda i: (l, i, 0, 0))
    qo = pl.BlockSpec((R, 1, w), lambda i: (i, 0, 0))
    return pl.pallas_call(
        _xattn_sample_kernel,
        grid=(bsz // R,),
        in_specs=[qo, kv, kv],
        out_specs=qo,
        out_shape=jax.ShapeDtypeStruct((bsz, 1, w), F32),
        compiler_params=_params("arbitrary"),
        name="xattn_sample",
    )(q3, k_all, v_all)


def _ffn_sample_kernel(x1_ref, ox_ref, wo_ref, g3_ref, wup_ref, cw_ref, cb_ref, wdn_ref, gf_ref, hist_ref,
                       o_ref, hist_out_ref, act, *, final):
    dff = wdn_ref.shape[0]
    x = x1_ref[...] + _dot(ox_ref[...].astype(BF16), wo_ref[...])
    h = _rms(x, g3_ref[...]).astype(BF16)
    for c in range(0, dff, FFN_CHUNK):
        a = _dot(h, wup_ref[:, c:c + FFN_CHUNK])
        g = _dot(h, wup_ref[:, dff + c:dff + c + FFN_CHUNK])
        h1 = hist_ref[:, dff + c:dff + c + FFN_CHUNK]
        gc = (cw_ref[0:1, c:c + FFN_CHUNK] * hist_ref[:, c:c + FFN_CHUNK]
              + cw_ref[1:2, c:c + FFN_CHUNK] * h1
              + cw_ref[2:3, c:c + FFN_CHUNK] * g + cb_ref[:, c:c + FFN_CHUNK])
        act[:, c:c + FFN_CHUNK] = (gc * _sigmoid(gc) * a).astype(BF16)
        hist_out_ref[:, c:c + FFN_CHUNK] = h1
        hist_out_ref[:, dff + c:dff + c + FFN_CHUNK] = g
    y = x + _dot(act[...], wdn_ref[...])
    o_ref[...] = _rms(y, gf_ref[...]) if final else y


def _ffn_sample(l, x1, ox, wo, g3, wup, cw, cb, wdn, gf, hist, *, final):
    bsz, d = x1.shape
    dff = wdn.shape[1]
    full = lambda a: pl.BlockSpec(a.shape, lambda i: (0,) * a.ndim)
    args = (x1, ox, wo, g3, wup, cw, cb, wdn, gf, hist)
    return pl.pallas_call(
        functools.partial(_ffn_sample_kernel, final=final),
        grid=(1,),
        in_specs=[full(x1), full(ox), _weight_spec(wo, l), full(g3), _weight_spec(wup, l), full(cw), full(cb),
                  _weight_spec(wdn, l), full(gf), full(hist)],
        out_specs=[pl.BlockSpec((bsz, d), lambda i: (0, 0)), pl.BlockSpec(hist.shape, lambda i: (0, 0))],
        out_shape=[jax.ShapeDtypeStruct((bsz, d), F32), jax.ShapeDtypeStruct(hist.shape, F32)],
        scratch_shapes=[pltpu.VMEM((bsz, dff), BF16)],
        compiler_params=_params("arbitrary"),
        name="ffn_sample",
    )(*args)


def _t5_buckets(dist):
    n = np.maximum(dist, 0)
    max_exact = N_BUCKETS // 2
    nf = np.maximum(n, max_exact).astype(np.float32)
    large = max_exact + (np.log(nf / np.float32(max_exact)) / np.float32(math.log(MAX_DISTANCE / max_exact))
                         * np.float32(N_BUCKETS - max_exact)).astype(np.int32)
    return np.where(n < max_exact, n, np.minimum(large, N_BUCKETS - 1))


def _prompt_buckets():
    W = WINDOW
    dist = np.arange(W)[None, :] + W - np.arange(2 * W)[:, None]
    band = (dist >= 0) & (dist < W)
    return np.where(band, _t5_buckets(dist), -1).astype(np.int32)


def _swa_tables(rel_bias):
    W = WINDOW
    dist_c = W - np.arange(W)
    tab = jnp.transpose(rel_bias[_t5_buckets(dist_c)], (1, 0))
    cache_bias = jnp.where((dist_c < W)[None], tab, NEG_INF)
    return cache_bias, rel_bias[0]


def _pad_rows(a, rows):
    return jnp.concatenate([a, jnp.zeros((rows - a.shape[0],) + a.shape[1:], a.dtype)], axis=0)


def _row(a):
    return a.reshape(1, -1)


def kernel(x_prompt, x_sample, mem_prompt, state_mlstm_C, state_mlstm_n, state_mlstm_m, state_conv, cache_swa_k, cache_swa_v, cache_mem_k, cache_mem_v, state_ffn_conv, rel_bias, norm1_g, w_in, b_i, b_f, mlstm_norm_g, conv_w, conv_b, conv_ln_g, conv_ln_b, swa_sinks, w_out, norm2_g, w_xq, w_xk, w_xv, w_xo, norm3_g, w_up, ffn_conv_w, ffn_conv_b, w_down, final_norm_g):
    depth = w_in.shape[0]
    bp, seq, d = x_prompt.shape
    bs = x_sample.shape[0]
    mem = mem_prompt.shape[1]
    dff = w_down.shape[1]
    W = WINDOW
    nh = C_KERNEL - 1

    xp = x_prompt
    xs = x_sample.reshape(bs, d)
    gf = _row(final_norm_g)
    c_all = jnp.transpose(state_mlstm_C, (0, 2, 3, 4, 1))
    n_all = jnp.transpose(state_mlstm_n, (0, 2, 3, 1))
    m_all = jnp.transpose(state_mlstm_m, (0, 2, 1))
    hist_all = jnp.transpose(state_conv, (0, 2, 1, 3))
    kc_all = jnp.transpose(cache_swa_k, (0, 1, 3, 4, 2)).reshape(depth, bs, S_KV_WIDTH, W)
    vc_all = jnp.transpose(cache_swa_v, (0, 1, 3, 4, 2)).reshape(depth, bs, S_KV_WIDTH, W)
    mk_all = jnp.transpose(cache_mem_k, (0, 1, 3, 4, 2)).reshape(depth, bs, X_WIDTH, mem)
    mv_all = jnp.transpose(cache_mem_v, (0, 1, 3, 4, 2)).reshape(depth, bs, X_WIDTH, mem)
    pm_c, pm_n, pm_m, p_conv, p_k, p_v, p_mk, p_mv, p_ffn = ([] for _ in range(9))
    s_c, s_n, s_m, s_conv, s_k, s_v, s_ffn = ([] for _ in range(7))

    gate_cols = w_in[:, :, 4 * M_WIDTH:4 * M_WIDTH + 2 * M_HEADS]
    win = jnp.concatenate(
        [w_in[:, :, 0:4 * M_WIDTH], w_in[:, :, 4 * M_WIDTH + 2 * M_HEADS:], gate_cols,
         jnp.zeros((depth, d, 128 - 2 * M_HEADS), F32)], axis=2).astype(BF16)
    wout = w_out.astype(BF16)
    wxq = w_xq.astype(BF16)
    wxkv = jnp.concatenate([w_xk, w_xv], axis=2).astype(BF16)
    wxo = w_xo.astype(BF16)
    wup = w_up.astype(BF16)
    wdn = w_down.astype(BF16)
    bucket = jnp.asarray(_prompt_buckets())

    for l in range(depth):
        last = l == depth - 1
        g1, g2, g3 = _row(norm1_g[l]), _row(norm2_g[l]), _row(norm3_g[l])
        gbias = jnp.concatenate([b_i[l], b_f[l], jnp.zeros((128 - 2 * M_HEADS,), F32)]).reshape(1, 128)
        ng = _row(mlstm_norm_g[l])
        cw = _pad_rows(conv_w[l], CONV_PAD)
        cb, lg, lb = _row(conv_b[l]), _row(conv_ln_g[l]), _row(conv_ln_b[l])
        fcw = _pad_rows(ffn_conv_w[l], SUBLANES)
        fcb = _row(ffn_conv_b[l])
        cache_bias, bias0 = _swa_tables(rel_bias)
        aux = jnp.concatenate([bias0[:, None], swa_sinks[l][:, None], jnp.zeros((S_Q_HEADS, 126), F32)], axis=1)

        mkv = _mem_kv(l, mem_prompt, wxkv)
        z = _norm_matmul(l, xp.reshape(bp * seq, d), g1, win, norm=True, tm=1024).reshape(bp, seq, Z_WIDTH)
        hm, cn, mm = _mlstm_prompt(z, gbias, ng)
        hc, ctail = _conv_prompt(z, cw, cb, lg, lb, tc=512)
        hs, kt, vt = _swa_prompt(z, rel_bias, swa_sinks[l], bucket, qb=2)
        xp = _mix_out_prompt(l, xp, hm, hc, hs, wout, g2, wxq, mkv, wxo, tm=1024)
        xp, ftail = _ffn_prompt(l, xp, g3, wup, fcw, fcb, wdn, gf, tm=1024, final=last)
        own = [cn[:, h, :, HEAD_DIM * (h % 2):HEAD_DIM * (h % 2 + 1)] for h in range(M_HEADS)]
        pm_c.append(jnp.stack([jnp.swapaxes(t[:, 0:HEAD_DIM, :], 1, 2) for t in own], axis=1))
        pm_n.append(jnp.stack([t[:, HEAD_DIM, :] for t in own], axis=1))
        pm_m.append(mm[:, :, 0, 0])
        p_conv.append(ctail[:, CONV_PAD - nh:, :])
        p_k.append(jnp.transpose(kt.reshape(bp, S_KV_HEADS, HEAD_DIM, W), (0, 3, 1, 2)))
        p_v.append(jnp.transpose(vt.reshape(bp, S_KV_HEADS, HEAD_DIM, W), (0, 3, 1, 2)))
        p_mk.append(jnp.transpose(mkv[:, 0:X_WIDTH, :].reshape(bp, X_HEADS, HEAD_DIM, mem), (0, 3, 1, 2)))
        p_mv.append(jnp.transpose(mkv[:, X_WIDTH:, :].reshape(bp, X_HEADS, HEAD_DIM, mem), (0, 3, 1, 2)))
        p_ffn.append(ftail[:, FFN_PAD - (FFN_KERNEL - 1):, :])

        zs, zst = _sample_in(l, xs, g1, win)
        gb8 = jnp.concatenate([b_i[l], b_f[l]])
        ngt = jnp.broadcast_to(mlstm_norm_g[l][:, None], (M_WIDTH, bs))
        hmt_s, c_new, n_new, m_new = _mlstm_sample(l, zst, gb8, ngt, c_all, n_all, m_all)
        hc_s, conv_new = _conv_sample(l, zs, hist_all, cw, cb, lg, lb)
        hs_s, k_new, v_new = _swa_sample(l, zs, zs.reshape(bs, 1, Z_WIDTH), kc_all, vc_all, cache_bias, aux)
        x1, qx = _mix_out_sample(l, xs, hmt_s, hc_s, hs_s.reshape(bs, S_WIDTH), wout, g2, wxq)
        ox = _xattn_sample(l, qx.reshape(bs, 1, X_WIDTH), mk_all, mv_all)
        xs, ffn_new = _ffn_sample(l, x1, ox.reshape(bs, X_WIDTH), wxo, g3, wup, fcw, fcb, wdn, gf,
                                  state_ffn_conv[l].reshape(bs, (FFN_KERNEL - 1) * dff), final=last)
        s_c.append(c_new)
        s_n.append(n_new)
        s_m.append(m_new)
        s_conv.append(conv_new)
        s_k.append(k_new.reshape(bs, S_KV_HEADS, HEAD_DIM, W))
        s_v.append(v_new.reshape(bs, S_KV_HEADS, HEAD_DIM, W))
        s_ffn.append(ffn_new.reshape(bs, FFN_KERNEL - 1, dff))

    st = jnp.stack
    tr = jnp.transpose
    return (xp, xs.reshape(bs, 1, d),
            st(pm_c), st(pm_n), st(pm_m), st(p_conv), st(p_k), st(p_v), st(p_mk), st(p_mv), st(p_ffn),
            tr(st(s_c), (0, 4, 1, 2, 3)), tr(st(s_n), (0, 3, 1, 2)), tr(st(s_m), (0, 2, 1)),
            tr(st(s_conv), (0, 2, 1, 3)), tr(st(s_k), (0, 1, 4, 2, 3)), tr(st(s_v), (0, 1, 4, 2, 3)),
            st(s_ffn))
```

```python
import functools
import math

import numpy as np
import jax
import jax.numpy as jnp
from jax import lax
from jax.experimental import pallas as pl
from jax.experimental.pallas import tpu as pltpu

F32 = jnp.float32
BF16 = jnp.bfloat16
EPS = 1e-6
NEG_INF = float("-inf")

HEAD_DIM = 64
M_HEADS = 4
M_WIDTH = M_HEADS * HEAD_DIM
C_WIDTH = 256
C_KERNEL = 31
S_Q_HEADS = 8
S_KV_HEADS = 2
S_WIDTH = S_Q_HEADS * HEAD_DIM
S_KV_WIDTH = S_KV_HEADS * HEAD_DIM
WINDOW = 128
N_BUCKETS = 32
MAX_DISTANCE = 128
X_HEADS = 4
X_WIDTH = X_HEADS * HEAD_DIM
FFN_KERNEL = 3
QK_SCALE = HEAD_DIM ** -0.5
LOG2E = math.log2(math.e)

Z_M = 0
Z_C = 4 * M_WIDTH
Z_SQ = Z_C + 2 * C_WIDTH
Z_SKV = Z_SQ + S_WIDTH
Z_G = Z_SKV + 2 * S_KV_WIDTH
Z_WIDTH = Z_G + 128

LANES = 128
SUBLANES = 8
VMEM_LIMIT = 56 * 1024 * 1024

M_CHUNK = 128
FFN_CHUNK = 256


def _params(*sem):
    return pltpu.CompilerParams(dimension_semantics=sem, vmem_limit_bytes=VMEM_LIMIT)


def _const_spec(shape):
    nd = len(shape)
    return pl.BlockSpec(shape, lambda *_: (0,) * nd, pipeline_mode=pl.Buffered(1))


def _weight_spec(w, l):
    nd = w.ndim - 1
    return pl.BlockSpec((None,) + w.shape[1:], lambda *_: (l,) + (0,) * nd, pipeline_mode=pl.Buffered(1))


def _rms(x, g):
    return x * lax.rsqrt(jnp.mean(x * x, axis=-1, keepdims=True) + EPS) * g


def _sigmoid(x):
    return 1.0 / (1.0 + jnp.exp(-x))


def _log_sigmoid(x):
    return jnp.minimum(x, 0.0) - jnp.log1p(jnp.exp(-jnp.abs(x)))


def _dot(a, b):
    return jnp.dot(a, b, preferred_element_type=F32)


def _dot_nt(a, b):
    return lax.dot_general(a, b, (((1,), (1,)), ((), ())), preferred_element_type=F32)


def _dot_tn(a, b):
    return lax.dot_general(a, b, (((0,), (0,)), ((), ())), preferred_element_type=F32)


NORM_SUB = 512


def _norm_matmul_kernel(x_ref, g_ref, w_ref, o_ref, *, norm, col_chunk):
    tm, n = o_ref.shape
    sub = min(NORM_SUB, tm)
    for r0 in range(0, tm, sub):
        x = x_ref[r0:r0 + sub, :]
        h = (_rms(x, g_ref[...]) if norm else x).astype(BF16)
        for c in range(0, n, col_chunk):
            w = min(col_chunk, n - c)
            o_ref[r0:r0 + sub, c:c + w] = _dot(h, w_ref[:, c:c + w])


def _norm_matmul(l, x, g, w, *, norm, tm):
    m, d = x.shape
    n = w.shape[2]
    tm = min(tm, m)
    return pl.pallas_call(
        functools.partial(_norm_matmul_kernel, norm=norm, col_chunk=512),
        grid=(m // tm,),
        in_specs=[pl.BlockSpec((tm, d), lambda i: (i, 0)), _const_spec((1, d)), _weight_spec(w, l)],
        out_specs=pl.BlockSpec((tm, n), lambda i: (i, 0)),
        out_shape=jax.ShapeDtypeStruct((m, n), F32),
        compiler_params=_params("arbitrary"),
        name="norm_matmul" if norm else "matmul",
    )(x, g, w)


def _mem_kv_kernel(x_ref, w_ref, o_ref, acc):
    acc[...] = _dot(x_ref[...].astype(BF16), w_ref[...])
    o_ref[...] = acc[...].T


def _mem_kv(l, mem3, w):
    b, mem, d = mem3.shape
    n = w.shape[2]
    return pl.pallas_call(
        _mem_kv_kernel,
        grid=(b,),
        in_specs=[pl.BlockSpec((None, mem, d), lambda i: (i, 0, 0)), _weight_spec(w, l)],
        out_specs=pl.BlockSpec((None, n, mem), lambda i: (i, 0, 0)),
        out_shape=jax.ShapeDtypeStruct((b, n, mem), F32),
        scratch_shapes=[pltpu.VMEM((mem, n), F32)],
        compiler_params=_params("arbitrary"),
        name="mem_kv",
    )(mem3, w)


def _sample_in_kernel(x_ref, g_ref, w_ref, z_ref, zt_ref):
    h = _rms(x_ref[...], g_ref[...]).astype(BF16)
    n = z_ref.shape[1]
    for c in range(0, n, 512):
        w = min(512, n - c)
        zc = _dot(h, w_ref[:, c:c + w])
        z_ref[:, c:c + w] = zc
        zt_ref[c:c + w, :] = zc.T


def _sample_in(l, x, g, w):
    m, d = x.shape
    n = w.shape[2]
    full = lambda shape: pl.BlockSpec(shape, lambda i: (0,) * len(shape))
    return pl.pallas_call(
        _sample_in_kernel,
        grid=(1,),
        in_specs=[full((m, d)), full((1, d)), _weight_spec(w, l)],
        out_specs=[full((m, n)), full((n, m))],
        out_shape=[jax.ShapeDtypeStruct((m, n), F32), jax.ShapeDtypeStruct((n, m), F32)],
        compiler_params=_params("arbitrary"),
        name="sample_in",
    )(x, g, w)


def _mlstm_prompt_kernel(z_ref, g_ref, gb_ref, ng_ref, h_ref, cp_ref, np_ref, m_ref,
                         cp_scr, np_scr, m_scr, ht_scr, st_scr, kw_scr):
    NB, L = z_ref.shape[0], z_ref.shape[1]
    D = HEAD_DIM
    hi = lax.Precision.HIGHEST

    @pl.when(pl.program_id(0) == 0)
    def _():
        cp_scr[...] = jnp.zeros(cp_scr.shape, F32)
        np_scr[...] = jnp.zeros(np_scr.shape, F32)
        m_scr[...] = jnp.zeros(m_scr.shape, F32)

    src = lax.broadcasted_iota(jnp.int32, (L, L), 0)
    qry = lax.broadcasted_iota(jnp.int32, (L, L), 1)
    causal_t = src <= qry
    upper = jnp.where(causal_t, 1.0, 0.0)
    lower = jnp.where(src >= qry, 1.0, 0.0)
    lane_half = lax.broadcasted_iota(jnp.int32, (L, 128), 1) // D
    row8 = lax.broadcasted_iota(jnp.int32, (SUBLANES, 128), 0)

    gates, b_rows, b_cols = [], [], []
    for b in range(NB):
        g = g_ref[b] + gb_ref[...]
        g_t = g.T[0:SUBLANES, :]
        gates.append(g)
        b_rows.append(jnp.dot(_log_sigmoid(g_t), upper, precision=hi, preferred_element_type=F32))
        b_cols.append(jnp.dot(lower, _log_sigmoid(g), precision=hi, preferred_element_type=F32))

    stats = {}
    for b in range(NB):
        for h in range(M_HEADS):
            j, par = h // 2, h % 2
            slab = 128 * j
            own = lane_half == par
            b_row = b_rows[b][M_HEADS + h:M_HEADS + h + 1, :]
            ci_col = gates[b][:, h:h + 1] - b_cols[b][:, M_HEADS + h:M_HEADS + h + 1]
            m_prev = m_scr[b, h, 0:1, 0:1]
            a_row = b_row + m_prev
            dm = jnp.where(causal_t, b_row + ci_col, NEG_INF)
            m_row = jnp.maximum(a_row, jnp.max(dm, axis=0, keepdims=True))
            q2 = z_ref[b, :, slab:slab + 128].astype(BF16)
            k2 = jnp.where(own, z_ref[b, :, M_WIDTH + slab:M_WIDTH + slab + 128] * QK_SCALE, 0.0)
            s_t = _dot_nt(k2.astype(BF16), q2) * jnp.exp(dm - m_row)
            st_scr[b, j, :, L * par:L * (par + 1)] = s_t.astype(BF16)
            b_last = b_row[:, L - 1:L]
            g_col = b_last + ci_col
            m_new = jnp.maximum(b_last + m_prev, jnp.max(g_col, axis=0, keepdims=True))
            kw = k2 * jnp.exp(g_col - m_new)
            if par == 0:
                kw_scr[b, j] = kw
            else:
                kw_scr[b, j] = kw_scr[b, j] + kw
            stats[b, h] = dict(w_inter=jnp.exp(a_row - m_row), floor=jnp.exp(-m_row),
                               den_s=jnp.sum(s_t, axis=0, keepdims=True),
                               decay=jnp.exp(b_last + m_prev - m_new), m_new=m_new,
                               k_sum=jnp.sum(kw, axis=0, keepdims=True))

    block_diag = (lax.broadcasted_iota(jnp.int32, (128, 128), 0) // D
                  == lax.broadcasted_iota(jnp.int32, (128, 128), 1) // D)
    for b in range(NB):
        for j in range(M_HEADS // 2):
            slab = 128 * j
            q2 = z_ref[b, :, slab:slab + 128].astype(BF16)
            v2 = z_ref[b, :, 2 * M_WIDTH + slab:2 * M_WIDTH + slab + 128].astype(BF16)
            cp = cp_scr[b, j]
            npair = np_scr[b, j]
            qc = _dot_nt(cp.astype(BF16), q2)
            qn = _dot_nt(npair.astype(BF16), q2)
            pv = _dot_tn(v2, st_scr[b, j])
            upd = jnp.where(block_diag, _dot_tn(v2, kw_scr[b, j].astype(BF16)), 0.0)
            n_new = jnp.zeros((SUBLANES, 128), F32)
            for par in range(2):
                h = 2 * j + par
                st = stats[b, h]
                rows = slice(D * par, D * (par + 1))
                num = st["w_inter"] * qc[rows, :] + pv[rows, L * par:L * (par + 1)]
                den = st["w_inter"] * qn[par:par + 1, :] + st["den_s"]
                hh = num * (1.0 / jnp.maximum(jnp.abs(den), st["floor"]))
                hh = hh * lax.rsqrt(jnp.mean(hh * hh, axis=0, keepdims=True) + EPS)
                ht_scr[b, D * h:D * (h + 1), :] = hh
                cp_scr[b, j, rows, :] = st["decay"] * cp[rows, :] + upd[rows, :]
                n_new = jnp.where(row8 == par, st["decay"] * npair[par:par + 1, :] + st["k_sum"], n_new)
                m_scr[b, h] = jnp.broadcast_to(st["m_new"], m_scr.shape[2:])
            np_scr[b, j] = n_new
        o_gate = _sigmoid(z_ref[b, :, 3 * M_WIDTH:4 * M_WIDTH])
        h_ref[b] = ht_scr[b].T * ng_ref[...] * o_gate

    cp_ref[...] = cp_scr[...]
    np_ref[...] = np_scr[...]
    m_ref[...] = m_scr[...]


def _mlstm_prompt(z3, gbias, ng):
    b, s, _ = z3.shape
    L = M_CHUNK
    P = M_HEADS // 2
    return pl.pallas_call(
        _mlstm_prompt_kernel,
        grid=(s // L,),
        in_specs=[
            pl.BlockSpec((b, L, 4 * M_WIDTH), lambda c: (0, c, Z_M // (4 * M_WIDTH))),
            pl.BlockSpec((b, L, 128), lambda c: (0, c, Z_G // 128)),
            _const_spec((1, 128)),
            _const_spec((1, M_WIDTH)),
        ],
        out_specs=[
            pl.BlockSpec((b, L, M_WIDTH), lambda c: (0, c, 0)),
            pl.BlockSpec((b, P, 128, 128), lambda c: (0, 0, 0, 0)),
            pl.BlockSpec((b, P, SUBLANES, 128), lambda c: (0, 0, 0, 0)),
            pl.BlockSpec((b, M_HEADS, SUBLANES, 128), lambda c: (0, 0, 0, 0)),
        ],
        out_shape=[
            jax.ShapeDtypeStruct((b, s, M_WIDTH), F32),
            jax.ShapeDtypeStruct((b, P, 128, 128), F32),
            jax.ShapeDtypeStruct((b, P, SUBLANES, 128), F32),
            jax.ShapeDtypeStruct((b, M_HEADS, SUBLANES, 128), F32),
        ],
        scratch_shapes=[
            pltpu.VMEM((b, P, 128, 128), F32),
            pltpu.VMEM((b, P, SUBLANES, 128), F32),
            pltpu.VMEM((b, M_HEADS, SUBLANES, 128), F32),
            pltpu.VMEM((b, M_WIDTH, L), F32),
            pltpu.VMEM((b, P, L, 2 * L), BF16),
            pltpu.VMEM((b, P, L, 128), F32),
        ],
        compiler_params=_params("arbitrary"),
        name="mlstm_prompt",
    )(z3, z3, gbias, ng)


CONV_PAD = 32


def _conv_prompt_kernel(z_ref, w_ref, cb_ref, lg_ref, lb_ref, h_ref, tail_ref, buf, shifted):
    tc = z_ref.shape[0]
    t = pl.program_id(1)

    @pl.when(t == 0)
    def _():
        buf[0:CONV_PAD, :] = jnp.zeros((CONV_PAD, C_WIDTH), F32)

    @pl.when(t > 0)
    def _():
        buf[0:CONV_PAD, :] = buf[tc:tc + CONV_PAD, :]

    u = z_ref[:, 0:C_WIDTH] * _sigmoid(z_ref[:, C_WIDTH:2 * C_WIDTH])
    buf[CONV_PAD:CONV_PAD + tc, :] = u
    n_sh = shifted.shape[1]
    for r in range(1, SUBLANES):
        shifted[r - 1] = buf[r:r + n_sh, :]
    off = CONV_PAD - (C_KERNEL - 1)
    acc = jnp.broadcast_to(cb_ref[...], (tc, C_WIDTH))
    for j in range(C_KERNEL):
        r = (off + j) % SUBLANES
        base = off + j - r
        win = buf[base:base + tc, :] if r == 0 else shifted[r - 1, base:base + tc, :]
        acc = acc + w_ref[j:j + 1, :] * win
    mu = jnp.mean(acc, axis=-1, keepdims=True)
    xc = acc - mu
    y = xc * lax.rsqrt(jnp.mean(xc * xc, axis=-1, keepdims=True) + EPS) * lg_ref[...] + lb_ref[...]
    h_ref[...] = y * _sigmoid(y)
    tail_ref[...] = buf[tc:tc + CONV_PAD, :]


def _conv_prompt(z3, w, cb, lg, lb, *, tc):
    b, s, _ = z3.shape
    return pl.pallas_call(
        _conv_prompt_kernel,
        grid=(b, s // tc),
        in_specs=[
            pl.BlockSpec((None, tc, 2 * C_WIDTH), lambda i, t: (i, t, Z_C // (2 * C_WIDTH))),
            _const_spec((CONV_PAD, C_WIDTH)),
            _const_spec((1, C_WIDTH)), _const_spec((1, C_WIDTH)), _const_spec((1, C_WIDTH)),
        ],
        out_specs=[
            pl.BlockSpec((None, tc, C_WIDTH), lambda i, t: (i, t, 0)),
            pl.BlockSpec((None, CONV_PAD, C_WIDTH), lambda i, t: (i, 0, 0)),
        ],
        out_shape=[
            jax.ShapeDtypeStruct((b, s, C_WIDTH), F32),
            jax.ShapeDtypeStruct((b, CONV_PAD, C_WIDTH), F32),
        ],
        scratch_shapes=[pltpu.VMEM((CONV_PAD + tc, C_WIDTH), F32),
                        pltpu.VMEM((SUBLANES - 1, CONV_PAD + tc - SUBLANES, C_WIDTH), F32)],
        compiler_params=_params("arbitrary", "arbitrary"),
        name="conv_prompt",
    )(z3, w, cb, lg, lb)


def _swa_prompt_kernel(rb_ref, sink_ref, bucket_ref, q_ref, kv_ref, kvp_ref, o_ref, kt_ref, vt_ref,
                       bias_scr, s_scr, p_scr, ot_scr):
    W = WINDOW
    H = S_Q_HEADS
    G = H // S_KV_HEADS
    i = pl.program_id(0)
    n = pl.program_id(1)

    @pl.when(jnp.logical_and(i == 0, n == 0))
    def _():
        bucket = bucket_ref[...]
        prev_key = lax.broadcasted_iota(jnp.int32, (2 * W, W), 0) < W
        for h in range(H):
            acc = jnp.full((2 * W, W), NEG_INF, F32)
            for b in range(N_BUCKETS):
                acc = jnp.where(bucket == b, rb_ref[b, h], acc)
            acc = acc * LOG2E
            bias_scr[0, h] = acc
            bias_scr[1, h] = jnp.where(prev_key, NEG_INF, acc)

    QB = q_ref.shape[0] // W
    NK = (QB + 1) * W
    first = jnp.where(n == 0, 1, 0)
    kk = jnp.concatenate([kvp_ref[:, 0:S_KV_WIDTH], kv_ref[:, 0:S_KV_WIDTH]], axis=0) * (QK_SCALE * LOG2E)
    vv = jnp.concatenate([kvp_ref[:, S_KV_WIDTH:2 * S_KV_WIDTH], kv_ref[:, S_KV_WIDTH:2 * S_KV_WIDTH]], axis=0)
    kk_r = pltpu.roll(kk, HEAD_DIM, axis=1)
    lo_lane = lax.broadcasted_iota(jnp.int32, (NK, S_KV_WIDTH), 1) < HEAD_DIM
    k_var = [[jnp.where(lo_lane, kk, 0.0).astype(BF16), jnp.where(lo_lane, 0.0, kk_r).astype(BF16)],
             [jnp.where(lo_lane, kk_r, 0.0).astype(BF16), jnp.where(lo_lane, 0.0, kk).astype(BF16)]]
    v_t = vv.T.astype(BF16)

    for j in range(QB):
        k0 = j * W
        masked = first if j == 0 else 0
        m_rows = [None] * H
        for hk in range(S_KV_HEADS):
            c0 = 2 * 128 * hk
            q_st = jnp.concatenate([q_ref[k0:k0 + W, c0:c0 + 128], q_ref[k0:k0 + W, c0 + 128:c0 + 256]],
                                   axis=0).astype(BF16)
            for half in range(2):
                s_t = _dot_nt(k_var[hk][half][k0:k0 + 2 * W, :], q_st)
                for slab in range(2):
                    head = G * hk + 2 * slab + half
                    sb = s_t[:, 128 * slab:128 * (slab + 1)] + bias_scr[masked, head]
                    s_scr[j, head] = sb
                    m_rows[head] = jnp.maximum(jnp.max(sb, axis=0, keepdims=True), sink_ref[head] * LOG2E)

        inv = [None] * H
        for head in range(H):
            e = jnp.exp2(s_scr[j, head] - m_rows[head])
            den = jnp.sum(e, axis=0, keepdims=True) + jnp.exp2(sink_ref[head] * LOG2E - m_rows[head])
            inv[head] = 1.0 / den
            p_scr[j, :, 128 * head:128 * (head + 1)] = e.astype(BF16)

        for hk in range(S_KV_HEADS):
            o_t = _dot(v_t[HEAD_DIM * hk:HEAD_DIM * (hk + 1), k0:k0 + 2 * W],
                       p_scr[j, :, 128 * G * hk:128 * G * (hk + 1)])
            for g in range(G):
                head = G * hk + g
                ot_scr[j, HEAD_DIM * head:HEAD_DIM * (head + 1), :] = o_t[:, 128 * g:128 * (g + 1)] * inv[head]
        o_ref[k0:k0 + W, :] = ot_scr[j].T

    @pl.when(n == pl.num_programs(1) - 1)
    def _():
        kt_ref[...] = kv_ref[(QB - 1) * W:QB * W, 0:S_KV_WIDTH].T
        vt_ref[...] = kv_ref[(QB - 1) * W:QB * W, S_KV_WIDTH:2 * S_KV_WIDTH].T


def _swa_prompt(z3, rel_bias, sinks, bucket, *, qb):
    b, s, _ = z3.shape
    W = WINDOW
    smem = pl.BlockSpec(memory_space=pltpu.SMEM)
    return pl.pallas_call(
        _swa_prompt_kernel,
        grid=(b, s // (qb * W)),
        in_specs=[
            smem, smem, _const_spec((2 * W, W)),
            pl.BlockSpec((None, qb * W, S_WIDTH), lambda i, n: (i, n, Z_SQ // S_WIDTH)),
            pl.BlockSpec((None, qb * W, 2 * S_KV_WIDTH), lambda i, n: (i, n, Z_SKV // (2 * S_KV_WIDTH))),
            pl.BlockSpec((None, W, 2 * S_KV_WIDTH),
                         lambda i, n: (i, jnp.maximum(n * qb - 1, 0), Z_SKV // (2 * S_KV_WIDTH))),
        ],
        out_specs=[
            pl.BlockSpec((None, qb * W, S_WIDTH), lambda i, n: (i, n, 0)),
            pl.BlockSpec((None, S_KV_WIDTH, W), lambda i, n: (i, 0, 0)),
            pl.BlockSpec((None, S_KV_WIDTH, W), lambda i, n: (i, 0, 0)),
        ],
        out_shape=[
            jax.ShapeDtypeStruct((b, s, S_WIDTH), F32),
            jax.ShapeDtypeStruct((b, S_KV_WIDTH, W), F32),
            jax.ShapeDtypeStruct((b, S_KV_WIDTH, W), F32),
        ],
        scratch_shapes=[
            pltpu.VMEM((2, S_Q_HEADS, 2 * W, W), F32),
            pltpu.VMEM((qb, S_Q_HEADS, 2 * W, W), F32),
            pltpu.VMEM((qb, 2 * W, S_Q_HEADS * W), BF16),
            pltpu.VMEM((qb, S_WIDTH, W), F32),
        ],
        compiler_params=_params("arbitrary", "arbitrary"),
        name="swa_prompt",
    )(rel_bias, sinks, bucket, z3, z3, z3)


MIX_SUB = 1024


def _cross_attend_rows(x1, g2_ref, wq_ref, k_heads, v_cat, wo_ref):
    qx = _dot(_rms(x1, g2_ref[...]).astype(BF16), wq_ref[...]).astype(BF16)
    probs = []
    for h in range(X_HEADS):
        s = _dot(qx, k_heads[h]) * QK_SCALE
        e = jnp.exp(s - jnp.max(s, axis=1, keepdims=True))
        probs.append((e / jnp.sum(e, axis=1, keepdims=True)).astype(BF16))
    o = _dot_nt(jnp.concatenate(probs, axis=1), v_cat)
    return x1 + _dot(o.astype(BF16), wo_ref[...])


def _mix_out_prompt_kernel(x_ref, hm_ref, hc_ref, hs_ref, wout_ref, g2_ref, wq_ref, mk_ref, mv_ref, wo_ref, o_ref):
    tm = x_ref.shape[0]
    mk_t = mk_ref[...]
    mv_t = mv_ref[...]
    row_head = lax.broadcasted_iota(jnp.int32, mk_t.shape, 0) // HEAD_DIM
    k_heads = [jnp.where(row_head == h, mk_t, 0.0).astype(BF16) for h in range(X_HEADS)]
    v_cat = jnp.concatenate([jnp.where(row_head == h, mv_t, 0.0).astype(BF16) for h in range(X_HEADS)], axis=1)
    for r0 in range(0, tm, MIX_SUB):
        rows = slice(r0, r0 + MIX_SUB)
        cat = jnp.concatenate([hm_ref[rows, :].astype(BF16), hc_ref[rows, :].astype(BF16),
                               hs_ref[rows, :].astype(BF16)], axis=1)
        x1 = x_ref[rows, :] + _dot(cat, wout_ref[...])
        o_ref[rows, :] = _cross_attend_rows(x1, g2_ref, wq_ref, k_heads, v_cat, wo_ref)


def _mix_out_prompt(l, x3, hm, hc, hs, wout, g2, wq, mkv, wo, *, tm):
    b, s, d = x3.shape
    tm = min(tm, s)
    mem = mkv.shape[2]
    row = lambda w: pl.BlockSpec((None, tm, w), lambda i, t: (i, t, 0))
    return pl.pallas_call(
        _mix_out_prompt_kernel,
        grid=(b, s // tm),
        in_specs=[
            row(d), row(M_WIDTH), row(C_WIDTH), row(S_WIDTH),
            _weight_spec(wout, l), _const_spec((1, d)), _weight_spec(wq, l),
            pl.BlockSpec((None, X_WIDTH, mem), lambda i, t: (i, 0, 0)),
            pl.BlockSpec((None, X_WIDTH, mem), lambda i, t: (i, 1, 0)),
            _weight_spec(wo, l),
        ],
        out_specs=row(d),
        out_shape=jax.ShapeDtypeStruct((b, s, d), F32),
        compiler_params=_params("arbitrary", "arbitrary"),
        name="mix_out_prompt",
    )(x3, hm, hc, hs, wout, g2, wq, mkv, mkv, wo)


FFN_PAD = 8


FFN_SUB = 512


def _ffn_prompt_kernel(x_ref, g3_ref, wup_ref, cw_ref, cb_ref, wdn_ref, gf_ref, o_ref, tail_ref,
                       gbuf, carry, act, *, final):
    tm = x_ref.shape[0]
    dff = wdn_ref.shape[0]
    sub = min(FFN_SUB, tm)

    @pl.when(pl.program_id(1) == 0)
    def _():
        carry[...] = jnp.zeros(carry.shape, F32)

    for s in range(tm // sub):
        rows = slice(s * sub, (s + 1) * sub)
        x = x_ref[rows, :]
        h = _rms(x, g3_ref[...]).astype(BF16)
        for c in range(0, dff, FFN_CHUNK):
            a = _dot(h, wup_ref[:, c:c + FFN_CHUNK])
            g = _dot(h, wup_ref[:, dff + c:dff + c + FFN_CHUNK])
            gbuf[s, 0:FFN_PAD, :] = carry[:, c:c + FFN_CHUNK]
            gbuf[s, FFN_PAD:FFN_PAD + sub, :] = g
            carry[:, c:c + FFN_CHUNK] = g[sub - FFN_PAD:sub, :]
            gc = (cw_ref[0:1, c:c + FFN_CHUNK] * gbuf[s, FFN_PAD - 2:FFN_PAD - 2 + sub, :]
                  + cw_ref[1:2, c:c + FFN_CHUNK] * gbuf[s, FFN_PAD - 1:FFN_PAD - 1 + sub, :]
                  + cw_ref[2:3, c:c + FFN_CHUNK] * g + cb_ref[:, c:c + FFN_CHUNK])
            act[s, :, c:c + FFN_CHUNK] = (gc * _sigmoid(gc) * a).astype(BF16)
        y = x + _dot(act[s], wdn_ref[...])
        o_ref[rows, :] = _rms(y, gf_ref[...]) if final else y
    tail_ref[...] = carry[...]


def _ffn_prompt(l, x3, g3, wup, cw, cb, wdn, gf, *, tm, final):
    b, s, d = x3.shape
    tm = min(tm, s)
    dff = wdn.shape[1]
    sub = min(FFN_SUB, tm)
    return pl.pallas_call(
        functools.partial(_ffn_prompt_kernel, final=final),
        grid=(b, s // tm),
        in_specs=[
            pl.BlockSpec((None, tm, d), lambda i, t: (i, t, 0)),
            _const_spec((1, d)), _weight_spec(wup, l), _const_spec(cw.shape), _const_spec((1, dff)),
            _weight_spec(wdn, l), _const_spec((1, d)),
        ],
        out_specs=[
            pl.BlockSpec((None, tm, d), lambda i, t: (i, t, 0)),
            pl.BlockSpec((None, FFN_PAD, dff), lambda i, t: (i, 0, 0)),
        ],
        out_shape=[jax.ShapeDtypeStruct((b, s, d), F32), jax.ShapeDtypeStruct((b, FFN_PAD, dff), F32)],
        scratch_shapes=[
            pltpu.VMEM((tm // sub, FFN_PAD + sub, FFN_CHUNK), F32),
            pltpu.VMEM((FFN_PAD, dff), F32),
            pltpu.VMEM((tm // sub, sub, dff), BF16),
        ],
        compiler_params=_params("arbitrary", "arbitrary"),
        name="ffn_prompt",
    )(x3, g3, wup, cw, cb, wdn, gf)


SWA_SAMPLE_BLOCK = 16
XATTN_SAMPLE_BLOCK = 8
CONV_SAMPLE_BLOCK = 32


def _mlstm_sample_kernel(gb_ref, q_ref, k_ref, v_ref, o_ref, g_ref, ng_ref, c_ref, n_ref, m_ref,
                         h_ref, co_ref, no_ref, mo_ref, kw_scr):
    h = pl.program_id(0)
    i_pre = g_ref[pl.ds(h, 1), :] + gb_ref[h]
    f_pre = g_ref[pl.ds(M_HEADS + h, 1), :] + gb_ref[M_HEADS + h]
    a = _log_sigmoid(f_pre) + m_ref[pl.ds(h, 1), :]
    m_t = jnp.maximum(a, i_pre)
    w_old = jnp.exp(a - m_t)
    w_new = jnp.exp(i_pre - m_t)
    q = q_ref[...]
    k = k_ref[...] * QK_SCALE
    v = v_ref[...]
    n_old = n_ref[...]
    kw_scr[...] = k * w_new

    def body(d, acc):
        c_old = c_ref[d]
        co_ref[d] = w_old * c_old + kw_scr[pl.ds(d, 1), :] * v
        return acc + q_ref[pl.ds(d, 1), :] * c_old

    qc = lax.fori_loop(0, HEAD_DIM, body, jnp.zeros(v.shape, F32), unroll=8)
    s = jnp.sum(q * k, axis=0, keepdims=True) * w_new
    num = w_old * qc + s * v
    den = w_old * jnp.sum(q * n_old, axis=0, keepdims=True) + s
    hh = num / jnp.maximum(jnp.abs(den), jnp.exp(-m_t))
    hh = hh * lax.rsqrt(jnp.mean(hh * hh, axis=0, keepdims=True) + EPS) * ng_ref[...]
    h_ref[...] = hh * _sigmoid(o_ref[...])
    no_ref[...] = w_old * n_old + kw_scr[...]
    mo_ref[pl.ds(h, 1), :] = m_t


def _mlstm_sample(l, zt, gb, ngt, c_all, n_all, m_all):
    bsz = zt.shape[1]
    D = HEAD_DIM
    feat = lambda off: pl.BlockSpec((D, bsz), lambda h: (off // D + h, 0))
    return pl.pallas_call(
        _mlstm_sample_kernel,
        grid=(M_HEADS,),
        in_specs=[
            pl.BlockSpec(memory_space=pltpu.SMEM),
            feat(Z_M), feat(Z_M + M_WIDTH), feat(Z_M + 2 * M_WIDTH), feat(Z_M + 3 * M_WIDTH),
            pl.BlockSpec((SUBLANES, bsz), lambda h: (Z_G // SUBLANES, 0)),
            pl.BlockSpec((D, bsz), lambda h: (h, 0)),
            pl.BlockSpec((None, None, D, D, bsz), lambda h: (l, h, 0, 0, 0)),
            pl.BlockSpec((None, None, D, bsz), lambda h: (l, h, 0, 0)),
            pl.BlockSpec((None, M_HEADS, bsz), lambda h: (l, 0, 0)),
        ],
        out_specs=[
            pl.BlockSpec((D, bsz), lambda h: (h, 0)),
            pl.BlockSpec((None, D, D, bsz), lambda h: (h, 0, 0, 0)),
            pl.BlockSpec((None, D, bsz), lambda h: (h, 0, 0)),
            pl.BlockSpec((M_HEADS, bsz), lambda h: (0, 0)),
        ],
        out_shape=[
            jax.ShapeDtypeStruct((M_WIDTH, bsz), F32),
            jax.ShapeDtypeStruct((M_HEADS, D, D, bsz), F32),
            jax.ShapeDtypeStruct((M_HEADS, D, bsz), F32),
            jax.ShapeDtypeStruct((M_HEADS, bsz), F32),
        ],
        scratch_shapes=[pltpu.VMEM((D, bsz), F32)],
        compiler_params=_params("arbitrary"),
        name="mlstm_sample",
    )(gb, zt, zt, zt, zt, zt, ngt, c_all, n_all, m_all)


def _conv_sample_kernel(z_ref, hist_ref, w_ref, cb_ref, lg_ref, lb_ref, h_ref, hist_out_ref):
    nh = C_KERNEL - 1
    u = z_ref[:, 0:C_WIDTH] * _sigmoid(z_ref[:, C_WIDTH:2 * C_WIDTH])
    acc = cb_ref[...] + w_ref[nh:nh + 1, :] * u
    for j in range(nh):
        acc = acc + w_ref[j:j + 1, :] * hist_ref[j]
    mu = jnp.mean(acc, axis=-1, keepdims=True)
    xc = acc - mu
    y = xc * lax.rsqrt(jnp.mean(xc * xc, axis=-1, keepdims=True) + EPS) * lg_ref[...] + lb_ref[...]
    h_ref[...] = y * _sigmoid(y)
    for j in range(nh - 1):
        hist_out_ref[j] = hist_ref[j + 1]
    hist_out_ref[nh - 1] = u


def _conv_sample(l, z, hist_all, w, cb, lg, lb):
    bsz = z.shape[0]
    nh = hist_all.shape[1]
    R = min(CONV_SAMPLE_BLOCK, bsz)
    return pl.pallas_call(
        _conv_sample_kernel,
        grid=(bsz // R,),
        in_specs=[
            pl.BlockSpec((R, 2 * C_WIDTH), lambda i: (i, Z_C // (2 * C_WIDTH))),
            pl.BlockSpec((None, nh, R, C_WIDTH), lambda i: (l, 0, i, 0)),
            _const_spec((CONV_PAD, C_WIDTH)),
            _const_spec((1, C_WIDTH)), _const_spec((1, C_WIDTH)), _const_spec((1, C_WIDTH)),
        ],
        out_specs=[pl.BlockSpec((R, C_WIDTH), lambda i: (i, 0)), pl.BlockSpec((nh, R, C_WIDTH), lambda i: (0, i, 0))],
        out_shape=[jax.ShapeDtypeStruct((bsz, C_WIDTH), F32), jax.ShapeDtypeStruct((nh, bsz, C_WIDTH), F32)],
        compiler_params=_params("arbitrary"),
        name="conv_sample",
    )(z, hist_all, w, cb, lg, lb)


def _swa_sample_kernel(q_ref, kvn_ref, kv2_ref, kc_ref, vc_ref, bias_ref, aux_ref, o_ref, ko_ref, vo_ref):
    R = q_ref.shape[0]
    W = kc_ref.shape[2]
    H = S_Q_HEADS
    shape = (R, H, 128)
    row = lax.broadcasted_iota(jnp.int32, shape, 1)
    lane_half = lax.broadcasted_iota(jnp.int32, shape, 2) // HEAD_DIM
    q_half = row % 2
    kv_head = row // (H // S_KV_HEADS)
    qs = jnp.zeros(shape, F32)
    for j in range(H // 2):
        qs = jnp.where(row // 2 == j, q_ref[:, :, 128 * j:128 * (j + 1)], qs)
    q8 = jnp.where(lane_half == kv_head, jnp.where(q_half == kv_head, qs, pltpu.roll(qs, HEAD_DIM, axis=2)), 0.0)
    k_new = kvn_ref[:, :, 0:S_KV_WIDTH]
    v_new = kvn_ref[:, :, S_KV_WIDTH:2 * S_KV_WIDTH]
    s = jnp.einsum("bqd,bdk->bqk", q8.astype(BF16), kc_ref[...].astype(BF16), preferred_element_type=F32) * QK_SCALE
    s = s + bias_ref[...][None]
    s_new = jnp.sum(q8 * k_new, axis=2, keepdims=True) * QK_SCALE + aux_ref[:, 0:1][None]
    sink = aux_ref[:, 1:2][None]
    m = jnp.maximum(jnp.maximum(jnp.max(s, axis=2, keepdims=True), s_new), sink)
    e = jnp.exp(s - m)
    e_new = jnp.exp(s_new - m)
    inv = 1.0 / (jnp.sum(e, axis=2, keepdims=True) + e_new + jnp.exp(sink - m))
    o8 = jnp.einsum("bqk,bdk->bqd", e.astype(BF16), vc_ref[...].astype(BF16), preferred_element_type=F32)
    o8 = (o8 + e_new * v_new) * inv
    o8 = jnp.where(lane_half == q_half, jnp.where(q_half == kv_head, o8, pltpu.roll(o8, HEAD_DIM, axis=2)), 0.0)
    for j in range(H // 2):
        o_ref[:, :, 128 * j:128 * (j + 1)] = jnp.sum(jnp.where(row // 2 == j, o8, 0.0), axis=1, keepdims=True)
    k_cols = kv2_ref[:, 0:S_KV_WIDTH].T
    v_cols = kv2_ref[:, S_KV_WIDTH:2 * S_KV_WIDTH].T
    last = lax.broadcasted_iota(jnp.int32, (S_KV_WIDTH, W), 1) == W - 1
    for r in range(R):
        ko_ref[r] = jnp.where(last, k_cols[:, r:r + 1], pltpu.roll(kc_ref[r], W - 1, axis=1))
        vo_ref[r] = jnp.where(last, v_cols[:, r:r + 1], pltpu.roll(vc_ref[r], W - 1, axis=1))


def _swa_sample(l, z, z3, kc_all, vc_all, bias, aux):
    bsz = z.shape[0]
    R = min(SWA_SAMPLE_BLOCK, bsz)
    W = kc_all.shape[3]
    cache_in = pl.BlockSpec((None, R, S_KV_WIDTH, W), lambda i: (l, i, 0, 0))
    cache_out = pl.BlockSpec((R, S_KV_WIDTH, W), lambda i: (i, 0, 0))
    return pl.pallas_call(
        _swa_sample_kernel,
        grid=(bsz // R,),
        in_specs=[
            pl.BlockSpec((R, 1, S_WIDTH), lambda i: (i, 0, Z_SQ // S_WIDTH)),
            pl.BlockSpec((R, 1, 2 * S_KV_WIDTH), lambda i: (i, 0, Z_SKV // (2 * S_KV_WIDTH))),
            pl.BlockSpec((R, 2 * S_KV_WIDTH), lambda i: (i, Z_SKV // (2 * S_KV_WIDTH))),
            cache_in, cache_in,
            _const_spec((S_Q_HEADS, 128)), _const_spec((S_Q_HEADS, 128)),
        ],
        out_specs=[pl.BlockSpec((R, 1, S_WIDTH), lambda i: (i, 0, 0)), cache_out, cache_out],
        out_shape=[
            jax.ShapeDtypeStruct((bsz, 1, S_WIDTH), F32),
            jax.ShapeDtypeStruct((bsz, S_KV_WIDTH, W), F32),
            jax.ShapeDtypeStruct((bsz, S_KV_WIDTH, W), F32),
        ],
        compiler_params=_params("arbitrary"),
        name="swa_sample",
    )(z3, z3, z, kc_all, vc_all, bias, aux)


def _mix_out_sample_kernel(x_ref, hmt_ref, hc_ref, hs_ref, wout_ref, g2_ref, wq_ref, x1_ref, q_ref):
    cat = jnp.concatenate([hmt_ref[...].T.astype(BF16), hc_ref[...].astype(BF16), hs_ref[...].astype(BF16)], axis=1)
    x1 = x_ref[...] + _dot(cat, wout_ref[...])
    x1_ref[...] = x1
    q_ref[...] = _dot(_rms(x1, g2_ref[...]).astype(BF16), wq_ref[...])


def _mix_out_sample(l, x, hm, hc, hs, wout, g2, wq):
    bsz, d = x.shape
    full = lambda a: pl.BlockSpec(a.shape, lambda i: (0,) * a.ndim)
    args = (x, hm, hc, hs, wout, g2, wq)
    return pl.pallas_call(
        _mix_out_sample_kernel,
        grid=(1,),
        in_specs=[full(x), full(hm), full(hc), full(hs), _weight_spec(wout, l), full(g2), _weight_spec(wq, l)],
        out_specs=[pl.BlockSpec((bsz, d), lambda i: (0, 0)), pl.BlockSpec((bsz, X_WIDTH), lambda i: (0, 0))],
        out_shape=[jax.ShapeDtypeStruct((bsz, d), F32), jax.ShapeDtypeStruct((bsz, X_WIDTH), F32)],
        compiler_params=_params("arbitrary"),
        name="mix_out_sample",
    )(*args)


def _xattn_sample_kernel(q_ref, k_ref, v_ref, o_ref):
    R = k_ref.shape[0]
    shape = (R, SUBLANES, X_WIDTH)
    row = lax.broadcasted_iota(jnp.int32, shape, 1)
    lane_head = lax.broadcasted_iota(jnp.int32, shape, 2) // HEAD_DIM
    own = row == lane_head
    q8 = jnp.where(own, jnp.broadcast_to(q_ref[...], shape), 0.0).astype(BF16)
    s = jnp.einsum("bqd,bdk->bqk", q8, k_ref[...].astype(BF16), preferred_element_type=F32) * QK_SCALE
    e = jnp.exp(s - jnp.max(s, axis=2, keepdims=True))
    p = (e / jnp.sum(e, axis=2, keepdims=True)).astype(BF16)
    o8 = jnp.einsum("bqk,bdk->bqd", p, v_ref[...].astype(BF16), preferred_element_type=F32)
    o_ref[...] = jnp.sum(jnp.where(own, o8, 0.0), axis=1, keepdims=True)


def _xattn_sample(l, q3, k_all, v_all):
    _, bsz, w, mem = k_all.shape
    R = min(XATTN_SAMPLE_BLOCK, bsz)
    kv = pl.BlockSpec((None, R, w, mem), lambda i: (l, i, 0, 0))
    qo = pl.BlockSpec((R, 1, w), lambda i: (i, 0, 0))
    return pl.pallas_call(
        _xattn_sample_kernel,
        grid=(bsz // R,),
        in_specs=[qo, kv, kv],
        out_specs=qo,
        out_shape=jax.ShapeDtypeStruct((bsz, 1, w), F32),
        compiler_params=_params("arbitrary"),
        name="xattn_sample",
    )(q3, k_all, v_all)


def _ffn_sample_kernel(x1_ref, ox_ref, wo_ref, g3_ref, wup_ref, cw_ref, cb_ref, wdn_ref, gf_ref, hist_ref,
                       o_ref, hist_out_ref, act, *, final):
    dff = wdn_ref.shape[0]
    x = x1_ref[...] + _dot(ox_ref[...].astype(BF16), wo_ref[...])
    h = _rms(x, g3_ref[...]).astype(BF16)
    for c in range(0, dff, FFN_CHUNK):
        a = _dot(h, wup_ref[:, c:c + FFN_CHUNK])
        g = _dot(h, wup_ref[:, dff + c:dff + c + FFN_CHUNK])
        h1 = hist_ref[:, dff + c:dff + c + FFN_CHUNK]
        gc = (cw_ref[0:1, c:c + FFN_CHUNK] * hist_ref[:, c:c + FFN_CHUNK]
              + cw_ref[1:2, c:c + FFN_CHUNK] * h1
              + cw_ref[2:3, c:c + FFN_CHUNK] * g + cb_ref[:, c:c + FFN_CHUNK])
        act[:, c:c + FFN_CHUNK] = (gc * _sigmoid(gc) * a).astype(BF16)
        hist_out_ref[:, c:c + FFN_CHUNK] = h1
        hist_out_ref[:, dff + c:dff + c + FFN_CHUNK] = g
    y = x + _dot(act[...], wdn_ref[...])
    o_ref[...] = _rms(y, gf_ref[...]) if final else y


def _ffn_sample(l, x1, ox, wo, g3, wup, cw, cb, wdn, gf, hist, *, final):
    bsz, d = x1.shape
    dff = wdn.shape[1]
    full = lambda a: pl.BlockSpec(a.shape, lambda i: (0,) * a.ndim)
    args = (x1, ox, wo, g3, wup, cw, cb, wdn, gf, hist)
    return pl.pallas_call(
        functools.partial(_ffn_sample_kernel, final=final),
        grid=(1,),
        in_specs=[full(x1), full(ox), _weight_spec(wo, l), full(g3), _weight_spec(wup, l), full(cw), full(cb),
                  _weight_spec(wdn, l), full(gf), full(hist)],
        out_specs=[pl.BlockSpec((bsz, d), lambda i: (0, 0)), pl.BlockSpec(hist.shape, lambda i: (0, 0))],
        out_shape=[jax.ShapeDtypeStruct((bsz, d), F32), jax.ShapeDtypeStruct(hist.shape, F32)],
        scratch_shapes=[pltpu.VMEM((bsz, dff), BF16)],
        compiler_params=_params("arbitrary"),
        name="ffn_sample",
    )(*args)


def _t5_buckets(dist):
    n = np.maximum(dist, 0)
    max_exact = N_BUCKETS // 2
    nf = np.maximum(n, max_exact).astype(np.float32)
    large = max_exact + (np.log(nf / np.float32(max_exact)) / np.float32(math.log(MAX_DISTANCE / max_exact))
                         * np.float32(N_BUCKETS - max_exact)).astype(np.int32)
    return np.where(n < max_exact, n, np.minimum(large, N_BUCKETS - 1))


def _prompt_buckets():
    W = WINDOW
    dist = np.arange(W)[None, :] + W - np.arange(2 * W)[:, None]
    band = (dist >= 0) & (dist < W)
    return np.where(band, _t5_buckets(dist), -1).astype(np.int32)


def _swa_tables(rel_bias):
    W = WINDOW
    dist_c = W - np.arange(W)
    tab = jnp.transpose(rel_bias[_t5_buckets(dist_c)], (1, 0))
    cache_bias = jnp.where((dist_c < W)[None], tab, NEG_INF)
    return cache_bias, rel_bias[0]


def _pad_rows(a, rows):
    return jnp.concatenate([a, jnp.zeros((rows - a.shape[0],) + a.shape[1:], a.dtype)], axis=0)


def _row(a):
    return a.reshape(1, -1)


def kernel(x_prompt, x_sample, mem_prompt, state_mlstm_C, state_mlstm_n, state_mlstm_m, state_conv, cache_swa_k, cache_swa_v, cache_mem_k, cache_mem_v, state_ffn_conv, rel_bias, norm1_g, w_in, b_i, b_f, mlstm_norm_g, conv_w, conv_b, conv_ln_g, conv_ln_b, swa_sinks, w_out, norm2_g, w_xq, w_xk, w_xv, w_xo, norm3_g, w_up, ffn_conv_w, ffn_conv_b, w_down, final_norm_g):
    depth = w_in.shape[0]
    bp, seq, d = x_prompt.shape
    bs = x_sample.shape[0]
    mem = mem_prompt.shape[1]
    dff = w_down.shape[1]
    W = WINDOW
    nh = C_KERNEL - 1

    xp = x_prompt
    xs = x_sample.reshape(bs, d)
    gf = _row(final_norm_g)
    c_all = jnp.transpose(state_mlstm_C, (0, 2, 3, 4, 1))
    n_all = jnp.transpose(state_mlstm_n, (0, 2, 3, 1))
    m_all = jnp.transpose(state_mlstm_m, (0, 2, 1))
    hist_all = jnp.transpose(state_conv, (0, 2, 1, 3))
    kc_all = jnp.transpose(cache_swa_k, (0, 1, 3, 4, 2)).reshape(depth, bs, S_KV_WIDTH, W)
    vc_all = jnp.transpose(cache_swa_v, (0, 1, 3, 4, 2)).reshape(depth, bs, S_KV_WIDTH, W)
    mk_all = jnp.transpose(cache_mem_k, (0, 1, 3, 4, 2)).reshape(depth, bs, X_WIDTH, mem)
    mv_all = jnp.transpose(cache_mem_v, (0, 1, 3, 4, 2)).reshape(depth, bs, X_WIDTH, mem)
    pm_c, pm_n, pm_m, p_conv, p_k, p_v, p_mk, p_mv, p_ffn = ([] for _ in range(9))
    s_c, s_n, s_m, s_conv, s_k, s_v, s_ffn = ([] for _ in range(7))

    gate_cols = w_in[:, :, 4 * M_WIDTH:4 * M_WIDTH + 2 * M_HEADS]
    win = jnp.concatenate(
        [w_in[:, :, 0:4 * M_WIDTH], w_in[:, :, 4 * M_WIDTH + 2 * M_HEADS:], gate_cols,
         jnp.zeros((depth, d, 128 - 2 * M_HEADS), F32)], axis=2).astype(BF16)
    wout = w_out.astype(BF16)
    wxq = w_xq.astype(BF16)
    wxkv = jnp.concatenate([w_xk, w_xv], axis=2).astype(BF16)
    wxo = w_xo.astype(BF16)
    wup = w_up.astype(BF16)
    wdn = w_down.astype(BF16)
    bucket = jnp.asarray(_prompt_buckets())

    for l in range(depth):
        last = l == depth - 1
        g1, g2, g3 = _row(norm1_g[l]), _row(norm2_g[l]), _row(norm3_g[l])
        gbias = jnp.concatenate([b_i[l], b_f[l], jnp.zeros((128 - 2 * M_HEADS,), F32)]).reshape(1, 128)
        ng = _row(mlstm_norm_g[l])
        cw = _pad_rows(conv_w[l], CONV_PAD)
        cb, lg, lb = _row(conv_b[l]), _row(conv_ln_g[l]), _row(conv_ln_b[l])
        fcw = _pad_rows(ffn_conv_w[l], SUBLANES)
        fcb = _row(ffn_conv_b[l])
        cache_bias, bias0 = _swa_tables(rel_bias)
        aux = jnp.concatenate([bias0[:, None], swa_sinks[l][:, None], jnp.zeros((S_Q_HEADS, 126), F32)], axis=1)

        mkv = _mem_kv(l, mem_prompt, wxkv)
        z = _norm_matmul(l, xp.reshape(bp * seq, d), g1, win, norm=True, tm=1024).reshape(bp, seq, Z_WIDTH)
        hm, cpair, npair, mm = _mlstm_prompt(z, gbias, ng)
        hc, ctail = _conv_prompt(z, cw, cb, lg, lb, tc=512)
        hs, kt, vt = _swa_prompt(z, rel_bias, swa_sinks[l], bucket, qb=min(8, seq // W))
        xp = _mix_out_prompt(l, xp, hm, hc, hs, wout, g2, wxq, mkv, wxo, tm=1024)
        xp, ftail = _ffn_prompt(l, xp, g3, wup, fcw, fcb, wdn, gf, tm=512, final=last)
        half = lambda h: slice(HEAD_DIM * (h % 2), HEAD_DIM * (h % 2 + 1))
        pm_c.append(jnp.stack([jnp.swapaxes(cpair[:, h // 2, half(h), half(h)], 1, 2) for h in range(M_HEADS)], axis=1))
        pm_n.append(jnp.stack([npair[:, h // 2, h % 2, half(h)] for h in range(M_HEADS)], axis=1))
        pm_m.append(mm[:, :, 0, 0])
        p_conv.append(ctail[:, CONV_PAD - nh:, :])
        p_k.append(jnp.transpose(kt.reshape(bp, S_KV_HEADS, HEAD_DIM, W), (0, 3, 1, 2)))
        p_v.append(jnp.transpose(vt.reshape(bp, S_KV_HEADS, HEAD_DIM, W), (0, 3, 1, 2)))
        p_mk.append(jnp.transpose(mkv[:, 0:X_WIDTH, :].reshape(bp, X_HEADS, HEAD_DIM, mem), (0, 3, 1, 2)))
        p_mv.append(jnp.transpose(mkv[:, X_WIDTH:, :].reshape(bp, X_HEADS, HEAD_DIM, mem), (0, 3, 1, 2)))
        p_ffn.append(ftail[:, FFN_PAD - (FFN_KERNEL - 1):, :])

        zs, zst = _sample_in(l, xs, g1, win)
        gb8 = jnp.concatenate([b_i[l], b_f[l]])
        ngt = jnp.broadcast_to(mlstm_norm_g[l][:, None], (M_WIDTH, bs))
        hmt_s, c_new, n_new, m_new = _mlstm_sample(l, zst, gb8, ngt, c_all, n_all, m_all)
        hc_s, conv_new = _conv_sample(l, zs, hist_all, cw, cb, lg, lb)
        hs_s, k_new, v_new = _swa_sample(l, zs, zs.reshape(bs, 1, Z_WIDTH), kc_all, vc_all, cache_bias, aux)
        x1, qx = _mix_out_sample(l, xs, hmt_s, hc_s, hs_s.reshape(bs, S_WIDTH), wout, g2, wxq)
        ox = _xattn_sample(l, qx.reshape(bs, 1, X_WIDTH), mk_all, mv_all)
        xs, ffn_new = _ffn_sample(l, x1, ox.reshape(bs, X_WIDTH), wxo, g3, wup, fcw, fcb, wdn, gf,
                                  state_ffn_conv[l].reshape(bs, (FFN_KERNEL - 1) * dff), final=last)
        s_c.append(c_new)
        s_n.append(n_new)
        s_m.append(m_new)
        s_conv.append(conv_new)
        s_k.append(k_new.reshape(bs, S_KV_HEADS, HEAD_DIM, W))
        s_v.append(v_new.reshape(bs, S_KV_HEADS, HEAD_DIM, W))
        s_ffn.append(ffn_new.reshape(bs, FFN_KERNEL - 1, dff))

    st = jnp.stack
    tr = jnp.transpose
    return (xp, xs.reshape(bs, 1, d),
            st(pm_c), st(pm_n), st(pm_m), st(p_conv), st(p_k), st(p_v), st(p_mk), st(p_mv), st(p_ffn),
            tr(st(s_c), (0, 4, 1, 2, 3)), tr(st(s_n), (0, 3, 1, 2)), tr(st(s_m), (0, 2, 1)),
            tr(st(s_conv), (0, 2, 1, 3)), tr(st(s_k), (0, 1, 4, 2, 3)), tr(st(s_v), (0, 1, 4, 2, 3)),
            st(s_ffn))
```

```python
import functools
import math

import numpy as np
import jax
import jax.numpy as jnp
from jax import lax
from jax.experimental import pallas as pl
from jax.experimental.pallas import tpu as pltpu

F32 = jnp.float32
BF16 = jnp.bfloat16
EPS = 1e-6
NEG_INF = float("-inf")

HEAD_DIM = 64
M_HEADS = 4
M_WIDTH = M_HEADS * HEAD_DIM
C_WIDTH = 256
C_KERNEL = 31
S_Q_HEADS = 8
S_KV_HEADS = 2
S_WIDTH = S_Q_HEADS * HEAD_DIM
S_KV_WIDTH = S_KV_HEADS * HEAD_DIM
WINDOW = 128
N_BUCKETS = 32
MAX_DISTANCE = 128
X_HEADS = 4
X_WIDTH = X_HEADS * HEAD_DIM
FFN_KERNEL = 3
QK_SCALE = HEAD_DIM ** -0.5
LOG2E = math.log2(math.e)

Z_M = 0
Z_C = 4 * M_WIDTH
Z_SQ = Z_C + 2 * C_WIDTH
Z_SKV = Z_SQ + S_WIDTH
Z_G = Z_SKV + 2 * S_KV_WIDTH
Z_WIDTH = Z_G + 128

LANES = 128
SUBLANES = 8
VMEM_LIMIT = 56 * 1024 * 1024

M_CHUNK = 128
FFN_CHUNK = 256


def _params(*sem):
    return pltpu.CompilerParams(dimension_semantics=sem, vmem_limit_bytes=VMEM_LIMIT)


def _const_spec(shape):
    nd = len(shape)
    return pl.BlockSpec(shape, lambda *_: (0,) * nd, pipeline_mode=pl.Buffered(1))


def _weight_spec(w, l):
    nd = w.ndim - 1
    return pl.BlockSpec((None,) + w.shape[1:], lambda *_: (l,) + (0,) * nd, pipeline_mode=pl.Buffered(1))


def _rms(x, g):
    return x * lax.rsqrt(jnp.mean(x * x, axis=-1, keepdims=True) + EPS) * g


def _sigmoid(x):
    return 1.0 / (1.0 + jnp.exp(-x))


def _log_sigmoid(x):
    return jnp.minimum(x, 0.0) - jnp.log1p(jnp.exp(-jnp.abs(x)))


def _dot(a, b):
    return jnp.dot(a, b, preferred_element_type=F32)


def _dot_nt(a, b):
    return lax.dot_general(a, b, (((1,), (1,)), ((), ())), preferred_element_type=F32)


def _dot_tn(a, b):
    return lax.dot_general(a, b, (((0,), (0,)), ((), ())), preferred_element_type=F32)


NORM_SUB = 512


IN_GATES = 4 * M_WIDTH
W_PREP_ROWS = 256


def _in_proj_kernel(x_ref, g_ref, wt_ref, o_ref, wprep_ref, w_scr, *, col_chunk):
    d = x_ref.shape[1]

    @pl.when(pl.program_id(0) == 0)
    def _():
        n_gate = 2 * M_HEADS
        for src, dst, n in ((0, 0, IN_GATES), (IN_GATES + n_gate, IN_GATES, Z_G - IN_GATES)):
            for c in range(0, n, W_PREP_ROWS):
                w_scr[:, dst + c:dst + c + W_PREP_ROWS] = wt_ref[src + c:src + c + W_PREP_ROWS, :].T.astype(BF16)
        gate_rows = jnp.concatenate([wt_ref[IN_GATES:IN_GATES + n_gate, :], jnp.zeros((128 - n_gate, d), F32)], axis=0)
        w_scr[:, Z_G:Z_G + 128] = gate_rows.T.astype(BF16)
        wprep_ref[...] = w_scr[...]

    tm, n = o_ref.shape
    sub = min(NORM_SUB, tm)
    for r0 in range(0, tm, sub):
        h = _rms(x_ref[r0:r0 + sub, :], g_ref[...]).astype(BF16)
        for c in range(0, n, col_chunk):
            w = min(col_chunk, n - c)
            o_ref[r0:r0 + sub, c:c + w] = _dot(h, w_scr[:, c:c + w])


def _in_proj(l, x, g, wt, *, tm):
    m, d = x.shape
    tm = min(tm, m)
    return pl.pallas_call(
        functools.partial(_in_proj_kernel, col_chunk=512),
        grid=(m // tm,),
        in_specs=[pl.BlockSpec((tm, d), lambda i: (i, 0)), _weight_spec(g, l), _weight_spec(wt, l)],
        out_specs=[pl.BlockSpec((tm, Z_WIDTH), lambda i: (i, 0)), pl.BlockSpec((d, Z_WIDTH), lambda i: (0, 0))],
        out_shape=[jax.ShapeDtypeStruct((m, Z_WIDTH), F32), jax.ShapeDtypeStruct((d, Z_WIDTH), BF16)],
        scratch_shapes=[pltpu.VMEM((d, Z_WIDTH), BF16)],
        compiler_params=_params("arbitrary"),
        name="in_proj",
    )(x, g, wt)


def _mem_kv_kernel(x_ref, w_ref, o_ref, acc):
    acc[...] = _dot(x_ref[...].astype(BF16), w_ref[...])
    o_ref[...] = acc[...].T


def _mem_kv(l, mem3, w):
    b, mem, d = mem3.shape
    n = w.shape[2]
    return pl.pallas_call(
        _mem_kv_kernel,
        grid=(b,),
        in_specs=[pl.BlockSpec((None, mem, d), lambda i: (i, 0, 0)), _weight_spec(w, l)],
        out_specs=pl.BlockSpec((None, n, mem), lambda i: (i, 0, 0)),
        out_shape=jax.ShapeDtypeStruct((b, n, mem), F32),
        scratch_shapes=[pltpu.VMEM((mem, n), F32)],
        compiler_params=_params("arbitrary"),
        name="mem_kv",
    )(mem3, w)


def _sample_in_kernel(x_ref, g_ref, w_ref, z_ref, zt_ref):
    h = _rms(x_ref[...], g_ref[...]).astype(BF16)
    n = z_ref.shape[1]
    for c in range(0, n, 512):
        w = min(512, n - c)
        zc = _dot(h, w_ref[:, c:c + w])
        z_ref[:, c:c + w] = zc
        zt_ref[c:c + w, :] = zc.T


def _sample_in(l, x, g, w):
    m, d = x.shape
    n = w.shape[1]
    full = lambda shape: pl.BlockSpec(shape, lambda i: (0,) * len(shape))
    return pl.pallas_call(
        _sample_in_kernel,
        grid=(1,),
        in_specs=[full((m, d)), _weight_spec(g, l), full((d, n))],
        out_specs=[full((m, n)), full((n, m))],
        out_shape=[jax.ShapeDtypeStruct((m, n), F32), jax.ShapeDtypeStruct((n, m), F32)],
        compiler_params=_params("arbitrary"),
        name="sample_in",
    )(x, g, w)


def _mlstm_prompt_kernel(z_ref, g_ref, gb_ref, ng_ref, h_ref, cp_ref, np_ref, m_ref,
                         cp_scr, np_scr, m_scr, ht_scr, st_scr, kw_scr):
    NB, L = z_ref.shape[0], z_ref.shape[1]
    D = HEAD_DIM
    hi = lax.Precision.HIGHEST

    @pl.when(pl.program_id(0) == 0)
    def _():
        cp_scr[...] = jnp.zeros(cp_scr.shape, F32)
        np_scr[...] = jnp.zeros(np_scr.shape, F32)
        m_scr[...] = jnp.zeros(m_scr.shape, F32)

    src = lax.broadcasted_iota(jnp.int32, (L, L), 0)
    qry = lax.broadcasted_iota(jnp.int32, (L, L), 1)
    causal_t = src <= qry
    upper = jnp.where(causal_t, 1.0, 0.0)
    lower = jnp.where(src >= qry, 1.0, 0.0)
    lane_half = lax.broadcasted_iota(jnp.int32, (L, 128), 1) // D
    row8 = lax.broadcasted_iota(jnp.int32, (SUBLANES, 128), 0)

    gates, b_rows, b_cols = [], [], []
    for b in range(NB):
        g = g_ref[b] + gb_ref[...]
        g_t = g.T[0:SUBLANES, :]
        gates.append(g)
        b_rows.append(jnp.dot(_log_sigmoid(g_t), upper, precision=hi, preferred_element_type=F32))
        b_cols.append(jnp.dot(lower, _log_sigmoid(g), precision=hi, preferred_element_type=F32))

    stats = {}
    for b in range(NB):
        for h in range(M_HEADS):
            j, par = h // 2, h % 2
            slab = 128 * j
            own = lane_half == par
            b_row = b_rows[b][M_HEADS + h:M_HEADS + h + 1, :]
            ci_col = gates[b][:, h:h + 1] - b_cols[b][:, M_HEADS + h:M_HEADS + h + 1]
            m_prev = m_scr[b, h, 0:1, 0:1]
            a_row = b_row + m_prev
            dm = jnp.where(causal_t, b_row + ci_col, NEG_INF)
            m_row = jnp.maximum(a_row, jnp.max(dm, axis=0, keepdims=True))
            q2 = z_ref[b, :, slab:slab + 128].astype(BF16)
            k2 = jnp.where(own, z_ref[b, :, M_WIDTH + slab:M_WIDTH + slab + 128] * QK_SCALE, 0.0)
            s_t = _dot_nt(k2.astype(BF16), q2) * jnp.exp(dm - m_row)
            st_scr[b, j, :, L * par:L * (par + 1)] = s_t.astype(BF16)
            b_last = b_row[:, L - 1:L]
            g_col = b_last + ci_col
            m_new = jnp.maximum(b_last + m_prev, jnp.max(g_col, axis=0, keepdims=True))
            kw = k2 * jnp.exp(g_col - m_new)
            if par == 0:
                kw_scr[b, j] = kw
            else:
                kw_scr[b, j] = kw_scr[b, j] + kw
            stats[b, h] = dict(w_inter=jnp.exp(a_row - m_row), floor=jnp.exp(-m_row),
                               den_s=jnp.sum(s_t, axis=0, keepdims=True),
                               decay=jnp.exp(b_last + m_prev - m_new), m_new=m_new,
                               k_sum=jnp.sum(kw, axis=0, keepdims=True))

    block_diag = (lax.broadcasted_iota(jnp.int32, (128, 128), 0) // D
                  == lax.broadcasted_iota(jnp.int32, (128, 128), 1) // D)
    for b in range(NB):
        for j in range(M_HEADS // 2):
            slab = 128 * j
            q2 = z_ref[b, :, slab:slab + 128].astype(BF16)
            v2 = z_ref[b, :, 2 * M_WIDTH + slab:2 * M_WIDTH + slab + 128].astype(BF16)
            cp = cp_scr[b, j]
            npair = np_scr[b, j]
            qc = _dot_nt(cp.astype(BF16), q2)
            qn = _dot_nt(npair.astype(BF16), q2)
            pv = _dot_tn(v2, st_scr[b, j])
            upd = jnp.where(block_diag, _dot_tn(v2, kw_scr[b, j].astype(BF16)), 0.0)
            n_new = jnp.zeros((SUBLANES, 128), F32)
            for par in range(2):
                h = 2 * j + par
                st = stats[b, h]
                rows = slice(D * par, D * (par + 1))
                num = st["w_inter"] * qc[rows, :] + pv[rows, L * par:L * (par + 1)]
                den = st["w_inter"] * qn[par:par + 1, :] + st["den_s"]
                hh = num * (1.0 / jnp.maximum(jnp.abs(den), st["floor"]))
                hh = hh * lax.rsqrt(jnp.mean(hh * hh, axis=0, keepdims=True) + EPS)
                ht_scr[b, D * h:D * (h + 1), :] = hh
                cp_scr[b, j, rows, :] = st["decay"] * cp[rows, :] + upd[rows, :]
                n_new = jnp.where(row8 == par, st["decay"] * npair[par:par + 1, :] + st["k_sum"], n_new)
                m_scr[b, h] = jnp.broadcast_to(st["m_new"], m_scr.shape[2:])
            np_scr[b, j] = n_new
        o_gate = _sigmoid(z_ref[b, :, 3 * M_WIDTH:4 * M_WIDTH])
        h_ref[b] = ht_scr[b].T * ng_ref[...] * o_gate

    cp_ref[...] = cp_scr[...]
    np_ref[...] = np_scr[...]
    m_ref[...] = m_scr[...]


def _mlstm_prompt(l, z3, gbias, ng):
    b, s, _ = z3.shape
    L = M_CHUNK
    P = M_HEADS // 2
    return pl.pallas_call(
        _mlstm_prompt_kernel,
        grid=(s // L,),
        in_specs=[
            pl.BlockSpec((b, L, 4 * M_WIDTH), lambda c: (0, c, Z_M // (4 * M_WIDTH))),
            pl.BlockSpec((b, L, 128), lambda c: (0, c, Z_G // 128)),
            _weight_spec(gbias, l),
            _weight_spec(ng, l),
        ],
        out_specs=[
            pl.BlockSpec((b, L, M_WIDTH), lambda c: (0, c, 0)),
            pl.BlockSpec((b, P, 128, 128), lambda c: (0, 0, 0, 0)),
            pl.BlockSpec((b, P, SUBLANES, 128), lambda c: (0, 0, 0, 0)),
            pl.BlockSpec((b, M_HEADS, SUBLANES, 128), lambda c: (0, 0, 0, 0)),
        ],
        out_shape=[
            jax.ShapeDtypeStruct((b, s, M_WIDTH), F32),
            jax.ShapeDtypeStruct((b, P, 128, 128), F32),
            jax.ShapeDtypeStruct((b, P, SUBLANES, 128), F32),
            jax.ShapeDtypeStruct((b, M_HEADS, SUBLANES, 128), F32),
        ],
        scratch_shapes=[
            pltpu.VMEM((b, P, 128, 128), F32),
            pltpu.VMEM((b, P, SUBLANES, 128), F32),
            pltpu.VMEM((b, M_HEADS, SUBLANES, 128), F32),
            pltpu.VMEM((b, M_WIDTH, L), F32),
            pltpu.VMEM((b, P, L, 2 * L), BF16),
            pltpu.VMEM((b, P, L, 128), F32),
        ],
        compiler_params=_params("arbitrary"),
        name="mlstm_prompt",
    )(z3, z3, gbias, ng)


CONV_PAD = 32


def _conv_prompt_kernel(z_ref, w_ref, cb_ref, lg_ref, lb_ref, h_ref, tail_ref, buf, shifted):
    tc = z_ref.shape[0]
    t = pl.program_id(1)

    @pl.when(t == 0)
    def _():
        buf[0:CONV_PAD, :] = jnp.zeros((CONV_PAD, C_WIDTH), F32)

    @pl.when(t > 0)
    def _():
        buf[0:CONV_PAD, :] = buf[tc:tc + CONV_PAD, :]

    u = z_ref[:, 0:C_WIDTH] * _sigmoid(z_ref[:, C_WIDTH:2 * C_WIDTH])
    buf[CONV_PAD:CONV_PAD + tc, :] = u
    n_sh = shifted.shape[1]
    for r in range(1, SUBLANES):
        shifted[r - 1] = buf[r:r + n_sh, :]
    off = CONV_PAD - (C_KERNEL - 1)
    acc = jnp.broadcast_to(cb_ref[...], (tc, C_WIDTH))
    for j in range(C_KERNEL):
        r = (off + j) % SUBLANES
        base = off + j - r
        win = buf[base:base + tc, :] if r == 0 else shifted[r - 1, base:base + tc, :]
        acc = acc + w_ref[j:j + 1, :] * win
    mu = jnp.mean(acc, axis=-1, keepdims=True)
    xc = acc - mu
    y = xc * lax.rsqrt(jnp.mean(xc * xc, axis=-1, keepdims=True) + EPS) * lg_ref[...] + lb_ref[...]
    h_ref[...] = y * _sigmoid(y)
    tail_ref[...] = buf[tc:tc + CONV_PAD, :]


def _conv_prompt(l, z3, w, cb, lg, lb, *, tc):
    b, s, _ = z3.shape
    return pl.pallas_call(
        _conv_prompt_kernel,
        grid=(b, s // tc),
        in_specs=[
            pl.BlockSpec((None, tc, 2 * C_WIDTH), lambda i, t: (i, t, Z_C // (2 * C_WIDTH))),
            _weight_spec(w, l), _weight_spec(cb, l), _weight_spec(lg, l), _weight_spec(lb, l),
        ],
        out_specs=[
            pl.BlockSpec((None, tc, C_WIDTH), lambda i, t: (i, t, 0)),
            pl.BlockSpec((None, CONV_PAD, C_WIDTH), lambda i, t: (i, 0, 0)),
        ],
        out_shape=[
            jax.ShapeDtypeStruct((b, s, C_WIDTH), F32),
            jax.ShapeDtypeStruct((b, CONV_PAD, C_WIDTH), F32),
        ],
        scratch_shapes=[pltpu.VMEM((CONV_PAD + tc, C_WIDTH), F32),
                        pltpu.VMEM((SUBLANES - 1, CONV_PAD + tc - SUBLANES, C_WIDTH), F32)],
        compiler_params=_params("arbitrary", "arbitrary"),
        name="conv_prompt",
    )(z3, w, cb, lg, lb)


def _swa_prompt_kernel(rb_ref, sink_ref, bucket_ref, q_ref, kv_ref, kvp_ref, o_ref, kt_ref, vt_ref,
                       bias_scr, s_scr, p_scr, ot_scr, *, layer):
    W = WINDOW
    H = S_Q_HEADS
    G = H // S_KV_HEADS
    i = pl.program_id(0)
    n = pl.program_id(1)

    @pl.when(jnp.logical_and(i == 0, n == 0))
    def _():
        bucket = bucket_ref[...]
        prev_key = lax.broadcasted_iota(jnp.int32, (2 * W, W), 0) < W
        for h in range(H):
            acc = jnp.full((2 * W, W), NEG_INF, F32)
            for b in range(N_BUCKETS):
                acc = jnp.where(bucket == b, rb_ref[b, h], acc)
            acc = acc * LOG2E
            bias_scr[0, h] = acc
            bias_scr[1, h] = jnp.where(prev_key, NEG_INF, acc)

    QB = q_ref.shape[0] // W
    NK = (QB + 1) * W
    first = jnp.where(n == 0, 1, 0)
    kk = jnp.concatenate([kvp_ref[:, 0:S_KV_WIDTH], kv_ref[:, 0:S_KV_WIDTH]], axis=0) * (QK_SCALE * LOG2E)
    vv = jnp.concatenate([kvp_ref[:, S_KV_WIDTH:2 * S_KV_WIDTH], kv_ref[:, S_KV_WIDTH:2 * S_KV_WIDTH]], axis=0)
    kk_r = pltpu.roll(kk, HEAD_DIM, axis=1)
    lo_lane = lax.broadcasted_iota(jnp.int32, (NK, S_KV_WIDTH), 1) < HEAD_DIM
    k_var = [[jnp.where(lo_lane, kk, 0.0).astype(BF16), jnp.where(lo_lane, 0.0, kk_r).astype(BF16)],
             [jnp.where(lo_lane, kk_r, 0.0).astype(BF16), jnp.where(lo_lane, 0.0, kk).astype(BF16)]]
    v_t = vv.T.astype(BF16)

    for j in range(QB):
        k0 = j * W
        masked = first if j == 0 else 0
        m_rows = [None] * H
        for hk in range(S_KV_HEADS):
            c0 = 2 * 128 * hk
            q_st = jnp.concatenate([q_ref[k0:k0 + W, c0:c0 + 128], q_ref[k0:k0 + W, c0 + 128:c0 + 256]],
                                   axis=0).astype(BF16)
            for half in range(2):
                s_t = _dot_nt(k_var[hk][half][k0:k0 + 2 * W, :], q_st)
                for slab in range(2):
                    head = G * hk + 2 * slab + half
                    sb = s_t[:, 128 * slab:128 * (slab + 1)] + bias_scr[masked, head]
                    s_scr[j, head] = sb
                    m_rows[head] = jnp.maximum(jnp.max(sb, axis=0, keepdims=True), sink_ref[layer, head] * LOG2E)

        inv = [None] * H
        for head in range(H):
            e = jnp.exp2(s_scr[j, head] - m_rows[head])
            den = jnp.sum(e, axis=0, keepdims=True) + jnp.exp2(sink_ref[layer, head] * LOG2E - m_rows[head])
            inv[head] = 1.0 / den
            p_scr[j, :, 128 * head:128 * (head + 1)] = e.astype(BF16)

        for hk in range(S_KV_HEADS):
            o_t = _dot(v_t[HEAD_DIM * hk:HEAD_DIM * (hk + 1), k0:k0 + 2 * W],
                       p_scr[j, :, 128 * G * hk:128 * G * (hk + 1)])
            for g in range(G):
                head = G * hk + g
                ot_scr[j, HEAD_DIM * head:HEAD_DIM * (head + 1), :] = o_t[:, 128 * g:128 * (g + 1)] * inv[head]
        o_ref[k0:k0 + W, :] = ot_scr[j].T

    @pl.when(n == pl.num_programs(1) - 1)
    def _():
        kt_ref[...] = kv_ref[(QB - 1) * W:QB * W, 0:S_KV_WIDTH].T
        vt_ref[...] = kv_ref[(QB - 1) * W:QB * W, S_KV_WIDTH:2 * S_KV_WIDTH].T


def _swa_prompt(l, z3, rel_bias, sinks, bucket, *, qb):
    b, s, _ = z3.shape
    W = WINDOW
    smem = pl.BlockSpec(memory_space=pltpu.SMEM)
    return pl.pallas_call(
        functools.partial(_swa_prompt_kernel, layer=l),
        grid=(b, s // (qb * W)),
        in_specs=[
            smem, smem, _const_spec((2 * W, W)),
            pl.BlockSpec((None, qb * W, S_WIDTH), lambda i, n: (i, n, Z_SQ // S_WIDTH)),
            pl.BlockSpec((None, qb * W, 2 * S_KV_WIDTH), lambda i, n: (i, n, Z_SKV // (2 * S_KV_WIDTH))),
            pl.BlockSpec((None, W, 2 * S_KV_WIDTH),
                         lambda i, n: (i, jnp.maximum(n * qb - 1, 0), Z_SKV // (2 * S_KV_WIDTH))),
        ],
        out_specs=[
            pl.BlockSpec((None, qb * W, S_WIDTH), lambda i, n: (i, n, 0)),
            pl.BlockSpec((None, S_KV_WIDTH, W), lambda i, n: (i, 0, 0)),
            pl.BlockSpec((None, S_KV_WIDTH, W), lambda i, n: (i, 0, 0)),
        ],
        out_shape=[
            jax.ShapeDtypeStruct((b, s, S_WIDTH), F32),
            jax.ShapeDtypeStruct((b, S_KV_WIDTH, W), F32),
            jax.ShapeDtypeStruct((b, S_KV_WIDTH, W), F32),
        ],
        scratch_shapes=[
            pltpu.VMEM((2, S_Q_HEADS, 2 * W, W), F32),
            pltpu.VMEM((qb, S_Q_HEADS, 2 * W, W), F32),
            pltpu.VMEM((qb, 2 * W, S_Q_HEADS * W), BF16),
            pltpu.VMEM((qb, S_WIDTH, W), F32),
        ],
        compiler_params=_params("arbitrary", "arbitrary"),
        name="swa_prompt",
    )(rel_bias, sinks, bucket, z3, z3, z3)


MIX_SUB = 512


def _mix_out_prompt_kernel(x_ref, hm_ref, hc_ref, hs_ref, wout_ref, g2_ref, wq_ref, mk_ref, mv_ref, wo_ref, o_ref,
                           x1_scr, qx_scr, p_scr):
    tm = x_ref.shape[0]
    mem = mk_ref.shape[1]
    mk_t = mk_ref[...]
    mv_t = mv_ref[...]
    row_head = lax.broadcasted_iota(jnp.int32, mk_t.shape, 0) // HEAD_DIM
    k_heads = [jnp.where(row_head == h, mk_t, 0.0).astype(BF16) for h in range(X_HEADS)]
    v_cat = jnp.concatenate([jnp.where(row_head == h, mv_t, 0.0).astype(BF16) for h in range(X_HEADS)], axis=1)
    subs = [slice(r0, r0 + MIX_SUB) for r0 in range(0, tm, MIX_SUB)]
    for rows in subs:
        cat = jnp.concatenate([hm_ref[rows, :].astype(BF16), hc_ref[rows, :].astype(BF16),
                               hs_ref[rows, :].astype(BF16)], axis=1)
        x1 = x_ref[rows, :] + _dot(cat, wout_ref[...])
        x1_scr[rows, :] = x1
        qx = _dot(_rms(x1, g2_ref[...]).astype(BF16), wq_ref[...])
        qx_scr[rows, :] = (qx * (QK_SCALE * LOG2E)).astype(BF16)
    for rows in subs:
        qx = qx_scr[rows, :]
        for h in range(X_HEADS):
            s = _dot(qx, k_heads[h])
            e = jnp.exp2(s - jnp.max(s, axis=1, keepdims=True))
            p_scr[rows, mem * h:mem * (h + 1)] = (e * (1.0 / jnp.sum(e, axis=1, keepdims=True))).astype(BF16)
    for rows in subs:
        o = _dot_nt(p_scr[rows, :], v_cat)
        o_ref[rows, :] = x1_scr[rows, :] + _dot(o.astype(BF16), wo_ref[...])


def _mix_out_prompt(l, x3, hm, hc, hs, wout, g2, wq, mkv, wo, *, tm):
    b, s, d = x3.shape
    tm = min(tm, s)
    mem = mkv.shape[2]
    row = lambda w: pl.BlockSpec((None, tm, w), lambda i, t: (i, t, 0))
    return pl.pallas_call(
        _mix_out_prompt_kernel,
        grid=(b, s // tm),
        in_specs=[
            row(d), row(M_WIDTH), row(C_WIDTH), row(S_WIDTH),
            _weight_spec(wout, l), _weight_spec(g2, l), _weight_spec(wq, l),
            pl.BlockSpec((None, X_WIDTH, mem), lambda i, t: (i, 0, 0)),
            pl.BlockSpec((None, X_WIDTH, mem), lambda i, t: (i, 1, 0)),
            _weight_spec(wo, l),
        ],
        out_specs=row(d),
        out_shape=jax.ShapeDtypeStruct((b, s, d), F32),
        scratch_shapes=[pltpu.VMEM((tm, d), F32), pltpu.VMEM((tm, X_WIDTH), BF16),
                        pltpu.VMEM((tm, X_HEADS * mem), BF16)],
        compiler_params=_params("arbitrary", "arbitrary"),
        name="mix_out_prompt",
    )(x3, hm, hc, hs, wout, g2, wq, mkv, mkv, wo)


FFN_PAD = 8


FFN_SUB = 512


def _ffn_prompt_kernel(x_ref, g3_ref, wup_ref, cw_ref, cb_ref, wdn_ref, gf_ref, o_ref, tail_ref,
                       gbuf, carry, act, *, final):
    tm = x_ref.shape[0]
    dff = wdn_ref.shape[0]
    sub = min(FFN_SUB, tm)

    @pl.when(pl.program_id(1) == 0)
    def _():
        carry[...] = jnp.zeros(carry.shape, F32)

    for s in range(tm // sub):
        rows = slice(s * sub, (s + 1) * sub)
        x = x_ref[rows, :]
        h = _rms(x, g3_ref[...]).astype(BF16)
        for c in range(0, dff, FFN_CHUNK):
            a = _dot(h, wup_ref[:, c:c + FFN_CHUNK])
            g = _dot(h, wup_ref[:, dff + c:dff + c + FFN_CHUNK])
            gbuf[s, 0:FFN_PAD, :] = carry[:, c:c + FFN_CHUNK]
            gbuf[s, FFN_PAD:FFN_PAD + sub, :] = g
            carry[:, c:c + FFN_CHUNK] = g[sub - FFN_PAD:sub, :]
            gc = (cw_ref[0:1, c:c + FFN_CHUNK] * gbuf[s, FFN_PAD - 2:FFN_PAD - 2 + sub, :]
                  + cw_ref[1:2, c:c + FFN_CHUNK] * gbuf[s, FFN_PAD - 1:FFN_PAD - 1 + sub, :]
                  + cw_ref[2:3, c:c + FFN_CHUNK] * g + cb_ref[:, c:c + FFN_CHUNK])
            act[s, :, c:c + FFN_CHUNK] = (gc * _sigmoid(gc) * a).astype(BF16)
        y = x + _dot(act[s], wdn_ref[...])
        o_ref[rows, :] = _rms(y, gf_ref[...]) if final else y
    tail_ref[...] = carry[...]


def _ffn_prompt(l, x3, g3, wup, cw, cb, wdn, gf, *, tm, final):
    b, s, d = x3.shape
    tm = min(tm, s)
    dff = wdn.shape[1]
    sub = min(FFN_SUB, tm)
    return pl.pallas_call(
        functools.partial(_ffn_prompt_kernel, final=final),
        grid=(b, s // tm),
        in_specs=[
            pl.BlockSpec((None, tm, d), lambda i, t: (i, t, 0)),
            _weight_spec(g3, l), _weight_spec(wup, l), _weight_spec(cw, l), _weight_spec(cb, l),
            _weight_spec(wdn, l), _const_spec((1, d)),
        ],
        out_specs=[
            pl.BlockSpec((None, tm, d), lambda i, t: (i, t, 0)),
            pl.BlockSpec((None, FFN_PAD, dff), lambda i, t: (i, 0, 0)),
        ],
        out_shape=[jax.ShapeDtypeStruct((b, s, d), F32), jax.ShapeDtypeStruct((b, FFN_PAD, dff), F32)],
        scratch_shapes=[
            pltpu.VMEM((tm // sub, FFN_PAD + sub, FFN_CHUNK), F32),
            pltpu.VMEM((FFN_PAD, dff), F32),
            pltpu.VMEM((tm // sub, sub, dff), BF16),
        ],
        compiler_params=_params("arbitrary", "arbitrary"),
        name="ffn_prompt",
    )(x3, g3, wup, cw, cb, wdn, gf)


SWA_SAMPLE_BLOCK = 16
XATTN_SAMPLE_BLOCK = 8
CONV_SAMPLE_BLOCK = 32


def _mlstm_sample_kernel(gb_ref, q_ref, k_ref, v_ref, o_ref, g_ref, ng_ref, c_ref, n_ref, m_ref,
                         h_ref, co_ref, no_ref, mo_ref, kw_scr, *, layer):
    h = pl.program_id(0)
    i_pre = g_ref[pl.ds(h, 1), :] + gb_ref[layer, h]
    f_pre = g_ref[pl.ds(M_HEADS + h, 1), :] + gb_ref[layer, M_HEADS + h]
    a = _log_sigmoid(f_pre) + m_ref[pl.ds(h, 1), :]
    m_t = jnp.maximum(a, i_pre)
    w_old = jnp.exp(a - m_t)
    w_new = jnp.exp(i_pre - m_t)
    q = q_ref[...]
    k = k_ref[...] * QK_SCALE
    v = v_ref[...]
    n_old = n_ref[...]
    kw_scr[...] = k * w_new

    def body(d, acc):
        c_old = c_ref[d]
        co_ref[d] = w_old * c_old + kw_scr[pl.ds(d, 1), :] * v
        return acc + q_ref[pl.ds(d, 1), :] * c_old

    qc = lax.fori_loop(0, HEAD_DIM, body, jnp.zeros(v.shape, F32), unroll=8)
    s = jnp.sum(q * k, axis=0, keepdims=True) * w_new
    num = w_old * qc + s * v
    den = w_old * jnp.sum(q * n_old, axis=0, keepdims=True) + s
    hh = num / jnp.maximum(jnp.abs(den), jnp.exp(-m_t))
    hh = hh * lax.rsqrt(jnp.mean(hh * hh, axis=0, keepdims=True) + EPS) * ng_ref[...]
    h_ref[...] = hh * _sigmoid(o_ref[...])
    no_ref[...] = w_old * n_old + kw_scr[...]
    mo_ref[pl.ds(h, 1), :] = m_t


def _mlstm_sample(l, zt, gb, ngt, c_all, n_all, m_all):
    bsz = zt.shape[1]
    D = HEAD_DIM
    feat = lambda off: pl.BlockSpec((D, bsz), lambda h: (off // D + h, 0))
    return pl.pallas_call(
        functools.partial(_mlstm_sample_kernel, layer=l),
        grid=(M_HEADS,),
        in_specs=[
            pl.BlockSpec(memory_space=pltpu.SMEM),
            feat(Z_M), feat(Z_M + M_WIDTH), feat(Z_M + 2 * M_WIDTH), feat(Z_M + 3 * M_WIDTH),
            pl.BlockSpec((SUBLANES, bsz), lambda h: (Z_G // SUBLANES, 0)),
            pl.BlockSpec((None, D, bsz), lambda h: (l, h, 0)),
            pl.BlockSpec((None, None, D, D, bsz), lambda h: (l, h, 0, 0, 0)),
            pl.BlockSpec((None, None, D, bsz), lambda h: (l, h, 0, 0)),
            pl.BlockSpec((None, M_HEADS, bsz), lambda h: (l, 0, 0)),
        ],
        out_specs=[
            pl.BlockSpec((D, bsz), lambda h: (h, 0)),
            pl.BlockSpec((None, D, D, bsz), lambda h: (h, 0, 0, 0)),
            pl.BlockSpec((None, D, bsz), lambda h: (h, 0, 0)),
            pl.BlockSpec((M_HEADS, bsz), lambda h: (0, 0)),
        ],
        out_shape=[
            jax.ShapeDtypeStruct((M_WIDTH, bsz), F32),
            jax.ShapeDtypeStruct((M_HEADS, D, D, bsz), F32),
            jax.ShapeDtypeStruct((M_HEADS, D, bsz), F32),
            jax.ShapeDtypeStruct((M_HEADS, bsz), F32),
        ],
        scratch_shapes=[pltpu.VMEM((D, bsz), F32)],
        compiler_params=_params("arbitrary"),
        name="mlstm_sample",
    )(gb, zt, zt, zt, zt, zt, ngt, c_all, n_all, m_all)


def _conv_sample_kernel(z_ref, hist_ref, w_ref, cb_ref, lg_ref, lb_ref, h_ref, hist_out_ref):
    nh = C_KERNEL - 1
    u = z_ref[:, 0:C_WIDTH] * _sigmoid(z_ref[:, C_WIDTH:2 * C_WIDTH])
    acc = cb_ref[...] + w_ref[nh:nh + 1, :] * u
    for j in range(nh):
        acc = acc + w_ref[j:j + 1, :] * hist_ref[j]
    mu = jnp.mean(acc, axis=-1, keepdims=True)
    xc = acc - mu
    y = xc * lax.rsqrt(jnp.mean(xc * xc, axis=-1, keepdims=True) + EPS) * lg_ref[...] + lb_ref[...]
    h_ref[...] = y * _sigmoid(y)
    for j in range(nh - 1):
        hist_out_ref[j] = hist_ref[j + 1]
    hist_out_ref[nh - 1] = u


def _conv_sample(l, z, hist_all, w, cb, lg, lb):
    bsz = z.shape[0]
    nh = hist_all.shape[1]
    R = min(CONV_SAMPLE_BLOCK, bsz)
    return pl.pallas_call(
        _conv_sample_kernel,
        grid=(bsz // R,),
        in_specs=[
            pl.BlockSpec((R, 2 * C_WIDTH), lambda i: (i, Z_C // (2 * C_WIDTH))),
            pl.BlockSpec((None, nh, R, C_WIDTH), lambda i: (l, 0, i, 0)),
            _weight_spec(w, l), _weight_spec(cb, l), _weight_spec(lg, l), _weight_spec(lb, l),
        ],
        out_specs=[pl.BlockSpec((R, C_WIDTH), lambda i: (i, 0)), pl.BlockSpec((nh, R, C_WIDTH), lambda i: (0, i, 0))],
        out_shape=[jax.ShapeDtypeStruct((bsz, C_WIDTH), F32), jax.ShapeDtypeStruct((nh, bsz, C_WIDTH), F32)],
        compiler_params=_params("arbitrary"),
        name="conv_sample",
    )(z, hist_all, w, cb, lg, lb)


def _swa_sample_kernel(q_ref, kvn_ref, kv2_ref, kc_ref, vc_ref, bias_ref, aux_ref, o_ref, ko_ref, vo_ref):
    R = q_ref.shape[0]
    W = kc_ref.shape[2]
    H = S_Q_HEADS
    shape = (R, H, 128)
    row = lax.broadcasted_iota(jnp.int32, shape, 1)
    lane_half = lax.broadcasted_iota(jnp.int32, shape, 2) // HEAD_DIM
    q_half = row % 2
    kv_head = row // (H // S_KV_HEADS)
    qs = jnp.zeros(shape, F32)
    for j in range(H // 2):
        qs = jnp.where(row // 2 == j, q_ref[:, :, 128 * j:128 * (j + 1)], qs)
    q8 = jnp.where(lane_half == kv_head, jnp.where(q_half == kv_head, qs, pltpu.roll(qs, HEAD_DIM, axis=2)), 0.0)
    k_new = kvn_ref[:, :, 0:S_KV_WIDTH]
    v_new = kvn_ref[:, :, S_KV_WIDTH:2 * S_KV_WIDTH]
    s = jnp.einsum("bqd,bdk->bqk", q8.astype(BF16), kc_ref[...].astype(BF16), preferred_element_type=F32) * QK_SCALE
    s = s + bias_ref[...][None]
    s_new = jnp.sum(q8 * k_new, axis=2, keepdims=True) * QK_SCALE + aux_ref[:, 0:1][None]
    sink = aux_ref[:, 1:2][None]
    m = jnp.maximum(jnp.maximum(jnp.max(s, axis=2, keepdims=True), s_new), sink)
    e = jnp.exp(s - m)
    e_new = jnp.exp(s_new - m)
    inv = 1.0 / (jnp.sum(e, axis=2, keepdims=True) + e_new + jnp.exp(sink - m))
    o8 = jnp.einsum("bqk,bdk->bqd", e.astype(BF16), vc_ref[...].astype(BF16), preferred_element_type=F32)
    o8 = (o8 + e_new * v_new) * inv
    o8 = jnp.where(lane_half == q_half, jnp.where(q_half == kv_head, o8, pltpu.roll(o8, HEAD_DIM, axis=2)), 0.0)
    for j in range(H // 2):
        o_ref[:, :, 128 * j:128 * (j + 1)] = jnp.sum(jnp.where(row // 2 == j, o8, 0.0), axis=1, keepdims=True)
    k_cols = kv2_ref[:, 0:S_KV_WIDTH].T
    v_cols = kv2_ref[:, S_KV_WIDTH:2 * S_KV_WIDTH].T
    last = lax.broadcasted_iota(jnp.int32, (S_KV_WIDTH, W), 1) == W - 1
    for r in range(R):
        ko_ref[r] = jnp.where(last, k_cols[:, r:r + 1], pltpu.roll(kc_ref[r], W - 1, axis=1))
        vo_ref[r] = jnp.where(last, v_cols[:, r:r + 1], pltpu.roll(vc_ref[r], W - 1, axis=1))


def _swa_sample(l, z, z3, kc_all, vc_all, bias, aux):
    bsz = z.shape[0]
    R = min(SWA_SAMPLE_BLOCK, bsz)
    W = kc_all.shape[3]
    cache_in = pl.BlockSpec((None, R, S_KV_WIDTH, W), lambda i: (l, i, 0, 0))
    cache_out = pl.BlockSpec((R, S_KV_WIDTH, W), lambda i: (i, 0, 0))
    return pl.pallas_call(
        _swa_sample_kernel,
        grid=(bsz // R,),
        in_specs=[
            pl.BlockSpec((R, 1, S_WIDTH), lambda i: (i, 0, Z_SQ // S_WIDTH)),
            pl.BlockSpec((R, 1, 2 * S_KV_WIDTH), lambda i: (i, 0, Z_SKV // (2 * S_KV_WIDTH))),
            pl.BlockSpec((R, 2 * S_KV_WIDTH), lambda i: (i, Z_SKV // (2 * S_KV_WIDTH))),
            cache_in, cache_in,
            _const_spec((S_Q_HEADS, 128)), _weight_spec(aux, l),
        ],
        out_specs=[pl.BlockSpec((R, 1, S_WIDTH), lambda i: (i, 0, 0)), cache_out, cache_out],
        out_shape=[
            jax.ShapeDtypeStruct((bsz, 1, S_WIDTH), F32),
            jax.ShapeDtypeStruct((bsz, S_KV_WIDTH, W), F32),
            jax.ShapeDtypeStruct((bsz, S_KV_WIDTH, W), F32),
        ],
        compiler_params=_params("arbitrary"),
        name="swa_sample",
    )(z3, z3, z, kc_all, vc_all, bias, aux)


def _mix_out_sample_kernel(x_ref, hmt_ref, hc_ref, hs_ref, wout_ref, g2_ref, wq_ref, x1_ref, q_ref):
    cat = jnp.concatenate([hmt_ref[...].T.astype(BF16), hc_ref[...].astype(BF16), hs_ref[...].astype(BF16)], axis=1)
    x1 = x_ref[...] + _dot(cat, wout_ref[...])
    x1_ref[...] = x1
    q_ref[...] = _dot(_rms(x1, g2_ref[...]).astype(BF16), wq_ref[...])


def _mix_out_sample(l, x, hm, hc, hs, wout, g2, wq):
    bsz, d = x.shape
    full = lambda a: pl.BlockSpec(a.shape, lambda i: (0,) * a.ndim)
    args = (x, hm, hc, hs, wout, g2, wq)
    return pl.pallas_call(
        _mix_out_sample_kernel,
        grid=(1,),
        in_specs=[full(x), full(hm), full(hc), full(hs), _weight_spec(wout, l), _weight_spec(g2, l),
                  _weight_spec(wq, l)],
        out_specs=[pl.BlockSpec((bsz, d), lambda i: (0, 0)), pl.BlockSpec((bsz, X_WIDTH), lambda i: (0, 0))],
        out_shape=[jax.ShapeDtypeStruct((bsz, d), F32), jax.ShapeDtypeStruct((bsz, X_WIDTH), F32)],
        compiler_params=_params("arbitrary"),
        name="mix_out_sample",
    )(*args)


def _xattn_sample_kernel(q_ref, k_ref, v_ref, o_ref):
    R = k_ref.shape[0]
    shape = (R, SUBLANES, X_WIDTH)
    row = lax.broadcasted_iota(jnp.int32, shape, 1)
    lane_head = lax.broadcasted_iota(jnp.int32, shape, 2) // HEAD_DIM
    own = row == lane_head
    q8 = jnp.where(own, jnp.broadcast_to(q_ref[...], shape), 0.0).astype(BF16)
    s = jnp.einsum("bqd,bdk->bqk", q8, k_ref[...].astype(BF16), preferred_element_type=F32) * QK_SCALE
    e = jnp.exp(s - jnp.max(s, axis=2, keepdims=True))
    p = (e / jnp.sum(e, axis=2, keepdims=True)).astype(BF16)
    o8 = jnp.einsum("bqk,bdk->bqd", p, v_ref[...].astype(BF16), preferred_element_type=F32)
    o_ref[...] = jnp.sum(jnp.where(own, o8, 0.0), axis=1, keepdims=True)


def _xattn_sample(l, q3, k_all, v_all):
    _, bsz, w, mem = k_all.shape
    R = min(XATTN_SAMPLE_BLOCK, bsz)
    kv = pl.BlockSpec((None, R, w, mem), lambda i: (l, i, 0, 0))
    qo = pl.BlockSpec((R, 1, w), lambda i: (i, 0, 0))
    return pl.pallas_call(
        _xattn_sample_kernel,
        grid=(bsz // R,),
        in_specs=[qo, kv, kv],
        out_specs=qo,
        out_shape=jax.ShapeDtypeStruct((bsz, 1, w), F32),
        compiler_params=_params("arbitrary"),
        name="xattn_sample",
    )(q3, k_all, v_all)


def _ffn_sample_kernel(x1_ref, ox_ref, wo_ref, g3_ref, wup_ref, cw_ref, cb_ref, wdn_ref, gf_ref, hist_ref,
                       o_ref, hist_out_ref, act, *, final):
    dff = wdn_ref.shape[0]
    x = x1_ref[...] + _dot(ox_ref[...].astype(BF16), wo_ref[...])
    h = _rms(x, g3_ref[...]).astype(BF16)
    tiles = FFN_CHUNK // LANES
    for c in range(0, dff, FFN_CHUNK):
        a = _dot(h, wup_ref[:, c:c + FFN_CHUNK])
        g = _dot(h, wup_ref[:, dff + c:dff + c + FFN_CHUNK])
        lo = [2 * (c + LANES * t) for t in range(tiles)]
        h0 = jnp.concatenate([hist_ref[:, o:o + LANES] for o in lo], axis=1)
        h1 = jnp.concatenate([hist_ref[:, o + LANES:o + 2 * LANES] for o in lo], axis=1)
        gc = (cw_ref[0:1, c:c + FFN_CHUNK] * h0 + cw_ref[1:2, c:c + FFN_CHUNK] * h1
              + cw_ref[2:3, c:c + FFN_CHUNK] * g + cb_ref[:, c:c + FFN_CHUNK])
        act[:, c:c + FFN_CHUNK] = (gc * _sigmoid(gc) * a).astype(BF16)
        for t, o in enumerate(lo):
            hist_out_ref[:, o:o + LANES] = h1[:, LANES * t:LANES * (t + 1)]
            hist_out_ref[:, o + LANES:o + 2 * LANES] = g[:, LANES * t:LANES * (t + 1)]
    y = x + _dot(act[...], wdn_ref[...])
    o_ref[...] = _rms(y, gf_ref[...]) if final else y


def _ffn_sample(l, x1, ox, wo, g3, wup, cw, cb, wdn, gf, hist_all, *, final):
    bsz, d = x1.shape
    dff = wdn.shape[1]
    full = lambda a: pl.BlockSpec(a.shape, lambda i: (0,) * a.ndim)
    hshape = hist_all.shape[1:]
    args = (x1, ox, wo, g3, wup, cw, cb, wdn, gf, hist_all)
    return pl.pallas_call(
        functools.partial(_ffn_sample_kernel, final=final),
        grid=(1,),
        in_specs=[full(x1), full(ox), _weight_spec(wo, l), _weight_spec(g3, l), _weight_spec(wup, l),
                  _weight_spec(cw, l), _weight_spec(cb, l),
                  _weight_spec(wdn, l), full(gf), pl.BlockSpec((None,) + hshape, lambda i: (l, 0, 0))],
        out_specs=[pl.BlockSpec((bsz, d), lambda i: (0, 0)), pl.BlockSpec(hshape, lambda i: (0, 0))],
        out_shape=[jax.ShapeDtypeStruct((bsz, d), F32), jax.ShapeDtypeStruct(hshape, F32)],
        scratch_shapes=[pltpu.VMEM((bsz, dff), BF16)],
        compiler_params=_params("arbitrary"),
        name="ffn_sample",
    )(*args)


def _t5_buckets(dist):
    n = np.maximum(dist, 0)
    max_exact = N_BUCKETS // 2
    nf = np.maximum(n, max_exact).astype(np.float32)
    large = max_exact + (np.log(nf / np.float32(max_exact)) / np.float32(math.log(MAX_DISTANCE / max_exact))
                         * np.float32(N_BUCKETS - max_exact)).astype(np.int32)
    return np.where(n < max_exact, n, np.minimum(large, N_BUCKETS - 1))


def _prompt_buckets():
    W = WINDOW
    dist = np.arange(W)[None, :] + W - np.arange(2 * W)[:, None]
    band = (dist >= 0) & (dist < W)
    return np.where(band, _t5_buckets(dist), -1).astype(np.int32)


def _swa_tables(rel_bias):
    W = WINDOW
    dist_c = W - np.arange(W)
    tab = jnp.transpose(rel_bias[_t5_buckets(dist_c)], (1, 0))
    cache_bias = jnp.where((dist_c < W)[None], tab, NEG_INF)
    return cache_bias, rel_bias[0]


def _pad_rows(a, rows):
    return jnp.concatenate([a, jnp.zeros((rows - a.shape[0],) + a.shape[1:], a.dtype)], axis=0)


def _row(a):
    return a.reshape(1, -1)


def kernel(x_prompt, x_sample, mem_prompt, state_mlstm_C, state_mlstm_n, state_mlstm_m, state_conv, cache_swa_k, cache_swa_v, cache_mem_k, cache_mem_v, state_ffn_conv, rel_bias, norm1_g, w_in, b_i, b_f, mlstm_norm_g, conv_w, conv_b, conv_ln_g, conv_ln_b, swa_sinks, w_out, norm2_g, w_xq, w_xk, w_xv, w_xo, norm3_g, w_up, ffn_conv_w, ffn_conv_b, w_down, final_norm_g):
    depth = w_in.shape[0]
    bp, seq, d = x_prompt.shape
    bs = x_sample.shape[0]
    mem = mem_prompt.shape[1]
    dff = w_down.shape[1]
    W = WINDOW
    nh = C_KERNEL - 1

    xp = x_prompt
    xs = x_sample.reshape(bs, d)
    gf = _row(final_norm_g)
    c_all = jnp.transpose(state_mlstm_C, (0, 2, 3, 4, 1))
    n_all = jnp.transpose(state_mlstm_n, (0, 2, 3, 1))
    m_all = jnp.transpose(state_mlstm_m, (0, 2, 1))
    hist_all = jnp.transpose(state_conv, (0, 2, 1, 3))
    kc_all = jnp.transpose(cache_swa_k, (0, 1, 3, 4, 2)).reshape(depth, bs, S_KV_WIDTH, W)
    vc_all = jnp.transpose(cache_swa_v, (0, 1, 3, 4, 2)).reshape(depth, bs, S_KV_WIDTH, W)
    mk_all = jnp.transpose(cache_mem_k, (0, 1, 3, 4, 2)).reshape(depth, bs, X_WIDTH, mem)
    mv_all = jnp.transpose(cache_mem_v, (0, 1, 3, 4, 2)).reshape(depth, bs, X_WIDTH, mem)
    fh_all = jnp.transpose(state_ffn_conv.reshape(depth, bs, FFN_KERNEL - 1, dff // LANES, LANES),
                           (0, 1, 3, 2, 4)).reshape(depth, bs, (FFN_KERNEL - 1) * dff)
    pm_c, pm_n, pm_m, p_conv, p_k, p_v, p_mk, p_mv, p_ffn = ([] for _ in range(9))
    s_c, s_n, s_m, s_conv, s_k, s_v, s_ffn = ([] for _ in range(7))

    win_t = jnp.swapaxes(w_in, 1, 2)
    wout = w_out.astype(BF16)
    wxq = w_xq.astype(BF16)
    wxkv = jnp.concatenate([w_xk, w_xv], axis=2).astype(BF16)
    wxo = w_xo.astype(BF16)
    wup = w_up.astype(BF16)
    wdn = w_down.astype(BF16)
    bucket = jnp.asarray(_prompt_buckets())

    rows = lambda a: a.reshape(depth, 1, -1)
    g1, g2, g3 = rows(norm1_g), rows(norm2_g), rows(norm3_g)
    gb8 = jnp.concatenate([b_i, b_f], axis=1)
    gbias = rows(jnp.concatenate([gb8, jnp.zeros((depth, 128 - 2 * M_HEADS), F32)], axis=1))
    ng = rows(mlstm_norm_g)
    ngt = jnp.broadcast_to(mlstm_norm_g[:, :, None], (depth, M_WIDTH, bs))
    cw = jnp.concatenate([conv_w, jnp.zeros((depth, CONV_PAD - C_KERNEL, C_WIDTH), F32)], axis=1)
    cb, lg, lb = rows(conv_b), rows(conv_ln_g), rows(conv_ln_b)
    fcw = jnp.concatenate([ffn_conv_w, jnp.zeros((depth, SUBLANES - FFN_KERNEL, dff), F32)], axis=1)
    fcb = rows(ffn_conv_b)
    cache_bias, bias0 = _swa_tables(rel_bias)
    aux = jnp.concatenate([jnp.broadcast_to(bias0[None, :, None], (depth, S_Q_HEADS, 1)), swa_sinks[:, :, None],
                           jnp.zeros((depth, S_Q_HEADS, 126), F32)], axis=2)

    for l in range(depth):
        last = l == depth - 1

        mkv = _mem_kv(l, mem_prompt, wxkv)
        z, win = _in_proj(l, xp.reshape(bp * seq, d), g1, win_t, tm=1024)
        z = z.reshape(bp, seq, Z_WIDTH)
        hm, cpair, npair, mm = _mlstm_prompt(l, z, gbias, ng)
        hc, ctail = _conv_prompt(l, z, cw, cb, lg, lb, tc=512)
        hs, kt, vt = _swa_prompt(l, z, rel_bias, swa_sinks, bucket, qb=min(8, seq // W))
        xp = _mix_out_prompt(l, xp, hm, hc, hs, wout, g2, wxq, mkv, wxo, tm=1024)
        xp, ftail = _ffn_prompt(l, xp, g3, wup, fcw, fcb, wdn, gf, tm=512, final=last)
        half = lambda h: slice(HEAD_DIM * (h % 2), HEAD_DIM * (h % 2 + 1))
        pm_c.append(jnp.stack([jnp.swapaxes(cpair[:, h // 2, half(h), half(h)], 1, 2) for h in range(M_HEADS)], axis=1))
        pm_n.append(jnp.stack([npair[:, h // 2, h % 2, half(h)] for h in range(M_HEADS)], axis=1))
        pm_m.append(mm[:, :, 0, 0])
        p_conv.append(ctail[:, CONV_PAD - nh:, :])
        p_k.append(jnp.transpose(kt.reshape(bp, S_KV_HEADS, HEAD_DIM, W), (0, 3, 1, 2)))
        p_v.append(jnp.transpose(vt.reshape(bp, S_KV_HEADS, HEAD_DIM, W), (0, 3, 1, 2)))
        p_mk.append(jnp.transpose(mkv[:, 0:X_WIDTH, :].reshape(bp, X_HEADS, HEAD_DIM, mem), (0, 3, 1, 2)))
        p_mv.append(jnp.transpose(mkv[:, X_WIDTH:, :].reshape(bp, X_HEADS, HEAD_DIM, mem), (0, 3, 1, 2)))
        p_ffn.append(ftail[:, FFN_PAD - (FFN_KERNEL - 1):, :])

        zs, zst = _sample_in(l, xs, g1, win)
        hmt_s, c_new, n_new, m_new = _mlstm_sample(l, zst, gb8, ngt, c_all, n_all, m_all)
        hc_s, conv_new = _conv_sample(l, zs, hist_all, cw, cb, lg, lb)
        hs_s, k_new, v_new = _swa_sample(l, zs, zs.reshape(bs, 1, Z_WIDTH), kc_all, vc_all, cache_bias, aux)
        x1, qx = _mix_out_sample(l, xs, hmt_s, hc_s, hs_s.reshape(bs, S_WIDTH), wout, g2, wxq)
        ox = _xattn_sample(l, qx.reshape(bs, 1, X_WIDTH), mk_all, mv_all)
        xs, ffn_new = _ffn_sample(l, x1, ox.reshape(bs, X_WIDTH), wxo, g3, wup, fcw, fcb, wdn, gf,
                                  fh_all, final=last)
        s_c.append(c_new)
        s_n.append(n_new)
        s_m.append(m_new)
        s_conv.append(conv_new)
        s_k.append(k_new.reshape(bs, S_KV_HEADS, HEAD_DIM, W))
        s_v.append(v_new.reshape(bs, S_KV_HEADS, HEAD_DIM, W))
        s_ffn.append(jnp.transpose(ffn_new.reshape(bs, dff // LANES, FFN_KERNEL - 1, LANES), (0, 2, 1, 3))
                     .reshape(bs, FFN_KERNEL - 1, dff))

    st = jnp.stack
    tr = jnp.transpose
    return (xp, xs.reshape(bs, 1, d),
            st(pm_c), st(pm_n), st(pm_m), st(p_conv), st(p_k), st(p_v), st(p_mk), st(p_mv), st(p_ffn),
            tr(st(s_c), (0, 4, 1, 2, 3)), tr(st(s_n), (0, 3, 1, 2)), tr(st(s_m), (0, 2, 1)),
            tr(st(s_conv), (0, 2, 1, 3)), tr(st(s_k), (0, 1, 4, 2, 3)), tr(st(s_v), (0, 1, 4, 2, 3)),
            st(s_ffn))
```

```python
import functools
import math

import numpy as np
import jax
import jax.numpy as jnp
from jax import lax
from jax.experimental import pallas as pl
from jax.experimental.pallas import tpu as pltpu

F32 = jnp.float32
BF16 = jnp.bfloat16
EPS = 1e-6
NEG_INF = float("-inf")

HEAD_DIM = 64
M_HEADS = 4
M_WIDTH = M_HEADS * HEAD_DIM
C_WIDTH = 256
C_KERNEL = 31
S_Q_HEADS = 8
S_KV_HEADS = 2
S_WIDTH = S_Q_HEADS * HEAD_DIM
S_KV_WIDTH = S_KV_HEADS * HEAD_DIM
WINDOW = 128
N_BUCKETS = 32
MAX_DISTANCE = 128
X_HEADS = 4
X_WIDTH = X_HEADS * HEAD_DIM
FFN_KERNEL = 3
QK_SCALE = HEAD_DIM ** -0.5
LOG2E = math.log2(math.e)

Z_M = 0
Z_C = 4 * M_WIDTH
Z_SQ = Z_C + 2 * C_WIDTH
Z_SKV = Z_SQ + S_WIDTH
Z_G = Z_SKV + 2 * S_KV_WIDTH
Z_WIDTH = Z_G + 128

LANES = 128
SUBLANES = 8
VMEM_LIMIT = 56 * 1024 * 1024

M_CHUNK = 128
FFN_CHUNK = 256


def _params(*sem):
    return pltpu.CompilerParams(dimension_semantics=sem, vmem_limit_bytes=VMEM_LIMIT)


def _const_spec(shape):
    nd = len(shape)
    return pl.BlockSpec(shape, lambda *_: (0,) * nd, pipeline_mode=pl.Buffered(1))


def _weight_spec(w, l):
    nd = w.ndim - 1
    return pl.BlockSpec((None,) + w.shape[1:], lambda *_: (l,) + (0,) * nd, pipeline_mode=pl.Buffered(1))


def _rms(x, g):
    return x * lax.rsqrt(jnp.mean(x * x, axis=-1, keepdims=True) + EPS) * g


def _sigmoid(x):
    return 1.0 / (1.0 + jnp.exp(-x))


def _log_sigmoid(x):
    return jnp.minimum(x, 0.0) - jnp.log1p(jnp.exp(-jnp.abs(x)))


def _dot(a, b):
    return jnp.dot(a, b, preferred_element_type=F32)


def _dot_nt(a, b):
    return lax.dot_general(a, b, (((1,), (1,)), ((), ())), preferred_element_type=F32)


def _dot_tn(a, b):
    return lax.dot_general(a, b, (((0,), (0,)), ((), ())), preferred_element_type=F32)


NORM_SUB = 512


IN_GATES = 4 * M_WIDTH
W_PREP_ROWS = 256


def _in_proj_kernel(x_ref, g_ref, wt_ref, o_ref, wprep_ref, w_scr, *, col_chunk):
    d = x_ref.shape[1]

    @pl.when(pl.program_id(0) == 0)
    def _():
        n_gate = 2 * M_HEADS
        for src, dst, n in ((0, 0, IN_GATES), (IN_GATES + n_gate, IN_GATES, Z_G - IN_GATES)):
            for c in range(0, n, W_PREP_ROWS):
                w_scr[:, dst + c:dst + c + W_PREP_ROWS] = wt_ref[src + c:src + c + W_PREP_ROWS, :].T.astype(BF16)
        gate_rows = jnp.concatenate([wt_ref[IN_GATES:IN_GATES + n_gate, :], jnp.zeros((128 - n_gate, d), F32)], axis=0)
        w_scr[:, Z_G:Z_G + 128] = gate_rows.T.astype(BF16)
        wprep_ref[...] = w_scr[...]

    tm, n = o_ref.shape
    sub = min(NORM_SUB, tm)
    for r0 in range(0, tm, sub):
        h = _rms(x_ref[r0:r0 + sub, :], g_ref[...]).astype(BF16)
        for c in range(0, n, col_chunk):
            w = min(col_chunk, n - c)
            o_ref[r0:r0 + sub, c:c + w] = _dot(h, w_scr[:, c:c + w])


def _in_proj(l, x, g, wt, *, tm):
    m, d = x.shape
    tm = min(tm, m)
    return pl.pallas_call(
        functools.partial(_in_proj_kernel, col_chunk=512),
        grid=(m // tm,),
        in_specs=[pl.BlockSpec((tm, d), lambda i: (i, 0)), _weight_spec(g, l), _weight_spec(wt, l)],
        out_specs=[pl.BlockSpec((tm, Z_WIDTH), lambda i: (i, 0)), pl.BlockSpec((d, Z_WIDTH), lambda i: (0, 0))],
        out_shape=[jax.ShapeDtypeStruct((m, Z_WIDTH), F32), jax.ShapeDtypeStruct((d, Z_WIDTH), BF16)],
        scratch_shapes=[pltpu.VMEM((d, Z_WIDTH), BF16)],
        compiler_params=_params("arbitrary"),
        name="in_proj",
    )(x, g, wt)


def _mem_kv_kernel(x_ref, w_ref, o_ref, acc):
    acc[...] = _dot(x_ref[...].astype(BF16), w_ref[...])
    o_ref[...] = acc[...].T


def _mem_kv(l, mem3, w):
    b, mem, d = mem3.shape
    n = w.shape[2]
    return pl.pallas_call(
        _mem_kv_kernel,
        grid=(b,),
        in_specs=[pl.BlockSpec((None, mem, d), lambda i: (i, 0, 0)), _weight_spec(w, l)],
        out_specs=pl.BlockSpec((None, n, mem), lambda i: (i, 0, 0)),
        out_shape=jax.ShapeDtypeStruct((b, n, mem), F32),
        scratch_shapes=[pltpu.VMEM((mem, n), F32)],
        compiler_params=_params("arbitrary"),
        name="mem_kv",
    )(mem3, w)


def _sample_in_kernel(x_ref, g_ref, w_ref, z_ref, zt_ref):
    h = _rms(x_ref[...], g_ref[...]).astype(BF16)
    n = z_ref.shape[1]
    for c in range(0, n, 512):
        w = min(512, n - c)
        zc = _dot(h, w_ref[:, c:c + w])
        z_ref[:, c:c + w] = zc
        zt_ref[c:c + w, :] = zc.T


def _sample_in(l, x, g, w):
    m, d = x.shape
    n = w.shape[1]
    full = lambda shape: pl.BlockSpec(shape, lambda i: (0,) * len(shape))
    return pl.pallas_call(
        _sample_in_kernel,
        grid=(1,),
        in_specs=[full((m, d)), _weight_spec(g, l), full((d, n))],
        out_specs=[full((m, n)), full((n, m))],
        out_shape=[jax.ShapeDtypeStruct((m, n), F32), jax.ShapeDtypeStruct((n, m), F32)],
        compiler_params=_params("arbitrary"),
        name="sample_in",
    )(x, g, w)


def _mlstm_prompt_kernel(z_ref, g_ref, gb_ref, ng_ref, h_ref, cp_ref, np_ref, m_ref,
                         cp_scr, np_scr, m_scr, ht_scr, st_scr, kw_scr):
    NB, L = z_ref.shape[0], z_ref.shape[1]
    D = HEAD_DIM
    hi = lax.Precision.HIGHEST

    @pl.when(pl.program_id(0) == 0)
    def _():
        cp_scr[...] = jnp.zeros(cp_scr.shape, F32)
        np_scr[...] = jnp.zeros(np_scr.shape, F32)
        m_scr[...] = jnp.zeros(m_scr.shape, F32)

    src = lax.broadcasted_iota(jnp.int32, (L, L), 0)
    qry = lax.broadcasted_iota(jnp.int32, (L, L), 1)
    causal_t = src <= qry
    upper = jnp.where(causal_t, 1.0, 0.0)
    lane_half = lax.broadcasted_iota(jnp.int32, (L, 128), 1) // D
    row8 = lax.broadcasted_iota(jnp.int32, (SUBLANES, 128), 0)

    rows, cols = {}, []
    for b in range(NB):
        g_t = (g_ref[b] + gb_ref[...]).T[0:SUBLANES, :]
        b_rows = jnp.dot(_log_sigmoid(g_t), upper, precision=hi, preferred_element_type=F32)
        to_cols = []
        for h in range(M_HEADS):
            b_row = b_rows[M_HEADS + h:M_HEADS + h + 1, :]
            ci_row = g_t[h:h + 1, :] - b_row
            m_prev = m_scr[b, h, 0:1, 0:1]
            b_last = b_row[:, L - 1:L]
            m_new = jnp.maximum(b_last + m_prev, jnp.max(b_last + ci_row, axis=1, keepdims=True))
            rows[b, h] = dict(b_row=b_row, m_prev=m_prev, m_new=m_new, decay=jnp.exp(b_last + m_prev - m_new))
            to_cols += [ci_row, jnp.exp(b_last + ci_row - m_new)]
        to_cols.append(jnp.zeros((128 - 2 * M_HEADS, L), F32))
        cols.append(jnp.concatenate(to_cols, axis=0).T)

    stats = {}
    for b in range(NB):
        for h in range(M_HEADS):
            j, par = h // 2, h % 2
            slab = 128 * j
            own = lane_half == par
            r = rows[b, h]
            ci_col = cols[b][:, 2 * h:2 * h + 1]
            wk_col = cols[b][:, 2 * h + 1:2 * h + 2]
            a_row = r["b_row"] + r["m_prev"]
            dm = jnp.where(causal_t, r["b_row"] + ci_col, NEG_INF)
            m_row = jnp.maximum(a_row, jnp.max(dm, axis=0, keepdims=True))
            q2 = z_ref[b, :, slab:slab + 128].astype(BF16)
            k2 = jnp.where(own, z_ref[b, :, M_WIDTH + slab:M_WIDTH + slab + 128] * QK_SCALE, 0.0)
            s_t = _dot_nt(k2.astype(BF16), q2) * jnp.exp(dm - m_row)
            st_scr[b, j, :, L * par:L * (par + 1)] = s_t.astype(BF16)
            kw = k2 * wk_col
            if par == 0:
                kw_scr[b, j] = kw
            else:
                kw_scr[b, j] = kw_scr[b, j] + kw
            stats[b, h] = dict(w_inter=jnp.exp(a_row - m_row), floor=jnp.exp(-m_row),
                               den_s=jnp.sum(s_t, axis=0, keepdims=True),
                               decay=r["decay"], m_new=r["m_new"],
                               k_sum=jnp.sum(kw, axis=0, keepdims=True))

    block_diag = (lax.broadcasted_iota(jnp.int32, (128, 128), 0) // D
                  == lax.broadcasted_iota(jnp.int32, (128, 128), 1) // D)
    for b in range(NB):
        for j in range(M_HEADS // 2):
            slab = 128 * j
            q2 = z_ref[b, :, slab:slab + 128].astype(BF16)
            v2 = z_ref[b, :, 2 * M_WIDTH + slab:2 * M_WIDTH + slab + 128].astype(BF16)
            cp = cp_scr[b, j]
            npair = np_scr[b, j]
            qc = _dot_nt(cp.astype(BF16), q2)
            qn = _dot_nt(npair.astype(BF16), q2)
            pv = _dot_tn(v2, st_scr[b, j])
            upd = jnp.where(block_diag, _dot_tn(v2, kw_scr[b, j].astype(BF16)), 0.0)
            n_new = jnp.zeros((SUBLANES, 128), F32)
            for par in range(2):
                h = 2 * j + par
                st = stats[b, h]
                rows = slice(D * par, D * (par + 1))
                num = st["w_inter"] * qc[rows, :] + pv[rows, L * par:L * (par + 1)]
                den = st["w_inter"] * qn[par:par + 1, :] + st["den_s"]
                hh = num * (1.0 / jnp.maximum(jnp.abs(den), st["floor"]))
                hh = hh * lax.rsqrt(jnp.mean(hh * hh, axis=0, keepdims=True) + EPS)
                ht_scr[b, D * h:D * (h + 1), :] = hh
                cp_scr[b, j, rows, :] = st["decay"] * cp[rows, :] + upd[rows, :]
                n_new = jnp.where(row8 == par, st["decay"] * npair[par:par + 1, :] + st["k_sum"], n_new)
                m_scr[b, h] = jnp.broadcast_to(st["m_new"], m_scr.shape[2:])
            np_scr[b, j] = n_new
        o_gate = _sigmoid(z_ref[b, :, 3 * M_WIDTH:4 * M_WIDTH])
        h_ref[b] = ht_scr[b].T * ng_ref[...] * o_gate

    cp_ref[...] = cp_scr[...]
    np_ref[...] = np_scr[...]
    m_ref[...] = m_scr[...]


def _mlstm_prompt(l, z3, gbias, ng):
    b, s, _ = z3.shape
    L = M_CHUNK
    P = M_HEADS // 2
    return pl.pallas_call(
        _mlstm_prompt_kernel,
        grid=(s // L,),
        in_specs=[
            pl.BlockSpec((b, L, 4 * M_WIDTH), lambda c: (0, c, Z_M // (4 * M_WIDTH))),
            pl.BlockSpec((b, L, 128), lambda c: (0, c, Z_G // 128)),
            _weight_spec(gbias, l),
            _weight_spec(ng, l),
        ],
        out_specs=[
            pl.BlockSpec((b, L, M_WIDTH), lambda c: (0, c, 0)),
            pl.BlockSpec((b, P, 128, 128), lambda c: (0, 0, 0, 0)),
            pl.BlockSpec((b, P, SUBLANES, 128), lambda c: (0, 0, 0, 0)),
            pl.BlockSpec((b, M_HEADS, SUBLANES, 128), lambda c: (0, 0, 0, 0)),
        ],
        out_shape=[
            jax.ShapeDtypeStruct((b, s, M_WIDTH), F32),
            jax.ShapeDtypeStruct((b, P, 128, 128), F32),
            jax.ShapeDtypeStruct((b, P, SUBLANES, 128), F32),
            jax.ShapeDtypeStruct((b, M_HEADS, SUBLANES, 128), F32),
        ],
        scratch_shapes=[
            pltpu.VMEM((b, P, 128, 128), F32),
            pltpu.VMEM((b, P, SUBLANES, 128), F32),
            pltpu.VMEM((b, M_HEADS, SUBLANES, 128), F32),
            pltpu.VMEM((b, M_WIDTH, L), F32),
            pltpu.VMEM((b, P, L, 2 * L), BF16),
            pltpu.VMEM((b, P, L, 128), F32),
        ],
        compiler_params=_params("arbitrary"),
        name="mlstm_prompt",
    )(z3, z3, gbias, ng)


CONV_PAD = 32


def _conv_prompt_kernel(z_ref, w_ref, cb_ref, lg_ref, lb_ref, h_ref, tail_ref, buf, shifted):
    tc = z_ref.shape[0]
    t = pl.program_id(1)

    @pl.when(t == 0)
    def _():
        buf[0:CONV_PAD, :] = jnp.zeros((CONV_PAD, C_WIDTH), F32)

    @pl.when(t > 0)
    def _():
        buf[0:CONV_PAD, :] = buf[tc:tc + CONV_PAD, :]

    u = z_ref[:, 0:C_WIDTH] * _sigmoid(z_ref[:, C_WIDTH:2 * C_WIDTH])
    buf[CONV_PAD:CONV_PAD + tc, :] = u
    n_sh = shifted.shape[1]
    for r in range(1, SUBLANES):
        shifted[r - 1] = buf[r:r + n_sh, :]
    off = CONV_PAD - (C_KERNEL - 1)
    acc = jnp.broadcast_to(cb_ref[...], (tc, C_WIDTH))
    for j in range(C_KERNEL):
        r = (off + j) % SUBLANES
        base = off + j - r
        win = buf[base:base + tc, :] if r == 0 else shifted[r - 1, base:base + tc, :]
        acc = acc + w_ref[j:j + 1, :] * win
    mu = jnp.mean(acc, axis=-1, keepdims=True)
    xc = acc - mu
    y = xc * lax.rsqrt(jnp.mean(xc * xc, axis=-1, keepdims=True) + EPS) * lg_ref[...] + lb_ref[...]
    h_ref[...] = y * _sigmoid(y)
    tail_ref[...] = buf[tc:tc + CONV_PAD, :]


def _conv_prompt(l, z3, w, cb, lg, lb, *, tc):
    b, s, _ = z3.shape
    return pl.pallas_call(
        _conv_prompt_kernel,
        grid=(b, s // tc),
        in_specs=[
            pl.BlockSpec((None, tc, 2 * C_WIDTH), lambda i, t: (i, t, Z_C // (2 * C_WIDTH))),
            _weight_spec(w, l), _weight_spec(cb, l), _weight_spec(lg, l), _weight_spec(lb, l),
        ],
        out_specs=[
            pl.BlockSpec((None, tc, C_WIDTH), lambda i, t: (i, t, 0)),
            pl.BlockSpec((None, CONV_PAD, C_WIDTH), lambda i, t: (i, 0, 0)),
        ],
        out_shape=[
            jax.ShapeDtypeStruct((b, s, C_WIDTH), F32),
            jax.ShapeDtypeStruct((b, CONV_PAD, C_WIDTH), F32),
        ],
        scratch_shapes=[pltpu.VMEM((CONV_PAD + tc, C_WIDTH), F32),
                        pltpu.VMEM((SUBLANES - 1, CONV_PAD + tc - SUBLANES, C_WIDTH), F32)],
        compiler_params=_params("arbitrary", "arbitrary"),
        name="conv_prompt",
    )(z3, w, cb, lg, lb)


def _swa_prompt_kernel(rb_ref, sink_ref, bucket_ref, q_ref, kv_ref, kvp_ref, o_ref, kt_ref, vt_ref,
                       bias_scr, s_scr, p_scr, ot_scr, *, layer):
    W = WINDOW
    H = S_Q_HEADS
    G = H // S_KV_HEADS
    i = pl.program_id(0)
    n = pl.program_id(1)

    @pl.when(jnp.logical_and(i == 0, n == 0))
    def _():
        bucket = bucket_ref[...]
        prev_key = lax.broadcasted_iota(jnp.int32, (2 * W, W), 0) < W
        for h in range(H):
            acc = jnp.full((2 * W, W), NEG_INF, F32)
            for b in range(N_BUCKETS):
                acc = jnp.where(bucket == b, rb_ref[b, h], acc)
            acc = acc * LOG2E
            bias_scr[0, h] = acc
            bias_scr[1, h] = jnp.where(prev_key, NEG_INF, acc)

    QB = q_ref.shape[0] // W
    NK = (QB + 1) * W
    first = jnp.where(n == 0, 1, 0)
    kk = jnp.concatenate([kvp_ref[:, 0:S_KV_WIDTH], kv_ref[:, 0:S_KV_WIDTH]], axis=0) * (QK_SCALE * LOG2E)
    vv = jnp.concatenate([kvp_ref[:, S_KV_WIDTH:2 * S_KV_WIDTH], kv_ref[:, S_KV_WIDTH:2 * S_KV_WIDTH]], axis=0)
    kk_r = pltpu.roll(kk, HEAD_DIM, axis=1)
    lo_lane = lax.broadcasted_iota(jnp.int32, (NK, S_KV_WIDTH), 1) < HEAD_DIM
    k_var = [[jnp.where(lo_lane, kk, 0.0).astype(BF16), jnp.where(lo_lane, 0.0, kk_r).astype(BF16)],
             [jnp.where(lo_lane, kk_r, 0.0).astype(BF16), jnp.where(lo_lane, 0.0, kk).astype(BF16)]]
    v_t = vv.T.astype(BF16)

    for j in range(QB):
        k0 = j * W
        masked = first if j == 0 else 0
        m_rows = [None] * H
        for hk in range(S_KV_HEADS):
            c0 = 2 * 128 * hk
            q_st = jnp.concatenate([q_ref[k0:k0 + W, c0:c0 + 128], q_ref[k0:k0 + W, c0 + 128:c0 + 256]],
                                   axis=0).astype(BF16)
            for half in range(2):
                s_t = _dot_nt(k_var[hk][half][k0:k0 + 2 * W, :], q_st)
                for slab in range(2):
                    head = G * hk + 2 * slab + half
                    sb = s_t[:, 128 * slab:128 * (slab + 1)] + bias_scr[masked, head]
                    s_scr[j, head] = sb
                    m_rows[head] = jnp.maximum(jnp.max(sb, axis=0, keepdims=True), sink_ref[layer, head] * LOG2E)

        inv = [None] * H
        for head in range(H):
            e = jnp.exp2(s_scr[j, head] - m_rows[head])
            den = jnp.sum(e, axis=0, keepdims=True) + jnp.exp2(sink_ref[layer, head] * LOG2E - m_rows[head])
            inv[head] = 1.0 / den
            p_scr[j, :, 128 * head:128 * (head + 1)] = e.astype(BF16)

        for hk in range(S_KV_HEADS):
            o_t = _dot(v_t[HEAD_DIM * hk:HEAD_DIM * (hk + 1), k0:k0 + 2 * W],
                       p_scr[j, :, 128 * G * hk:128 * G * (hk + 1)])
            for g in range(G):
                head = G * hk + g
                ot_scr[j, HEAD_DIM * head:HEAD_DIM * (head + 1), :] = o_t[:, 128 * g:128 * (g + 1)] * inv[head]
        o_ref[k0:k0 + W, :] = ot_scr[j].T

    @pl.when(n == pl.num_programs(1) - 1)
    def _():
        kt_ref[...] = kv_ref[(QB - 1) * W:QB * W, 0:S_KV_WIDTH].T
        vt_ref[...] = kv_ref[(QB - 1) * W:QB * W, S_KV_WIDTH:2 * S_KV_WIDTH].T


def _swa_prompt(l, z3, rel_bias, sinks, bucket, *, qb):
    b, s, _ = z3.shape
    W = WINDOW
    smem = pl.BlockSpec(memory_space=pltpu.SMEM)
    return pl.pallas_call(
        functools.partial(_swa_prompt_kernel, layer=l),
        grid=(b, s // (qb * W)),
        in_specs=[
            smem, smem, _const_spec((2 * W, W)),
            pl.BlockSpec((None, qb * W, S_WIDTH), lambda i, n: (i, n, Z_SQ // S_WIDTH)),
            pl.BlockSpec((None, qb * W, 2 * S_KV_WIDTH), lambda i, n: (i, n, Z_SKV // (2 * S_KV_WIDTH))),
            pl.BlockSpec((None, W, 2 * S_KV_WIDTH),
                         lambda i, n: (i, jnp.maximum(n * qb - 1, 0), Z_SKV // (2 * S_KV_WIDTH))),
        ],
        out_specs=[
            pl.BlockSpec((None, qb * W, S_WIDTH), lambda i, n: (i, n, 0)),
            pl.BlockSpec((None, S_KV_WIDTH, W), lambda i, n: (i, 0, 0)),
            pl.BlockSpec((None, S_KV_WIDTH, W), lambda i, n: (i, 0, 0)),
        ],
        out_shape=[
            jax.ShapeDtypeStruct((b, s, S_WIDTH), F32),
            jax.ShapeDtypeStruct((b, S_KV_WIDTH, W), F32),
            jax.ShapeDtypeStruct((b, S_KV_WIDTH, W), F32),
        ],
        scratch_shapes=[
            pltpu.VMEM((2, S_Q_HEADS, 2 * W, W), F32),
            pltpu.VMEM((qb, S_Q_HEADS, 2 * W, W), F32),
            pltpu.VMEM((qb, 2 * W, S_Q_HEADS * W), BF16),
            pltpu.VMEM((qb, S_WIDTH, W), F32),
        ],
        compiler_params=_params("arbitrary", "arbitrary"),
        name="swa_prompt",
    )(rel_bias, sinks, bucket, z3, z3, z3)


MIX_SUB = 512


def _mix_out_prompt_kernel(x_ref, hm_ref, hc_ref, hs_ref, wout_ref, g2_ref, wq_ref, mk_ref, mv_ref, wo_ref, o_ref,
                           x1_scr, qx_scr, p_scr):
    tm = x_ref.shape[0]
    mem = mk_ref.shape[1]
    mk_t = mk_ref[...]
    mv_t = mv_ref[...]
    row_head = lax.broadcasted_iota(jnp.int32, mk_t.shape, 0) // HEAD_DIM
    k_heads = [jnp.where(row_head == h, mk_t, 0.0).astype(BF16) for h in range(X_HEADS)]
    v_cat = jnp.concatenate([jnp.where(row_head == h, mv_t, 0.0).astype(BF16) for h in range(X_HEADS)], axis=1)
    subs = [slice(r0, r0 + MIX_SUB) for r0 in range(0, tm, MIX_SUB)]
    for rows in subs:
        cat = jnp.concatenate([hm_ref[rows, :].astype(BF16), hc_ref[rows, :].astype(BF16),
                               hs_ref[rows, :].astype(BF16)], axis=1)
        x1 = x_ref[rows, :] + _dot(cat, wout_ref[...])
        x1_scr[rows, :] = x1
        qx = _dot(_rms(x1, g2_ref[...]).astype(BF16), wq_ref[...])
        qx_scr[rows, :] = (qx * (QK_SCALE * LOG2E)).astype(BF16)
    for rows in subs:
        qx = qx_scr[rows, :]
        for h in range(X_HEADS):
            s = _dot(qx, k_heads[h])
            e = jnp.exp2(s - jnp.max(s, axis=1, keepdims=True))
            p_scr[rows, mem * h:mem * (h + 1)] = (e * (1.0 / jnp.sum(e, axis=1, keepdims=True))).astype(BF16)
    for rows in subs:
        o = _dot_nt(p_scr[rows, :], v_cat)
        o_ref[rows, :] = x1_scr[rows, :] + _dot(o.astype(BF16), wo_ref[...])


def _mix_out_prompt(l, x3, hm, hc, hs, wout, g2, wq, mkv, wo, *, tm):
    b, s, d = x3.shape
    tm = min(tm, s)
    mem = mkv.shape[2]
    row = lambda w: pl.BlockSpec((None, tm, w), lambda i, t: (i, t, 0))
    return pl.pallas_call(
        _mix_out_prompt_kernel,
        grid=(b, s // tm),
        in_specs=[
            row(d), row(M_WIDTH), row(C_WIDTH), row(S_WIDTH),
            _weight_spec(wout, l), _weight_spec(g2, l), _weight_spec(wq, l),
            pl.BlockSpec((None, X_WIDTH, mem), lambda i, t: (i, 0, 0)),
            pl.BlockSpec((None, X_WIDTH, mem), lambda i, t: (i, 1, 0)),
            _weight_spec(wo, l),
        ],
        out_specs=row(d),
        out_shape=jax.ShapeDtypeStruct((b, s, d), F32),
        scratch_shapes=[pltpu.VMEM((tm, d), F32), pltpu.VMEM((tm, X_WIDTH), BF16),
                        pltpu.VMEM((tm, X_HEADS * mem), BF16)],
        compiler_params=_params("arbitrary", "arbitrary"),
        name="mix_out_prompt",
    )(x3, hm, hc, hs, wout, g2, wq, mkv, mkv, wo)


FFN_PAD = 8


FFN_SUB = 512


def _ffn_prompt_kernel(x_ref, g3_ref, wup_ref, cw_ref, cb_ref, wdn_ref, gf_ref, o_ref, tail_ref,
                       gbuf, carry, act, *, final):
    tm = x_ref.shape[0]
    dff = wdn_ref.shape[0]
    sub = min(FFN_SUB, tm)

    @pl.when(pl.program_id(1) == 0)
    def _():
        carry[...] = jnp.zeros(carry.shape, F32)

    for s in range(tm // sub):
        rows = slice(s * sub, (s + 1) * sub)
        x = x_ref[rows, :]
        h = _rms(x, g3_ref[...]).astype(BF16)
        for c in range(0, dff, FFN_CHUNK):
            a = _dot(h, wup_ref[:, c:c + FFN_CHUNK])
            g = _dot(h, wup_ref[:, dff + c:dff + c + FFN_CHUNK])
            gbuf[s, 0:FFN_PAD, :] = carry[:, c:c + FFN_CHUNK]
            gbuf[s, FFN_PAD:FFN_PAD + sub, :] = g
            carry[:, c:c + FFN_CHUNK] = g[sub - FFN_PAD:sub, :]
            gc = (cw_ref[0:1, c:c + FFN_CHUNK] * gbuf[s, FFN_PAD - 2:FFN_PAD - 2 + sub, :]
                  + cw_ref[1:2, c:c + FFN_CHUNK] * gbuf[s, FFN_PAD - 1:FFN_PAD - 1 + sub, :]
                  + cw_ref[2:3, c:c + FFN_CHUNK] * g + cb_ref[:, c:c + FFN_CHUNK])
            act[s, :, c:c + FFN_CHUNK] = (gc * _sigmoid(gc) * a).astype(BF16)
        y = x + _dot(act[s], wdn_ref[...])
        o_ref[rows, :] = _rms(y, gf_ref[...]) if final else y
    tail_ref[...] = carry[...]


def _ffn_prompt(l, x3, g3, wup, cw, cb, wdn, gf, *, tm, final):
    b, s, d = x3.shape
    tm = min(tm, s)
    dff = wdn.shape[1]
    sub = min(FFN_SUB, tm)
    return pl.pallas_call(
        functools.partial(_ffn_prompt_kernel, final=final),
        grid=(b, s // tm),
        in_specs=[
            pl.BlockSpec((None, tm, d), lambda i, t: (i, t, 0)),
            _weight_spec(g3, l), _weight_spec(wup, l), _weight_spec(cw, l), _weight_spec(cb, l),
            _weight_spec(wdn, l), _const_spec((1, d)),
        ],
        out_specs=[
            pl.BlockSpec((None, tm, d), lambda i, t: (i, t, 0)),
            pl.BlockSpec((None, FFN_PAD, dff), lambda i, t: (i, 0, 0)),
        ],
        out_shape=[jax.ShapeDtypeStruct((b, s, d), F32), jax.ShapeDtypeStruct((b, FFN_PAD, dff), F32)],
        scratch_shapes=[
            pltpu.VMEM((tm // sub, FFN_PAD + sub, FFN_CHUNK), F32),
            pltpu.VMEM((FFN_PAD, dff), F32),
            pltpu.VMEM((tm // sub, sub, dff), BF16),
        ],
        compiler_params=_params("arbitrary", "arbitrary"),
        name="ffn_prompt",
    )(x3, g3, wup, cw, cb, wdn, gf)


SWA_SAMPLE_BLOCK = 32
XATTN_SAMPLE_BLOCK = 16
CONV_SAMPLE_BLOCK = 32


def _mlstm_sample_kernel(gb_ref, q_ref, k_ref, v_ref, o_ref, g_ref, ng_ref, c_ref, n_ref, m_ref,
                         h_ref, co_ref, no_ref, mo_ref, kw_scr, *, layer):
    h = pl.program_id(0)
    i_pre = g_ref[pl.ds(h, 1), :] + gb_ref[layer, h]
    f_pre = g_ref[pl.ds(M_HEADS + h, 1), :] + gb_ref[layer, M_HEADS + h]
    a = _log_sigmoid(f_pre) + m_ref[pl.ds(h, 1), :]
    m_t = jnp.maximum(a, i_pre)
    w_old = jnp.exp(a - m_t)
    w_new = jnp.exp(i_pre - m_t)
    q = q_ref[...]
    k = k_ref[...] * QK_SCALE
    v = v_ref[...]
    n_old = n_ref[...]
    kw_scr[...] = k * w_new

    def body(d, acc):
        c_old = c_ref[d]
        co_ref[d] = w_old * c_old + kw_scr[pl.ds(d, 1), :] * v
        return acc + q_ref[pl.ds(d, 1), :] * c_old

    qc = lax.fori_loop(0, HEAD_DIM, body, jnp.zeros(v.shape, F32), unroll=8)
    s = jnp.sum(q * k, axis=0, keepdims=True) * w_new
    num = w_old * qc + s * v
    den = w_old * jnp.sum(q * n_old, axis=0, keepdims=True) + s
    hh = num / jnp.maximum(jnp.abs(den), jnp.exp(-m_t))
    hh = hh * lax.rsqrt(jnp.mean(hh * hh, axis=0, keepdims=True) + EPS) * ng_ref[...]
    h_ref[...] = hh * _sigmoid(o_ref[...])
    no_ref[...] = w_old * n_old + kw_scr[...]
    mo_ref[pl.ds(h, 1), :] = m_t


def _mlstm_sample(l, zt, gb, ngt, c_all, n_all, m_all):
    bsz = zt.shape[1]
    D = HEAD_DIM
    feat = lambda off: pl.BlockSpec((D, bsz), lambda h: (off // D + h, 0))
    return pl.pallas_call(
        functools.partial(_mlstm_sample_kernel, layer=l),
        grid=(M_HEADS,),
        in_specs=[
            pl.BlockSpec(memory_space=pltpu.SMEM),
            feat(Z_M), feat(Z_M + M_WIDTH), feat(Z_M + 2 * M_WIDTH), feat(Z_M + 3 * M_WIDTH),
            pl.BlockSpec((SUBLANES, bsz), lambda h: (Z_G // SUBLANES, 0)),
            pl.BlockSpec((None, D, bsz), lambda h: (l, h, 0)),
            pl.BlockSpec((None, None, D, D, bsz), lambda h: (l, h, 0, 0, 0)),
            pl.BlockSpec((None, None, D, bsz), lambda h: (l, h, 0, 0)),
            pl.BlockSpec((None, M_HEADS, bsz), lambda h: (l, 0, 0)),
        ],
        out_specs=[
            pl.BlockSpec((D, bsz), lambda h: (h, 0)),
            pl.BlockSpec((None, D, D, bsz), lambda h: (h, 0, 0, 0)),
            pl.BlockSpec((None, D, bsz), lambda h: (h, 0, 0)),
            pl.BlockSpec((M_HEADS, bsz), lambda h: (0, 0)),
        ],
        out_shape=[
            jax.ShapeDtypeStruct((M_WIDTH, bsz), F32),
            jax.ShapeDtypeStruct((M_HEADS, D, D, bsz), F32),
            jax.ShapeDtypeStruct((M_HEADS, D, bsz), F32),
            jax.ShapeDtypeStruct((M_HEADS, bsz), F32),
        ],
        scratch_shapes=[pltpu.VMEM((D, bsz), F32)],
        compiler_params=_params("arbitrary"),
        name="mlstm_sample",
    )(gb, zt, zt, zt, zt, zt, ngt, c_all, n_all, m_all)


def _conv_sample_kernel(z_ref, hist_ref, w_ref, cb_ref, lg_ref, lb_ref, h_ref, hist_out_ref):
    nh = C_KERNEL - 1
    u = z_ref[:, 0:C_WIDTH] * _sigmoid(z_ref[:, C_WIDTH:2 * C_WIDTH])
    acc = cb_ref[...] + w_ref[nh:nh + 1, :] * u
    for j in range(nh):
        acc = acc + w_ref[j:j + 1, :] * hist_ref[j]
    mu = jnp.mean(acc, axis=-1, keepdims=True)
    xc = acc - mu
    y = xc * lax.rsqrt(jnp.mean(xc * xc, axis=-1, keepdims=True) + EPS) * lg_ref[...] + lb_ref[...]
    h_ref[...] = y * _sigmoid(y)
    for j in range(nh - 1):
        hist_out_ref[j] = hist_ref[j + 1]
    hist_out_ref[nh - 1] = u


def _conv_sample(l, z, hist_all, w, cb, lg, lb):
    bsz = z.shape[0]
    nh = hist_all.shape[1]
    R = min(CONV_SAMPLE_BLOCK, bsz)
    return pl.pallas_call(
        _conv_sample_kernel,
        grid=(bsz // R,),
        in_specs=[
            pl.BlockSpec((R, 2 * C_WIDTH), lambda i: (i, Z_C // (2 * C_WIDTH))),
            pl.BlockSpec((None, nh, R, C_WIDTH), lambda i: (l, 0, i, 0)),
            _weight_spec(w, l), _weight_spec(cb, l), _weight_spec(lg, l), _weight_spec(lb, l),
        ],
        out_specs=[pl.BlockSpec((R, C_WIDTH), lambda i: (i, 0)), pl.BlockSpec((nh, R, C_WIDTH), lambda i: (0, i, 0))],
        out_shape=[jax.ShapeDtypeStruct((bsz, C_WIDTH), F32), jax.ShapeDtypeStruct((nh, bsz, C_WIDTH), F32)],
        compiler_params=_params("arbitrary"),
        name="conv_sample",
    )(z, hist_all, w, cb, lg, lb)


def _swa_sample_kernel(q_ref, kvn_ref, kv2_ref, kc_ref, vc_ref, bias_ref, aux_ref, o_ref, ko_ref, vo_ref):
    R = q_ref.shape[0]
    W = kc_ref.shape[2]
    H = S_Q_HEADS
    shape = (R, H, 128)
    row = lax.broadcasted_iota(jnp.int32, shape, 1)
    lane_half = lax.broadcasted_iota(jnp.int32, shape, 2) // HEAD_DIM
    q_half = row % 2
    kv_head = row // (H // S_KV_HEADS)
    qs = jnp.zeros(shape, F32)
    for j in range(H // 2):
        qs = jnp.where(row // 2 == j, q_ref[:, :, 128 * j:128 * (j + 1)], qs)
    q8 = jnp.where(lane_half == kv_head, jnp.where(q_half == kv_head, qs, pltpu.roll(qs, HEAD_DIM, axis=2)), 0.0)
    k_new = kvn_ref[:, :, 0:S_KV_WIDTH]
    v_new = kvn_ref[:, :, S_KV_WIDTH:2 * S_KV_WIDTH]
    s = jnp.einsum("bqd,bdk->bqk", q8.astype(BF16), kc_ref[...].astype(BF16), preferred_element_type=F32) * QK_SCALE
    s = s + bias_ref[...][None]
    s_new = jnp.sum(q8 * k_new, axis=2, keepdims=True) * QK_SCALE + aux_ref[:, 0:1][None]
    sink = aux_ref[:, 1:2][None]
    m = jnp.maximum(jnp.maximum(jnp.max(s, axis=2, keepdims=True), s_new), sink)
    e = jnp.exp(s - m)
    e_new = jnp.exp(s_new - m)
    inv = 1.0 / (jnp.sum(e, axis=2, keepdims=True) + e_new + jnp.exp(sink - m))
    o8 = jnp.einsum("bqk,bdk->bqd", e.astype(BF16), vc_ref[...].astype(BF16), preferred_element_type=F32)
    o8 = (o8 + e_new * v_new) * inv
    o8 = jnp.where(lane_half == q_half, jnp.where(q_half == kv_head, o8, pltpu.roll(o8, HEAD_DIM, axis=2)), 0.0)
    for j in range(H // 2):
        o_ref[:, :, 128 * j:128 * (j + 1)] = jnp.sum(jnp.where(row // 2 == j, o8, 0.0), axis=1, keepdims=True)
    k_cols = kv2_ref[:, 0:S_KV_WIDTH].T
    v_cols = kv2_ref[:, S_KV_WIDTH:2 * S_KV_WIDTH].T
    last = lax.broadcasted_iota(jnp.int32, (S_KV_WIDTH, W), 1) == W - 1
    for r in range(R):
        ko_ref[r] = jnp.where(last, k_cols[:, r:r + 1], pltpu.roll(kc_ref[r], W - 1, axis=1))
        vo_ref[r] = jnp.where(last, v_cols[:, r:r + 1], pltpu.roll(vc_ref[r], W - 1, axis=1))


def _swa_sample(l, z, z3, kc_all, vc_all, bias, aux):
    bsz = z.shape[0]
    R = min(SWA_SAMPLE_BLOCK, bsz)
    W = kc_all.shape[3]
    cache_in = pl.BlockSpec((None, R, S_KV_WIDTH, W), lambda i: (l, i, 0, 0))
    cache_out = pl.BlockSpec((R, S_KV_WIDTH, W), lambda i: (i, 0, 0))
    return pl.pallas_call(
        _swa_sample_kernel,
        grid=(bsz // R,),
        in_specs=[
            pl.BlockSpec((R, 1, S_WIDTH), lambda i: (i, 0, Z_SQ // S_WIDTH)),
            pl.BlockSpec((R, 1, 2 * S_KV_WIDTH), lambda i: (i, 0, Z_SKV // (2 * S_KV_WIDTH))),
            pl.BlockSpec((R, 2 * S_KV_WIDTH), lambda i: (i, Z_SKV // (2 * S_KV_WIDTH))),
            cache_in, cache_in,
            _const_spec((S_Q_HEADS, 128)), _weight_spec(aux, l),
        ],
        out_specs=[pl.BlockSpec((R, 1, S_WIDTH), lambda i: (i, 0, 0)), cache_out, cache_out],
        out_shape=[
            jax.ShapeDtypeStruct((bsz, 1, S_WIDTH), F32),
            jax.ShapeDtypeStruct((bsz, S_KV_WIDTH, W), F32),
            jax.ShapeDtypeStruct((bsz, S_KV_WIDTH, W), F32),
        ],
        compiler_params=_params("arbitrary"),
        name="swa_sample",
    )(z3, z3, z, kc_all, vc_all, bias, aux)


def _mix_out_sample_kernel(x_ref, hmt_ref, hc_ref, hs_ref, wout_ref, g2_ref, wq_ref, x1_ref, q_ref):
    cat = jnp.concatenate([hmt_ref[...].T.astype(BF16), hc_ref[...].astype(BF16), hs_ref[...].astype(BF16)], axis=1)
    x1 = x_ref[...] + _dot(cat, wout_ref[...])
    x1_ref[...] = x1
    q_ref[...] = _dot(_rms(x1, g2_ref[...]).astype(BF16), wq_ref[...])


def _mix_out_sample(l, x, hm, hc, hs, wout, g2, wq):
    bsz, d = x.shape
    full = lambda a: pl.BlockSpec(a.shape, lambda i: (0,) * a.ndim)
    args = (x, hm, hc, hs, wout, g2, wq)
    return pl.pallas_call(
        _mix_out_sample_kernel,
        grid=(1,),
        in_specs=[full(x), full(hm), full(hc), full(hs), _weight_spec(wout, l), _weight_spec(g2, l),
                  _weight_spec(wq, l)],
        out_specs=[pl.BlockSpec((bsz, d), lambda i: (0, 0)), pl.BlockSpec((bsz, X_WIDTH), lambda i: (0, 0))],
        out_shape=[jax.ShapeDtypeStruct((bsz, d), F32), jax.ShapeDtypeStruct((bsz, X_WIDTH), F32)],
        compiler_params=_params("arbitrary"),
        name="mix_out_sample",
    )(*args)


def _xattn_sample_kernel(q_ref, k_ref, v_ref, o_ref):
    R = k_ref.shape[0]
    shape = (R, SUBLANES, X_WIDTH)
    row = lax.broadcasted_iota(jnp.int32, shape, 1)
    lane_head = lax.broadcasted_iota(jnp.int32, shape, 2) // HEAD_DIM
    own = row == lane_head
    q8 = jnp.where(own, jnp.broadcast_to(q_ref[...], shape), 0.0).astype(BF16)
    s = jnp.einsum("bqd,bdk->bqk", q8, k_ref[...].astype(BF16), preferred_element_type=F32) * QK_SCALE
    e = jnp.exp(s - jnp.max(s, axis=2, keepdims=True))
    p = (e / jnp.sum(e, axis=2, keepdims=True)).astype(BF16)
    o8 = jnp.einsum("bqk,bdk->bqd", p, v_ref[...].astype(BF16), preferred_element_type=F32)
    o_ref[...] = jnp.sum(jnp.where(own, o8, 0.0), axis=1, keepdims=True)


def _xattn_sample(l, q3, k_all, v_all):
    _, bsz, w, mem = k_all.shape
    R = min(XATTN_SAMPLE_BLOCK, bsz)
    kv = pl.BlockSpec((None, R, w, mem), lambda i: (l, i, 0, 0))
    qo = pl.BlockSpec((R, 1, w), lambda i: (i, 0, 0))
    return pl.pallas_call(
        _xattn_sample_kernel,
        grid=(bsz // R,),
        in_specs=[qo, kv, kv],
        out_specs=qo,
        out_shape=jax.ShapeDtypeStruct((bsz, 1, w), F32),
        compiler_params=_params("arbitrary"),
        name="xattn_sample",
    )(q3, k_all, v_all)


def _ffn_sample_kernel(x1_ref, ox_ref, wo_ref, g3_ref, wup_ref, cw_ref, cb_ref, wdn_ref, gf_ref, hist_ref,
                       o_ref, hist_out_ref, act, *, final):
    dff = wdn_ref.shape[0]
    x = x1_ref[...] + _dot(ox_ref[...].astype(BF16), wo_ref[...])
    h = _rms(x, g3_ref[...]).astype(BF16)
    tiles = FFN_CHUNK // LANES
    for c in range(0, dff, FFN_CHUNK):
        a = _dot(h, wup_ref[:, c:c + FFN_CHUNK])
        g = _dot(h, wup_ref[:, dff + c:dff + c + FFN_CHUNK])
        lo = [2 * (c + LANES * t) for t in range(tiles)]
        h0 = jnp.concatenate([hist_ref[:, o:o + LANES] for o in lo], axis=1)
        h1 = jnp.concatenate([hist_ref[:, o + LANES:o + 2 * LANES] for o in lo], axis=1)
        gc = (cw_ref[0:1, c:c + FFN_CHUNK] * h0 + cw_ref[1:2, c:c + FFN_CHUNK] * h1
              + cw_ref[2:3, c:c + FFN_CHUNK] * g + cb_ref[:, c:c + FFN_CHUNK])
        act[:, c:c + FFN_CHUNK] = (gc * _sigmoid(gc) * a).astype(BF16)
        for t, o in enumerate(lo):
            hist_out_ref[:, o:o + LANES] = h1[:, LANES * t:LANES * (t + 1)]
            hist_out_ref[:, o + LANES:o + 2 * LANES] = g[:, LANES * t:LANES * (t + 1)]
    y = x + _dot(act[...], wdn_ref[...])
    o_ref[...] = _rms(y, gf_ref[...]) if final else y


def _ffn_sample(l, x1, ox, wo, g3, wup, cw, cb, wdn, gf, hist_all, *, final):
    bsz, d = x1.shape
    dff = wdn.shape[1]
    full = lambda a: pl.BlockSpec(a.shape, lambda i: (0,) * a.ndim)
    hshape = hist_all.shape[1:]
    args = (x1, ox, wo, g3, wup, cw, cb, wdn, gf, hist_all)
    return pl.pallas_call(
        functools.partial(_ffn_sample_kernel, final=final),
        grid=(1,),
        in_specs=[full(x1), full(ox), _weight_spec(wo, l), _weight_spec(g3, l), _weight_spec(wup, l),
                  _weight_spec(cw, l), _weight_spec(cb, l),
                  _weight_spec(wdn, l), full(gf), pl.BlockSpec((None,) + hshape, lambda i: (l, 0, 0))],
        out_specs=[pl.BlockSpec((bsz, d), lambda i: (0, 0)), pl.BlockSpec(hshape, lambda i: (0, 0))],
        out_shape=[jax.ShapeDtypeStruct((bsz, d), F32), jax.ShapeDtypeStruct(hshape, F32)],
        scratch_shapes=[pltpu.VMEM((bsz, dff), BF16)],
        compiler_params=_params("arbitrary"),
        name="ffn_sample",
    )(*args)


def _t5_buckets(dist):
    n = np.maximum(dist, 0)
    max_exact = N_BUCKETS // 2
    nf = np.maximum(n, max_exact).astype(np.float32)
    large = max_exact + (np.log(nf / np.float32(max_exact)) / np.float32(math.log(MAX_DISTANCE / max_exact))
                         * np.float32(N_BUCKETS - max_exact)).astype(np.int32)
    return np.where(n < max_exact, n, np.minimum(large, N_BUCKETS - 1))


def _prompt_buckets():
    W = WINDOW
    dist = np.arange(W)[None, :] + W - np.arange(2 * W)[:, None]
    band = (dist >= 0) & (dist < W)
    return np.where(band, _t5_buckets(dist), -1).astype(np.int32)


def _swa_tables(rel_bias):
    W = WINDOW
    dist_c = W - np.arange(W)
    tab = jnp.transpose(rel_bias[_t5_buckets(dist_c)], (1, 0))
    cache_bias = jnp.where((dist_c < W)[None], tab, NEG_INF)
    return cache_bias, rel_bias[0]


def kernel(x_prompt, x_sample, mem_prompt, state_mlstm_C, state_mlstm_n, state_mlstm_m, state_conv, cache_swa_k, cache_swa_v, cache_mem_k, cache_mem_v, state_ffn_conv, rel_bias, norm1_g, w_in, b_i, b_f, mlstm_norm_g, conv_w, conv_b, conv_ln_g, conv_ln_b, swa_sinks, w_out, norm2_g, w_xq, w_xk, w_xv, w_xo, norm3_g, w_up, ffn_conv_w, ffn_conv_b, w_down, final_norm_g):
    depth = w_in.shape[0]
    bp, seq, d = x_prompt.shape
    bs = x_sample.shape[0]
    mem = mem_prompt.shape[1]
    dff = w_down.shape[1]
    W = WINDOW
    nh = C_KERNEL - 1

    xp = x_prompt
    xs = x_sample.reshape(bs, d)
    gf = final_norm_g.reshape(1, d)
    c_all = jnp.transpose(state_mlstm_C, (0, 2, 3, 4, 1))
    n_all = jnp.transpose(state_mlstm_n, (0, 2, 3, 1))
    m_all = jnp.transpose(state_mlstm_m, (0, 2, 1))
    hist_all = jnp.transpose(state_conv, (0, 2, 1, 3))
    kc_all = jnp.transpose(cache_swa_k, (0, 1, 3, 4, 2)).reshape(depth, bs, S_KV_WIDTH, W)
    vc_all = jnp.transpose(cache_swa_v, (0, 1, 3, 4, 2)).reshape(depth, bs, S_KV_WIDTH, W)
    mk_all = jnp.transpose(cache_mem_k, (0, 1, 3, 4, 2)).reshape(depth, bs, X_WIDTH, mem)
    mv_all = jnp.transpose(cache_mem_v, (0, 1, 3, 4, 2)).reshape(depth, bs, X_WIDTH, mem)
    fh_all = jnp.transpose(state_ffn_conv.reshape(depth, bs, FFN_KERNEL - 1, dff // LANES, LANES),
                           (0, 1, 3, 2, 4)).reshape(depth, bs, (FFN_KERNEL - 1) * dff)
    pm_c, pm_n, pm_m, p_conv, p_k, p_v, p_mk, p_mv, p_ffn = ([] for _ in range(9))
    s_c, s_n, s_m, s_conv, s_k, s_v, s_ffn = ([] for _ in range(7))

    win_t = jnp.swapaxes(w_in, 1, 2)
    wout = w_out.astype(BF16)
    wxq = w_xq.astype(BF16)
    wxkv = jnp.concatenate([w_xk, w_xv], axis=2).astype(BF16)
    wxo = w_xo.astype(BF16)
    wup = w_up.astype(BF16)
    wdn = w_down.astype(BF16)
    bucket = jnp.asarray(_prompt_buckets())

    rows = lambda a: a.reshape(depth, 1, -1)
    g1, g2, g3 = rows(norm1_g), rows(norm2_g), rows(norm3_g)
    gb8 = jnp.concatenate([b_i, b_f], axis=1)
    gbias = rows(jnp.concatenate([gb8, jnp.zeros((depth, 128 - 2 * M_HEADS), F32)], axis=1))
    ng = rows(mlstm_norm_g)
    ngt = jnp.broadcast_to(mlstm_norm_g[:, :, None], (depth, M_WIDTH, bs))
    cw = jnp.concatenate([conv_w, jnp.zeros((depth, CONV_PAD - C_KERNEL, C_WIDTH), F32)], axis=1)
    cb, lg, lb = rows(conv_b), rows(conv_ln_g), rows(conv_ln_b)
    fcw = jnp.concatenate([ffn_conv_w, jnp.zeros((depth, SUBLANES - FFN_KERNEL, dff), F32)], axis=1)
    fcb = rows(ffn_conv_b)
    cache_bias, bias0 = _swa_tables(rel_bias)
    aux = jnp.concatenate([jnp.broadcast_to(bias0[None, :, None], (depth, S_Q_HEADS, 1)), swa_sinks[:, :, None],
                           jnp.zeros((depth, S_Q_HEADS, 126), F32)], axis=2)

    for l in range(depth):
        last = l == depth - 1

        mkv = _mem_kv(l, mem_prompt, wxkv)
        z, win = _in_proj(l, xp.reshape(bp * seq, d), g1, win_t, tm=1024)
        z = z.reshape(bp, seq, Z_WIDTH)
        hm, cpair, npair, mm = _mlstm_prompt(l, z, gbias, ng)
        hc, ctail = _conv_prompt(l, z, cw, cb, lg, lb, tc=min(1024, seq))
        hs, kt, vt = _swa_prompt(l, z, rel_bias, swa_sinks, bucket, qb=min(8, seq // W))
        xp = _mix_out_prompt(l, xp, hm, hc, hs, wout, g2, wxq, mkv, wxo, tm=1024)
        xp, ftail = _ffn_prompt(l, xp, g3, wup, fcw, fcb, wdn, gf, tm=512, final=last)
        half = lambda h: slice(HEAD_DIM * (h % 2), HEAD_DIM * (h % 2 + 1))
        pm_c.append(jnp.stack([jnp.swapaxes(cpair[:, h // 2, half(h), half(h)], 1, 2) for h in range(M_HEADS)], axis=1))
        pm_n.append(jnp.stack([npair[:, h // 2, h % 2, half(h)] for h in range(M_HEADS)], axis=1))
        pm_m.append(mm[:, :, 0, 0])
        p_conv.append(ctail[:, CONV_PAD - nh:, :])
        p_k.append(jnp.transpose(kt.reshape(bp, S_KV_HEADS, HEAD_DIM, W), (0, 3, 1, 2)))
        p_v.append(jnp.transpose(vt.reshape(bp, S_KV_HEADS, HEAD_DIM, W), (0, 3, 1, 2)))
        p_mk.append(jnp.transpose(mkv[:, 0:X_WIDTH, :].reshape(bp, X_HEADS, HEAD_DIM, mem), (0, 3, 1, 2)))
        p_mv.append(jnp.transpose(mkv[:, X_WIDTH:, :].reshape(bp, X_HEADS, HEAD_DIM, mem), (0, 3, 1, 2)))
        p_ffn.append(ftail[:, FFN_PAD - (FFN_KERNEL - 1):, :])

        zs, zst = _sample_in(l, xs, g1, win)
        hmt_s, c_new, n_new, m_new = _mlstm_sample(l, zst, gb8, ngt, c_all, n_all, m_all)
        hc_s, conv_new = _conv_sample(l, zs, hist_all, cw, cb, lg, lb)
        hs_s, k_new, v_new = _swa_sample(l, zs, zs.reshape(bs, 1, Z_WIDTH), kc_all, vc_all, cache_bias, aux)
        x1, qx = _mix_out_sample(l, xs, hmt_s, hc_s, hs_s.reshape(bs, S_WIDTH), wout, g2, wxq)
        ox = _xattn_sample(l, qx.reshape(bs, 1, X_WIDTH), mk_all, mv_all)
        xs, ffn_new = _ffn_sample(l, x1, ox.reshape(bs, X_WIDTH), wxo, g3, wup, fcw, fcb, wdn, gf,
                                  fh_all, final=last)
        s_c.append(c_new)
        s_n.append(n_new)
        s_m.append(m_new)
        s_conv.append(conv_new)
        s_k.append(k_new.reshape(bs, S_KV_HEADS, HEAD_DIM, W))
        s_v.append(v_new.reshape(bs, S_KV_HEADS, HEAD_DIM, W))
        s_ffn.append(jnp.transpose(ffn_new.reshape(bs, dff // LANES, FFN_KERNEL - 1, LANES), (0, 2, 1, 3))
                     .reshape(bs, FFN_KERNEL - 1, dff))

    st = jnp.stack
    tr = jnp.transpose
    return (xp, xs.reshape(bs, 1, d),
            st(pm_c), st(pm_n), st(pm_m), st(p_conv), st(p_k), st(p_v), st(p_mk), st(p_mv), st(p_ffn),
            tr(st(s_c), (0, 4, 1, 2, 3)), tr(st(s_n), (0, 3, 1, 2)), tr(st(s_m), (0, 2, 1)),
            tr(st(s_conv), (0, 2, 1, 3)), tr(st(s_k), (0, 1, 4, 2, 3)), tr(st(s_v), (0, 1, 4, 2, 3)),
            st(s_ffn))
```

```python
import functools
import math

import numpy as np
import jax
import jax.numpy as jnp
from jax import lax
from jax.experimental import pallas as pl
from jax.experimental.pallas import tpu as pltpu

F32 = jnp.float32
BF16 = jnp.bfloat16
EPS = 1e-6
NEG_INF = float("-inf")

HEAD_DIM = 64
M_HEADS = 4
M_WIDTH = M_HEADS * HEAD_DIM
C_WIDTH = 256
C_KERNEL = 31
S_Q_HEADS = 8
S_KV_HEADS = 2
S_WIDTH = S_Q_HEADS * HEAD_DIM
S_KV_WIDTH = S_KV_HEADS * HEAD_DIM
WINDOW = 128
N_BUCKETS = 32
MAX_DISTANCE = 128
X_HEADS = 4
X_WIDTH = X_HEADS * HEAD_DIM
FFN_KERNEL = 3
QK_SCALE = HEAD_DIM ** -0.5
LOG2E = math.log2(math.e)

Z_M = 0
Z_C = 4 * M_WIDTH
Z_SQ = Z_C + 2 * C_WIDTH
Z_SKV = Z_SQ + S_WIDTH
Z_G = Z_SKV + 2 * S_KV_WIDTH
Z_WIDTH = Z_G + 128

LANES = 128
SUBLANES = 8
VMEM_LIMIT = 56 * 1024 * 1024

M_CHUNK = 128
FFN_CHUNK = 256


def _params(*sem):
    return pltpu.CompilerParams(dimension_semantics=sem, vmem_limit_bytes=VMEM_LIMIT)


def _const_spec(shape):
    nd = len(shape)
    return pl.BlockSpec(shape, lambda *_: (0,) * nd, pipeline_mode=pl.Buffered(1))


def _weight_spec(w, l):
    nd = w.ndim - 1
    return pl.BlockSpec((None,) + w.shape[1:], lambda *_: (l,) + (0,) * nd, pipeline_mode=pl.Buffered(1))


def _rms(x, g):
    return x * lax.rsqrt(jnp.mean(x * x, axis=-1, keepdims=True) + EPS) * g


def _sigmoid(x):
    return 1.0 / (1.0 + jnp.exp(-x))


def _log_sigmoid(x):
    return jnp.minimum(x, 0.0) - jnp.log1p(jnp.exp(-jnp.abs(x)))


def _dot(a, b):
    return jnp.dot(a, b, preferred_element_type=F32)


def _dot_nt(a, b):
    return lax.dot_general(a, b, (((1,), (1,)), ((), ())), preferred_element_type=F32)


def _dot_tn(a, b):
    return lax.dot_general(a, b, (((0,), (0,)), ((), ())), preferred_element_type=F32)


NORM_SUB = 512


IN_GATES = 4 * M_WIDTH
W_PREP_ROWS = 256


def _in_proj_kernel(x_ref, g_ref, wt_ref, o_ref, wprep_ref, w_scr, *, col_chunk):
    d = x_ref.shape[1]

    @pl.when(pl.program_id(0) == 0)
    def _():
        n_gate = 2 * M_HEADS
        for src, dst, n in ((0, 0, IN_GATES), (IN_GATES + n_gate, IN_GATES, Z_G - IN_GATES)):
            for c in range(0, n, W_PREP_ROWS):
                w_scr[:, dst + c:dst + c + W_PREP_ROWS] = wt_ref[src + c:src + c + W_PREP_ROWS, :].T.astype(BF16)
        gate_rows = jnp.concatenate([wt_ref[IN_GATES:IN_GATES + n_gate, :], jnp.zeros((128 - n_gate, d), F32)], axis=0)
        w_scr[:, Z_G:Z_G + 128] = gate_rows.T.astype(BF16)
        wprep_ref[...] = w_scr[...]

    tm, n = o_ref.shape
    sub = min(NORM_SUB, tm)
    for r0 in range(0, tm, sub):
        h = _rms(x_ref[r0:r0 + sub, :], g_ref[...]).astype(BF16)
        for c in range(0, n, col_chunk):
            w = min(col_chunk, n - c)
            o_ref[r0:r0 + sub, c:c + w] = _dot(h, w_scr[:, c:c + w])


def _in_proj(l, x, g, wt, *, tm):
    m, d = x.shape
    tm = min(tm, m)
    return pl.pallas_call(
        functools.partial(_in_proj_kernel, col_chunk=512),
        grid=(m // tm,),
        in_specs=[pl.BlockSpec((tm, d), lambda i: (i, 0)), _weight_spec(g, l), _weight_spec(wt, l)],
        out_specs=[pl.BlockSpec((tm, Z_WIDTH), lambda i: (i, 0)), pl.BlockSpec((d, Z_WIDTH), lambda i: (0, 0))],
        out_shape=[jax.ShapeDtypeStruct((m, Z_WIDTH), F32), jax.ShapeDtypeStruct((d, Z_WIDTH), BF16)],
        scratch_shapes=[pltpu.VMEM((d, Z_WIDTH), BF16)],
        compiler_params=_params("arbitrary"),
        name="in_proj",
    )(x, g, wt)


def _mem_kv_kernel(x_ref, w_ref, o_ref, acc):
    acc[...] = _dot(x_ref[...].astype(BF16), w_ref[...])
    o_ref[...] = acc[...].T


def _mem_kv(l, mem3, w):
    b, mem, d = mem3.shape
    n = w.shape[2]
    return pl.pallas_call(
        _mem_kv_kernel,
        grid=(b,),
        in_specs=[pl.BlockSpec((None, mem, d), lambda i: (i, 0, 0)), _weight_spec(w, l)],
        out_specs=pl.BlockSpec((None, n, mem), lambda i: (i, 0, 0)),
        out_shape=jax.ShapeDtypeStruct((b, n, mem), F32),
        scratch_shapes=[pltpu.VMEM((mem, n), F32)],
        compiler_params=_params("arbitrary"),
        name="mem_kv",
    )(mem3, w)


def _sample_in_kernel(x_ref, g_ref, w_ref, z_ref, zt_ref):
    h = _rms(x_ref[...], g_ref[...]).astype(BF16)
    n = z_ref.shape[1]
    for c in range(0, n, 512):
        w = min(512, n - c)
        zc = _dot(h, w_ref[:, c:c + w])
        z_ref[:, c:c + w] = zc
        zt_ref[c:c + w, :] = zc.T


def _sample_in(l, x, g, w):
    m, d = x.shape
    n = w.shape[1]
    full = lambda shape: pl.BlockSpec(shape, lambda i: (0,) * len(shape))
    return pl.pallas_call(
        _sample_in_kernel,
        grid=(1,),
        in_specs=[full((m, d)), _weight_spec(g, l), full((d, n))],
        out_specs=[full((m, n)), full((n, m))],
        out_shape=[jax.ShapeDtypeStruct((m, n), F32), jax.ShapeDtypeStruct((n, m), F32)],
        compiler_params=_params("arbitrary"),
        name="sample_in",
    )(x, g, w)


def _mlstm_prompt_kernel(z_ref, g_ref, gb_ref, ng_ref, h_ref, cp_ref, np_ref, m_ref,
                         cp_scr, np_scr, m_scr, ht_scr, st_scr, kw_scr):
    NB, L = z_ref.shape[0], z_ref.shape[1]
    D = HEAD_DIM
    hi = lax.Precision.HIGHEST

    @pl.when(pl.program_id(0) == 0)
    def _():
        cp_scr[...] = jnp.zeros(cp_scr.shape, F32)
        np_scr[...] = jnp.zeros(np_scr.shape, F32)
        m_scr[...] = jnp.zeros(m_scr.shape, F32)

    src = lax.broadcasted_iota(jnp.int32, (L, L), 0)
    qry = lax.broadcasted_iota(jnp.int32, (L, L), 1)
    causal_t = src <= qry
    upper = jnp.where(causal_t, 1.0, 0.0)
    lane_half = lax.broadcasted_iota(jnp.int32, (L, 128), 1) // D
    row8 = lax.broadcasted_iota(jnp.int32, (SUBLANES, 128), 0)

    rows, cols = {}, []
    for b in range(NB):
        g_t = (g_ref[b] + gb_ref[...]).T[0:SUBLANES, :]
        b_rows = jnp.dot(_log_sigmoid(g_t), upper, precision=hi, preferred_element_type=F32)
        to_cols = []
        for h in range(M_HEADS):
            b_row = b_rows[M_HEADS + h:M_HEADS + h + 1, :]
            ci_row = g_t[h:h + 1, :] - b_row
            m_prev = m_scr[b, h, 0:1, 0:1]
            b_last = b_row[:, L - 1:L]
            m_new = jnp.maximum(b_last + m_prev, jnp.max(b_last + ci_row, axis=1, keepdims=True))
            rows[b, h] = dict(b_row=b_row, m_prev=m_prev, m_new=m_new, decay=jnp.exp(b_last + m_prev - m_new))
            to_cols += [ci_row, jnp.exp(b_last + ci_row - m_new)]
        to_cols.append(jnp.zeros((128 - 2 * M_HEADS, L), F32))
        cols.append(jnp.concatenate(to_cols, axis=0).T)

    stats = {}
    for b in range(NB):
        for j in range(M_HEADS // 2):
            slab = 128 * j
            q2 = z_ref[b, :, slab:slab + 128].astype(BF16)
            k2 = z_ref[b, :, M_WIDTH + slab:M_WIDTH + slab + 128] * QK_SCALE
            for par in range(2):
                h = 2 * j + par
                r = rows[b, h]
                ci_col = cols[b][:, 2 * h:2 * h + 1]
                a_row = r["b_row"] + r["m_prev"]
                dm = jnp.where(causal_t, r["b_row"] + ci_col, NEG_INF)
                m_row = jnp.maximum(a_row, jnp.max(dm, axis=0, keepdims=True))
                k_own = jnp.where(lane_half == par, k2, 0.0).astype(BF16)
                s_t = _dot_nt(k_own, q2) * jnp.exp(dm - m_row)
                st_scr[b, j, :, L * par:L * (par + 1)] = s_t.astype(BF16)
                stats[b, h] = dict(w_inter=jnp.exp(a_row - m_row), floor=jnp.exp(-m_row),
                                   den_s=jnp.sum(s_t, axis=0, keepdims=True),
                                   decay=r["decay"], m_new=r["m_new"])
            wk = jnp.where(lane_half == 0, cols[b][:, 4 * j + 1:4 * j + 2], cols[b][:, 4 * j + 3:4 * j + 4])
            kw = k2 * wk
            kw_scr[b, j] = kw.astype(BF16)
            stats[b, j, "k_sum"] = jnp.sum(kw, axis=0, keepdims=True)

    block_diag = (lax.broadcasted_iota(jnp.int32, (128, 128), 0) // D
                  == lax.broadcasted_iota(jnp.int32, (128, 128), 1) // D)
    for b in range(NB):
        for j in range(M_HEADS // 2):
            slab = 128 * j
            q2 = z_ref[b, :, slab:slab + 128].astype(BF16)
            v2 = z_ref[b, :, 2 * M_WIDTH + slab:2 * M_WIDTH + slab + 128].astype(BF16)
            cp = cp_scr[b, j]
            npair = np_scr[b, j]
            qc = _dot_nt(cp.astype(BF16), q2)
            qn = _dot_nt(npair.astype(BF16), q2)
            pv = _dot_tn(v2, st_scr[b, j])
            upd = jnp.where(block_diag, _dot_tn(v2, kw_scr[b, j]), 0.0)
            n_new = jnp.zeros((SUBLANES, 128), F32)
            for par in range(2):
                h = 2 * j + par
                st = stats[b, h]
                hr = slice(D * par, D * (par + 1))
                num = st["w_inter"] * qc[hr, :] + pv[hr, L * par:L * (par + 1)]
                den = st["w_inter"] * qn[par:par + 1, :] + st["den_s"]
                hh = num * (1.0 / jnp.maximum(jnp.abs(den), st["floor"]))
                hh = hh * lax.rsqrt(jnp.mean(hh * hh, axis=0, keepdims=True) + EPS)
                ht_scr[b, D * h:D * (h + 1), :] = hh
                cp_scr[b, j, hr, :] = st["decay"] * cp[hr, :] + upd[hr, :]
                k_sum = jnp.where(lane_half[0:1, :] == par, stats[b, j, "k_sum"], 0.0)
                n_new = jnp.where(row8 == par, st["decay"] * npair[par:par + 1, :] + k_sum, n_new)
                m_scr[b, h] = jnp.broadcast_to(st["m_new"], m_scr.shape[2:])
            np_scr[b, j] = n_new
        o_gate = _sigmoid(z_ref[b, :, 3 * M_WIDTH:4 * M_WIDTH])
        h_ref[b] = ht_scr[b].T * ng_ref[...] * o_gate

    cp_ref[...] = cp_scr[...]
    np_ref[...] = np_scr[...]
    m_ref[...] = m_scr[...]


def _mlstm_prompt(l, z3, gbias, ng):
    b, s, _ = z3.shape
    L = M_CHUNK
    P = M_HEADS // 2
    return pl.pallas_call(
        _mlstm_prompt_kernel,
        grid=(s // L,),
        in_specs=[
            pl.BlockSpec((b, L, 4 * M_WIDTH), lambda c: (0, c, Z_M // (4 * M_WIDTH))),
            pl.BlockSpec((b, L, 128), lambda c: (0, c, Z_G // 128)),
            _weight_spec(gbias, l),
            _weight_spec(ng, l),
        ],
        out_specs=[
            pl.BlockSpec((b, L, M_WIDTH), lambda c: (0, c, 0)),
            pl.BlockSpec((b, P, 128, 128), lambda c: (0, 0, 0, 0)),
            pl.BlockSpec((b, P, SUBLANES, 128), lambda c: (0, 0, 0, 0)),
            pl.BlockSpec((b, M_HEADS, SUBLANES, 128), lambda c: (0, 0, 0, 0)),
        ],
        out_shape=[
            jax.ShapeDtypeStruct((b, s, M_WIDTH), F32),
            jax.ShapeDtypeStruct((b, P, 128, 128), F32),
            jax.ShapeDtypeStruct((b, P, SUBLANES, 128), F32),
            jax.ShapeDtypeStruct((b, M_HEADS, SUBLANES, 128), F32),
        ],
        scratch_shapes=[
            pltpu.VMEM((b, P, 128, 128), F32),
            pltpu.VMEM((b, P, SUBLANES, 128), F32),
            pltpu.VMEM((b, M_HEADS, SUBLANES, 128), F32),
            pltpu.VMEM((b, M_WIDTH, L), F32),
            pltpu.VMEM((b, P, L, 2 * L), BF16),
            pltpu.VMEM((b, P, L, 128), BF16),
        ],
        compiler_params=_params("arbitrary"),
        name="mlstm_prompt",
    )(z3, z3, gbias, ng)


CONV_PAD = 32


def _conv_prompt_kernel(z_ref, w_ref, cb_ref, lg_ref, lb_ref, h_ref, tail_ref, buf, shifted):
    tc = z_ref.shape[0]
    t = pl.program_id(1)

    @pl.when(t == 0)
    def _():
        buf[0:CONV_PAD, :] = jnp.zeros((CONV_PAD, C_WIDTH), F32)

    @pl.when(t > 0)
    def _():
        buf[0:CONV_PAD, :] = buf[tc:tc + CONV_PAD, :]

    u = z_ref[:, 0:C_WIDTH] * _sigmoid(z_ref[:, C_WIDTH:2 * C_WIDTH])
    buf[CONV_PAD:CONV_PAD + tc, :] = u
    n_sh = shifted.shape[1]
    for r in range(1, SUBLANES):
        shifted[r - 1] = buf[r:r + n_sh, :]
    off = CONV_PAD - (C_KERNEL - 1)
    acc = jnp.broadcast_to(cb_ref[...], (tc, C_WIDTH))
    for j in range(C_KERNEL):
        r = (off + j) % SUBLANES
        base = off + j - r
        win = buf[base:base + tc, :] if r == 0 else shifted[r - 1, base:base + tc, :]
        acc = acc + w_ref[j:j + 1, :] * win
    mu = jnp.mean(acc, axis=-1, keepdims=True)
    xc = acc - mu
    y = xc * lax.rsqrt(jnp.mean(xc * xc, axis=-1, keepdims=True) + EPS) * lg_ref[...] + lb_ref[...]
    h_ref[...] = y * _sigmoid(y)
    tail_ref[...] = buf[tc:tc + CONV_PAD, :]


def _conv_prompt(l, z3, w, cb, lg, lb, *, tc):
    b, s, _ = z3.shape
    return pl.pallas_call(
        _conv_prompt_kernel,
        grid=(b, s // tc),
        in_specs=[
            pl.BlockSpec((None, tc, 2 * C_WIDTH), lambda i, t: (i, t, Z_C // (2 * C_WIDTH))),
            _weight_spec(w, l), _weight_spec(cb, l), _weight_spec(lg, l), _weight_spec(lb, l),
        ],
        out_specs=[
            pl.BlockSpec((None, tc, C_WIDTH), lambda i, t: (i, t, 0)),
            pl.BlockSpec((None, CONV_PAD, C_WIDTH), lambda i, t: (i, 0, 0)),
        ],
        out_shape=[
            jax.ShapeDtypeStruct((b, s, C_WIDTH), F32),
            jax.ShapeDtypeStruct((b, CONV_PAD, C_WIDTH), F32),
        ],
        scratch_shapes=[pltpu.VMEM((CONV_PAD + tc, C_WIDTH), F32),
                        pltpu.VMEM((SUBLANES - 1, CONV_PAD + tc - SUBLANES, C_WIDTH), F32)],
        compiler_params=_params("arbitrary", "arbitrary"),
        name="conv_prompt",
    )(z3, w, cb, lg, lb)


def _swa_prompt_kernel(rb_ref, sink_ref, bucket_ref, q_ref, kv_ref, kvp_ref, o_ref, kt_ref, vt_ref,
                       bias_scr, s_scr, p_scr, ot_scr, *, layer):
    W = WINDOW
    H = S_Q_HEADS
    G = H // S_KV_HEADS
    i = pl.program_id(0)
    n = pl.program_id(1)

    @pl.when(jnp.logical_and(i == 0, n == 0))
    def _():
        bucket = bucket_ref[...]
        prev_key = lax.broadcasted_iota(jnp.int32, (2 * W, W), 0) < W
        for h in range(H):
            acc = jnp.full((2 * W, W), NEG_INF, F32)
            for b in range(N_BUCKETS):
                acc = jnp.where(bucket == b, rb_ref[b, h], acc)
            acc = acc * LOG2E
            bias_scr[0, h] = acc
            bias_scr[1, h] = jnp.where(prev_key, NEG_INF, acc)

    QB = q_ref.shape[0] // W
    NK = (QB + 1) * W
    first = jnp.where(n == 0, 1, 0)
    kk = jnp.concatenate([kvp_ref[:, 0:S_KV_WIDTH], kv_ref[:, 0:S_KV_WIDTH]], axis=0) * (QK_SCALE * LOG2E)
    vv = jnp.concatenate([kvp_ref[:, S_KV_WIDTH:2 * S_KV_WIDTH], kv_ref[:, S_KV_WIDTH:2 * S_KV_WIDTH]], axis=0)
    kk_r = pltpu.roll(kk, HEAD_DIM, axis=1)
    lo_lane = lax.broadcasted_iota(jnp.int32, (NK, S_KV_WIDTH), 1) < HEAD_DIM
    k_var = [[jnp.where(lo_lane, kk, 0.0).astype(BF16), jnp.where(lo_lane, 0.0, kk_r).astype(BF16)],
             [jnp.where(lo_lane, kk_r, 0.0).astype(BF16), jnp.where(lo_lane, 0.0, kk).astype(BF16)]]
    v_t = vv.T.astype(BF16)

    for j in range(QB):
        k0 = j * W
        masked = first if j == 0 else 0
        m_rows = [None] * H
        for hk in range(S_KV_HEADS):
            c0 = 2 * 128 * hk
            q_st = jnp.concatenate([q_ref[k0:k0 + W, c0:c0 + 128], q_ref[k0:k0 + W, c0 + 128:c0 + 256]],
                                   axis=0).astype(BF16)
            for half in range(2):
                s_t = _dot_nt(k_var[hk][half][k0:k0 + 2 * W, :], q_st)
                for slab in range(2):
                    head = G * hk + 2 * slab + half
                    sb = s_t[:, 128 * slab:128 * (slab + 1)] + bias_scr[masked, head]
                    s_scr[j, head] = sb
                    m_rows[head] = jnp.maximum(jnp.max(sb, axis=0, keepdims=True), sink_ref[layer, head] * LOG2E)

        inv = [None] * H
        for head in range(H):
            e = jnp.exp2(s_scr[j, head] - m_rows[head])
            den = jnp.sum(e, axis=0, keepdims=True) + jnp.exp2(sink_ref[layer, head] * LOG2E - m_rows[head])
            inv[head] = 1.0 / den
            p_scr[j, :, 128 * head:128 * (head + 1)] = e.astype(BF16)

        for hk in range(S_KV_HEADS):
            o_t = _dot(v_t[HEAD_DIM * hk:HEAD_DIM * (hk + 1), k0:k0 + 2 * W],
                       p_scr[j, :, 128 * G * hk:128 * G * (hk + 1)])
            for g in range(G):
                head = G * hk + g
                ot_scr[j, HEAD_DIM * head:HEAD_DIM * (head + 1), :] = o_t[:, 128 * g:128 * (g + 1)] * inv[head]
        o_ref[k0:k0 + W, :] = ot_scr[j].T

    @pl.when(n == pl.num_programs(1) - 1)
    def _():
        kt_ref[...] = kv_ref[(QB - 1) * W:QB * W, 0:S_KV_WIDTH].T
        vt_ref[...] = kv_ref[(QB - 1) * W:QB * W, S_KV_WIDTH:2 * S_KV_WIDTH].T


def _swa_prompt(l, z3, rel_bias, sinks, bucket, *, qb):
    b, s, _ = z3.shape
    W = WINDOW
    smem = pl.BlockSpec(memory_space=pltpu.SMEM)
    return pl.pallas_call(
        functools.partial(_swa_prompt_kernel, layer=l),
        grid=(b, s // (qb * W)),
        in_specs=[
            smem, smem, _const_spec((2 * W, W)),
            pl.BlockSpec((None, qb * W, S_WIDTH), lambda i, n: (i, n, Z_SQ // S_WIDTH)),
            pl.BlockSpec((None, qb * W, 2 * S_KV_WIDTH), lambda i, n: (i, n, Z_SKV // (2 * S_KV_WIDTH))),
            pl.BlockSpec((None, W, 2 * S_KV_WIDTH),
                         lambda i, n: (i, jnp.maximum(n * qb - 1, 0), Z_SKV // (2 * S_KV_WIDTH))),
        ],
        out_specs=[
            pl.BlockSpec((None, qb * W, S_WIDTH), lambda i, n: (i, n, 0)),
            pl.BlockSpec((None, S_KV_WIDTH, W), lambda i, n: (i, 0, 0)),
            pl.BlockSpec((None, S_KV_WIDTH, W), lambda i, n: (i, 0, 0)),
        ],
        out_shape=[
            jax.ShapeDtypeStruct((b, s, S_WIDTH), F32),
            jax.ShapeDtypeStruct((b, S_KV_WIDTH, W), F32),
            jax.ShapeDtypeStruct((b, S_KV_WIDTH, W), F32),
        ],
        scratch_shapes=[
            pltpu.VMEM((2, S_Q_HEADS, 2 * W, W), F32),
            pltpu.VMEM((qb, S_Q_HEADS, 2 * W, W), F32),
            pltpu.VMEM((qb, 2 * W, S_Q_HEADS * W), BF16),
            pltpu.VMEM((qb, S_WIDTH, W), F32),
        ],
        compiler_params=_params("arbitrary", "arbitrary"),
        name="swa_prompt",
    )(rel_bias, sinks, bucket, z3, z3, z3)


MIX_SUB = 512


def _mix_out_prompt_kernel(x_ref, hm_ref, hc_ref, hs_ref, wout_ref, g2_ref, wq_ref, mk_ref, mv_ref, wo_ref, o_ref,
                           x1_scr, qx_scr, p_scr):
    tm = x_ref.shape[0]
    mem = mk_ref.shape[1]
    mk_t = mk_ref[...]
    mv_t = mv_ref[...]
    row_head = lax.broadcasted_iota(jnp.int32, mk_t.shape, 0) // HEAD_DIM
    k_heads = [jnp.where(row_head == h, mk_t, 0.0).astype(BF16) for h in range(X_HEADS)]
    v_cat = jnp.concatenate([jnp.where(row_head == h, mv_t, 0.0).astype(BF16) for h in range(X_HEADS)], axis=1)
    subs = [slice(r0, r0 + MIX_SUB) for r0 in range(0, tm, MIX_SUB)]
    for rows in subs:
        cat = jnp.concatenate([hm_ref[rows, :].astype(BF16), hc_ref[rows, :].astype(BF16),
                               hs_ref[rows, :].astype(BF16)], axis=1)
        x1 = x_ref[rows, :] + _dot(cat, wout_ref[...])
        x1_scr[rows, :] = x1
        qx = _dot(_rms(x1, g2_ref[...]).astype(BF16), wq_ref[...])
        qx_scr[rows, :] = (qx * (QK_SCALE * LOG2E)).astype(BF16)
    for rows in subs:
        qx = qx_scr[rows, :]
        for h in range(X_HEADS):
            s = _dot(qx, k_heads[h])
            e = jnp.exp2(s - jnp.max(s, axis=1, keepdims=True))
            p_scr[rows, mem * h:mem * (h + 1)] = (e * (1.0 / jnp.sum(e, axis=1, keepdims=True))).astype(BF16)
    for rows in subs:
        o = _dot_nt(p_scr[rows, :], v_cat)
        o_ref[rows, :] = x1_scr[rows, :] + _dot(o.astype(BF16), wo_ref[...])


def _mix_out_prompt(l, x3, hm, hc, hs, wout, g2, wq, mkv, wo, *, tm):
    b, s, d = x3.shape
    tm = min(tm, s)
    mem = mkv.shape[2]
    row = lambda w: pl.BlockSpec((None, tm, w), lambda i, t: (i, t, 0))
    return pl.pallas_call(
        _mix_out_prompt_kernel,
        grid=(b, s // tm),
        in_specs=[
            row(d), row(M_WIDTH), row(C_WIDTH), row(S_WIDTH),
            _weight_spec(wout, l), _weight_spec(g2, l), _weight_spec(wq, l),
            pl.BlockSpec((None, X_WIDTH, mem), lambda i, t: (i, 0, 0)),
            pl.BlockSpec((None, X_WIDTH, mem), lambda i, t: (i, 1, 0)),
            _weight_spec(wo, l),
        ],
        out_specs=row(d),
        out_shape=jax.ShapeDtypeStruct((b, s, d), F32),
        scratch_shapes=[pltpu.VMEM((tm, d), F32), pltpu.VMEM((tm, X_WIDTH), BF16),
                        pltpu.VMEM((tm, X_HEADS * mem), BF16)],
        compiler_params=_params("arbitrary", "arbitrary"),
        name="mix_out_prompt",
    )(x3, hm, hc, hs, wout, g2, wq, mkv, mkv, wo)


FFN_PAD = 8


def _ffn_prompt_kernel(x_ref, g3_ref, wup_ref, cw_ref, cb_ref, wdn_ref, gf_ref, o_ref, tail_ref,
                       gbuf, carry, act, *, final):
    tm = x_ref.shape[0]
    dff = wdn_ref.shape[0]

    @pl.when(pl.program_id(1) == 0)
    def _():
        carry[...] = jnp.zeros(carry.shape, F32)

    x = x_ref[...]
    h = _rms(x, g3_ref[...]).astype(BF16)
    w = FFN_CHUNK
    for c in range(0, dff, w):
        a = _dot(h, wup_ref[:, c:c + w])
        g = _dot(h, wup_ref[:, dff + c:dff + c + w])
        gbuf[0:FFN_PAD, 0:w] = carry[:, c:c + w]
        gbuf[FFN_PAD:FFN_PAD + tm, 0:w] = g
        carry[:, c:c + w] = g[tm - FFN_PAD:tm, :]
        gc = (cw_ref[0:1, c:c + w] * gbuf[FFN_PAD - 2:FFN_PAD - 2 + tm, 0:w]
              + cw_ref[1:2, c:c + w] * gbuf[FFN_PAD - 1:FFN_PAD - 1 + tm, 0:w]
              + cw_ref[2:3, c:c + w] * g + cb_ref[:, c:c + w])
        act[:, c:c + w] = (gc * _sigmoid(gc) * a).astype(BF16)
    y = x + _dot(act[...], wdn_ref[...])
    o_ref[...] = _rms(y, gf_ref[...]) if final else y
    tail_ref[...] = carry[...]


def _ffn_prompt(l, x3, g3, wup, cw, cb, wdn, gf, *, tm, final):
    b, s, d = x3.shape
    tm = min(tm, s)
    dff = wdn.shape[1]
    return pl.pallas_call(
        functools.partial(_ffn_prompt_kernel, final=final),
        grid=(b, s // tm),
        in_specs=[
            pl.BlockSpec((None, tm, d), lambda i, t: (i, t, 0)),
            _weight_spec(g3, l), _weight_spec(wup, l), _weight_spec(cw, l), _weight_spec(cb, l),
            _weight_spec(wdn, l), _const_spec((1, d)),
        ],
        out_specs=[
            pl.BlockSpec((None, tm, d), lambda i, t: (i, t, 0)),
            pl.BlockSpec((None, FFN_PAD, dff), lambda i, t: (i, 0, 0)),
        ],
        out_shape=[jax.ShapeDtypeStruct((b, s, d), F32), jax.ShapeDtypeStruct((b, FFN_PAD, dff), F32)],
        scratch_shapes=[
            pltpu.VMEM((FFN_PAD + tm, FFN_CHUNK), F32),
            pltpu.VMEM((FFN_PAD, dff), F32),
            pltpu.VMEM((tm, dff), BF16),
        ],
        compiler_params=_params("arbitrary", "arbitrary"),
        name="ffn_prompt",
    )(x3, g3, wup, cw, cb, wdn, gf)


SWA_SAMPLE_BLOCK = 32
XATTN_SAMPLE_BLOCK = 16
CONV_SAMPLE_BLOCK = 32


def _mlstm_sample_kernel(gb_ref, q_ref, k_ref, v_ref, o_ref, g_ref, ng_ref, c_ref, n_ref, m_ref,
                         h_ref, co_ref, no_ref, mo_ref, kw_scr, *, layer):
    h = pl.program_id(0)
    i_pre = g_ref[pl.ds(h, 1), :] + gb_ref[layer, h]
    f_pre = g_ref[pl.ds(M_HEADS + h, 1), :] + gb_ref[layer, M_HEADS + h]
    a = _log_sigmoid(f_pre) + m_ref[pl.ds(h, 1), :]
    m_t = jnp.maximum(a, i_pre)
    w_old = jnp.exp(a - m_t)
    w_new = jnp.exp(i_pre - m_t)
    q = q_ref[...]
    k = k_ref[...] * QK_SCALE
    v = v_ref[...]
    n_old = n_ref[...]
    kw_scr[...] = k * w_new

    def body(d, acc):
        c_old = c_ref[d]
        co_ref[d] = w_old * c_old + kw_scr[pl.ds(d, 1), :] * v
        return acc + q_ref[pl.ds(d, 1), :] * c_old

    qc = lax.fori_loop(0, HEAD_DIM, body, jnp.zeros(v.shape, F32), unroll=8)
    s = jnp.sum(q * k, axis=0, keepdims=True) * w_new
    num = w_old * qc + s * v
    den = w_old * jnp.sum(q * n_old, axis=0, keepdims=True) + s
    hh = num / jnp.maximum(jnp.abs(den), jnp.exp(-m_t))
    hh = hh * lax.rsqrt(jnp.mean(hh * hh, axis=0, keepdims=True) + EPS) * ng_ref[...]
    h_ref[...] = hh * _sigmoid(o_ref[...])
    no_ref[...] = w_old * n_old + kw_scr[...]
    mo_ref[pl.ds(h, 1), :] = m_t


def _mlstm_sample(l, zt, gb, ngt, c_all, n_all, m_all):
    bsz = zt.shape[1]
    D = HEAD_DIM
    feat = lambda off: pl.BlockSpec((D, bsz), lambda h: (off // D + h, 0))
    return pl.pallas_call(
        functools.partial(_mlstm_sample_kernel, layer=l),
        grid=(M_HEADS,),
        in_specs=[
            pl.BlockSpec(memory_space=pltpu.SMEM),
            feat(Z_M), feat(Z_M + M_WIDTH), feat(Z_M + 2 * M_WIDTH), feat(Z_M + 3 * M_WIDTH),
            pl.BlockSpec((SUBLANES, bsz), lambda h: (Z_G // SUBLANES, 0)),
            pl.BlockSpec((None, D, bsz), lambda h: (l, h, 0)),
            pl.BlockSpec((None, None, D, D, bsz), lambda h: (l, h, 0, 0, 0)),
            pl.BlockSpec((None, None, D, bsz), lambda h: (l, h, 0, 0)),
            pl.BlockSpec((None, M_HEADS, bsz), lambda h: (l, 0, 0)),
        ],
        out_specs=[
            pl.BlockSpec((D, bsz), lambda h: (h, 0)),
            pl.BlockSpec((None, D, D, bsz), lambda h: (h, 0, 0, 0)),
            pl.BlockSpec((None, D, bsz), lambda h: (h, 0, 0)),
            pl.BlockSpec((M_HEADS, bsz), lambda h: (0, 0)),
        ],
        out_shape=[
            jax.ShapeDtypeStruct((M_WIDTH, bsz), F32),
            jax.ShapeDtypeStruct((M_HEADS, D, D, bsz), F32),
            jax.ShapeDtypeStruct((M_HEADS, D, bsz), F32),
            jax.ShapeDtypeStruct((M_HEADS, bsz), F32),
        ],
        scratch_shapes=[pltpu.VMEM((D, bsz), F32)],
        compiler_params=_params("arbitrary"),
        name="mlstm_sample",
    )(gb, zt, zt, zt, zt, zt, ngt, c_all, n_all, m_all)


def _conv_sample_kernel(z_ref, hist_ref, w_ref, cb_ref, lg_ref, lb_ref, h_ref, hist_out_ref):
    nh = C_KERNEL - 1
    u = z_ref[:, 0:C_WIDTH] * _sigmoid(z_ref[:, C_WIDTH:2 * C_WIDTH])
    acc = cb_ref[...] + w_ref[nh:nh + 1, :] * u
    for j in range(nh):
        acc = acc + w_ref[j:j + 1, :] * hist_ref[j]
    mu = jnp.mean(acc, axis=-1, keepdims=True)
    xc = acc - mu
    y = xc * lax.rsqrt(jnp.mean(xc * xc, axis=-1, keepdims=True) + EPS) * lg_ref[...] + lb_ref[...]
    h_ref[...] = y * _sigmoid(y)
    for j in range(nh - 1):
        hist_out_ref[j] = hist_ref[j + 1]
    hist_out_ref[nh - 1] = u


def _conv_sample(l, z, hist_all, w, cb, lg, lb):
    bsz = z.shape[0]
    nh = hist_all.shape[1]
    R = min(CONV_SAMPLE_BLOCK, bsz)
    return pl.pallas_call(
        _conv_sample_kernel,
        grid=(bsz // R,),
        in_specs=[
            pl.BlockSpec((R, 2 * C_WIDTH), lambda i: (i, Z_C // (2 * C_WIDTH))),
            pl.BlockSpec((None, nh, R, C_WIDTH), lambda i: (l, 0, i, 0)),
            _weight_spec(w, l), _weight_spec(cb, l), _weight_spec(lg, l), _weight_spec(lb, l),
        ],
        out_specs=[pl.BlockSpec((R, C_WIDTH), lambda i: (i, 0)), pl.BlockSpec((nh, R, C_WIDTH), lambda i: (0, i, 0))],
        out_shape=[jax.ShapeDtypeStruct((bsz, C_WIDTH), F32), jax.ShapeDtypeStruct((nh, bsz, C_WIDTH), F32)],
        compiler_params=_params("arbitrary"),
        name="conv_sample",
    )(z, hist_all, w, cb, lg, lb)


def _swa_sample_kernel(q_ref, kvn_ref, kv2_ref, kc_ref, vc_ref, bias_ref, aux_ref, o_ref, ko_ref, vo_ref):
    R = q_ref.shape[0]
    W = kc_ref.shape[2]
    H = S_Q_HEADS
    shape = (R, H, 128)
    row = lax.broadcasted_iota(jnp.int32, shape, 1)
    lane_half = lax.broadcasted_iota(jnp.int32, shape, 2) // HEAD_DIM
    q_half = row % 2
    kv_head = row // (H // S_KV_HEADS)
    qs = jnp.zeros(shape, F32)
    for j in range(H // 2):
        qs = jnp.where(row // 2 == j, q_ref[:, :, 128 * j:128 * (j + 1)], qs)
    q8 = jnp.where(lane_half == kv_head, jnp.where(q_half == kv_head, qs, pltpu.roll(qs, HEAD_DIM, axis=2)), 0.0)
    k_new = kvn_ref[:, :, 0:S_KV_WIDTH]
    v_new = kvn_ref[:, :, S_KV_WIDTH:2 * S_KV_WIDTH]
    s = jnp.einsum("bqd,bdk->bqk", q8.astype(BF16), kc_ref[...].astype(BF16), preferred_element_type=F32) * QK_SCALE
    s = s + bias_ref[...][None]
    s_new = jnp.sum(q8 * k_new, axis=2, keepdims=True) * QK_SCALE + aux_ref[:, 0:1][None]
    sink = aux_ref[:, 1:2][None]
    m = jnp.maximum(jnp.maximum(jnp.max(s, axis=2, keepdims=True), s_new), sink)
    e = jnp.exp(s - m)
    e_new = jnp.exp(s_new - m)
    inv = 1.0 / (jnp.sum(e, axis=2, keepdims=True) + e_new + jnp.exp(sink - m))
    o8 = jnp.einsum("bqk,bdk->bqd", e.astype(BF16), vc_ref[...].astype(BF16), preferred_element_type=F32)
    o8 = (o8 + e_new * v_new) * inv
    o8 = jnp.where(lane_half == q_half, jnp.where(q_half == kv_head, o8, pltpu.roll(o8, HEAD_DIM, axis=2)), 0.0)
    for j in range(H // 2):
        o_ref[:, :, 128 * j:128 * (j + 1)] = jnp.sum(jnp.where(row // 2 == j, o8, 0.0), axis=1, keepdims=True)
    k_cols = kv2_ref[:, 0:S_KV_WIDTH].T
    v_cols = kv2_ref[:, S_KV_WIDTH:2 * S_KV_WIDTH].T
    last = lax.broadcasted_iota(jnp.int32, (S_KV_WIDTH, W), 1) == W - 1
    for r in range(R):
        ko_ref[r] = jnp.where(last, k_cols[:, r:r + 1], pltpu.roll(kc_ref[r], W - 1, axis=1))
        vo_ref[r] = jnp.where(last, v_cols[:, r:r + 1], pltpu.roll(vc_ref[r], W - 1, axis=1))


def _swa_sample(l, z, z3, kc_all, vc_all, bias, aux):
    bsz = z.shape[0]
    R = min(SWA_SAMPLE_BLOCK, bsz)
    W = kc_all.shape[3]
    cache_in = pl.BlockSpec((None, R, S_KV_WIDTH, W), lambda i: (l, i, 0, 0))
    cache_out = pl.BlockSpec((R, S_KV_WIDTH, W), lambda i: (i, 0, 0))
    return pl.pallas_call(
        _swa_sample_kernel,
        grid=(bsz // R,),
        in_specs=[
            pl.BlockSpec((R, 1, S_WIDTH), lambda i: (i, 0, Z_SQ // S_WIDTH)),
            pl.BlockSpec((R, 1, 2 * S_KV_WIDTH), lambda i: (i, 0, Z_SKV // (2 * S_KV_WIDTH))),
            pl.BlockSpec((R, 2 * S_KV_WIDTH), lambda i: (i, Z_SKV // (2 * S_KV_WIDTH))),
            cache_in, cache_in,
            _const_spec((S_Q_HEADS, 128)), _weight_spec(aux, l),
        ],
        out_specs=[pl.BlockSpec((R, 1, S_WIDTH), lambda i: (i, 0, 0)), cache_out, cache_out],
        out_shape=[
            jax.ShapeDtypeStruct((bsz, 1, S_WIDTH), F32),
            jax.ShapeDtypeStruct((bsz, S_KV_WIDTH, W), F32),
            jax.ShapeDtypeStruct((bsz, S_KV_WIDTH, W), F32),
        ],
        compiler_params=_params("arbitrary"),
        name="swa_sample",
    )(z3, z3, z, kc_all, vc_all, bias, aux)


def _mix_out_sample_kernel(x_ref, hmt_ref, hc_ref, hs_ref, wout_ref, g2_ref, wq_ref, x1_ref, q_ref):
    cat = jnp.concatenate([hmt_ref[...].T.astype(BF16), hc_ref[...].astype(BF16), hs_ref[...].astype(BF16)], axis=1)
    x1 = x_ref[...] + _dot(cat, wout_ref[...])
    x1_ref[...] = x1
    q_ref[...] = _dot(_rms(x1, g2_ref[...]).astype(BF16), wq_ref[...])


def _mix_out_sample(l, x, hm, hc, hs, wout, g2, wq):
    bsz, d = x.shape
    full = lambda a: pl.BlockSpec(a.shape, lambda i: (0,) * a.ndim)
    args = (x, hm, hc, hs, wout, g2, wq)
    return pl.pallas_call(
        _mix_out_sample_kernel,
        grid=(1,),
        in_specs=[full(x), full(hm), full(hc), full(hs), _weight_spec(wout, l), _weight_spec(g2, l),
                  _weight_spec(wq, l)],
        out_specs=[pl.BlockSpec((bsz, d), lambda i: (0, 0)), pl.BlockSpec((bsz, X_WIDTH), lambda i: (0, 0))],
        out_shape=[jax.ShapeDtypeStruct((bsz, d), F32), jax.ShapeDtypeStruct((bsz, X_WIDTH), F32)],
        compiler_params=_params("arbitrary"),
        name="mix_out_sample",
    )(*args)


def _xattn_sample_kernel(q_ref, k_ref, v_ref, o_ref):
    R = k_ref.shape[0]
    shape = (R, SUBLANES, X_WIDTH)
    row = lax.broadcasted_iota(jnp.int32, shape, 1)
    lane_head = lax.broadcasted_iota(jnp.int32, shape, 2) // HEAD_DIM
    own = row == lane_head
    q8 = jnp.where(own, jnp.broadcast_to(q_ref[...], shape), 0.0).astype(BF16)
    s = jnp.einsum("bqd,bdk->bqk", q8, k_ref[...].astype(BF16), preferred_element_type=F32) * QK_SCALE
    e = jnp.exp(s - jnp.max(s, axis=2, keepdims=True))
    p = (e / jnp.sum(e, axis=2, keepdims=True)).astype(BF16)
    o8 = jnp.einsum("bqk,bdk->bqd", p, v_ref[...].astype(BF16), preferred_element_type=F32)
    o_ref[...] = jnp.sum(jnp.where(own, o8, 0.0), axis=1, keepdims=True)


def _xattn_sample(l, q3, k_all, v_all):
    _, bsz, w, mem = k_all.shape
    R = min(XATTN_SAMPLE_BLOCK, bsz)
    kv = pl.BlockSpec((None, R, w, mem), lambda i: (l, i, 0, 0))
    qo = pl.BlockSpec((R, 1, w), lambda i: (i, 0, 0))
    return pl.pallas_call(
        _xattn_sample_kernel,
        grid=(bsz // R,),
        in_specs=[qo, kv, kv],
        out_specs=qo,
        out_shape=jax.ShapeDtypeStruct((bsz, 1, w), F32),
        compiler_params=_params("arbitrary"),
        name="xattn_sample",
    )(q3, k_all, v_all)


def _ffn_sample_kernel(x1_ref, ox_ref, wo_ref, g3_ref, wup_ref, cw_ref, cb_ref, wdn_ref, gf_ref, hist_ref,
                       o_ref, hist_out_ref, act, *, final):
    dff = wdn_ref.shape[0]
    x = x1_ref[...] + _dot(ox_ref[...].astype(BF16), wo_ref[...])
    h = _rms(x, g3_ref[...]).astype(BF16)
    tiles = FFN_CHUNK // LANES
    for c in range(0, dff, FFN_CHUNK):
        a = _dot(h, wup_ref[:, c:c + FFN_CHUNK])
        g = _dot(h, wup_ref[:, dff + c:dff + c + FFN_CHUNK])
        lo = [2 * (c + LANES * t) for t in range(tiles)]
        h0 = jnp.concatenate([hist_ref[:, o:o + LANES] for o in lo], axis=1)
        h1 = jnp.concatenate([hist_ref[:, o + LANES:o + 2 * LANES] for o in lo], axis=1)
        gc = (cw_ref[0:1, c:c + FFN_CHUNK] * h0 + cw_ref[1:2, c:c + FFN_CHUNK] * h1
              + cw_ref[2:3, c:c + FFN_CHUNK] * g + cb_ref[:, c:c + FFN_CHUNK])
        act[:, c:c + FFN_CHUNK] = (gc * _sigmoid(gc) * a).astype(BF16)
        for t, o in enumerate(lo):
            hist_out_ref[:, o:o + LANES] = h1[:, LANES * t:LANES * (t + 1)]
            hist_out_ref[:, o + LANES:o + 2 * LANES] = g[:, LANES * t:LANES * (t + 1)]
    y = x + _dot(act[...], wdn_ref[...])
    o_ref[...] = _rms(y, gf_ref[...]) if final else y


def _ffn_sample(l, x1, ox, wo, g3, wup, cw, cb, wdn, gf, hist_all, *, final):
    bsz, d = x1.shape
    dff = wdn.shape[1]
    full = lambda a: pl.BlockSpec(a.shape, lambda i: (0,) * a.ndim)
    hshape = hist_all.shape[1:]
    args = (x1, ox, wo, g3, wup, cw, cb, wdn, gf, hist_all)
    return pl.pallas_call(
        functools.partial(_ffn_sample_kernel, final=final),
        grid=(1,),
        in_specs=[full(x1), full(ox), _weight_spec(wo, l), _weight_spec(g3, l), _weight_spec(wup, l),
                  _weight_spec(cw, l), _weight_spec(cb, l),
                  _weight_spec(wdn, l), full(gf), pl.BlockSpec((None,) + hshape, lambda i: (l, 0, 0))],
        out_specs=[pl.BlockSpec((bsz, d), lambda i: (0, 0)), pl.BlockSpec(hshape, lambda i: (0, 0))],
        out_shape=[jax.ShapeDtypeStruct((bsz, d), F32), jax.ShapeDtypeStruct(hshape, F32)],
        scratch_shapes=[pltpu.VMEM((bsz, dff), BF16)],
        compiler_params=_params("arbitrary"),
        name="ffn_sample",
    )(*args)


def _t5_buckets(dist):
    n = np.maximum(dist, 0)
    max_exact = N_BUCKETS // 2
    nf = np.maximum(n, max_exact).astype(np.float32)
    large = max_exact + (np.log(nf / np.float32(max_exact)) / np.float32(math.log(MAX_DISTANCE / max_exact))
                         * np.float32(N_BUCKETS - max_exact)).astype(np.int32)
    return np.where(n < max_exact, n, np.minimum(large, N_BUCKETS - 1))


def _prompt_buckets():
    W = WINDOW
    dist = np.arange(W)[None, :] + W - np.arange(2 * W)[:, None]
    band = (dist >= 0) & (dist < W)
    return np.where(band, _t5_buckets(dist), -1).astype(np.int32)


def _swa_tables(rel_bias):
    W = WINDOW
    dist_c = W - np.arange(W)
    tab = jnp.transpose(rel_bias[_t5_buckets(dist_c)], (1, 0))
    cache_bias = jnp.where((dist_c < W)[None], tab, NEG_INF)
    return cache_bias, rel_bias[0]


def kernel(x_prompt, x_sample, mem_prompt, state_mlstm_C, state_mlstm_n, state_mlstm_m, state_conv, cache_swa_k, cache_swa_v, cache_mem_k, cache_mem_v, state_ffn_conv, rel_bias, norm1_g, w_in, b_i, b_f, mlstm_norm_g, conv_w, conv_b, conv_ln_g, conv_ln_b, swa_sinks, w_out, norm2_g, w_xq, w_xk, w_xv, w_xo, norm3_g, w_up, ffn_conv_w, ffn_conv_b, w_down, final_norm_g):
    depth = w_in.shape[0]
    bp, seq, d = x_prompt.shape
    bs = x_sample.shape[0]
    mem = mem_prompt.shape[1]
    dff = w_down.shape[1]
    W = WINDOW
    nh = C_KERNEL - 1

    xp = x_prompt
    xs = x_sample.reshape(bs, d)
    gf = final_norm_g.reshape(1, d)
    c_all = jnp.transpose(state_mlstm_C, (0, 2, 3, 4, 1))
    n_all = jnp.transpose(state_mlstm_n, (0, 2, 3, 1))
    m_all = jnp.transpose(state_mlstm_m, (0, 2, 1))
    hist_all = jnp.transpose(state_conv, (0, 2, 1, 3))
    kc_all = jnp.transpose(cache_swa_k, (0, 1, 3, 4, 2)).reshape(depth, bs, S_KV_WIDTH, W)
    vc_all = jnp.transpose(cache_swa_v, (0, 1, 3, 4, 2)).reshape(depth, bs, S_KV_WIDTH, W)
    mk_all = jnp.transpose(cache_mem_k, (0, 1, 3, 4, 2)).reshape(depth, bs, X_WIDTH, mem)
    mv_all = jnp.transpose(cache_mem_v, (0, 1, 3, 4, 2)).reshape(depth, bs, X_WIDTH, mem)
    fh_all = jnp.transpose(state_ffn_conv.reshape(depth, bs, FFN_KERNEL - 1, dff // LANES, LANES),
                           (0, 1, 3, 2, 4)).reshape(depth, bs, (FFN_KERNEL - 1) * dff)
    pm_c, pm_n, pm_m, p_conv, p_k, p_v, p_mk, p_mv, p_ffn = ([] for _ in range(9))
    s_c, s_n, s_m, s_conv, s_k, s_v, s_ffn = ([] for _ in range(7))

    win_t = jnp.swapaxes(w_in, 1, 2)
    wout = w_out.astype(BF16)
    wxq = w_xq.astype(BF16)
    wxkv = jnp.concatenate([w_xk, w_xv], axis=2).astype(BF16)
    wxo = w_xo.astype(BF16)
    wup = w_up.astype(BF16)
    wdn = w_down.astype(BF16)
    bucket = jnp.asarray(_prompt_buckets())

    rows = lambda a: a.reshape(depth, 1, -1)
    g1, g2, g3 = rows(norm1_g), rows(norm2_g), rows(norm3_g)
    gb8 = jnp.concatenate([b_i, b_f], axis=1)
    gbias = rows(jnp.concatenate([gb8, jnp.zeros((depth, 128 - 2 * M_HEADS), F32)], axis=1))
    ng = rows(mlstm_norm_g)
    ngt = jnp.broadcast_to(mlstm_norm_g[:, :, None], (depth, M_WIDTH, bs))
    cw = jnp.concatenate([conv_w, jnp.zeros((depth, CONV_PAD - C_KERNEL, C_WIDTH), F32)], axis=1)
    cb, lg, lb = rows(conv_b), rows(conv_ln_g), rows(conv_ln_b)
    fcw = jnp.concatenate([ffn_conv_w, jnp.zeros((depth, SUBLANES - FFN_KERNEL, dff), F32)], axis=1)
    fcb = rows(ffn_conv_b)
    cache_bias, bias0 = _swa_tables(rel_bias)
    aux = jnp.concatenate([jnp.broadcast_to(bias0[None, :, None], (depth, S_Q_HEADS, 1)), swa_sinks[:, :, None],
                           jnp.zeros((depth, S_Q_HEADS, 126), F32)], axis=2)

    for l in range(depth):
        last = l == depth - 1

        mkv = _mem_kv(l, mem_prompt, wxkv)
        z, win = _in_proj(l, xp.reshape(bp * seq, d), g1, win_t, tm=1024)
        z = z.reshape(bp, seq, Z_WIDTH)
        hm, cpair, npair, mm = _mlstm_prompt(l, z, gbias, ng)
        hc, ctail = _conv_prompt(l, z, cw, cb, lg, lb, tc=min(1024, seq))
        hs, kt, vt = _swa_prompt(l, z, rel_bias, swa_sinks, bucket, qb=min(8, seq // W))
        xp = _mix_out_prompt(l, xp, hm, hc, hs, wout, g2, wxq, mkv, wxo, tm=1024)
        xp, ftail = _ffn_prompt(l, xp, g3, wup, fcw, fcb, wdn, gf, tm=512, final=last)
        half = lambda h: slice(HEAD_DIM * (h % 2), HEAD_DIM * (h % 2 + 1))
        pm_c.append(jnp.stack([jnp.swapaxes(cpair[:, h // 2, half(h), half(h)], 1, 2) for h in range(M_HEADS)], axis=1))
        pm_n.append(jnp.stack([npair[:, h // 2, h % 2, half(h)] for h in range(M_HEADS)], axis=1))
        pm_m.append(mm[:, :, 0, 0])
        p_conv.append(ctail[:, CONV_PAD - nh:, :])
        p_k.append(jnp.transpose(kt.reshape(bp, S_KV_HEADS, HEAD_DIM, W), (0, 3, 1, 2)))
        p_v.append(jnp.transpose(vt.reshape(bp, S_KV_HEADS, HEAD_DIM, W), (0, 3, 1, 2)))
        p_mk.append(jnp.transpose(mkv[:, 0:X_WIDTH, :].reshape(bp, X_HEADS, HEAD_DIM, mem), (0, 3, 1, 2)))
        p_mv.append(jnp.transpose(mkv[:, X_WIDTH:, :].reshape(bp, X_HEADS, HEAD_DIM, mem), (0, 3, 1, 2)))
        p_ffn.append(ftail[:, FFN_PAD - (FFN_KERNEL - 1):, :])

        zs, zst = _sample_in(l, xs, g1, win)
        hmt_s, c_new, n_new, m_new = _mlstm_sample(l, zst, gb8, ngt, c_all, n_all, m_all)
        hc_s, conv_new = _conv_sample(l, zs, hist_all, cw, cb, lg, lb)
        hs_s, k_new, v_new = _swa_sample(l, zs, zs.reshape(bs, 1, Z_WIDTH), kc_all, vc_all, cache_bias, aux)
        x1, qx = _mix_out_sample(l, xs, hmt_s, hc_s, hs_s.reshape(bs, S_WIDTH), wout, g2, wxq)
        ox = _xattn_sample(l, qx.reshape(bs, 1, X_WIDTH), mk_all, mv_all)
        xs, ffn_new = _ffn_sample(l, x1, ox.reshape(bs, X_WIDTH), wxo, g3, wup, fcw, fcb, wdn, gf,
                                  fh_all, final=last)
        s_c.append(c_new)
        s_n.append(n_new)
        s_m.append(m_new)
        s_conv.append(conv_new)
        s_k.append(k_new.reshape(bs, S_KV_HEADS, HEAD_DIM, W))
        s_v.append(v_new.reshape(bs, S_KV_HEADS, HEAD_DIM, W))
        s_ffn.append(jnp.transpose(ffn_new.reshape(bs, dff // LANES, FFN_KERNEL - 1, LANES), (0, 2, 1, 3))
                     .reshape(bs, FFN_KERNEL - 1, dff))

    st = jnp.stack
    tr = jnp.transpose
    return (xp, xs.reshape(bs, 1, d),
            st(pm_c), st(pm_n), st(pm_m), st(p_conv), st(p_k), st(p_v), st(p_mk), st(p_mv), st(p_ffn),
            tr(st(s_c), (0, 4, 1, 2, 3)), tr(st(s_n), (0, 3, 1, 2)), tr(st(s_m), (0, 2, 1)),
            tr(st(s_conv), (0, 2, 1, 3)), tr(st(s_k), (0, 1, 4, 2, 3)), tr(st(s_v), (0, 1, 4, 2, 3)),
            st(s_ffn))
```

```python
import functools
import math

import numpy as np
import jax
import jax.numpy as jnp
from jax import lax
from jax.experimental import pallas as pl
from jax.experimental.pallas import tpu as pltpu

F32 = jnp.float32
BF16 = jnp.bfloat16
EPS = 1e-6
NEG_INF = float("-inf")

HEAD_DIM = 64
M_HEADS = 4
M_WIDTH = M_HEADS * HEAD_DIM
C_WIDTH = 256
C_KERNEL = 31
S_Q_HEADS = 8
S_KV_HEADS = 2
S_WIDTH = S_Q_HEADS * HEAD_DIM
S_KV_WIDTH = S_KV_HEADS * HEAD_DIM
WINDOW = 128
N_BUCKETS = 32
MAX_DISTANCE = 128
X_HEADS = 4
X_WIDTH = X_HEADS * HEAD_DIM
FFN_KERNEL = 3
QK_SCALE = HEAD_DIM ** -0.5
LOG2E = math.log2(math.e)

Z_M = 0
Z_C = 4 * M_WIDTH
Z_SQ = Z_C + 2 * C_WIDTH
Z_SKV = Z_SQ + S_WIDTH
Z_G = Z_SKV + 2 * S_KV_WIDTH
LANES = 128
SUBLANES = 8
Z_WIDTH = Z_G + LANES
VMEM_LIMIT = 56 * 1024 * 1024

M_CHUNK = 128
FFN_CHUNK = 256


def _params(*sem):
    return pltpu.CompilerParams(dimension_semantics=sem, vmem_limit_bytes=VMEM_LIMIT)


def _const_spec(shape):
    nd = len(shape)
    return pl.BlockSpec(shape, lambda *_: (0,) * nd, pipeline_mode=pl.Buffered(1))


def _weight_spec(w, l):
    nd = w.ndim - 1
    return pl.BlockSpec((None,) + w.shape[1:], lambda *_: (l,) + (0,) * nd, pipeline_mode=pl.Buffered(1))


def _rms(x, g):
    return x * lax.rsqrt(jnp.mean(x * x, axis=-1, keepdims=True) + EPS) * g


def _sigmoid(x):
    return 1.0 / (1.0 + jnp.exp(-x))


def _log_sigmoid(x):
    return jnp.minimum(x, 0.0) - jnp.log1p(jnp.exp(-jnp.abs(x)))


def _dot(a, b):
    return jnp.dot(a, b, preferred_element_type=F32)


def _dot_nt(a, b):
    return lax.dot_general(a, b, (((1,), (1,)), ((), ())), preferred_element_type=F32)


def _dot_tn(a, b):
    return lax.dot_general(a, b, (((0,), (0,)), ((), ())), preferred_element_type=F32)


IN_PROJ_ROWS = 512
IN_PROJ_COLS = 512
IN_GATES = 4 * M_WIDTH
W_PREP_ROWS = 256


def _in_proj_kernel(x_ref, g_ref, wt_ref, o_ref, wprep_ref, w_scr):
    d = x_ref.shape[1]

    @pl.when(pl.program_id(0) == 0)
    def _():
        n_gate = 2 * M_HEADS
        for src, dst, n in ((0, 0, IN_GATES), (IN_GATES + n_gate, IN_GATES, Z_G - IN_GATES)):
            for c in range(0, n, W_PREP_ROWS):
                w_scr[:, dst + c:dst + c + W_PREP_ROWS] = wt_ref[src + c:src + c + W_PREP_ROWS, :].T.astype(BF16)
        gate_rows = jnp.concatenate([wt_ref[IN_GATES:IN_GATES + n_gate, :], jnp.zeros((LANES - n_gate, d), F32)],
                                    axis=0)
        w_scr[:, Z_G:Z_G + LANES] = gate_rows.T.astype(BF16)
        wprep_ref[...] = w_scr[...]

    tm, n = o_ref.shape
    sub = min(IN_PROJ_ROWS, tm)
    for r0 in range(0, tm, sub):
        h = _rms(x_ref[r0:r0 + sub, :], g_ref[...]).astype(BF16)
        for c in range(0, n, IN_PROJ_COLS):
            w = min(IN_PROJ_COLS, n - c)
            o_ref[r0:r0 + sub, c:c + w] = _dot(h, w_scr[:, c:c + w])


def _in_proj(l, x, g, wt, *, tm):
    m, d = x.shape
    tm = min(tm, m)
    return pl.pallas_call(
        _in_proj_kernel,
        grid=(m // tm,),
        in_specs=[pl.BlockSpec((tm, d), lambda i: (i, 0)), _weight_spec(g, l), _weight_spec(wt, l)],
        out_specs=[pl.BlockSpec((tm, Z_WIDTH), lambda i: (i, 0)), pl.BlockSpec((d, Z_WIDTH), lambda i: (0, 0))],
        out_shape=[jax.ShapeDtypeStruct((m, Z_WIDTH), F32), jax.ShapeDtypeStruct((d, Z_WIDTH), BF16)],
        scratch_shapes=[pltpu.VMEM((d, Z_WIDTH), BF16)],
        compiler_params=_params("arbitrary"),
        name="in_proj",
    )(x, g, wt)


CAST_ROWS = 256


def _cast_kernel(x_ref, o_ref):
    o_ref[...] = x_ref[...].astype(o_ref.dtype)


def _to_bf16(w):
    depth, rows, cols = w.shape
    tr = min(CAST_ROWS, rows)
    spec = pl.BlockSpec((None, tr, cols), lambda l, i: (l, i, 0))
    return pl.pallas_call(
        _cast_kernel,
        grid=(depth, rows // tr),
        in_specs=[spec],
        out_specs=spec,
        out_shape=jax.ShapeDtypeStruct(w.shape, BF16),
        compiler_params=_params("arbitrary", "arbitrary"),
        name="to_bf16",
    )(w)


def _mem_kv_kernel(x_ref, w_ref, o_ref, acc):
    acc[...] = _dot(x_ref[...].astype(BF16), w_ref[...])
    o_ref[...] = acc[...].T


def _mem_kv(l, mem3, w):
    b, mem, d = mem3.shape
    n = w.shape[2]
    return pl.pallas_call(
        _mem_kv_kernel,
        grid=(b,),
        in_specs=[pl.BlockSpec((None, mem, d), lambda i: (i, 0, 0)), _weight_spec(w, l)],
        out_specs=pl.BlockSpec((None, n, mem), lambda i: (i, 0, 0)),
        out_shape=jax.ShapeDtypeStruct((b, n, mem), F32),
        scratch_shapes=[pltpu.VMEM((mem, n), F32)],
        compiler_params=_params("arbitrary"),
        name="mem_kv",
    )(mem3, w)


def _sample_in_kernel(x_ref, g_ref, w_ref, z_ref, zt_ref):
    h = _rms(x_ref[...], g_ref[...]).astype(BF16)
    n = z_ref.shape[1]
    for c in range(0, n, IN_PROJ_COLS):
        w = min(IN_PROJ_COLS, n - c)
        zc = _dot(h, w_ref[:, c:c + w])
        z_ref[:, c:c + w] = zc
        zt_ref[c:c + w, :] = zc.T


def _sample_in(l, x, g, w):
    m, d = x.shape
    n = w.shape[1]
    full = lambda shape: pl.BlockSpec(shape, lambda i: (0,) * len(shape))
    return pl.pallas_call(
        _sample_in_kernel,
        grid=(1,),
        in_specs=[full((m, d)), _weight_spec(g, l), full((d, n))],
        out_specs=[full((m, n)), full((n, m))],
        out_shape=[jax.ShapeDtypeStruct((m, n), F32), jax.ShapeDtypeStruct((n, m), F32)],
        compiler_params=_params("arbitrary"),
        name="sample_in",
    )(x, g, w)


def _mlstm_prompt_kernel(z_ref, g_ref, gb_ref, ng_ref, h_ref, cp_ref, np_ref, m_ref,
                         cp_scr, np_scr, m_scr, ht_scr, st_scr, kw_scr):
    NB, L = z_ref.shape[0], z_ref.shape[1]
    D = HEAD_DIM
    hi = lax.Precision.HIGHEST

    @pl.when(pl.program_id(0) == 0)
    def _():
        cp_scr[...] = jnp.zeros(cp_scr.shape, F32)
        np_scr[...] = jnp.zeros(np_scr.shape, F32)
        m_scr[...] = jnp.zeros(m_scr.shape, F32)

    src = lax.broadcasted_iota(jnp.int32, (L, L), 0)
    qry = lax.broadcasted_iota(jnp.int32, (L, L), 1)
    causal_t = src <= qry
    upper = jnp.where(causal_t, 1.0, 0.0)
    lane_half = lax.broadcasted_iota(jnp.int32, (L, 128), 1) // D
    row8 = lax.broadcasted_iota(jnp.int32, (SUBLANES, 128), 0)

    rows, cols = {}, []
    for b in range(NB):
        g_t = (g_ref[b] + gb_ref[...]).T[0:SUBLANES, :]
        b_rows = jnp.dot(_log_sigmoid(g_t), upper, precision=hi, preferred_element_type=F32)
        to_cols = []
        for h in range(M_HEADS):
            b_row = b_rows[M_HEADS + h:M_HEADS + h + 1, :]
            ci_row = g_t[h:h + 1, :] - b_row
            m_prev = m_scr[b, h, 0:1, 0:1]
            b_last = b_row[:, L - 1:L]
            m_new = jnp.maximum(b_last + m_prev, jnp.max(b_last + ci_row, axis=1, keepdims=True))
            rows[b, h] = dict(b_row=b_row, m_prev=m_prev, m_new=m_new, decay=jnp.exp(b_last + m_prev - m_new))
            to_cols += [ci_row, jnp.exp(b_last + ci_row - m_new)]
        to_cols.append(jnp.zeros((128 - 2 * M_HEADS, L), F32))
        cols.append(jnp.concatenate(to_cols, axis=0).T)

    stats = {}
    for b in range(NB):
        for j in range(M_HEADS // 2):
            slab = 128 * j
            q2 = z_ref[b, :, slab:slab + 128].astype(BF16)
            k2 = z_ref[b, :, M_WIDTH + slab:M_WIDTH + slab + 128] * QK_SCALE
            for par in range(2):
                h = 2 * j + par
                r = rows[b, h]
                ci_col = cols[b][:, 2 * h:2 * h + 1]
                a_row = r["b_row"] + r["m_prev"]
                dm = jnp.where(causal_t, r["b_row"] + ci_col, NEG_INF)
                m_row = jnp.maximum(a_row, jnp.max(dm, axis=0, keepdims=True))
                k_own = jnp.where(lane_half == par, k2, 0.0).astype(BF16)
                s_t = _dot_nt(k_own, q2) * jnp.exp(dm - m_row)
                st_scr[b, j, :, L * par:L * (par + 1)] = s_t.astype(BF16)
                stats[b, h] = dict(w_inter=jnp.exp(a_row - m_row), floor=jnp.exp(-m_row),
                                   den_s=jnp.sum(s_t, axis=0, keepdims=True),
                                   decay=r["decay"], m_new=r["m_new"])
            wk = jnp.where(lane_half == 0, cols[b][:, 4 * j + 1:4 * j + 2], cols[b][:, 4 * j + 3:4 * j + 4])
            kw = k2 * wk
            kw_scr[b, j] = kw.astype(BF16)
            stats[b, j, "k_sum"] = jnp.sum(kw, axis=0, keepdims=True)

    block_diag = (lax.broadcasted_iota(jnp.int32, (128, 128), 0) // D
                  == lax.broadcasted_iota(jnp.int32, (128, 128), 1) // D)
    for b in range(NB):
        for j in range(M_HEADS // 2):
            slab = 128 * j
            q2 = z_ref[b, :, slab:slab + 128].astype(BF16)
            v2 = z_ref[b, :, 2 * M_WIDTH + slab:2 * M_WIDTH + slab + 128].astype(BF16)
            cp = cp_scr[b, j]
            npair = np_scr[b, j]
            qc = _dot_nt(cp.astype(BF16), q2)
            qn = _dot_nt(npair.astype(BF16), q2)
            pv = _dot_tn(v2, st_scr[b, j])
            upd = jnp.where(block_diag, _dot_tn(v2, kw_scr[b, j]), 0.0)
            n_new = jnp.zeros((SUBLANES, 128), F32)
            for par in range(2):
                h = 2 * j + par
                st = stats[b, h]
                hr = slice(D * par, D * (par + 1))
                num = st["w_inter"] * qc[hr, :] + pv[hr, L * par:L * (par + 1)]
                den = st["w_inter"] * qn[par:par + 1, :] + st["den_s"]
                hh = num * (1.0 / jnp.maximum(jnp.abs(den), st["floor"]))
                hh = hh * lax.rsqrt(jnp.mean(hh * hh, axis=0, keepdims=True) + EPS)
                ht_scr[b, D * h:D * (h + 1), :] = hh
                cp_scr[b, j, hr, :] = st["decay"] * cp[hr, :] + upd[hr, :]
                k_sum = jnp.where(lane_half[0:1, :] == par, stats[b, j, "k_sum"], 0.0)
                n_new = jnp.where(row8 == par, st["decay"] * npair[par:par + 1, :] + k_sum, n_new)
                m_scr[b, h] = jnp.broadcast_to(st["m_new"], m_scr.shape[2:])
            np_scr[b, j] = n_new
        o_gate = _sigmoid(z_ref[b, :, 3 * M_WIDTH:4 * M_WIDTH])
        h_ref[b] = ht_scr[b].T * ng_ref[...] * o_gate

    cp_ref[...] = cp_scr[...]
    np_ref[...] = np_scr[...]
    m_ref[...] = m_scr[...]


def _mlstm_prompt(l, z3, gbias, ng):
    b, s, _ = z3.shape
    L = M_CHUNK
    P = M_HEADS // 2
    return pl.pallas_call(
        _mlstm_prompt_kernel,
        grid=(s // L,),
        in_specs=[
            pl.BlockSpec((b, L, 4 * M_WIDTH), lambda c: (0, c, Z_M // (4 * M_WIDTH))),
            pl.BlockSpec((b, L, 128), lambda c: (0, c, Z_G // 128)),
            _weight_spec(gbias, l),
            _weight_spec(ng, l),
        ],
        out_specs=[
            pl.BlockSpec((b, L, M_WIDTH), lambda c: (0, c, 0)),
            pl.BlockSpec((b, P, 128, 128), lambda c: (0, 0, 0, 0)),
            pl.BlockSpec((b, P, SUBLANES, 128), lambda c: (0, 0, 0, 0)),
            pl.BlockSpec((b, M_HEADS, SUBLANES, 128), lambda c: (0, 0, 0, 0)),
        ],
        out_shape=[
            jax.ShapeDtypeStruct((b, s, M_WIDTH), F32),
            jax.ShapeDtypeStruct((b, P, 128, 128), F32),
            jax.ShapeDtypeStruct((b, P, SUBLANES, 128), F32),
            jax.ShapeDtypeStruct((b, M_HEADS, SUBLANES, 128), F32),
        ],
        scratch_shapes=[
            pltpu.VMEM((b, P, 128, 128), F32),
            pltpu.VMEM((b, P, SUBLANES, 128), F32),
            pltpu.VMEM((b, M_HEADS, SUBLANES, 128), F32),
            pltpu.VMEM((b, M_WIDTH, L), F32),
            pltpu.VMEM((b, P, L, 2 * L), BF16),
            pltpu.VMEM((b, P, L, 128), BF16),
        ],
        compiler_params=_params("arbitrary"),
        name="mlstm_prompt",
    )(z3, z3, gbias, ng)


CONV_PAD = 32


def _conv_prompt_kernel(z_ref, w_ref, cb_ref, lg_ref, lb_ref, h_ref, tail_ref, buf, shifted):
    tc = z_ref.shape[0]
    t = pl.program_id(1)

    @pl.when(t == 0)
    def _():
        buf[0:CONV_PAD, :] = jnp.zeros((CONV_PAD, C_WIDTH), F32)

    @pl.when(t > 0)
    def _():
        buf[0:CONV_PAD, :] = buf[tc:tc + CONV_PAD, :]

    u = z_ref[:, 0:C_WIDTH] * _sigmoid(z_ref[:, C_WIDTH:2 * C_WIDTH])
    buf[CONV_PAD:CONV_PAD + tc, :] = u
    n_sh = shifted.shape[1]
    for r in range(1, SUBLANES):
        shifted[r - 1] = buf[r:r + n_sh, :]
    off = CONV_PAD - (C_KERNEL - 1)
    acc = jnp.broadcast_to(cb_ref[...], (tc, C_WIDTH))
    for j in range(C_KERNEL):
        r = (off + j) % SUBLANES
        base = off + j - r
        win = buf[base:base + tc, :] if r == 0 else shifted[r - 1, base:base + tc, :]
        acc = acc + w_ref[j:j + 1, :] * win
    mu = jnp.mean(acc, axis=-1, keepdims=True)
    xc = acc - mu
    y = xc * lax.rsqrt(jnp.mean(xc * xc, axis=-1, keepdims=True) + EPS) * lg_ref[...] + lb_ref[...]
    h_ref[...] = y * _sigmoid(y)
    tail_ref[...] = buf[tc:tc + CONV_PAD, :]


def _conv_prompt(l, z3, w, cb, lg, lb, *, tc):
    b, s, _ = z3.shape
    return pl.pallas_call(
        _conv_prompt_kernel,
        grid=(b, s // tc),
        in_specs=[
            pl.BlockSpec((None, tc, 2 * C_WIDTH), lambda i, t: (i, t, Z_C // (2 * C_WIDTH))),
            _weight_spec(w, l), _weight_spec(cb, l), _weight_spec(lg, l), _weight_spec(lb, l),
        ],
        out_specs=[
            pl.BlockSpec((None, tc, C_WIDTH), lambda i, t: (i, t, 0)),
            pl.BlockSpec((None, CONV_PAD, C_WIDTH), lambda i, t: (i, 0, 0)),
        ],
        out_shape=[
            jax.ShapeDtypeStruct((b, s, C_WIDTH), F32),
            jax.ShapeDtypeStruct((b, CONV_PAD, C_WIDTH), F32),
        ],
        scratch_shapes=[pltpu.VMEM((CONV_PAD + tc, C_WIDTH), F32),
                        pltpu.VMEM((SUBLANES - 1, CONV_PAD + tc - SUBLANES, C_WIDTH), F32)],
        compiler_params=_params("arbitrary", "arbitrary"),
        name="conv_prompt",
    )(z3, w, cb, lg, lb)


def _swa_prompt_kernel(rb_ref, sink_ref, bucket_ref, q_ref, kv_ref, kvp_ref, o_ref, kt_ref, vt_ref,
                       bias_scr, s_scr, p_scr, ot_scr, *, layer):
    W = WINDOW
    H = S_Q_HEADS
    G = H // S_KV_HEADS
    i = pl.program_id(0)
    n = pl.program_id(1)

    @pl.when(jnp.logical_and(i == 0, n == 0))
    def _():
        bucket = bucket_ref[...]
        prev_key = lax.broadcasted_iota(jnp.int32, (2 * W, W), 0) < W
        for h in range(H):
            acc = jnp.full((2 * W, W), NEG_INF, F32)
            for b in range(N_BUCKETS):
                acc = jnp.where(bucket == b, rb_ref[b, h], acc)
            acc = acc * LOG2E
            bias_scr[0, h] = acc
            bias_scr[1, h] = jnp.where(prev_key, NEG_INF, acc)

    QB = q_ref.shape[0] // W
    NK = (QB + 1) * W
    first = jnp.where(n == 0, 1, 0)
    kk = jnp.concatenate([kvp_ref[:, 0:S_KV_WIDTH], kv_ref[:, 0:S_KV_WIDTH]], axis=0) * (QK_SCALE * LOG2E)
    vv = jnp.concatenate([kvp_ref[:, S_KV_WIDTH:2 * S_KV_WIDTH], kv_ref[:, S_KV_WIDTH:2 * S_KV_WIDTH]], axis=0)
    kk_r = pltpu.roll(kk, HEAD_DIM, axis=1)
    lo_lane = lax.broadcasted_iota(jnp.int32, (NK, S_KV_WIDTH), 1) < HEAD_DIM
    k_var = [[jnp.where(lo_lane, kk, 0.0).astype(BF16), jnp.where(lo_lane, 0.0, kk_r).astype(BF16)],
             [jnp.where(lo_lane, kk_r, 0.0).astype(BF16), jnp.where(lo_lane, 0.0, kk).astype(BF16)]]
    v_t = vv.T.astype(BF16)

    for j in range(QB):
        k0 = j * W
        masked = first if j == 0 else 0
        m_rows = [None] * H
        for hk in range(S_KV_HEADS):
            c0 = 2 * 128 * hk
            q_st = jnp.concatenate([q_ref[k0:k0 + W, c0:c0 + 128], q_ref[k0:k0 + W, c0 + 128:c0 + 256]],
                                   axis=0).astype(BF16)
            for half in range(2):
                s_t = _dot_nt(k_var[hk][half][k0:k0 + 2 * W, :], q_st)
                for slab in range(2):
                    head = G * hk + 2 * slab + half
                    sb = s_t[:, 128 * slab:128 * (slab + 1)] + bias_scr[masked, head]
                    s_scr[j, head] = sb
                    m_rows[head] = jnp.maximum(jnp.max(sb, axis=0, keepdims=True), sink_ref[layer, head] * LOG2E)

        inv = [None] * H
        for head in range(H):
            e = jnp.exp2(s_scr[j, head] - m_rows[head])
            den = jnp.sum(e, axis=0, keepdims=True) + jnp.exp2(sink_ref[layer, head] * LOG2E - m_rows[head])
            inv[head] = 1.0 / den
            p_scr[j, :, 128 * head:128 * (head + 1)] = e.astype(BF16)

        for hk in range(S_KV_HEADS):
            o_t = _dot(v_t[HEAD_DIM * hk:HEAD_DIM * (hk + 1), k0:k0 + 2 * W],
                       p_scr[j, :, 128 * G * hk:128 * G * (hk + 1)])
            for g in range(G):
                head = G * hk + g
                ot_scr[j, HEAD_DIM * head:HEAD_DIM * (head + 1), :] = o_t[:, 128 * g:128 * (g + 1)] * inv[head]
        o_ref[k0:k0 + W, :] = ot_scr[j].T

    @pl.when(n == pl.num_programs(1) - 1)
    def _():
        kt_ref[...] = kv_ref[(QB - 1) * W:QB * W, 0:S_KV_WIDTH].T
        vt_ref[...] = kv_ref[(QB - 1) * W:QB * W, S_KV_WIDTH:2 * S_KV_WIDTH].T


def _swa_prompt(l, z3, rel_bias, sinks, bucket, *, qb):
    b, s, _ = z3.shape
    W = WINDOW
    smem = pl.BlockSpec(memory_space=pltpu.SMEM)
    return pl.pallas_call(
        functools.partial(_swa_prompt_kernel, layer=l),
        grid=(b, s // (qb * W)),
        in_specs=[
            smem, smem, _const_spec((2 * W, W)),
            pl.BlockSpec((None, qb * W, S_WIDTH), lambda i, n: (i, n, Z_SQ // S_WIDTH)),
            pl.BlockSpec((None, qb * W, 2 * S_KV_WIDTH), lambda i, n: (i, n, Z_SKV // (2 * S_KV_WIDTH))),
            pl.BlockSpec((None, W, 2 * S_KV_WIDTH),
                         lambda i, n: (i, jnp.maximum(n * qb - 1, 0), Z_SKV // (2 * S_KV_WIDTH))),
        ],
        out_specs=[
            pl.BlockSpec((None, qb * W, S_WIDTH), lambda i, n: (i, n, 0)),
            pl.BlockSpec((None, S_KV_WIDTH, W), lambda i, n: (i, 0, 0)),
            pl.BlockSpec((None, S_KV_WIDTH, W), lambda i, n: (i, 0, 0)),
        ],
        out_shape=[
            jax.ShapeDtypeStruct((b, s, S_WIDTH), F32),
            jax.ShapeDtypeStruct((b, S_KV_WIDTH, W), F32),
            jax.ShapeDtypeStruct((b, S_KV_WIDTH, W), F32),
        ],
        scratch_shapes=[
            pltpu.VMEM((2, S_Q_HEADS, 2 * W, W), F32),
            pltpu.VMEM((qb, S_Q_HEADS, 2 * W, W), F32),
            pltpu.VMEM((qb, 2 * W, S_Q_HEADS * W), BF16),
            pltpu.VMEM((qb, S_WIDTH, W), F32),
        ],
        compiler_params=_params("arbitrary", "arbitrary"),
        name="swa_prompt",
    )(rel_bias, sinks, bucket, z3, z3, z3)


MIX_SUB = 512


def _mix_out_prompt_kernel(x_ref, hm_ref, hc_ref, hs_ref, wout_ref, g2_ref, wq_ref, mk_ref, mv_ref, wo_ref, o_ref,
                           x1_scr, qx_scr, p_scr):
    tm = x_ref.shape[0]
    mem = mk_ref.shape[1]
    mk_t = mk_ref[...]
    mv_t = mv_ref[...]
    row_head = lax.broadcasted_iota(jnp.int32, mk_t.shape, 0) // HEAD_DIM
    k_heads = [jnp.where(row_head == h, mk_t, 0.0).astype(BF16) for h in range(X_HEADS)]
    v_cat = jnp.concatenate([jnp.where(row_head == h, mv_t, 0.0).astype(BF16) for h in range(X_HEADS)], axis=1)
    subs = [slice(r0, r0 + MIX_SUB) for r0 in range(0, tm, MIX_SUB)]
    for rows in subs:
        cat = jnp.concatenate([hm_ref[rows, :].astype(BF16), hc_ref[rows, :].astype(BF16),
                               hs_ref[rows, :].astype(BF16)], axis=1)
        x1 = x_ref[rows, :] + _dot(cat, wout_ref[...])
        x1_scr[rows, :] = x1
        qx = _dot(_rms(x1, g2_ref[...]).astype(BF16), wq_ref[...])
        qx_scr[rows, :] = (qx * (QK_SCALE * LOG2E)).astype(BF16)
    for rows in subs:
        qx = qx_scr[rows, :]
        for h in range(X_HEADS):
            s = _dot(qx, k_heads[h])
            e = jnp.exp2(s - jnp.max(s, axis=1, keepdims=True))
            p_scr[rows, mem * h:mem * (h + 1)] = (e * (1.0 / jnp.sum(e, axis=1, keepdims=True))).astype(BF16)
    for rows in subs:
        o = _dot_nt(p_scr[rows, :], v_cat)
        o_ref[rows, :] = x1_scr[rows, :] + _dot(o.astype(BF16), wo_ref[...])


def _mix_out_prompt(l, x3, hm, hc, hs, wout, g2, wq, mkv, wo, *, tm):
    b, s, d = x3.shape
    tm = min(tm, s)
    mem = mkv.shape[2]
    row = lambda w: pl.BlockSpec((None, tm, w), lambda i, t: (i, t, 0))
    return pl.pallas_call(
        _mix_out_prompt_kernel,
        grid=(b, s // tm),
        in_specs=[
            row(d), row(M_WIDTH), row(C_WIDTH), row(S_WIDTH),
            _weight_spec(wout, l), _weight_spec(g2, l), _weight_spec(wq, l),
            pl.BlockSpec((None, X_WIDTH, mem), lambda i, t: (i, 0, 0)),
            pl.BlockSpec((None, X_WIDTH, mem), lambda i, t: (i, 1, 0)),
            _weight_spec(wo, l),
        ],
        out_specs=row(d),
        out_shape=jax.ShapeDtypeStruct((b, s, d), F32),
        scratch_shapes=[pltpu.VMEM((tm, d), F32), pltpu.VMEM((tm, X_WIDTH), BF16),
                        pltpu.VMEM((tm, X_HEADS * mem), BF16)],
        compiler_params=_params("arbitrary", "arbitrary"),
        name="mix_out_prompt",
    )(x3, hm, hc, hs, wout, g2, wq, mkv, mkv, wo)


FFN_PAD = 8


def _ffn_prompt_kernel(x_ref, g3_ref, wup_ref, cw_ref, cb_ref, wdn_ref, gf_ref, o_ref, tail_ref,
                       gbuf, carry, act, *, final):
    tm = x_ref.shape[0]
    dff = wdn_ref.shape[0]

    @pl.when(pl.program_id(1) == 0)
    def _():
        carry[...] = jnp.zeros(carry.shape, F32)

    x = x_ref[...]
    h = _rms(x, g3_ref[...]).astype(BF16)
    w = FFN_CHUNK
    for c in range(0, dff, w):
        a = _dot(h, wup_ref[:, c:c + w])
        g = _dot(h, wup_ref[:, dff + c:dff + c + w])
        gbuf[0:FFN_PAD, 0:w] = carry[:, c:c + w]
        gbuf[FFN_PAD:FFN_PAD + tm, 0:w] = g
        carry[:, c:c + w] = g[tm - FFN_PAD:tm, :]
        gc = (cw_ref[0:1, c:c + w] * gbuf[FFN_PAD - 2:FFN_PAD - 2 + tm, 0:w]
              + cw_ref[1:2, c:c + w] * gbuf[FFN_PAD - 1:FFN_PAD - 1 + tm, 0:w]
              + cw_ref[2:3, c:c + w] * g + cb_ref[:, c:c + w])
        act[:, c:c + w] = (gc * _sigmoid(gc) * a).astype(BF16)
    y = x + _dot(act[...], wdn_ref[...])
    o_ref[...] = _rms(y, gf_ref[...]) if final else y
    tail_ref[...] = carry[...]


def _ffn_prompt(l, x3, g3, wup, cw, cb, wdn, gf, *, tm, final):
    b, s, d = x3.shape
    tm = min(tm, s)
    dff = wdn.shape[1]
    return pl.pallas_call(
        functools.partial(_ffn_prompt_kernel, final=final),
        grid=(b, s // tm),
        in_specs=[
            pl.BlockSpec((None, tm, d), lambda i, t: (i, t, 0)),
            _weight_spec(g3, l), _weight_spec(wup, l), _weight_spec(cw, l), _weight_spec(cb, l),
            _weight_spec(wdn, l), _const_spec((1, d)),
        ],
        out_specs=[
            pl.BlockSpec((None, tm, d), lambda i, t: (i, t, 0)),
            pl.BlockSpec((None, FFN_PAD, dff), lambda i, t: (i, 0, 0)),
        ],
        out_shape=[jax.ShapeDtypeStruct((b, s, d), F32), jax.ShapeDtypeStruct((b, FFN_PAD, dff), F32)],
        scratch_shapes=[
            pltpu.VMEM((FFN_PAD + tm, FFN_CHUNK), F32),
            pltpu.VMEM((FFN_PAD, dff), F32),
            pltpu.VMEM((tm, dff), BF16),
        ],
        compiler_params=_params("arbitrary", "arbitrary"),
        name="ffn_prompt",
    )(x3, g3, wup, cw, cb, wdn, gf)


SWA_SAMPLE_BLOCK = 32
XATTN_SAMPLE_BLOCK = 16
CONV_SAMPLE_BLOCK = 32


def _mlstm_sample_kernel(gb_ref, q_ref, k_ref, v_ref, o_ref, g_ref, ng_ref, c_ref, n_ref, m_ref,
                         h_ref, co_ref, no_ref, mo_ref, kw_scr, *, layer):
    h = pl.program_id(0)
    i_pre = g_ref[pl.ds(h, 1), :] + gb_ref[layer, h]
    f_pre = g_ref[pl.ds(M_HEADS + h, 1), :] + gb_ref[layer, M_HEADS + h]
    a = _log_sigmoid(f_pre) + m_ref[pl.ds(h, 1), :]
    m_t = jnp.maximum(a, i_pre)
    w_old = jnp.exp(a - m_t)
    w_new = jnp.exp(i_pre - m_t)
    q = q_ref[...]
    k = k_ref[...] * QK_SCALE
    v = v_ref[...]
    n_old = n_ref[...]
    kw_scr[...] = k * w_new

    def body(d, acc):
        c_old = c_ref[d]
        co_ref[d] = w_old * c_old + kw_scr[pl.ds(d, 1), :] * v
        return acc + q_ref[pl.ds(d, 1), :] * c_old

    qc = lax.fori_loop(0, HEAD_DIM, body, jnp.zeros(v.shape, F32), unroll=8)
    s = jnp.sum(q * k, axis=0, keepdims=True) * w_new
    num = w_old * qc + s * v
    den = w_old * jnp.sum(q * n_old, axis=0, keepdims=True) + s
    hh = num / jnp.maximum(jnp.abs(den), jnp.exp(-m_t))
    hh = hh * lax.rsqrt(jnp.mean(hh * hh, axis=0, keepdims=True) + EPS) * ng_ref[...]
    h_ref[...] = hh * _sigmoid(o_ref[...])
    no_ref[...] = w_old * n_old + kw_scr[...]
    mo_ref[pl.ds(h, 1), :] = m_t


def _mlstm_sample(l, zt, gb, ngt, c_all, n_all, m_all):
    bsz = zt.shape[1]
    D = HEAD_DIM
    feat = lambda off: pl.BlockSpec((D, bsz), lambda h: (off // D + h, 0))
    return pl.pallas_call(
        functools.partial(_mlstm_sample_kernel, layer=l),
        grid=(M_HEADS,),
        in_specs=[
            pl.BlockSpec(memory_space=pltpu.SMEM),
            feat(Z_M), feat(Z_M + M_WIDTH), feat(Z_M + 2 * M_WIDTH), feat(Z_M + 3 * M_WIDTH),
            pl.BlockSpec((SUBLANES, bsz), lambda h: (Z_G // SUBLANES, 0)),
            pl.BlockSpec((None, D, bsz), lambda h: (l, h, 0)),
            pl.BlockSpec((None, None, D, D, bsz), lambda h: (l, h, 0, 0, 0)),
            pl.BlockSpec((None, None, D, bsz), lambda h: (l, h, 0, 0)),
            pl.BlockSpec((None, M_HEADS, bsz), lambda h: (l, 0, 0)),
        ],
        out_specs=[
            pl.BlockSpec((D, bsz), lambda h: (h, 0)),
            pl.BlockSpec((None, D, D, bsz), lambda h: (h, 0, 0, 0)),
            pl.BlockSpec((None, D, bsz), lambda h: (h, 0, 0)),
            pl.BlockSpec((M_HEADS, bsz), lambda h: (0, 0)),
        ],
        out_shape=[
            jax.ShapeDtypeStruct((M_WIDTH, bsz), F32),
            jax.ShapeDtypeStruct((M_HEADS, D, D, bsz), F32),
            jax.ShapeDtypeStruct((M_HEADS, D, bsz), F32),
            jax.ShapeDtypeStruct((M_HEADS, bsz), F32),
        ],
        scratch_shapes=[pltpu.VMEM((D, bsz), F32)],
        compiler_params=_params("arbitrary"),
        name="mlstm_sample",
    )(gb, zt, zt, zt, zt, zt, ngt, c_all, n_all, m_all)


def _conv_sample_kernel(z_ref, hist_ref, w_ref, cb_ref, lg_ref, lb_ref, h_ref, hist_out_ref):
    nh = C_KERNEL - 1
    u = z_ref[:, 0:C_WIDTH] * _sigmoid(z_ref[:, C_WIDTH:2 * C_WIDTH])
    acc = cb_ref[...] + w_ref[nh:nh + 1, :] * u
    for j in range(nh):
        acc = acc + w_ref[j:j + 1, :] * hist_ref[j]
    mu = jnp.mean(acc, axis=-1, keepdims=True)
    xc = acc - mu
    y = xc * lax.rsqrt(jnp.mean(xc * xc, axis=-1, keepdims=True) + EPS) * lg_ref[...] + lb_ref[...]
    h_ref[...] = y * _sigmoid(y)
    for j in range(nh - 1):
        hist_out_ref[j] = hist_ref[j + 1]
    hist_out_ref[nh - 1] = u


def _conv_sample(l, z, hist_all, w, cb, lg, lb):
    bsz = z.shape[0]
    nh = hist_all.shape[1]
    R = min(CONV_SAMPLE_BLOCK, bsz)
    return pl.pallas_call(
        _conv_sample_kernel,
        grid=(bsz // R,),
        in_specs=[
            pl.BlockSpec((R, 2 * C_WIDTH), lambda i: (i, Z_C // (2 * C_WIDTH))),
            pl.BlockSpec((None, nh, R, C_WIDTH), lambda i: (l, 0, i, 0)),
            _weight_spec(w, l), _weight_spec(cb, l), _weight_spec(lg, l), _weight_spec(lb, l),
        ],
        out_specs=[pl.BlockSpec((R, C_WIDTH), lambda i: (i, 0)), pl.BlockSpec((nh, R, C_WIDTH), lambda i: (0, i, 0))],
        out_shape=[jax.ShapeDtypeStruct((bsz, C_WIDTH), F32), jax.ShapeDtypeStruct((nh, bsz, C_WIDTH), F32)],
        compiler_params=_params("arbitrary"),
        name="conv_sample",
    )(z, hist_all, w, cb, lg, lb)


def _swa_sample_kernel(q_ref, kvn_ref, kv2_ref, kc_ref, vc_ref, bias_ref, aux_ref, o_ref, ko_ref, vo_ref):
    R = q_ref.shape[0]
    W = kc_ref.shape[2]
    H = S_Q_HEADS
    shape = (R, H, 128)
    row = lax.broadcasted_iota(jnp.int32, shape, 1)
    lane_half = lax.broadcasted_iota(jnp.int32, shape, 2) // HEAD_DIM
    q_half = row % 2
    kv_head = row // (H // S_KV_HEADS)
    qs = jnp.zeros(shape, F32)
    for j in range(H // 2):
        qs = jnp.where(row // 2 == j, q_ref[:, :, 128 * j:128 * (j + 1)], qs)
    q8 = jnp.where(lane_half == kv_head, jnp.where(q_half == kv_head, qs, pltpu.roll(qs, HEAD_DIM, axis=2)), 0.0)
    k_new = kvn_ref[:, :, 0:S_KV_WIDTH]
    v_new = kvn_ref[:, :, S_KV_WIDTH:2 * S_KV_WIDTH]
    s = jnp.einsum("bqd,bdk->bqk", q8.astype(BF16), kc_ref[...].astype(BF16), preferred_element_type=F32) * QK_SCALE
    s = s + bias_ref[...][None]
    s_new = jnp.sum(q8 * k_new, axis=2, keepdims=True) * QK_SCALE + aux_ref[:, 0:1][None]
    sink = aux_ref[:, 1:2][None]
    m = jnp.maximum(jnp.maximum(jnp.max(s, axis=2, keepdims=True), s_new), sink)
    e = jnp.exp(s - m)
    e_new = jnp.exp(s_new - m)
    inv = 1.0 / (jnp.sum(e, axis=2, keepdims=True) + e_new + jnp.exp(sink - m))
    o8 = jnp.einsum("bqk,bdk->bqd", e.astype(BF16), vc_ref[...].astype(BF16), preferred_element_type=F32)
    o8 = (o8 + e_new * v_new) * inv
    o8 = jnp.where(lane_half == q_half, jnp.where(q_half == kv_head, o8, pltpu.roll(o8, HEAD_DIM, axis=2)), 0.0)
    for j in range(H // 2):
        o_ref[:, :, 128 * j:128 * (j + 1)] = jnp.sum(jnp.where(row // 2 == j, o8, 0.0), axis=1, keepdims=True)
    k_cols = kv2_ref[:, 0:S_KV_WIDTH].T
    v_cols = kv2_ref[:, S_KV_WIDTH:2 * S_KV_WIDTH].T
    last = lax.broadcasted_iota(jnp.int32, (S_KV_WIDTH, W), 1) == W - 1
    for r in range(R):
        ko_ref[r] = jnp.where(last, k_cols[:, r:r + 1], pltpu.roll(kc_ref[r], W - 1, axis=1))
        vo_ref[r] = jnp.where(last, v_cols[:, r:r + 1], pltpu.roll(vc_ref[r], W - 1, axis=1))


def _swa_sample(l, z, z3, kc_all, vc_all, bias, aux):
    bsz = z.shape[0]
    R = min(SWA_SAMPLE_BLOCK, bsz)
    W = kc_all.shape[3]
    cache_in = pl.BlockSpec((None, R, S_KV_WIDTH, W), lambda i: (l, i, 0, 0))
    cache_out = pl.BlockSpec((R, S_KV_WIDTH, W), lambda i: (i, 0, 0))
    return pl.pallas_call(
        _swa_sample_kernel,
        grid=(bsz // R,),
        in_specs=[
            pl.BlockSpec((R, 1, S_WIDTH), lambda i: (i, 0, Z_SQ // S_WIDTH)),
            pl.BlockSpec((R, 1, 2 * S_KV_WIDTH), lambda i: (i, 0, Z_SKV // (2 * S_KV_WIDTH))),
            pl.BlockSpec((R, 2 * S_KV_WIDTH), lambda i: (i, Z_SKV // (2 * S_KV_WIDTH))),
            cache_in, cache_in,
            _const_spec((S_Q_HEADS, 128)), _weight_spec(aux, l),
        ],
        out_specs=[pl.BlockSpec((R, 1, S_WIDTH), lambda i: (i, 0, 0)), cache_out, cache_out],
        out_shape=[
            jax.ShapeDtypeStruct((bsz, 1, S_WIDTH), F32),
            jax.ShapeDtypeStruct((bsz, S_KV_WIDTH, W), F32),
            jax.ShapeDtypeStruct((bsz, S_KV_WIDTH, W), F32),
        ],
        compiler_params=_params("arbitrary"),
        name="swa_sample",
    )(z3, z3, z, kc_all, vc_all, bias, aux)


def _mix_out_sample_kernel(x_ref, hmt_ref, hc_ref, hs_ref, wout_ref, g2_ref, wq_ref, x1_ref, q_ref):
    cat = jnp.concatenate([hmt_ref[...].T.astype(BF16), hc_ref[...].astype(BF16), hs_ref[...].astype(BF16)], axis=1)
    x1 = x_ref[...] + _dot(cat, wout_ref[...])
    x1_ref[...] = x1
    q_ref[...] = _dot(_rms(x1, g2_ref[...]).astype(BF16), wq_ref[...])


def _mix_out_sample(l, x, hm, hc, hs, wout, g2, wq):
    bsz, d = x.shape
    full = lambda a: pl.BlockSpec(a.shape, lambda i: (0,) * a.ndim)
    args = (x, hm, hc, hs, wout, g2, wq)
    return pl.pallas_call(
        _mix_out_sample_kernel,
        grid=(1,),
        in_specs=[full(x), full(hm), full(hc), full(hs), _weight_spec(wout, l), _weight_spec(g2, l),
                  _weight_spec(wq, l)],
        out_specs=[pl.BlockSpec((bsz, d), lambda i: (0, 0)), pl.BlockSpec((bsz, X_WIDTH), lambda i: (0, 0))],
        out_shape=[jax.ShapeDtypeStruct((bsz, d), F32), jax.ShapeDtypeStruct((bsz, X_WIDTH), F32)],
        compiler_params=_params("arbitrary"),
        name="mix_out_sample",
    )(*args)


def _xattn_sample_kernel(q_ref, k_ref, v_ref, o_ref):
    R = k_ref.shape[0]
    shape = (R, SUBLANES, X_WIDTH)
    row = lax.broadcasted_iota(jnp.int32, shape, 1)
    lane_head = lax.broadcasted_iota(jnp.int32, shape, 2) // HEAD_DIM
    own = row == lane_head
    q8 = jnp.where(own, jnp.broadcast_to(q_ref[...], shape), 0.0).astype(BF16)
    s = jnp.einsum("bqd,bdk->bqk", q8, k_ref[...].astype(BF16), preferred_element_type=F32) * QK_SCALE
    e = jnp.exp(s - jnp.max(s, axis=2, keepdims=True))
    p = (e / jnp.sum(e, axis=2, keepdims=True)).astype(BF16)
    o8 = jnp.einsum("bqk,bdk->bqd", p, v_ref[...].astype(BF16), preferred_element_type=F32)
    o_ref[...] = jnp.sum(jnp.where(own, o8, 0.0), axis=1, keepdims=True)


def _xattn_sample(l, q3, k_all, v_all):
    _, bsz, w, mem = k_all.shape
    R = min(XATTN_SAMPLE_BLOCK, bsz)
    kv = pl.BlockSpec((None, R, w, mem), lambda i: (l, i, 0, 0))
    qo = pl.BlockSpec((R, 1, w), lambda i: (i, 0, 0))
    return pl.pallas_call(
        _xattn_sample_kernel,
        grid=(bsz // R,),
        in_specs=[qo, kv, kv],
        out_specs=qo,
        out_shape=jax.ShapeDtypeStruct((bsz, 1, w), F32),
        compiler_params=_params("arbitrary"),
        name="xattn_sample",
    )(q3, k_all, v_all)


def _ffn_sample_kernel(x1_ref, ox_ref, wo_ref, g3_ref, wup_ref, cw_ref, cb_ref, wdn_ref, gf_ref, hist_ref,
                       o_ref, hist_out_ref, act, *, final):
    dff = wdn_ref.shape[0]
    x = x1_ref[...] + _dot(ox_ref[...].astype(BF16), wo_ref[...])
    h = _rms(x, g3_ref[...]).astype(BF16)
    for c in range(0, dff, FFN_CHUNK):
        a = _dot(h, wup_ref[:, c:c + FFN_CHUNK])
        g = _dot(h, wup_ref[:, dff + c:dff + c + FFN_CHUNK])
        h1 = hist_ref[:, dff + c:dff + c + FFN_CHUNK]
        gc = (cw_ref[0:1, c:c + FFN_CHUNK] * hist_ref[:, c:c + FFN_CHUNK]
              + cw_ref[1:2, c:c + FFN_CHUNK] * h1
              + cw_ref[2:3, c:c + FFN_CHUNK] * g + cb_ref[:, c:c + FFN_CHUNK])
        act[:, c:c + FFN_CHUNK] = (gc * _sigmoid(gc) * a).astype(BF16)
        hist_out_ref[:, c:c + FFN_CHUNK] = h1
        hist_out_ref[:, dff + c:dff + c + FFN_CHUNK] = g
    y = x + _dot(act[...], wdn_ref[...])
    o_ref[...] = _rms(y, gf_ref[...]) if final else y


def _ffn_sample(l, x1, ox, wo, g3, wup, cw, cb, wdn, gf, hist_all, *, final):
    bsz, d = x1.shape
    dff = wdn.shape[1]
    full = lambda a: pl.BlockSpec(a.shape, lambda i: (0,) * a.ndim)
    hshape = hist_all.shape[1:]
    args = (x1, ox, wo, g3, wup, cw, cb, wdn, gf, hist_all)
    return pl.pallas_call(
        functools.partial(_ffn_sample_kernel, final=final),
        grid=(1,),
        in_specs=[full(x1), full(ox), _weight_spec(wo, l), _weight_spec(g3, l), _weight_spec(wup, l),
                  _weight_spec(cw, l), _weight_spec(cb, l),
                  _weight_spec(wdn, l), full(gf), pl.BlockSpec((None,) + hshape, lambda i: (l, 0, 0))],
        out_specs=[pl.BlockSpec((bsz, d), lambda i: (0, 0)), pl.BlockSpec(hshape, lambda i: (0, 0))],
        out_shape=[jax.ShapeDtypeStruct((bsz, d), F32), jax.ShapeDtypeStruct(hshape, F32)],
        scratch_shapes=[pltpu.VMEM((bsz, dff), BF16)],
        compiler_params=_params("arbitrary"),
        name="ffn_sample",
    )(*args)


def _t5_buckets(dist):
    n = np.maximum(dist, 0)
    max_exact = N_BUCKETS // 2
    nf = np.maximum(n, max_exact).astype(np.float32)
    large = max_exact + (np.log(nf / np.float32(max_exact)) / np.float32(math.log(MAX_DISTANCE / max_exact))
                         * np.float32(N_BUCKETS - max_exact)).astype(np.int32)
    return np.where(n < max_exact, n, np.minimum(large, N_BUCKETS - 1))


def _prompt_buckets():
    W = WINDOW
    dist = np.arange(W)[None, :] + W - np.arange(2 * W)[:, None]
    band = (dist >= 0) & (dist < W)
    return np.where(band, _t5_buckets(dist), -1).astype(np.int32)


def _swa_tables(rel_bias):
    W = WINDOW
    dist_c = W - np.arange(W)
    tab = jnp.transpose(rel_bias[_t5_buckets(dist_c)], (1, 0))
    cache_bias = jnp.where((dist_c < W)[None], tab, NEG_INF)
    return cache_bias, rel_bias[0]


def kernel(x_prompt, x_sample, mem_prompt, state_mlstm_C, state_mlstm_n, state_mlstm_m, state_conv, cache_swa_k, cache_swa_v, cache_mem_k, cache_mem_v, state_ffn_conv, rel_bias, norm1_g, w_in, b_i, b_f, mlstm_norm_g, conv_w, conv_b, conv_ln_g, conv_ln_b, swa_sinks, w_out, norm2_g, w_xq, w_xk, w_xv, w_xo, norm3_g, w_up, ffn_conv_w, ffn_conv_b, w_down, final_norm_g):
    depth = w_in.shape[0]
    bp, seq, d = x_prompt.shape
    bs = x_sample.shape[0]
    mem = mem_prompt.shape[1]
    dff = w_down.shape[1]
    W = WINDOW
    nh = C_KERNEL - 1

    xp = x_prompt
    xs = x_sample.reshape(bs, d)
    gf = final_norm_g.reshape(1, d)
    c_all = jnp.transpose(state_mlstm_C, (0, 2, 3, 4, 1))
    n_all = jnp.transpose(state_mlstm_n, (0, 2, 3, 1))
    m_all = jnp.transpose(state_mlstm_m, (0, 2, 1))
    hist_all = jnp.transpose(state_conv, (0, 2, 1, 3))
    kc_all = jnp.transpose(cache_swa_k, (0, 1, 3, 4, 2)).reshape(depth, bs, S_KV_WIDTH, W)
    vc_all = jnp.transpose(cache_swa_v, (0, 1, 3, 4, 2)).reshape(depth, bs, S_KV_WIDTH, W)
    mk_all = jnp.transpose(cache_mem_k, (0, 1, 3, 4, 2)).reshape(depth, bs, X_WIDTH, mem)
    mv_all = jnp.transpose(cache_mem_v, (0, 1, 3, 4, 2)).reshape(depth, bs, X_WIDTH, mem)
    fh_all = state_ffn_conv.reshape(depth, bs, (FFN_KERNEL - 1) * dff)
    pm_c, pm_n, pm_m, p_conv, p_k, p_v, p_mk, p_mv, p_ffn = ([] for _ in range(9))
    s_c, s_n, s_m, s_conv, s_k, s_v, s_ffn = ([] for _ in range(7))

    win_t = jnp.swapaxes(w_in, 1, 2)
    wout = w_out.astype(BF16)
    wxq = w_xq.astype(BF16)
    wxkv = jnp.concatenate([w_xk, w_xv], axis=2).astype(BF16)
    wxo = w_xo.astype(BF16)
    wup = _to_bf16(w_up)
    wdn = _to_bf16(w_down)
    bucket = jnp.asarray(_prompt_buckets())

    rows = lambda a: a.reshape(depth, 1, -1)
    g1, g2, g3 = rows(norm1_g), rows(norm2_g), rows(norm3_g)
    gb8 = jnp.concatenate([b_i, b_f], axis=1)
    gbias = rows(jnp.concatenate([gb8, jnp.zeros((depth, 128 - 2 * M_HEADS), F32)], axis=1))
    ng = rows(mlstm_norm_g)
    ngt = jnp.broadcast_to(mlstm_norm_g[:, :, None], (depth, M_WIDTH, bs))
    cw = jnp.concatenate([conv_w, jnp.zeros((depth, CONV_PAD - C_KERNEL, C_WIDTH), F32)], axis=1)
    cb, lg, lb = rows(conv_b), rows(conv_ln_g), rows(conv_ln_b)
    fcw = jnp.concatenate([ffn_conv_w, jnp.zeros((depth, SUBLANES - FFN_KERNEL, dff), F32)], axis=1)
    fcb = rows(ffn_conv_b)
    cache_bias, bias0 = _swa_tables(rel_bias)
    aux = jnp.concatenate([jnp.broadcast_to(bias0[None, :, None], (depth, S_Q_HEADS, 1)), swa_sinks[:, :, None],
                           jnp.zeros((depth, S_Q_HEADS, 126), F32)], axis=2)

    for l in range(depth):
        last = l == depth - 1

        mkv = _mem_kv(l, mem_prompt, wxkv)
        z, win = _in_proj(l, xp.reshape(bp * seq, d), g1, win_t, tm=1024)
        z = z.reshape(bp, seq, Z_WIDTH)
        hm, cpair, npair, mm = _mlstm_prompt(l, z, gbias, ng)
        hc, ctail = _conv_prompt(l, z, cw, cb, lg, lb, tc=min(1024, seq))
        hs, kt, vt = _swa_prompt(l, z, rel_bias, swa_sinks, bucket, qb=min(8, seq // W))
        xp = _mix_out_prompt(l, xp, hm, hc, hs, wout, g2, wxq, mkv, wxo, tm=1024)
        xp, ftail = _ffn_prompt(l, xp, g3, wup, fcw, fcb, wdn, gf, tm=512, final=last)
        half = lambda h: slice(HEAD_DIM * (h % 2), HEAD_DIM * (h % 2 + 1))
        pm_c.append(jnp.stack([jnp.swapaxes(cpair[:, h // 2, half(h), half(h)], 1, 2) for h in range(M_HEADS)], axis=1))
        pm_n.append(jnp.stack([npair[:, h // 2, h % 2, half(h)] for h in range(M_HEADS)], axis=1))
        pm_m.append(mm[:, :, 0, 0])
        p_conv.append(ctail[:, CONV_PAD - nh:, :])
        p_k.append(jnp.transpose(kt.reshape(bp, S_KV_HEADS, HEAD_DIM, W), (0, 3, 1, 2)))
        p_v.append(jnp.transpose(vt.reshape(bp, S_KV_HEADS, HEAD_DIM, W), (0, 3, 1, 2)))
        p_mk.append(jnp.transpose(mkv[:, 0:X_WIDTH, :].reshape(bp, X_HEADS, HEAD_DIM, mem), (0, 3, 1, 2)))
        p_mv.append(jnp.transpose(mkv[:, X_WIDTH:, :].reshape(bp, X_HEADS, HEAD_DIM, mem), (0, 3, 1, 2)))
        p_ffn.append(ftail[:, FFN_PAD - (FFN_KERNEL - 1):, :])

        zs, zst = _sample_in(l, xs, g1, win)
        hmt_s, c_new, n_new, m_new = _mlstm_sample(l, zst, gb8, ngt, c_all, n_all, m_all)
        hc_s, conv_new = _conv_sample(l, zs, hist_all, cw, cb, lg, lb)
        hs_s, k_new, v_new = _swa_sample(l, zs, zs.reshape(bs, 1, Z_WIDTH), kc_all, vc_all, cache_bias, aux)
        x1, qx = _mix_out_sample(l, xs, hmt_s, hc_s, hs_s.reshape(bs, S_WIDTH), wout, g2, wxq)
        ox = _xattn_sample(l, qx.reshape(bs, 1, X_WIDTH), mk_all, mv_all)
        xs, ffn_new = _ffn_sample(l, x1, ox.reshape(bs, X_WIDTH), wxo, g3, wup, fcw, fcb, wdn, gf,
                                  fh_all, final=last)
        s_c.append(c_new)
        s_n.append(n_new)
        s_m.append(m_new)
        s_conv.append(conv_new)
        s_k.append(k_new.reshape(bs, S_KV_HEADS, HEAD_DIM, W))
        s_v.append(v_new.reshape(bs, S_KV_HEADS, HEAD_DIM, W))
        s_ffn.append(ffn_new.reshape(bs, FFN_KERNEL - 1, dff))

    st = jnp.stack
    tr = jnp.transpose
    return (xp, xs.reshape(bs, 1, d),
            st(pm_c), st(pm_n), st(pm_m), st(p_conv), st(p_k), st(p_v), st(p_mk), st(p_mv), st(p_ffn),
            tr(st(s_c), (0, 4, 1, 2, 3)), tr(st(s_n), (0, 3, 1, 2)), tr(st(s_m), (0, 2, 1)),
            tr(st(s_conv), (0, 2, 1, 3)), tr(st(s_k), (0, 1, 4, 2, 3)), tr(st(s_v), (0, 1, 4, 2, 3)),
            st(s_ffn))
```

```python
import functools
import math

import numpy as np
import jax
import jax.numpy as jnp
from jax import lax
from jax.experimental import pallas as pl
from jax.experimental.pallas import tpu as pltpu

F32 = jnp.float32
BF16 = jnp.bfloat16
EPS = 1e-6
NEG_INF = float("-inf")

HEAD_DIM = 64
M_HEADS = 4
M_WIDTH = M_HEADS * HEAD_DIM
C_WIDTH = 256
C_KERNEL = 31
S_Q_HEADS = 8
S_KV_HEADS = 2
S_WIDTH = S_Q_HEADS * HEAD_DIM
S_KV_WIDTH = S_KV_HEADS * HEAD_DIM
WINDOW = 128
N_BUCKETS = 32
MAX_DISTANCE = 128
X_HEADS = 4
X_WIDTH = X_HEADS * HEAD_DIM
FFN_KERNEL = 3
QK_SCALE = HEAD_DIM ** -0.5
LOG2E = math.log2(math.e)

Z_M = 0
Z_C = 4 * M_WIDTH
Z_SQ = Z_C + 2 * C_WIDTH
Z_SKV = Z_SQ + S_WIDTH
Z_G = Z_SKV + 2 * S_KV_WIDTH
LANES = 128
SUBLANES = 8
Z_WIDTH = Z_G + LANES
VMEM_LIMIT = 56 * 1024 * 1024

M_CHUNK = 128
FFN_CHUNK = 256


def _params(*sem):
    return pltpu.CompilerParams(dimension_semantics=sem, vmem_limit_bytes=VMEM_LIMIT)


def _const_spec(shape):
    nd = len(shape)
    return pl.BlockSpec(shape, lambda *_: (0,) * nd, pipeline_mode=pl.Buffered(1))


def _weight_spec(w, l):
    nd = w.ndim - 1
    return pl.BlockSpec((None,) + w.shape[1:], lambda *_: (l,) + (0,) * nd, pipeline_mode=pl.Buffered(1))


def _rms(x, g):
    return x * lax.rsqrt(jnp.mean(x * x, axis=-1, keepdims=True) + EPS) * g


def _sigmoid(x):
    return 1.0 / (1.0 + jnp.exp(-x))


def _log_sigmoid(x):
    return jnp.minimum(x, 0.0) - jnp.log1p(jnp.exp(-jnp.abs(x)))


def _dot(a, b):
    return jnp.dot(a, b, preferred_element_type=F32)


def _dot_nt(a, b):
    return lax.dot_general(a, b, (((1,), (1,)), ((), ())), preferred_element_type=F32)


def _dot_tn(a, b):
    return lax.dot_general(a, b, (((0,), (0,)), ((), ())), preferred_element_type=F32)


IN_PROJ_ROWS = 512
IN_PROJ_COLS = 512
IN_GATES = 4 * M_WIDTH
W_PREP_ROWS = 256


def _in_proj_kernel(x_ref, g_ref, wt_ref, o_ref, wprep_ref, w_scr):
    d = x_ref.shape[1]

    @pl.when(pl.program_id(0) == 0)
    def _():
        n_gate = 2 * M_HEADS
        for src, dst, n in ((0, 0, IN_GATES), (IN_GATES + n_gate, IN_GATES, Z_G - IN_GATES)):
            for c in range(0, n, W_PREP_ROWS):
                w_scr[:, dst + c:dst + c + W_PREP_ROWS] = wt_ref[src + c:src + c + W_PREP_ROWS, :].T.astype(BF16)
        gate_rows = jnp.concatenate([wt_ref[IN_GATES:IN_GATES + n_gate, :], jnp.zeros((LANES - n_gate, d), F32)],
                                    axis=0)
        w_scr[:, Z_G:Z_G + LANES] = gate_rows.T.astype(BF16)
        wprep_ref[...] = w_scr[...]

    tm, n = o_ref.shape
    sub = min(IN_PROJ_ROWS, tm)
    for r0 in range(0, tm, sub):
        h = _rms(x_ref[r0:r0 + sub, :], g_ref[...]).astype(BF16)
        for c in range(0, n, IN_PROJ_COLS):
            w = min(IN_PROJ_COLS, n - c)
            o_ref[r0:r0 + sub, c:c + w] = _dot(h, w_scr[:, c:c + w])


def _in_proj(l, x, g, wt, *, tm):
    m, d = x.shape
    tm = min(tm, m)
    return pl.pallas_call(
        _in_proj_kernel,
        grid=(m // tm,),
        in_specs=[pl.BlockSpec((tm, d), lambda i: (i, 0)), _weight_spec(g, l), _weight_spec(wt, l)],
        out_specs=[pl.BlockSpec((tm, Z_WIDTH), lambda i: (i, 0)), pl.BlockSpec((d, Z_WIDTH), lambda i: (0, 0))],
        out_shape=[jax.ShapeDtypeStruct((m, Z_WIDTH), F32), jax.ShapeDtypeStruct((d, Z_WIDTH), BF16)],
        scratch_shapes=[pltpu.VMEM((d, Z_WIDTH), BF16)],
        compiler_params=_params("arbitrary"),
        name="in_proj",
    )(x, g, wt)


def _mem_kv_kernel(x_ref, w_ref, o_ref, acc):
    acc[...] = _dot(x_ref[...].astype(BF16), w_ref[...])
    o_ref[...] = acc[...].T


def _mem_kv(l, mem3, w):
    b, mem, d = mem3.shape
    n = w.shape[2]
    return pl.pallas_call(
        _mem_kv_kernel,
        grid=(b,),
        in_specs=[pl.BlockSpec((None, mem, d), lambda i: (i, 0, 0)), _weight_spec(w, l)],
        out_specs=pl.BlockSpec((None, n, mem), lambda i: (i, 0, 0)),
        out_shape=jax.ShapeDtypeStruct((b, n, mem), F32),
        scratch_shapes=[pltpu.VMEM((mem, n), F32)],
        compiler_params=_params("arbitrary"),
        name="mem_kv",
    )(mem3, w)


def _sample_in_kernel(x_ref, g_ref, w_ref, z_ref, zt_ref):
    h = _rms(x_ref[...], g_ref[...]).astype(BF16)
    n = z_ref.shape[1]
    for c in range(0, n, IN_PROJ_COLS):
        w = min(IN_PROJ_COLS, n - c)
        zc = _dot(h, w_ref[:, c:c + w])
        z_ref[:, c:c + w] = zc
        zt_ref[c:c + w, :] = zc.T


def _sample_in(l, x, g, w):
    m, d = x.shape
    n = w.shape[1]
    full = lambda shape: pl.BlockSpec(shape, lambda i: (0,) * len(shape))
    return pl.pallas_call(
        _sample_in_kernel,
        grid=(1,),
        in_specs=[full((m, d)), _weight_spec(g, l), full((d, n))],
        out_specs=[full((m, n)), full((n, m))],
        out_shape=[jax.ShapeDtypeStruct((m, n), F32), jax.ShapeDtypeStruct((n, m), F32)],
        compiler_params=_params("arbitrary"),
        name="sample_in",
    )(x, g, w)


def _mlstm_prompt_kernel(z_ref, g_ref, gb_ref, ng_ref, h_ref, cp_ref, np_ref, m_ref,
                         cp_scr, np_scr, m_scr, ht_scr, st_scr, kw_scr):
    NB, L = z_ref.shape[0], z_ref.shape[1]
    D = HEAD_DIM
    hi = lax.Precision.HIGHEST

    @pl.when(pl.program_id(0) == 0)
    def _():
        cp_scr[...] = jnp.zeros(cp_scr.shape, F32)
        np_scr[...] = jnp.zeros(np_scr.shape, F32)
        m_scr[...] = jnp.zeros(m_scr.shape, F32)

    src = lax.broadcasted_iota(jnp.int32, (L, L), 0)
    qry = lax.broadcasted_iota(jnp.int32, (L, L), 1)
    causal_t = src <= qry
    upper = jnp.where(causal_t, 1.0, 0.0)
    lane_half = lax.broadcasted_iota(jnp.int32, (L, 128), 1) // D
    row8 = lax.broadcasted_iota(jnp.int32, (SUBLANES, 128), 0)

    rows, cols = {}, []
    for b in range(NB):
        g_t = (g_ref[b] + gb_ref[...]).T[0:SUBLANES, :]
        b_rows = jnp.dot(_log_sigmoid(g_t), upper, precision=hi, preferred_element_type=F32)
        to_cols = []
        for h in range(M_HEADS):
            b_row = b_rows[M_HEADS + h:M_HEADS + h + 1, :]
            ci_row = g_t[h:h + 1, :] - b_row
            m_prev = m_scr[b, h, 0:1, 0:1]
            b_last = b_row[:, L - 1:L]
            m_new = jnp.maximum(b_last + m_prev, jnp.max(b_last + ci_row, axis=1, keepdims=True))
            rows[b, h] = dict(b_row=b_row, m_prev=m_prev, m_new=m_new, decay=jnp.exp(b_last + m_prev - m_new))
            to_cols += [ci_row, jnp.exp(b_last + ci_row - m_new)]
        to_cols.append(jnp.zeros((128 - 2 * M_HEADS, L), F32))
        cols.append(jnp.concatenate(to_cols, axis=0).T)

    stats = {}
    for b in range(NB):
        for j in range(M_HEADS // 2):
            slab = 128 * j
            q2 = z_ref[b, :, slab:slab + 128].astype(BF16)
            k2 = z_ref[b, :, M_WIDTH + slab:M_WIDTH + slab + 128] * QK_SCALE
            for par in range(2):
                h = 2 * j + par
                r = rows[b, h]
                ci_col = cols[b][:, 2 * h:2 * h + 1]
                a_row = r["b_row"] + r["m_prev"]
                dm = jnp.where(causal_t, r["b_row"] + ci_col, NEG_INF)
                m_row = jnp.maximum(a_row, jnp.max(dm, axis=0, keepdims=True))
                k_own = jnp.where(lane_half == par, k2, 0.0).astype(BF16)
                s_t = _dot_nt(k_own, q2) * jnp.exp(dm - m_row)
                st_scr[b, j, :, L * par:L * (par + 1)] = s_t.astype(BF16)
                stats[b, h] = dict(w_inter=jnp.exp(a_row - m_row), floor=jnp.exp(-m_row),
                                   den_s=jnp.sum(s_t, axis=0, keepdims=True),
                                   decay=r["decay"], m_new=r["m_new"])
            wk = jnp.where(lane_half == 0, cols[b][:, 4 * j + 1:4 * j + 2], cols[b][:, 4 * j + 3:4 * j + 4])
            kw = k2 * wk
            kw_scr[b, j] = kw.astype(BF16)
            stats[b, j, "k_sum"] = jnp.sum(kw, axis=0, keepdims=True)

    block_diag = (lax.broadcasted_iota(jnp.int32, (128, 128), 0) // D
                  == lax.broadcasted_iota(jnp.int32, (128, 128), 1) // D)
    for b in range(NB):
        for j in range(M_HEADS // 2):
            slab = 128 * j
            q2 = z_ref[b, :, slab:slab + 128].astype(BF16)
            v2 = z_ref[b, :, 2 * M_WIDTH + slab:2 * M_WIDTH + slab + 128].astype(BF16)
            cp = cp_scr[b, j]
            npair = np_scr[b, j]
            qc = _dot_nt(cp.astype(BF16), q2)
            qn = _dot_nt(npair.astype(BF16), q2)
            pv = _dot_tn(v2, st_scr[b, j])
            upd = jnp.where(block_diag, _dot_tn(v2, kw_scr[b, j]), 0.0)
            n_new = jnp.zeros((SUBLANES, 128), F32)
            for par in range(2):
                h = 2 * j + par
                st = stats[b, h]
                hr = slice(D * par, D * (par + 1))
                num = st["w_inter"] * qc[hr, :] + pv[hr, L * par:L * (par + 1)]
                den = st["w_inter"] * qn[par:par + 1, :] + st["den_s"]
                hh = num * (1.0 / jnp.maximum(jnp.abs(den), st["floor"]))
                hh = hh * lax.rsqrt(jnp.mean(hh * hh, axis=0, keepdims=True) + EPS)
                ht_scr[b, D * h:D * (h + 1), :] = hh
                cp_scr[b, j, hr, :] = st["decay"] * cp[hr, :] + upd[hr, :]
                k_sum = jnp.where(lane_half[0:1, :] == par, stats[b, j, "k_sum"], 0.0)
                n_new = jnp.where(row8 == par, st["decay"] * npair[par:par + 1, :] + k_sum, n_new)
                m_scr[b, h] = jnp.broadcast_to(st["m_new"], m_scr.shape[2:])
            np_scr[b, j] = n_new
        o_gate = _sigmoid(z_ref[b, :, 3 * M_WIDTH:4 * M_WIDTH])
        h_ref[b] = ht_scr[b].T * ng_ref[...] * o_gate

    cp_ref[...] = cp_scr[...]
    np_ref[...] = np_scr[...]
    m_ref[...] = m_scr[...]


def _mlstm_prompt(l, z3, gbias, ng):
    b, s, _ = z3.shape
    L = M_CHUNK
    P = M_HEADS // 2
    return pl.pallas_call(
        _mlstm_prompt_kernel,
        grid=(s // L,),
        in_specs=[
            pl.BlockSpec((b, L, 4 * M_WIDTH), lambda c: (0, c, Z_M // (4 * M_WIDTH))),
            pl.BlockSpec((b, L, 128), lambda c: (0, c, Z_G // 128)),
            _weight_spec(gbias, l),
            _weight_spec(ng, l),
        ],
        out_specs=[
            pl.BlockSpec((b, L, M_WIDTH), lambda c: (0, c, 0)),
            pl.BlockSpec((b, P, 128, 128), lambda c: (0, 0, 0, 0)),
            pl.BlockSpec((b, P, SUBLANES, 128), lambda c: (0, 0, 0, 0)),
            pl.BlockSpec((b, M_HEADS, SUBLANES, 128), lambda c: (0, 0, 0, 0)),
        ],
        out_shape=[
            jax.ShapeDtypeStruct((b, s, M_WIDTH), F32),
            jax.ShapeDtypeStruct((b, P, 128, 128), F32),
            jax.ShapeDtypeStruct((b, P, SUBLANES, 128), F32),
            jax.ShapeDtypeStruct((b, M_HEADS, SUBLANES, 128), F32),
        ],
        scratch_shapes=[
            pltpu.VMEM((b, P, 128, 128), F32),
            pltpu.VMEM((b, P, SUBLANES, 128), F32),
            pltpu.VMEM((b, M_HEADS, SUBLANES, 128), F32),
            pltpu.VMEM((b, M_WIDTH, L), F32),
            pltpu.VMEM((b, P, L, 2 * L), BF16),
            pltpu.VMEM((b, P, L, 128), BF16),
        ],
        compiler_params=_params("arbitrary"),
        name="mlstm_prompt",
    )(z3, z3, gbias, ng)


CONV_PAD = 32


def _conv_prompt_kernel(z_ref, w_ref, cb_ref, lg_ref, lb_ref, h_ref, tail_ref, buf, shifted):
    tc = z_ref.shape[0]
    t = pl.program_id(1)

    @pl.when(t == 0)
    def _():
        buf[0:CONV_PAD, :] = jnp.zeros((CONV_PAD, C_WIDTH), F32)

    @pl.when(t > 0)
    def _():
        buf[0:CONV_PAD, :] = buf[tc:tc + CONV_PAD, :]

    u = z_ref[:, 0:C_WIDTH] * _sigmoid(z_ref[:, C_WIDTH:2 * C_WIDTH])
    buf[CONV_PAD:CONV_PAD + tc, :] = u
    n_sh = shifted.shape[1]
    for r in range(1, SUBLANES):
        shifted[r - 1] = buf[r:r + n_sh, :]
    off = CONV_PAD - (C_KERNEL - 1)
    acc = jnp.broadcast_to(cb_ref[...], (tc, C_WIDTH))
    for j in range(C_KERNEL):
        r = (off + j) % SUBLANES
        base = off + j - r
        win = buf[base:base + tc, :] if r == 0 else shifted[r - 1, base:base + tc, :]
        acc = acc + w_ref[j:j + 1, :] * win
    mu = jnp.mean(acc, axis=-1, keepdims=True)
    xc = acc - mu
    y = xc * lax.rsqrt(jnp.mean(xc * xc, axis=-1, keepdims=True) + EPS) * lg_ref[...] + lb_ref[...]
    h_ref[...] = y * _sigmoid(y)
    tail_ref[...] = buf[tc:tc + CONV_PAD, :]


def _conv_prompt(l, z3, w, cb, lg, lb, *, tc):
    b, s, _ = z3.shape
    return pl.pallas_call(
        _conv_prompt_kernel,
        grid=(b, s // tc),
        in_specs=[
            pl.BlockSpec((None, tc, 2 * C_WIDTH), lambda i, t: (i, t, Z_C // (2 * C_WIDTH))),
            _weight_spec(w, l), _weight_spec(cb, l), _weight_spec(lg, l), _weight_spec(lb, l),
        ],
        out_specs=[
            pl.BlockSpec((None, tc, C_WIDTH), lambda i, t: (i, t, 0)),
            pl.BlockSpec((None, CONV_PAD, C_WIDTH), lambda i, t: (i, 0, 0)),
        ],
        out_shape=[
            jax.ShapeDtypeStruct((b, s, C_WIDTH), F32),
            jax.ShapeDtypeStruct((b, CONV_PAD, C_WIDTH), F32),
        ],
        scratch_shapes=[pltpu.VMEM((CONV_PAD + tc, C_WIDTH), F32),
                        pltpu.VMEM((SUBLANES - 1, CONV_PAD + tc - SUBLANES, C_WIDTH), F32)],
        compiler_params=_params("arbitrary", "arbitrary"),
        name="conv_prompt",
    )(z3, w, cb, lg, lb)


def _swa_prompt_kernel(rb_ref, sink_ref, bucket_ref, q_ref, kv_ref, kvp_ref, o_ref, kt_ref, vt_ref,
                       bias_scr, s_scr, p_scr, ot_scr, *, layer):
    W = WINDOW
    H = S_Q_HEADS
    G = H // S_KV_HEADS
    i = pl.program_id(0)
    n = pl.program_id(1)

    @pl.when(jnp.logical_and(i == 0, n == 0))
    def _():
        bucket = bucket_ref[...]
        prev_key = lax.broadcasted_iota(jnp.int32, (2 * W, W), 0) < W
        for h in range(H):
            acc = jnp.full((2 * W, W), NEG_INF, F32)
            for b in range(N_BUCKETS):
                acc = jnp.where(bucket == b, rb_ref[b, h], acc)
            acc = acc * LOG2E
            bias_scr[0, h] = acc
            bias_scr[1, h] = jnp.where(prev_key, NEG_INF, acc)

    QB = q_ref.shape[0] // W
    NK = (QB + 1) * W
    first = jnp.where(n == 0, 1, 0)
    kk = jnp.concatenate([kvp_ref[:, 0:S_KV_WIDTH], kv_ref[:, 0:S_KV_WIDTH]], axis=0) * (QK_SCALE * LOG2E)
    vv = jnp.concatenate([kvp_ref[:, S_KV_WIDTH:2 * S_KV_WIDTH], kv_ref[:, S_KV_WIDTH:2 * S_KV_WIDTH]], axis=0)
    kk_r = pltpu.roll(kk, HEAD_DIM, axis=1)
    lo_lane = lax.broadcasted_iota(jnp.int32, (NK, S_KV_WIDTH), 1) < HEAD_DIM
    k_var = [[jnp.where(lo_lane, kk, 0.0).astype(BF16), jnp.where(lo_lane, 0.0, kk_r).astype(BF16)],
             [jnp.where(lo_lane, kk_r, 0.0).astype(BF16), jnp.where(lo_lane, 0.0, kk).astype(BF16)]]
    v_t = vv.T.astype(BF16)

    for j in range(QB):
        k0 = j * W
        masked = first if j == 0 else 0
        m_rows = [None] * H
        for hk in range(S_KV_HEADS):
            c0 = 2 * 128 * hk
            q_st = jnp.concatenate([q_ref[k0:k0 + W, c0:c0 + 128], q_ref[k0:k0 + W, c0 + 128:c0 + 256]],
                                   axis=0).astype(BF16)
            for half in range(2):
                s_t = _dot_nt(k_var[hk][half][k0:k0 + 2 * W, :], q_st)
                for slab in range(2):
                    head = G * hk + 2 * slab + half
                    sb = s_t[:, 128 * slab:128 * (slab + 1)] + bias_scr[masked, head]
                    s_scr[j, head] = sb
                    m_rows[head] = jnp.maximum(jnp.max(sb, axis=0, keepdims=True), sink_ref[layer, head] * LOG2E)

        inv = [None] * H
        for head in range(H):
            e = jnp.exp2(s_scr[j, head] - m_rows[head])
            den = jnp.sum(e, axis=0, keepdims=True) + jnp.exp2(sink_ref[layer, head] * LOG2E - m_rows[head])
            inv[head] = 1.0 / den
            p_scr[j, :, 128 * head:128 * (head + 1)] = e.astype(BF16)

        for hk in range(S_KV_HEADS):
            o_t = _dot(v_t[HEAD_DIM * hk:HEAD_DIM * (hk + 1), k0:k0 + 2 * W],
                       p_scr[j, :, 128 * G * hk:128 * G * (hk + 1)])
            for g in range(G):
                head = G * hk + g
                ot_scr[j, HEAD_DIM * head:HEAD_DIM * (head + 1), :] = o_t[:, 128 * g:128 * (g + 1)] * inv[head]
        o_ref[k0:k0 + W, :] = ot_scr[j].T

    @pl.when(n == pl.num_programs(1) - 1)
    def _():
        kt_ref[...] = kv_ref[(QB - 1) * W:QB * W, 0:S_KV_WIDTH].T
        vt_ref[...] = kv_ref[(QB - 1) * W:QB * W, S_KV_WIDTH:2 * S_KV_WIDTH].T


def _swa_prompt(l, z3, rel_bias, sinks, bucket, *, qb):
    b, s, _ = z3.shape
    W = WINDOW
    smem = pl.BlockSpec(memory_space=pltpu.SMEM)
    return pl.pallas_call(
        functools.partial(_swa_prompt_kernel, layer=l),
        grid=(b, s // (qb * W)),
        in_specs=[
            smem, smem, _const_spec((2 * W, W)),
            pl.BlockSpec((None, qb * W, S_WIDTH), lambda i, n: (i, n, Z_SQ // S_WIDTH)),
            pl.BlockSpec((None, qb * W, 2 * S_KV_WIDTH), lambda i, n: (i, n, Z_SKV // (2 * S_KV_WIDTH))),
            pl.BlockSpec((None, W, 2 * S_KV_WIDTH),
                         lambda i, n: (i, jnp.maximum(n * qb - 1, 0), Z_SKV // (2 * S_KV_WIDTH))),
        ],
        out_specs=[
            pl.BlockSpec((None, qb * W, S_WIDTH), lambda i, n: (i, n, 0)),
            pl.BlockSpec((None, S_KV_WIDTH, W), lambda i, n: (i, 0, 0)),
            pl.BlockSpec((None, S_KV_WIDTH, W), lambda i, n: (i, 0, 0)),
        ],
        out_shape=[
            jax.ShapeDtypeStruct((b, s, S_WIDTH), F32),
            jax.ShapeDtypeStruct((b, S_KV_WIDTH, W), F32),
            jax.ShapeDtypeStruct((b, S_KV_WIDTH, W), F32),
        ],
        scratch_shapes=[
            pltpu.VMEM((2, S_Q_HEADS, 2 * W, W), F32),
            pltpu.VMEM((qb, S_Q_HEADS, 2 * W, W), F32),
            pltpu.VMEM((qb, 2 * W, S_Q_HEADS * W), BF16),
            pltpu.VMEM((qb, S_WIDTH, W), F32),
        ],
        compiler_params=_params("arbitrary", "arbitrary"),
        name="swa_prompt",
    )(rel_bias, sinks, bucket, z3, z3, z3)


MIX_SUB = 512


def _mix_out_prompt_kernel(x_ref, hm_ref, hc_ref, hs_ref, wout_ref, g2_ref, wq_ref, mk_ref, mv_ref, wo_ref, o_ref,
                           x1_scr, qx_scr, p_scr):
    tm = x_ref.shape[0]
    mem = mk_ref.shape[1]
    mk_t = mk_ref[...]
    mv_t = mv_ref[...]
    row_head = lax.broadcasted_iota(jnp.int32, mk_t.shape, 0) // HEAD_DIM
    k_heads = [jnp.where(row_head == h, mk_t, 0.0).astype(BF16) for h in range(X_HEADS)]
    v_cat = jnp.concatenate([jnp.where(row_head == h, mv_t, 0.0).astype(BF16) for h in range(X_HEADS)], axis=1)
    subs = [slice(r0, r0 + MIX_SUB) for r0 in range(0, tm, MIX_SUB)]
    for rows in subs:
        cat = jnp.concatenate([hm_ref[rows, :].astype(BF16), hc_ref[rows, :].astype(BF16),
                               hs_ref[rows, :].astype(BF16)], axis=1)
        x1 = x_ref[rows, :] + _dot(cat, wout_ref[...])
        x1_scr[rows, :] = x1
        qx = _dot(_rms(x1, g2_ref[...]).astype(BF16), wq_ref[...])
        qx_scr[rows, :] = (qx * (QK_SCALE * LOG2E)).astype(BF16)
    for rows in subs:
        qx = qx_scr[rows, :]
        for h in range(X_HEADS):
            s = _dot(qx, k_heads[h])
            e = jnp.exp2(s - jnp.max(s, axis=1, keepdims=True))
            p_scr[rows, mem * h:mem * (h + 1)] = (e * (1.0 / jnp.sum(e, axis=1, keepdims=True))).astype(BF16)
    for rows in subs:
        o = _dot_nt(p_scr[rows, :], v_cat)
        o_ref[rows, :] = x1_scr[rows, :] + _dot(o.astype(BF16), wo_ref[...])


def _mix_out_prompt(l, x3, hm, hc, hs, wout, g2, wq, mkv, wo, *, tm):
    b, s, d = x3.shape
    tm = min(tm, s)
    mem = mkv.shape[2]
    row = lambda w: pl.BlockSpec((None, tm, w), lambda i, t: (i, t, 0))
    return pl.pallas_call(
        _mix_out_prompt_kernel,
        grid=(b, s // tm),
        in_specs=[
            row(d), row(M_WIDTH), row(C_WIDTH), row(S_WIDTH),
            _weight_spec(wout, l), _weight_spec(g2, l), _weight_spec(wq, l),
            pl.BlockSpec((None, X_WIDTH, mem), lambda i, t: (i, 0, 0)),
            pl.BlockSpec((None, X_WIDTH, mem), lambda i, t: (i, 1, 0)),
            _weight_spec(wo, l),
        ],
        out_specs=row(d),
        out_shape=jax.ShapeDtypeStruct((b, s, d), F32),
        scratch_shapes=[pltpu.VMEM((tm, d), F32), pltpu.VMEM((tm, X_WIDTH), BF16),
                        pltpu.VMEM((tm, X_HEADS * mem), BF16)],
        compiler_params=_params("arbitrary", "arbitrary"),
        name="mix_out_prompt",
    )(x3, hm, hc, hs, wout, g2, wq, mkv, mkv, wo)


FFN_PAD = 8


def _ffn_prompt_kernel(x_ref, g3_ref, wup_ref, cw_ref, cb_ref, wdn_ref, gf_ref, o_ref, tail_ref,
                       gbuf, carry, act, *, final):
    tm = x_ref.shape[0]
    dff = wdn_ref.shape[0]

    @pl.when(pl.program_id(1) == 0)
    def _():
        carry[...] = jnp.zeros(carry.shape, F32)

    x = x_ref[...]
    h = _rms(x, g3_ref[...]).astype(BF16)
    w = FFN_CHUNK
    for c in range(0, dff, w):
        a = _dot(h, wup_ref[:, c:c + w])
        g = _dot(h, wup_ref[:, dff + c:dff + c + w])
        gbuf[0:FFN_PAD, 0:w] = carry[:, c:c + w]
        gbuf[FFN_PAD:FFN_PAD + tm, 0:w] = g
        carry[:, c:c + w] = g[tm - FFN_PAD:tm, :]
        gc = (cw_ref[0:1, c:c + w] * gbuf[FFN_PAD - 2:FFN_PAD - 2 + tm, 0:w]
              + cw_ref[1:2, c:c + w] * gbuf[FFN_PAD - 1:FFN_PAD - 1 + tm, 0:w]
              + cw_ref[2:3, c:c + w] * g + cb_ref[:, c:c + w])
        act[:, c:c + w] = (gc * _sigmoid(gc) * a).astype(BF16)
    y = x + _dot(act[...], wdn_ref[...])
    o_ref[...] = _rms(y, gf_ref[...]) if final else y
    tail_ref[...] = carry[...]


def _ffn_prompt(l, x3, g3, wup, cw, cb, wdn, gf, *, tm, final):
    b, s, d = x3.shape
    tm = min(tm, s)
    dff = wdn.shape[1]
    return pl.pallas_call(
        functools.partial(_ffn_prompt_kernel, final=final),
        grid=(b, s // tm),
        in_specs=[
            pl.BlockSpec((None, tm, d), lambda i, t: (i, t, 0)),
            _weight_spec(g3, l), _weight_spec(wup, l), _weight_spec(cw, l), _weight_spec(cb, l),
            _weight_spec(wdn, l), _const_spec((1, d)),
        ],
        out_specs=[
            pl.BlockSpec((None, tm, d), lambda i, t: (i, t, 0)),
            pl.BlockSpec((None, FFN_PAD, dff), lambda i, t: (i, 0, 0)),
        ],
        out_shape=[jax.ShapeDtypeStruct((b, s, d), F32), jax.ShapeDtypeStruct((b, FFN_PAD, dff), F32)],
        scratch_shapes=[
            pltpu.VMEM((FFN_PAD + tm, FFN_CHUNK), F32),
            pltpu.VMEM((FFN_PAD, dff), F32),
            pltpu.VMEM((tm, dff), BF16),
        ],
        compiler_params=_params("arbitrary", "arbitrary"),
        name="ffn_prompt",
    )(x3, g3, wup, cw, cb, wdn, gf)


SWA_SAMPLE_BLOCK = 32
XATTN_SAMPLE_BLOCK = 16
CONV_SAMPLE_BLOCK = 32


def _mlstm_sample_kernel(gb_ref, q_ref, k_ref, v_ref, o_ref, g_ref, ng_ref, c_ref, n_ref, m_ref,
                         h_ref, co_ref, no_ref, mo_ref, kw_scr, *, layer):
    h = pl.program_id(0)
    i_pre = g_ref[pl.ds(h, 1), :] + gb_ref[layer, h]
    f_pre = g_ref[pl.ds(M_HEADS + h, 1), :] + gb_ref[layer, M_HEADS + h]
    a = _log_sigmoid(f_pre) + m_ref[pl.ds(h, 1), :]
    m_t = jnp.maximum(a, i_pre)
    w_old = jnp.exp(a - m_t)
    w_new = jnp.exp(i_pre - m_t)
    q = q_ref[...]
    k = k_ref[...] * QK_SCALE
    v = v_ref[...]
    n_old = n_ref[...]
    kw_scr[...] = k * w_new

    def body(d, acc):
        c_old = c_ref[d]
        co_ref[d] = w_old * c_old + kw_scr[pl.ds(d, 1), :] * v
        return acc + q_ref[pl.ds(d, 1), :] * c_old

    qc = lax.fori_loop(0, HEAD_DIM, body, jnp.zeros(v.shape, F32), unroll=8)
    s = jnp.sum(q * k, axis=0, keepdims=True) * w_new
    num = w_old * qc + s * v
    den = w_old * jnp.sum(q * n_old, axis=0, keepdims=True) + s
    hh = num / jnp.maximum(jnp.abs(den), jnp.exp(-m_t))
    hh = hh * lax.rsqrt(jnp.mean(hh * hh, axis=0, keepdims=True) + EPS) * ng_ref[...]
    h_ref[...] = hh * _sigmoid(o_ref[...])
    no_ref[...] = w_old * n_old + kw_scr[...]
    mo_ref[pl.ds(h, 1), :] = m_t


def _mlstm_sample(l, zt, gb, ngt, c_all, n_all, m_all):
    bsz = zt.shape[1]
    D = HEAD_DIM
    feat = lambda off: pl.BlockSpec((D, bsz), lambda h: (off // D + h, 0))
    return pl.pallas_call(
        functools.partial(_mlstm_sample_kernel, layer=l),
        grid=(M_HEADS,),
        in_specs=[
            pl.BlockSpec(memory_space=pltpu.SMEM),
            feat(Z_M), feat(Z_M + M_WIDTH), feat(Z_M + 2 * M_WIDTH), feat(Z_M + 3 * M_WIDTH),
            pl.BlockSpec((SUBLANES, bsz), lambda h: (Z_G // SUBLANES, 0)),
            pl.BlockSpec((None, D, bsz), lambda h: (l, h, 0)),
            pl.BlockSpec((None, None, D, D, bsz), lambda h: (l, h, 0, 0, 0)),
            pl.BlockSpec((None, None, D, bsz), lambda h: (l, h, 0, 0)),
            pl.BlockSpec((None, M_HEADS, bsz), lambda h: (l, 0, 0)),
        ],
        out_specs=[
            pl.BlockSpec((D, bsz), lambda h: (h, 0)),
            pl.BlockSpec((None, D, D, bsz), lambda h: (h, 0, 0, 0)),
            pl.BlockSpec((None, D, bsz), lambda h: (h, 0, 0)),
            pl.BlockSpec((M_HEADS, bsz), lambda h: (0, 0)),
        ],
        out_shape=[
            jax.ShapeDtypeStruct((M_WIDTH, bsz), F32),
            jax.ShapeDtypeStruct((M_HEADS, D, D, bsz), F32),
            jax.ShapeDtypeStruct((M_HEADS, D, bsz), F32),
            jax.ShapeDtypeStruct((M_HEADS, bsz), F32),
        ],
        scratch_shapes=[pltpu.VMEM((D, bsz), F32)],
        compiler_params=_params("arbitrary"),
        name="mlstm_sample",
    )(gb, zt, zt, zt, zt, zt, ngt, c_all, n_all, m_all)


def _conv_sample_kernel(z_ref, hist_ref, w_ref, cb_ref, lg_ref, lb_ref, h_ref, hist_out_ref):
    nh = C_KERNEL - 1
    u = z_ref[:, 0:C_WIDTH] * _sigmoid(z_ref[:, C_WIDTH:2 * C_WIDTH])
    acc = cb_ref[...] + w_ref[nh:nh + 1, :] * u
    for j in range(nh):
        acc = acc + w_ref[j:j + 1, :] * hist_ref[j]
    mu = jnp.mean(acc, axis=-1, keepdims=True)
    xc = acc - mu
    y = xc * lax.rsqrt(jnp.mean(xc * xc, axis=-1, keepdims=True) + EPS) * lg_ref[...] + lb_ref[...]
    h_ref[...] = y * _sigmoid(y)
    for j in range(nh - 1):
        hist_out_ref[j] = hist_ref[j + 1]
    hist_out_ref[nh - 1] = u


def _conv_sample(l, z, hist_all, w, cb, lg, lb):
    bsz = z.shape[0]
    nh = hist_all.shape[1]
    R = min(CONV_SAMPLE_BLOCK, bsz)
    return pl.pallas_call(
        _conv_sample_kernel,
        grid=(bsz // R,),
        in_specs=[
            pl.BlockSpec((R, 2 * C_WIDTH), lambda i: (i, Z_C // (2 * C_WIDTH))),
            pl.BlockSpec((None, nh, R, C_WIDTH), lambda i: (l, 0, i, 0)),
            _weight_spec(w, l), _weight_spec(cb, l), _weight_spec(lg, l), _weight_spec(lb, l),
        ],
        out_specs=[pl.BlockSpec((R, C_WIDTH), lambda i: (i, 0)), pl.BlockSpec((nh, R, C_WIDTH), lambda i: (0, i, 0))],
        out_shape=[jax.ShapeDtypeStruct((bsz, C_WIDTH), F32), jax.ShapeDtypeStruct((nh, bsz, C_WIDTH), F32)],
        compiler_params=_params("arbitrary"),
        name="conv_sample",
    )(z, hist_all, w, cb, lg, lb)


def _swa_sample_kernel(q_ref, kvn_ref, kv2_ref, kc_ref, vc_ref, bias_ref, aux_ref, o_ref, ko_ref, vo_ref):
    R = q_ref.shape[0]
    W = kc_ref.shape[2]
    H = S_Q_HEADS
    shape = (R, H, 128)
    row = lax.broadcasted_iota(jnp.int32, shape, 1)
    lane_half = lax.broadcasted_iota(jnp.int32, shape, 2) // HEAD_DIM
    q_half = row % 2
    kv_head = row // (H // S_KV_HEADS)
    qs = jnp.zeros(shape, F32)
    for j in range(H // 2):
        qs = jnp.where(row // 2 == j, q_ref[:, :, 128 * j:128 * (j + 1)], qs)
    q8 = jnp.where(lane_half == kv_head, jnp.where(q_half == kv_head, qs, pltpu.roll(qs, HEAD_DIM, axis=2)), 0.0)
    k_new = kvn_ref[:, :, 0:S_KV_WIDTH]
    v_new = kvn_ref[:, :, S_KV_WIDTH:2 * S_KV_WIDTH]
    s = jnp.einsum("bqd,bdk->bqk", q8.astype(BF16), kc_ref[...].astype(BF16), preferred_element_type=F32) * QK_SCALE
    s = s + bias_ref[...][None]
    s_new = jnp.sum(q8 * k_new, axis=2, keepdims=True) * QK_SCALE + aux_ref[:, 0:1][None]
    sink = aux_ref[:, 1:2][None]
    m = jnp.maximum(jnp.maximum(jnp.max(s, axis=2, keepdims=True), s_new), sink)
    e = jnp.exp(s - m)
    e_new = jnp.exp(s_new - m)
    inv = 1.0 / (jnp.sum(e, axis=2, keepdims=True) + e_new + jnp.exp(sink - m))
    o8 = jnp.einsum("bqk,bdk->bqd", e.astype(BF16), vc_ref[...].astype(BF16), preferred_element_type=F32)
    o8 = (o8 + e_new * v_new) * inv
    o8 = jnp.where(lane_half == q_half, jnp.where(q_half == kv_head, o8, pltpu.roll(o8, HEAD_DIM, axis=2)), 0.0)
    for j in range(H // 2):
        o_ref[:, :, 128 * j:128 * (j + 1)] = jnp.sum(jnp.where(row // 2 == j, o8, 0.0), axis=1, keepdims=True)
    k_cols = kv2_ref[:, 0:S_KV_WIDTH].T
    v_cols = kv2_ref[:, S_KV_WIDTH:2 * S_KV_WIDTH].T
    last = lax.broadcasted_iota(jnp.int32, (S_KV_WIDTH, W), 1) == W - 1
    for r in range(R):
        ko_ref[r] = jnp.where(last, k_cols[:, r:r + 1], pltpu.roll(kc_ref[r], W - 1, axis=1))
        vo_ref[r] = jnp.where(last, v_cols[:, r:r + 1], pltpu.roll(vc_ref[r], W - 1, axis=1))


def _swa_sample(l, z, z3, kc_all, vc_all, bias, aux):
    bsz = z.shape[0]
    R = min(SWA_SAMPLE_BLOCK, bsz)
    W = kc_all.shape[3]
    cache_in = pl.BlockSpec((None, R, S_KV_WIDTH, W), lambda i: (l, i, 0, 0))
    cache_out = pl.BlockSpec((R, S_KV_WIDTH, W), lambda i: (i, 0, 0))
    return pl.pallas_call(
        _swa_sample_kernel,
        grid=(bsz // R,),
        in_specs=[
            pl.BlockSpec((R, 1, S_WIDTH), lambda i: (i, 0, Z_SQ // S_WIDTH)),
            pl.BlockSpec((R, 1, 2 * S_KV_WIDTH), lambda i: (i, 0, Z_SKV // (2 * S_KV_WIDTH))),
            pl.BlockSpec((R, 2 * S_KV_WIDTH), lambda i: (i, Z_SKV // (2 * S_KV_WIDTH))),
            cache_in, cache_in,
            _const_spec((S_Q_HEADS, 128)), _weight_spec(aux, l),
        ],
        out_specs=[pl.BlockSpec((R, 1, S_WIDTH), lambda i: (i, 0, 0)), cache_out, cache_out],
        out_shape=[
            jax.ShapeDtypeStruct((bsz, 1, S_WIDTH), F32),
            jax.ShapeDtypeStruct((bsz, S_KV_WIDTH, W), F32),
            jax.ShapeDtypeStruct((bsz, S_KV_WIDTH, W), F32),
        ],
        compiler_params=_params("arbitrary"),
        name="swa_sample",
    )(z3, z3, z, kc_all, vc_all, bias, aux)


def _mix_out_sample_kernel(x_ref, hmt_ref, hc_ref, hs_ref, wout_ref, g2_ref, wq_ref, x1_ref, q_ref):
    cat = jnp.concatenate([hmt_ref[...].T.astype(BF16), hc_ref[...].astype(BF16), hs_ref[...].astype(BF16)], axis=1)
    x1 = x_ref[...] + _dot(cat, wout_ref[...])
    x1_ref[...] = x1
    q_ref[...] = _dot(_rms(x1, g2_ref[...]).astype(BF16), wq_ref[...])


def _mix_out_sample(l, x, hm, hc, hs, wout, g2, wq):
    bsz, d = x.shape
    full = lambda a: pl.BlockSpec(a.shape, lambda i: (0,) * a.ndim)
    args = (x, hm, hc, hs, wout, g2, wq)
    return pl.pallas_call(
        _mix_out_sample_kernel,
        grid=(1,),
        in_specs=[full(x), full(hm), full(hc), full(hs), _weight_spec(wout, l), _weight_spec(g2, l),
                  _weight_spec(wq, l)],
        out_specs=[pl.BlockSpec((bsz, d), lambda i: (0, 0)), pl.BlockSpec((bsz, X_WIDTH), lambda i: (0, 0))],
        out_shape=[jax.ShapeDtypeStruct((bsz, d), F32), jax.ShapeDtypeStruct((bsz, X_WIDTH), F32)],
        compiler_params=_params("arbitrary"),
        name="mix_out_sample",
    )(*args)


def _xattn_sample_kernel(q_ref, k_ref, v_ref, o_ref):
    R = k_ref.shape[0]
    shape = (R, SUBLANES, X_WIDTH)
    row = lax.broadcasted_iota(jnp.int32, shape, 1)
    lane_head = lax.broadcasted_iota(jnp.int32, shape, 2) // HEAD_DIM
    own = row == lane_head
    q8 = jnp.where(own, jnp.broadcast_to(q_ref[...], shape), 0.0).astype(BF16)
    s = jnp.einsum("bqd,bdk->bqk", q8, k_ref[...].astype(BF16), preferred_element_type=F32) * QK_SCALE
    e = jnp.exp(s - jnp.max(s, axis=2, keepdims=True))
    p = (e / jnp.sum(e, axis=2, keepdims=True)).astype(BF16)
    o8 = jnp.einsum("bqk,bdk->bqd", p, v_ref[...].astype(BF16), preferred_element_type=F32)
    o_ref[...] = jnp.sum(jnp.where(own, o8, 0.0), axis=1, keepdims=True)


def _xattn_sample(l, q3, k_all, v_all):
    _, bsz, w, mem = k_all.shape
    R = min(XATTN_SAMPLE_BLOCK, bsz)
    kv = pl.BlockSpec((None, R, w, mem), lambda i: (l, i, 0, 0))
    qo = pl.BlockSpec((R, 1, w), lambda i: (i, 0, 0))
    return pl.pallas_call(
        _xattn_sample_kernel,
        grid=(bsz // R,),
        in_specs=[qo, kv, kv],
        out_specs=qo,
        out_shape=jax.ShapeDtypeStruct((bsz, 1, w), F32),
        compiler_params=_params("arbitrary"),
        name="xattn_sample",
    )(q3, k_all, v_all)


def _ffn_sample_kernel(x1_ref, ox_ref, wo_ref, g3_ref, wup_ref, cw_ref, cb_ref, wdn_ref, gf_ref, hist_ref,
                       o_ref, hist_out_ref, act, *, final):
    dff = wdn_ref.shape[0]
    x = x1_ref[...] + _dot(ox_ref[...].astype(BF16), wo_ref[...])
    h = _rms(x, g3_ref[...]).astype(BF16)
    for c in range(0, dff, FFN_CHUNK):
        a = _dot(h, wup_ref[:, c:c + FFN_CHUNK])
        g = _dot(h, wup_ref[:, dff + c:dff + c + FFN_CHUNK])
        h1 = hist_ref[:, 1, c:c + FFN_CHUNK]
        gc = (cw_ref[0:1, c:c + FFN_CHUNK] * hist_ref[:, 0, c:c + FFN_CHUNK]
              + cw_ref[1:2, c:c + FFN_CHUNK] * h1
              + cw_ref[2:3, c:c + FFN_CHUNK] * g + cb_ref[:, c:c + FFN_CHUNK])
        act[:, c:c + FFN_CHUNK] = (gc * _sigmoid(gc) * a).astype(BF16)
        hist_out_ref[:, 0, c:c + FFN_CHUNK] = h1
        hist_out_ref[:, 1, c:c + FFN_CHUNK] = g
    y = x + _dot(act[...], wdn_ref[...])
    o_ref[...] = _rms(y, gf_ref[...]) if final else y


def _ffn_sample(l, x1, ox, wo, g3, wup, cw, cb, wdn, gf, hist_all, *, final):
    bsz, d = x1.shape
    dff = wdn.shape[1]
    full = lambda a: pl.BlockSpec(a.shape, lambda i: (0,) * a.ndim)
    hshape = hist_all.shape[1:]
    args = (x1, ox, wo, g3, wup, cw, cb, wdn, gf, hist_all)
    return pl.pallas_call(
        functools.partial(_ffn_sample_kernel, final=final),
        grid=(1,),
        in_specs=[full(x1), full(ox), _weight_spec(wo, l), _weight_spec(g3, l), _weight_spec(wup, l),
                  _weight_spec(cw, l), _weight_spec(cb, l),
                  _weight_spec(wdn, l), full(gf), pl.BlockSpec((None,) + hshape, lambda i: (l, 0, 0, 0))],
        out_specs=[pl.BlockSpec((bsz, d), lambda i: (0, 0)), pl.BlockSpec(hshape, lambda i: (0, 0, 0))],
        out_shape=[jax.ShapeDtypeStruct((bsz, d), F32), jax.ShapeDtypeStruct(hshape, F32)],
        scratch_shapes=[pltpu.VMEM((bsz, dff), BF16)],
        compiler_params=_params("arbitrary"),
        name="ffn_sample",
    )(*args)


def _t5_buckets(dist):
    n = np.maximum(dist, 0)
    max_exact = N_BUCKETS // 2
    nf = np.maximum(n, max_exact).astype(np.float32)
    large = max_exact + (np.log(nf / np.float32(max_exact)) / np.float32(math.log(MAX_DISTANCE / max_exact))
                         * np.float32(N_BUCKETS - max_exact)).astype(np.int32)
    return np.where(n < max_exact, n, np.minimum(large, N_BUCKETS - 1))


def _prompt_buckets():
    W = WINDOW
    dist = np.arange(W)[None, :] + W - np.arange(2 * W)[:, None]
    band = (dist >= 0) & (dist < W)
    return np.where(band, _t5_buckets(dist), -1).astype(np.int32)


def _swa_tables(rel_bias):
    W = WINDOW
    dist_c = W - np.arange(W)
    tab = jnp.transpose(rel_bias[_t5_buckets(dist_c)], (1, 0))
    cache_bias = jnp.where((dist_c < W)[None], tab, NEG_INF)
    return cache_bias, rel_bias[0]


def kernel(x_prompt, x_sample, mem_prompt, state_mlstm_C, state_mlstm_n, state_mlstm_m, state_conv, cache_swa_k, cache_swa_v, cache_mem_k, cache_mem_v, state_ffn_conv, rel_bias, norm1_g, w_in, b_i, b_f, mlstm_norm_g, conv_w, conv_b, conv_ln_g, conv_ln_b, swa_sinks, w_out, norm2_g, w_xq, w_xk, w_xv, w_xo, norm3_g, w_up, ffn_conv_w, ffn_conv_b, w_down, final_norm_g):
    depth = w_in.shape[0]
    bp, seq, d = x_prompt.shape
    bs = x_sample.shape[0]
    mem = mem_prompt.shape[1]
    dff = w_down.shape[1]
    W = WINDOW
    nh = C_KERNEL - 1

    xp = x_prompt
    xs = x_sample.reshape(bs, d)
    gf = final_norm_g.reshape(1, d)
    c_all = jnp.transpose(state_mlstm_C, (0, 2, 3, 4, 1))
    n_all = jnp.transpose(state_mlstm_n, (0, 2, 3, 1))
    m_all = jnp.transpose(state_mlstm_m, (0, 2, 1))
    hist_all = jnp.transpose(state_conv, (0, 2, 1, 3))
    kc_all = jnp.transpose(cache_swa_k, (0, 1, 3, 4, 2)).reshape(depth, bs, S_KV_WIDTH, W)
    vc_all = jnp.transpose(cache_swa_v, (0, 1, 3, 4, 2)).reshape(depth, bs, S_KV_WIDTH, W)
    mk_all = jnp.transpose(cache_mem_k, (0, 1, 3, 4, 2)).reshape(depth, bs, X_WIDTH, mem)
    mv_all = jnp.transpose(cache_mem_v, (0, 1, 3, 4, 2)).reshape(depth, bs, X_WIDTH, mem)
    pm_c, pm_n, pm_m, p_conv, p_k, p_v, p_mk, p_mv, p_ffn = ([] for _ in range(9))
    s_c, s_n, s_m, s_conv, s_k, s_v, s_ffn = ([] for _ in range(7))

    win_t = jnp.swapaxes(w_in, 1, 2)
    wout = w_out.astype(BF16)
    wxq = w_xq.astype(BF16)
    wxkv = jnp.concatenate([w_xk, w_xv], axis=2).astype(BF16)
    wxo = w_xo.astype(BF16)
    wup = w_up.astype(BF16)
    wdn = w_down.astype(BF16)
    bucket = jnp.asarray(_prompt_buckets())

    rows = lambda a: a.reshape(depth, 1, -1)
    g1, g2, g3 = rows(norm1_g), rows(norm2_g), rows(norm3_g)
    gb8 = jnp.concatenate([b_i, b_f], axis=1)
    gbias = rows(jnp.concatenate([gb8, jnp.zeros((depth, 128 - 2 * M_HEADS), F32)], axis=1))
    ng = rows(mlstm_norm_g)
    ngt = jnp.broadcast_to(mlstm_norm_g[:, :, None], (depth, M_WIDTH, bs))
    cw = jnp.concatenate([conv_w, jnp.zeros((depth, CONV_PAD - C_KERNEL, C_WIDTH), F32)], axis=1)
    cb, lg, lb = rows(conv_b), rows(conv_ln_g), rows(conv_ln_b)
    fcw = jnp.concatenate([ffn_conv_w, jnp.zeros((depth, SUBLANES - FFN_KERNEL, dff), F32)], axis=1)
    fcb = rows(ffn_conv_b)
    cache_bias, bias0 = _swa_tables(rel_bias)
    aux = jnp.concatenate([jnp.broadcast_to(bias0[None, :, None], (depth, S_Q_HEADS, 1)), swa_sinks[:, :, None],
                           jnp.zeros((depth, S_Q_HEADS, 126), F32)], axis=2)

    for l in range(depth):
        last = l == depth - 1

        mkv = _mem_kv(l, mem_prompt, wxkv)
        z, win = _in_proj(l, xp.reshape(bp * seq, d), g1, win_t, tm=1024)
        z = z.reshape(bp, seq, Z_WIDTH)
        hm, cpair, npair, mm = _mlstm_prompt(l, z, gbias, ng)
        hc, ctail = _conv_prompt(l, z, cw, cb, lg, lb, tc=min(1024, seq))
        hs, kt, vt = _swa_prompt(l, z, rel_bias, swa_sinks, bucket, qb=min(8, seq // W))
        xp = _mix_out_prompt(l, xp, hm, hc, hs, wout, g2, wxq, mkv, wxo, tm=1024)
        xp, ftail = _ffn_prompt(l, xp, g3, wup, fcw, fcb, wdn, gf, tm=512, final=last)
        half = lambda h: slice(HEAD_DIM * (h % 2), HEAD_DIM * (h % 2 + 1))
        pm_c.append(jnp.stack([jnp.swapaxes(cpair[:, h // 2, half(h), half(h)], 1, 2) for h in range(M_HEADS)], axis=1))
        pm_n.append(jnp.stack([npair[:, h // 2, h % 2, half(h)] for h in range(M_HEADS)], axis=1))
        pm_m.append(mm[:, :, 0, 0])
        p_conv.append(ctail[:, CONV_PAD - nh:, :])
        p_k.append(jnp.transpose(kt.reshape(bp, S_KV_HEADS, HEAD_DIM, W), (0, 3, 1, 2)))
        p_v.append(jnp.transpose(vt.reshape(bp, S_KV_HEADS, HEAD_DIM, W), (0, 3, 1, 2)))
        p_mk.append(jnp.transpose(mkv[:, 0:X_WIDTH, :].reshape(bp, X_HEADS, HEAD_DIM, mem), (0, 3, 1, 2)))
        p_mv.append(jnp.transpose(mkv[:, X_WIDTH:, :].reshape(bp, X_HEADS, HEAD_DIM, mem), (0, 3, 1, 2)))
        p_ffn.append(ftail[:, FFN_PAD - (FFN_KERNEL - 1):, :])

        zs, zst = _sample_in(l, xs, g1, win)
        hmt_s, c_new, n_new, m_new = _mlstm_sample(l, zst, gb8, ngt, c_all, n_all, m_all)
        hc_s, conv_new = _conv_sample(l, zs, hist_all, cw, cb, lg, lb)
        hs_s, k_new, v_new = _swa_sample(l, zs, zs.reshape(bs, 1, Z_WIDTH), kc_all, vc_all, cache_bias, aux)
        x1, qx = _mix_out_sample(l, xs, hmt_s, hc_s, hs_s.reshape(bs, S_WIDTH), wout, g2, wxq)
        ox = _xattn_sample(l, qx.reshape(bs, 1, X_WIDTH), mk_all, mv_all)
        xs, ffn_new = _ffn_sample(l, x1, ox.reshape(bs, X_WIDTH), wxo, g3, wup, fcw, fcb, wdn, gf,
                                  state_ffn_conv, final=last)
        s_c.append(c_new)
        s_n.append(n_new)
        s_m.append(m_new)
        s_conv.append(conv_new)
        s_k.append(k_new.reshape(bs, S_KV_HEADS, HEAD_DIM, W))
        s_v.append(v_new.reshape(bs, S_KV_HEADS, HEAD_DIM, W))
        s_ffn.append(ffn_new)

    st = jnp.stack
    tr = jnp.transpose
    return (xp, xs.reshape(bs, 1, d),
            st(pm_c), st(pm_n), st(pm_m), st(p_conv), st(p_k), st(p_v), st(p_mk), st(p_mv), st(p_ffn),
            tr(st(s_c), (0, 4, 1, 2, 3)), tr(st(s_n), (0, 3, 1, 2)), tr(st(s_m), (0, 2, 1)),
            tr(st(s_conv), (0, 2, 1, 3)), tr(st(s_k), (0, 1, 4, 2, 3)), tr(st(s_v), (0, 1, 4, 2, 3)),
            st(s_ffn))
```

```python
import functools
import math

import numpy as np
import jax
import jax.numpy as jnp
from jax import lax
from jax.experimental import pallas as pl
from jax.experimental.pallas import tpu as pltpu

F32 = jnp.float32
BF16 = jnp.bfloat16
EPS = 1e-6
NEG_INF = float("-inf")

HEAD_DIM = 64
M_HEADS = 4
M_WIDTH = M_HEADS * HEAD_DIM
C_WIDTH = 256
C_KERNEL = 31
S_Q_HEADS = 8
S_KV_HEADS = 2
S_WIDTH = S_Q_HEADS * HEAD_DIM
S_KV_WIDTH = S_KV_HEADS * HEAD_DIM
WINDOW = 128
N_BUCKETS = 32
MAX_DISTANCE = 128
X_HEADS = 4
X_WIDTH = X_HEADS * HEAD_DIM
FFN_KERNEL = 3
QK_SCALE = HEAD_DIM ** -0.5
LOG2E = math.log2(math.e)

Z_M = 0
Z_C = 4 * M_WIDTH
Z_SQ = Z_C + 2 * C_WIDTH
Z_SKV = Z_SQ + S_WIDTH
Z_G = Z_SKV + 2 * S_KV_WIDTH
LANES = 128
SUBLANES = 8
Z_WIDTH = Z_G + LANES
VMEM_LIMIT = 56 * 1024 * 1024

M_CHUNK = 128
FFN_CHUNK = 256


def _params(*sem):
    return pltpu.CompilerParams(dimension_semantics=sem, vmem_limit_bytes=VMEM_LIMIT)


def _const_spec(shape):
    nd = len(shape)
    return pl.BlockSpec(shape, lambda *_: (0,) * nd, pipeline_mode=pl.Buffered(1))


def _weight_spec(w, l):
    nd = w.ndim - 1
    return pl.BlockSpec((None,) + w.shape[1:], lambda *_: (l,) + (0,) * nd, pipeline_mode=pl.Buffered(1))


def _rms(x, g):
    return x * lax.rsqrt(jnp.mean(x * x, axis=-1, keepdims=True) + EPS) * g


def _sigmoid(x):
    return 1.0 / (1.0 + jnp.exp(-x))


def _log_sigmoid(x):
    return jnp.minimum(x, 0.0) - jnp.log1p(jnp.exp(-jnp.abs(x)))


def _dot(a, b):
    return jnp.dot(a, b, preferred_element_type=F32)


def _dot_nt(a, b):
    return lax.dot_general(a, b, (((1,), (1,)), ((), ())), preferred_element_type=F32)


def _dot_tn(a, b):
    return lax.dot_general(a, b, (((0,), (0,)), ((), ())), preferred_element_type=F32)


IN_PROJ_ROWS = 512
IN_PROJ_COLS = 512
IN_GATES = 4 * M_WIDTH
W_PREP_ROWS = 256


def _in_proj_kernel(x_ref, g_ref, wt_ref, o_ref, wprep_ref, w_scr):
    d = x_ref.shape[1]

    @pl.when(pl.program_id(0) == 0)
    def _():
        n_gate = 2 * M_HEADS
        for src, dst, n in ((0, 0, IN_GATES), (IN_GATES + n_gate, IN_GATES, Z_G - IN_GATES)):
            for c in range(0, n, W_PREP_ROWS):
                w_scr[:, dst + c:dst + c + W_PREP_ROWS] = wt_ref[src + c:src + c + W_PREP_ROWS, :].T.astype(BF16)
        gate_rows = jnp.concatenate([wt_ref[IN_GATES:IN_GATES + n_gate, :], jnp.zeros((LANES - n_gate, d), F32)],
                                    axis=0)
        w_scr[:, Z_G:Z_G + LANES] = gate_rows.T.astype(BF16)
        wprep_ref[...] = w_scr[...]

    tm, n = o_ref.shape
    sub = min(IN_PROJ_ROWS, tm)
    for r0 in range(0, tm, sub):
        h = _rms(x_ref[r0:r0 + sub, :], g_ref[...]).astype(BF16)
        for c in range(0, n, IN_PROJ_COLS):
            w = min(IN_PROJ_COLS, n - c)
            o_ref[r0:r0 + sub, c:c + w] = _dot(h, w_scr[:, c:c + w])


def _in_proj(l, x, g, wt, *, tm):
    m, d = x.shape
    tm = min(tm, m)
    return pl.pallas_call(
        _in_proj_kernel,
        grid=(m // tm,),
        in_specs=[pl.BlockSpec((tm, d), lambda i: (i, 0)), _weight_spec(g, l), _weight_spec(wt, l)],
        out_specs=[pl.BlockSpec((tm, Z_WIDTH), lambda i: (i, 0)), pl.BlockSpec((d, Z_WIDTH), lambda i: (0, 0))],
        out_shape=[jax.ShapeDtypeStruct((m, Z_WIDTH), F32), jax.ShapeDtypeStruct((d, Z_WIDTH), BF16)],
        scratch_shapes=[pltpu.VMEM((d, Z_WIDTH), BF16)],
        compiler_params=_params("arbitrary"),
        name="in_proj",
    )(x, g, wt)


def _mem_kv_kernel(x_ref, w_ref, o_ref, acc):
    acc[...] = _dot(x_ref[...].astype(BF16), w_ref[...])
    o_ref[...] = acc[...].T


def _mem_kv(l, mem3, w):
    b, mem, d = mem3.shape
    n = w.shape[2]
    return pl.pallas_call(
        _mem_kv_kernel,
        grid=(b,),
        in_specs=[pl.BlockSpec((None, mem, d), lambda i: (i, 0, 0)), _weight_spec(w, l)],
        out_specs=pl.BlockSpec((None, n, mem), lambda i: (i, 0, 0)),
        out_shape=jax.ShapeDtypeStruct((b, n, mem), F32),
        scratch_shapes=[pltpu.VMEM((mem, n), F32)],
        compiler_params=_params("arbitrary"),
        name="mem_kv",
    )(mem3, w)


def _sample_in_kernel(x_ref, g_ref, w_ref, z_ref, zt_ref):
    h = _rms(x_ref[...], g_ref[...]).astype(BF16)
    n = z_ref.shape[1]
    for c in range(0, n, IN_PROJ_COLS):
        w = min(IN_PROJ_COLS, n - c)
        zc = _dot(h, w_ref[:, c:c + w])
        z_ref[:, c:c + w] = zc
        zt_ref[c:c + w, :] = zc.T


def _sample_in(l, x, g, w):
    m, d = x.shape
    n = w.shape[1]
    full = lambda shape: pl.BlockSpec(shape, lambda i: (0,) * len(shape))
    return pl.pallas_call(
        _sample_in_kernel,
        grid=(1,),
        in_specs=[full((m, d)), _weight_spec(g, l), full((d, n))],
        out_specs=[full((m, n)), full((n, m))],
        out_shape=[jax.ShapeDtypeStruct((m, n), F32), jax.ShapeDtypeStruct((n, m), F32)],
        compiler_params=_params("arbitrary"),
        name="sample_in",
    )(x, g, w)


def _mlstm_prompt_kernel(z_ref, g_ref, gb_ref, ng_ref, h_ref, cp_ref, np_ref, m_ref,
                         cp_scr, np_scr, m_scr, ht_scr, st_scr, kw_scr):
    NB, L = z_ref.shape[0], z_ref.shape[1]
    D = HEAD_DIM
    hi = lax.Precision.HIGHEST

    @pl.when(pl.program_id(0) == 0)
    def _():
        cp_scr[...] = jnp.zeros(cp_scr.shape, F32)
        np_scr[...] = jnp.zeros(np_scr.shape, F32)
        m_scr[...] = jnp.zeros(m_scr.shape, F32)

    src = lax.broadcasted_iota(jnp.int32, (L, L), 0)
    qry = lax.broadcasted_iota(jnp.int32, (L, L), 1)
    causal_t = src <= qry
    upper = jnp.where(causal_t, 1.0, 0.0)
    lane_half = lax.broadcasted_iota(jnp.int32, (L, 128), 1) // D
    row8 = lax.broadcasted_iota(jnp.int32, (SUBLANES, 128), 0)

    rows, cols = {}, []
    for b in range(NB):
        g_t = (g_ref[b] + gb_ref[...]).T[0:SUBLANES, :]
        b_rows = jnp.dot(_log_sigmoid(g_t), upper, precision=hi, preferred_element_type=F32)
        to_cols = []
        for h in range(M_HEADS):
            b_row = b_rows[M_HEADS + h:M_HEADS + h + 1, :]
            ci_row = g_t[h:h + 1, :] - b_row
            m_prev = m_scr[b, h, 0:1, 0:1]
            b_last = b_row[:, L - 1:L]
            m_new = jnp.maximum(b_last + m_prev, jnp.max(b_last + ci_row, axis=1, keepdims=True))
            rows[b, h] = dict(b_row=b_row, m_prev=m_prev, m_new=m_new, decay=jnp.exp(b_last + m_prev - m_new))
            to_cols += [ci_row, jnp.exp(b_last + ci_row - m_new)]
        to_cols.append(jnp.zeros((128 - 2 * M_HEADS, L), F32))
        cols.append(jnp.concatenate(to_cols, axis=0).T)

    stats = {}
    for b in range(NB):
        for j in range(M_HEADS // 2):
            slab = 128 * j
            q2 = z_ref[b, :, slab:slab + 128].astype(BF16)
            k2 = z_ref[b, :, M_WIDTH + slab:M_WIDTH + slab + 128] * QK_SCALE
            for par in range(2):
                h = 2 * j + par
                r = rows[b, h]
                ci_col = cols[b][:, 2 * h:2 * h + 1]
                a_row = r["b_row"] + r["m_prev"]
                dm = jnp.where(causal_t, r["b_row"] + ci_col, NEG_INF)
                m_row = jnp.maximum(a_row, jnp.max(dm, axis=0, keepdims=True))
                k_own = jnp.where(lane_half == par, k2, 0.0).astype(BF16)
                s_t = _dot_nt(k_own, q2) * jnp.exp(dm - m_row)
                st_scr[b, j, :, L * par:L * (par + 1)] = s_t.astype(BF16)
                stats[b, h] = dict(w_inter=jnp.exp(a_row - m_row), floor=jnp.exp(-m_row),
                                   den_s=jnp.sum(s_t, axis=0, keepdims=True),
                                   decay=r["decay"], m_new=r["m_new"])
            wk = jnp.where(lane_half == 0, cols[b][:, 4 * j + 1:4 * j + 2], cols[b][:, 4 * j + 3:4 * j + 4])
            kw = k2 * wk
            kw_scr[b, j] = kw.astype(BF16)
            stats[b, j, "k_sum"] = jnp.sum(kw, axis=0, keepdims=True)

    block_diag = (lax.broadcasted_iota(jnp.int32, (128, 128), 0) // D
                  == lax.broadcasted_iota(jnp.int32, (128, 128), 1) // D)
    for b in range(NB):
        for j in range(M_HEADS // 2):
            slab = 128 * j
            q2 = z_ref[b, :, slab:slab + 128].astype(BF16)
            v2 = z_ref[b, :, 2 * M_WIDTH + slab:2 * M_WIDTH + slab + 128].astype(BF16)
            cp = cp_scr[b, j]
            npair = np_scr[b, j]
            qc = _dot_nt(cp.astype(BF16), q2)
            qn = _dot_nt(npair.astype(BF16), q2)
            pv = _dot_tn(v2, st_scr[b, j])
            upd = jnp.where(block_diag, _dot_tn(v2, kw_scr[b, j]), 0.0)
            n_new = jnp.zeros((SUBLANES, 128), F32)
            for par in range(2):
                h = 2 * j + par
                st = stats[b, h]
                hr = slice(D * par, D * (par + 1))
                num = st["w_inter"] * qc[hr, :] + pv[hr, L * par:L * (par + 1)]
                den = st["w_inter"] * qn[par:par + 1, :] + st["den_s"]
                hh = num * (1.0 / jnp.maximum(jnp.abs(den), st["floor"]))
                hh = hh * lax.rsqrt(jnp.mean(hh * hh, axis=0, keepdims=True) + EPS)
                ht_scr[b, D * h:D * (h + 1), :] = hh
                cp_scr[b, j, hr, :] = st["decay"] * cp[hr, :] + upd[hr, :]
                k_sum = jnp.where(lane_half[0:1, :] == par, stats[b, j, "k_sum"], 0.0)
                n_new = jnp.where(row8 == par, st["decay"] * npair[par:par + 1, :] + k_sum, n_new)
                m_scr[b, h] = jnp.broadcast_to(st["m_new"], m_scr.shape[2:])
            np_scr[b, j] = n_new
        o_gate = _sigmoid(z_ref[b, :, 3 * M_WIDTH:4 * M_WIDTH])
        h_ref[b] = ht_scr[b].T * ng_ref[...] * o_gate

    cp_ref[...] = cp_scr[...]
    np_ref[...] = np_scr[...]
    m_ref[...] = m_scr[...]


def _mlstm_prompt(l, z3, gbias, ng):
    b, s, _ = z3.shape
    L = M_CHUNK
    P = M_HEADS // 2
    return pl.pallas_call(
        _mlstm_prompt_kernel,
        grid=(s // L,),
        in_specs=[
            pl.BlockSpec((b, L, 4 * M_WIDTH), lambda c: (0, c, Z_M // (4 * M_WIDTH))),
            pl.BlockSpec((b, L, 128), lambda c: (0, c, Z_G // 128)),
            _weight_spec(gbias, l),
            _weight_spec(ng, l),
        ],
        out_specs=[
            pl.BlockSpec((b, L, M_WIDTH), lambda c: (0, c, 0)),
            pl.BlockSpec((b, P, 128, 128), lambda c: (0, 0, 0, 0)),
            pl.BlockSpec((b, P, SUBLANES, 128), lambda c: (0, 0, 0, 0)),
            pl.BlockSpec((b, M_HEADS, SUBLANES, 128), lambda c: (0, 0, 0, 0)),
        ],
        out_shape=[
            jax.ShapeDtypeStruct((b, s, M_WIDTH), F32),
            jax.ShapeDtypeStruct((b, P, 128, 128), F32),
            jax.ShapeDtypeStruct((b, P, SUBLANES, 128), F32),
            jax.ShapeDtypeStruct((b, M_HEADS, SUBLANES, 128), F32),
        ],
        scratch_shapes=[
            pltpu.VMEM((b, P, 128, 128), F32),
            pltpu.VMEM((b, P, SUBLANES, 128), F32),
            pltpu.VMEM((b, M_HEADS, SUBLANES, 128), F32),
            pltpu.VMEM((b, M_WIDTH, L), F32),
            pltpu.VMEM((b, P, L, 2 * L), BF16),
            pltpu.VMEM((b, P, L, 128), BF16),
        ],
        compiler_params=_params("arbitrary"),
        name="mlstm_prompt",
    )(z3, z3, gbias, ng)


CONV_PAD = 32


def _conv_prompt_kernel(z_ref, w_ref, cb_ref, lg_ref, lb_ref, h_ref, tail_ref, buf, shifted):
    tc = z_ref.shape[0]
    t = pl.program_id(1)

    @pl.when(t == 0)
    def _():
        buf[0:CONV_PAD, :] = jnp.zeros((CONV_PAD, C_WIDTH), F32)

    @pl.when(t > 0)
    def _():
        buf[0:CONV_PAD, :] = buf[tc:tc + CONV_PAD, :]

    u = z_ref[:, 0:C_WIDTH] * _sigmoid(z_ref[:, C_WIDTH:2 * C_WIDTH])
    buf[CONV_PAD:CONV_PAD + tc, :] = u
    n_sh = shifted.shape[1]
    for r in range(1, SUBLANES):
        shifted[r - 1] = buf[r:r + n_sh, :]
    off = CONV_PAD - (C_KERNEL - 1)
    acc = jnp.broadcast_to(cb_ref[...], (tc, C_WIDTH))
    for j in range(C_KERNEL):
        r = (off + j) % SUBLANES
        base = off + j - r
        win = buf[base:base + tc, :] if r == 0 else shifted[r - 1, base:base + tc, :]
        acc = acc + w_ref[j:j + 1, :] * win
    mu = jnp.mean(acc, axis=-1, keepdims=True)
    xc = acc - mu
    y = xc * lax.rsqrt(jnp.mean(xc * xc, axis=-1, keepdims=True) + EPS) * lg_ref[...] + lb_ref[...]
    h_ref[...] = y * _sigmoid(y)
    tail_ref[...] = buf[tc:tc + CONV_PAD, :]


def _conv_prompt(l, z3, w, cb, lg, lb, *, tc):
    b, s, _ = z3.shape
    return pl.pallas_call(
        _conv_prompt_kernel,
        grid=(b, s // tc),
        in_specs=[
            pl.BlockSpec((None, tc, 2 * C_WIDTH), lambda i, t: (i, t, Z_C // (2 * C_WIDTH))),
            _weight_spec(w, l), _weight_spec(cb, l), _weight_spec(lg, l), _weight_spec(lb, l),
        ],
        out_specs=[
            pl.BlockSpec((None, tc, C_WIDTH), lambda i, t: (i, t, 0)),
            pl.BlockSpec((None, CONV_PAD, C_WIDTH), lambda i, t: (i, 0, 0)),
        ],
        out_shape=[
            jax.ShapeDtypeStruct((b, s, C_WIDTH), F32),
            jax.ShapeDtypeStruct((b, CONV_PAD, C_WIDTH), F32),
        ],
        scratch_shapes=[pltpu.VMEM((CONV_PAD + tc, C_WIDTH), F32),
                        pltpu.VMEM((SUBLANES - 1, CONV_PAD + tc - SUBLANES, C_WIDTH), F32)],
        compiler_params=_params("arbitrary", "arbitrary"),
        name="conv_prompt",
    )(z3, w, cb, lg, lb)


def _swa_prompt_kernel(rb_ref, sink_ref, bucket_ref, q_ref, kv_ref, kvp_ref, o_ref, kt_ref, vt_ref,
                       bias_scr, s_scr, p_scr, ot_scr, *, layer):
    W = WINDOW
    H = S_Q_HEADS
    G = H // S_KV_HEADS
    i = pl.program_id(0)
    n = pl.program_id(1)

    @pl.when(jnp.logical_and(i == 0, n == 0))
    def _():
        bucket = bucket_ref[...]
        prev_key = lax.broadcasted_iota(jnp.int32, (2 * W, W), 0) < W
        for h in range(H):
            acc = jnp.full((2 * W, W), NEG_INF, F32)
            for b in range(N_BUCKETS):
                acc = jnp.where(bucket == b, rb_ref[b, h], acc)
            acc = acc * LOG2E
            bias_scr[0, h] = acc
            bias_scr[1, h] = jnp.where(prev_key, NEG_INF, acc)

    QB = q_ref.shape[0] // W
    NK = (QB + 1) * W
    first = jnp.where(n == 0, 1, 0)
    kk = jnp.concatenate([kvp_ref[:, 0:S_KV_WIDTH], kv_ref[:, 0:S_KV_WIDTH]], axis=0) * (QK_SCALE * LOG2E)
    vv = jnp.concatenate([kvp_ref[:, S_KV_WIDTH:2 * S_KV_WIDTH], kv_ref[:, S_KV_WIDTH:2 * S_KV_WIDTH]], axis=0)
    kk_r = pltpu.roll(kk, HEAD_DIM, axis=1)
    lo_lane = lax.broadcasted_iota(jnp.int32, (NK, S_KV_WIDTH), 1) < HEAD_DIM
    k_var = [[jnp.where(lo_lane, kk, 0.0).astype(BF16), jnp.where(lo_lane, 0.0, kk_r).astype(BF16)],
             [jnp.where(lo_lane, kk_r, 0.0).astype(BF16), jnp.where(lo_lane, 0.0, kk).astype(BF16)]]
    v_t = vv.T.astype(BF16)

    for j in range(QB):
        k0 = j * W
        masked = first if j == 0 else 0
        m_rows = [None] * H
        for hk in range(S_KV_HEADS):
            c0 = 2 * 128 * hk
            q_st = jnp.concatenate([q_ref[k0:k0 + W, c0:c0 + 128], q_ref[k0:k0 + W, c0 + 128:c0 + 256]],
                                   axis=0).astype(BF16)
            for half in range(2):
                s_t = _dot_nt(k_var[hk][half][k0:k0 + 2 * W, :], q_st)
                for slab in range(2):
                    head = G * hk + 2 * slab + half
                    sb = s_t[:, 128 * slab:128 * (slab + 1)] + bias_scr[masked, head]
                    s_scr[j, head] = sb
                    m_rows[head] = jnp.maximum(jnp.max(sb, axis=0, keepdims=True), sink_ref[layer, head] * LOG2E)

        inv = [None] * H
        for head in range(H):
            e = jnp.exp2(s_scr[j, head] - m_rows[head])
            den = jnp.sum(e, axis=0, keepdims=True) + jnp.exp2(sink_ref[layer, head] * LOG2E - m_rows[head])
            inv[head] = 1.0 / den
            p_scr[j, :, 128 * head:128 * (head + 1)] = e.astype(BF16)

        for hk in range(S_KV_HEADS):
            o_t = _dot(v_t[HEAD_DIM * hk:HEAD_DIM * (hk + 1), k0:k0 + 2 * W],
                       p_scr[j, :, 128 * G * hk:128 * G * (hk + 1)])
            for g in range(G):
                head = G * hk + g
                ot_scr[j, HEAD_DIM * head:HEAD_DIM * (head + 1), :] = o_t[:, 128 * g:128 * (g + 1)] * inv[head]
        o_ref[k0:k0 + W, :] = ot_scr[j].T

    @pl.when(n == pl.num_programs(1) - 1)
    def _():
        kt_ref[...] = kv_ref[(QB - 1) * W:QB * W, 0:S_KV_WIDTH].T
        vt_ref[...] = kv_ref[(QB - 1) * W:QB * W, S_KV_WIDTH:2 * S_KV_WIDTH].T


def _swa_prompt(l, z3, rel_bias, sinks, bucket, *, qb):
    b, s, _ = z3.shape
    W = WINDOW
    smem = pl.BlockSpec(memory_space=pltpu.SMEM)
    return pl.pallas_call(
        functools.partial(_swa_prompt_kernel, layer=l),
        grid=(b, s // (qb * W)),
        in_specs=[
            smem, smem, _const_spec((2 * W, W)),
            pl.BlockSpec((None, qb * W, S_WIDTH), lambda i, n: (i, n, Z_SQ // S_WIDTH)),
            pl.BlockSpec((None, qb * W, 2 * S_KV_WIDTH), lambda i, n: (i, n, Z_SKV // (2 * S_KV_WIDTH))),
            pl.BlockSpec((None, W, 2 * S_KV_WIDTH),
                         lambda i, n: (i, jnp.maximum(n * qb - 1, 0), Z_SKV // (2 * S_KV_WIDTH))),
        ],
        out_specs=[
            pl.BlockSpec((None, qb * W, S_WIDTH), lambda i, n: (i, n, 0)),
            pl.BlockSpec((None, S_KV_WIDTH, W), lambda i, n: (i, 0, 0)),
            pl.BlockSpec((None, S_KV_WIDTH, W), lambda i, n: (i, 0, 0)),
        ],
        out_shape=[
            jax.ShapeDtypeStruct((b, s, S_WIDTH), F32),
            jax.ShapeDtypeStruct((b, S_KV_WIDTH, W), F32),
            jax.ShapeDtypeStruct((b, S_KV_WIDTH, W), F32),
        ],
        scratch_shapes=[
            pltpu.VMEM((2, S_Q_HEADS, 2 * W, W), F32),
            pltpu.VMEM((qb, S_Q_HEADS, 2 * W, W), F32),
            pltpu.VMEM((qb, 2 * W, S_Q_HEADS * W), BF16),
            pltpu.VMEM((qb, S_WIDTH, W), F32),
        ],
        compiler_params=_params("arbitrary", "arbitrary"),
        name="swa_prompt",
    )(rel_bias, sinks, bucket, z3, z3, z3)


MIX_SUB = 512


def _mix_out_prompt_kernel(x_ref, hm_ref, hc_ref, hs_ref, wout_ref, g2_ref, wq_ref, mk_ref, mv_ref, wo_ref, o_ref,
                           x1_scr, qx_scr, p_scr):
    tm = x_ref.shape[0]
    mem = mk_ref.shape[1]
    mk_t = mk_ref[...]
    mv_t = mv_ref[...]
    row_head = lax.broadcasted_iota(jnp.int32, mk_t.shape, 0) // HEAD_DIM
    k_heads = [jnp.where(row_head == h, mk_t, 0.0).astype(BF16) for h in range(X_HEADS)]
    v_cat = jnp.concatenate([jnp.where(row_head == h, mv_t, 0.0).astype(BF16) for h in range(X_HEADS)], axis=1)
    subs = [slice(r0, r0 + MIX_SUB) for r0 in range(0, tm, MIX_SUB)]
    for rows in subs:
        cat = jnp.concatenate([hm_ref[rows, :].astype(BF16), hc_ref[rows, :].astype(BF16),
                               hs_ref[rows, :].astype(BF16)], axis=1)
        x1 = x_ref[rows, :] + _dot(cat, wout_ref[...])
        x1_scr[rows, :] = x1
        qx = _dot(_rms(x1, g2_ref[...]).astype(BF16), wq_ref[...])
        qx_scr[rows, :] = (qx * (QK_SCALE * LOG2E)).astype(BF16)
    for rows in subs:
        qx = qx_scr[rows, :]
        for h in range(X_HEADS):
            s = _dot(qx, k_heads[h])
            e = jnp.exp2(s - jnp.max(s, axis=1, keepdims=True))
            p_scr[rows, mem * h:mem * (h + 1)] = (e * (1.0 / jnp.sum(e, axis=1, keepdims=True))).astype(BF16)
    for rows in subs:
        o = _dot_nt(p_scr[rows, :], v_cat)
        o_ref[rows, :] = x1_scr[rows, :] + _dot(o.astype(BF16), wo_ref[...])


def _mix_out_prompt(l, x3, hm, hc, hs, wout, g2, wq, mkv, wo, *, tm):
    b, s, d = x3.shape
    tm = min(tm, s)
    mem = mkv.shape[2]
    row = lambda w: pl.BlockSpec((None, tm, w), lambda i, t: (i, t, 0))
    return pl.pallas_call(
        _mix_out_prompt_kernel,
        grid=(b, s // tm),
        in_specs=[
            row(d), row(M_WIDTH), row(C_WIDTH), row(S_WIDTH),
            _weight_spec(wout, l), _weight_spec(g2, l), _weight_spec(wq, l),
            pl.BlockSpec((None, X_WIDTH, mem), lambda i, t: (i, 0, 0)),
            pl.BlockSpec((None, X_WIDTH, mem), lambda i, t: (i, 1, 0)),
            _weight_spec(wo, l),
        ],
        out_specs=row(d),
        out_shape=jax.ShapeDtypeStruct((b, s, d), F32),
        scratch_shapes=[pltpu.VMEM((tm, d), F32), pltpu.VMEM((tm, X_WIDTH), BF16),
                        pltpu.VMEM((tm, X_HEADS * mem), BF16)],
        compiler_params=_params("arbitrary", "arbitrary"),
        name="mix_out_prompt",
    )(x3, hm, hc, hs, wout, g2, wq, mkv, mkv, wo)


FFN_PAD = 8


def _ffn_prompt_kernel(x_ref, g3_ref, wup_ref, cw_ref, cb_ref, wdn_ref, gf_ref, o_ref, tail_ref,
                       gbuf, carry, act, *, final):
    tm = x_ref.shape[0]
    dff = wdn_ref.shape[0]

    @pl.when(pl.program_id(1) == 0)
    def _():
        carry[...] = jnp.zeros(carry.shape, F32)

    x = x_ref[...]
    h = _rms(x, g3_ref[...]).astype(BF16)
    w = FFN_CHUNK
    for c in range(0, dff, w):
        a = _dot(h, wup_ref[:, c:c + w])
        g = _dot(h, wup_ref[:, dff + c:dff + c + w])
        gbuf[0:FFN_PAD, 0:w] = carry[:, c:c + w]
        gbuf[FFN_PAD:FFN_PAD + tm, 0:w] = g
        carry[:, c:c + w] = g[tm - FFN_PAD:tm, :]
        gc = (cw_ref[0:1, c:c + w] * gbuf[FFN_PAD - 2:FFN_PAD - 2 + tm, 0:w]
              + cw_ref[1:2, c:c + w] * gbuf[FFN_PAD - 1:FFN_PAD - 1 + tm, 0:w]
              + cw_ref[2:3, c:c + w] * g + cb_ref[:, c:c + w])
        act[:, c:c + w] = (gc * _sigmoid(gc) * a).astype(BF16)
    y = x + _dot(act[...], wdn_ref[...])
    o_ref[...] = _rms(y, gf_ref[...]) if final else y
    tail_ref[...] = carry[...]


def _ffn_prompt(l, x3, g3, wup, cw, cb, wdn, gf, *, tm, final):
    b, s, d = x3.shape
    tm = min(tm, s)
    dff = wdn.shape[1]
    return pl.pallas_call(
        functools.partial(_ffn_prompt_kernel, final=final),
        grid=(b, s // tm),
        in_specs=[
            pl.BlockSpec((None, tm, d), lambda i, t: (i, t, 0)),
            _weight_spec(g3, l), _weight_spec(wup, l), _weight_spec(cw, l), _weight_spec(cb, l),
            _weight_spec(wdn, l), _const_spec((1, d)),
        ],
        out_specs=[
            pl.BlockSpec((None, tm, d), lambda i, t: (i, t, 0)),
            pl.BlockSpec((None, FFN_PAD, dff), lambda i, t: (i, 0, 0)),
        ],
        out_shape=[jax.ShapeDtypeStruct((b, s, d), F32), jax.ShapeDtypeStruct((b, FFN_PAD, dff), F32)],
        scratch_shapes=[
            pltpu.VMEM((FFN_PAD + tm, FFN_CHUNK), F32),
            pltpu.VMEM((FFN_PAD, dff), F32),
            pltpu.VMEM((tm, dff), BF16),
        ],
        compiler_params=_params("arbitrary", "arbitrary"),
        name="ffn_prompt",
    )(x3, g3, wup, cw, cb, wdn, gf)


SWA_SAMPLE_BLOCK = 32
XATTN_SAMPLE_BLOCK = 16
CONV_SAMPLE_BLOCK = 32


def _mlstm_sample_kernel(gb_ref, q_ref, k_ref, v_ref, o_ref, g_ref, ng_ref, c_ref, n_ref, m_ref, *rest, layer):
    h_ref, co_ref, no_ref, mo_ref, kw_scr = rest[-5:]
    _copy_earlier_layers(rest[:-5], (co_ref, no_ref, mo_ref))
    h = pl.program_id(0)
    i_pre = g_ref[pl.ds(h, 1), :] + gb_ref[layer, h]
    f_pre = g_ref[pl.ds(M_HEADS + h, 1), :] + gb_ref[layer, M_HEADS + h]
    a = _log_sigmoid(f_pre) + m_ref[pl.ds(h, 1), :]
    m_t = jnp.maximum(a, i_pre)
    w_old = jnp.exp(a - m_t)
    w_new = jnp.exp(i_pre - m_t)
    q = q_ref[...]
    k = k_ref[...] * QK_SCALE
    v = v_ref[...]
    n_old = n_ref[...]
    kw_scr[...] = k * w_new

    def body(d, acc):
        c_old = c_ref[d]
        co_ref[layer, d] = w_old * c_old + kw_scr[pl.ds(d, 1), :] * v
        return acc + q_ref[pl.ds(d, 1), :] * c_old

    qc = lax.fori_loop(0, HEAD_DIM, body, jnp.zeros(v.shape, F32), unroll=8)
    s = jnp.sum(q * k, axis=0, keepdims=True) * w_new
    num = w_old * qc + s * v
    den = w_old * jnp.sum(q * n_old, axis=0, keepdims=True) + s
    hh = num / jnp.maximum(jnp.abs(den), jnp.exp(-m_t))
    hh = hh * lax.rsqrt(jnp.mean(hh * hh, axis=0, keepdims=True) + EPS) * ng_ref[...]
    h_ref[...] = hh * _sigmoid(o_ref[...])
    no_ref[layer] = w_old * n_old + kw_scr[...]
    mo_ref[layer, pl.ds(h, 1), :] = m_t


def _mlstm_sample(l, zt, gb, ngt, c_all, n_all, m_all, prev):
    bsz = zt.shape[1]
    D = HEAD_DIM
    feat = lambda off: pl.BlockSpec((D, bsz), lambda h: (off // D + h, 0))
    c_spec = lambda n: pl.BlockSpec((n, None, D, D, bsz), lambda h: (0, h, 0, 0, 0))
    n_spec = lambda n: pl.BlockSpec((n, None, D, bsz), lambda h: (0, h, 0, 0))
    m_spec = lambda n: pl.BlockSpec((n, M_HEADS, bsz), lambda h: (0, 0, 0))
    prev_specs = [c_spec(l), n_spec(l), m_spec(l)] if prev else []
    return pl.pallas_call(
        functools.partial(_mlstm_sample_kernel, layer=l),
        grid=(M_HEADS,),
        in_specs=[
            pl.BlockSpec(memory_space=pltpu.SMEM),
            feat(Z_M), feat(Z_M + M_WIDTH), feat(Z_M + 2 * M_WIDTH), feat(Z_M + 3 * M_WIDTH),
            pl.BlockSpec((SUBLANES, bsz), lambda h: (Z_G // SUBLANES, 0)),
            pl.BlockSpec((None, D, bsz), lambda h: (l, h, 0)),
            pl.BlockSpec((None, None, D, D, bsz), lambda h: (l, h, 0, 0, 0)),
            pl.BlockSpec((None, None, D, bsz), lambda h: (l, h, 0, 0)),
            pl.BlockSpec((None, M_HEADS, bsz), lambda h: (l, 0, 0)),
        ] + prev_specs,
        out_specs=[pl.BlockSpec((D, bsz), lambda h: (h, 0)), c_spec(l + 1), n_spec(l + 1), m_spec(l + 1)],
        out_shape=[
            jax.ShapeDtypeStruct((M_WIDTH, bsz), F32),
            jax.ShapeDtypeStruct((l + 1, M_HEADS, D, D, bsz), F32),
            jax.ShapeDtypeStruct((l + 1, M_HEADS, D, bsz), F32),
            jax.ShapeDtypeStruct((l + 1, M_HEADS, bsz), F32),
        ],
        scratch_shapes=[pltpu.VMEM((D, bsz), F32)],
        compiler_params=_params("arbitrary"),
        name="mlstm_sample",
    )(gb, zt, zt, zt, zt, zt, ngt, c_all, n_all, m_all, *prev)


def _conv_sample_kernel(z_ref, hist_ref, w_ref, cb_ref, lg_ref, lb_ref, *rest):
    h_ref, hist_out_ref = rest[-2:]
    _copy_earlier_layers(rest[:-2], (hist_out_ref,))
    l = hist_out_ref.shape[0] - 1
    nh = C_KERNEL - 1
    u = z_ref[:, 0:C_WIDTH] * _sigmoid(z_ref[:, C_WIDTH:2 * C_WIDTH])
    acc = cb_ref[...] + w_ref[nh:nh + 1, :] * u
    for j in range(nh):
        acc = acc + w_ref[j:j + 1, :] * hist_ref[j]
    mu = jnp.mean(acc, axis=-1, keepdims=True)
    xc = acc - mu
    y = xc * lax.rsqrt(jnp.mean(xc * xc, axis=-1, keepdims=True) + EPS) * lg_ref[...] + lb_ref[...]
    h_ref[...] = y * _sigmoid(y)
    for j in range(nh - 1):
        hist_out_ref[l, j] = hist_ref[j + 1]
    hist_out_ref[l, nh - 1] = u


def _conv_sample(l, z, hist_all, w, cb, lg, lb, prev):
    bsz = z.shape[0]
    nh = hist_all.shape[1]
    R = min(CONV_SAMPLE_BLOCK, bsz)
    hist_spec = lambda n: pl.BlockSpec((n, nh, R, C_WIDTH), lambda i: (0, 0, i, 0))
    return pl.pallas_call(
        _conv_sample_kernel,
        grid=(bsz // R,),
        in_specs=[
            pl.BlockSpec((R, 2 * C_WIDTH), lambda i: (i, Z_C // (2 * C_WIDTH))),
            pl.BlockSpec((None, nh, R, C_WIDTH), lambda i: (l, 0, i, 0)),
            _weight_spec(w, l), _weight_spec(cb, l), _weight_spec(lg, l), _weight_spec(lb, l),
        ] + [hist_spec(l)] * len(prev),
        out_specs=[pl.BlockSpec((R, C_WIDTH), lambda i: (i, 0)), hist_spec(l + 1)],
        out_shape=[jax.ShapeDtypeStruct((bsz, C_WIDTH), F32), jax.ShapeDtypeStruct((l + 1, nh, bsz, C_WIDTH), F32)],
        compiler_params=_params("arbitrary"),
        name="conv_sample",
    )(z, hist_all, w, cb, lg, lb, *prev)


def _copy_earlier_layers(prev_refs, out_refs):
    for p_ref, o_ref in zip(prev_refs, out_refs):
        o_ref[0:p_ref.shape[0]] = p_ref[...]


def _swa_sample_kernel(q_ref, kvn_ref, kv2_ref, kc_ref, vc_ref, bias_ref, aux_ref, *rest):
    o_ref, ko_ref, vo_ref = rest[-3:]
    _copy_earlier_layers(rest[:-3], (ko_ref, vo_ref))
    l = ko_ref.shape[0] - 1
    R = q_ref.shape[0]
    W = kc_ref.shape[2]
    H = S_Q_HEADS
    shape = (R, H, 128)
    row = lax.broadcasted_iota(jnp.int32, shape, 1)
    lane_half = lax.broadcasted_iota(jnp.int32, shape, 2) // HEAD_DIM
    q_half = row % 2
    kv_head = row // (H // S_KV_HEADS)
    qs = jnp.zeros(shape, F32)
    for j in range(H // 2):
        qs = jnp.where(row // 2 == j, q_ref[:, :, 128 * j:128 * (j + 1)], qs)
    q8 = jnp.where(lane_half == kv_head, jnp.where(q_half == kv_head, qs, pltpu.roll(qs, HEAD_DIM, axis=2)), 0.0)
    k_new = kvn_ref[:, :, 0:S_KV_WIDTH]
    v_new = kvn_ref[:, :, S_KV_WIDTH:2 * S_KV_WIDTH]
    s = jnp.einsum("bqd,bdk->bqk", q8.astype(BF16), kc_ref[...].astype(BF16), preferred_element_type=F32) * QK_SCALE
    s = s + bias_ref[...][None]
    s_new = jnp.sum(q8 * k_new, axis=2, keepdims=True) * QK_SCALE + aux_ref[:, 0:1][None]
    sink = aux_ref[:, 1:2][None]
    m = jnp.maximum(jnp.maximum(jnp.max(s, axis=2, keepdims=True), s_new), sink)
    e = jnp.exp(s - m)
    e_new = jnp.exp(s_new - m)
    inv = 1.0 / (jnp.sum(e, axis=2, keepdims=True) + e_new + jnp.exp(sink - m))
    o8 = jnp.einsum("bqk,bdk->bqd", e.astype(BF16), vc_ref[...].astype(BF16), preferred_element_type=F32)
    o8 = (o8 + e_new * v_new) * inv
    o8 = jnp.where(lane_half == q_half, jnp.where(q_half == kv_head, o8, pltpu.roll(o8, HEAD_DIM, axis=2)), 0.0)
    for j in range(H // 2):
        o_ref[:, :, 128 * j:128 * (j + 1)] = jnp.sum(jnp.where(row // 2 == j, o8, 0.0), axis=1, keepdims=True)
    k_cols = kv2_ref[:, 0:S_KV_WIDTH].T
    v_cols = kv2_ref[:, S_KV_WIDTH:2 * S_KV_WIDTH].T
    last = lax.broadcasted_iota(jnp.int32, (S_KV_WIDTH, W), 1) == W - 1
    for r in range(R):
        ko_ref[l, r] = jnp.where(last, k_cols[:, r:r + 1], pltpu.roll(kc_ref[r], W - 1, axis=1))
        vo_ref[l, r] = jnp.where(last, v_cols[:, r:r + 1], pltpu.roll(vc_ref[r], W - 1, axis=1))


def _swa_sample(l, z, z3, kc_all, vc_all, bias, aux, prev):
    bsz = z.shape[0]
    R = min(SWA_SAMPLE_BLOCK, bsz)
    W = kc_all.shape[3]
    cache_in = pl.BlockSpec((None, R, S_KV_WIDTH, W), lambda i: (l, i, 0, 0))
    cache_prev = pl.BlockSpec((l, R, S_KV_WIDTH, W), lambda i: (0, i, 0, 0))
    cache_out = pl.BlockSpec((l + 1, R, S_KV_WIDTH, W), lambda i: (0, i, 0, 0))
    return pl.pallas_call(
        _swa_sample_kernel,
        grid=(bsz // R,),
        in_specs=[
            pl.BlockSpec((R, 1, S_WIDTH), lambda i: (i, 0, Z_SQ // S_WIDTH)),
            pl.BlockSpec((R, 1, 2 * S_KV_WIDTH), lambda i: (i, 0, Z_SKV // (2 * S_KV_WIDTH))),
            pl.BlockSpec((R, 2 * S_KV_WIDTH), lambda i: (i, Z_SKV // (2 * S_KV_WIDTH))),
            cache_in, cache_in,
            _const_spec((S_Q_HEADS, 128)), _weight_spec(aux, l),
        ] + [cache_prev] * len(prev),
        out_specs=[pl.BlockSpec((R, 1, S_WIDTH), lambda i: (i, 0, 0)), cache_out, cache_out],
        out_shape=[
            jax.ShapeDtypeStruct((bsz, 1, S_WIDTH), F32),
            jax.ShapeDtypeStruct((l + 1, bsz, S_KV_WIDTH, W), F32),
            jax.ShapeDtypeStruct((l + 1, bsz, S_KV_WIDTH, W), F32),
        ],
        compiler_params=_params("arbitrary"),
        name="swa_sample",
    )(z3, z3, z, kc_all, vc_all, bias, aux, *prev)


def _mix_out_sample_kernel(x_ref, hmt_ref, hc_ref, hs_ref, wout_ref, g2_ref, wq_ref, x1_ref, q_ref):
    cat = jnp.concatenate([hmt_ref[...].T.astype(BF16), hc_ref[...].astype(BF16), hs_ref[...].astype(BF16)], axis=1)
    x1 = x_ref[...] + _dot(cat, wout_ref[...])
    x1_ref[...] = x1
    q_ref[...] = _dot(_rms(x1, g2_ref[...]).astype(BF16), wq_ref[...])


def _mix_out_sample(l, x, hm, hc, hs, wout, g2, wq):
    bsz, d = x.shape
    full = lambda a: pl.BlockSpec(a.shape, lambda i: (0,) * a.ndim)
    args = (x, hm, hc, hs, wout, g2, wq)
    return pl.pallas_call(
        _mix_out_sample_kernel,
        grid=(1,),
        in_specs=[full(x), full(hm), full(hc), full(hs), _weight_spec(wout, l), _weight_spec(g2, l),
                  _weight_spec(wq, l)],
        out_specs=[pl.BlockSpec((bsz, d), lambda i: (0, 0)), pl.BlockSpec((bsz, X_WIDTH), lambda i: (0, 0))],
        out_shape=[jax.ShapeDtypeStruct((bsz, d), F32), jax.ShapeDtypeStruct((bsz, X_WIDTH), F32)],
        compiler_params=_params("arbitrary"),
        name="mix_out_sample",
    )(*args)


def _xattn_sample_kernel(q_ref, k_ref, v_ref, o_ref):
    R = k_ref.shape[0]
    shape = (R, SUBLANES, X_WIDTH)
    row = lax.broadcasted_iota(jnp.int32, shape, 1)
    lane_head = lax.broadcasted_iota(jnp.int32, shape, 2) // HEAD_DIM
    own = row == lane_head
    q8 = jnp.where(own, jnp.broadcast_to(q_ref[...], shape), 0.0).astype(BF16)
    s = jnp.einsum("bqd,bdk->bqk", q8, k_ref[...].astype(BF16), preferred_element_type=F32) * QK_SCALE
    e = jnp.exp(s - jnp.max(s, axis=2, keepdims=True))
    p = (e / jnp.sum(e, axis=2, keepdims=True)).astype(BF16)
    o8 = jnp.einsum("bqk,bdk->bqd", p, v_ref[...].astype(BF16), preferred_element_type=F32)
    o_ref[...] = jnp.sum(jnp.where(own, o8, 0.0), axis=1, keepdims=True)


def _xattn_sample(l, q3, k_all, v_all):
    _, bsz, w, mem = k_all.shape
    R = min(XATTN_SAMPLE_BLOCK, bsz)
    kv = pl.BlockSpec((None, R, w, mem), lambda i: (l, i, 0, 0))
    qo = pl.BlockSpec((R, 1, w), lambda i: (i, 0, 0))
    return pl.pallas_call(
        _xattn_sample_kernel,
        grid=(bsz // R,),
        in_specs=[qo, kv, kv],
        out_specs=qo,
        out_shape=jax.ShapeDtypeStruct((bsz, 1, w), F32),
        compiler_params=_params("arbitrary"),
        name="xattn_sample",
    )(q3, k_all, v_all)


def _ffn_sample_kernel(x1_ref, ox_ref, wo_ref, g3_ref, wup_ref, cw_ref, cb_ref, wdn_ref, gf_ref, hist_ref,
                       *rest, final):
    o_ref, hist_out_ref, act = rest[-3:]
    _copy_earlier_layers(rest[:-3], (hist_out_ref,))
    l = hist_out_ref.shape[0] - 1
    dff = wdn_ref.shape[0]
    x = x1_ref[...] + _dot(ox_ref[...].astype(BF16), wo_ref[...])
    h = _rms(x, g3_ref[...]).astype(BF16)
    for c in range(0, dff, FFN_CHUNK):
        a = _dot(h, wup_ref[:, c:c + FFN_CHUNK])
        g = _dot(h, wup_ref[:, dff + c:dff + c + FFN_CHUNK])
        h1 = hist_ref[:, 1, c:c + FFN_CHUNK]
        gc = (cw_ref[0:1, c:c + FFN_CHUNK] * hist_ref[:, 0, c:c + FFN_CHUNK]
              + cw_ref[1:2, c:c + FFN_CHUNK] * h1
              + cw_ref[2:3, c:c + FFN_CHUNK] * g + cb_ref[:, c:c + FFN_CHUNK])
        act[:, c:c + FFN_CHUNK] = (gc * _sigmoid(gc) * a).astype(BF16)
        hist_out_ref[l, :, 0, c:c + FFN_CHUNK] = h1
        hist_out_ref[l, :, 1, c:c + FFN_CHUNK] = g
    y = x + _dot(act[...], wdn_ref[...])
    o_ref[...] = _rms(y, gf_ref[...]) if final else y


def _ffn_sample(l, x1, ox, wo, g3, wup, cw, cb, wdn, gf, hist_all, prev, *, final):
    bsz, d = x1.shape
    dff = wdn.shape[1]
    full = lambda a: pl.BlockSpec(a.shape, lambda i: (0,) * a.ndim)
    hshape = hist_all.shape[1:]
    return pl.pallas_call(
        functools.partial(_ffn_sample_kernel, final=final),
        grid=(1,),
        in_specs=[full(x1), full(ox), _weight_spec(wo, l), _weight_spec(g3, l), _weight_spec(wup, l),
                  _weight_spec(cw, l), _weight_spec(cb, l),
                  _weight_spec(wdn, l), full(gf), pl.BlockSpec((None,) + hshape, lambda i: (l, 0, 0, 0))]
                 + [full(p) for p in prev],
        out_specs=[pl.BlockSpec((bsz, d), lambda i: (0, 0)), pl.BlockSpec((l + 1,) + hshape, lambda i: (0, 0, 0, 0))],
        out_shape=[jax.ShapeDtypeStruct((bsz, d), F32), jax.ShapeDtypeStruct((l + 1,) + hshape, F32)],
        scratch_shapes=[pltpu.VMEM((bsz, dff), BF16)],
        compiler_params=_params("arbitrary"),
        name="ffn_sample",
    )(x1, ox, wo, g3, wup, cw, cb, wdn, gf, hist_all, *prev)


def _t5_buckets(dist):
    n = np.maximum(dist, 0)
    max_exact = N_BUCKETS // 2
    nf = np.maximum(n, max_exact).astype(np.float32)
    large = max_exact + (np.log(nf / np.float32(max_exact)) / np.float32(math.log(MAX_DISTANCE / max_exact))
                         * np.float32(N_BUCKETS - max_exact)).astype(np.int32)
    return np.where(n < max_exact, n, np.minimum(large, N_BUCKETS - 1))


def _prompt_buckets():
    W = WINDOW
    dist = np.arange(W)[None, :] + W - np.arange(2 * W)[:, None]
    band = (dist >= 0) & (dist < W)
    return np.where(band, _t5_buckets(dist), -1).astype(np.int32)


def _swa_tables(rel_bias):
    W = WINDOW
    dist_c = W - np.arange(W)
    tab = jnp.transpose(rel_bias[_t5_buckets(dist_c)], (1, 0))
    cache_bias = jnp.where((dist_c < W)[None], tab, NEG_INF)
    return cache_bias, rel_bias[0]


def kernel(x_prompt, x_sample, mem_prompt, state_mlstm_C, state_mlstm_n, state_mlstm_m, state_conv, cache_swa_k, cache_swa_v, cache_mem_k, cache_mem_v, state_ffn_conv, rel_bias, norm1_g, w_in, b_i, b_f, mlstm_norm_g, conv_w, conv_b, conv_ln_g, conv_ln_b, swa_sinks, w_out, norm2_g, w_xq, w_xk, w_xv, w_xo, norm3_g, w_up, ffn_conv_w, ffn_conv_b, w_down, final_norm_g):
    depth = w_in.shape[0]
    bp, seq, d = x_prompt.shape
    bs = x_sample.shape[0]
    mem = mem_prompt.shape[1]
    dff = w_down.shape[1]
    W = WINDOW
    nh = C_KERNEL - 1

    xp = x_prompt
    xs = x_sample.reshape(bs, d)
    gf = final_norm_g.reshape(1, d)
    c_all = jnp.transpose(state_mlstm_C, (0, 2, 3, 4, 1))
    n_all = jnp.transpose(state_mlstm_n, (0, 2, 3, 1))
    m_all = jnp.transpose(state_mlstm_m, (0, 2, 1))
    hist_all = jnp.transpose(state_conv, (0, 2, 1, 3))
    kc_all = jnp.transpose(cache_swa_k, (0, 1, 3, 4, 2)).reshape(depth, bs, S_KV_WIDTH, W)
    vc_all = jnp.transpose(cache_swa_v, (0, 1, 3, 4, 2)).reshape(depth, bs, S_KV_WIDTH, W)
    mk_all = jnp.transpose(cache_mem_k, (0, 1, 3, 4, 2)).reshape(depth, bs, X_WIDTH, mem)
    mv_all = jnp.transpose(cache_mem_v, (0, 1, 3, 4, 2)).reshape(depth, bs, X_WIDTH, mem)
    pm_c, pm_n, pm_m, p_conv, p_k, p_v, p_mk, p_mv, p_ffn = ([] for _ in range(9))
    s_mlstm, s_conv, s_kv, s_ffn = [], [], [], []

    win_t = jnp.swapaxes(w_in, 1, 2)
    wout = w_out.astype(BF16)
    wxq = w_xq.astype(BF16)
    wxkv = jnp.concatenate([w_xk, w_xv], axis=2).astype(BF16)
    wxo = w_xo.astype(BF16)
    wup = w_up.astype(BF16)
    wdn = w_down.astype(BF16)
    bucket = jnp.asarray(_prompt_buckets())

    rows = lambda a: a.reshape(depth, 1, -1)
    g1, g2, g3 = rows(norm1_g), rows(norm2_g), rows(norm3_g)
    gb8 = jnp.concatenate([b_i, b_f], axis=1)
    gbias = rows(jnp.concatenate([gb8, jnp.zeros((depth, 128 - 2 * M_HEADS), F32)], axis=1))
    ng = rows(mlstm_norm_g)
    ngt = jnp.broadcast_to(mlstm_norm_g[:, :, None], (depth, M_WIDTH, bs))
    cw = jnp.concatenate([conv_w, jnp.zeros((depth, CONV_PAD - C_KERNEL, C_WIDTH), F32)], axis=1)
    cb, lg, lb = rows(conv_b), rows(conv_ln_g), rows(conv_ln_b)
    fcw = jnp.concatenate([ffn_conv_w, jnp.zeros((depth, SUBLANES - FFN_KERNEL, dff), F32)], axis=1)
    fcb = rows(ffn_conv_b)
    cache_bias, bias0 = _swa_tables(rel_bias)
    aux = jnp.concatenate([jnp.broadcast_to(bias0[None, :, None], (depth, S_Q_HEADS, 1)), swa_sinks[:, :, None],
                           jnp.zeros((depth, S_Q_HEADS, 126), F32)], axis=2)

    for l in range(depth):
        last = l == depth - 1

        mkv = _mem_kv(l, mem_prompt, wxkv)
        z, win = _in_proj(l, xp.reshape(bp * seq, d), g1, win_t, tm=1024)
        z = z.reshape(bp, seq, Z_WIDTH)
        hm, cpair, npair, mm = _mlstm_prompt(l, z, gbias, ng)
        hc, ctail = _conv_prompt(l, z, cw, cb, lg, lb, tc=min(1024, seq))
        hs, kt, vt = _swa_prompt(l, z, rel_bias, swa_sinks, bucket, qb=min(8, seq // W))
        xp = _mix_out_prompt(l, xp, hm, hc, hs, wout, g2, wxq, mkv, wxo, tm=1024)
        xp, ftail = _ffn_prompt(l, xp, g3, wup, fcw, fcb, wdn, gf, tm=512, final=last)
        half = lambda h: slice(HEAD_DIM * (h % 2), HEAD_DIM * (h % 2 + 1))
        pm_c.append(jnp.stack([jnp.swapaxes(cpair[:, h // 2, half(h), half(h)], 1, 2) for h in range(M_HEADS)], axis=1))
        pm_n.append(jnp.stack([npair[:, h // 2, h % 2, half(h)] for h in range(M_HEADS)], axis=1))
        pm_m.append(mm[:, :, 0, 0])
        p_conv.append(ctail[:, CONV_PAD - nh:, :])
        p_k.append(jnp.transpose(kt.reshape(bp, S_KV_HEADS, HEAD_DIM, W), (0, 3, 1, 2)))
        p_v.append(jnp.transpose(vt.reshape(bp, S_KV_HEADS, HEAD_DIM, W), (0, 3, 1, 2)))
        p_mk.append(jnp.transpose(mkv[:, 0:X_WIDTH, :].reshape(bp, X_HEADS, HEAD_DIM, mem), (0, 3, 1, 2)))
        p_mv.append(jnp.transpose(mkv[:, X_WIDTH:, :].reshape(bp, X_HEADS, HEAD_DIM, mem), (0, 3, 1, 2)))
        p_ffn.append(ftail[:, FFN_PAD - (FFN_KERNEL - 1):, :])

        zs, zst = _sample_in(l, xs, g1, win)
        hmt_s, *s_mlstm = _mlstm_sample(l, zst, gb8, ngt, c_all, n_all, m_all, s_mlstm)
        hc_s, *s_conv = _conv_sample(l, zs, hist_all, cw, cb, lg, lb, s_conv)
        hs_s, *s_kv = _swa_sample(l, zs, zs.reshape(bs, 1, Z_WIDTH), kc_all, vc_all, cache_bias, aux, s_kv)
        x1, qx = _mix_out_sample(l, xs, hmt_s, hc_s, hs_s.reshape(bs, S_WIDTH), wout, g2, wxq)
        ox = _xattn_sample(l, qx.reshape(bs, 1, X_WIDTH), mk_all, mv_all)
        xs, *s_ffn = _ffn_sample(l, x1, ox.reshape(bs, X_WIDTH), wxo, g3, wup, fcw, fcb, wdn, gf,
                                 state_ffn_conv, s_ffn, final=last)

    st = jnp.stack
    tr = jnp.transpose
    s_c, s_n, s_m = s_mlstm
    s_k, s_v = (a.reshape(depth, bs, S_KV_HEADS, HEAD_DIM, W) for a in s_kv)
    return (xp, xs.reshape(bs, 1, d),
            st(pm_c), st(pm_n), st(pm_m), st(p_conv), st(p_k), st(p_v), st(p_mk), st(p_mv), st(p_ffn),
            tr(s_c, (0, 4, 1, 2, 3)), tr(s_n, (0, 3, 1, 2)), tr(s_m, (0, 2, 1)),
            tr(s_conv[0], (0, 2, 1, 3)), tr(s_k, (0, 1, 4, 2, 3)), tr(s_v, (0, 1, 4, 2, 3)),
            s_ffn[0])
```

```python
import functools
import math

import numpy as np
import jax
import jax.numpy as jnp
from jax import lax
from jax.experimental import pallas as pl
from jax.experimental.pallas import tpu as pltpu

F32 = jnp.float32
BF16 = jnp.bfloat16
EPS = 1e-6
NEG_INF = float("-inf")

HEAD_DIM = 64
M_HEADS = 4
M_WIDTH = M_HEADS * HEAD_DIM
C_WIDTH = 256
C_KERNEL = 31
S_Q_HEADS = 8
S_KV_HEADS = 2
S_WIDTH = S_Q_HEADS * HEAD_DIM
S_KV_WIDTH = S_KV_HEADS * HEAD_DIM
WINDOW = 128
N_BUCKETS = 32
MAX_DISTANCE = 128
X_HEADS = 4
X_WIDTH = X_HEADS * HEAD_DIM
FFN_KERNEL = 3
QK_SCALE = HEAD_DIM ** -0.5
LOG2E = math.log2(math.e)

Z_M = 0
Z_C = 4 * M_WIDTH
Z_SQ = Z_C + 2 * C_WIDTH
Z_SKV = Z_SQ + S_WIDTH
Z_G = Z_SKV + 2 * S_KV_WIDTH
LANES = 128
SUBLANES = 8
Z_WIDTH = Z_G + LANES
VMEM_LIMIT = 56 * 1024 * 1024

M_CHUNK = 128
FFN_CHUNK = 256


def _params(*sem):
    return pltpu.CompilerParams(dimension_semantics=sem, vmem_limit_bytes=VMEM_LIMIT)


def _const_spec(shape):
    nd = len(shape)
    return pl.BlockSpec(shape, lambda *_: (0,) * nd, pipeline_mode=pl.Buffered(1))


def _weight_spec(w, l):
    nd = w.ndim - 1
    return pl.BlockSpec((None,) + w.shape[1:], lambda *_: (l,) + (0,) * nd, pipeline_mode=pl.Buffered(1))


def _rms(x, g):
    return x * lax.rsqrt(jnp.mean(x * x, axis=-1, keepdims=True) + EPS) * g


def _sigmoid(x):
    return 1.0 / (1.0 + jnp.exp(-x))


def _log_sigmoid(x):
    return jnp.minimum(x, 0.0) - jnp.log1p(jnp.exp(-jnp.abs(x)))


def _dot(a, b):
    return jnp.dot(a, b, preferred_element_type=F32)


def _dot_nt(a, b):
    return lax.dot_general(a, b, (((1,), (1,)), ((), ())), preferred_element_type=F32)


def _dot_tn(a, b):
    return lax.dot_general(a, b, (((0,), (0,)), ((), ())), preferred_element_type=F32)


IN_PROJ_ROWS = 512
IN_PROJ_COLS = 512
IN_GATES = 4 * M_WIDTH
W_PREP_ROWS = 256


def _in_proj_kernel(x_ref, g_ref, wt_ref, o_ref, wprep_ref, w_scr):
    d = x_ref.shape[1]

    @pl.when(pl.program_id(0) == 0)
    def _():
        n_gate = 2 * M_HEADS
        for src, dst, n in ((0, 0, IN_GATES), (IN_GATES + n_gate, IN_GATES, Z_G - IN_GATES)):
            for c in range(0, n, W_PREP_ROWS):
                w_scr[:, dst + c:dst + c + W_PREP_ROWS] = wt_ref[src + c:src + c + W_PREP_ROWS, :].T.astype(BF16)
        gate_rows = jnp.concatenate([wt_ref[IN_GATES:IN_GATES + n_gate, :], jnp.zeros((LANES - n_gate, d), F32)],
                                    axis=0)
        w_scr[:, Z_G:Z_G + LANES] = gate_rows.T.astype(BF16)
        wprep_ref[...] = w_scr[...]

    tm, n = o_ref.shape
    sub = min(IN_PROJ_ROWS, tm)
    for r0 in range(0, tm, sub):
        h = _rms(x_ref[r0:r0 + sub, :], g_ref[...]).astype(BF16)
        for c in range(0, n, IN_PROJ_COLS):
            w = min(IN_PROJ_COLS, n - c)
            o_ref[r0:r0 + sub, c:c + w] = _dot(h, w_scr[:, c:c + w])


def _in_proj(l, x, g, wt, *, tm):
    m, d = x.shape
    tm = min(tm, m)
    return pl.pallas_call(
        _in_proj_kernel,
        grid=(m // tm,),
        in_specs=[pl.BlockSpec((tm, d), lambda i: (i, 0)), _weight_spec(g, l), _weight_spec(wt, l)],
        out_specs=[pl.BlockSpec((tm, Z_WIDTH), lambda i: (i, 0)), pl.BlockSpec((d, Z_WIDTH), lambda i: (0, 0))],
        out_shape=[jax.ShapeDtypeStruct((m, Z_WIDTH), F32), jax.ShapeDtypeStruct((d, Z_WIDTH), BF16)],
        scratch_shapes=[pltpu.VMEM((d, Z_WIDTH), BF16)],
        compiler_params=_params("arbitrary"),
        name="in_proj",
    )(x, g, wt)


def _mem_kv_kernel(x_ref, w_ref, o_ref, acc):
    acc[...] = _dot(x_ref[...].astype(BF16), w_ref[...])
    o_ref[...] = acc[...].T


def _mem_kv(l, mem3, w):
    b, mem, d = mem3.shape
    n = w.shape[2]
    return pl.pallas_call(
        _mem_kv_kernel,
        grid=(b,),
        in_specs=[pl.BlockSpec((None, mem, d), lambda i: (i, 0, 0)), _weight_spec(w, l)],
        out_specs=pl.BlockSpec((None, n, mem), lambda i: (i, 0, 0)),
        out_shape=jax.ShapeDtypeStruct((b, n, mem), F32),
        scratch_shapes=[pltpu.VMEM((mem, n), F32)],
        compiler_params=_params("arbitrary"),
        name="mem_kv",
    )(mem3, w)


def _sample_in_kernel(x_ref, g_ref, w_ref, z_ref, zt_ref):
    h = _rms(x_ref[...], g_ref[...]).astype(BF16)
    n = z_ref.shape[1]
    for c in range(0, n, IN_PROJ_COLS):
        w = min(IN_PROJ_COLS, n - c)
        zc = _dot(h, w_ref[:, c:c + w])
        z_ref[:, c:c + w] = zc
        zt_ref[c:c + w, :] = zc.T


def _sample_in(l, x, g, w):
    m, d = x.shape
    n = w.shape[1]
    full = lambda shape: pl.BlockSpec(shape, lambda i: (0,) * len(shape))
    return pl.pallas_call(
        _sample_in_kernel,
        grid=(1,),
        in_specs=[full((m, d)), _weight_spec(g, l), full((d, n))],
        out_specs=[full((m, n)), full((n, m))],
        out_shape=[jax.ShapeDtypeStruct((m, n), F32), jax.ShapeDtypeStruct((n, m), F32)],
        compiler_params=_params("arbitrary"),
        name="sample_in",
    )(x, g, w)


def _mlstm_prompt_kernel(z_ref, g_ref, gb_ref, ng_ref, h_ref, cp_ref, np_ref, m_ref,
                         cp_scr, np_scr, m_scr, ht_scr, st_scr, kw_scr):
    NB, L = z_ref.shape[0], z_ref.shape[1]
    D = HEAD_DIM
    hi = lax.Precision.HIGHEST

    @pl.when(pl.program_id(0) == 0)
    def _():
        cp_scr[...] = jnp.zeros(cp_scr.shape, F32)
        np_scr[...] = jnp.zeros(np_scr.shape, F32)
        m_scr[...] = jnp.zeros(m_scr.shape, F32)

    src = lax.broadcasted_iota(jnp.int32, (L, L), 0)
    qry = lax.broadcasted_iota(jnp.int32, (L, L), 1)
    causal_t = src <= qry
    upper = jnp.where(causal_t, 1.0, 0.0)
    lane_half = lax.broadcasted_iota(jnp.int32, (L, LANES), 1) // D
    row8 = lax.broadcasted_iota(jnp.int32, (SUBLANES, LANES), 0)

    rows, cols = {}, []
    for b in range(NB):
        g_t = (g_ref[b] + gb_ref[...]).T[0:SUBLANES, :]
        b_rows = jnp.dot(_log_sigmoid(g_t), upper, precision=hi, preferred_element_type=F32)
        to_cols = []
        for h in range(M_HEADS):
            b_row = b_rows[M_HEADS + h:M_HEADS + h + 1, :]
            ci_row = g_t[h:h + 1, :] - b_row
            m_prev = m_scr[b, h, 0:1, 0:1]
            b_last = b_row[:, L - 1:L]
            m_new = jnp.maximum(b_last + m_prev, jnp.max(b_last + ci_row, axis=1, keepdims=True))
            rows[b, h] = dict(b_row=b_row, m_prev=m_prev, m_new=m_new, decay=jnp.exp(b_last + m_prev - m_new))
            to_cols += [ci_row, jnp.exp(b_last + ci_row - m_new)]
        to_cols.append(jnp.zeros((LANES - 2 * M_HEADS, L), F32))
        cols.append(jnp.concatenate(to_cols, axis=0).T)

    stats = {}
    for b in range(NB):
        for j in range(M_HEADS // 2):
            slab = LANES * j
            q2 = z_ref[b, :, slab:slab + LANES].astype(BF16)
            k2 = z_ref[b, :, M_WIDTH + slab:M_WIDTH + slab + LANES] * QK_SCALE
            for par in range(2):
                h = 2 * j + par
                r = rows[b, h]
                ci_col = cols[b][:, 2 * h:2 * h + 1]
                a_row = r["b_row"] + r["m_prev"]
                dm = jnp.where(causal_t, r["b_row"] + ci_col, NEG_INF)
                m_row = jnp.maximum(a_row, jnp.max(dm, axis=0, keepdims=True))
                k_own = jnp.where(lane_half == par, k2, 0.0).astype(BF16)
                s_t = _dot_nt(k_own, q2) * jnp.exp(dm - m_row)
                st_scr[b, j, :, L * par:L * (par + 1)] = s_t.astype(BF16)
                stats[b, h] = dict(w_inter=jnp.exp(a_row - m_row), floor=jnp.exp(-m_row),
                                   den_s=jnp.sum(s_t, axis=0, keepdims=True),
                                   decay=r["decay"], m_new=r["m_new"])
            wk = jnp.where(lane_half == 0, cols[b][:, 4 * j + 1:4 * j + 2], cols[b][:, 4 * j + 3:4 * j + 4])
            kw = k2 * wk
            kw_scr[b, j] = kw.astype(BF16)
            stats[b, j, "k_sum"] = jnp.sum(kw, axis=0, keepdims=True)

    block_diag = (lax.broadcasted_iota(jnp.int32, (LANES, LANES), 0) // D
                  == lax.broadcasted_iota(jnp.int32, (LANES, LANES), 1) // D)
    for b in range(NB):
        for j in range(M_HEADS // 2):
            slab = LANES * j
            q2 = z_ref[b, :, slab:slab + LANES].astype(BF16)
            v2 = z_ref[b, :, 2 * M_WIDTH + slab:2 * M_WIDTH + slab + LANES].astype(BF16)
            cp = cp_scr[b, j]
            npair = np_scr[b, j]
            qc = _dot_nt(cp.astype(BF16), q2)
            qn = _dot_nt(npair.astype(BF16), q2)
            pv = _dot_tn(v2, st_scr[b, j])
            upd = jnp.where(block_diag, _dot_tn(v2, kw_scr[b, j]), 0.0)
            n_new = jnp.zeros((SUBLANES, LANES), F32)
            for par in range(2):
                h = 2 * j + par
                st = stats[b, h]
                hr = slice(D * par, D * (par + 1))
                num = st["w_inter"] * qc[hr, :] + pv[hr, L * par:L * (par + 1)]
                den = st["w_inter"] * qn[par:par + 1, :] + st["den_s"]
                hh = num * (1.0 / jnp.maximum(jnp.abs(den), st["floor"]))
                hh = hh * lax.rsqrt(jnp.mean(hh * hh, axis=0, keepdims=True) + EPS)
                ht_scr[b, D * h:D * (h + 1), :] = hh
                cp_scr[b, j, hr, :] = st["decay"] * cp[hr, :] + upd[hr, :]
                k_sum = jnp.where(lane_half[0:1, :] == par, stats[b, j, "k_sum"], 0.0)
                n_new = jnp.where(row8 == par, st["decay"] * npair[par:par + 1, :] + k_sum, n_new)
                m_scr[b, h] = jnp.broadcast_to(st["m_new"], m_scr.shape[2:])
            np_scr[b, j] = n_new
        o_gate = _sigmoid(z_ref[b, :, 3 * M_WIDTH:4 * M_WIDTH])
        h_ref[b] = ht_scr[b].T * ng_ref[...] * o_gate

    cp_ref[...] = cp_scr[...]
    np_ref[...] = np_scr[...]
    m_ref[...] = m_scr[...]


def _mlstm_prompt(l, z3, gbias, ng):
    b, s, _ = z3.shape
    L = M_CHUNK
    P = M_HEADS // 2
    return pl.pallas_call(
        _mlstm_prompt_kernel,
        grid=(s // L,),
        in_specs=[
            pl.BlockSpec((b, L, 4 * M_WIDTH), lambda c: (0, c, Z_M // (4 * M_WIDTH))),
            pl.BlockSpec((b, L, LANES), lambda c: (0, c, Z_G // LANES)),
            _weight_spec(gbias, l),
            _weight_spec(ng, l),
        ],
        out_specs=[
            pl.BlockSpec((b, L, M_WIDTH), lambda c: (0, c, 0)),
            pl.BlockSpec((b, P, LANES, LANES), lambda c: (0, 0, 0, 0)),
            pl.BlockSpec((b, P, SUBLANES, LANES), lambda c: (0, 0, 0, 0)),
            pl.BlockSpec((b, M_HEADS, SUBLANES, LANES), lambda c: (0, 0, 0, 0)),
        ],
        out_shape=[
            jax.ShapeDtypeStruct((b, s, M_WIDTH), F32),
            jax.ShapeDtypeStruct((b, P, LANES, LANES), F32),
            jax.ShapeDtypeStruct((b, P, SUBLANES, LANES), F32),
            jax.ShapeDtypeStruct((b, M_HEADS, SUBLANES, LANES), F32),
        ],
        scratch_shapes=[
            pltpu.VMEM((b, P, LANES, LANES), F32),
            pltpu.VMEM((b, P, SUBLANES, LANES), F32),
            pltpu.VMEM((b, M_HEADS, SUBLANES, LANES), F32),
            pltpu.VMEM((b, M_WIDTH, L), F32),
            pltpu.VMEM((b, P, L, 2 * L), BF16),
            pltpu.VMEM((b, P, L, LANES), BF16),
        ],
        compiler_params=_params("arbitrary"),
        name="mlstm_prompt",
    )(z3, z3, gbias, ng)


CONV_PAD = 32


def _conv_prompt_kernel(z_ref, w_ref, cb_ref, lg_ref, lb_ref, h_ref, tail_ref, buf, shifted):
    tc = z_ref.shape[0]
    t = pl.program_id(1)

    @pl.when(t == 0)
    def _():
        buf[0:CONV_PAD, :] = jnp.zeros((CONV_PAD, C_WIDTH), F32)

    @pl.when(t > 0)
    def _():
        buf[0:CONV_PAD, :] = buf[tc:tc + CONV_PAD, :]

    u = z_ref[:, 0:C_WIDTH] * _sigmoid(z_ref[:, C_WIDTH:2 * C_WIDTH])
    buf[CONV_PAD:CONV_PAD + tc, :] = u
    n_sh = shifted.shape[1]
    for r in range(1, SUBLANES):
        shifted[r - 1] = buf[r:r + n_sh, :]
    off = CONV_PAD - (C_KERNEL - 1)
    acc = jnp.broadcast_to(cb_ref[...], (tc, C_WIDTH))
    for j in range(C_KERNEL):
        r = (off + j) % SUBLANES
        base = off + j - r
        win = buf[base:base + tc, :] if r == 0 else shifted[r - 1, base:base + tc, :]
        acc = acc + w_ref[j:j + 1, :] * win
    mu = jnp.mean(acc, axis=-1, keepdims=True)
    xc = acc - mu
    y = xc * lax.rsqrt(jnp.mean(xc * xc, axis=-1, keepdims=True) + EPS) * lg_ref[...] + lb_ref[...]
    h_ref[...] = y * _sigmoid(y)
    tail_ref[...] = buf[tc:tc + CONV_PAD, :]


def _conv_prompt(l, z3, w, cb, lg, lb, *, tc):
    b, s, _ = z3.shape
    return pl.pallas_call(
        _conv_prompt_kernel,
        grid=(b, s // tc),
        in_specs=[
            pl.BlockSpec((None, tc, 2 * C_WIDTH), lambda i, t: (i, t, Z_C // (2 * C_WIDTH))),
            _weight_spec(w, l), _weight_spec(cb, l), _weight_spec(lg, l), _weight_spec(lb, l),
        ],
        out_specs=[
            pl.BlockSpec((None, tc, C_WIDTH), lambda i, t: (i, t, 0)),
            pl.BlockSpec((None, CONV_PAD, C_WIDTH), lambda i, t: (i, 0, 0)),
        ],
        out_shape=[
            jax.ShapeDtypeStruct((b, s, C_WIDTH), F32),
            jax.ShapeDtypeStruct((b, CONV_PAD, C_WIDTH), F32),
        ],
        scratch_shapes=[pltpu.VMEM((CONV_PAD + tc, C_WIDTH), F32),
                        pltpu.VMEM((SUBLANES - 1, CONV_PAD + tc - SUBLANES, C_WIDTH), F32)],
        compiler_params=_params("arbitrary", "arbitrary"),
        name="conv_prompt",
    )(z3, w, cb, lg, lb)


def _swa_prompt_kernel(rb_ref, sink_ref, bucket_ref, q_ref, kv_ref, kvp_ref, o_ref, kt_ref, vt_ref,
                       bias_scr, s_scr, p_scr, ot_scr, *, layer):
    W = WINDOW
    H = S_Q_HEADS
    G = H // S_KV_HEADS
    i = pl.program_id(0)
    n = pl.program_id(1)

    @pl.when(jnp.logical_and(i == 0, n == 0))
    def _():
        bucket = bucket_ref[...]
        prev_key = lax.broadcasted_iota(jnp.int32, (2 * W, W), 0) < W
        for h in range(H):
            acc = jnp.full((2 * W, W), NEG_INF, F32)
            for b in range(N_BUCKETS):
                acc = jnp.where(bucket == b, rb_ref[b, h], acc)
            acc = acc * LOG2E
            bias_scr[0, h] = acc
            bias_scr[1, h] = jnp.where(prev_key, NEG_INF, acc)

    QB = q_ref.shape[0] // W
    NK = (QB + 1) * W
    first = jnp.where(n == 0, 1, 0)
    kk = jnp.concatenate([kvp_ref[:, 0:S_KV_WIDTH], kv_ref[:, 0:S_KV_WIDTH]], axis=0) * (QK_SCALE * LOG2E)
    vv = jnp.concatenate([kvp_ref[:, S_KV_WIDTH:2 * S_KV_WIDTH], kv_ref[:, S_KV_WIDTH:2 * S_KV_WIDTH]], axis=0)
    kk_r = pltpu.roll(kk, HEAD_DIM, axis=1)
    lo_lane = lax.broadcasted_iota(jnp.int32, (NK, S_KV_WIDTH), 1) < HEAD_DIM
    k_var = [[jnp.where(lo_lane, kk, 0.0).astype(BF16), jnp.where(lo_lane, 0.0, kk_r).astype(BF16)],
             [jnp.where(lo_lane, kk_r, 0.0).astype(BF16), jnp.where(lo_lane, 0.0, kk).astype(BF16)]]
    v_t = vv.T.astype(BF16)

    for j in range(QB):
        k0 = j * W
        masked = first if j == 0 else 0
        m_rows = [None] * H
        for hk in range(S_KV_HEADS):
            c0 = 2 * LANES * hk
            q_st = jnp.concatenate([q_ref[k0:k0 + W, c0:c0 + LANES], q_ref[k0:k0 + W, c0 + LANES:c0 + 2 * LANES]],
                                   axis=0).astype(BF16)
            for half in range(2):
                s_t = _dot_nt(k_var[hk][half][k0:k0 + 2 * W, :], q_st)
                for slab in range(2):
                    head = G * hk + 2 * slab + half
                    sb = s_t[:, W * slab:W * (slab + 1)] + bias_scr[masked, head]
                    s_scr[j, head] = sb
                    m_rows[head] = jnp.maximum(jnp.max(sb, axis=0, keepdims=True), sink_ref[layer, head] * LOG2E)

        inv = [None] * H
        for head in range(H):
            e = jnp.exp2(s_scr[j, head] - m_rows[head])
            den = jnp.sum(e, axis=0, keepdims=True) + jnp.exp2(sink_ref[layer, head] * LOG2E - m_rows[head])
            inv[head] = 1.0 / den
            p_scr[j, :, W * head:W * (head + 1)] = e.astype(BF16)

        for hk in range(S_KV_HEADS):
            o_t = _dot(v_t[HEAD_DIM * hk:HEAD_DIM * (hk + 1), k0:k0 + 2 * W],
                       p_scr[j, :, W * G * hk:W * G * (hk + 1)])
            for g in range(G):
                head = G * hk + g
                ot_scr[j, HEAD_DIM * head:HEAD_DIM * (head + 1), :] = o_t[:, W * g:W * (g + 1)] * inv[head]
        o_ref[k0:k0 + W, :] = ot_scr[j].T

    @pl.when(n == pl.num_programs(1) - 1)
    def _():
        kt_ref[...] = kv_ref[(QB - 1) * W:QB * W, 0:S_KV_WIDTH].T
        vt_ref[...] = kv_ref[(QB - 1) * W:QB * W, S_KV_WIDTH:2 * S_KV_WIDTH].T


def _swa_prompt(l, z3, rel_bias, sinks, bucket, *, qb):
    b, s, _ = z3.shape
    W = WINDOW
    smem = pl.BlockSpec(memory_space=pltpu.SMEM)
    return pl.pallas_call(
        functools.partial(_swa_prompt_kernel, layer=l),
        grid=(b, s // (qb * W)),
        in_specs=[
            smem, smem, _const_spec((2 * W, W)),
            pl.BlockSpec((None, qb * W, S_WIDTH), lambda i, n: (i, n, Z_SQ // S_WIDTH)),
            pl.BlockSpec((None, qb * W, 2 * S_KV_WIDTH), lambda i, n: (i, n, Z_SKV // (2 * S_KV_WIDTH))),
            pl.BlockSpec((None, W, 2 * S_KV_WIDTH),
                         lambda i, n: (i, jnp.maximum(n * qb - 1, 0), Z_SKV // (2 * S_KV_WIDTH))),
        ],
        out_specs=[
            pl.BlockSpec((None, qb * W, S_WIDTH), lambda i, n: (i, n, 0)),
            pl.BlockSpec((None, S_KV_WIDTH, W), lambda i, n: (i, 0, 0)),
            pl.BlockSpec((None, S_KV_WIDTH, W), lambda i, n: (i, 0, 0)),
        ],
        out_shape=[
            jax.ShapeDtypeStruct((b, s, S_WIDTH), F32),
            jax.ShapeDtypeStruct((b, S_KV_WIDTH, W), F32),
            jax.ShapeDtypeStruct((b, S_KV_WIDTH, W), F32),
        ],
        scratch_shapes=[
            pltpu.VMEM((2, S_Q_HEADS, 2 * W, W), F32),
            pltpu.VMEM((qb, S_Q_HEADS, 2 * W, W), F32),
            pltpu.VMEM((qb, 2 * W, S_Q_HEADS * W), BF16),
            pltpu.VMEM((qb, S_WIDTH, W), F32),
        ],
        compiler_params=_params("arbitrary", "arbitrary"),
        name="swa_prompt",
    )(rel_bias, sinks, bucket, z3, z3, z3)


MIX_SUB = 512


def _mix_out_prompt_kernel(x_ref, hm_ref, hc_ref, hs_ref, wout_ref, g2_ref, wq_ref, mk_ref, mv_ref, wo_ref, o_ref,
                           x1_scr, qx_scr, p_scr):
    tm = x_ref.shape[0]
    mem = mk_ref.shape[1]
    mk_t = mk_ref[...]
    mv_t = mv_ref[...]
    row_head = lax.broadcasted_iota(jnp.int32, mk_t.shape, 0) // HEAD_DIM
    k_heads = [jnp.where(row_head == h, mk_t, 0.0).astype(BF16) for h in range(X_HEADS)]
    v_cat = jnp.concatenate([jnp.where(row_head == h, mv_t, 0.0).astype(BF16) for h in range(X_HEADS)], axis=1)
    subs = [slice(r0, r0 + MIX_SUB) for r0 in range(0, tm, MIX_SUB)]
    for rows in subs:
        cat = jnp.concatenate([hm_ref[rows, :].astype(BF16), hc_ref[rows, :].astype(BF16),
                               hs_ref[rows, :].astype(BF16)], axis=1)
        x1 = x_ref[rows, :] + _dot(cat, wout_ref[...])
        x1_scr[rows, :] = x1
        qx = _dot(_rms(x1, g2_ref[...]).astype(BF16), wq_ref[...])
        qx_scr[rows, :] = (qx * (QK_SCALE * LOG2E)).astype(BF16)
    for rows in subs:
        qx = qx_scr[rows, :]
        for h in range(X_HEADS):
            s = _dot(qx, k_heads[h])
            e = jnp.exp2(s - jnp.max(s, axis=1, keepdims=True))
            p_scr[rows, mem * h:mem * (h + 1)] = (e * (1.0 / jnp.sum(e, axis=1, keepdims=True))).astype(BF16)
    for rows in subs:
        o = _dot_nt(p_scr[rows, :], v_cat)
        o_ref[rows, :] = x1_scr[rows, :] + _dot(o.astype(BF16), wo_ref[...])


def _mix_out_prompt(l, x3, hm, hc, hs, wout, g2, wq, mkv, wo, *, tm):
    b, s, d = x3.shape
    tm = min(tm, s)
    mem = mkv.shape[2]
    row = lambda w: pl.BlockSpec((None, tm, w), lambda i, t: (i, t, 0))
    return pl.pallas_call(
        _mix_out_prompt_kernel,
        grid=(b, s // tm),
        in_specs=[
            row(d), row(M_WIDTH), row(C_WIDTH), row(S_WIDTH),
            _weight_spec(wout, l), _weight_spec(g2, l), _weight_spec(wq, l),
            pl.BlockSpec((None, X_WIDTH, mem), lambda i, t: (i, 0, 0)),
            pl.BlockSpec((None, X_WIDTH, mem), lambda i, t: (i, 1, 0)),
            _weight_spec(wo, l),
        ],
        out_specs=row(d),
        out_shape=jax.ShapeDtypeStruct((b, s, d), F32),
        scratch_shapes=[pltpu.VMEM((tm, d), F32), pltpu.VMEM((tm, X_WIDTH), BF16),
                        pltpu.VMEM((tm, X_HEADS * mem), BF16)],
        compiler_params=_params("arbitrary", "arbitrary"),
        name="mix_out_prompt",
    )(x3, hm, hc, hs, wout, g2, wq, mkv, mkv, wo)


FFN_PAD = 8


def _ffn_prompt_kernel(x_ref, g3_ref, wup_ref, cw_ref, cb_ref, wdn_ref, gf_ref, o_ref, tail_ref,
                       gbuf, carry, act, *, final):
    tm = x_ref.shape[0]
    dff = wdn_ref.shape[0]

    @pl.when(pl.program_id(1) == 0)
    def _():
        carry[...] = jnp.zeros(carry.shape, F32)

    x = x_ref[...]
    h = _rms(x, g3_ref[...]).astype(BF16)
    w = FFN_CHUNK
    for c in range(0, dff, w):
        a = _dot(h, wup_ref[:, c:c + w])
        g = _dot(h, wup_ref[:, dff + c:dff + c + w])
        gbuf[0:FFN_PAD, 0:w] = carry[:, c:c + w]
        gbuf[FFN_PAD:FFN_PAD + tm, 0:w] = g
        carry[:, c:c + w] = g[tm - FFN_PAD:tm, :]
        gc = (cw_ref[0:1, c:c + w] * gbuf[FFN_PAD - 2:FFN_PAD - 2 + tm, 0:w]
              + cw_ref[1:2, c:c + w] * gbuf[FFN_PAD - 1:FFN_PAD - 1 + tm, 0:w]
              + cw_ref[2:3, c:c + w] * g + cb_ref[:, c:c + w])
        act[:, c:c + w] = (gc * _sigmoid(gc) * a).astype(BF16)
    y = x + _dot(act[...], wdn_ref[...])
    o_ref[...] = _rms(y, gf_ref[...]) if final else y
    tail_ref[...] = carry[...]


def _ffn_prompt(l, x3, g3, wup, cw, cb, wdn, gf, *, tm, final):
    b, s, d = x3.shape
    tm = min(tm, s)
    dff = wdn.shape[1]
    return pl.pallas_call(
        functools.partial(_ffn_prompt_kernel, final=final),
        grid=(b, s // tm),
        in_specs=[
            pl.BlockSpec((None, tm, d), lambda i, t: (i, t, 0)),
            _weight_spec(g3, l), _weight_spec(wup, l), _weight_spec(cw, l), _weight_spec(cb, l),
            _weight_spec(wdn, l), _const_spec((1, d)),
        ],
        out_specs=[
            pl.BlockSpec((None, tm, d), lambda i, t: (i, t, 0)),
            pl.BlockSpec((None, FFN_PAD, dff), lambda i, t: (i, 0, 0)),
        ],
        out_shape=[jax.ShapeDtypeStruct((b, s, d), F32), jax.ShapeDtypeStruct((b, FFN_PAD, dff), F32)],
        scratch_shapes=[
            pltpu.VMEM((FFN_PAD + tm, FFN_CHUNK), F32),
            pltpu.VMEM((FFN_PAD, dff), F32),
            pltpu.VMEM((tm, dff), BF16),
        ],
        compiler_params=_params("arbitrary", "arbitrary"),
        name="ffn_prompt",
    )(x3, g3, wup, cw, cb, wdn, gf)


SWA_SAMPLE_BLOCK = 32
XATTN_SAMPLE_BLOCK = 16
CONV_SAMPLE_BLOCK = 32


def _mlstm_sample_kernel(gb_ref, q_ref, k_ref, v_ref, o_ref, g_ref, ng_ref, c_ref, n_ref, m_ref, *rest, layer):
    h_ref, co_ref, no_ref, mo_ref, kw_scr = rest[-5:]
    _copy_earlier_layers(rest[:-5], (co_ref, no_ref, mo_ref))
    h = pl.program_id(0)
    i_pre = g_ref[pl.ds(h, 1), :] + gb_ref[layer, h]
    f_pre = g_ref[pl.ds(M_HEADS + h, 1), :] + gb_ref[layer, M_HEADS + h]
    a = _log_sigmoid(f_pre) + m_ref[pl.ds(h, 1), :]
    m_t = jnp.maximum(a, i_pre)
    w_old = jnp.exp(a - m_t)
    w_new = jnp.exp(i_pre - m_t)
    q = q_ref[...]
    k = k_ref[...] * QK_SCALE
    v = v_ref[...]
    n_old = n_ref[...]
    kw_scr[...] = k * w_new

    def body(d, acc):
        c_old = c_ref[d]
        co_ref[layer, d] = w_old * c_old + kw_scr[pl.ds(d, 1), :] * v
        return acc + q_ref[pl.ds(d, 1), :] * c_old

    qc = lax.fori_loop(0, HEAD_DIM, body, jnp.zeros(v.shape, F32), unroll=8)
    s = jnp.sum(q * k, axis=0, keepdims=True) * w_new
    num = w_old * qc + s * v
    den = w_old * jnp.sum(q * n_old, axis=0, keepdims=True) + s
    hh = num / jnp.maximum(jnp.abs(den), jnp.exp(-m_t))
    hh = hh * lax.rsqrt(jnp.mean(hh * hh, axis=0, keepdims=True) + EPS) * ng_ref[...]
    h_ref[...] = hh * _sigmoid(o_ref[...])
    no_ref[layer] = w_old * n_old + kw_scr[...]
    mo_ref[layer, pl.ds(h, 1), :] = m_t


def _mlstm_sample(l, zt, gb, ngt, c_all, n_all, m_all, prev):
    bsz = zt.shape[1]
    D = HEAD_DIM
    feat = lambda off: pl.BlockSpec((D, bsz), lambda h: (off // D + h, 0))
    c_spec = lambda n: pl.BlockSpec((n, None, D, D, bsz), lambda h: (0, h, 0, 0, 0))
    n_spec = lambda n: pl.BlockSpec((n, None, D, bsz), lambda h: (0, h, 0, 0))
    m_spec = lambda n: pl.BlockSpec((n, M_HEADS, bsz), lambda h: (0, 0, 0))
    prev_specs = [c_spec(l), n_spec(l), m_spec(l)] if prev else []
    return pl.pallas_call(
        functools.partial(_mlstm_sample_kernel, layer=l),
        grid=(M_HEADS,),
        in_specs=[
            pl.BlockSpec(memory_space=pltpu.SMEM),
            feat(Z_M), feat(Z_M + M_WIDTH), feat(Z_M + 2 * M_WIDTH), feat(Z_M + 3 * M_WIDTH),
            pl.BlockSpec((SUBLANES, bsz), lambda h: (Z_G // SUBLANES, 0)),
            pl.BlockSpec((None, D, bsz), lambda h: (l, h, 0)),
            pl.BlockSpec((None, None, D, D, bsz), lambda h: (l, h, 0, 0, 0)),
            pl.BlockSpec((None, None, D, bsz), lambda h: (l, h, 0, 0)),
            pl.BlockSpec((None, M_HEADS, bsz), lambda h: (l, 0, 0)),
        ] + prev_specs,
        out_specs=[pl.BlockSpec((D, bsz), lambda h: (h, 0)), c_spec(l + 1), n_spec(l + 1), m_spec(l + 1)],
        out_shape=[
            jax.ShapeDtypeStruct((M_WIDTH, bsz), F32),
            jax.ShapeDtypeStruct((l + 1, M_HEADS, D, D, bsz), F32),
            jax.ShapeDtypeStruct((l + 1, M_HEADS, D, bsz), F32),
            jax.ShapeDtypeStruct((l + 1, M_HEADS, bsz), F32),
        ],
        scratch_shapes=[pltpu.VMEM((D, bsz), F32)],
        compiler_params=_params("arbitrary"),
        name="mlstm_sample",
    )(gb, zt, zt, zt, zt, zt, ngt, c_all, n_all, m_all, *prev)


def _conv_sample_kernel(z_ref, hist_ref, w_ref, cb_ref, lg_ref, lb_ref, *rest):
    h_ref, hist_out_ref = rest[-2:]
    _copy_earlier_layers(rest[:-2], (hist_out_ref,))
    l = hist_out_ref.shape[0] - 1
    nh = C_KERNEL - 1
    u = z_ref[:, 0:C_WIDTH] * _sigmoid(z_ref[:, C_WIDTH:2 * C_WIDTH])
    acc = cb_ref[...] + w_ref[nh:nh + 1, :] * u
    for j in range(nh):
        acc = acc + w_ref[j:j + 1, :] * hist_ref[j]
    mu = jnp.mean(acc, axis=-1, keepdims=True)
    xc = acc - mu
    y = xc * lax.rsqrt(jnp.mean(xc * xc, axis=-1, keepdims=True) + EPS) * lg_ref[...] + lb_ref[...]
    h_ref[...] = y * _sigmoid(y)
    for j in range(nh - 1):
        hist_out_ref[l, j] = hist_ref[j + 1]
    hist_out_ref[l, nh - 1] = u


def _conv_sample(l, z, hist_all, w, cb, lg, lb, prev):
    bsz = z.shape[0]
    nh = hist_all.shape[1]
    R = min(CONV_SAMPLE_BLOCK, bsz)
    hist_spec = lambda n: pl.BlockSpec((n, nh, R, C_WIDTH), lambda i: (0, 0, i, 0))
    return pl.pallas_call(
        _conv_sample_kernel,
        grid=(bsz // R,),
        in_specs=[
            pl.BlockSpec((R, 2 * C_WIDTH), lambda i: (i, Z_C // (2 * C_WIDTH))),
            pl.BlockSpec((None, nh, R, C_WIDTH), lambda i: (l, 0, i, 0)),
            _weight_spec(w, l), _weight_spec(cb, l), _weight_spec(lg, l), _weight_spec(lb, l),
        ] + [hist_spec(l)] * len(prev),
        out_specs=[pl.BlockSpec((R, C_WIDTH), lambda i: (i, 0)), hist_spec(l + 1)],
        out_shape=[jax.ShapeDtypeStruct((bsz, C_WIDTH), F32), jax.ShapeDtypeStruct((l + 1, nh, bsz, C_WIDTH), F32)],
        compiler_params=_params("arbitrary"),
        name="conv_sample",
    )(z, hist_all, w, cb, lg, lb, *prev)


def _copy_earlier_layers(prev_refs, out_refs):
    for p_ref, o_ref in zip(prev_refs, out_refs):
        o_ref[0:p_ref.shape[0]] = p_ref[...]


def _swa_sample_kernel(q_ref, kvn_ref, kv2_ref, kc_ref, vc_ref, bias_ref, aux_ref, *rest):
    o_ref, ko_ref, vo_ref = rest[-3:]
    _copy_earlier_layers(rest[:-3], (ko_ref, vo_ref))
    l = ko_ref.shape[0] - 1
    R = q_ref.shape[0]
    W = kc_ref.shape[2]
    H = S_Q_HEADS
    shape = (R, H, LANES)
    row = lax.broadcasted_iota(jnp.int32, shape, 1)
    lane_half = lax.broadcasted_iota(jnp.int32, shape, 2) // HEAD_DIM
    q_half = row % 2
    kv_head = row // (H // S_KV_HEADS)
    qs = jnp.zeros(shape, F32)
    for j in range(H // 2):
        qs = jnp.where(row // 2 == j, q_ref[:, :, LANES * j:LANES * (j + 1)], qs)
    q8 = jnp.where(lane_half == kv_head, jnp.where(q_half == kv_head, qs, pltpu.roll(qs, HEAD_DIM, axis=2)), 0.0)
    k_new = kvn_ref[:, :, 0:S_KV_WIDTH]
    v_new = kvn_ref[:, :, S_KV_WIDTH:2 * S_KV_WIDTH]
    s = jnp.einsum("bqd,bdk->bqk", q8.astype(BF16), kc_ref[...].astype(BF16), preferred_element_type=F32) * QK_SCALE
    s = s + bias_ref[...][None]
    s_new = jnp.sum(q8 * k_new, axis=2, keepdims=True) * QK_SCALE + aux_ref[:, 0:1][None]
    sink = aux_ref[:, 1:2][None]
    m = jnp.maximum(jnp.maximum(jnp.max(s, axis=2, keepdims=True), s_new), sink)
    e = jnp.exp(s - m)
    e_new = jnp.exp(s_new - m)
    inv = 1.0 / (jnp.sum(e, axis=2, keepdims=True) + e_new + jnp.exp(sink - m))
    o8 = jnp.einsum("bqk,bdk->bqd", e.astype(BF16), vc_ref[...].astype(BF16), preferred_element_type=F32)
    o8 = (o8 + e_new * v_new) * inv
    o8 = jnp.where(lane_half == q_half, jnp.where(q_half == kv_head, o8, pltpu.roll(o8, HEAD_DIM, axis=2)), 0.0)
    for j in range(H // 2):
        o_ref[:, :, LANES * j:LANES * (j + 1)] = jnp.sum(jnp.where(row // 2 == j, o8, 0.0), axis=1, keepdims=True)
    k_cols = kv2_ref[:, 0:S_KV_WIDTH].T
    v_cols = kv2_ref[:, S_KV_WIDTH:2 * S_KV_WIDTH].T
    last = lax.broadcasted_iota(jnp.int32, (S_KV_WIDTH, W), 1) == W - 1
    for r in range(R):
        ko_ref[l, r] = jnp.where(last, k_cols[:, r:r + 1], pltpu.roll(kc_ref[r], W - 1, axis=1))
        vo_ref[l, r] = jnp.where(last, v_cols[:, r:r + 1], pltpu.roll(vc_ref[r], W - 1, axis=1))


def _swa_sample(l, z, z3, kc_all, vc_all, bias, aux, prev):
    bsz = z.shape[0]
    R = min(SWA_SAMPLE_BLOCK, bsz)
    W = kc_all.shape[3]
    cache_in = pl.BlockSpec((None, R, S_KV_WIDTH, W), lambda i: (l, i, 0, 0))
    cache_prev = pl.BlockSpec((l, R, S_KV_WIDTH, W), lambda i: (0, i, 0, 0))
    cache_out = pl.BlockSpec((l + 1, R, S_KV_WIDTH, W), lambda i: (0, i, 0, 0))
    return pl.pallas_call(
        _swa_sample_kernel,
        grid=(bsz // R,),
        in_specs=[
            pl.BlockSpec((R, 1, S_WIDTH), lambda i: (i, 0, Z_SQ // S_WIDTH)),
            pl.BlockSpec((R, 1, 2 * S_KV_WIDTH), lambda i: (i, 0, Z_SKV // (2 * S_KV_WIDTH))),
            pl.BlockSpec((R, 2 * S_KV_WIDTH), lambda i: (i, Z_SKV // (2 * S_KV_WIDTH))),
            cache_in, cache_in,
            _const_spec((S_Q_HEADS, LANES)), _weight_spec(aux, l),
        ] + [cache_prev] * len(prev),
        out_specs=[pl.BlockSpec((R, 1, S_WIDTH), lambda i: (i, 0, 0)), cache_out, cache_out],
        out_shape=[
            jax.ShapeDtypeStruct((bsz, 1, S_WIDTH), F32),
            jax.ShapeDtypeStruct((l + 1, bsz, S_KV_WIDTH, W), F32),
            jax.ShapeDtypeStruct((l + 1, bsz, S_KV_WIDTH, W), F32),
        ],
        compiler_params=_params("arbitrary"),
        name="swa_sample",
    )(z3, z3, z, kc_all, vc_all, bias, aux, *prev)


def _mix_out_sample_kernel(x_ref, hmt_ref, hc_ref, hs_ref, wout_ref, g2_ref, wq_ref, x1_ref, q_ref):
    cat = jnp.concatenate([hmt_ref[...].T.astype(BF16), hc_ref[...].astype(BF16), hs_ref[...].astype(BF16)], axis=1)
    x1 = x_ref[...] + _dot(cat, wout_ref[...])
    x1_ref[...] = x1
    q_ref[...] = _dot(_rms(x1, g2_ref[...]).astype(BF16), wq_ref[...])


def _mix_out_sample(l, x, hm, hc, hs, wout, g2, wq):
    bsz, d = x.shape
    full = lambda a: pl.BlockSpec(a.shape, lambda i: (0,) * a.ndim)
    args = (x, hm, hc, hs, wout, g2, wq)
    return pl.pallas_call(
        _mix_out_sample_kernel,
        grid=(1,),
        in_specs=[full(x), full(hm), full(hc), full(hs), _weight_spec(wout, l), _weight_spec(g2, l),
                  _weight_spec(wq, l)],
        out_specs=[pl.BlockSpec((bsz, d), lambda i: (0, 0)), pl.BlockSpec((bsz, X_WIDTH), lambda i: (0, 0))],
        out_shape=[jax.ShapeDtypeStruct((bsz, d), F32), jax.ShapeDtypeStruct((bsz, X_WIDTH), F32)],
        compiler_params=_params("arbitrary"),
        name="mix_out_sample",
    )(*args)


def _xattn_sample_kernel(q_ref, k_ref, v_ref, o_ref):
    R = k_ref.shape[0]
    shape = (R, SUBLANES, X_WIDTH)
    row = lax.broadcasted_iota(jnp.int32, shape, 1)
    lane_head = lax.broadcasted_iota(jnp.int32, shape, 2) // HEAD_DIM
    own = row == lane_head
    q8 = jnp.where(own, jnp.broadcast_to(q_ref[...], shape), 0.0).astype(BF16)
    s = jnp.einsum("bqd,bdk->bqk", q8, k_ref[...].astype(BF16), preferred_element_type=F32) * QK_SCALE
    e = jnp.exp(s - jnp.max(s, axis=2, keepdims=True))
    p = (e / jnp.sum(e, axis=2, keepdims=True)).astype(BF16)
    o8 = jnp.einsum("bqk,bdk->bqd", p, v_ref[...].astype(BF16), preferred_element_type=F32)
    o_ref[...] = jnp.sum(jnp.where(own, o8, 0.0), axis=1, keepdims=True)


def _xattn_sample(l, q3, k_all, v_all):
    _, bsz, w, mem = k_all.shape
    R = min(XATTN_SAMPLE_BLOCK, bsz)
    kv = pl.BlockSpec((None, R, w, mem), lambda i: (l, i, 0, 0))
    qo = pl.BlockSpec((R, 1, w), lambda i: (i, 0, 0))
    return pl.pallas_call(
        _xattn_sample_kernel,
        grid=(bsz // R,),
        in_specs=[qo, kv, kv],
        out_specs=qo,
        out_shape=jax.ShapeDtypeStruct((bsz, 1, w), F32),
        compiler_params=_params("arbitrary"),
        name="xattn_sample",
    )(q3, k_all, v_all)


def _ffn_sample_kernel(x1_ref, ox_ref, wo_ref, g3_ref, wup_ref, cw_ref, cb_ref, wdn_ref, gf_ref, hist_ref,
                       *rest, final):
    o_ref, hist_out_ref, act = rest[-3:]
    _copy_earlier_layers(rest[:-3], (hist_out_ref,))
    l = hist_out_ref.shape[0] - 1
    dff = wdn_ref.shape[0]
    x = x1_ref[...] + _dot(ox_ref[...].astype(BF16), wo_ref[...])
    h = _rms(x, g3_ref[...]).astype(BF16)
    for c in range(0, dff, FFN_CHUNK):
        a = _dot(h, wup_ref[:, c:c + FFN_CHUNK])
        g = _dot(h, wup_ref[:, dff + c:dff + c + FFN_CHUNK])
        h1 = hist_ref[:, 1, c:c + FFN_CHUNK]
        gc = (cw_ref[0:1, c:c + FFN_CHUNK] * hist_ref[:, 0, c:c + FFN_CHUNK]
              + cw_ref[1:2, c:c + FFN_CHUNK] * h1
              + cw_ref[2:3, c:c + FFN_CHUNK] * g + cb_ref[:, c:c + FFN_CHUNK])
        act[:, c:c + FFN_CHUNK] = (gc * _sigmoid(gc) * a).astype(BF16)
        hist_out_ref[l, :, 0, c:c + FFN_CHUNK] = h1
        hist_out_ref[l, :, 1, c:c + FFN_CHUNK] = g
    y = x + _dot(act[...], wdn_ref[...])
    o_ref[...] = _rms(y, gf_ref[...]) if final else y


def _ffn_sample(l, x1, ox, wo, g3, wup, cw, cb, wdn, gf, hist_all, prev, *, final):
    bsz, d = x1.shape
    dff = wdn.shape[1]
    full = lambda a: pl.BlockSpec(a.shape, lambda i: (0,) * a.ndim)
    hshape = hist_all.shape[1:]
    return pl.pallas_call(
        functools.partial(_ffn_sample_kernel, final=final),
        grid=(1,),
        in_specs=[full(x1), full(ox), _weight_spec(wo, l), _weight_spec(g3, l), _weight_spec(wup, l),
                  _weight_spec(cw, l), _weight_spec(cb, l),
                  _weight_spec(wdn, l), full(gf), pl.BlockSpec((None,) + hshape, lambda i: (l, 0, 0, 0))]
                 + [full(p) for p in prev],
        out_specs=[pl.BlockSpec((bsz, d), lambda i: (0, 0)), pl.BlockSpec((l + 1,) + hshape, lambda i: (0, 0, 0, 0))],
        out_shape=[jax.ShapeDtypeStruct((bsz, d), F32), jax.ShapeDtypeStruct((l + 1,) + hshape, F32)],
        scratch_shapes=[pltpu.VMEM((bsz, dff), BF16)],
        compiler_params=_params("arbitrary"),
        name="ffn_sample",
    )(x1, ox, wo, g3, wup, cw, cb, wdn, gf, hist_all, *prev)


def _t5_buckets(dist):
    n = np.maximum(dist, 0)
    max_exact = N_BUCKETS // 2
    nf = np.maximum(n, max_exact).astype(np.float32)
    large = max_exact + (np.log(nf / np.float32(max_exact)) / np.float32(math.log(MAX_DISTANCE / max_exact))
                         * np.float32(N_BUCKETS - max_exact)).astype(np.int32)
    return np.where(n < max_exact, n, np.minimum(large, N_BUCKETS - 1))


def _prompt_buckets():
    W = WINDOW
    dist = np.arange(W)[None, :] + W - np.arange(2 * W)[:, None]
    band = (dist >= 0) & (dist < W)
    return np.where(band, _t5_buckets(dist), -1).astype(np.int32)


def _swa_tables(rel_bias):
    W = WINDOW
    dist_c = W - np.arange(W)
    tab = jnp.transpose(rel_bias[_t5_buckets(dist_c)], (1, 0))
    cache_bias = jnp.where((dist_c < W)[None], tab, NEG_INF)
    return cache_bias, rel_bias[0]


def kernel(x_prompt, x_sample, mem_prompt, state_mlstm_C, state_mlstm_n, state_mlstm_m, state_conv, cache_swa_k, cache_swa_v, cache_mem_k, cache_mem_v, state_ffn_conv, rel_bias, norm1_g, w_in, b_i, b_f, mlstm_norm_g, conv_w, conv_b, conv_ln_g, conv_ln_b, swa_sinks, w_out, norm2_g, w_xq, w_xk, w_xv, w_xo, norm3_g, w_up, ffn_conv_w, ffn_conv_b, w_down, final_norm_g):
    depth = w_in.shape[0]
    bp, seq, d = x_prompt.shape
    bs = x_sample.shape[0]
    mem = mem_prompt.shape[1]
    dff = w_down.shape[1]
    W = WINDOW
    nh = C_KERNEL - 1

    xp = x_prompt
    xs = x_sample.reshape(bs, d)
    gf = final_norm_g.reshape(1, d)
    c_all = jnp.transpose(state_mlstm_C, (0, 2, 3, 4, 1))
    n_all = jnp.transpose(state_mlstm_n, (0, 2, 3, 1))
    m_all = jnp.transpose(state_mlstm_m, (0, 2, 1))
    hist_all = jnp.transpose(state_conv, (0, 2, 1, 3))
    kc_all = jnp.transpose(cache_swa_k, (0, 1, 3, 4, 2)).reshape(depth, bs, S_KV_WIDTH, W)
    vc_all = jnp.transpose(cache_swa_v, (0, 1, 3, 4, 2)).reshape(depth, bs, S_KV_WIDTH, W)
    mk_all = jnp.transpose(cache_mem_k, (0, 1, 3, 4, 2)).reshape(depth, bs, X_WIDTH, mem)
    mv_all = jnp.transpose(cache_mem_v, (0, 1, 3, 4, 2)).reshape(depth, bs, X_WIDTH, mem)
    pm_c, pm_n, pm_m, p_conv, p_k, p_v, p_mk, p_mv, p_ffn = ([] for _ in range(9))
    s_mlstm, s_conv, s_kv, s_ffn = [], [], [], []

    win_t = jnp.swapaxes(w_in, 1, 2)
    wout = w_out.astype(BF16)
    wxq = w_xq.astype(BF16)
    wxkv = jnp.concatenate([w_xk, w_xv], axis=2).astype(BF16)
    wxo = w_xo.astype(BF16)
    wup = w_up.astype(BF16)
    wdn = w_down.astype(BF16)
    bucket = jnp.asarray(_prompt_buckets())

    rows = lambda a: a.reshape(depth, 1, -1)
    g1, g2, g3 = rows(norm1_g), rows(norm2_g), rows(norm3_g)
    gb8 = jnp.concatenate([b_i, b_f], axis=1)
    gbias = rows(jnp.concatenate([gb8, jnp.zeros((depth, LANES - 2 * M_HEADS), F32)], axis=1))
    ng = rows(mlstm_norm_g)
    ngt = jnp.broadcast_to(mlstm_norm_g[:, :, None], (depth, M_WIDTH, bs))
    cw = jnp.concatenate([conv_w, jnp.zeros((depth, CONV_PAD - C_KERNEL, C_WIDTH), F32)], axis=1)
    cb, lg, lb = rows(conv_b), rows(conv_ln_g), rows(conv_ln_b)
    fcw = jnp.concatenate([ffn_conv_w, jnp.zeros((depth, SUBLANES - FFN_KERNEL, dff), F32)], axis=1)
    fcb = rows(ffn_conv_b)
    cache_bias, bias0 = _swa_tables(rel_bias)
    aux = jnp.concatenate([jnp.broadcast_to(bias0[None, :, None], (depth, S_Q_HEADS, 1)), swa_sinks[:, :, None],
                           jnp.zeros((depth, S_Q_HEADS, LANES - 2), F32)], axis=2)

    for l in range(depth):
        last = l == depth - 1

        mkv = _mem_kv(l, mem_prompt, wxkv)
        z, win = _in_proj(l, xp.reshape(bp * seq, d), g1, win_t, tm=1024)
        z = z.reshape(bp, seq, Z_WIDTH)
        hm, cpair, npair, mm = _mlstm_prompt(l, z, gbias, ng)
        hc, ctail = _conv_prompt(l, z, cw, cb, lg, lb, tc=min(2048, seq))
        hs, kt, vt = _swa_prompt(l, z, rel_bias, swa_sinks, bucket, qb=min(8, seq // W))
        xp = _mix_out_prompt(l, xp, hm, hc, hs, wout, g2, wxq, mkv, wxo, tm=1024)
        xp, ftail = _ffn_prompt(l, xp, g3, wup, fcw, fcb, wdn, gf, tm=512, final=last)
        half = lambda h: slice(HEAD_DIM * (h % 2), HEAD_DIM * (h % 2 + 1))
        pm_c.append(jnp.stack([jnp.swapaxes(cpair[:, h // 2, half(h), half(h)], 1, 2) for h in range(M_HEADS)], axis=1))
        pm_n.append(jnp.stack([npair[:, h // 2, h % 2, half(h)] for h in range(M_HEADS)], axis=1))
        pm_m.append(mm[:, :, 0, 0])
        p_conv.append(ctail[:, CONV_PAD - nh:, :])
        p_k.append(jnp.transpose(kt.reshape(bp, S_KV_HEADS, HEAD_DIM, W), (0, 3, 1, 2)))
        p_v.append(jnp.transpose(vt.reshape(bp, S_KV_HEADS, HEAD_DIM, W), (0, 3, 1, 2)))
        p_mk.append(jnp.transpose(mkv[:, 0:X_WIDTH, :].reshape(bp, X_HEADS, HEAD_DIM, mem), (0, 3, 1, 2)))
        p_mv.append(jnp.transpose(mkv[:, X_WIDTH:, :].reshape(bp, X_HEADS, HEAD_DIM, mem), (0, 3, 1, 2)))
        p_ffn.append(ftail[:, FFN_PAD - (FFN_KERNEL - 1):, :])

        zs, zst = _sample_in(l, xs, g1, win)
        hmt_s, *s_mlstm = _mlstm_sample(l, zst, gb8, ngt, c_all, n_all, m_all, s_mlstm)
        hc_s, *s_conv = _conv_sample(l, zs, hist_all, cw, cb, lg, lb, s_conv)
        hs_s, *s_kv = _swa_sample(l, zs, zs.reshape(bs, 1, Z_WIDTH), kc_all, vc_all, cache_bias, aux, s_kv)
        x1, qx = _mix_out_sample(l, xs, hmt_s, hc_s, hs_s.reshape(bs, S_WIDTH), wout, g2, wxq)
        ox = _xattn_sample(l, qx.reshape(bs, 1, X_WIDTH), mk_all, mv_all)
        xs, *s_ffn = _ffn_sample(l, x1, ox.reshape(bs, X_WIDTH), wxo, g3, wup, fcw, fcb, wdn, gf,
                                 state_ffn_conv, s_ffn, final=last)

    st = jnp.stack
    tr = jnp.transpose
    s_c, s_n, s_m = s_mlstm
    s_k, s_v = (a.reshape(depth, bs, S_KV_HEADS, HEAD_DIM, W) for a in s_kv)
    return (xp, xs.reshape(bs, 1, d),
            st(pm_c), st(pm_n), st(pm_m), st(p_conv), st(p_k), st(p_v), st(p_mk), st(p_mv), st(p_ffn),
            tr(s_c, (0, 4, 1, 2, 3)), tr(s_n, (0, 3, 1, 2)), tr(s_m, (0, 2, 1)),
            tr(s_conv[0], (0, 2, 1, 3)), tr(s_k, (0, 1, 4, 2, 3)), tr(s_v, (0, 1, 4, 2, 3)),
            s_ffn[0])
```

```python
import functools
import math

import numpy as np
import jax
import jax.numpy as jnp
from jax import lax
from jax.experimental import pallas as pl
from jax.experimental.pallas import tpu as pltpu

F32 = jnp.float32
BF16 = jnp.bfloat16
EPS = 1e-6
NEG_INF = float("-inf")

HEAD_DIM = 64
M_HEADS = 4
M_WIDTH = M_HEADS * HEAD_DIM
C_WIDTH = 256
C_KERNEL = 31
S_Q_HEADS = 8
S_KV_HEADS = 2
S_WIDTH = S_Q_HEADS * HEAD_DIM
S_KV_WIDTH = S_KV_HEADS * HEAD_DIM
WINDOW = 128
N_BUCKETS = 32
MAX_DISTANCE = 128
X_HEADS = 4
X_WIDTH = X_HEADS * HEAD_DIM
FFN_KERNEL = 3
QK_SCALE = HEAD_DIM ** -0.5
LOG2E = math.log2(math.e)

Z_M = 0
Z_C = 4 * M_WIDTH
Z_SQ = Z_C + 2 * C_WIDTH
Z_SKV = Z_SQ + S_WIDTH
Z_G = Z_SKV + 2 * S_KV_WIDTH
LANES = 128
SUBLANES = 8
Z_WIDTH = Z_G + LANES
VMEM_LIMIT = 56 * 1024 * 1024

M_CHUNK = 128
FFN_CHUNK = 256


def _params(*sem):
    return pltpu.CompilerParams(dimension_semantics=sem, vmem_limit_bytes=VMEM_LIMIT)


def _const_spec(shape):
    nd = len(shape)
    return pl.BlockSpec(shape, lambda *_: (0,) * nd, pipeline_mode=pl.Buffered(1))


def _weight_spec(w, l):
    nd = w.ndim - 1
    return pl.BlockSpec((None,) + w.shape[1:], lambda *_: (l,) + (0,) * nd, pipeline_mode=pl.Buffered(1))


def _rms(x, g):
    return x * lax.rsqrt(jnp.mean(x * x, axis=-1, keepdims=True) + EPS) * g


def _sigmoid(x):
    return 1.0 / (1.0 + jnp.exp(-x))


def _log_sigmoid(x):
    return jnp.minimum(x, 0.0) - jnp.log1p(jnp.exp(-jnp.abs(x)))


def _dot(a, b):
    return jnp.dot(a, b, preferred_element_type=F32)


def _dot_nt(a, b):
    return lax.dot_general(a, b, (((1,), (1,)), ((), ())), preferred_element_type=F32)


def _dot_tn(a, b):
    return lax.dot_general(a, b, (((0,), (0,)), ((), ())), preferred_element_type=F32)


IN_PROJ_ROWS = 512
IN_PROJ_COLS = 512
IN_GATES = 4 * M_WIDTH
W_PREP_ROWS = 256


def _in_proj_kernel(x_ref, g_ref, wt_ref, o_ref, wprep_ref, w_scr):
    d = x_ref.shape[1]

    @pl.when(pl.program_id(0) == 0)
    def _():
        n_gate = 2 * M_HEADS
        for src, dst, n in ((0, 0, IN_GATES), (IN_GATES + n_gate, IN_GATES, Z_G - IN_GATES)):
            for c in range(0, n, W_PREP_ROWS):
                w_scr[:, dst + c:dst + c + W_PREP_ROWS] = wt_ref[src + c:src + c + W_PREP_ROWS, :].T.astype(BF16)
        gate_rows = jnp.concatenate([wt_ref[IN_GATES:IN_GATES + n_gate, :], jnp.zeros((LANES - n_gate, d), F32)],
                                    axis=0)
        w_scr[:, Z_G:Z_G + LANES] = gate_rows.T.astype(BF16)
        wprep_ref[...] = w_scr[...]

    tm, n = o_ref.shape
    sub = min(IN_PROJ_ROWS, tm)
    for r0 in range(0, tm, sub):
        h = _rms(x_ref[r0:r0 + sub, :], g_ref[...]).astype(BF16)
        for c in range(0, n, IN_PROJ_COLS):
            w = min(IN_PROJ_COLS, n - c)
            o_ref[r0:r0 + sub, c:c + w] = _dot(h, w_scr[:, c:c + w])


def _in_proj(l, x, g, wt, *, tm):
    m, d = x.shape
    tm = min(tm, m)
    return pl.pallas_call(
        _in_proj_kernel,
        grid=(m // tm,),
        in_specs=[pl.BlockSpec((tm, d), lambda i: (i, 0)), _weight_spec(g, l), _weight_spec(wt, l)],
        out_specs=[pl.BlockSpec((tm, Z_WIDTH), lambda i: (i, 0)), pl.BlockSpec((d, Z_WIDTH), lambda i: (0, 0))],
        out_shape=[jax.ShapeDtypeStruct((m, Z_WIDTH), F32), jax.ShapeDtypeStruct((d, Z_WIDTH), BF16)],
        scratch_shapes=[pltpu.VMEM((d, Z_WIDTH), BF16)],
        compiler_params=_params("arbitrary"),
        name="in_proj",
    )(x, g, wt)


def _mem_kv_kernel(x_ref, w_ref, o_ref, acc):
    acc[...] = _dot(x_ref[...].astype(BF16), w_ref[...])
    o_ref[...] = acc[...].T


def _mem_kv(l, mem3, w):
    b, mem, d = mem3.shape
    n = w.shape[2]
    return pl.pallas_call(
        _mem_kv_kernel,
        grid=(b,),
        in_specs=[pl.BlockSpec((None, mem, d), lambda i: (i, 0, 0)), _weight_spec(w, l)],
        out_specs=pl.BlockSpec((None, n, mem), lambda i: (i, 0, 0)),
        out_shape=jax.ShapeDtypeStruct((b, n, mem), F32),
        scratch_shapes=[pltpu.VMEM((mem, n), F32)],
        compiler_params=_params("arbitrary"),
        name="mem_kv",
    )(mem3, w)


def _sample_in_kernel(x_ref, g_ref, w_ref, z_ref, zt_ref):
    h = _rms(x_ref[...], g_ref[...]).astype(BF16)
    n = z_ref.shape[1]
    for c in range(0, n, IN_PROJ_COLS):
        w = min(IN_PROJ_COLS, n - c)
        zc = _dot(h, w_ref[:, c:c + w])
        z_ref[:, c:c + w] = zc
        zt_ref[c:c + w, :] = zc.T


def _sample_in(l, x, g, w):
    m, d = x.shape
    n = w.shape[1]
    full = lambda shape: pl.BlockSpec(shape, lambda i: (0,) * len(shape))
    return pl.pallas_call(
        _sample_in_kernel,
        grid=(1,),
        in_specs=[full((m, d)), _weight_spec(g, l), full((d, n))],
        out_specs=[full((m, n)), full((n, m))],
        out_shape=[jax.ShapeDtypeStruct((m, n), F32), jax.ShapeDtypeStruct((n, m), F32)],
        compiler_params=_params("arbitrary"),
        name="sample_in",
    )(x, g, w)


def _mlstm_prompt_kernel(z_ref, g_ref, gb_ref, ng_ref, h_ref, cp_ref, np_ref, m_ref,
                         cp_scr, np_scr, m_scr, ht_scr, st_scr, kw_scr):
    NB, L = z_ref.shape[0], z_ref.shape[1]
    D = HEAD_DIM
    hi = lax.Precision.HIGHEST

    @pl.when(pl.program_id(0) == 0)
    def _():
        cp_scr[...] = jnp.zeros(cp_scr.shape, F32)
        np_scr[...] = jnp.zeros(np_scr.shape, F32)
        m_scr[...] = jnp.zeros(m_scr.shape, F32)

    src = lax.broadcasted_iota(jnp.int32, (L, L), 0)
    qry = lax.broadcasted_iota(jnp.int32, (L, L), 1)
    causal_t = src <= qry
    upper = jnp.where(causal_t, 1.0, 0.0)
    lane_half = lax.broadcasted_iota(jnp.int32, (L, LANES), 1) // D
    row8 = lax.broadcasted_iota(jnp.int32, (SUBLANES, LANES), 0)

    rows, cols = {}, []
    for b in range(NB):
        g_t = (g_ref[b] + gb_ref[...]).T[0:SUBLANES, :]
        b_rows = jnp.dot(_log_sigmoid(g_t), upper, precision=hi, preferred_element_type=F32)
        to_cols = []
        for h in range(M_HEADS):
            b_row = b_rows[M_HEADS + h:M_HEADS + h + 1, :]
            ci_row = g_t[h:h + 1, :] - b_row
            m_prev = m_scr[b, h, 0:1, 0:1]
            b_last = b_row[:, L - 1:L]
            m_new = jnp.maximum(b_last + m_prev, jnp.max(b_last + ci_row, axis=1, keepdims=True))
            rows[b, h] = dict(b_row2=b_row * LOG2E, a_row2=(b_row + m_prev) * LOG2E, m_new=m_new,
                              decay=jnp.exp(b_last + m_prev - m_new))
            to_cols += [ci_row * LOG2E, jnp.exp(b_last + ci_row - m_new)]
        to_cols.append(jnp.zeros((LANES - 2 * M_HEADS, L), F32))
        cols.append(jnp.concatenate(to_cols, axis=0).T)

    stats = {}
    for b in range(NB):
        for j in range(M_HEADS // 2):
            slab = LANES * j
            q2 = z_ref[b, :, slab:slab + LANES].astype(BF16)
            k2 = z_ref[b, :, M_WIDTH + slab:M_WIDTH + slab + LANES] * QK_SCALE
            for par in range(2):
                h = 2 * j + par
                r = rows[b, h]
                ci_col = cols[b][:, 2 * h:2 * h + 1]
                a_row = r["a_row2"]
                dm = jnp.where(causal_t, r["b_row2"] + ci_col, NEG_INF)
                m_row = jnp.maximum(a_row, jnp.max(dm, axis=0, keepdims=True))
                k_own = jnp.where(lane_half == par, k2, 0.0).astype(BF16)
                s_t = _dot_nt(k_own, q2) * jnp.exp2(dm - m_row)
                st_scr[b, j, :, L * par:L * (par + 1)] = s_t.astype(BF16)
                stats[b, h] = dict(w_inter=jnp.exp2(a_row - m_row), floor=jnp.exp2(-m_row),
                                   den_s=jnp.sum(s_t, axis=0, keepdims=True),
                                   decay=r["decay"], m_new=r["m_new"])
            wk = jnp.where(lane_half == 0, cols[b][:, 4 * j + 1:4 * j + 2], cols[b][:, 4 * j + 3:4 * j + 4])
            kw = k2 * wk
            kw_scr[b, j] = kw.astype(BF16)
            stats[b, j, "k_sum"] = jnp.sum(kw, axis=0, keepdims=True)

    block_diag = (lax.broadcasted_iota(jnp.int32, (LANES, LANES), 0) // D
                  == lax.broadcasted_iota(jnp.int32, (LANES, LANES), 1) // D)
    for b in range(NB):
        for j in range(M_HEADS // 2):
            slab = LANES * j
            q2 = z_ref[b, :, slab:slab + LANES].astype(BF16)
            v2 = z_ref[b, :, 2 * M_WIDTH + slab:2 * M_WIDTH + slab + LANES].astype(BF16)
            cp = cp_scr[b, j]
            npair = np_scr[b, j]
            qc = _dot_nt(cp.astype(BF16), q2)
            qn = _dot_nt(npair.astype(BF16), q2)
            pv = _dot_tn(v2, st_scr[b, j])
            upd = jnp.where(block_diag, _dot_tn(v2, kw_scr[b, j]), 0.0)
            n_new = jnp.zeros((SUBLANES, LANES), F32)
            for par in range(2):
                h = 2 * j + par
                st = stats[b, h]
                hr = slice(D * par, D * (par + 1))
                num = st["w_inter"] * qc[hr, :] + pv[hr, L * par:L * (par + 1)]
                den = st["w_inter"] * qn[par:par + 1, :] + st["den_s"]
                hh = num * (1.0 / jnp.maximum(jnp.abs(den), st["floor"]))
                hh = hh * lax.rsqrt(jnp.mean(hh * hh, axis=0, keepdims=True) + EPS)
                ht_scr[b, D * h:D * (h + 1), :] = hh
                cp_scr[b, j, hr, :] = st["decay"] * cp[hr, :] + upd[hr, :]
                k_sum = jnp.where(lane_half[0:1, :] == par, stats[b, j, "k_sum"], 0.0)
                n_new = jnp.where(row8 == par, st["decay"] * npair[par:par + 1, :] + k_sum, n_new)
                m_scr[b, h] = jnp.broadcast_to(st["m_new"], m_scr.shape[2:])
            np_scr[b, j] = n_new
        o_gate = _sigmoid(z_ref[b, :, 3 * M_WIDTH:4 * M_WIDTH])
        h_ref[b] = (ht_scr[b].T * ng_ref[...] * o_gate).astype(h_ref.dtype)

    cp_ref[...] = cp_scr[...]
    np_ref[...] = np_scr[...]
    m_ref[...] = m_scr[...]


def _mlstm_prompt(l, z3, gbias, ng):
    b, s, _ = z3.shape
    L = M_CHUNK
    P = M_HEADS // 2
    return pl.pallas_call(
        _mlstm_prompt_kernel,
        grid=(s // L,),
        in_specs=[
            pl.BlockSpec((b, L, 4 * M_WIDTH), lambda c: (0, c, Z_M // (4 * M_WIDTH))),
            pl.BlockSpec((b, L, LANES), lambda c: (0, c, Z_G // LANES)),
            _weight_spec(gbias, l),
            _weight_spec(ng, l),
        ],
        out_specs=[
            pl.BlockSpec((b, L, M_WIDTH), lambda c: (0, c, 0)),
            pl.BlockSpec((b, P, LANES, LANES), lambda c: (0, 0, 0, 0)),
            pl.BlockSpec((b, P, SUBLANES, LANES), lambda c: (0, 0, 0, 0)),
            pl.BlockSpec((b, M_HEADS, SUBLANES, LANES), lambda c: (0, 0, 0, 0)),
        ],
        out_shape=[
            jax.ShapeDtypeStruct((b, s, M_WIDTH), BF16),
            jax.ShapeDtypeStruct((b, P, LANES, LANES), F32),
            jax.ShapeDtypeStruct((b, P, SUBLANES, LANES), F32),
            jax.ShapeDtypeStruct((b, M_HEADS, SUBLANES, LANES), F32),
        ],
        scratch_shapes=[
            pltpu.VMEM((b, P, LANES, LANES), F32),
            pltpu.VMEM((b, P, SUBLANES, LANES), F32),
            pltpu.VMEM((b, M_HEADS, SUBLANES, LANES), F32),
            pltpu.VMEM((b, M_WIDTH, L), F32),
            pltpu.VMEM((b, P, L, 2 * L), BF16),
            pltpu.VMEM((b, P, L, LANES), BF16),
        ],
        compiler_params=_params("arbitrary"),
        name="mlstm_prompt",
    )(z3, z3, gbias, ng)


CONV_PAD = 32


def _conv_prompt_kernel(z_ref, w_ref, cb_ref, lg_ref, lb_ref, h_ref, tail_ref, buf, shifted):
    tc = z_ref.shape[0]
    t = pl.program_id(1)

    @pl.when(t == 0)
    def _():
        buf[0:CONV_PAD, :] = jnp.zeros((CONV_PAD, C_WIDTH), F32)

    @pl.when(t > 0)
    def _():
        buf[0:CONV_PAD, :] = buf[tc:tc + CONV_PAD, :]

    u = z_ref[:, 0:C_WIDTH] * _sigmoid(z_ref[:, C_WIDTH:2 * C_WIDTH])
    buf[CONV_PAD:CONV_PAD + tc, :] = u
    n_sh = shifted.shape[1]
    for r in range(1, SUBLANES):
        shifted[r - 1] = buf[r:r + n_sh, :]
    off = CONV_PAD - (C_KERNEL - 1)
    acc = jnp.broadcast_to(cb_ref[...], (tc, C_WIDTH))
    for j in range(C_KERNEL):
        r = (off + j) % SUBLANES
        base = off + j - r
        win = buf[base:base + tc, :] if r == 0 else shifted[r - 1, base:base + tc, :]
        acc = acc + w_ref[j:j + 1, :] * win
    mu = jnp.mean(acc, axis=-1, keepdims=True)
    xc = acc - mu
    y = xc * lax.rsqrt(jnp.mean(xc * xc, axis=-1, keepdims=True) + EPS) * lg_ref[...] + lb_ref[...]
    h_ref[...] = (y * _sigmoid(y)).astype(h_ref.dtype)
    tail_ref[...] = buf[tc:tc + CONV_PAD, :]


def _conv_prompt(l, z3, w, cb, lg, lb, *, tc):
    b, s, _ = z3.shape
    return pl.pallas_call(
        _conv_prompt_kernel,
        grid=(b, s // tc),
        in_specs=[
            pl.BlockSpec((None, tc, 2 * C_WIDTH), lambda i, t: (i, t, Z_C // (2 * C_WIDTH))),
            _weight_spec(w, l), _weight_spec(cb, l), _weight_spec(lg, l), _weight_spec(lb, l),
        ],
        out_specs=[
            pl.BlockSpec((None, tc, C_WIDTH), lambda i, t: (i, t, 0)),
            pl.BlockSpec((None, CONV_PAD, C_WIDTH), lambda i, t: (i, 0, 0)),
        ],
        out_shape=[
            jax.ShapeDtypeStruct((b, s, C_WIDTH), BF16),
            jax.ShapeDtypeStruct((b, CONV_PAD, C_WIDTH), F32),
        ],
        scratch_shapes=[pltpu.VMEM((CONV_PAD + tc, C_WIDTH), F32),
                        pltpu.VMEM((SUBLANES - 1, CONV_PAD + tc - SUBLANES, C_WIDTH), F32)],
        compiler_params=_params("arbitrary", "arbitrary"),
        name="conv_prompt",
    )(z3, w, cb, lg, lb)


def _swa_prompt_kernel(rb_ref, sink_ref, bucket_ref, q_ref, kv_ref, kvp_ref, o_ref, kt_ref, vt_ref,
                       bias_scr, s_scr, p_scr, ot_scr, *, layer):
    W = WINDOW
    H = S_Q_HEADS
    G = H // S_KV_HEADS
    i = pl.program_id(0)
    n = pl.program_id(1)

    @pl.when(jnp.logical_and(i == 0, n == 0))
    def _():
        bucket = bucket_ref[...]
        prev_key = lax.broadcasted_iota(jnp.int32, (2 * W, W), 0) < W
        for h in range(H):
            acc = jnp.full((2 * W, W), NEG_INF, F32)
            for b in range(N_BUCKETS):
                acc = jnp.where(bucket == b, rb_ref[b, h], acc)
            acc = acc * LOG2E
            bias_scr[0, h] = acc
            bias_scr[1, h] = jnp.where(prev_key, NEG_INF, acc)

    QB = q_ref.shape[0] // W
    NK = (QB + 1) * W
    first = jnp.where(n == 0, 1, 0)
    kk = jnp.concatenate([kvp_ref[:, 0:S_KV_WIDTH], kv_ref[:, 0:S_KV_WIDTH]], axis=0) * (QK_SCALE * LOG2E)
    vv = jnp.concatenate([kvp_ref[:, S_KV_WIDTH:2 * S_KV_WIDTH], kv_ref[:, S_KV_WIDTH:2 * S_KV_WIDTH]], axis=0)
    kk_r = pltpu.roll(kk, HEAD_DIM, axis=1)
    lo_lane = lax.broadcasted_iota(jnp.int32, (NK, S_KV_WIDTH), 1) < HEAD_DIM
    k_var = [[jnp.where(lo_lane, kk, 0.0).astype(BF16), jnp.where(lo_lane, 0.0, kk_r).astype(BF16)],
             [jnp.where(lo_lane, kk_r, 0.0).astype(BF16), jnp.where(lo_lane, 0.0, kk).astype(BF16)]]
    v_t = vv.T.astype(BF16)

    for j in range(QB):
        k0 = j * W
        masked = first if j == 0 else 0
        m_rows = [None] * H
        for hk in range(S_KV_HEADS):
            c0 = 2 * LANES * hk
            q_st = jnp.concatenate([q_ref[k0:k0 + W, c0:c0 + LANES], q_ref[k0:k0 + W, c0 + LANES:c0 + 2 * LANES]],
                                   axis=0).astype(BF16)
            for half in range(2):
                s_t = _dot_nt(k_var[hk][half][k0:k0 + 2 * W, :], q_st)
                for slab in range(2):
                    head = G * hk + 2 * slab + half
                    sb = s_t[:, W * slab:W * (slab + 1)] + bias_scr[masked, head]
                    s_scr[j, head] = sb
                    m_rows[head] = jnp.maximum(jnp.max(sb, axis=0, keepdims=True), sink_ref[layer, head] * LOG2E)

        inv = [None] * H
        for head in range(H):
            e = jnp.exp2(s_scr[j, head] - m_rows[head])
            den = jnp.sum(e, axis=0, keepdims=True) + jnp.exp2(sink_ref[layer, head] * LOG2E - m_rows[head])
            inv[head] = 1.0 / den
            p_scr[j, :, W * head:W * (head + 1)] = e.astype(BF16)

        for hk in range(S_KV_HEADS):
            o_t = _dot(v_t[HEAD_DIM * hk:HEAD_DIM * (hk + 1), k0:k0 + 2 * W],
                       p_scr[j, :, W * G * hk:W * G * (hk + 1)])
            for g in range(G):
                head = G * hk + g
                ot_scr[j, HEAD_DIM * head:HEAD_DIM * (head + 1), :] = o_t[:, W * g:W * (g + 1)] * inv[head]
        o_ref[k0:k0 + W, :] = ot_scr[j].T.astype(o_ref.dtype)

    @pl.when(n == pl.num_programs(1) - 1)
    def _():
        kt_ref[...] = kv_ref[(QB - 1) * W:QB * W, 0:S_KV_WIDTH].T
        vt_ref[...] = kv_ref[(QB - 1) * W:QB * W, S_KV_WIDTH:2 * S_KV_WIDTH].T


def _swa_prompt(l, z3, rel_bias, sinks, bucket, *, qb):
    b, s, _ = z3.shape
    W = WINDOW
    smem = pl.BlockSpec(memory_space=pltpu.SMEM)
    return pl.pallas_call(
        functools.partial(_swa_prompt_kernel, layer=l),
        grid=(b, s // (qb * W)),
        in_specs=[
            smem, smem, _const_spec((2 * W, W)),
            pl.BlockSpec((None, qb * W, S_WIDTH), lambda i, n: (i, n, Z_SQ // S_WIDTH)),
            pl.BlockSpec((None, qb * W, 2 * S_KV_WIDTH), lambda i, n: (i, n, Z_SKV // (2 * S_KV_WIDTH))),
            pl.BlockSpec((None, W, 2 * S_KV_WIDTH),
                         lambda i, n: (i, jnp.maximum(n * qb - 1, 0), Z_SKV // (2 * S_KV_WIDTH))),
        ],
        out_specs=[
            pl.BlockSpec((None, qb * W, S_WIDTH), lambda i, n: (i, n, 0)),
            pl.BlockSpec((None, S_KV_WIDTH, W), lambda i, n: (i, 0, 0)),
            pl.BlockSpec((None, S_KV_WIDTH, W), lambda i, n: (i, 0, 0)),
        ],
        out_shape=[
            jax.ShapeDtypeStruct((b, s, S_WIDTH), BF16),
            jax.ShapeDtypeStruct((b, S_KV_WIDTH, W), F32),
            jax.ShapeDtypeStruct((b, S_KV_WIDTH, W), F32),
        ],
        scratch_shapes=[
            pltpu.VMEM((2, S_Q_HEADS, 2 * W, W), F32),
            pltpu.VMEM((qb, S_Q_HEADS, 2 * W, W), F32),
            pltpu.VMEM((qb, 2 * W, S_Q_HEADS * W), BF16),
            pltpu.VMEM((qb, S_WIDTH, W), F32),
        ],
        compiler_params=_params("arbitrary", "arbitrary"),
        name="swa_prompt",
    )(rel_bias, sinks, bucket, z3, z3, z3)


MIX_SUB = 512


def _mix_out_prompt_kernel(x_ref, hm_ref, hc_ref, hs_ref, wout_ref, g2_ref, wq_ref, mk_ref, mv_ref, wo_ref, o_ref,
                           x1_scr, qx_scr, p_scr):
    tm = x_ref.shape[0]
    mem = mk_ref.shape[1]
    mk_t = mk_ref[...]
    mv_t = mv_ref[...]
    row_head = lax.broadcasted_iota(jnp.int32, mk_t.shape, 0) // HEAD_DIM
    k_heads = [jnp.where(row_head == h, mk_t, 0.0).astype(BF16) for h in range(X_HEADS)]
    v_cat = jnp.concatenate([jnp.where(row_head == h, mv_t, 0.0).astype(BF16) for h in range(X_HEADS)], axis=1)
    subs = [slice(r0, r0 + MIX_SUB) for r0 in range(0, tm, MIX_SUB)]
    for rows in subs:
        cat = jnp.concatenate([hm_ref[rows, :], hc_ref[rows, :], hs_ref[rows, :]], axis=1)
        x1 = x_ref[rows, :] + _dot(cat, wout_ref[...])
        x1_scr[rows, :] = x1
        qx = _dot(_rms(x1, g2_ref[...]).astype(BF16), wq_ref[...])
        qx_scr[rows, :] = (qx * (QK_SCALE * LOG2E)).astype(BF16)
    for rows in subs:
        qx = qx_scr[rows, :]
        for h in range(X_HEADS):
            s = _dot(qx, k_heads[h])
            e = jnp.exp2(s - jnp.max(s, axis=1, keepdims=True))
            p_scr[rows, mem * h:mem * (h + 1)] = (e * (1.0 / jnp.sum(e, axis=1, keepdims=True))).astype(BF16)
    for rows in subs:
        o = _dot_nt(p_scr[rows, :], v_cat)
        o_ref[rows, :] = x1_scr[rows, :] + _dot(o.astype(BF16), wo_ref[...])


def _mix_out_prompt(l, x3, hm, hc, hs, wout, g2, wq, mkv, wo, *, tm):
    b, s, d = x3.shape
    tm = min(tm, s)
    mem = mkv.shape[2]
    row = lambda w: pl.BlockSpec((None, tm, w), lambda i, t: (i, t, 0))
    return pl.pallas_call(
        _mix_out_prompt_kernel,
        grid=(b, s // tm),
        in_specs=[
            row(d), row(M_WIDTH), row(C_WIDTH), row(S_WIDTH),
            _weight_spec(wout, l), _weight_spec(g2, l), _weight_spec(wq, l),
            pl.BlockSpec((None, X_WIDTH, mem), lambda i, t: (i, 0, 0)),
            pl.BlockSpec((None, X_WIDTH, mem), lambda i, t: (i, 1, 0)),
            _weight_spec(wo, l),
        ],
        out_specs=row(d),
        out_shape=jax.ShapeDtypeStruct((b, s, d), F32),
        scratch_shapes=[pltpu.VMEM((tm, d), F32), pltpu.VMEM((tm, X_WIDTH), BF16),
                        pltpu.VMEM((tm, X_HEADS * mem), BF16)],
        compiler_params=_params("arbitrary", "arbitrary"),
        name="mix_out_prompt",
    )(x3, hm, hc, hs, wout, g2, wq, mkv, mkv, wo)


FFN_PAD = 8


def _ffn_prompt_kernel(x_ref, g3_ref, wup_ref, cw_ref, cb_ref, wdn_ref, gf_ref, o_ref, tail_ref,
                       gbuf, carry, act, *, final):
    tm = x_ref.shape[0]
    dff = wdn_ref.shape[0]

    @pl.when(pl.program_id(1) == 0)
    def _():
        carry[...] = jnp.zeros(carry.shape, F32)

    x = x_ref[...]
    h = _rms(x, g3_ref[...]).astype(BF16)
    w = FFN_CHUNK
    for c in range(0, dff, w):
        a = _dot(h, wup_ref[:, c:c + w])
        g = _dot(h, wup_ref[:, dff + c:dff + c + w])
        gbuf[0:FFN_PAD, 0:w] = carry[:, c:c + w]
        gbuf[FFN_PAD:FFN_PAD + tm, 0:w] = g
        carry[:, c:c + w] = g[tm - FFN_PAD:tm, :]
        gc = (cw_ref[0:1, c:c + w] * gbuf[FFN_PAD - 2:FFN_PAD - 2 + tm, 0:w]
              + cw_ref[1:2, c:c + w] * gbuf[FFN_PAD - 1:FFN_PAD - 1 + tm, 0:w]
              + cw_ref[2:3, c:c + w] * g + cb_ref[:, c:c + w])
        act[:, c:c + w] = (gc * _sigmoid(gc) * a).astype(BF16)
    y = x + _dot(act[...], wdn_ref[...])
    o_ref[...] = _rms(y, gf_ref[...]) if final else y
    tail_ref[...] = carry[...]


def _ffn_prompt(l, x3, g3, wup, cw, cb, wdn, gf, *, tm, final):
    b, s, d = x3.shape
    tm = min(tm, s)
    dff = wdn.shape[1]
    return pl.pallas_call(
        functools.partial(_ffn_prompt_kernel, final=final),
        grid=(b, s // tm),
        in_specs=[
            pl.BlockSpec((None, tm, d), lambda i, t: (i, t, 0)),
            _weight_spec(g3, l), _weight_spec(wup, l), _weight_spec(cw, l), _weight_spec(cb, l),
            _weight_spec(wdn, l), _const_spec((1, d)),
        ],
        out_specs=[
            pl.BlockSpec((None, tm, d), lambda i, t: (i, t, 0)),
            pl.BlockSpec((None, FFN_PAD, dff), lambda i, t: (i, 0, 0)),
        ],
        out_shape=[jax.ShapeDtypeStruct((b, s, d), F32), jax.ShapeDtypeStruct((b, FFN_PAD, dff), F32)],
        scratch_shapes=[
            pltpu.VMEM((FFN_PAD + tm, FFN_CHUNK), F32),
            pltpu.VMEM((FFN_PAD, dff), F32),
            pltpu.VMEM((tm, dff), BF16),
        ],
        compiler_params=_params("arbitrary", "arbitrary"),
        name="ffn_prompt",
    )(x3, g3, wup, cw, cb, wdn, gf)


SWA_SAMPLE_BLOCK = 32
XATTN_SAMPLE_BLOCK = 16
CONV_SAMPLE_BLOCK = 32


def _mlstm_sample_kernel(gb_ref, q_ref, k_ref, v_ref, o_ref, g_ref, ng_ref, c_ref, n_ref, m_ref, *rest, layer):
    h_ref, co_ref, no_ref, mo_ref, kw_scr = rest[-5:]
    _copy_earlier_layers(rest[:-5], (co_ref, no_ref, mo_ref))
    h = pl.program_id(0)
    i_pre = g_ref[pl.ds(h, 1), :] + gb_ref[layer, h]
    f_pre = g_ref[pl.ds(M_HEADS + h, 1), :] + gb_ref[layer, M_HEADS + h]
    a = _log_sigmoid(f_pre) + m_ref[pl.ds(h, 1), :]
    m_t = jnp.maximum(a, i_pre)
    w_old = jnp.exp(a - m_t)
    w_new = jnp.exp(i_pre - m_t)
    q = q_ref[...]
    k = k_ref[...] * QK_SCALE
    v = v_ref[...]
    n_old = n_ref[...]
    kw_scr[...] = k * w_new

    def body(d, acc):
        c_old = c_ref[d]
        co_ref[layer, d] = w_old * c_old + kw_scr[pl.ds(d, 1), :] * v
        return acc + q_ref[pl.ds(d, 1), :] * c_old

    qc = lax.fori_loop(0, HEAD_DIM, body, jnp.zeros(v.shape, F32), unroll=8)
    s = jnp.sum(q * k, axis=0, keepdims=True) * w_new
    num = w_old * qc + s * v
    den = w_old * jnp.sum(q * n_old, axis=0, keepdims=True) + s
    hh = num / jnp.maximum(jnp.abs(den), jnp.exp(-m_t))
    hh = hh * lax.rsqrt(jnp.mean(hh * hh, axis=0, keepdims=True) + EPS) * ng_ref[...]
    h_ref[...] = hh * _sigmoid(o_ref[...])
    no_ref[layer] = w_old * n_old + kw_scr[...]
    mo_ref[layer, pl.ds(h, 1), :] = m_t


def _mlstm_sample(l, zt, gb, ngt, c_all, n_all, m_all, prev):
    bsz = zt.shape[1]
    D = HEAD_DIM
    feat = lambda off: pl.BlockSpec((D, bsz), lambda h: (off // D + h, 0))
    c_spec = lambda n: pl.BlockSpec((n, None, D, D, bsz), lambda h: (0, h, 0, 0, 0))
    n_spec = lambda n: pl.BlockSpec((n, None, D, bsz), lambda h: (0, h, 0, 0))
    m_spec = lambda n: pl.BlockSpec((n, M_HEADS, bsz), lambda h: (0, 0, 0))
    prev_specs = [c_spec(l), n_spec(l), m_spec(l)] if prev else []
    return pl.pallas_call(
        functools.partial(_mlstm_sample_kernel, layer=l),
        grid=(M_HEADS,),
        in_specs=[
            pl.BlockSpec(memory_space=pltpu.SMEM),
            feat(Z_M), feat(Z_M + M_WIDTH), feat(Z_M + 2 * M_WIDTH), feat(Z_M + 3 * M_WIDTH),
            pl.BlockSpec((SUBLANES, bsz), lambda h: (Z_G // SUBLANES, 0)),
            pl.BlockSpec((None, D, bsz), lambda h: (l, h, 0)),
            pl.BlockSpec((None, None, D, D, bsz), lambda h: (l, h, 0, 0, 0)),
            pl.BlockSpec((None, None, D, bsz), lambda h: (l, h, 0, 0)),
            pl.BlockSpec((None, M_HEADS, bsz), lambda h: (l, 0, 0)),
        ] + prev_specs,
        out_specs=[pl.BlockSpec((D, bsz), lambda h: (h, 0)), c_spec(l + 1), n_spec(l + 1), m_spec(l + 1)],
        out_shape=[
            jax.ShapeDtypeStruct((M_WIDTH, bsz), F32),
            jax.ShapeDtypeStruct((l + 1, M_HEADS, D, D, bsz), F32),
            jax.ShapeDtypeStruct((l + 1, M_HEADS, D, bsz), F32),
            jax.ShapeDtypeStruct((l + 1, M_HEADS, bsz), F32),
        ],
        scratch_shapes=[pltpu.VMEM((D, bsz), F32)],
        compiler_params=_params("arbitrary"),
        name="mlstm_sample",
    )(gb, zt, zt, zt, zt, zt, ngt, c_all, n_all, m_all, *prev)


def _conv_sample_kernel(z_ref, hist_ref, w_ref, cb_ref, lg_ref, lb_ref, *rest):
    h_ref, hist_out_ref = rest[-2:]
    _copy_earlier_layers(rest[:-2], (hist_out_ref,))
    l = hist_out_ref.shape[0] - 1
    nh = C_KERNEL - 1
    u = z_ref[:, 0:C_WIDTH] * _sigmoid(z_ref[:, C_WIDTH:2 * C_WIDTH])
    acc = cb_ref[...] + w_ref[nh:nh + 1, :] * u
    for j in range(nh):
        acc = acc + w_ref[j:j + 1, :] * hist_ref[j]
    mu = jnp.mean(acc, axis=-1, keepdims=True)
    xc = acc - mu
    y = xc * lax.rsqrt(jnp.mean(xc * xc, axis=-1, keepdims=True) + EPS) * lg_ref[...] + lb_ref[...]
    h_ref[...] = y * _sigmoid(y)
    for j in range(nh - 1):
        hist_out_ref[l, j] = hist_ref[j + 1]
    hist_out_ref[l, nh - 1] = u


def _conv_sample(l, z, hist_all, w, cb, lg, lb, prev):
    bsz = z.shape[0]
    nh = hist_all.shape[1]
    R = min(CONV_SAMPLE_BLOCK, bsz)
    hist_spec = lambda n: pl.BlockSpec((n, nh, R, C_WIDTH), lambda i: (0, 0, i, 0))
    return pl.pallas_call(
        _conv_sample_kernel,
        grid=(bsz // R,),
        in_specs=[
            pl.BlockSpec((R, 2 * C_WIDTH), lambda i: (i, Z_C // (2 * C_WIDTH))),
            pl.BlockSpec((None, nh, R, C_WIDTH), lambda i: (l, 0, i, 0)),
            _weight_spec(w, l), _weight_spec(cb, l), _weight_spec(lg, l), _weight_spec(lb, l),
        ] + [hist_spec(l)] * len(prev),
        out_specs=[pl.BlockSpec((R, C_WIDTH), lambda i: (i, 0)), hist_spec(l + 1)],
        out_shape=[jax.ShapeDtypeStruct((bsz, C_WIDTH), F32), jax.ShapeDtypeStruct((l + 1, nh, bsz, C_WIDTH), F32)],
        compiler_params=_params("arbitrary"),
        name="conv_sample",
    )(z, hist_all, w, cb, lg, lb, *prev)


def _copy_earlier_layers(prev_refs, out_refs):
    for p_ref, o_ref in zip(prev_refs, out_refs):
        o_ref[0:p_ref.shape[0]] = p_ref[...]


def _swa_sample_kernel(q_ref, kvn_ref, kv2_ref, kc_ref, vc_ref, bias_ref, aux_ref, *rest):
    o_ref, ko_ref, vo_ref = rest[-3:]
    _copy_earlier_layers(rest[:-3], (ko_ref, vo_ref))
    l = ko_ref.shape[0] - 1
    R = q_ref.shape[0]
    W = kc_ref.shape[2]
    H = S_Q_HEADS
    shape = (R, H, LANES)
    row = lax.broadcasted_iota(jnp.int32, shape, 1)
    lane_half = lax.broadcasted_iota(jnp.int32, shape, 2) // HEAD_DIM
    q_half = row % 2
    kv_head = row // (H // S_KV_HEADS)
    qs = jnp.zeros(shape, F32)
    for j in range(H // 2):
        qs = jnp.where(row // 2 == j, q_ref[:, :, LANES * j:LANES * (j + 1)], qs)
    q8 = jnp.where(lane_half == kv_head, jnp.where(q_half == kv_head, qs, pltpu.roll(qs, HEAD_DIM, axis=2)), 0.0)
    k_new = kvn_ref[:, :, 0:S_KV_WIDTH]
    v_new = kvn_ref[:, :, S_KV_WIDTH:2 * S_KV_WIDTH]
    s = jnp.einsum("bqd,bdk->bqk", q8.astype(BF16), kc_ref[...].astype(BF16), preferred_element_type=F32) * QK_SCALE
    s = s + bias_ref[...][None]
    s_new = jnp.sum(q8 * k_new, axis=2, keepdims=True) * QK_SCALE + aux_ref[:, 0:1][None]
    sink = aux_ref[:, 1:2][None]
    m = jnp.maximum(jnp.maximum(jnp.max(s, axis=2, keepdims=True), s_new), sink)
    e = jnp.exp(s - m)
    e_new = jnp.exp(s_new - m)
    inv = 1.0 / (jnp.sum(e, axis=2, keepdims=True) + e_new + jnp.exp(sink - m))
    o8 = jnp.einsum("bqk,bdk->bqd", e.astype(BF16), vc_ref[...].astype(BF16), preferred_element_type=F32)
    o8 = (o8 + e_new * v_new) * inv
    o8 = jnp.where(lane_half == q_half, jnp.where(q_half == kv_head, o8, pltpu.roll(o8, HEAD_DIM, axis=2)), 0.0)
    for j in range(H // 2):
        o_ref[:, :, LANES * j:LANES * (j + 1)] = jnp.sum(jnp.where(row // 2 == j, o8, 0.0), axis=1, keepdims=True)
    k_cols = kv2_ref[:, 0:S_KV_WIDTH].T
    v_cols = kv2_ref[:, S_KV_WIDTH:2 * S_KV_WIDTH].T
    last = lax.broadcasted_iota(jnp.int32, (S_KV_WIDTH, W), 1) == W - 1
    for r in range(R):
        ko_ref[l, r] = jnp.where(last, k_cols[:, r:r + 1], pltpu.roll(kc_ref[r], W - 1, axis=1))
        vo_ref[l, r] = jnp.where(last, v_cols[:, r:r + 1], pltpu.roll(vc_ref[r], W - 1, axis=1))


def _swa_sample(l, z, z3, kc_all, vc_all, bias, aux, prev):
    bsz = z.shape[0]
    R = min(SWA_SAMPLE_BLOCK, bsz)
    W = kc_all.shape[3]
    cache_in = pl.BlockSpec((None, R, S_KV_WIDTH, W), lambda i: (l, i, 0, 0))
    cache_prev = pl.BlockSpec((l, R, S_KV_WIDTH, W), lambda i: (0, i, 0, 0))
    cache_out = pl.BlockSpec((l + 1, R, S_KV_WIDTH, W), lambda i: (0, i, 0, 0))
    return pl.pallas_call(
        _swa_sample_kernel,
        grid=(bsz // R,),
        in_specs=[
            pl.BlockSpec((R, 1, S_WIDTH), lambda i: (i, 0, Z_SQ // S_WIDTH)),
            pl.BlockSpec((R, 1, 2 * S_KV_WIDTH), lambda i: (i, 0, Z_SKV // (2 * S_KV_WIDTH))),
            pl.BlockSpec((R, 2 * S_KV_WIDTH), lambda i: (i, Z_SKV // (2 * S_KV_WIDTH))),
            cache_in, cache_in,
            _const_spec((S_Q_HEADS, LANES)), _weight_spec(aux, l),
        ] + [cache_prev] * len(prev),
        out_specs=[pl.BlockSpec((R, 1, S_WIDTH), lambda i: (i, 0, 0)), cache_out, cache_out],
        out_shape=[
            jax.ShapeDtypeStruct((bsz, 1, S_WIDTH), F32),
            jax.ShapeDtypeStruct((l + 1, bsz, S_KV_WIDTH, W), F32),
            jax.ShapeDtypeStruct((l + 1, bsz, S_KV_WIDTH, W), F32),
        ],
        compiler_params=_params("arbitrary"),
        name="swa_sample",
    )(z3, z3, z, kc_all, vc_all, bias, aux, *prev)


def _mix_out_sample_kernel(x_ref, hmt_ref, hc_ref, hs_ref, wout_ref, g2_ref, wq_ref, x1_ref, q_ref):
    cat = jnp.concatenate([hmt_ref[...].T.astype(BF16), hc_ref[...].astype(BF16), hs_ref[...].astype(BF16)], axis=1)
    x1 = x_ref[...] + _dot(cat, wout_ref[...])
    x1_ref[...] = x1
    q_ref[...] = _dot(_rms(x1, g2_ref[...]).astype(BF16), wq_ref[...])


def _mix_out_sample(l, x, hm, hc, hs, wout, g2, wq):
    bsz, d = x.shape
    full = lambda a: pl.BlockSpec(a.shape, lambda i: (0,) * a.ndim)
    args = (x, hm, hc, hs, wout, g2, wq)
    return pl.pallas_call(
        _mix_out_sample_kernel,
        grid=(1,),
        in_specs=[full(x), full(hm), full(hc), full(hs), _weight_spec(wout, l), _weight_spec(g2, l),
                  _weight_spec(wq, l)],
        out_specs=[pl.BlockSpec((bsz, d), lambda i: (0, 0)), pl.BlockSpec((bsz, X_WIDTH), lambda i: (0, 0))],
        out_shape=[jax.ShapeDtypeStruct((bsz, d), F32), jax.ShapeDtypeStruct((bsz, X_WIDTH), F32)],
        compiler_params=_params("arbitrary"),
        name="mix_out_sample",
    )(*args)


def _xattn_sample_kernel(q_ref, k_ref, v_ref, o_ref):
    R = k_ref.shape[0]
    shape = (R, SUBLANES, X_WIDTH)
    row = lax.broadcasted_iota(jnp.int32, shape, 1)
    lane_head = lax.broadcasted_iota(jnp.int32, shape, 2) // HEAD_DIM
    own = row == lane_head
    q8 = jnp.where(own, jnp.broadcast_to(q_ref[...], shape), 0.0).astype(BF16)
    s = jnp.einsum("bqd,bdk->bqk", q8, k_ref[...].astype(BF16), preferred_element_type=F32) * QK_SCALE
    e = jnp.exp(s - jnp.max(s, axis=2, keepdims=True))
    p = (e / jnp.sum(e, axis=2, keepdims=True)).astype(BF16)
    o8 = jnp.einsum("bqk,bdk->bqd", p, v_ref[...].astype(BF16), preferred_element_type=F32)
    o_ref[...] = jnp.sum(jnp.where(own, o8, 0.0), axis=1, keepdims=True)


def _xattn_sample(l, q3, k_all, v_all):
    _, bsz, w, mem = k_all.shape
    R = min(XATTN_SAMPLE_BLOCK, bsz)
    kv = pl.BlockSpec((None, R, w, mem), lambda i: (l, i, 0, 0))
    qo = pl.BlockSpec((R, 1, w), lambda i: (i, 0, 0))
    return pl.pallas_call(
        _xattn_sample_kernel,
        grid=(bsz // R,),
        in_specs=[qo, kv, kv],
        out_specs=qo,
        out_shape=jax.ShapeDtypeStruct((bsz, 1, w), F32),
        compiler_params=_params("arbitrary"),
        name="xattn_sample",
    )(q3, k_all, v_all)


def _ffn_sample_kernel(x1_ref, ox_ref, wo_ref, g3_ref, wup_ref, cw_ref, cb_ref, wdn_ref, gf_ref, hist_ref,
                       *rest, final):
    o_ref, hist_out_ref, act = rest[-3:]
    _copy_earlier_layers(rest[:-3], (hist_out_ref,))
    l = hist_out_ref.shape[0] - 1
    dff = wdn_ref.shape[0]
    x = x1_ref[...] + _dot(ox_ref[...].astype(BF16), wo_ref[...])
    h = _rms(x, g3_ref[...]).astype(BF16)
    for c in range(0, dff, FFN_CHUNK):
        a = _dot(h, wup_ref[:, c:c + FFN_CHUNK])
        g = _dot(h, wup_ref[:, dff + c:dff + c + FFN_CHUNK])
        h1 = hist_ref[:, 1, c:c + FFN_CHUNK]
        gc = (cw_ref[0:1, c:c + FFN_CHUNK] * hist_ref[:, 0, c:c + FFN_CHUNK]
              + cw_ref[1:2, c:c + FFN_CHUNK] * h1
              + cw_ref[2:3, c:c + FFN_CHUNK] * g + cb_ref[:, c:c + FFN_CHUNK])
        act[:, c:c + FFN_CHUNK] = (gc * _sigmoid(gc) * a).astype(BF16)
        hist_out_ref[l, :, 0, c:c + FFN_CHUNK] = h1
        hist_out_ref[l, :, 1, c:c + FFN_CHUNK] = g
    y = x + _dot(act[...], wdn_ref[...])
    o_ref[...] = _rms(y, gf_ref[...]) if final else y


def _ffn_sample(l, x1, ox, wo, g3, wup, cw, cb, wdn, gf, hist_all, prev, *, final):
    bsz, d = x1.shape
    dff = wdn.shape[1]
    full = lambda a: pl.BlockSpec(a.shape, lambda i: (0,) * a.ndim)
    hshape = hist_all.shape[1:]
    return pl.pallas_call(
        functools.partial(_ffn_sample_kernel, final=final),
        grid=(1,),
        in_specs=[full(x1), full(ox), _weight_spec(wo, l), _weight_spec(g3, l), _weight_spec(wup, l),
                  _weight_spec(cw, l), _weight_spec(cb, l),
                  _weight_spec(wdn, l), full(gf), pl.BlockSpec((None,) + hshape, lambda i: (l, 0, 0, 0))]
                 + [full(p) for p in prev],
        out_specs=[pl.BlockSpec((bsz, d), lambda i: (0, 0)), pl.BlockSpec((l + 1,) + hshape, lambda i: (0, 0, 0, 0))],
        out_shape=[jax.ShapeDtypeStruct((bsz, d), F32), jax.ShapeDtypeStruct((l + 1,) + hshape, F32)],
        scratch_shapes=[pltpu.VMEM((bsz, dff), BF16)],
        compiler_params=_params("arbitrary"),
        name="ffn_sample",
    )(x1, ox, wo, g3, wup, cw, cb, wdn, gf, hist_all, *prev)


def _t5_buckets(dist):
    n = np.maximum(dist, 0)
    max_exact = N_BUCKETS // 2
    nf = np.maximum(n, max_exact).astype(np.float32)
    large = max_exact + (np.log(nf / np.float32(max_exact)) / np.float32(math.log(MAX_DISTANCE / max_exact))
                         * np.float32(N_BUCKETS - max_exact)).astype(np.int32)
    return np.where(n < max_exact, n, np.minimum(large, N_BUCKETS - 1))


def _prompt_buckets():
    W = WINDOW
    dist = np.arange(W)[None, :] + W - np.arange(2 * W)[:, None]
    band = (dist >= 0) & (dist < W)
    return np.where(band, _t5_buckets(dist), -1).astype(np.int32)


def _swa_tables(rel_bias):
    W = WINDOW
    dist_c = W - np.arange(W)
    tab = jnp.transpose(rel_bias[_t5_buckets(dist_c)], (1, 0))
    cache_bias = jnp.where((dist_c < W)[None], tab, NEG_INF)
    return cache_bias, rel_bias[0]


def kernel(x_prompt, x_sample, mem_prompt, state_mlstm_C, state_mlstm_n, state_mlstm_m, state_conv, cache_swa_k, cache_swa_v, cache_mem_k, cache_mem_v, state_ffn_conv, rel_bias, norm1_g, w_in, b_i, b_f, mlstm_norm_g, conv_w, conv_b, conv_ln_g, conv_ln_b, swa_sinks, w_out, norm2_g, w_xq, w_xk, w_xv, w_xo, norm3_g, w_up, ffn_conv_w, ffn_conv_b, w_down, final_norm_g):
    depth = w_in.shape[0]
    bp, seq, d = x_prompt.shape
    bs = x_sample.shape[0]
    mem = mem_prompt.shape[1]
    dff = w_down.shape[1]
    W = WINDOW
    nh = C_KERNEL - 1

    xp = x_prompt
    xs = x_sample.reshape(bs, d)
    gf = final_norm_g.reshape(1, d)
    c_all = jnp.transpose(state_mlstm_C, (0, 2, 3, 4, 1))
    n_all = jnp.transpose(state_mlstm_n, (0, 2, 3, 1))
    m_all = jnp.transpose(state_mlstm_m, (0, 2, 1))
    hist_all = jnp.transpose(state_conv, (0, 2, 1, 3))
    kc_all = jnp.transpose(cache_swa_k, (0, 1, 3, 4, 2)).reshape(depth, bs, S_KV_WIDTH, W)
    vc_all = jnp.transpose(cache_swa_v, (0, 1, 3, 4, 2)).reshape(depth, bs, S_KV_WIDTH, W)
    mk_all = jnp.transpose(cache_mem_k, (0, 1, 3, 4, 2)).reshape(depth, bs, X_WIDTH, mem)
    mv_all = jnp.transpose(cache_mem_v, (0, 1, 3, 4, 2)).reshape(depth, bs, X_WIDTH, mem)
    pm_c, pm_n, pm_m, p_conv, p_k, p_v, p_mk, p_mv, p_ffn = ([] for _ in range(9))
    s_mlstm, s_conv, s_kv, s_ffn = [], [], [], []

    win_t = jnp.swapaxes(w_in, 1, 2)
    wout = w_out.astype(BF16)
    wxq = w_xq.astype(BF16)
    wxkv = jnp.concatenate([w_xk, w_xv], axis=2).astype(BF16)
    wxo = w_xo.astype(BF16)
    wup = w_up.astype(BF16)
    wdn = w_down.astype(BF16)
    bucket = jnp.asarray(_prompt_buckets())

    rows = lambda a: a.reshape(depth, 1, -1)
    g1, g2, g3 = rows(norm1_g), rows(norm2_g), rows(norm3_g)
    gb8 = jnp.concatenate([b_i, b_f], axis=1)
    gbias = rows(jnp.concatenate([gb8, jnp.zeros((depth, LANES - 2 * M_HEADS), F32)], axis=1))
    ng = rows(mlstm_norm_g)
    ngt = jnp.broadcast_to(mlstm_norm_g[:, :, None], (depth, M_WIDTH, bs))
    cw = jnp.concatenate([conv_w, jnp.zeros((depth, CONV_PAD - C_KERNEL, C_WIDTH), F32)], axis=1)
    cb, lg, lb = rows(conv_b), rows(conv_ln_g), rows(conv_ln_b)
    fcw = jnp.concatenate([ffn_conv_w, jnp.zeros((depth, SUBLANES - FFN_KERNEL, dff), F32)], axis=1)
    fcb = rows(ffn_conv_b)
    cache_bias, bias0 = _swa_tables(rel_bias)
    aux = jnp.concatenate([jnp.broadcast_to(bias0[None, :, None], (depth, S_Q_HEADS, 1)), swa_sinks[:, :, None],
                           jnp.zeros((depth, S_Q_HEADS, LANES - 2), F32)], axis=2)

    for l in range(depth):
        last = l == depth - 1

        mkv = _mem_kv(l, mem_prompt, wxkv)
        z, win = _in_proj(l, xp.reshape(bp * seq, d), g1, win_t, tm=1024)
        z = z.reshape(bp, seq, Z_WIDTH)
        hm, cpair, npair, mm = _mlstm_prompt(l, z, gbias, ng)
        hc, ctail = _conv_prompt(l, z, cw, cb, lg, lb, tc=min(2048, seq))
        hs, kt, vt = _swa_prompt(l, z, rel_bias, swa_sinks, bucket, qb=min(8, seq // W))
        xp = _mix_out_prompt(l, xp, hm, hc, hs, wout, g2, wxq, mkv, wxo, tm=1024)
        xp, ftail = _ffn_prompt(l, xp, g3, wup, fcw, fcb, wdn, gf, tm=512, final=last)
        half = lambda h: slice(HEAD_DIM * (h % 2), HEAD_DIM * (h % 2 + 1))
        pm_c.append(jnp.stack([jnp.swapaxes(cpair[:, h // 2, half(h), half(h)], 1, 2) for h in range(M_HEADS)], axis=1))
        pm_n.append(jnp.stack([npair[:, h // 2, h % 2, half(h)] for h in range(M_HEADS)], axis=1))
        pm_m.append(mm[:, :, 0, 0])
        p_conv.append(ctail[:, CONV_PAD - nh:, :])
        p_k.append(jnp.transpose(kt.reshape(bp, S_KV_HEADS, HEAD_DIM, W), (0, 3, 1, 2)))
        p_v.append(jnp.transpose(vt.reshape(bp, S_KV_HEADS, HEAD_DIM, W), (0, 3, 1, 2)))
        p_mk.append(jnp.transpose(mkv[:, 0:X_WIDTH, :].reshape(bp, X_HEADS, HEAD_DIM, mem), (0, 3, 1, 2)))
        p_mv.append(jnp.transpose(mkv[:, X_WIDTH:, :].reshape(bp, X_HEADS, HEAD_DIM, mem), (0, 3, 1, 2)))
        p_ffn.append(ftail[:, FFN_PAD - (FFN_KERNEL - 1):, :])

        zs, zst = _sample_in(l, xs, g1, win)
        hmt_s, *s_mlstm = _mlstm_sample(l, zst, gb8, ngt, c_all, n_all, m_all, s_mlstm)
        hc_s, *s_conv = _conv_sample(l, zs, hist_all, cw, cb, lg, lb, s_conv)
        hs_s, *s_kv = _swa_sample(l, zs, zs.reshape(bs, 1, Z_WIDTH), kc_all, vc_all, cache_bias, aux, s_kv)
        x1, qx = _mix_out_sample(l, xs, hmt_s, hc_s, hs_s.reshape(bs, S_WIDTH), wout, g2, wxq)
        ox = _xattn_sample(l, qx.reshape(bs, 1, X_WIDTH), mk_all, mv_all)
        xs, *s_ffn = _ffn_sample(l, x1, ox.reshape(bs, X_WIDTH), wxo, g3, wup, fcw, fcb, wdn, gf,
                                 state_ffn_conv, s_ffn, final=last)

    st = jnp.stack
    tr = jnp.transpose
    s_c, s_n, s_m = s_mlstm
    s_k, s_v = (a.reshape(depth, bs, S_KV_HEADS, HEAD_DIM, W) for a in s_kv)
    return (xp, xs.reshape(bs, 1, d),
            st(pm_c), st(pm_n), st(pm_m), st(p_conv), st(p_k), st(p_v), st(p_mk), st(p_mv), st(p_ffn),
            tr(s_c, (0, 4, 1, 2, 3)), tr(s_n, (0, 3, 1, 2)), tr(s_m, (0, 2, 1)),
            tr(s_conv[0], (0, 2, 1, 3)), tr(s_k, (0, 1, 4, 2, 3)), tr(s_v, (0, 1, 4, 2, 3)),
            s_ffn[0])
```

```python
import functools
import math

import numpy as np
import jax
import jax.numpy as jnp
from jax import lax
from jax.experimental import pallas as pl
from jax.experimental.pallas import tpu as pltpu

F32 = jnp.float32
BF16 = jnp.bfloat16
EPS = 1e-6
NEG_INF = float("-inf")

HEAD_DIM = 64
M_HEADS = 4
M_WIDTH = M_HEADS * HEAD_DIM
C_WIDTH = 256
C_KERNEL = 31
S_Q_HEADS = 8
S_KV_HEADS = 2
S_WIDTH = S_Q_HEADS * HEAD_DIM
S_KV_WIDTH = S_KV_HEADS * HEAD_DIM
WINDOW = 128
N_BUCKETS = 32
MAX_DISTANCE = 128
X_HEADS = 4
X_WIDTH = X_HEADS * HEAD_DIM
FFN_KERNEL = 3
QK_SCALE = HEAD_DIM ** -0.5
LOG2E = math.log2(math.e)

Z_M = 0
Z_C = 4 * M_WIDTH
Z_SQ = Z_C + 2 * C_WIDTH
Z_SKV = Z_SQ + S_WIDTH
Z_G = Z_SKV + 2 * S_KV_WIDTH
LANES = 128
SUBLANES = 8
Z_WIDTH = Z_G + LANES
VMEM_LIMIT = 56 * 1024 * 1024

M_CHUNK = 128
FFN_CHUNK = 256


def _params(*sem):
    return pltpu.CompilerParams(dimension_semantics=sem, vmem_limit_bytes=VMEM_LIMIT)


def _const_spec(shape):
    nd = len(shape)
    return pl.BlockSpec(shape, lambda *_: (0,) * nd, pipeline_mode=pl.Buffered(1))


def _weight_spec(w, l):
    nd = w.ndim - 1
    return pl.BlockSpec((None,) + w.shape[1:], lambda *_: (l,) + (0,) * nd, pipeline_mode=pl.Buffered(1))


def _rms(x, g):
    return x * lax.rsqrt(jnp.mean(x * x, axis=-1, keepdims=True) + EPS) * g


def _sigmoid(x):
    return 1.0 / (1.0 + jnp.exp(-x))


def _log_sigmoid(x):
    return jnp.minimum(x, 0.0) - jnp.log1p(jnp.exp(-jnp.abs(x)))


def _dot(a, b):
    return jnp.dot(a, b, preferred_element_type=F32)


def _dot_nt(a, b):
    return lax.dot_general(a, b, (((1,), (1,)), ((), ())), preferred_element_type=F32)


def _dot_tn(a, b):
    return lax.dot_general(a, b, (((0,), (0,)), ((), ())), preferred_element_type=F32)


IN_PROJ_ROWS = 512
IN_PROJ_COLS = 512
IN_GATES = 4 * M_WIDTH
W_PREP_ROWS = 256


def _in_proj_kernel(x_ref, g_ref, wt_ref, o_ref, wprep_ref, w_scr):
    d = x_ref.shape[1]

    @pl.when(pl.program_id(0) == 0)
    def _():
        n_gate = 2 * M_HEADS
        for src, dst, n in ((0, 0, IN_GATES), (IN_GATES + n_gate, IN_GATES, Z_G - IN_GATES)):
            for c in range(0, n, W_PREP_ROWS):
                w_scr[:, dst + c:dst + c + W_PREP_ROWS] = wt_ref[src + c:src + c + W_PREP_ROWS, :].T.astype(BF16)
        gate_rows = jnp.concatenate([wt_ref[IN_GATES:IN_GATES + n_gate, :], jnp.zeros((LANES - n_gate, d), F32)],
                                    axis=0)
        w_scr[:, Z_G:Z_G + LANES] = gate_rows.T.astype(BF16)
        wprep_ref[...] = w_scr[...]

    tm, n = o_ref.shape
    sub = min(IN_PROJ_ROWS, tm)
    for r0 in range(0, tm, sub):
        h = _rms(x_ref[r0:r0 + sub, :], g_ref[...]).astype(BF16)
        for c in range(0, n, IN_PROJ_COLS):
            w = min(IN_PROJ_COLS, n - c)
            o_ref[r0:r0 + sub, c:c + w] = _dot(h, w_scr[:, c:c + w])


def _in_proj(l, x, g, wt, *, tm):
    m, d = x.shape
    tm = min(tm, m)
    return pl.pallas_call(
        _in_proj_kernel,
        grid=(m // tm,),
        in_specs=[pl.BlockSpec((tm, d), lambda i: (i, 0)), _weight_spec(g, l), _weight_spec(wt, l)],
        out_specs=[pl.BlockSpec((tm, Z_WIDTH), lambda i: (i, 0)), pl.BlockSpec((d, Z_WIDTH), lambda i: (0, 0))],
        out_shape=[jax.ShapeDtypeStruct((m, Z_WIDTH), F32), jax.ShapeDtypeStruct((d, Z_WIDTH), BF16)],
        scratch_shapes=[pltpu.VMEM((d, Z_WIDTH), BF16)],
        compiler_params=_params("arbitrary"),
        name="in_proj",
    )(x, g, wt)


def _mem_kv_kernel(x_ref, w_ref, o_ref, acc):
    acc[...] = _dot(x_ref[...].astype(BF16), w_ref[...])
    o_ref[...] = acc[...].T


def _mem_kv(l, mem3, w):
    b, mem, d = mem3.shape
    n = w.shape[2]
    return pl.pallas_call(
        _mem_kv_kernel,
        grid=(b,),
        in_specs=[pl.BlockSpec((None, mem, d), lambda i: (i, 0, 0)), _weight_spec(w, l)],
        out_specs=pl.BlockSpec((None, n, mem), lambda i: (i, 0, 0)),
        out_shape=jax.ShapeDtypeStruct((b, n, mem), F32),
        scratch_shapes=[pltpu.VMEM((mem, n), F32)],
        compiler_params=_params("arbitrary"),
        name="mem_kv",
    )(mem3, w)


def _sample_in_kernel(x_ref, g_ref, w_ref, z_ref, zt_ref):
    h = _rms(x_ref[...], g_ref[...]).astype(BF16)
    n = z_ref.shape[1]
    for c in range(0, n, IN_PROJ_COLS):
        w = min(IN_PROJ_COLS, n - c)
        zc = _dot(h, w_ref[:, c:c + w])
        z_ref[:, c:c + w] = zc
        zt_ref[c:c + w, :] = zc.T


def _sample_in(l, x, g, w):
    m, d = x.shape
    n = w.shape[1]
    full = lambda shape: pl.BlockSpec(shape, lambda i: (0,) * len(shape))
    return pl.pallas_call(
        _sample_in_kernel,
        grid=(1,),
        in_specs=[full((m, d)), _weight_spec(g, l), full((d, n))],
        out_specs=[full((m, n)), full((n, m))],
        out_shape=[jax.ShapeDtypeStruct((m, n), F32), jax.ShapeDtypeStruct((n, m), F32)],
        compiler_params=_params("arbitrary"),
        name="sample_in",
    )(x, g, w)


def _mlstm_prompt_kernel(z_ref, g_ref, gb_ref, ng_ref, h_ref, cp_ref, np_ref, m_ref,
                         cp_scr, np_scr, m_scr, ht_scr, st_scr, kw_scr):
    NB, L = z_ref.shape[0], z_ref.shape[1]
    D = HEAD_DIM
    hi = lax.Precision.HIGHEST

    @pl.when(pl.program_id(0) == 0)
    def _():
        cp_scr[...] = jnp.zeros(cp_scr.shape, F32)
        np_scr[...] = jnp.zeros(np_scr.shape, F32)
        m_scr[...] = jnp.zeros(m_scr.shape, F32)

    src = lax.broadcasted_iota(jnp.int32, (L, L), 0)
    qry = lax.broadcasted_iota(jnp.int32, (L, L), 1)
    causal_t = src <= qry
    upper = jnp.where(causal_t, 1.0, 0.0)
    lane_half = lax.broadcasted_iota(jnp.int32, (L, LANES), 1) // D
    row8 = lax.broadcasted_iota(jnp.int32, (SUBLANES, LANES), 0)

    rows, cols = {}, []
    for b in range(NB):
        g_t = (g_ref[b] + gb_ref[...]).T[0:SUBLANES, :]
        b_rows = jnp.dot(_log_sigmoid(g_t), upper, precision=hi, preferred_element_type=F32)
        to_cols = []
        for h in range(M_HEADS):
            b_row = b_rows[M_HEADS + h:M_HEADS + h + 1, :]
            ci_row = g_t[h:h + 1, :] - b_row
            m_prev = m_scr[b, h, 0:1, 0:1]
            b_last = b_row[:, L - 1:L]
            m_new = jnp.maximum(b_last + m_prev, jnp.max(b_last + ci_row, axis=1, keepdims=True))
            rows[b, h] = dict(b_row2=b_row * LOG2E, a_row2=(b_row + m_prev) * LOG2E, m_new=m_new,
                              decay=jnp.exp(b_last + m_prev - m_new))
            to_cols += [ci_row * LOG2E, jnp.exp(b_last + ci_row - m_new)]
        to_cols.append(jnp.zeros((LANES - 2 * M_HEADS, L), F32))
        cols.append(jnp.concatenate(to_cols, axis=0).T)

    stats = {}
    for b in range(NB):
        for j in range(M_HEADS // 2):
            slab = LANES * j
            q2 = z_ref[b, :, slab:slab + LANES].astype(BF16)
            k2 = z_ref[b, :, M_WIDTH + slab:M_WIDTH + slab + LANES] * QK_SCALE
            for par in range(2):
                h = 2 * j + par
                r = rows[b, h]
                ci_col = cols[b][:, 2 * h:2 * h + 1]
                a_row = r["a_row2"]
                dm = jnp.where(causal_t, r["b_row2"] + ci_col, NEG_INF)
                m_row = jnp.maximum(a_row, jnp.max(dm, axis=0, keepdims=True))
                k_own = jnp.where(lane_half == par, k2, 0.0).astype(BF16)
                s_t = _dot_nt(k_own, q2) * jnp.exp2(dm - m_row)
                st_scr[b, j, :, L * par:L * (par + 1)] = s_t.astype(BF16)
                stats[b, h] = dict(w_inter=jnp.exp2(a_row - m_row), floor=jnp.exp2(-m_row),
                                   den_s=jnp.sum(s_t, axis=0, keepdims=True),
                                   decay=r["decay"], m_new=r["m_new"])
            wk = jnp.where(lane_half == 0, cols[b][:, 4 * j + 1:4 * j + 2], cols[b][:, 4 * j + 3:4 * j + 4])
            kw = k2 * wk
            kw_scr[b, j] = kw.astype(BF16)
            stats[b, j, "k_sum"] = jnp.sum(kw, axis=0, keepdims=True)

    block_diag = (lax.broadcasted_iota(jnp.int32, (LANES, LANES), 0) // D
                  == lax.broadcasted_iota(jnp.int32, (LANES, LANES), 1) // D)
    for b in range(NB):
        for j in range(M_HEADS // 2):
            slab = LANES * j
            q2 = z_ref[b, :, slab:slab + LANES].astype(BF16)
            v2 = z_ref[b, :, 2 * M_WIDTH + slab:2 * M_WIDTH + slab + LANES].astype(BF16)
            cp = cp_scr[b, j]
            npair = np_scr[b, j]
            qc = _dot_nt(cp.astype(BF16), q2)
            qn = _dot_nt(npair.astype(BF16), q2)
            pv = _dot_tn(v2, st_scr[b, j])
            upd = jnp.where(block_diag, _dot_tn(v2, kw_scr[b, j]), 0.0)
            n_new = jnp.zeros((SUBLANES, LANES), F32)
            for par in range(2):
                h = 2 * j + par
                st = stats[b, h]
                hr = slice(D * par, D * (par + 1))
                num = st["w_inter"] * qc[hr, :] + pv[hr, L * par:L * (par + 1)]
                den = st["w_inter"] * qn[par:par + 1, :] + st["den_s"]
                hh = num * (1.0 / jnp.maximum(jnp.abs(den), st["floor"]))
                hh = hh * lax.rsqrt(jnp.mean(hh * hh, axis=0, keepdims=True) + EPS)
                ht_scr[b, D * h:D * (h + 1), :] = hh
                cp_scr[b, j, hr, :] = st["decay"] * cp[hr, :] + upd[hr, :]
                k_sum = jnp.where(lane_half[0:1, :] == par, stats[b, j, "k_sum"], 0.0)
                n_new = jnp.where(row8 == par, st["decay"] * npair[par:par + 1, :] + k_sum, n_new)
                m_scr[b, h] = jnp.broadcast_to(st["m_new"], m_scr.shape[2:])
            np_scr[b, j] = n_new
        o_gate = _sigmoid(z_ref[b, :, 3 * M_WIDTH:4 * M_WIDTH])
        h_ref[b] = (ht_scr[b].T * ng_ref[...] * o_gate).astype(h_ref.dtype)

    cp_ref[...] = cp_scr[...]
    np_ref[...] = np_scr[...]
    m_ref[...] = m_scr[...]


def _mlstm_prompt(l, z3, gbias, ng):
    b, s, _ = z3.shape
    L = M_CHUNK
    P = M_HEADS // 2
    return pl.pallas_call(
        _mlstm_prompt_kernel,
        grid=(s // L,),
        in_specs=[
            pl.BlockSpec((b, L, 4 * M_WIDTH), lambda c: (0, c, Z_M // (4 * M_WIDTH))),
            pl.BlockSpec((b, L, LANES), lambda c: (0, c, Z_G // LANES)),
            _weight_spec(gbias, l),
            _weight_spec(ng, l),
        ],
        out_specs=[
            pl.BlockSpec((b, L, M_WIDTH), lambda c: (0, c, 0)),
            pl.BlockSpec((b, P, LANES, LANES), lambda c: (0, 0, 0, 0)),
            pl.BlockSpec((b, P, SUBLANES, LANES), lambda c: (0, 0, 0, 0)),
            pl.BlockSpec((b, M_HEADS, SUBLANES, LANES), lambda c: (0, 0, 0, 0)),
        ],
        out_shape=[
            jax.ShapeDtypeStruct((b, s, M_WIDTH), BF16),
            jax.ShapeDtypeStruct((b, P, LANES, LANES), F32),
            jax.ShapeDtypeStruct((b, P, SUBLANES, LANES), F32),
            jax.ShapeDtypeStruct((b, M_HEADS, SUBLANES, LANES), F32),
        ],
        scratch_shapes=[
            pltpu.VMEM((b, P, LANES, LANES), F32),
            pltpu.VMEM((b, P, SUBLANES, LANES), F32),
            pltpu.VMEM((b, M_HEADS, SUBLANES, LANES), F32),
            pltpu.VMEM((b, M_WIDTH, L), F32),
            pltpu.VMEM((b, P, L, 2 * L), BF16),
            pltpu.VMEM((b, P, L, LANES), BF16),
        ],
        compiler_params=_params("arbitrary"),
        name="mlstm_prompt",
    )(z3, z3, gbias, ng)


CONV_PAD = 32


def _conv_prompt_kernel(z_ref, w_ref, cb_ref, lg_ref, lb_ref, h_ref, tail_ref, buf, shifted):
    tc = z_ref.shape[0]
    t = pl.program_id(1)

    @pl.when(t == 0)
    def _():
        buf[0:CONV_PAD, :] = jnp.zeros((CONV_PAD, C_WIDTH), F32)

    @pl.when(t > 0)
    def _():
        buf[0:CONV_PAD, :] = buf[tc:tc + CONV_PAD, :]

    u = z_ref[:, 0:C_WIDTH] * _sigmoid(z_ref[:, C_WIDTH:2 * C_WIDTH])
    buf[CONV_PAD:CONV_PAD + tc, :] = u
    n_sh = shifted.shape[1]
    for r in range(1, SUBLANES):
        shifted[r - 1] = buf[r:r + n_sh, :]
    off = CONV_PAD - (C_KERNEL - 1)
    acc = jnp.broadcast_to(cb_ref[...], (tc, C_WIDTH))
    for j in range(C_KERNEL):
        r = (off + j) % SUBLANES
        base = off + j - r
        win = buf[base:base + tc, :] if r == 0 else shifted[r - 1, base:base + tc, :]
        acc = acc + w_ref[j:j + 1, :] * win
    mu = jnp.mean(acc, axis=-1, keepdims=True)
    xc = acc - mu
    y = xc * lax.rsqrt(jnp.mean(xc * xc, axis=-1, keepdims=True) + EPS) * lg_ref[...] + lb_ref[...]
    h_ref[...] = (y * _sigmoid(y)).astype(h_ref.dtype)
    tail_ref[...] = buf[tc:tc + CONV_PAD, :]


def _conv_prompt(l, z3, w, cb, lg, lb, *, tc):
    b, s, _ = z3.shape
    return pl.pallas_call(
        _conv_prompt_kernel,
        grid=(b, s // tc),
        in_specs=[
            pl.BlockSpec((None, tc, 2 * C_WIDTH), lambda i, t: (i, t, Z_C // (2 * C_WIDTH))),
            _weight_spec(w, l), _weight_spec(cb, l), _weight_spec(lg, l), _weight_spec(lb, l),
        ],
        out_specs=[
            pl.BlockSpec((None, tc, C_WIDTH), lambda i, t: (i, t, 0)),
            pl.BlockSpec((None, CONV_PAD, C_WIDTH), lambda i, t: (i, 0, 0)),
        ],
        out_shape=[
            jax.ShapeDtypeStruct((b, s, C_WIDTH), BF16),
            jax.ShapeDtypeStruct((b, CONV_PAD, C_WIDTH), F32),
        ],
        scratch_shapes=[pltpu.VMEM((CONV_PAD + tc, C_WIDTH), F32),
                        pltpu.VMEM((SUBLANES - 1, CONV_PAD + tc - SUBLANES, C_WIDTH), F32)],
        compiler_params=_params("arbitrary", "arbitrary"),
        name="conv_prompt",
    )(z3, w, cb, lg, lb)


def _swa_prompt_kernel(rb_ref, sink_ref, bucket_ref, q_ref, kv_ref, kvp_ref, o_ref, kt_ref, vt_ref,
                       bias_scr, s_scr, p_scr, ot_scr, *, layer):
    W = WINDOW
    H = S_Q_HEADS
    G = H // S_KV_HEADS
    i = pl.program_id(0)
    n = pl.program_id(1)

    @pl.when(jnp.logical_and(i == 0, n == 0))
    def _():
        bucket = bucket_ref[...]
        prev_key = lax.broadcasted_iota(jnp.int32, (2 * W, W), 0) < W
        for h in range(H):
            acc = jnp.full((2 * W, W), NEG_INF, F32)
            for b in range(N_BUCKETS):
                acc = jnp.where(bucket == b, rb_ref[b, h], acc)
            acc = acc * LOG2E
            bias_scr[0, h] = acc
            bias_scr[1, h] = jnp.where(prev_key, NEG_INF, acc)

    QB = q_ref.shape[0] // W
    NK = (QB + 1) * W
    first = jnp.where(n == 0, 1, 0)
    kk = jnp.concatenate([kvp_ref[:, 0:S_KV_WIDTH], kv_ref[:, 0:S_KV_WIDTH]], axis=0) * (QK_SCALE * LOG2E)
    vv = jnp.concatenate([kvp_ref[:, S_KV_WIDTH:2 * S_KV_WIDTH], kv_ref[:, S_KV_WIDTH:2 * S_KV_WIDTH]], axis=0)
    kk_r = pltpu.roll(kk, HEAD_DIM, axis=1)
    lo_lane = lax.broadcasted_iota(jnp.int32, (NK, S_KV_WIDTH), 1) < HEAD_DIM
    k_var = [[jnp.where(lo_lane, kk, 0.0).astype(BF16), jnp.where(lo_lane, 0.0, kk_r).astype(BF16)],
             [jnp.where(lo_lane, kk_r, 0.0).astype(BF16), jnp.where(lo_lane, 0.0, kk).astype(BF16)]]
    v_t = vv.T.astype(BF16)

    for j in range(QB):
        k0 = j * W
        masked = first if j == 0 else 0
        m_rows = [None] * H
        for hk in range(S_KV_HEADS):
            c0 = 2 * LANES * hk
            q_st = jnp.concatenate([q_ref[k0:k0 + W, c0:c0 + LANES], q_ref[k0:k0 + W, c0 + LANES:c0 + 2 * LANES]],
                                   axis=0).astype(BF16)
            for half in range(2):
                s_t = _dot_nt(k_var[hk][half][k0:k0 + 2 * W, :], q_st)
                for slab in range(2):
                    head = G * hk + 2 * slab + half
                    sb = s_t[:, W * slab:W * (slab + 1)] + bias_scr[masked, head]
                    s_scr[j, head] = sb
                    m_rows[head] = jnp.maximum(jnp.max(sb, axis=0, keepdims=True), sink_ref[layer, head] * LOG2E)

        inv = [None] * H
        for head in range(H):
            e = jnp.exp2(s_scr[j, head] - m_rows[head])
            den = jnp.sum(e, axis=0, keepdims=True) + jnp.exp2(sink_ref[layer, head] * LOG2E - m_rows[head])
            inv[head] = 1.0 / den
            p_scr[j, :, W * head:W * (head + 1)] = e.astype(BF16)

        for hk in range(S_KV_HEADS):
            o_t = _dot(v_t[HEAD_DIM * hk:HEAD_DIM * (hk + 1), k0:k0 + 2 * W],
                       p_scr[j, :, W * G * hk:W * G * (hk + 1)])
            for g in range(G):
                head = G * hk + g
                ot_scr[j, HEAD_DIM * head:HEAD_DIM * (head + 1), :] = o_t[:, W * g:W * (g + 1)] * inv[head]
        o_ref[k0:k0 + W, :] = ot_scr[j].T.astype(o_ref.dtype)

    @pl.when(n == pl.num_programs(1) - 1)
    def _():
        kt_ref[...] = kv_ref[(QB - 1) * W:QB * W, 0:S_KV_WIDTH].T
        vt_ref[...] = kv_ref[(QB - 1) * W:QB * W, S_KV_WIDTH:2 * S_KV_WIDTH].T


def _swa_prompt(l, z3, rel_bias, sinks, bucket, *, qb):
    b, s, _ = z3.shape
    W = WINDOW
    smem = pl.BlockSpec(memory_space=pltpu.SMEM)
    return pl.pallas_call(
        functools.partial(_swa_prompt_kernel, layer=l),
        grid=(b, s // (qb * W)),
        in_specs=[
            smem, smem, _const_spec((2 * W, W)),
            pl.BlockSpec((None, qb * W, S_WIDTH), lambda i, n: (i, n, Z_SQ // S_WIDTH)),
            pl.BlockSpec((None, qb * W, 2 * S_KV_WIDTH), lambda i, n: (i, n, Z_SKV // (2 * S_KV_WIDTH))),
            pl.BlockSpec((None, W, 2 * S_KV_WIDTH),
                         lambda i, n: (i, jnp.maximum(n * qb - 1, 0), Z_SKV // (2 * S_KV_WIDTH))),
        ],
        out_specs=[
            pl.BlockSpec((None, qb * W, S_WIDTH), lambda i, n: (i, n, 0)),
            pl.BlockSpec((None, S_KV_WIDTH, W), lambda i, n: (i, 0, 0)),
            pl.BlockSpec((None, S_KV_WIDTH, W), lambda i, n: (i, 0, 0)),
        ],
        out_shape=[
            jax.ShapeDtypeStruct((b, s, S_WIDTH), BF16),
            jax.ShapeDtypeStruct((b, S_KV_WIDTH, W), F32),
            jax.ShapeDtypeStruct((b, S_KV_WIDTH, W), F32),
        ],
        scratch_shapes=[
            pltpu.VMEM((2, S_Q_HEADS, 2 * W, W), F32),
            pltpu.VMEM((qb, S_Q_HEADS, 2 * W, W), F32),
            pltpu.VMEM((qb, 2 * W, S_Q_HEADS * W), BF16),
            pltpu.VMEM((qb, S_WIDTH, W), F32),
        ],
        compiler_params=_params("arbitrary", "arbitrary"),
        name="swa_prompt",
    )(rel_bias, sinks, bucket, z3, z3, z3)


MIX_SUB = 512


def _mix_out_prompt_kernel(x_ref, hm_ref, hc_ref, hs_ref, wout_ref, g2_ref, wq_ref, mk_ref, mv_ref, wo_ref, o_ref,
                           x1_scr, qx_scr, p_scr):
    tm = x_ref.shape[0]
    mem = mk_ref.shape[1]
    mk_t = mk_ref[...]
    mv_t = mv_ref[...]
    row_head = lax.broadcasted_iota(jnp.int32, mk_t.shape, 0) // HEAD_DIM
    k_heads = [jnp.where(row_head == h, mk_t, 0.0).astype(BF16) for h in range(X_HEADS)]
    v_cat = jnp.concatenate([jnp.where(row_head == h, mv_t, 0.0).astype(BF16) for h in range(X_HEADS)], axis=1)
    subs = [slice(r0, r0 + MIX_SUB) for r0 in range(0, tm, MIX_SUB)]
    for rows in subs:
        cat = jnp.concatenate([hm_ref[rows, :], hc_ref[rows, :], hs_ref[rows, :]], axis=1)
        x1 = x_ref[rows, :] + _dot(cat, wout_ref[...])
        x1_scr[rows, :] = x1
        qx = _dot(_rms(x1, g2_ref[...]).astype(BF16), wq_ref[...])
        qx_scr[rows, :] = (qx * (QK_SCALE * LOG2E)).astype(BF16)
    for rows in subs:
        qx = qx_scr[rows, :]
        for h in range(X_HEADS):
            s = _dot(qx, k_heads[h])
            e = jnp.exp2(s - jnp.max(s, axis=1, keepdims=True))
            p_scr[rows, mem * h:mem * (h + 1)] = (e * (1.0 / jnp.sum(e, axis=1, keepdims=True))).astype(BF16)
    for rows in subs:
        o = _dot_nt(p_scr[rows, :], v_cat)
        o_ref[rows, :] = x1_scr[rows, :] + _dot(o.astype(BF16), wo_ref[...])


def _mix_out_prompt(l, x3, hm, hc, hs, wout, g2, wq, mkv, wo, *, tm):
    b, s, d = x3.shape
    tm = min(tm, s)
    mem = mkv.shape[2]
    row = lambda w: pl.BlockSpec((None, tm, w), lambda i, t: (i, t, 0))
    return pl.pallas_call(
        _mix_out_prompt_kernel,
        grid=(b, s // tm),
        in_specs=[
            row(d), row(M_WIDTH), row(C_WIDTH), row(S_WIDTH),
            _weight_spec(wout, l), _weight_spec(g2, l), _weight_spec(wq, l),
            pl.BlockSpec((None, X_WIDTH, mem), lambda i, t: (i, 0, 0)),
            pl.BlockSpec((None, X_WIDTH, mem), lambda i, t: (i, 1, 0)),
            _weight_spec(wo, l),
        ],
        out_specs=row(d),
        out_shape=jax.ShapeDtypeStruct((b, s, d), F32),
        scratch_shapes=[pltpu.VMEM((tm, d), F32), pltpu.VMEM((tm, X_WIDTH), BF16),
                        pltpu.VMEM((tm, X_HEADS * mem), BF16)],
        compiler_params=_params("arbitrary", "arbitrary"),
        name="mix_out_prompt",
    )(x3, hm, hc, hs, wout, g2, wq, mkv, mkv, wo)


FFN_PAD = 8


def _ffn_kernel(x_ref, g3_ref, wup_ref, cw_ref, cb_ref, wdn_ref, gf_ref, x1s_ref, oxs_ref, wo_ref, hist_ref, *rest,
                final, n_prompt, tiles_per_seq):
    o_ref, tail_ref, os_ref, hist_out_ref, gbuf, carry, act = rest[-7:]
    step = pl.program_id(0)
    tm = x_ref.shape[0]
    dff = wdn_ref.shape[0]
    w = FFN_CHUNK

    @pl.when(step < n_prompt)
    def _():
        @pl.when(step % tiles_per_seq == 0)
        def _():
            carry[...] = jnp.zeros(carry.shape, F32)

        x = x_ref[...]
        h = _rms(x, g3_ref[...]).astype(BF16)
        for c in range(0, dff, w):
            a = _dot(h, wup_ref[:, c:c + w])
            g = _dot(h, wup_ref[:, dff + c:dff + c + w])
            gbuf[0:FFN_PAD, 0:w] = carry[:, c:c + w]
            gbuf[FFN_PAD:FFN_PAD + tm, 0:w] = g
            carry[:, c:c + w] = g[tm - FFN_PAD:tm, :]
            gc = (cw_ref[0:1, c:c + w] * gbuf[FFN_PAD - 2:FFN_PAD - 2 + tm, 0:w]
                  + cw_ref[1:2, c:c + w] * gbuf[FFN_PAD - 1:FFN_PAD - 1 + tm, 0:w]
                  + cw_ref[2:3, c:c + w] * g + cb_ref[:, c:c + w])
            act[:, c:c + w] = (gc * _sigmoid(gc) * a).astype(BF16)
        y = x + _dot(act[...], wdn_ref[...])
        o_ref[...] = _rms(y, gf_ref[...]) if final else y
        tail_ref[...] = carry[...]

    @pl.when(step == n_prompt)
    def _():
        _copy_earlier_layers(rest[:-7], (hist_out_ref,))
        l = hist_out_ref.shape[0] - 1
        bs = x1s_ref.shape[0]
        xs = x1s_ref[...] + _dot(oxs_ref[...].astype(BF16), wo_ref[...])
        hs = _rms(xs, g3_ref[...]).astype(BF16)
        for c in range(0, dff, w):
            a = _dot(hs, wup_ref[:, c:c + w])
            g = _dot(hs, wup_ref[:, dff + c:dff + c + w])
            h1 = hist_ref[:, 1, c:c + w]
            gc = (cw_ref[0:1, c:c + w] * hist_ref[:, 0, c:c + w] + cw_ref[1:2, c:c + w] * h1
                  + cw_ref[2:3, c:c + w] * g + cb_ref[:, c:c + w])
            act[0:bs, c:c + w] = (gc * _sigmoid(gc) * a).astype(BF16)
            hist_out_ref[l, :, 0, c:c + w] = h1
            hist_out_ref[l, :, 1, c:c + w] = g
        ys = xs + _dot(act[0:bs, :], wdn_ref[...])
        os_ref[...] = _rms(ys, gf_ref[...]) if final else ys


def _ffn(l, x3, g3, wup, cw, cb, wdn, gf, x1s, oxs, wo, hist_all, prev, *, tm, final):
    b, s, d = x3.shape
    bs = x1s.shape[0]
    tm = min(tm, s)
    assert bs <= tm
    dff = wdn.shape[1]
    tiles = s // tm
    n_prompt = b * tiles
    hshape = hist_all.shape[1:]

    def tile(step):
        p = jnp.minimum(step, n_prompt - 1)
        return p // tiles, p % tiles

    full = lambda a: pl.BlockSpec(a.shape, lambda step: (0,) * a.ndim)
    return pl.pallas_call(
        functools.partial(_ffn_kernel, final=final, n_prompt=n_prompt, tiles_per_seq=tiles),
        grid=(n_prompt + 1,),
        in_specs=[
            pl.BlockSpec((None, tm, d), lambda step: tile(step) + (0,)),
            _weight_spec(g3, l), _weight_spec(wup, l), _weight_spec(cw, l), _weight_spec(cb, l),
            _weight_spec(wdn, l), _const_spec((1, d)),
            full(x1s), full(oxs), _weight_spec(wo, l),
            pl.BlockSpec((None,) + hshape, lambda step: (l, 0, 0, 0)),
        ] + [full(p) for p in prev],
        out_specs=[
            pl.BlockSpec((None, tm, d), lambda step: tile(step) + (0,)),
            pl.BlockSpec((None, FFN_PAD, dff), lambda step: (tile(step)[0], 0, 0)),
            pl.BlockSpec((bs, d), lambda step: (0, 0)),
            pl.BlockSpec((l + 1,) + hshape, lambda step: (0, 0, 0, 0)),
        ],
        out_shape=[jax.ShapeDtypeStruct((b, s, d), F32), jax.ShapeDtypeStruct((b, FFN_PAD, dff), F32),
                   jax.ShapeDtypeStruct((bs, d), F32), jax.ShapeDtypeStruct((l + 1,) + hshape, F32)],
        scratch_shapes=[
            pltpu.VMEM((FFN_PAD + tm, FFN_CHUNK), F32),
            pltpu.VMEM((FFN_PAD, dff), F32),
            pltpu.VMEM((tm, dff), BF16),
        ],
        compiler_params=_params("arbitrary"),
        name="ffn",
    )(x3, g3, wup, cw, cb, wdn, gf, x1s, oxs, wo, hist_all, *prev)


SWA_SAMPLE_BLOCK = 32
XATTN_SAMPLE_BLOCK = 16
CONV_SAMPLE_BLOCK = 32


def _mlstm_sample_kernel(gb_ref, q_ref, k_ref, v_ref, o_ref, g_ref, ng_ref, c_ref, n_ref, m_ref, *rest, layer):
    h_ref, co_ref, no_ref, mo_ref, kw_scr = rest[-5:]
    _copy_earlier_layers(rest[:-5], (co_ref, no_ref, mo_ref))
    h = pl.program_id(0)
    i_pre = g_ref[pl.ds(h, 1), :] + gb_ref[layer, h]
    f_pre = g_ref[pl.ds(M_HEADS + h, 1), :] + gb_ref[layer, M_HEADS + h]
    a = _log_sigmoid(f_pre) + m_ref[pl.ds(h, 1), :]
    m_t = jnp.maximum(a, i_pre)
    w_old = jnp.exp(a - m_t)
    w_new = jnp.exp(i_pre - m_t)
    q = q_ref[...]
    k = k_ref[...] * QK_SCALE
    v = v_ref[...]
    n_old = n_ref[...]
    kw_scr[...] = k * w_new

    def body(d, acc):
        c_old = c_ref[d]
        co_ref[layer, d] = w_old * c_old + kw_scr[pl.ds(d, 1), :] * v
        return acc + q_ref[pl.ds(d, 1), :] * c_old

    qc = lax.fori_loop(0, HEAD_DIM, body, jnp.zeros(v.shape, F32), unroll=8)
    s = jnp.sum(q * k, axis=0, keepdims=True) * w_new
    num = w_old * qc + s * v
    den = w_old * jnp.sum(q * n_old, axis=0, keepdims=True) + s
    hh = num / jnp.maximum(jnp.abs(den), jnp.exp(-m_t))
    hh = hh * lax.rsqrt(jnp.mean(hh * hh, axis=0, keepdims=True) + EPS) * ng_ref[...]
    h_ref[...] = hh * _sigmoid(o_ref[...])
    no_ref[layer] = w_old * n_old + kw_scr[...]
    mo_ref[layer, pl.ds(h, 1), :] = m_t


def _mlstm_sample(l, zt, gb, ngt, c_all, n_all, m_all, prev):
    bsz = zt.shape[1]
    D = HEAD_DIM
    feat = lambda off: pl.BlockSpec((D, bsz), lambda h: (off // D + h, 0))
    c_spec = lambda n: pl.BlockSpec((n, None, D, D, bsz), lambda h: (0, h, 0, 0, 0))
    n_spec = lambda n: pl.BlockSpec((n, None, D, bsz), lambda h: (0, h, 0, 0))
    m_spec = lambda n: pl.BlockSpec((n, M_HEADS, bsz), lambda h: (0, 0, 0))
    prev_specs = [c_spec(l), n_spec(l), m_spec(l)] if prev else []
    return pl.pallas_call(
        functools.partial(_mlstm_sample_kernel, layer=l),
        grid=(M_HEADS,),
        in_specs=[
            pl.BlockSpec(memory_space=pltpu.SMEM),
            feat(Z_M), feat(Z_M + M_WIDTH), feat(Z_M + 2 * M_WIDTH), feat(Z_M + 3 * M_WIDTH),
            pl.BlockSpec((SUBLANES, bsz), lambda h: (Z_G // SUBLANES, 0)),
            pl.BlockSpec((None, D, bsz), lambda h: (l, h, 0)),
            pl.BlockSpec((None, None, D, D, bsz), lambda h: (l, h, 0, 0, 0)),
            pl.BlockSpec((None, None, D, bsz), lambda h: (l, h, 0, 0)),
            pl.BlockSpec((None, M_HEADS, bsz), lambda h: (l, 0, 0)),
        ] + prev_specs,
        out_specs=[pl.BlockSpec((D, bsz), lambda h: (h, 0)), c_spec(l + 1), n_spec(l + 1), m_spec(l + 1)],
        out_shape=[
            jax.ShapeDtypeStruct((M_WIDTH, bsz), F32),
            jax.ShapeDtypeStruct((l + 1, M_HEADS, D, D, bsz), F32),
            jax.ShapeDtypeStruct((l + 1, M_HEADS, D, bsz), F32),
            jax.ShapeDtypeStruct((l + 1, M_HEADS, bsz), F32),
        ],
        scratch_shapes=[pltpu.VMEM((D, bsz), F32)],
        compiler_params=_params("arbitrary"),
        name="mlstm_sample",
    )(gb, zt, zt, zt, zt, zt, ngt, c_all, n_all, m_all, *prev)


def _conv_sample_kernel(z_ref, hist_ref, w_ref, cb_ref, lg_ref, lb_ref, *rest):
    h_ref, hist_out_ref = rest[-2:]
    _copy_earlier_layers(rest[:-2], (hist_out_ref,))
    l = hist_out_ref.shape[0] - 1
    nh = C_KERNEL - 1
    u = z_ref[:, 0:C_WIDTH] * _sigmoid(z_ref[:, C_WIDTH:2 * C_WIDTH])
    acc = cb_ref[...] + w_ref[nh:nh + 1, :] * u
    for j in range(nh):
        acc = acc + w_ref[j:j + 1, :] * hist_ref[j]
    mu = jnp.mean(acc, axis=-1, keepdims=True)
    xc = acc - mu
    y = xc * lax.rsqrt(jnp.mean(xc * xc, axis=-1, keepdims=True) + EPS) * lg_ref[...] + lb_ref[...]
    h_ref[...] = y * _sigmoid(y)
    for j in range(nh - 1):
        hist_out_ref[l, j] = hist_ref[j + 1]
    hist_out_ref[l, nh - 1] = u


def _conv_sample(l, z, hist_all, w, cb, lg, lb, prev):
    bsz = z.shape[0]
    nh = hist_all.shape[1]
    R = min(CONV_SAMPLE_BLOCK, bsz)
    hist_spec = lambda n: pl.BlockSpec((n, nh, R, C_WIDTH), lambda i: (0, 0, i, 0))
    return pl.pallas_call(
        _conv_sample_kernel,
        grid=(bsz // R,),
        in_specs=[
            pl.BlockSpec((R, 2 * C_WIDTH), lambda i: (i, Z_C // (2 * C_WIDTH))),
            pl.BlockSpec((None, nh, R, C_WIDTH), lambda i: (l, 0, i, 0)),
            _weight_spec(w, l), _weight_spec(cb, l), _weight_spec(lg, l), _weight_spec(lb, l),
        ] + [hist_spec(l)] * len(prev),
        out_specs=[pl.BlockSpec((R, C_WIDTH), lambda i: (i, 0)), hist_spec(l + 1)],
        out_shape=[jax.ShapeDtypeStruct((bsz, C_WIDTH), F32), jax.ShapeDtypeStruct((l + 1, nh, bsz, C_WIDTH), F32)],
        compiler_params=_params("arbitrary"),
        name="conv_sample",
    )(z, hist_all, w, cb, lg, lb, *prev)


def _copy_earlier_layers(prev_refs, out_refs):
    for p_ref, o_ref in zip(prev_refs, out_refs):
        o_ref[0:p_ref.shape[0]] = p_ref[...]


def _swa_sample_kernel(q_ref, kvn_ref, kv2_ref, kc_ref, vc_ref, bias_ref, aux_ref, *rest):
    o_ref, ko_ref, vo_ref = rest[-3:]
    _copy_earlier_layers(rest[:-3], (ko_ref, vo_ref))
    l = ko_ref.shape[0] - 1
    R = q_ref.shape[0]
    W = kc_ref.shape[2]
    H = S_Q_HEADS
    shape = (R, H, LANES)
    row = lax.broadcasted_iota(jnp.int32, shape, 1)
    lane_half = lax.broadcasted_iota(jnp.int32, shape, 2) // HEAD_DIM
    q_half = row % 2
    kv_head = row // (H // S_KV_HEADS)
    qs = jnp.zeros(shape, F32)
    for j in range(H // 2):
        qs = jnp.where(row // 2 == j, q_ref[:, :, LANES * j:LANES * (j + 1)], qs)
    q8 = jnp.where(lane_half == kv_head, jnp.where(q_half == kv_head, qs, pltpu.roll(qs, HEAD_DIM, axis=2)), 0.0)
    k_new = kvn_ref[:, :, 0:S_KV_WIDTH]
    v_new = kvn_ref[:, :, S_KV_WIDTH:2 * S_KV_WIDTH]
    s = jnp.einsum("bqd,bdk->bqk", q8.astype(BF16), kc_ref[...].astype(BF16), preferred_element_type=F32) * QK_SCALE
    s = s + bias_ref[...][None]
    s_new = jnp.sum(q8 * k_new, axis=2, keepdims=True) * QK_SCALE + aux_ref[:, 0:1][None]
    sink = aux_ref[:, 1:2][None]
    m = jnp.maximum(jnp.maximum(jnp.max(s, axis=2, keepdims=True), s_new), sink)
    e = jnp.exp(s - m)
    e_new = jnp.exp(s_new - m)
    inv = 1.0 / (jnp.sum(e, axis=2, keepdims=True) + e_new + jnp.exp(sink - m))
    o8 = jnp.einsum("bqk,bdk->bqd", e.astype(BF16), vc_ref[...].astype(BF16), preferred_element_type=F32)
    o8 = (o8 + e_new * v_new) * inv
    o8 = jnp.where(lane_half == q_half, jnp.where(q_half == kv_head, o8, pltpu.roll(o8, HEAD_DIM, axis=2)), 0.0)
    for j in range(H // 2):
        o_ref[:, :, LANES * j:LANES * (j + 1)] = jnp.sum(jnp.where(row // 2 == j, o8, 0.0), axis=1, keepdims=True)
    k_cols = kv2_ref[:, 0:S_KV_WIDTH].T
    v_cols = kv2_ref[:, S_KV_WIDTH:2 * S_KV_WIDTH].T
    last = lax.broadcasted_iota(jnp.int32, (S_KV_WIDTH, W), 1) == W - 1
    for r in range(R):
        ko_ref[l, r] = jnp.where(last, k_cols[:, r:r + 1], pltpu.roll(kc_ref[r], W - 1, axis=1))
        vo_ref[l, r] = jnp.where(last, v_cols[:, r:r + 1], pltpu.roll(vc_ref[r], W - 1, axis=1))


def _swa_sample(l, z, z3, kc_all, vc_all, bias, aux, prev):
    bsz = z.shape[0]
    R = min(SWA_SAMPLE_BLOCK, bsz)
    W = kc_all.shape[3]
    cache_in = pl.BlockSpec((None, R, S_KV_WIDTH, W), lambda i: (l, i, 0, 0))
    cache_prev = pl.BlockSpec((l, R, S_KV_WIDTH, W), lambda i: (0, i, 0, 0))
    cache_out = pl.BlockSpec((l + 1, R, S_KV_WIDTH, W), lambda i: (0, i, 0, 0))
    return pl.pallas_call(
        _swa_sample_kernel,
        grid=(bsz // R,),
        in_specs=[
            pl.BlockSpec((R, 1, S_WIDTH), lambda i: (i, 0, Z_SQ // S_WIDTH)),
            pl.BlockSpec((R, 1, 2 * S_KV_WIDTH), lambda i: (i, 0, Z_SKV // (2 * S_KV_WIDTH))),
            pl.BlockSpec((R, 2 * S_KV_WIDTH), lambda i: (i, Z_SKV // (2 * S_KV_WIDTH))),
            cache_in, cache_in,
            _const_spec((S_Q_HEADS, LANES)), _weight_spec(aux, l),
        ] + [cache_prev] * len(prev),
        out_specs=[pl.BlockSpec((R, 1, S_WIDTH), lambda i: (i, 0, 0)), cache_out, cache_out],
        out_shape=[
            jax.ShapeDtypeStruct((bsz, 1, S_WIDTH), F32),
            jax.ShapeDtypeStruct((l + 1, bsz, S_KV_WIDTH, W), F32),
            jax.ShapeDtypeStruct((l + 1, bsz, S_KV_WIDTH, W), F32),
        ],
        compiler_params=_params("arbitrary"),
        name="swa_sample",
    )(z3, z3, z, kc_all, vc_all, bias, aux, *prev)


def _mix_out_sample_kernel(x_ref, hmt_ref, hc_ref, hs_ref, wout_ref, g2_ref, wq_ref, x1_ref, q_ref):
    cat = jnp.concatenate([hmt_ref[...].T.astype(BF16), hc_ref[...].astype(BF16), hs_ref[...].astype(BF16)], axis=1)
    x1 = x_ref[...] + _dot(cat, wout_ref[...])
    x1_ref[...] = x1
    q_ref[...] = _dot(_rms(x1, g2_ref[...]).astype(BF16), wq_ref[...])


def _mix_out_sample(l, x, hm, hc, hs, wout, g2, wq):
    bsz, d = x.shape
    full = lambda a: pl.BlockSpec(a.shape, lambda i: (0,) * a.ndim)
    args = (x, hm, hc, hs, wout, g2, wq)
    return pl.pallas_call(
        _mix_out_sample_kernel,
        grid=(1,),
        in_specs=[full(x), full(hm), full(hc), full(hs), _weight_spec(wout, l), _weight_spec(g2, l),
                  _weight_spec(wq, l)],
        out_specs=[pl.BlockSpec((bsz, d), lambda i: (0, 0)), pl.BlockSpec((bsz, X_WIDTH), lambda i: (0, 0))],
        out_shape=[jax.ShapeDtypeStruct((bsz, d), F32), jax.ShapeDtypeStruct((bsz, X_WIDTH), F32)],
        compiler_params=_params("arbitrary"),
        name="mix_out_sample",
    )(*args)


def _xattn_sample_kernel(q_ref, k_ref, v_ref, o_ref):
    R = k_ref.shape[0]
    shape = (R, SUBLANES, X_WIDTH)
    row = lax.broadcasted_iota(jnp.int32, shape, 1)
    lane_head = lax.broadcasted_iota(jnp.int32, shape, 2) // HEAD_DIM
    own = row == lane_head
    q8 = jnp.where(own, jnp.broadcast_to(q_ref[...], shape), 0.0).astype(BF16)
    s = jnp.einsum("bqd,bdk->bqk", q8, k_ref[...].astype(BF16), preferred_element_type=F32) * QK_SCALE
    e = jnp.exp(s - jnp.max(s, axis=2, keepdims=True))
    p = (e / jnp.sum(e, axis=2, keepdims=True)).astype(BF16)
    o8 = jnp.einsum("bqk,bdk->bqd", p, v_ref[...].astype(BF16), preferred_element_type=F32)
    o_ref[...] = jnp.sum(jnp.where(own, o8, 0.0), axis=1, keepdims=True)


def _xattn_sample(l, q3, k_all, v_all):
    _, bsz, w, mem = k_all.shape
    R = min(XATTN_SAMPLE_BLOCK, bsz)
    kv = pl.BlockSpec((None, R, w, mem), lambda i: (l, i, 0, 0))
    qo = pl.BlockSpec((R, 1, w), lambda i: (i, 0, 0))
    return pl.pallas_call(
        _xattn_sample_kernel,
        grid=(bsz // R,),
        in_specs=[qo, kv, kv],
        out_specs=qo,
        out_shape=jax.ShapeDtypeStruct((bsz, 1, w), F32),
        compiler_params=_params("arbitrary"),
        name="xattn_sample",
    )(q3, k_all, v_all)


def _t5_buckets(dist):
    n = np.maximum(dist, 0)
    max_exact = N_BUCKETS // 2
    nf = np.maximum(n, max_exact).astype(np.float32)
    large = max_exact + (np.log(nf / np.float32(max_exact)) / np.float32(math.log(MAX_DISTANCE / max_exact))
                         * np.float32(N_BUCKETS - max_exact)).astype(np.int32)
    return np.where(n < max_exact, n, np.minimum(large, N_BUCKETS - 1))


def _prompt_buckets():
    W = WINDOW
    dist = np.arange(W)[None, :] + W - np.arange(2 * W)[:, None]
    band = (dist >= 0) & (dist < W)
    return np.where(band, _t5_buckets(dist), -1).astype(np.int32)


def _swa_tables(rel_bias):
    W = WINDOW
    dist_c = W - np.arange(W)
    tab = jnp.transpose(rel_bias[_t5_buckets(dist_c)], (1, 0))
    cache_bias = jnp.where((dist_c < W)[None], tab, NEG_INF)
    return cache_bias, rel_bias[0]


def kernel(x_prompt, x_sample, mem_prompt, state_mlstm_C, state_mlstm_n, state_mlstm_m, state_conv, cache_swa_k, cache_swa_v, cache_mem_k, cache_mem_v, state_ffn_conv, rel_bias, norm1_g, w_in, b_i, b_f, mlstm_norm_g, conv_w, conv_b, conv_ln_g, conv_ln_b, swa_sinks, w_out, norm2_g, w_xq, w_xk, w_xv, w_xo, norm3_g, w_up, ffn_conv_w, ffn_conv_b, w_down, final_norm_g):
    depth = w_in.shape[0]
    bp, seq, d = x_prompt.shape
    bs = x_sample.shape[0]
    mem = mem_prompt.shape[1]
    dff = w_down.shape[1]
    W = WINDOW
    nh = C_KERNEL - 1

    xp = x_prompt
    xs = x_sample.reshape(bs, d)
    gf = final_norm_g.reshape(1, d)
    c_all = jnp.transpose(state_mlstm_C, (0, 2, 3, 4, 1))
    n_all = jnp.transpose(state_mlstm_n, (0, 2, 3, 1))
    m_all = jnp.transpose(state_mlstm_m, (0, 2, 1))
    hist_all = jnp.transpose(state_conv, (0, 2, 1, 3))
    kc_all = jnp.transpose(cache_swa_k, (0, 1, 3, 4, 2)).reshape(depth, bs, S_KV_WIDTH, W)
    vc_all = jnp.transpose(cache_swa_v, (0, 1, 3, 4, 2)).reshape(depth, bs, S_KV_WIDTH, W)
    mk_all = jnp.transpose(cache_mem_k, (0, 1, 3, 4, 2)).reshape(depth, bs, X_WIDTH, mem)
    mv_all = jnp.transpose(cache_mem_v, (0, 1, 3, 4, 2)).reshape(depth, bs, X_WIDTH, mem)
    pm_c, pm_n, pm_m, p_conv, p_k, p_v, p_mk, p_mv, p_ffn = ([] for _ in range(9))
    s_mlstm, s_conv, s_kv, s_ffn = [], [], [], []

    win_t = jnp.swapaxes(w_in, 1, 2)
    wout = w_out.astype(BF16)
    wxq = w_xq.astype(BF16)
    wxkv = jnp.concatenate([w_xk, w_xv], axis=2).astype(BF16)
    wxo = w_xo.astype(BF16)
    wup = w_up.astype(BF16)
    wdn = w_down.astype(BF16)
    bucket = jnp.asarray(_prompt_buckets())

    rows = lambda a: a.reshape(depth, 1, -1)
    g1, g2, g3 = rows(norm1_g), rows(norm2_g), rows(norm3_g)
    gb8 = jnp.concatenate([b_i, b_f], axis=1)
    gbias = rows(jnp.concatenate([gb8, jnp.zeros((depth, LANES - 2 * M_HEADS), F32)], axis=1))
    ng = rows(mlstm_norm_g)
    ngt = jnp.broadcast_to(mlstm_norm_g[:, :, None], (depth, M_WIDTH, bs))
    cw = jnp.concatenate([conv_w, jnp.zeros((depth, CONV_PAD - C_KERNEL, C_WIDTH), F32)], axis=1)
    cb, lg, lb = rows(conv_b), rows(conv_ln_g), rows(conv_ln_b)
    fcw = jnp.concatenate([ffn_conv_w, jnp.zeros((depth, SUBLANES - FFN_KERNEL, dff), F32)], axis=1)
    fcb = rows(ffn_conv_b)
    cache_bias, bias0 = _swa_tables(rel_bias)
    aux = jnp.concatenate([jnp.broadcast_to(bias0[None, :, None], (depth, S_Q_HEADS, 1)), swa_sinks[:, :, None],
                           jnp.zeros((depth, S_Q_HEADS, LANES - 2), F32)], axis=2)

    for l in range(depth):
        last = l == depth - 1

        mkv = _mem_kv(l, mem_prompt, wxkv)
        z, win = _in_proj(l, xp.reshape(bp * seq, d), g1, win_t, tm=1024)
        z = z.reshape(bp, seq, Z_WIDTH)
        hm, cpair, npair, mm = _mlstm_prompt(l, z, gbias, ng)
        hc, ctail = _conv_prompt(l, z, cw, cb, lg, lb, tc=min(2048, seq))
        hs, kt, vt = _swa_prompt(l, z, rel_bias, swa_sinks, bucket, qb=min(8, seq // W))
        xp = _mix_out_prompt(l, xp, hm, hc, hs, wout, g2, wxq, mkv, wxo, tm=1024)
        half = lambda h: slice(HEAD_DIM * (h % 2), HEAD_DIM * (h % 2 + 1))
        pm_c.append(jnp.stack([jnp.swapaxes(cpair[:, h // 2, half(h), half(h)], 1, 2) for h in range(M_HEADS)], axis=1))
        pm_n.append(jnp.stack([npair[:, h // 2, h % 2, half(h)] for h in range(M_HEADS)], axis=1))
        pm_m.append(mm[:, :, 0, 0])
        p_conv.append(ctail[:, CONV_PAD - nh:, :])
        p_k.append(jnp.transpose(kt.reshape(bp, S_KV_HEADS, HEAD_DIM, W), (0, 3, 1, 2)))
        p_v.append(jnp.transpose(vt.reshape(bp, S_KV_HEADS, HEAD_DIM, W), (0, 3, 1, 2)))
        p_mk.append(jnp.transpose(mkv[:, 0:X_WIDTH, :].reshape(bp, X_HEADS, HEAD_DIM, mem), (0, 3, 1, 2)))
        p_mv.append(jnp.transpose(mkv[:, X_WIDTH:, :].reshape(bp, X_HEADS, HEAD_DIM, mem), (0, 3, 1, 2)))

        zs, zst = _sample_in(l, xs, g1, win)
        hmt_s, *s_mlstm = _mlstm_sample(l, zst, gb8, ngt, c_all, n_all, m_all, s_mlstm)
        hc_s, *s_conv = _conv_sample(l, zs, hist_all, cw, cb, lg, lb, s_conv)
        hs_s, *s_kv = _swa_sample(l, zs, zs.reshape(bs, 1, Z_WIDTH), kc_all, vc_all, cache_bias, aux, s_kv)
        x1, qx = _mix_out_sample(l, xs, hmt_s, hc_s, hs_s.reshape(bs, S_WIDTH), wout, g2, wxq)
        ox = _xattn_sample(l, qx.reshape(bs, 1, X_WIDTH), mk_all, mv_all)

        xp, ftail, xs, *s_ffn = _ffn(l, xp, g3, wup, fcw, fcb, wdn, gf, x1, ox.reshape(bs, X_WIDTH), wxo,
                                     state_ffn_conv, s_ffn, tm=512, final=last)
        p_ffn.append(ftail[:, FFN_PAD - (FFN_KERNEL - 1):, :])

    st = jnp.stack
    tr = jnp.transpose
    s_c, s_n, s_m = s_mlstm
    s_k, s_v = (a.reshape(depth, bs, S_KV_HEADS, HEAD_DIM, W) for a in s_kv)
    return (xp, xs.reshape(bs, 1, d),
            st(pm_c), st(pm_n), st(pm_m), st(p_conv), st(p_k), st(p_v), st(p_mk), st(p_mv), st(p_ffn),
            tr(s_c, (0, 4, 1, 2, 3)), tr(s_n, (0, 3, 1, 2)), tr(s_m, (0, 2, 1)),
            tr(s_conv[0], (0, 2, 1, 3)), tr(s_k, (0, 1, 4, 2, 3)), tr(s_v, (0, 1, 4, 2, 3)),
            s_ffn[0])
```

```python
import functools
import math

import numpy as np
import jax
import jax.numpy as jnp
from jax import lax
from jax.experimental import pallas as pl
from jax.experimental.pallas import tpu as pltpu

F32 = jnp.float32
BF16 = jnp.bfloat16
EPS = 1e-6
NEG_INF = float("-inf")

HEAD_DIM = 64
M_HEADS = 4
M_WIDTH = M_HEADS * HEAD_DIM
C_WIDTH = 256
C_KERNEL = 31
S_Q_HEADS = 8
S_KV_HEADS = 2
S_WIDTH = S_Q_HEADS * HEAD_DIM
S_KV_WIDTH = S_KV_HEADS * HEAD_DIM
WINDOW = 128
N_BUCKETS = 32
MAX_DISTANCE = 128
X_HEADS = 4
X_WIDTH = X_HEADS * HEAD_DIM
FFN_KERNEL = 3
QK_SCALE = HEAD_DIM ** -0.5
LOG2E = math.log2(math.e)

Z_M = 0
Z_C = 4 * M_WIDTH
Z_SQ = Z_C + 2 * C_WIDTH
Z_SKV = Z_SQ + S_WIDTH
Z_G = Z_SKV + 2 * S_KV_WIDTH
LANES = 128
SUBLANES = 8
Z_WIDTH = Z_G + LANES
VMEM_LIMIT = 56 * 1024 * 1024

M_CHUNK = 128
FFN_CHUNK = 256


def _params(*sem):
    return pltpu.CompilerParams(dimension_semantics=sem, vmem_limit_bytes=VMEM_LIMIT)


def _const_spec(shape):
    nd = len(shape)
    return pl.BlockSpec(shape, lambda *_: (0,) * nd, pipeline_mode=pl.Buffered(1))


def _weight_spec(w, l):
    nd = w.ndim - 1
    return pl.BlockSpec((None,) + w.shape[1:], lambda *_: (l,) + (0,) * nd, pipeline_mode=pl.Buffered(1))


def _rms(x, g):
    return x * lax.rsqrt(jnp.mean(x * x, axis=-1, keepdims=True) + EPS) * g


def _sigmoid(x):
    return 0.5 * jnp.tanh(0.5 * x) + 0.5


def _swish(x):
    h = 0.5 * x
    return h + h * jnp.tanh(h)


def _log_sigmoid(x):
    return jnp.minimum(x, 0.0) - jnp.log1p(jnp.exp(-jnp.abs(x)))


def _dot(a, b):
    return jnp.dot(a, b, preferred_element_type=F32)


def _dot_nt(a, b):
    return lax.dot_general(a, b, (((1,), (1,)), ((), ())), preferred_element_type=F32)


def _dot_tn(a, b):
    return lax.dot_general(a, b, (((0,), (0,)), ((), ())), preferred_element_type=F32)


IN_PROJ_ROWS = 512
IN_PROJ_COLS = 512
IN_GATES = 4 * M_WIDTH
W_PREP_ROWS = 256


def _in_proj_kernel(x_ref, g_ref, wt_ref, o_ref, wprep_ref, w_scr):
    d = x_ref.shape[1]

    @pl.when(pl.program_id(0) == 0)
    def _():
        n_gate = 2 * M_HEADS
        for src, dst, n in ((0, 0, IN_GATES), (IN_GATES + n_gate, IN_GATES, Z_G - IN_GATES)):
            for c in range(0, n, W_PREP_ROWS):
                w_scr[:, dst + c:dst + c + W_PREP_ROWS] = wt_ref[src + c:src + c + W_PREP_ROWS, :].T.astype(BF16)
        gate_rows = jnp.concatenate([wt_ref[IN_GATES:IN_GATES + n_gate, :], jnp.zeros((LANES - n_gate, d), F32)],
                                    axis=0)
        w_scr[:, Z_G:Z_G + LANES] = gate_rows.T.astype(BF16)
        wprep_ref[...] = w_scr[...]

    tm, n = o_ref.shape
    sub = min(IN_PROJ_ROWS, tm)
    for r0 in range(0, tm, sub):
        h = _rms(x_ref[r0:r0 + sub, :], g_ref[...]).astype(BF16)
        for c in range(0, n, IN_PROJ_COLS):
            w = min(IN_PROJ_COLS, n - c)
            o_ref[r0:r0 + sub, c:c + w] = _dot(h, w_scr[:, c:c + w])


def _in_proj(l, x, g, wt, *, tm):
    m, d = x.shape
    tm = min(tm, m)
    return pl.pallas_call(
        _in_proj_kernel,
        grid=(m // tm,),
        in_specs=[pl.BlockSpec((tm, d), lambda i: (i, 0)), _weight_spec(g, l), _weight_spec(wt, l)],
        out_specs=[pl.BlockSpec((tm, Z_WIDTH), lambda i: (i, 0)), pl.BlockSpec((d, Z_WIDTH), lambda i: (0, 0))],
        out_shape=[jax.ShapeDtypeStruct((m, Z_WIDTH), F32), jax.ShapeDtypeStruct((d, Z_WIDTH), BF16)],
        scratch_shapes=[pltpu.VMEM((d, Z_WIDTH), BF16)],
        compiler_params=_params("arbitrary"),
        name="in_proj",
    )(x, g, wt)


def _mem_kv_kernel(x_ref, w_ref, o_ref, acc):
    acc[...] = _dot(x_ref[...].astype(BF16), w_ref[...])
    o_ref[...] = acc[...].T


def _mem_kv(l, mem3, w):
    b, mem, d = mem3.shape
    n = w.shape[2]
    return pl.pallas_call(
        _mem_kv_kernel,
        grid=(b,),
        in_specs=[pl.BlockSpec((None, mem, d), lambda i: (i, 0, 0)), _weight_spec(w, l)],
        out_specs=pl.BlockSpec((None, n, mem), lambda i: (i, 0, 0)),
        out_shape=jax.ShapeDtypeStruct((b, n, mem), F32),
        scratch_shapes=[pltpu.VMEM((mem, n), F32)],
        compiler_params=_params("arbitrary"),
        name="mem_kv",
    )(mem3, w)


def _sample_in_kernel(x_ref, g_ref, w_ref, z_ref, zt_ref):
    h = _rms(x_ref[...], g_ref[...]).astype(BF16)
    n = z_ref.shape[1]
    for c in range(0, n, IN_PROJ_COLS):
        w = min(IN_PROJ_COLS, n - c)
        zc = _dot(h, w_ref[:, c:c + w])
        z_ref[:, c:c + w] = zc
        zt_ref[c:c + w, :] = zc.T


def _sample_in(l, x, g, w):
    m, d = x.shape
    n = w.shape[1]
    full = lambda shape: pl.BlockSpec(shape, lambda i: (0,) * len(shape))
    return pl.pallas_call(
        _sample_in_kernel,
        grid=(1,),
        in_specs=[full((m, d)), _weight_spec(g, l), full((d, n))],
        out_specs=[full((m, n)), full((n, m))],
        out_shape=[jax.ShapeDtypeStruct((m, n), F32), jax.ShapeDtypeStruct((n, m), F32)],
        compiler_params=_params("arbitrary"),
        name="sample_in",
    )(x, g, w)


def _mlstm_prompt_kernel(z_ref, g_ref, gb_ref, ng_ref, h_ref, cp_ref, np_ref, m_ref,
                         cp_scr, np_scr, m_scr, ht_scr, st_scr, kw_scr):
    NB, L = z_ref.shape[0], z_ref.shape[1]
    D = HEAD_DIM
    hi = lax.Precision.HIGHEST

    @pl.when(pl.program_id(0) == 0)
    def _():
        cp_scr[...] = jnp.zeros(cp_scr.shape, F32)
        np_scr[...] = jnp.zeros(np_scr.shape, F32)
        m_scr[...] = jnp.zeros(m_scr.shape, F32)

    src = lax.broadcasted_iota(jnp.int32, (L, L), 0)
    qry = lax.broadcasted_iota(jnp.int32, (L, L), 1)
    causal_t = src <= qry
    upper = jnp.where(causal_t, 1.0, 0.0)
    lane_half = lax.broadcasted_iota(jnp.int32, (L, 128), 1) // D
    row8 = lax.broadcasted_iota(jnp.int32, (SUBLANES, 128), 0)

    rows, cols = {}, []
    for b in range(NB):
        g_t = (g_ref[b] + gb_ref[...]).T[0:SUBLANES, :]
        b_rows = jnp.dot(_log_sigmoid(g_t), upper, precision=hi, preferred_element_type=F32)
        to_cols = []
        for h in range(M_HEADS):
            b_row = b_rows[M_HEADS + h:M_HEADS + h + 1, :]
            ci_row = g_t[h:h + 1, :] - b_row
            m_prev = m_scr[b, h, 0:1, 0:1]
            b_last = b_row[:, L - 1:L]
            m_new = jnp.maximum(b_last + m_prev, jnp.max(b_last + ci_row, axis=1, keepdims=True))
            rows[b, h] = dict(b_row=b_row, m_prev=m_prev, m_new=m_new, decay=jnp.exp(b_last + m_prev - m_new))
            to_cols += [ci_row, jnp.exp(b_last + ci_row - m_new)]
        to_cols.append(jnp.zeros((128 - 2 * M_HEADS, L), F32))
        cols.append(jnp.concatenate(to_cols, axis=0).T)

    stats = {}
    for b in range(NB):
        for j in range(M_HEADS // 2):
            slab = 128 * j
            q2 = z_ref[b, :, slab:slab + 128].astype(BF16)
            k2 = z_ref[b, :, M_WIDTH + slab:M_WIDTH + slab + 128] * QK_SCALE
            for par in range(2):
                h = 2 * j + par
                r = rows[b, h]
                ci_col = cols[b][:, 2 * h:2 * h + 1]
                a_row = r["b_row"] + r["m_prev"]
                dm = jnp.where(causal_t, r["b_row"] + ci_col, NEG_INF)
                m_row = jnp.maximum(a_row, jnp.max(dm, axis=0, keepdims=True))
                k_own = jnp.where(lane_half == par, k2, 0.0).astype(BF16)
                s_t = _dot_nt(k_own, q2) * jnp.exp(dm - m_row)
                st_scr[b, j, :, L * par:L * (par + 1)] = s_t.astype(BF16)
                stats[b, h] = dict(w_inter=jnp.exp(a_row - m_row), floor=jnp.exp(-m_row),
                                   den_s=jnp.sum(s_t, axis=0, keepdims=True),
                                   decay=r["decay"], m_new=r["m_new"])
            wk = jnp.where(lane_half == 0, cols[b][:, 4 * j + 1:4 * j + 2], cols[b][:, 4 * j + 3:4 * j + 4])
            kw = k2 * wk
            kw_scr[b, j] = kw.astype(BF16)
            stats[b, j, "k_sum"] = jnp.sum(kw, axis=0, keepdims=True)

    block_diag = (lax.broadcasted_iota(jnp.int32, (128, 128), 0) // D
                  == lax.broadcasted_iota(jnp.int32, (128, 128), 1) // D)
    for b in range(NB):
        for j in range(M_HEADS // 2):
            slab = 128 * j
            q2 = z_ref[b, :, slab:slab + 128].astype(BF16)
            v2 = z_ref[b, :, 2 * M_WIDTH + slab:2 * M_WIDTH + slab + 128].astype(BF16)
            cp = cp_scr[b, j]
            npair = np_scr[b, j]
            qc = _dot_nt(cp.astype(BF16), q2)
            qn = _dot_nt(npair.astype(BF16), q2)
            pv = _dot_tn(v2, st_scr[b, j])
            upd = jnp.where(block_diag, _dot_tn(v2, kw_scr[b, j]), 0.0)
            n_new = jnp.zeros((SUBLANES, 128), F32)
            for par in range(2):
                h = 2 * j + par
                st = stats[b, h]
                hr = slice(D * par, D * (par + 1))
                num = st["w_inter"] * qc[hr, :] + pv[hr, L * par:L * (par + 1)]
                den = st["w_inter"] * qn[par:par + 1, :] + st["den_s"]
                hh = num * (1.0 / jnp.maximum(jnp.abs(den), st["floor"]))
                hh = hh * lax.rsqrt(jnp.mean(hh * hh, axis=0, keepdims=True) + EPS)
                ht_scr[b, D * h:D * (h + 1), :] = hh
                cp_scr[b, j, hr, :] = st["decay"] * cp[hr, :] + upd[hr, :]
                k_sum = jnp.where(lane_half[0:1, :] == par, stats[b, j, "k_sum"], 0.0)
                n_new = jnp.where(row8 == par, st["decay"] * npair[par:par + 1, :] + k_sum, n_new)
                m_scr[b, h] = jnp.broadcast_to(st["m_new"], m_scr.shape[2:])
            np_scr[b, j] = n_new
        o_gate = _sigmoid(z_ref[b, :, 3 * M_WIDTH:4 * M_WIDTH])
        h_ref[b] = ht_scr[b].T * ng_ref[...] * o_gate

    cp_ref[...] = cp_scr[...]
    np_ref[...] = np_scr[...]
    m_ref[...] = m_scr[...]


def _mlstm_prompt(l, z3, gbias, ng):
    b, s, _ = z3.shape
    L = M_CHUNK
    P = M_HEADS // 2
    return pl.pallas_call(
        _mlstm_prompt_kernel,
        grid=(s // L,),
        in_specs=[
            pl.BlockSpec((b, L, 4 * M_WIDTH), lambda c: (0, c, Z_M // (4 * M_WIDTH))),
            pl.BlockSpec((b, L, 128), lambda c: (0, c, Z_G // 128)),
            _weight_spec(gbias, l),
            _weight_spec(ng, l),
        ],
        out_specs=[
            pl.BlockSpec((b, L, M_WIDTH), lambda c: (0, c, 0)),
            pl.BlockSpec((b, P, 128, 128), lambda c: (0, 0, 0, 0)),
            pl.BlockSpec((b, P, SUBLANES, 128), lambda c: (0, 0, 0, 0)),
            pl.BlockSpec((b, M_HEADS, SUBLANES, 128), lambda c: (0, 0, 0, 0)),
        ],
        out_shape=[
            jax.ShapeDtypeStruct((b, s, M_WIDTH), F32),
            jax.ShapeDtypeStruct((b, P, 128, 128), F32),
            jax.ShapeDtypeStruct((b, P, SUBLANES, 128), F32),
            jax.ShapeDtypeStruct((b, M_HEADS, SUBLANES, 128), F32),
        ],
        scratch_shapes=[
            pltpu.VMEM((b, P, 128, 128), F32),
            pltpu.VMEM((b, P, SUBLANES, 128), F32),
            pltpu.VMEM((b, M_HEADS, SUBLANES, 128), F32),
            pltpu.VMEM((b, M_WIDTH, L), F32),
            pltpu.VMEM((b, P, L, 2 * L), BF16),
            pltpu.VMEM((b, P, L, 128), BF16),
        ],
        compiler_params=_params("arbitrary"),
        name="mlstm_prompt",
    )(z3, z3, gbias, ng)


CONV_PAD = 32


def _conv_prompt_kernel(z_ref, w_ref, cb_ref, lg_ref, lb_ref, h_ref, tail_ref, buf, shifted):
    tc = z_ref.shape[0]
    t = pl.program_id(1)

    @pl.when(t == 0)
    def _():
        buf[0:CONV_PAD, :] = jnp.zeros((CONV_PAD, C_WIDTH), F32)

    @pl.when(t > 0)
    def _():
        buf[0:CONV_PAD, :] = buf[tc:tc + CONV_PAD, :]

    u = z_ref[:, 0:C_WIDTH] * _sigmoid(z_ref[:, C_WIDTH:2 * C_WIDTH])
    buf[CONV_PAD:CONV_PAD + tc, :] = u
    n_sh = shifted.shape[1]
    for r in range(1, SUBLANES):
        shifted[r - 1] = buf[r:r + n_sh, :]
    off = CONV_PAD - (C_KERNEL - 1)
    acc = jnp.broadcast_to(cb_ref[...], (tc, C_WIDTH))
    for j in range(C_KERNEL):
        r = (off + j) % SUBLANES
        base = off + j - r
        win = buf[base:base + tc, :] if r == 0 else shifted[r - 1, base:base + tc, :]
        acc = acc + w_ref[j:j + 1, :] * win
    mu = jnp.mean(acc, axis=-1, keepdims=True)
    xc = acc - mu
    y = xc * lax.rsqrt(jnp.mean(xc * xc, axis=-1, keepdims=True) + EPS) * lg_ref[...] + lb_ref[...]
    h_ref[...] = _swish(y)
    tail_ref[...] = buf[tc:tc + CONV_PAD, :]


def _conv_prompt(l, z3, w, cb, lg, lb, *, tc):
    b, s, _ = z3.shape
    return pl.pallas_call(
        _conv_prompt_kernel,
        grid=(b, s // tc),
        in_specs=[
            pl.BlockSpec((None, tc, 2 * C_WIDTH), lambda i, t: (i, t, Z_C // (2 * C_WIDTH))),
            _weight_spec(w, l), _weight_spec(cb, l), _weight_spec(lg, l), _weight_spec(lb, l),
        ],
        out_specs=[
            pl.BlockSpec((None, tc, C_WIDTH), lambda i, t: (i, t, 0)),
            pl.BlockSpec((None, CONV_PAD, C_WIDTH), lambda i, t: (i, 0, 0)),
        ],
        out_shape=[
            jax.ShapeDtypeStruct((b, s, C_WIDTH), F32),
            jax.ShapeDtypeStruct((b, CONV_PAD, C_WIDTH), F32),
        ],
        scratch_shapes=[pltpu.VMEM((CONV_PAD + tc, C_WIDTH), F32),
                        pltpu.VMEM((SUBLANES - 1, CONV_PAD + tc - SUBLANES, C_WIDTH), F32)],
        compiler_params=_params("arbitrary", "arbitrary"),
        name="conv_prompt",
    )(z3, w, cb, lg, lb)


def _swa_prompt_kernel(rb_ref, sink_ref, bucket_ref, q_ref, kv_ref, kvp_ref, o_ref, kt_ref, vt_ref,
                       bias_scr, s_scr, p_scr, ot_scr, *, layer):
    W = WINDOW
    H = S_Q_HEADS
    G = H // S_KV_HEADS
    i = pl.program_id(0)
    n = pl.program_id(1)

    @pl.when(jnp.logical_and(i == 0, n == 0))
    def _():
        bucket = bucket_ref[...]
        prev_key = lax.broadcasted_iota(jnp.int32, (2 * W, W), 0) < W
        for h in range(H):
            acc = jnp.full((2 * W, W), NEG_INF, F32)
            for b in range(N_BUCKETS):
                acc = jnp.where(bucket == b, rb_ref[b, h], acc)
            acc = acc * LOG2E
            bias_scr[0, h] = acc
            bias_scr[1, h] = jnp.where(prev_key, NEG_INF, acc)

    QB = q_ref.shape[0] // W
    NK = (QB + 1) * W
    first = jnp.where(n == 0, 1, 0)
    kk = jnp.concatenate([kvp_ref[:, 0:S_KV_WIDTH], kv_ref[:, 0:S_KV_WIDTH]], axis=0) * (QK_SCALE * LOG2E)
    vv = jnp.concatenate([kvp_ref[:, S_KV_WIDTH:2 * S_KV_WIDTH], kv_ref[:, S_KV_WIDTH:2 * S_KV_WIDTH]], axis=0)
    kk_r = pltpu.roll(kk, HEAD_DIM, axis=1)
    lo_lane = lax.broadcasted_iota(jnp.int32, (NK, S_KV_WIDTH), 1) < HEAD_DIM
    k_var = [[jnp.where(lo_lane, kk, 0.0).astype(BF16), jnp.where(lo_lane, 0.0, kk_r).astype(BF16)],
             [jnp.where(lo_lane, kk_r, 0.0).astype(BF16), jnp.where(lo_lane, 0.0, kk).astype(BF16)]]
    v_t = vv.T.astype(BF16)

    for j in range(QB):
        k0 = j * W
        masked = first if j == 0 else 0
        m_rows = [None] * H
        for hk in range(S_KV_HEADS):
            c0 = 2 * 128 * hk
            q_st = jnp.concatenate([q_ref[k0:k0 + W, c0:c0 + 128], q_ref[k0:k0 + W, c0 + 128:c0 + 256]],
                                   axis=0).astype(BF16)
            for half in range(2):
                s_t = _dot_nt(k_var[hk][half][k0:k0 + 2 * W, :], q_st)
                for slab in range(2):
                    head = G * hk + 2 * slab + half
                    sb = s_t[:, 128 * slab:128 * (slab + 1)] + bias_scr[masked, head]
                    s_scr[j, head] = sb
                    m_rows[head] = jnp.maximum(jnp.max(sb, axis=0, keepdims=True), sink_ref[layer, head] * LOG2E)

        inv = [None] * H
        for head in range(H):
            e = jnp.exp2(s_scr[j, head] - m_rows[head])
            den = jnp.sum(e, axis=0, keepdims=True) + jnp.exp2(sink_ref[layer, head] * LOG2E - m_rows[head])
            inv[head] = 1.0 / den
            p_scr[j, :, 128 * head:128 * (head + 1)] = e.astype(BF16)

        for hk in range(S_KV_HEADS):
            o_t = _dot(v_t[HEAD_DIM * hk:HEAD_DIM * (hk + 1), k0:k0 + 2 * W],
                       p_scr[j, :, 128 * G * hk:128 * G * (hk + 1)])
            for g in range(G):
                head = G * hk + g
                ot_scr[j, HEAD_DIM * head:HEAD_DIM * (head + 1), :] = o_t[:, 128 * g:128 * (g + 1)] * inv[head]
        o_ref[k0:k0 + W, :] = ot_scr[j].T

    @pl.when(n == pl.num_programs(1) - 1)
    def _():
        kt_ref[...] = kv_ref[(QB - 1) * W:QB * W, 0:S_KV_WIDTH].T
        vt_ref[...] = kv_ref[(QB - 1) * W:QB * W, S_KV_WIDTH:2 * S_KV_WIDTH].T


def _swa_prompt(l, z3, rel_bias, sinks, bucket, *, qb):
    b, s, _ = z3.shape
    W = WINDOW
    smem = pl.BlockSpec(memory_space=pltpu.SMEM)
    return pl.pallas_call(
        functools.partial(_swa_prompt_kernel, layer=l),
        grid=(b, s // (qb * W)),
        in_specs=[
            smem, smem, _const_spec((2 * W, W)),
            pl.BlockSpec((None, qb * W, S_WIDTH), lambda i, n: (i, n, Z_SQ // S_WIDTH)),
            pl.BlockSpec((None, qb * W, 2 * S_KV_WIDTH), lambda i, n: (i, n, Z_SKV // (2 * S_KV_WIDTH))),
            pl.BlockSpec((None, W, 2 * S_KV_WIDTH),
                         lambda i, n: (i, jnp.maximum(n * qb - 1, 0), Z_SKV // (2 * S_KV_WIDTH))),
        ],
        out_specs=[
            pl.BlockSpec((None, qb * W, S_WIDTH), lambda i, n: (i, n, 0)),
            pl.BlockSpec((None, S_KV_WIDTH, W), lambda i, n: (i, 0, 0)),
            pl.BlockSpec((None, S_KV_WIDTH, W), lambda i, n: (i, 0, 0)),
        ],
        out_shape=[
            jax.ShapeDtypeStruct((b, s, S_WIDTH), F32),
            jax.ShapeDtypeStruct((b, S_KV_WIDTH, W), F32),
            jax.ShapeDtypeStruct((b, S_KV_WIDTH, W), F32),
        ],
        scratch_shapes=[
            pltpu.VMEM((2, S_Q_HEADS, 2 * W, W), F32),
            pltpu.VMEM((qb, S_Q_HEADS, 2 * W, W), F32),
            pltpu.VMEM((qb, 2 * W, S_Q_HEADS * W), BF16),
            pltpu.VMEM((qb, S_WIDTH, W), F32),
        ],
        compiler_params=_params("arbitrary", "arbitrary"),
        name="swa_prompt",
    )(rel_bias, sinks, bucket, z3, z3, z3)


MIX_SUB = 512


def _mix_out_prompt_kernel(x_ref, hm_ref, hc_ref, hs_ref, wout_ref, g2_ref, wq_ref, mk_ref, mv_ref, wo_ref, o_ref,
                           x1_scr, qx_scr, p_scr):
    tm = x_ref.shape[0]
    mem = mk_ref.shape[1]
    mk_t = mk_ref[...]
    mv_t = mv_ref[...]
    row_head = lax.broadcasted_iota(jnp.int32, mk_t.shape, 0) // HEAD_DIM
    k_heads = [jnp.where(row_head == h, mk_t, 0.0).astype(BF16) for h in range(X_HEADS)]
    v_cat = jnp.concatenate([jnp.where(row_head == h, mv_t, 0.0).astype(BF16) for h in range(X_HEADS)], axis=1)
    subs = [slice(r0, r0 + MIX_SUB) for r0 in range(0, tm, MIX_SUB)]
    for rows in subs:
        cat = jnp.concatenate([hm_ref[rows, :].astype(BF16), hc_ref[rows, :].astype(BF16),
                               hs_ref[rows, :].astype(BF16)], axis=1)
        x1 = x_ref[rows, :] + _dot(cat, wout_ref[...])
        x1_scr[rows, :] = x1
        qx = _dot(_rms(x1, g2_ref[...]).astype(BF16), wq_ref[...])
        qx_scr[rows, :] = (qx * (QK_SCALE * LOG2E)).astype(BF16)
    for rows in subs:
        qx = qx_scr[rows, :]
        for h in range(X_HEADS):
            s = _dot(qx, k_heads[h])
            e = jnp.exp2(s - jnp.max(s, axis=1, keepdims=True))
            p_scr[rows, mem * h:mem * (h + 1)] = (e * (1.0 / jnp.sum(e, axis=1, keepdims=True))).astype(BF16)
    for rows in subs:
        o = _dot_nt(p_scr[rows, :], v_cat)
        o_ref[rows, :] = x1_scr[rows, :] + _dot(o.astype(BF16), wo_ref[...])


def _mix_out_prompt(l, x3, hm, hc, hs, wout, g2, wq, mkv, wo, *, tm):
    b, s, d = x3.shape
    tm = min(tm, s)
    mem = mkv.shape[2]
    row = lambda w: pl.BlockSpec((None, tm, w), lambda i, t: (i, t, 0))
    return pl.pallas_call(
        _mix_out_prompt_kernel,
        grid=(b, s // tm),
        in_specs=[
            row(d), row(M_WIDTH), row(C_WIDTH), row(S_WIDTH),
            _weight_spec(wout, l), _weight_spec(g2, l), _weight_spec(wq, l),
            pl.BlockSpec((None, X_WIDTH, mem), lambda i, t: (i, 0, 0)),
            pl.BlockSpec((None, X_WIDTH, mem), lambda i, t: (i, 1, 0)),
            _weight_spec(wo, l),
        ],
        out_specs=row(d),
        out_shape=jax.ShapeDtypeStruct((b, s, d), F32),
        scratch_shapes=[pltpu.VMEM((tm, d), F32), pltpu.VMEM((tm, X_WIDTH), BF16),
                        pltpu.VMEM((tm, X_HEADS * mem), BF16)],
        compiler_params=_params("arbitrary", "arbitrary"),
        name="mix_out_prompt",
    )(x3, hm, hc, hs, wout, g2, wq, mkv, mkv, wo)


FFN_PAD = 8


def _ffn_prompt_kernel(x_ref, g3_ref, wup_ref, cw_ref, cb_ref, wdn_ref, gf_ref, o_ref, tail_ref,
                       carry, act, *, final):
    tm = x_ref.shape[0]
    dff = wdn_ref.shape[0]

    @pl.when(pl.program_id(1) == 0)
    def _():
        carry[...] = jnp.zeros(carry.shape, F32)

    x = x_ref[...]
    h = _rms(x, g3_ref[...]).astype(BF16)
    w = FFN_CHUNK
    row8 = lax.broadcasted_iota(jnp.int32, (FFN_PAD, w), 0)
    for c in range(0, dff, w):
        a = _dot(h, wup_ref[:, c:c + w])
        g = _dot(h, wup_ref[:, dff + c:dff + c + w])
        prev = carry[:, c:c + w]
        carry[:, c:c + w] = g[tm - FFN_PAD:tm, :]
        taps = []
        for s in (2, 1):
            gs = pltpu.roll(g, s, axis=0)
            head = jnp.where(row8 < s, pltpu.roll(prev, s, axis=0), gs[0:FFN_PAD, :])
            taps.append(jnp.concatenate([head, gs[FFN_PAD:, :]], axis=0))
        gc = (cw_ref[0:1, c:c + w] * taps[0] + cw_ref[1:2, c:c + w] * taps[1]
              + cw_ref[2:3, c:c + w] * g + cb_ref[:, c:c + w])
        act[:, c:c + w] = (_swish(gc) * a).astype(BF16)
    y = x + _dot(act[...], wdn_ref[...])
    o_ref[...] = _rms(y, gf_ref[...]) if final else y
    tail_ref[...] = carry[...]


def _ffn_prompt(l, x3, g3, wup, cw, cb, wdn, gf, *, tm, final):
    b, s, d = x3.shape
    tm = min(tm, s)
    dff = wdn.shape[1]
    return pl.pallas_call(
        functools.partial(_ffn_prompt_kernel, final=final),
        grid=(b, s // tm),
        in_specs=[
            pl.BlockSpec((None, tm, d), lambda i, t: (i, t, 0)),
            _weight_spec(g3, l), _weight_spec(wup, l), _weight_spec(cw, l), _weight_spec(cb, l),
            _weight_spec(wdn, l), _const_spec((1, d)),
        ],
        out_specs=[
            pl.BlockSpec((None, tm, d), lambda i, t: (i, t, 0)),
            pl.BlockSpec((None, FFN_PAD, dff), lambda i, t: (i, 0, 0)),
        ],
        out_shape=[jax.ShapeDtypeStruct((b, s, d), F32), jax.ShapeDtypeStruct((b, FFN_PAD, dff), F32)],
        scratch_shapes=[
            pltpu.VMEM((FFN_PAD, dff), F32),
            pltpu.VMEM((tm, dff), BF16),
        ],
        compiler_params=_params("arbitrary", "arbitrary"),
        name="ffn_prompt",
    )(x3, g3, wup, cw, cb, wdn, gf)


SWA_SAMPLE_BLOCK = 32
XATTN_SAMPLE_BLOCK = 16
CONV_SAMPLE_BLOCK = 32


def _mlstm_sample_kernel(gb_ref, q_ref, k_ref, v_ref, o_ref, g_ref, ng_ref, c_ref, n_ref, m_ref, *rest, layer):
    h_ref, co_ref, no_ref, mo_ref, kw_scr = rest[-5:]
    _copy_earlier_layers(rest[:-5], (co_ref, no_ref, mo_ref))
    h = pl.program_id(0)
    i_pre = g_ref[pl.ds(h, 1), :] + gb_ref[layer, h]
    f_pre = g_ref[pl.ds(M_HEADS + h, 1), :] + gb_ref[layer, M_HEADS + h]
    a = _log_sigmoid(f_pre) + m_ref[pl.ds(h, 1), :]
    m_t = jnp.maximum(a, i_pre)
    w_old = jnp.exp(a - m_t)
    w_new = jnp.exp(i_pre - m_t)
    q = q_ref[...]
    k = k_ref[...] * QK_SCALE
    v = v_ref[...]
    n_old = n_ref[...]
    kw_scr[...] = k * w_new

    def body(d, acc):
        c_old = c_ref[d]
        co_ref[layer, d] = w_old * c_old + kw_scr[pl.ds(d, 1), :] * v
        return acc + q_ref[pl.ds(d, 1), :] * c_old

    qc = lax.fori_loop(0, HEAD_DIM, body, jnp.zeros(v.shape, F32), unroll=8)
    s = jnp.sum(q * k, axis=0, keepdims=True) * w_new
    num = w_old * qc + s * v
    den = w_old * jnp.sum(q * n_old, axis=0, keepdims=True) + s
    hh = num / jnp.maximum(jnp.abs(den), jnp.exp(-m_t))
    hh = hh * lax.rsqrt(jnp.mean(hh * hh, axis=0, keepdims=True) + EPS) * ng_ref[...]
    h_ref[...] = hh * _sigmoid(o_ref[...])
    no_ref[layer] = w_old * n_old + kw_scr[...]
    mo_ref[layer, pl.ds(h, 1), :] = m_t


def _mlstm_sample(l, zt, gb, ngt, c_all, n_all, m_all, prev):
    bsz = zt.shape[1]
    D = HEAD_DIM
    feat = lambda off: pl.BlockSpec((D, bsz), lambda h: (off // D + h, 0))
    c_spec = lambda n: pl.BlockSpec((n, None, D, D, bsz), lambda h: (0, h, 0, 0, 0))
    n_spec = lambda n: pl.BlockSpec((n, None, D, bsz), lambda h: (0, h, 0, 0))
    m_spec = lambda n: pl.BlockSpec((n, M_HEADS, bsz), lambda h: (0, 0, 0))
    prev_specs = [c_spec(l), n_spec(l), m_spec(l)] if prev else []
    return pl.pallas_call(
        functools.partial(_mlstm_sample_kernel, layer=l),
        grid=(M_HEADS,),
        in_specs=[
            pl.BlockSpec(memory_space=pltpu.SMEM),
            feat(Z_M), feat(Z_M + M_WIDTH), feat(Z_M + 2 * M_WIDTH), feat(Z_M + 3 * M_WIDTH),
            pl.BlockSpec((SUBLANES, bsz), lambda h: (Z_G // SUBLANES, 0)),
            pl.BlockSpec((None, D, bsz), lambda h: (l, h, 0)),
            pl.BlockSpec((None, None, D, D, bsz), lambda h: (l, h, 0, 0, 0)),
            pl.BlockSpec((None, None, D, bsz), lambda h: (l, h, 0, 0)),
            pl.BlockSpec((None, M_HEADS, bsz), lambda h: (l, 0, 0)),
        ] + prev_specs,
        out_specs=[pl.BlockSpec((D, bsz), lambda h: (h, 0)), c_spec(l + 1), n_spec(l + 1), m_spec(l + 1)],
        out_shape=[
            jax.ShapeDtypeStruct((M_WIDTH, bsz), F32),
            jax.ShapeDtypeStruct((l + 1, M_HEADS, D, D, bsz), F32),
            jax.ShapeDtypeStruct((l + 1, M_HEADS, D, bsz), F32),
            jax.ShapeDtypeStruct((l + 1, M_HEADS, bsz), F32),
        ],
        scratch_shapes=[pltpu.VMEM((D, bsz), F32)],
        compiler_params=_params("arbitrary"),
        name="mlstm_sample",
    )(gb, zt, zt, zt, zt, zt, ngt, c_all, n_all, m_all, *prev)


def _conv_sample_kernel(z_ref, hist_ref, w_ref, cb_ref, lg_ref, lb_ref, *rest):
    h_ref, hist_out_ref = rest[-2:]
    _copy_earlier_layers(rest[:-2], (hist_out_ref,))
    l = hist_out_ref.shape[0] - 1
    nh = C_KERNEL - 1
    u = z_ref[:, 0:C_WIDTH] * _sigmoid(z_ref[:, C_WIDTH:2 * C_WIDTH])
    acc = cb_ref[...] + w_ref[nh:nh + 1, :] * u
    for j in range(nh):
        acc = acc + w_ref[j:j + 1, :] * hist_ref[j]
    mu = jnp.mean(acc, axis=-1, keepdims=True)
    xc = acc - mu
    y = xc * lax.rsqrt(jnp.mean(xc * xc, axis=-1, keepdims=True) + EPS) * lg_ref[...] + lb_ref[...]
    h_ref[...] = _swish(y)
    for j in range(nh - 1):
        hist_out_ref[l, j] = hist_ref[j + 1]
    hist_out_ref[l, nh - 1] = u


def _conv_sample(l, z, hist_all, w, cb, lg, lb, prev):
    bsz = z.shape[0]
    nh = hist_all.shape[1]
    R = min(CONV_SAMPLE_BLOCK, bsz)
    hist_spec = lambda n: pl.BlockSpec((n, nh, R, C_WIDTH), lambda i: (0, 0, i, 0))
    return pl.pallas_call(
        _conv_sample_kernel,
        grid=(bsz // R,),
        in_specs=[
            pl.BlockSpec((R, 2 * C_WIDTH), lambda i: (i, Z_C // (2 * C_WIDTH))),
            pl.BlockSpec((None, nh, R, C_WIDTH), lambda i: (l, 0, i, 0)),
            _weight_spec(w, l), _weight_spec(cb, l), _weight_spec(lg, l), _weight_spec(lb, l),
        ] + [hist_spec(l)] * len(prev),
        out_specs=[pl.BlockSpec((R, C_WIDTH), lambda i: (i, 0)), hist_spec(l + 1)],
        out_shape=[jax.ShapeDtypeStruct((bsz, C_WIDTH), F32), jax.ShapeDtypeStruct((l + 1, nh, bsz, C_WIDTH), F32)],
        compiler_params=_params("arbitrary"),
        name="conv_sample",
    )(z, hist_all, w, cb, lg, lb, *prev)


def _copy_earlier_layers(prev_refs, out_refs):
    for p_ref, o_ref in zip(prev_refs, out_refs):
        o_ref[0:p_ref.shape[0]] = p_ref[...]


def _swa_sample_kernel(q_ref, kvn_ref, kv2_ref, kc_ref, vc_ref, bias_ref, aux_ref, *rest):
    o_ref, ko_ref, vo_ref = rest[-3:]
    _copy_earlier_layers(rest[:-3], (ko_ref, vo_ref))
    l = ko_ref.shape[0] - 1
    R = q_ref.shape[0]
    W = kc_ref.shape[2]
    H = S_Q_HEADS
    shape = (R, H, 128)
    row = lax.broadcasted_iota(jnp.int32, shape, 1)
    lane_half = lax.broadcasted_iota(jnp.int32, shape, 2) // HEAD_DIM
    q_half = row % 2
    kv_head = row // (H // S_KV_HEADS)
    qs = jnp.zeros(shape, F32)
    for j in range(H // 2):
        qs = jnp.where(row // 2 == j, q_ref[:, :, 128 * j:128 * (j + 1)], qs)
    q8 = jnp.where(lane_half == kv_head, jnp.where(q_half == kv_head, qs, pltpu.roll(qs, HEAD_DIM, axis=2)), 0.0)
    k_new = kvn_ref[:, :, 0:S_KV_WIDTH]
    v_new = kvn_ref[:, :, S_KV_WIDTH:2 * S_KV_WIDTH]
    s = jnp.einsum("bqd,bdk->bqk", q8.astype(BF16), kc_ref[...].astype(BF16), preferred_element_type=F32) * QK_SCALE
    s = s + bias_ref[...][None]
    s_new = jnp.sum(q8 * k_new, axis=2, keepdims=True) * QK_SCALE + aux_ref[:, 0:1][None]
    sink = aux_ref[:, 1:2][None]
    m = jnp.maximum(jnp.maximum(jnp.max(s, axis=2, keepdims=True), s_new), sink)
    e = jnp.exp(s - m)
    e_new = jnp.exp(s_new - m)
    inv = 1.0 / (jnp.sum(e, axis=2, keepdims=True) + e_new + jnp.exp(sink - m))
    o8 = jnp.einsum("bqk,bdk->bqd", e.astype(BF16), vc_ref[...].astype(BF16), preferred_element_type=F32)
    o8 = (o8 + e_new * v_new) * inv
    o8 = jnp.where(lane_half == q_half, jnp.where(q_half == kv_head, o8, pltpu.roll(o8, HEAD_DIM, axis=2)), 0.0)
    for j in range(H // 2):
        o_ref[:, :, 128 * j:128 * (j + 1)] = jnp.sum(jnp.where(row // 2 == j, o8, 0.0), axis=1, keepdims=True)
    k_cols = kv2_ref[:, 0:S_KV_WIDTH].T
    v_cols = kv2_ref[:, S_KV_WIDTH:2 * S_KV_WIDTH].T
    last = lax.broadcasted_iota(jnp.int32, (S_KV_WIDTH, W), 1) == W - 1
    for r in range(R):
        ko_ref[l, r] = jnp.where(last, k_cols[:, r:r + 1], pltpu.roll(kc_ref[r], W - 1, axis=1))
        vo_ref[l, r] = jnp.where(last, v_cols[:, r:r + 1], pltpu.roll(vc_ref[r], W - 1, axis=1))


def _swa_sample(l, z, z3, kc_all, vc_all, bias, aux, prev):
    bsz = z.shape[0]
    R = min(SWA_SAMPLE_BLOCK, bsz)
    W = kc_all.shape[3]
    cache_in = pl.BlockSpec((None, R, S_KV_WIDTH, W), lambda i: (l, i, 0, 0))
    cache_prev = pl.BlockSpec((l, R, S_KV_WIDTH, W), lambda i: (0, i, 0, 0))
    cache_out = pl.BlockSpec((l + 1, R, S_KV_WIDTH, W), lambda i: (0, i, 0, 0))
    return pl.pallas_call(
        _swa_sample_kernel,
        grid=(bsz // R,),
        in_specs=[
            pl.BlockSpec((R, 1, S_WIDTH), lambda i: (i, 0, Z_SQ // S_WIDTH)),
            pl.BlockSpec((R, 1, 2 * S_KV_WIDTH), lambda i: (i, 0, Z_SKV // (2 * S_KV_WIDTH))),
            pl.BlockSpec((R, 2 * S_KV_WIDTH), lambda i: (i, Z_SKV // (2 * S_KV_WIDTH))),
            cache_in, cache_in,
            _const_spec((S_Q_HEADS, 128)), _weight_spec(aux, l),
        ] + [cache_prev] * len(prev),
        out_specs=[pl.BlockSpec((R, 1, S_WIDTH), lambda i: (i, 0, 0)), cache_out, cache_out],
        out_shape=[
            jax.ShapeDtypeStruct((bsz, 1, S_WIDTH), F32),
            jax.ShapeDtypeStruct((l + 1, bsz, S_KV_WIDTH, W), F32),
            jax.ShapeDtypeStruct((l + 1, bsz, S_KV_WIDTH, W), F32),
        ],
        compiler_params=_params("arbitrary"),
        name="swa_sample",
    )(z3, z3, z, kc_all, vc_all, bias, aux, *prev)


def _mix_out_sample_kernel(x_ref, hmt_ref, hc_ref, hs_ref, wout_ref, g2_ref, wq_ref, x1_ref, q_ref):
    cat = jnp.concatenate([hmt_ref[...].T.astype(BF16), hc_ref[...].astype(BF16), hs_ref[...].astype(BF16)], axis=1)
    x1 = x_ref[...] + _dot(cat, wout_ref[...])
    x1_ref[...] = x1
    q_ref[...] = _dot(_rms(x1, g2_ref[...]).astype(BF16), wq_ref[...])


def _mix_out_sample(l, x, hm, hc, hs, wout, g2, wq):
    bsz, d = x.shape
    full = lambda a: pl.BlockSpec(a.shape, lambda i: (0,) * a.ndim)
    args = (x, hm, hc, hs, wout, g2, wq)
    return pl.pallas_call(
        _mix_out_sample_kernel,
        grid=(1,),
        in_specs=[full(x), full(hm), full(hc), full(hs), _weight_spec(wout, l), _weight_spec(g2, l),
                  _weight_spec(wq, l)],
        out_specs=[pl.BlockSpec((bsz, d), lambda i: (0, 0)), pl.BlockSpec((bsz, X_WIDTH), lambda i: (0, 0))],
        out_shape=[jax.ShapeDtypeStruct((bsz, d), F32), jax.ShapeDtypeStruct((bsz, X_WIDTH), F32)],
        compiler_params=_params("arbitrary"),
        name="mix_out_sample",
    )(*args)


def _xattn_sample_kernel(q_ref, k_ref, v_ref, o_ref):
    R = k_ref.shape[0]
    shape = (R, SUBLANES, X_WIDTH)
    row = lax.broadcasted_iota(jnp.int32, shape, 1)
    lane_head = lax.broadcasted_iota(jnp.int32, shape, 2) // HEAD_DIM
    own = row == lane_head
    q8 = jnp.where(own, jnp.broadcast_to(q_ref[...], shape), 0.0).astype(BF16)
    s = jnp.einsum("bqd,bdk->bqk", q8, k_ref[...].astype(BF16), preferred_element_type=F32) * QK_SCALE
    e = jnp.exp(s - jnp.max(s, axis=2, keepdims=True))
    p = (e / jnp.sum(e, axis=2, keepdims=True)).astype(BF16)
    o8 = jnp.einsum("bqk,bdk->bqd", p, v_ref[...].astype(BF16), preferred_element_type=F32)
    o_ref[...] = jnp.sum(jnp.where(own, o8, 0.0), axis=1, keepdims=True)


def _xattn_sample(l, q3, k_all, v_all):
    _, bsz, w, mem = k_all.shape
    R = min(XATTN_SAMPLE_BLOCK, bsz)
    kv = pl.BlockSpec((None, R, w, mem), lambda i: (l, i, 0, 0))
    qo = pl.BlockSpec((R, 1, w), lambda i: (i, 0, 0))
    return pl.pallas_call(
        _xattn_sample_kernel,
        grid=(bsz // R,),
        in_specs=[qo, kv, kv],
        out_specs=qo,
        out_shape=jax.ShapeDtypeStruct((bsz, 1, w), F32),
        compiler_params=_params("arbitrary"),
        name="xattn_sample",
    )(q3, k_all, v_all)


def _ffn_sample_kernel(x1_ref, ox_ref, wo_ref, g3_ref, wup_ref, cw_ref, cb_ref, wdn_ref, gf_ref, hist_ref,
                       *rest, final):
    o_ref, hist_out_ref, act = rest[-3:]
    _copy_earlier_layers(rest[:-3], (hist_out_ref,))
    l = hist_out_ref.shape[0] - 1
    dff = wdn_ref.shape[0]
    x = x1_ref[...] + _dot(ox_ref[...].astype(BF16), wo_ref[...])
    h = _rms(x, g3_ref[...]).astype(BF16)
    for c in range(0, dff, FFN_CHUNK):
        a = _dot(h, wup_ref[:, c:c + FFN_CHUNK])
        g = _dot(h, wup_ref[:, dff + c:dff + c + FFN_CHUNK])
        h1 = hist_ref[:, 1, c:c + FFN_CHUNK]
        gc = (cw_ref[0:1, c:c + FFN_CHUNK] * hist_ref[:, 0, c:c + FFN_CHUNK]
              + cw_ref[1:2, c:c + FFN_CHUNK] * h1
              + cw_ref[2:3, c:c + FFN_CHUNK] * g + cb_ref[:, c:c + FFN_CHUNK])
        act[:, c:c + FFN_CHUNK] = (_swish(gc) * a).astype(BF16)
        hist_out_ref[l, :, 0, c:c + FFN_CHUNK] = h1
        hist_out_ref[l, :, 1, c:c + FFN_CHUNK] = g
    y = x + _dot(act[...], wdn_ref[...])
    o_ref[...] = _rms(y, gf_ref[...]) if final else y


def _ffn_sample(l, x1, ox, wo, g3, wup, cw, cb, wdn, gf, hist_all, prev, *, final):
    bsz, d = x1.shape
    dff = wdn.shape[1]
    full = lambda a: pl.BlockSpec(a.shape, lambda i: (0,) * a.ndim)
    hshape = hist_all.shape[1:]
    return pl.pallas_call(
        functools.partial(_ffn_sample_kernel, final=final),
        grid=(1,),
        in_specs=[full(x1), full(ox), _weight_spec(wo, l), _weight_spec(g3, l), _weight_spec(wup, l),
                  _weight_spec(cw, l), _weight_spec(cb, l),
                  _weight_spec(wdn, l), full(gf), pl.BlockSpec((None,) + hshape, lambda i: (l, 0, 0, 0))]
                 + [full(p) for p in prev],
        out_specs=[pl.BlockSpec((bsz, d), lambda i: (0, 0)), pl.BlockSpec((l + 1,) + hshape, lambda i: (0, 0, 0, 0))],
        out_shape=[jax.ShapeDtypeStruct((bsz, d), F32), jax.ShapeDtypeStruct((l + 1,) + hshape, F32)],
        scratch_shapes=[pltpu.VMEM((bsz, dff), BF16)],
        compiler_params=_params("arbitrary"),
        name="ffn_sample",
    )(x1, ox, wo, g3, wup, cw, cb, wdn, gf, hist_all, *prev)


def _t5_buckets(dist):
    n = np.maximum(dist, 0)
    max_exact = N_BUCKETS // 2
    nf = np.maximum(n, max_exact).astype(np.float32)
    large = max_exact + (np.log(nf / np.float32(max_exact)) / np.float32(math.log(MAX_DISTANCE / max_exact))
                         * np.float32(N_BUCKETS - max_exact)).astype(np.int32)
    return np.where(n < max_exact, n, np.minimum(large, N_BUCKETS - 1))


def _prompt_buckets():
    W = WINDOW
    dist = np.arange(W)[None, :] + W - np.arange(2 * W)[:, None]
    band = (dist >= 0) & (dist < W)
    return np.where(band, _t5_buckets(dist), -1).astype(np.int32)


def _swa_tables(rel_bias):
    W = WINDOW
    dist_c = W - np.arange(W)
    tab = jnp.transpose(rel_bias[_t5_buckets(dist_c)], (1, 0))
    cache_bias = jnp.where((dist_c < W)[None], tab, NEG_INF)
    return cache_bias, rel_bias[0]


def kernel(x_prompt, x_sample, mem_prompt, state_mlstm_C, state_mlstm_n, state_mlstm_m, state_conv, cache_swa_k, cache_swa_v, cache_mem_k, cache_mem_v, state_ffn_conv, rel_bias, norm1_g, w_in, b_i, b_f, mlstm_norm_g, conv_w, conv_b, conv_ln_g, conv_ln_b, swa_sinks, w_out, norm2_g, w_xq, w_xk, w_xv, w_xo, norm3_g, w_up, ffn_conv_w, ffn_conv_b, w_down, final_norm_g):
    depth = w_in.shape[0]
    bp, seq, d = x_prompt.shape
    bs = x_sample.shape[0]
    mem = mem_prompt.shape[1]
    dff = w_down.shape[1]
    W = WINDOW
    nh = C_KERNEL - 1

    xp = x_prompt
    xs = x_sample.reshape(bs, d)
    gf = final_norm_g.reshape(1, d)
    c_all = jnp.transpose(state_mlstm_C, (0, 2, 3, 4, 1))
    n_all = jnp.transpose(state_mlstm_n, (0, 2, 3, 1))
    m_all = jnp.transpose(state_mlstm_m, (0, 2, 1))
    hist_all = jnp.transpose(state_conv, (0, 2, 1, 3))
    kc_all = jnp.transpose(cache_swa_k, (0, 1, 3, 4, 2)).reshape(depth, bs, S_KV_WIDTH, W)
    vc_all = jnp.transpose(cache_swa_v, (0, 1, 3, 4, 2)).reshape(depth, bs, S_KV_WIDTH, W)
    mk_all = jnp.transpose(cache_mem_k, (0, 1, 3, 4, 2)).reshape(depth, bs, X_WIDTH, mem)
    mv_all = jnp.transpose(cache_mem_v, (0, 1, 3, 4, 2)).reshape(depth, bs, X_WIDTH, mem)
    pm_c, pm_n, pm_m, p_conv, p_k, p_v, p_mk, p_mv, p_ffn = ([] for _ in range(9))
    s_mlstm, s_conv, s_kv, s_ffn = [], [], [], []

    win_t = jnp.swapaxes(w_in, 1, 2)
    wout = w_out.astype(BF16)
    wxq = w_xq.astype(BF16)
    wxkv = jnp.concatenate([w_xk, w_xv], axis=2).astype(BF16)
    wxo = w_xo.astype(BF16)
    wup = w_up.astype(BF16)
    wdn = w_down.astype(BF16)
    bucket = jnp.asarray(_prompt_buckets())

    rows = lambda a: a.reshape(depth, 1, -1)
    g1, g2, g3 = rows(norm1_g), rows(norm2_g), rows(norm3_g)
    gb8 = jnp.concatenate([b_i, b_f], axis=1)
    gbias = rows(jnp.concatenate([gb8, jnp.zeros((depth, 128 - 2 * M_HEADS), F32)], axis=1))
    ng = rows(mlstm_norm_g)
    ngt = jnp.broadcast_to(mlstm_norm_g[:, :, None], (depth, M_WIDTH, bs))
    cw = jnp.concatenate([conv_w, jnp.zeros((depth, CONV_PAD - C_KERNEL, C_WIDTH), F32)], axis=1)
    cb, lg, lb = rows(conv_b), rows(conv_ln_g), rows(conv_ln_b)
    fcw = jnp.concatenate([ffn_conv_w, jnp.zeros((depth, SUBLANES - FFN_KERNEL, dff), F32)], axis=1)
    fcb = rows(ffn_conv_b)
    cache_bias, bias0 = _swa_tables(rel_bias)
    aux = jnp.concatenate([jnp.broadcast_to(bias0[None, :, None], (depth, S_Q_HEADS, 1)), swa_sinks[:, :, None],
                           jnp.zeros((depth, S_Q_HEADS, 126), F32)], axis=2)

    for l in range(depth):
        last = l == depth - 1

        mkv = _mem_kv(l, mem_prompt, wxkv)
        z, win = _in_proj(l, xp.reshape(bp * seq, d), g1, win_t, tm=1024)
        z = z.reshape(bp, seq, Z_WIDTH)
        hm, cpair, npair, mm = _mlstm_prompt(l, z, gbias, ng)
        hc, ctail = _conv_prompt(l, z, cw, cb, lg, lb, tc=min(1024, seq))
        hs, kt, vt = _swa_prompt(l, z, rel_bias, swa_sinks, bucket, qb=min(8, seq // W))
        xp = _mix_out_prompt(l, xp, hm, hc, hs, wout, g2, wxq, mkv, wxo, tm=1024)
        xp, ftail = _ffn_prompt(l, xp, g3, wup, fcw, fcb, wdn, gf, tm=512, final=last)
        half = lambda h: slice(HEAD_DIM * (h % 2), HEAD_DIM * (h % 2 + 1))
        pm_c.append(jnp.stack([jnp.swapaxes(cpair[:, h // 2, half(h), half(h)], 1, 2) for h in range(M_HEADS)], axis=1))
        pm_n.append(jnp.stack([npair[:, h // 2, h % 2, half(h)] for h in range(M_HEADS)], axis=1))
        pm_m.append(mm[:, :, 0, 0])
        p_conv.append(ctail[:, CONV_PAD - nh:, :])
        p_k.append(jnp.transpose(kt.reshape(bp, S_KV_HEADS, HEAD_DIM, W), (0, 3, 1, 2)))
        p_v.append(jnp.transpose(vt.reshape(bp, S_KV_HEADS, HEAD_DIM, W), (0, 3, 1, 2)))
        p_mk.append(jnp.transpose(mkv[:, 0:X_WIDTH, :].reshape(bp, X_HEADS, HEAD_DIM, mem), (0, 3, 1, 2)))
        p_mv.append(jnp.transpose(mkv[:, X_WIDTH:, :].reshape(bp, X_HEADS, HEAD_DIM, mem), (0, 3, 1, 2)))
        p_ffn.append(ftail[:, FFN_PAD - (FFN_KERNEL - 1):, :])

        zs, zst = _sample_in(l, xs, g1, win)
        hmt_s, *s_mlstm = _mlstm_sample(l, zst, gb8, ngt, c_all, n_all, m_all, s_mlstm)
        hc_s, *s_conv = _conv_sample(l, zs, hist_all, cw, cb, lg, lb, s_conv)
        hs_s, *s_kv = _swa_sample(l, zs, zs.reshape(bs, 1, Z_WIDTH), kc_all, vc_all, cache_bias, aux, s_kv)
        x1, qx = _mix_out_sample(l, xs, hmt_s, hc_s, hs_s.reshape(bs, S_WIDTH), wout, g2, wxq)
        ox = _xattn_sample(l, qx.reshape(bs, 1, X_WIDTH), mk_all, mv_all)
        xs, *s_ffn = _ffn_sample(l, x1, ox.reshape(bs, X_WIDTH), wxo, g3, wup, fcw, fcb, wdn, gf,
                                 state_ffn_conv, s_ffn, final=last)

    st = jnp.stack
    tr = jnp.transpose
    s_c, s_n, s_m = s_mlstm
    s_k, s_v = (a.reshape(depth, bs, S_KV_HEADS, HEAD_DIM, W) for a in s_kv)
    return (xp, xs.reshape(bs, 1, d),
            st(pm_c), st(pm_n), st(pm_m), st(p_conv), st(p_k), st(p_v), st(p_mk), st(p_mv), st(p_ffn),
            tr(s_c, (0, 4, 1, 2, 3)), tr(s_n, (0, 3, 1, 2)), tr(s_m, (0, 2, 1)),
            tr(s_conv[0], (0, 2, 1, 3)), tr(s_k, (0, 1, 4, 2, 3)), tr(s_v, (0, 1, 4, 2, 3)),
            s_ffn[0])
```

```python
import functools
import math

import numpy as np
import jax
import jax.numpy as jnp
from jax import lax
from jax.experimental import pallas as pl
from jax.experimental.pallas import tpu as pltpu

F32 = jnp.float32
BF16 = jnp.bfloat16
EPS = 1e-6
NEG_INF = float("-inf")

HEAD_DIM = 64
M_HEADS = 4
M_WIDTH = M_HEADS * HEAD_DIM
C_WIDTH = 256
C_KERNEL = 31
S_Q_HEADS = 8
S_KV_HEADS = 2
S_WIDTH = S_Q_HEADS * HEAD_DIM
S_KV_WIDTH = S_KV_HEADS * HEAD_DIM
WINDOW = 128
N_BUCKETS = 32
MAX_DISTANCE = 128
X_HEADS = 4
X_WIDTH = X_HEADS * HEAD_DIM
FFN_KERNEL = 3
QK_SCALE = HEAD_DIM ** -0.5
LOG2E = math.log2(math.e)

Z_M = 0
Z_C = 4 * M_WIDTH
Z_SQ = Z_C + 2 * C_WIDTH
Z_SKV = Z_SQ + S_WIDTH
Z_G = Z_SKV + 2 * S_KV_WIDTH
LANES = 128
SUBLANES = 8
Z_WIDTH = Z_G + LANES
VMEM_LIMIT = 56 * 1024 * 1024

M_CHUNK = 128
FFN_CHUNK = 256


def _params(*sem):
    return pltpu.CompilerParams(dimension_semantics=sem, vmem_limit_bytes=VMEM_LIMIT)


def _const_spec(shape):
    nd = len(shape)
    return pl.BlockSpec(shape, lambda *_: (0,) * nd, pipeline_mode=pl.Buffered(1))


def _weight_spec(w, l):
    nd = w.ndim - 1
    return pl.BlockSpec((None,) + w.shape[1:], lambda *_: (l,) + (0,) * nd, pipeline_mode=pl.Buffered(1))


def _rms(x, g):
    return x * lax.rsqrt(jnp.mean(x * x, axis=-1, keepdims=True) + EPS) * g


def _sigmoid(x):
    return 0.5 * jnp.tanh(0.5 * x) + 0.5


def _swish(x):
    h = 0.5 * x
    return h + h * jnp.tanh(h)


def _log_sigmoid(x):
    return jnp.minimum(x, 0.0) - jnp.log1p(jnp.exp(-jnp.abs(x)))


def _dot(a, b):
    return jnp.dot(a, b, preferred_element_type=F32)


def _dot_nt(a, b):
    return lax.dot_general(a, b, (((1,), (1,)), ((), ())), preferred_element_type=F32)


def _dot_tn(a, b):
    return lax.dot_general(a, b, (((0,), (0,)), ((), ())), preferred_element_type=F32)


IN_PROJ_ROWS = 512
IN_PROJ_COLS = 512
IN_GATES = 4 * M_WIDTH
W_PREP_ROWS = 256


def _in_proj_kernel(x_ref, g_ref, wt_ref, o_ref, wprep_ref, w_scr):
    d = x_ref.shape[1]

    @pl.when(pl.program_id(0) == 0)
    def _():
        n_gate = 2 * M_HEADS
        for src, dst, n in ((0, 0, IN_GATES), (IN_GATES + n_gate, IN_GATES, Z_G - IN_GATES)):
            for c in range(0, n, W_PREP_ROWS):
                w_scr[:, dst + c:dst + c + W_PREP_ROWS] = wt_ref[src + c:src + c + W_PREP_ROWS, :].T.astype(BF16)
        gate_rows = jnp.concatenate([wt_ref[IN_GATES:IN_GATES + n_gate, :], jnp.zeros((LANES - n_gate, d), F32)],
                                    axis=0)
        w_scr[:, Z_G:Z_G + LANES] = gate_rows.T.astype(BF16)
        wprep_ref[...] = w_scr[...]

    tm, n = o_ref.shape
    sub = min(IN_PROJ_ROWS, tm)
    for r0 in range(0, tm, sub):
        h = _rms(x_ref[r0:r0 + sub, :], g_ref[...]).astype(BF16)
        for c in range(0, n, IN_PROJ_COLS):
            w = min(IN_PROJ_COLS, n - c)
            o_ref[r0:r0 + sub, c:c + w] = _dot(h, w_scr[:, c:c + w])


def _in_proj(l, x, g, wt, *, tm):
    m, d = x.shape
    tm = min(tm, m)
    return pl.pallas_call(
        _in_proj_kernel,
        grid=(m // tm,),
        in_specs=[pl.BlockSpec((tm, d), lambda i: (i, 0)), _weight_spec(g, l), _weight_spec(wt, l)],
        out_specs=[pl.BlockSpec((tm, Z_WIDTH), lambda i: (i, 0)), pl.BlockSpec((d, Z_WIDTH), lambda i: (0, 0))],
        out_shape=[jax.ShapeDtypeStruct((m, Z_WIDTH), F32), jax.ShapeDtypeStruct((d, Z_WIDTH), BF16)],
        scratch_shapes=[pltpu.VMEM((d, Z_WIDTH), BF16)],
        compiler_params=_params("arbitrary"),
        name="in_proj",
    )(x, g, wt)


def _mem_kv_kernel(x_ref, w_ref, o_ref, acc):
    acc[...] = _dot(x_ref[...].astype(BF16), w_ref[...])
    o_ref[...] = acc[...].T


def _mem_kv(l, mem3, w):
    b, mem, d = mem3.shape
    n = w.shape[2]
    return pl.pallas_call(
        _mem_kv_kernel,
        grid=(b,),
        in_specs=[pl.BlockSpec((None, mem, d), lambda i: (i, 0, 0)), _weight_spec(w, l)],
        out_specs=pl.BlockSpec((None, n, mem), lambda i: (i, 0, 0)),
        out_shape=jax.ShapeDtypeStruct((b, n, mem), F32),
        scratch_shapes=[pltpu.VMEM((mem, n), F32)],
        compiler_params=_params("arbitrary"),
        name="mem_kv",
    )(mem3, w)


def _sample_in_kernel(x_ref, g_ref, w_ref, z_ref, zt_ref):
    h = _rms(x_ref[...], g_ref[...]).astype(BF16)
    n = z_ref.shape[1]
    for c in range(0, n, IN_PROJ_COLS):
        w = min(IN_PROJ_COLS, n - c)
        zc = _dot(h, w_ref[:, c:c + w])
        z_ref[:, c:c + w] = zc
        zt_ref[c:c + w, :] = zc.T


def _sample_in(l, x, g, w):
    m, d = x.shape
    n = w.shape[1]
    full = lambda shape: pl.BlockSpec(shape, lambda i: (0,) * len(shape))
    return pl.pallas_call(
        _sample_in_kernel,
        grid=(1,),
        in_specs=[full((m, d)), _weight_spec(g, l), full((d, n))],
        out_specs=[full((m, n)), full((n, m))],
        out_shape=[jax.ShapeDtypeStruct((m, n), F32), jax.ShapeDtypeStruct((n, m), F32)],
        compiler_params=_params("arbitrary"),
        name="sample_in",
    )(x, g, w)


def _mlstm_prompt_kernel(z_ref, g_ref, gb_ref, ng_ref, h_ref, cp_ref, np_ref, m_ref,
                         cp_scr, np_scr, m_scr, ht_scr, st_scr, kw_scr):
    NB, L = z_ref.shape[0], z_ref.shape[1]
    D = HEAD_DIM
    hi = lax.Precision.HIGHEST

    @pl.when(pl.program_id(0) == 0)
    def _():
        cp_scr[...] = jnp.zeros(cp_scr.shape, F32)
        np_scr[...] = jnp.zeros(np_scr.shape, F32)
        m_scr[...] = jnp.zeros(m_scr.shape, F32)

    src = lax.broadcasted_iota(jnp.int32, (L, L), 0)
    qry = lax.broadcasted_iota(jnp.int32, (L, L), 1)
    causal_t = src <= qry
    upper = jnp.where(causal_t, 1.0, 0.0)
    lane_half = lax.broadcasted_iota(jnp.int32, (L, 128), 1) // D
    row8 = lax.broadcasted_iota(jnp.int32, (SUBLANES, 128), 0)

    rows, cols = {}, []
    for b in range(NB):
        g_t = (g_ref[b] + gb_ref[...]).T[0:SUBLANES, :]
        b_rows = jnp.dot(_log_sigmoid(g_t), upper, precision=hi, preferred_element_type=F32)
        to_cols = []
        for h in range(M_HEADS):
            b_row = b_rows[M_HEADS + h:M_HEADS + h + 1, :]
            ci_row = g_t[h:h + 1, :] - b_row
            m_prev = m_scr[b, h, 0:1, 0:1]
            b_last = b_row[:, L - 1:L]
            m_new = jnp.maximum(b_last + m_prev, jnp.max(b_last + ci_row, axis=1, keepdims=True))
            rows[b, h] = dict(b_row=b_row, m_prev=m_prev, m_new=m_new, decay=jnp.exp(b_last + m_prev - m_new))
            to_cols += [ci_row, jnp.exp(b_last + ci_row - m_new)]
        to_cols.append(jnp.zeros((128 - 2 * M_HEADS, L), F32))
        cols.append(jnp.concatenate(to_cols, axis=0).T)

    stats = {}
    for b in range(NB):
        for j in range(M_HEADS // 2):
            slab = 128 * j
            q2 = z_ref[b, :, slab:slab + 128].astype(BF16)
            k2 = z_ref[b, :, M_WIDTH + slab:M_WIDTH + slab + 128] * QK_SCALE
            for par in range(2):
                h = 2 * j + par
                r = rows[b, h]
                ci_col = cols[b][:, 2 * h:2 * h + 1]
                a_row = r["b_row"] + r["m_prev"]
                dm = jnp.where(causal_t, r["b_row"] + ci_col, NEG_INF)
                m_row = jnp.maximum(a_row, jnp.max(dm, axis=0, keepdims=True))
                k_own = jnp.where(lane_half == par, k2, 0.0).astype(BF16)
                s_t = _dot_nt(k_own, q2) * jnp.exp(dm - m_row)
                st_scr[b, j, :, L * par:L * (par + 1)] = s_t.astype(BF16)
                stats[b, h] = dict(w_inter=jnp.exp(a_row - m_row), floor=jnp.exp(-m_row),
                                   den_s=jnp.sum(s_t, axis=0, keepdims=True),
                                   decay=r["decay"], m_new=r["m_new"])
            wk = jnp.where(lane_half == 0, cols[b][:, 4 * j + 1:4 * j + 2], cols[b][:, 4 * j + 3:4 * j + 4])
            kw = k2 * wk
            kw_scr[b, j] = kw.astype(BF16)
            stats[b, j, "k_sum"] = jnp.sum(kw, axis=0, keepdims=True)

    block_diag = (lax.broadcasted_iota(jnp.int32, (128, 128), 0) // D
                  == lax.broadcasted_iota(jnp.int32, (128, 128), 1) // D)
    for b in range(NB):
        for j in range(M_HEADS // 2):
            slab = 128 * j
            q2 = z_ref[b, :, slab:slab + 128].astype(BF16)
            v2 = z_ref[b, :, 2 * M_WIDTH + slab:2 * M_WIDTH + slab + 128].astype(BF16)
            cp = cp_scr[b, j]
            npair = np_scr[b, j]
            qc = _dot_nt(cp.astype(BF16), q2)
            qn = _dot_nt(npair.astype(BF16), q2)
            pv = _dot_tn(v2, st_scr[b, j])
            upd = jnp.where(block_diag, _dot_tn(v2, kw_scr[b, j]), 0.0)
            n_new = jnp.zeros((SUBLANES, 128), F32)
            for par in range(2):
                h = 2 * j + par
                st = stats[b, h]
                hr = slice(D * par, D * (par + 1))
                num = st["w_inter"] * qc[hr, :] + pv[hr, L * par:L * (par + 1)]
                den = st["w_inter"] * qn[par:par + 1, :] + st["den_s"]
                hh = num * (1.0 / jnp.maximum(jnp.abs(den), st["floor"]))
                hh = hh * lax.rsqrt(jnp.mean(hh * hh, axis=0, keepdims=True) + EPS)
                ht_scr[b, D * h:D * (h + 1), :] = hh
                cp_scr[b, j, hr, :] = st["decay"] * cp[hr, :] + upd[hr, :]
                k_sum = jnp.where(lane_half[0:1, :] == par, stats[b, j, "k_sum"], 0.0)
                n_new = jnp.where(row8 == par, st["decay"] * npair[par:par + 1, :] + k_sum, n_new)
                m_scr[b, h] = jnp.broadcast_to(st["m_new"], m_scr.shape[2:])
            np_scr[b, j] = n_new
        o_gate = _sigmoid(z_ref[b, :, 3 * M_WIDTH:4 * M_WIDTH])
        h_ref[b] = ht_scr[b].T * ng_ref[...] * o_gate

    cp_ref[...] = cp_scr[...]
    np_ref[...] = np_scr[...]
    m_ref[...] = m_scr[...]


def _mlstm_prompt(l, z3, gbias, ng):
    b, s, _ = z3.shape
    L = M_CHUNK
    P = M_HEADS // 2
    return pl.pallas_call(
        _mlstm_prompt_kernel,
        grid=(s // L,),
        in_specs=[
            pl.BlockSpec((b, L, 4 * M_WIDTH), lambda c: (0, c, Z_M // (4 * M_WIDTH))),
            pl.BlockSpec((b, L, 128), lambda c: (0, c, Z_G // 128)),
            _weight_spec(gbias, l),
            _weight_spec(ng, l),
        ],
        out_specs=[
            pl.BlockSpec((b, L, M_WIDTH), lambda c: (0, c, 0)),
            pl.BlockSpec((b, P, 128, 128), lambda c: (0, 0, 0, 0)),
            pl.BlockSpec((b, P, SUBLANES, 128), lambda c: (0, 0, 0, 0)),
            pl.BlockSpec((b, M_HEADS, SUBLANES, 128), lambda c: (0, 0, 0, 0)),
        ],
        out_shape=[
            jax.ShapeDtypeStruct((b, s, M_WIDTH), F32),
            jax.ShapeDtypeStruct((b, P, 128, 128), F32),
            jax.ShapeDtypeStruct((b, P, SUBLANES, 128), F32),
            jax.ShapeDtypeStruct((b, M_HEADS, SUBLANES, 128), F32),
        ],
        scratch_shapes=[
            pltpu.VMEM((b, P, 128, 128), F32),
            pltpu.VMEM((b, P, SUBLANES, 128), F32),
            pltpu.VMEM((b, M_HEADS, SUBLANES, 128), F32),
            pltpu.VMEM((b, M_WIDTH, L), F32),
            pltpu.VMEM((b, P, L, 2 * L), BF16),
            pltpu.VMEM((b, P, L, 128), BF16),
        ],
        compiler_params=_params("arbitrary"),
        name="mlstm_prompt",
    )(z3, z3, gbias, ng)


CONV_PAD = 32


def _conv_prompt_kernel(z_ref, w_ref, cb_ref, lg_ref, lb_ref, h_ref, tail_ref, buf, shifted):
    tc = z_ref.shape[0]
    t = pl.program_id(1)

    @pl.when(t == 0)
    def _():
        buf[0:CONV_PAD, :] = jnp.zeros((CONV_PAD, C_WIDTH), F32)

    @pl.when(t > 0)
    def _():
        buf[0:CONV_PAD, :] = buf[tc:tc + CONV_PAD, :]

    u = z_ref[:, 0:C_WIDTH] * _sigmoid(z_ref[:, C_WIDTH:2 * C_WIDTH])
    buf[CONV_PAD:CONV_PAD + tc, :] = u
    n_sh = shifted.shape[1]
    for r in range(1, SUBLANES):
        shifted[r - 1] = buf[r:r + n_sh, :]
    off = CONV_PAD - (C_KERNEL - 1)
    acc = jnp.broadcast_to(cb_ref[...], (tc, C_WIDTH))
    for j in range(C_KERNEL):
        r = (off + j) % SUBLANES
        base = off + j - r
        win = buf[base:base + tc, :] if r == 0 else shifted[r - 1, base:base + tc, :]
        acc = acc + w_ref[j:j + 1, :] * win
    mu = jnp.mean(acc, axis=-1, keepdims=True)
    xc = acc - mu
    y = xc * lax.rsqrt(jnp.mean(xc * xc, axis=-1, keepdims=True) + EPS) * lg_ref[...] + lb_ref[...]
    h_ref[...] = _swish(y)
    tail_ref[...] = buf[tc:tc + CONV_PAD, :]


def _conv_prompt(l, z3, w, cb, lg, lb, *, tc):
    b, s, _ = z3.shape
    return pl.pallas_call(
        _conv_prompt_kernel,
        grid=(b, s // tc),
        in_specs=[
            pl.BlockSpec((None, tc, 2 * C_WIDTH), lambda i, t: (i, t, Z_C // (2 * C_WIDTH))),
            _weight_spec(w, l), _weight_spec(cb, l), _weight_spec(lg, l), _weight_spec(lb, l),
        ],
        out_specs=[
            pl.BlockSpec((None, tc, C_WIDTH), lambda i, t: (i, t, 0)),
            pl.BlockSpec((None, CONV_PAD, C_WIDTH), lambda i, t: (i, 0, 0)),
        ],
        out_shape=[
            jax.ShapeDtypeStruct((b, s, C_WIDTH), F32),
            jax.ShapeDtypeStruct((b, CONV_PAD, C_WIDTH), F32),
        ],
        scratch_shapes=[pltpu.VMEM((CONV_PAD + tc, C_WIDTH), F32),
                        pltpu.VMEM((SUBLANES - 1, CONV_PAD + tc - SUBLANES, C_WIDTH), F32)],
        compiler_params=_params("arbitrary", "arbitrary"),
        name="conv_prompt",
    )(z3, w, cb, lg, lb)


def _swa_prompt_kernel(rb_ref, sink_ref, bucket_ref, q_ref, kv_ref, kvp_ref, o_ref, kt_ref, vt_ref,
                       bias_scr, s_scr, p_scr, ot_scr, *, layer):
    W = WINDOW
    H = S_Q_HEADS
    G = H // S_KV_HEADS
    i = pl.program_id(0)
    n = pl.program_id(1)
    from_prev = lax.broadcasted_iota(jnp.int32, (W, W), 0) > lax.broadcasted_iota(jnp.int32, (W, W), 1)

    @pl.when(jnp.logical_and(i == 0, n == 0))
    def _():
        bucket = bucket_ref[...]
        for h in range(H):
            acc = jnp.full((2 * W, W), NEG_INF, F32)
            for b in range(N_BUCKETS):
                acc = jnp.where(bucket == b, rb_ref[b, h], acc)
            acc = acc * LOG2E
            bias_scr[0, h] = jnp.where(from_prev, acc[0:W, :], acc[W:2 * W, :])
            bias_scr[1, h] = jnp.where(from_prev, NEG_INF, acc[W:2 * W, :])

    QB = q_ref.shape[0] // W
    NK = (QB + 1) * W
    first = jnp.where(n == 0, 1, 0)
    kk = jnp.concatenate([kvp_ref[:, 0:S_KV_WIDTH], kv_ref[:, 0:S_KV_WIDTH]], axis=0) * (QK_SCALE * LOG2E)
    vv = jnp.concatenate([kvp_ref[:, S_KV_WIDTH:2 * S_KV_WIDTH], kv_ref[:, S_KV_WIDTH:2 * S_KV_WIDTH]], axis=0)
    kk_r = pltpu.roll(kk, HEAD_DIM, axis=1)
    lo_lane = lax.broadcasted_iota(jnp.int32, (NK, S_KV_WIDTH), 1) < HEAD_DIM
    k_var = [[jnp.where(lo_lane, kk, 0.0).astype(BF16), jnp.where(lo_lane, 0.0, kk_r).astype(BF16)],
             [jnp.where(lo_lane, kk_r, 0.0).astype(BF16), jnp.where(lo_lane, 0.0, kk).astype(BF16)]]
    v_t = vv.T.astype(BF16)

    for j in range(QB):
        k0 = j * W
        masked = first if j == 0 else 0
        m_rows = [None] * H
        for hk in range(S_KV_HEADS):
            c0 = 2 * 128 * hk
            q_st = jnp.concatenate([q_ref[k0:k0 + W, c0:c0 + 128], q_ref[k0:k0 + W, c0 + 128:c0 + 256]],
                                   axis=0).astype(BF16)
            for half in range(2):
                s_t = _dot_nt(k_var[hk][half][k0:k0 + 2 * W, :], q_st)
                for slab in range(2):
                    head = G * hk + 2 * slab + half
                    lanes = slice(128 * slab, 128 * (slab + 1))
                    sb = jnp.where(from_prev, s_t[0:W, lanes], s_t[W:2 * W, lanes]) + bias_scr[masked, head]
                    s_scr[j, head] = sb
                    m_rows[head] = jnp.maximum(jnp.max(sb, axis=0, keepdims=True), sink_ref[layer, head] * LOG2E)

        inv = [None] * H
        for head in range(H):
            e = jnp.exp2(s_scr[j, head] - m_rows[head])
            den = jnp.sum(e, axis=0, keepdims=True) + jnp.exp2(sink_ref[layer, head] * LOG2E - m_rows[head])
            inv[head] = 1.0 / den
            p_scr[j, 0:W, 128 * head:128 * (head + 1)] = jnp.where(from_prev, e, 0.0).astype(BF16)
            p_scr[j, W:2 * W, 128 * head:128 * (head + 1)] = jnp.where(from_prev, 0.0, e).astype(BF16)

        for hk in range(S_KV_HEADS):
            o_t = _dot(v_t[HEAD_DIM * hk:HEAD_DIM * (hk + 1), k0:k0 + 2 * W],
                       p_scr[j, :, 128 * G * hk:128 * G * (hk + 1)])
            for g in range(G):
                head = G * hk + g
                ot_scr[j, HEAD_DIM * head:HEAD_DIM * (head + 1), :] = o_t[:, 128 * g:128 * (g + 1)] * inv[head]
        o_ref[k0:k0 + W, :] = ot_scr[j].T

    @pl.when(n == pl.num_programs(1) - 1)
    def _():
        kt_ref[...] = kv_ref[(QB - 1) * W:QB * W, 0:S_KV_WIDTH].T
        vt_ref[...] = kv_ref[(QB - 1) * W:QB * W, S_KV_WIDTH:2 * S_KV_WIDTH].T


def _swa_prompt(l, z3, rel_bias, sinks, bucket, *, qb):
    b, s, _ = z3.shape
    W = WINDOW
    smem = pl.BlockSpec(memory_space=pltpu.SMEM)
    return pl.pallas_call(
        functools.partial(_swa_prompt_kernel, layer=l),
        grid=(b, s // (qb * W)),
        in_specs=[
            smem, smem, _const_spec((2 * W, W)),
            pl.BlockSpec((None, qb * W, S_WIDTH), lambda i, n: (i, n, Z_SQ // S_WIDTH)),
            pl.BlockSpec((None, qb * W, 2 * S_KV_WIDTH), lambda i, n: (i, n, Z_SKV // (2 * S_KV_WIDTH))),
            pl.BlockSpec((None, W, 2 * S_KV_WIDTH),
                         lambda i, n: (i, jnp.maximum(n * qb - 1, 0), Z_SKV // (2 * S_KV_WIDTH))),
        ],
        out_specs=[
            pl.BlockSpec((None, qb * W, S_WIDTH), lambda i, n: (i, n, 0)),
            pl.BlockSpec((None, S_KV_WIDTH, W), lambda i, n: (i, 0, 0)),
            pl.BlockSpec((None, S_KV_WIDTH, W), lambda i, n: (i, 0, 0)),
        ],
        out_shape=[
            jax.ShapeDtypeStruct((b, s, S_WIDTH), F32),
            jax.ShapeDtypeStruct((b, S_KV_WIDTH, W), F32),
            jax.ShapeDtypeStruct((b, S_KV_WIDTH, W), F32),
        ],
        scratch_shapes=[
            pltpu.VMEM((2, S_Q_HEADS, W, W), F32),
            pltpu.VMEM((qb, S_Q_HEADS, W, W), F32),
            pltpu.VMEM((qb, 2 * W, S_Q_HEADS * W), BF16),
            pltpu.VMEM((qb, S_WIDTH, W), F32),
        ],
        compiler_params=_params("arbitrary", "arbitrary"),
        name="swa_prompt",
    )(rel_bias, sinks, bucket, z3, z3, z3)


MIX_SUB = 512


def _mix_out_prompt_kernel(x_ref, hm_ref, hc_ref, hs_ref, wout_ref, g2_ref, wq_ref, mk_ref, mv_ref, wo_ref, o_ref,
                           x1_scr, qx_scr, p_scr):
    tm = x_ref.shape[0]
    mem = mk_ref.shape[1]
    mk_t = mk_ref[...]
    mv_t = mv_ref[...]
    row_head = lax.broadcasted_iota(jnp.int32, mk_t.shape, 0) // HEAD_DIM
    k_heads = [jnp.where(row_head == h, mk_t, 0.0).astype(BF16) for h in range(X_HEADS)]
    v_cat = jnp.concatenate([jnp.where(row_head == h, mv_t, 0.0).astype(BF16) for h in range(X_HEADS)], axis=1)
    subs = [slice(r0, r0 + MIX_SUB) for r0 in range(0, tm, MIX_SUB)]
    for rows in subs:
        cat = jnp.concatenate([hm_ref[rows, :].astype(BF16), hc_ref[rows, :].astype(BF16),
                               hs_ref[rows, :].astype(BF16)], axis=1)
        x1 = x_ref[rows, :] + _dot(cat, wout_ref[...])
        x1_scr[rows, :] = x1
        qx = _dot(_rms(x1, g2_ref[...]).astype(BF16), wq_ref[...])
        qx_scr[rows, :] = (qx * (QK_SCALE * LOG2E)).astype(BF16)
    for rows in subs:
        qx = qx_scr[rows, :]
        for h in range(X_HEADS):
            s = _dot(qx, k_heads[h])
            e = jnp.exp2(s - jnp.max(s, axis=1, keepdims=True))
            p_scr[rows, mem * h:mem * (h + 1)] = (e * (1.0 / jnp.sum(e, axis=1, keepdims=True))).astype(BF16)
    for rows in subs:
        o = _dot_nt(p_scr[rows, :], v_cat)
        o_ref[rows, :] = x1_scr[rows, :] + _dot(o.astype(BF16), wo_ref[...])


def _mix_out_prompt(l, x3, hm, hc, hs, wout, g2, wq, mkv, wo, *, tm):
    b, s, d = x3.shape
    tm = min(tm, s)
    mem = mkv.shape[2]
    row = lambda w: pl.BlockSpec((None, tm, w), lambda i, t: (i, t, 0))
    return pl.pallas_call(
        _mix_out_prompt_kernel,
        grid=(b, s // tm),
        in_specs=[
            row(d), row(M_WIDTH), row(C_WIDTH), row(S_WIDTH),
            _weight_spec(wout, l), _weight_spec(g2, l), _weight_spec(wq, l),
            pl.BlockSpec((None, X_WIDTH, mem), lambda i, t: (i, 0, 0)),
            pl.BlockSpec((None, X_WIDTH, mem), lambda i, t: (i, 1, 0)),
            _weight_spec(wo, l),
        ],
        out_specs=row(d),
        out_shape=jax.ShapeDtypeStruct((b, s, d), F32),
        scratch_shapes=[pltpu.VMEM((tm, d), F32), pltpu.VMEM((tm, X_WIDTH), BF16),
                        pltpu.VMEM((tm, X_HEADS * mem), BF16)],
        compiler_params=_params("arbitrary", "arbitrary"),
        name="mix_out_prompt",
    )(x3, hm, hc, hs, wout, g2, wq, mkv, mkv, wo)


FFN_PAD = 8


def _ffn_prompt_kernel(x_ref, g3_ref, wup_ref, cw_ref, cb_ref, wdn_ref, gf_ref, o_ref, tail_ref,
                       carry, act, *, final):
    tm = x_ref.shape[0]
    dff = wdn_ref.shape[0]

    @pl.when(pl.program_id(1) == 0)
    def _():
        carry[...] = jnp.zeros(carry.shape, F32)

    x = x_ref[...]
    h = _rms(x, g3_ref[...]).astype(BF16)
    w = FFN_CHUNK
    row8 = lax.broadcasted_iota(jnp.int32, (FFN_PAD, w), 0)
    for c in range(0, dff, w):
        a = _dot(h, wup_ref[:, c:c + w])
        g = _dot(h, wup_ref[:, dff + c:dff + c + w])
        prev = carry[:, c:c + w]
        carry[:, c:c + w] = g[tm - FFN_PAD:tm, :]
        taps = []
        for s in (2, 1):
            gs = pltpu.roll(g, s, axis=0)
            head = jnp.where(row8 < s, pltpu.roll(prev, s, axis=0), gs[0:FFN_PAD, :])
            taps.append(jnp.concatenate([head, gs[FFN_PAD:, :]], axis=0))
        gc = (cw_ref[0:1, c:c + w] * taps[0] + cw_ref[1:2, c:c + w] * taps[1]
              + cw_ref[2:3, c:c + w] * g + cb_ref[:, c:c + w])
        act[:, c:c + w] = (_swish(gc) * a).astype(BF16)
    y = x + _dot(act[...], wdn_ref[...])
    o_ref[...] = _rms(y, gf_ref[...]) if final else y
    tail_ref[...] = carry[...]


def _ffn_prompt(l, x3, g3, wup, cw, cb, wdn, gf, *, tm, final):
    b, s, d = x3.shape
    tm = min(tm, s)
    dff = wdn.shape[1]
    return pl.pallas_call(
        functools.partial(_ffn_prompt_kernel, final=final),
        grid=(b, s // tm),
        in_specs=[
            pl.BlockSpec((None, tm, d), lambda i, t: (i, t, 0)),
            _weight_spec(g3, l), _weight_spec(wup, l), _weight_spec(cw, l), _weight_spec(cb, l),
            _weight_spec(wdn, l), _const_spec((1, d)),
        ],
        out_specs=[
            pl.BlockSpec((None, tm, d), lambda i, t: (i, t, 0)),
            pl.BlockSpec((None, FFN_PAD, dff), lambda i, t: (i, 0, 0)),
        ],
        out_shape=[jax.ShapeDtypeStruct((b, s, d), F32), jax.ShapeDtypeStruct((b, FFN_PAD, dff), F32)],
        scratch_shapes=[
            pltpu.VMEM((FFN_PAD, dff), F32),
            pltpu.VMEM((tm, dff), BF16),
        ],
        compiler_params=_params("arbitrary", "arbitrary"),
        name="ffn_prompt",
    )(x3, g3, wup, cw, cb, wdn, gf)


SWA_SAMPLE_BLOCK = 32
XATTN_SAMPLE_BLOCK = 16
CONV_SAMPLE_BLOCK = 32


def _mlstm_sample_kernel(gb_ref, q_ref, k_ref, v_ref, o_ref, g_ref, ng_ref, c_ref, n_ref, m_ref, *rest, layer):
    h_ref, co_ref, no_ref, mo_ref, kw_scr = rest[-5:]
    _copy_earlier_layers(rest[:-5], (co_ref, no_ref, mo_ref))
    h = pl.program_id(0)
    i_pre = g_ref[pl.ds(h, 1), :] + gb_ref[layer, h]
    f_pre = g_ref[pl.ds(M_HEADS + h, 1), :] + gb_ref[layer, M_HEADS + h]
    a = _log_sigmoid(f_pre) + m_ref[pl.ds(h, 1), :]
    m_t = jnp.maximum(a, i_pre)
    w_old = jnp.exp(a - m_t)
    w_new = jnp.exp(i_pre - m_t)
    q = q_ref[...]
    k = k_ref[...] * QK_SCALE
    v = v_ref[...]
    n_old = n_ref[...]
    kw_scr[...] = k * w_new

    def body(d, acc):
        c_old = c_ref[d]
        co_ref[layer, d] = w_old * c_old + kw_scr[pl.ds(d, 1), :] * v
        return acc + q_ref[pl.ds(d, 1), :] * c_old

    qc = lax.fori_loop(0, HEAD_DIM, body, jnp.zeros(v.shape, F32), unroll=8)
    s = jnp.sum(q * k, axis=0, keepdims=True) * w_new
    num = w_old * qc + s * v
    den = w_old * jnp.sum(q * n_old, axis=0, keepdims=True) + s
    hh = num / jnp.maximum(jnp.abs(den), jnp.exp(-m_t))
    hh = hh * lax.rsqrt(jnp.mean(hh * hh, axis=0, keepdims=True) + EPS) * ng_ref[...]
    h_ref[...] = hh * _sigmoid(o_ref[...])
    no_ref[layer] = w_old * n_old + kw_scr[...]
    mo_ref[layer, pl.ds(h, 1), :] = m_t


def _mlstm_sample(l, zt, gb, ngt, c_all, n_all, m_all, prev):
    bsz = zt.shape[1]
    D = HEAD_DIM
    feat = lambda off: pl.BlockSpec((D, bsz), lambda h: (off // D + h, 0))
    c_spec = lambda n: pl.BlockSpec((n, None, D, D, bsz), lambda h: (0, h, 0, 0, 0))
    n_spec = lambda n: pl.BlockSpec((n, None, D, bsz), lambda h: (0, h, 0, 0))
    m_spec = lambda n: pl.BlockSpec((n, M_HEADS, bsz), lambda h: (0, 0, 0))
    prev_specs = [c_spec(l), n_spec(l), m_spec(l)] if prev else []
    return pl.pallas_call(
        functools.partial(_mlstm_sample_kernel, layer=l),
        grid=(M_HEADS,),
        in_specs=[
            pl.BlockSpec(memory_space=pltpu.SMEM),
            feat(Z_M), feat(Z_M + M_WIDTH), feat(Z_M + 2 * M_WIDTH), feat(Z_M + 3 * M_WIDTH),
            pl.BlockSpec((SUBLANES, bsz), lambda h: (Z_G // SUBLANES, 0)),
            pl.BlockSpec((None, D, bsz), lambda h: (l, h, 0)),
            pl.BlockSpec((None, None, D, D, bsz), lambda h: (l, h, 0, 0, 0)),
            pl.BlockSpec((None, None, D, bsz), lambda h: (l, h, 0, 0)),
            pl.BlockSpec((None, M_HEADS, bsz), lambda h: (l, 0, 0)),
        ] + prev_specs,
        out_specs=[pl.BlockSpec((D, bsz), lambda h: (h, 0)), c_spec(l + 1), n_spec(l + 1), m_spec(l + 1)],
        out_shape=[
            jax.ShapeDtypeStruct((M_WIDTH, bsz), F32),
            jax.ShapeDtypeStruct((l + 1, M_HEADS, D, D, bsz), F32),
            jax.ShapeDtypeStruct((l + 1, M_HEADS, D, bsz), F32),
            jax.ShapeDtypeStruct((l + 1, M_HEADS, bsz), F32),
        ],
        scratch_shapes=[pltpu.VMEM((D, bsz), F32)],
        compiler_params=_params("arbitrary"),
        name="mlstm_sample",
    )(gb, zt, zt, zt, zt, zt, ngt, c_all, n_all, m_all, *prev)


def _conv_sample_kernel(z_ref, hist_ref, w_ref, cb_ref, lg_ref, lb_ref, *rest):
    h_ref, hist_out_ref = rest[-2:]
    _copy_earlier_layers(rest[:-2], (hist_out_ref,))
    l = hist_out_ref.shape[0] - 1
    nh = C_KERNEL - 1
    u = z_ref[:, 0:C_WIDTH] * _sigmoid(z_ref[:, C_WIDTH:2 * C_WIDTH])
    acc = cb_ref[...] + w_ref[nh:nh + 1, :] * u
    for j in range(nh):
        acc = acc + w_ref[j:j + 1, :] * hist_ref[j]
    mu = jnp.mean(acc, axis=-1, keepdims=True)
    xc = acc - mu
    y = xc * lax.rsqrt(jnp.mean(xc * xc, axis=-1, keepdims=True) + EPS) * lg_ref[...] + lb_ref[...]
    h_ref[...] = _swish(y)
    for j in range(nh - 1):
        hist_out_ref[l, j] = hist_ref[j + 1]
    hist_out_ref[l, nh - 1] = u


def _conv_sample(l, z, hist_all, w, cb, lg, lb, prev):
    bsz = z.shape[0]
    nh = hist_all.shape[1]
    R = min(CONV_SAMPLE_BLOCK, bsz)
    hist_spec = lambda n: pl.BlockSpec((n, nh, R, C_WIDTH), lambda i: (0, 0, i, 0))
    return pl.pallas_call(
        _conv_sample_kernel,
        grid=(bsz // R,),
        in_specs=[
            pl.BlockSpec((R, 2 * C_WIDTH), lambda i: (i, Z_C // (2 * C_WIDTH))),
            pl.BlockSpec((None, nh, R, C_WIDTH), lambda i: (l, 0, i, 0)),
            _weight_spec(w, l), _weight_spec(cb, l), _weight_spec(lg, l), _weight_spec(lb, l),
        ] + [hist_spec(l)] * len(prev),
        out_specs=[pl.BlockSpec((R, C_WIDTH), lambda i: (i, 0)), hist_spec(l + 1)],
        out_shape=[jax.ShapeDtypeStruct((bsz, C_WIDTH), F32), jax.ShapeDtypeStruct((l + 1, nh, bsz, C_WIDTH), F32)],
        compiler_params=_params("arbitrary"),
        name="conv_sample",
    )(z, hist_all, w, cb, lg, lb, *prev)


def _copy_earlier_layers(prev_refs, out_refs):
    for p_ref, o_ref in zip(prev_refs, out_refs):
        o_ref[0:p_ref.shape[0]] = p_ref[...]


def _swa_sample_kernel(q_ref, kvn_ref, kv2_ref, kc_ref, vc_ref, bias_ref, aux_ref, *rest):
    o_ref, ko_ref, vo_ref = rest[-3:]
    _copy_earlier_layers(rest[:-3], (ko_ref, vo_ref))
    l = ko_ref.shape[0] - 1
    R = q_ref.shape[0]
    W = kc_ref.shape[2]
    H = S_Q_HEADS
    shape = (R, H, 128)
    row = lax.broadcasted_iota(jnp.int32, shape, 1)
    lane_half = lax.broadcasted_iota(jnp.int32, shape, 2) // HEAD_DIM
    q_half = row % 2
    kv_head = row // (H // S_KV_HEADS)
    qs = jnp.zeros(shape, F32)
    for j in range(H // 2):
        qs = jnp.where(row // 2 == j, q_ref[:, :, 128 * j:128 * (j + 1)], qs)
    q8 = jnp.where(lane_half == kv_head, jnp.where(q_half == kv_head, qs, pltpu.roll(qs, HEAD_DIM, axis=2)), 0.0)
    k_new = kvn_ref[:, :, 0:S_KV_WIDTH]
    v_new = kvn_ref[:, :, S_KV_WIDTH:2 * S_KV_WIDTH]
    s = jnp.einsum("bqd,bdk->bqk", q8.astype(BF16), kc_ref[...].astype(BF16), preferred_element_type=F32) * QK_SCALE
    s = s + bias_ref[...][None]
    s_new = jnp.sum(q8 * k_new, axis=2, keepdims=True) * QK_SCALE + aux_ref[:, 0:1][None]
    sink = aux_ref[:, 1:2][None]
    m = jnp.maximum(jnp.maximum(jnp.max(s, axis=2, keepdims=True), s_new), sink)
    e = jnp.exp(s - m)
    e_new = jnp.exp(s_new - m)
    inv = 1.0 / (jnp.sum(e, axis=2, keepdims=True) + e_new + jnp.exp(sink - m))
    o8 = jnp.einsum("bqk,bdk->bqd", e.astype(BF16), vc_ref[...].astype(BF16), preferred_element_type=F32)
    o8 = (o8 + e_new * v_new) * inv
    o8 = jnp.where(lane_half == q_half, jnp.where(q_half == kv_head, o8, pltpu.roll(o8, HEAD_DIM, axis=2)), 0.0)
    for j in range(H // 2):
        o_ref[:, :, 128 * j:128 * (j + 1)] = jnp.sum(jnp.where(row // 2 == j, o8, 0.0), axis=1, keepdims=True)
    k_cols = kv2_ref[:, 0:S_KV_WIDTH].T
    v_cols = kv2_ref[:, S_KV_WIDTH:2 * S_KV_WIDTH].T
    last = lax.broadcasted_iota(jnp.int32, (S_KV_WIDTH, W), 1) == W - 1
    for r in range(R):
        ko_ref[l, r] = jnp.where(last, k_cols[:, r:r + 1], pltpu.roll(kc_ref[r], W - 1, axis=1))
        vo_ref[l, r] = jnp.where(last, v_cols[:, r:r + 1], pltpu.roll(vc_ref[r], W - 1, axis=1))


def _swa_sample(l, z, z3, kc_all, vc_all, bias, aux, prev):
    bsz = z.shape[0]
    R = min(SWA_SAMPLE_BLOCK, bsz)
    W = kc_all.shape[3]
    cache_in = pl.BlockSpec((None, R, S_KV_WIDTH, W), lambda i: (l, i, 0, 0))
    cache_prev = pl.BlockSpec((l, R, S_KV_WIDTH, W), lambda i: (0, i, 0, 0))
    cache_out = pl.BlockSpec((l + 1, R, S_KV_WIDTH, W), lambda i: (0, i, 0, 0))
    return pl.pallas_call(
        _swa_sample_kernel,
        grid=(bsz // R,),
        in_specs=[
            pl.BlockSpec((R, 1, S_WIDTH), lambda i: (i, 0, Z_SQ // S_WIDTH)),
            pl.BlockSpec((R, 1, 2 * S_KV_WIDTH), lambda i: (i, 0, Z_SKV // (2 * S_KV_WIDTH))),
            pl.BlockSpec((R, 2 * S_KV_WIDTH), lambda i: (i, Z_SKV // (2 * S_KV_WIDTH))),
            cache_in, cache_in,
            _const_spec((S_Q_HEADS, 128)), _weight_spec(aux, l),
        ] + [cache_prev] * len(prev),
        out_specs=[pl.BlockSpec((R, 1, S_WIDTH), lambda i: (i, 0, 0)), cache_out, cache_out],
        out_shape=[
            jax.ShapeDtypeStruct((bsz, 1, S_WIDTH), F32),
            jax.ShapeDtypeStruct((l + 1, bsz, S_KV_WIDTH, W), F32),
            jax.ShapeDtypeStruct((l + 1, bsz, S_KV_WIDTH, W), F32),
        ],
        compiler_params=_params("arbitrary"),
        name="swa_sample",
    )(z3, z3, z, kc_all, vc_all, bias, aux, *prev)


def _mix_out_sample_kernel(x_ref, hmt_ref, hc_ref, hs_ref, wout_ref, g2_ref, wq_ref, x1_ref, q_ref):
    cat = jnp.concatenate([hmt_ref[...].T.astype(BF16), hc_ref[...].astype(BF16), hs_ref[...].astype(BF16)], axis=1)
    x1 = x_ref[...] + _dot(cat, wout_ref[...])
    x1_ref[...] = x1
    q_ref[...] = _dot(_rms(x1, g2_ref[...]).astype(BF16), wq_ref[...])


def _mix_out_sample(l, x, hm, hc, hs, wout, g2, wq):
    bsz, d = x.shape
    full = lambda a: pl.BlockSpec(a.shape, lambda i: (0,) * a.ndim)
    args = (x, hm, hc, hs, wout, g2, wq)
    return pl.pallas_call(
        _mix_out_sample_kernel,
        grid=(1,),
        in_specs=[full(x), full(hm), full(hc), full(hs), _weight_spec(wout, l), _weight_spec(g2, l),
                  _weight_spec(wq, l)],
        out_specs=[pl.BlockSpec((bsz, d), lambda i: (0, 0)), pl.BlockSpec((bsz, X_WIDTH), lambda i: (0, 0))],
        out_shape=[jax.ShapeDtypeStruct((bsz, d), F32), jax.ShapeDtypeStruct((bsz, X_WIDTH), F32)],
        compiler_params=_params("arbitrary"),
        name="mix_out_sample",
    )(*args)


def _xattn_sample_kernel(q_ref, k_ref, v_ref, o_ref):
    R = k_ref.shape[0]
    shape = (R, SUBLANES, X_WIDTH)
    row = lax.broadcasted_iota(jnp.int32, shape, 1)
    lane_head = lax.broadcasted_iota(jnp.int32, shape, 2) // HEAD_DIM
    own = row == lane_head
    q8 = jnp.where(own, jnp.broadcast_to(q_ref[...], shape), 0.0).astype(BF16)
    s = jnp.einsum("bqd,bdk->bqk", q8, k_ref[...].astype(BF16), preferred_element_type=F32) * QK_SCALE
    e = jnp.exp(s - jnp.max(s, axis=2, keepdims=True))
    p = (e / jnp.sum(e, axis=2, keepdims=True)).astype(BF16)
    o8 = jnp.einsum("bqk,bdk->bqd", p, v_ref[...].astype(BF16), preferred_element_type=F32)
    o_ref[...] = jnp.sum(jnp.where(own, o8, 0.0), axis=1, keepdims=True)


def _xattn_sample(l, q3, k_all, v_all):
    _, bsz, w, mem = k_all.shape
    R = min(XATTN_SAMPLE_BLOCK, bsz)
    kv = pl.BlockSpec((None, R, w, mem), lambda i: (l, i, 0, 0))
    qo = pl.BlockSpec((R, 1, w), lambda i: (i, 0, 0))
    return pl.pallas_call(
        _xattn_sample_kernel,
        grid=(bsz // R,),
        in_specs=[qo, kv, kv],
        out_specs=qo,
        out_shape=jax.ShapeDtypeStruct((bsz, 1, w), F32),
        compiler_params=_params("arbitrary"),
        name="xattn_sample",
    )(q3, k_all, v_all)


def _ffn_sample_kernel(x1_ref, ox_ref, wo_ref, g3_ref, wup_ref, cw_ref, cb_ref, wdn_ref, gf_ref, hist_ref,
                       *rest, final):
    o_ref, hist_out_ref, act = rest[-3:]
    _copy_earlier_layers(rest[:-3], (hist_out_ref,))
    l = hist_out_ref.shape[0] - 1
    dff = wdn_ref.shape[0]
    x = x1_ref[...] + _dot(ox_ref[...].astype(BF16), wo_ref[...])
    h = _rms(x, g3_ref[...]).astype(BF16)
    for c in range(0, dff, FFN_CHUNK):
        a = _dot(h, wup_ref[:, c:c + FFN_CHUNK])
        g = _dot(h, wup_ref[:, dff + c:dff + c + FFN_CHUNK])
        h1 = hist_ref[:, 1, c:c + FFN_CHUNK]
        gc = (cw_ref[0:1, c:c + FFN_CHUNK] * hist_ref[:, 0, c:c + FFN_CHUNK]
              + cw_ref[1:2, c:c + FFN_CHUNK] * h1
              + cw_ref[2:3, c:c + FFN_CHUNK] * g + cb_ref[:, c:c + FFN_CHUNK])
        act[:, c:c + FFN_CHUNK] = (_swish(gc) * a).astype(BF16)
        hist_out_ref[l, :, 0, c:c + FFN_CHUNK] = h1
        hist_out_ref[l, :, 1, c:c + FFN_CHUNK] = g
    y = x + _dot(act[...], wdn_ref[...])
    o_ref[...] = _rms(y, gf_ref[...]) if final else y


def _ffn_sample(l, x1, ox, wo, g3, wup, cw, cb, wdn, gf, hist_all, prev, *, final):
    bsz, d = x1.shape
    dff = wdn.shape[1]
    full = lambda a: pl.BlockSpec(a.shape, lambda i: (0,) * a.ndim)
    hshape = hist_all.shape[1:]
    return pl.pallas_call(
        functools.partial(_ffn_sample_kernel, final=final),
        grid=(1,),
        in_specs=[full(x1), full(ox), _weight_spec(wo, l), _weight_spec(g3, l), _weight_spec(wup, l),
                  _weight_spec(cw, l), _weight_spec(cb, l),
                  _weight_spec(wdn, l), full(gf), pl.BlockSpec((None,) + hshape, lambda i: (l, 0, 0, 0))]
                 + [full(p) for p in prev],
        out_specs=[pl.BlockSpec((bsz, d), lambda i: (0, 0)), pl.BlockSpec((l + 1,) + hshape, lambda i: (0, 0, 0, 0))],
        out_shape=[jax.ShapeDtypeStruct((bsz, d), F32), jax.ShapeDtypeStruct((l + 1,) + hshape, F32)],
        scratch_shapes=[pltpu.VMEM((bsz, dff), BF16)],
        compiler_params=_params("arbitrary"),
        name="ffn_sample",
    )(x1, ox, wo, g3, wup, cw, cb, wdn, gf, hist_all, *prev)


def _t5_buckets(dist):
    n = np.maximum(dist, 0)
    max_exact = N_BUCKETS // 2
    nf = np.maximum(n, max_exact).astype(np.float32)
    large = max_exact + (np.log(nf / np.float32(max_exact)) / np.float32(math.log(MAX_DISTANCE / max_exact))
                         * np.float32(N_BUCKETS - max_exact)).astype(np.int32)
    return np.where(n < max_exact, n, np.minimum(large, N_BUCKETS - 1))


def _prompt_buckets():
    W = WINDOW
    dist = np.arange(W)[None, :] + W - np.arange(2 * W)[:, None]
    band = (dist >= 0) & (dist < W)
    return np.where(band, _t5_buckets(dist), -1).astype(np.int32)


def _swa_tables(rel_bias):
    W = WINDOW
    dist_c = W - np.arange(W)
    tab = jnp.transpose(rel_bias[_t5_buckets(dist_c)], (1, 0))
    cache_bias = jnp.where((dist_c < W)[None], tab, NEG_INF)
    return cache_bias, rel_bias[0]


def kernel(x_prompt, x_sample, mem_prompt, state_mlstm_C, state_mlstm_n, state_mlstm_m, state_conv, cache_swa_k, cache_swa_v, cache_mem_k, cache_mem_v, state_ffn_conv, rel_bias, norm1_g, w_in, b_i, b_f, mlstm_norm_g, conv_w, conv_b, conv_ln_g, conv_ln_b, swa_sinks, w_out, norm2_g, w_xq, w_xk, w_xv, w_xo, norm3_g, w_up, ffn_conv_w, ffn_conv_b, w_down, final_norm_g):
    depth = w_in.shape[0]
    bp, seq, d = x_prompt.shape
    bs = x_sample.shape[0]
    mem = mem_prompt.shape[1]
    dff = w_down.shape[1]
    W = WINDOW
    nh = C_KERNEL - 1

    xp = x_prompt
    xs = x_sample.reshape(bs, d)
    gf = final_norm_g.reshape(1, d)
    c_all = jnp.transpose(state_mlstm_C, (0, 2, 3, 4, 1))
    n_all = jnp.transpose(state_mlstm_n, (0, 2, 3, 1))
    m_all = jnp.transpose(state_mlstm_m, (0, 2, 1))
    hist_all = jnp.transpose(state_conv, (0, 2, 1, 3))
    kc_all = jnp.transpose(cache_swa_k, (0, 1, 3, 4, 2)).reshape(depth, bs, S_KV_WIDTH, W)
    vc_all = jnp.transpose(cache_swa_v, (0, 1, 3, 4, 2)).reshape(depth, bs, S_KV_WIDTH, W)
    mk_all = jnp.transpose(cache_mem_k, (0, 1, 3, 4, 2)).reshape(depth, bs, X_WIDTH, mem)
    mv_all = jnp.transpose(cache_mem_v, (0, 1, 3, 4, 2)).reshape(depth, bs, X_WIDTH, mem)
    pm_c, pm_n, pm_m, p_conv, p_k, p_v, p_mk, p_mv, p_ffn = ([] for _ in range(9))
    s_mlstm, s_conv, s_kv, s_ffn = [], [], [], []

    win_t = jnp.swapaxes(w_in, 1, 2)
    wout = w_out.astype(BF16)
    wxq = w_xq.astype(BF16)
    wxkv = jnp.concatenate([w_xk, w_xv], axis=2).astype(BF16)
    wxo = w_xo.astype(BF16)
    wup = w_up.astype(BF16)
    wdn = w_down.astype(BF16)
    bucket = jnp.asarray(_prompt_buckets())

    rows = lambda a: a.reshape(depth, 1, -1)
    g1, g2, g3 = rows(norm1_g), rows(norm2_g), rows(norm3_g)
    gb8 = jnp.concatenate([b_i, b_f], axis=1)
    gbias = rows(jnp.concatenate([gb8, jnp.zeros((depth, 128 - 2 * M_HEADS), F32)], axis=1))
    ng = rows(mlstm_norm_g)
    ngt = jnp.broadcast_to(mlstm_norm_g[:, :, None], (depth, M_WIDTH, bs))
    cw = jnp.concatenate([conv_w, jnp.zeros((depth, CONV_PAD - C_KERNEL, C_WIDTH), F32)], axis=1)
    cb, lg, lb = rows(conv_b), rows(conv_ln_g), rows(conv_ln_b)
    fcw = jnp.concatenate([ffn_conv_w, jnp.zeros((depth, SUBLANES - FFN_KERNEL, dff), F32)], axis=1)
    fcb = rows(ffn_conv_b)
    cache_bias, bias0 = _swa_tables(rel_bias)
    aux = jnp.concatenate([jnp.broadcast_to(bias0[None, :, None], (depth, S_Q_HEADS, 1)), swa_sinks[:, :, None],
                           jnp.zeros((depth, S_Q_HEADS, 126), F32)], axis=2)

    for l in range(depth):
        last = l == depth - 1

        mkv = _mem_kv(l, mem_prompt, wxkv)
        z, win = _in_proj(l, xp.reshape(bp * seq, d), g1, win_t, tm=1024)
        z = z.reshape(bp, seq, Z_WIDTH)
        hm, cpair, npair, mm = _mlstm_prompt(l, z, gbias, ng)
        hc, ctail = _conv_prompt(l, z, cw, cb, lg, lb, tc=min(1024, seq))
        hs, kt, vt = _swa_prompt(l, z, rel_bias, swa_sinks, bucket, qb=min(8, seq // W))
        xp = _mix_out_prompt(l, xp, hm, hc, hs, wout, g2, wxq, mkv, wxo, tm=1024)
        xp, ftail = _ffn_prompt(l, xp, g3, wup, fcw, fcb, wdn, gf, tm=512, final=last)
        half = lambda h: slice(HEAD_DIM * (h % 2), HEAD_DIM * (h % 2 + 1))
        pm_c.append(jnp.stack([jnp.swapaxes(cpair[:, h // 2, half(h), half(h)], 1, 2) for h in range(M_HEADS)], axis=1))
        pm_n.append(jnp.stack([npair[:, h // 2, h % 2, half(h)] for h in range(M_HEADS)], axis=1))
        pm_m.append(mm[:, :, 0, 0])
        p_conv.append(ctail[:, CONV_PAD - nh:, :])
        p_k.append(jnp.transpose(kt.reshape(bp, S_KV_HEADS, HEAD_DIM, W), (0, 3, 1, 2)))
        p_v.append(jnp.transpose(vt.reshape(bp, S_KV_HEADS, HEAD_DIM, W), (0, 3, 1, 2)))
        p_mk.append(jnp.transpose(mkv[:, 0:X_WIDTH, :].reshape(bp, X_HEADS, HEAD_DIM, mem), (0, 3, 1, 2)))
        p_mv.append(jnp.transpose(mkv[:, X_WIDTH:, :].reshape(bp, X_HEADS, HEAD_DIM, mem), (0, 3, 1, 2)))
        p_ffn.append(ftail[:, FFN_PAD - (FFN_KERNEL - 1):, :])

        zs, zst = _sample_in(l, xs, g1, win)
        hmt_s, *s_mlstm = _mlstm_sample(l, zst, gb8, ngt, c_all, n_all, m_all, s_mlstm)
        hc_s, *s_conv = _conv_sample(l, zs, hist_all, cw, cb, lg, lb, s_conv)
        hs_s, *s_kv = _swa_sample(l, zs, zs.reshape(bs, 1, Z_WIDTH), kc_all, vc_all, cache_bias, aux, s_kv)
        x1, qx = _mix_out_sample(l, xs, hmt_s, hc_s, hs_s.reshape(bs, S_WIDTH), wout, g2, wxq)
        ox = _xattn_sample(l, qx.reshape(bs, 1, X_WIDTH), mk_all, mv_all)
        xs, *s_ffn = _ffn_sample(l, x1, ox.reshape(bs, X_WIDTH), wxo, g3, wup, fcw, fcb, wdn, gf,
                                 state_ffn_conv, s_ffn, final=last)

    st = jnp.stack
    tr = jnp.transpose
    s_c, s_n, s_m = s_mlstm
    s_k, s_v = (a.reshape(depth, bs, S_KV_HEADS, HEAD_DIM, W) for a in s_kv)
    return (xp, xs.reshape(bs, 1, d),
            st(pm_c), st(pm_n), st(pm_m), st(p_conv), st(p_k), st(p_v), st(p_mk), st(p_mv), st(p_ffn),
            tr(s_c, (0, 4, 1, 2, 3)), tr(s_n, (0, 3, 1, 2)), tr(s_m, (0, 2, 1)),
            tr(s_conv[0], (0, 2, 1, 3)), tr(s_k, (0, 1, 4, 2, 3)), tr(s_v, (0, 1, 4, 2, 3)),
            s_ffn[0])
```

```python
import functools
import math

import numpy as np
import jax
import jax.numpy as jnp
from jax import lax
from jax.experimental import pallas as pl
from jax.experimental.pallas import tpu as pltpu

F32 = jnp.float32
BF16 = jnp.bfloat16
EPS = 1e-6
NEG_INF = float("-inf")

HEAD_DIM = 64
M_HEADS = 4
M_WIDTH = M_HEADS * HEAD_DIM
C_WIDTH = 256
C_KERNEL = 31
S_Q_HEADS = 8
S_KV_HEADS = 2
S_WIDTH = S_Q_HEADS * HEAD_DIM
S_KV_WIDTH = S_KV_HEADS * HEAD_DIM
WINDOW = 128
N_BUCKETS = 32
MAX_DISTANCE = 128
X_HEADS = 4
X_WIDTH = X_HEADS * HEAD_DIM
FFN_KERNEL = 3
QK_SCALE = HEAD_DIM ** -0.5
LOG2E = math.log2(math.e)

Z_M = 0
Z_C = 4 * M_WIDTH
Z_SQ = Z_C + 2 * C_WIDTH
Z_SKV = Z_SQ + S_WIDTH
Z_G = Z_SKV + 2 * S_KV_WIDTH
LANES = 128
SUBLANES = 8
Z_WIDTH = Z_G + LANES
VMEM_LIMIT = 56 * 1024 * 1024

M_CHUNK = 128
FFN_CHUNK = 256


def _params(*sem):
    return pltpu.CompilerParams(dimension_semantics=sem, vmem_limit_bytes=VMEM_LIMIT)


def _const_spec(shape):
    nd = len(shape)
    return pl.BlockSpec(shape, lambda *_: (0,) * nd, pipeline_mode=pl.Buffered(1))


def _weight_spec(w, l):
    nd = w.ndim - 1
    return pl.BlockSpec((None,) + w.shape[1:], lambda *_: (l,) + (0,) * nd, pipeline_mode=pl.Buffered(1))


def _rms(x, g):
    return x * lax.rsqrt(jnp.mean(x * x, axis=-1, keepdims=True) + EPS) * g


def _sigmoid(x):
    return 0.5 * jnp.tanh(0.5 * x) + 0.5


def _swish(x):
    h = 0.5 * x
    return h + h * jnp.tanh(h)


def _log_sigmoid(x):
    return jnp.minimum(x, 0.0) - jnp.log1p(jnp.exp(-jnp.abs(x)))


def _dot(a, b):
    return jnp.dot(a, b, preferred_element_type=F32)


def _dot_nt(a, b):
    return lax.dot_general(a, b, (((1,), (1,)), ((), ())), preferred_element_type=F32)


def _dot_tn(a, b):
    return lax.dot_general(a, b, (((0,), (0,)), ((), ())), preferred_element_type=F32)


IN_PROJ_ROWS = 512
IN_PROJ_COLS = 512
IN_GATES = 4 * M_WIDTH
W_PREP_ROWS = 256


def _in_proj_kernel(x_ref, g_ref, wt_ref, zm_ref, zc_ref, zq_ref, zkv_ref, zg_ref, wprep_ref, w_scr):
    d = x_ref.shape[1]

    @pl.when(pl.program_id(0) == 0)
    def _():
        n_gate = 2 * M_HEADS
        for src, dst, n in ((0, 0, IN_GATES), (IN_GATES + n_gate, IN_GATES, Z_G - IN_GATES)):
            for c in range(0, n, W_PREP_ROWS):
                w_scr[:, dst + c:dst + c + W_PREP_ROWS] = wt_ref[src + c:src + c + W_PREP_ROWS, :].T.astype(BF16)
        gate_rows = jnp.concatenate([wt_ref[IN_GATES:IN_GATES + n_gate, :], jnp.zeros((LANES - n_gate, d), F32)],
                                    axis=0)
        w_scr[:, Z_G:Z_G + LANES] = gate_rows.T.astype(BF16)
        wprep_ref[...] = w_scr[...]

    tm = x_ref.shape[0]
    sub = min(IN_PROJ_ROWS, tm)
    outs = ((zm_ref, Z_M), (zc_ref, Z_C), (zq_ref, Z_SQ), (zkv_ref, Z_SKV), (zg_ref, Z_G))
    for r0 in range(0, tm, sub):
        h = _rms(x_ref[r0:r0 + sub, :], g_ref[...]).astype(BF16)
        for o_ref, off in outs:
            n = o_ref.shape[1]
            for c in range(0, n, IN_PROJ_COLS):
                w = min(IN_PROJ_COLS, n - c)
                o_ref[r0:r0 + sub, c:c + w] = _dot(h, w_scr[:, off + c:off + c + w])


IN_PROJ_WIDTHS = (4 * M_WIDTH, 2 * C_WIDTH, S_WIDTH, 2 * S_KV_WIDTH, LANES)


def _in_proj(l, x, g, wt, *, tm):
    m, d = x.shape
    tm = min(tm, m)
    return pl.pallas_call(
        _in_proj_kernel,
        grid=(m // tm,),
        in_specs=[pl.BlockSpec((tm, d), lambda i: (i, 0)), _weight_spec(g, l), _weight_spec(wt, l)],
        out_specs=[pl.BlockSpec((tm, w), lambda i: (i, 0)) for w in IN_PROJ_WIDTHS]
                  + [pl.BlockSpec((d, Z_WIDTH), lambda i: (0, 0))],
        out_shape=[jax.ShapeDtypeStruct((m, w), F32) for w in IN_PROJ_WIDTHS]
                  + [jax.ShapeDtypeStruct((d, Z_WIDTH), BF16)],
        scratch_shapes=[pltpu.VMEM((d, Z_WIDTH), BF16)],
        compiler_params=_params("arbitrary"),
        name="in_proj",
    )(x, g, wt)


def _mem_kv_kernel(x_ref, w_ref, o_ref, acc):
    acc[...] = _dot(x_ref[...].astype(BF16), w_ref[...])
    o_ref[...] = acc[...].T


def _mem_kv(l, mem3, w):
    b, mem, d = mem3.shape
    n = w.shape[2]
    return pl.pallas_call(
        _mem_kv_kernel,
        grid=(b,),
        in_specs=[pl.BlockSpec((None, mem, d), lambda i: (i, 0, 0)), _weight_spec(w, l)],
        out_specs=pl.BlockSpec((None, n, mem), lambda i: (i, 0, 0)),
        out_shape=jax.ShapeDtypeStruct((b, n, mem), F32),
        scratch_shapes=[pltpu.VMEM((mem, n), F32)],
        compiler_params=_params("arbitrary"),
        name="mem_kv",
    )(mem3, w)


def _sample_in_kernel(x_ref, g_ref, w_ref, z_ref, zt_ref):
    h = _rms(x_ref[...], g_ref[...]).astype(BF16)
    n = z_ref.shape[1]
    for c in range(0, n, IN_PROJ_COLS):
        w = min(IN_PROJ_COLS, n - c)
        zc = _dot(h, w_ref[:, c:c + w])
        z_ref[:, c:c + w] = zc
        zt_ref[c:c + w, :] = zc.T


def _sample_in(l, x, g, w):
    m, d = x.shape
    n = w.shape[1]
    full = lambda shape: pl.BlockSpec(shape, lambda i: (0,) * len(shape))
    return pl.pallas_call(
        _sample_in_kernel,
        grid=(1,),
        in_specs=[full((m, d)), _weight_spec(g, l), full((d, n))],
        out_specs=[full((m, n)), full((n, m))],
        out_shape=[jax.ShapeDtypeStruct((m, n), F32), jax.ShapeDtypeStruct((n, m), F32)],
        compiler_params=_params("arbitrary"),
        name="sample_in",
    )(x, g, w)


def _mlstm_prompt_kernel(z_ref, g_ref, gb_ref, ng_ref, h_ref, cp_ref, np_ref, m_ref,
                         cp_scr, np_scr, m_scr, ht_scr, st_scr, kw_scr):
    NB, L = z_ref.shape[0], z_ref.shape[1]
    D = HEAD_DIM
    hi = lax.Precision.HIGHEST

    @pl.when(pl.program_id(0) == 0)
    def _():
        cp_scr[...] = jnp.zeros(cp_scr.shape, F32)
        np_scr[...] = jnp.zeros(np_scr.shape, F32)
        m_scr[...] = jnp.zeros(m_scr.shape, F32)

    src = lax.broadcasted_iota(jnp.int32, (L, L), 0)
    qry = lax.broadcasted_iota(jnp.int32, (L, L), 1)
    causal_t = src <= qry
    upper = jnp.where(causal_t, 1.0, 0.0)
    lane_half = lax.broadcasted_iota(jnp.int32, (L, 128), 1) // D
    row8 = lax.broadcasted_iota(jnp.int32, (SUBLANES, 128), 0)

    rows, cols = {}, []
    for b in range(NB):
        g_t = (g_ref[b] + gb_ref[...]).T[0:SUBLANES, :]
        b_rows = jnp.dot(_log_sigmoid(g_t), upper, precision=hi, preferred_element_type=F32)
        to_cols = []
        for h in range(M_HEADS):
            b_row = b_rows[M_HEADS + h:M_HEADS + h + 1, :]
            ci_row = g_t[h:h + 1, :] - b_row
            m_prev = m_scr[b, h, 0:1, 0:1]
            b_last = b_row[:, L - 1:L]
            m_new = jnp.maximum(b_last + m_prev, jnp.max(b_last + ci_row, axis=1, keepdims=True))
            rows[b, h] = dict(b_row=b_row, m_prev=m_prev, m_new=m_new, decay=jnp.exp(b_last + m_prev - m_new))
            to_cols += [ci_row, jnp.exp(b_last + ci_row - m_new)]
        to_cols.append(jnp.zeros((128 - 2 * M_HEADS, L), F32))
        cols.append(jnp.concatenate(to_cols, axis=0).T)

    stats = {}
    for b in range(NB):
        for j in range(M_HEADS // 2):
            slab = 128 * j
            q2 = z_ref[b, :, slab:slab + 128].astype(BF16)
            k2 = z_ref[b, :, M_WIDTH + slab:M_WIDTH + slab + 128] * QK_SCALE
            for par in range(2):
                h = 2 * j + par
                r = rows[b, h]
                ci_col = cols[b][:, 2 * h:2 * h + 1]
                a_row = r["b_row"] + r["m_prev"]
                dm = jnp.where(causal_t, r["b_row"] + ci_col, NEG_INF)
                m_row = jnp.maximum(a_row, jnp.max(dm, axis=0, keepdims=True))
                k_own = jnp.where(lane_half == par, k2, 0.0).astype(BF16)
                s_t = _dot_nt(k_own, q2) * jnp.exp(dm - m_row)
                st_scr[b, j, :, L * par:L * (par + 1)] = s_t.astype(BF16)
                stats[b, h] = dict(w_inter=jnp.exp(a_row - m_row), floor=jnp.exp(-m_row),
                                   den_s=jnp.sum(s_t, axis=0, keepdims=True),
                                   decay=r["decay"], m_new=r["m_new"])
            wk = jnp.where(lane_half == 0, cols[b][:, 4 * j + 1:4 * j + 2], cols[b][:, 4 * j + 3:4 * j + 4])
            kw = k2 * wk
            kw_scr[b, j] = kw.astype(BF16)
            stats[b, j, "k_sum"] = jnp.sum(kw, axis=0, keepdims=True)

    block_diag = (lax.broadcasted_iota(jnp.int32, (128, 128), 0) // D
                  == lax.broadcasted_iota(jnp.int32, (128, 128), 1) // D)
    for b in range(NB):
        for j in range(M_HEADS // 2):
            slab = 128 * j
            q2 = z_ref[b, :, slab:slab + 128].astype(BF16)
            v2 = z_ref[b, :, 2 * M_WIDTH + slab:2 * M_WIDTH + slab + 128].astype(BF16)
            cp = cp_scr[b, j]
            npair = np_scr[b, j]
            qc = _dot_nt(cp.astype(BF16), q2)
            qn = _dot_nt(npair.astype(BF16), q2)
            pv = _dot_tn(v2, st_scr[b, j])
            upd = jnp.where(block_diag, _dot_tn(v2, kw_scr[b, j]), 0.0)
            n_new = jnp.zeros((SUBLANES, 128), F32)
            for par in range(2):
                h = 2 * j + par
                st = stats[b, h]
                hr = slice(D * par, D * (par + 1))
                num = st["w_inter"] * qc[hr, :] + pv[hr, L * par:L * (par + 1)]
                den = st["w_inter"] * qn[par:par + 1, :] + st["den_s"]
                hh = num * (1.0 / jnp.maximum(jnp.abs(den), st["floor"]))
                hh = hh * lax.rsqrt(jnp.mean(hh * hh, axis=0, keepdims=True) + EPS)
                ht_scr[b, D * h:D * (h + 1), :] = hh
                cp_scr[b, j, hr, :] = st["decay"] * cp[hr, :] + upd[hr, :]
                k_sum = jnp.where(lane_half[0:1, :] == par, stats[b, j, "k_sum"], 0.0)
                n_new = jnp.where(row8 == par, st["decay"] * npair[par:par + 1, :] + k_sum, n_new)
                m_scr[b, h] = jnp.broadcast_to(st["m_new"], m_scr.shape[2:])
            np_scr[b, j] = n_new
        o_gate = _sigmoid(z_ref[b, :, 3 * M_WIDTH:4 * M_WIDTH])
        h_ref[b] = ht_scr[b].T * ng_ref[...] * o_gate

    cp_ref[...] = cp_scr[...]
    np_ref[...] = np_scr[...]
    m_ref[...] = m_scr[...]


def _mlstm_prompt(l, zm3, zg3, gbias, ng):
    b, s, _ = zm3.shape
    L = M_CHUNK
    P = M_HEADS // 2
    return pl.pallas_call(
        _mlstm_prompt_kernel,
        grid=(s // L,),
        in_specs=[
            pl.BlockSpec((b, L, 4 * M_WIDTH), lambda c: (0, c, 0)),
            pl.BlockSpec((b, L, 128), lambda c: (0, c, 0)),
            _weight_spec(gbias, l),
            _weight_spec(ng, l),
        ],
        out_specs=[
            pl.BlockSpec((b, L, M_WIDTH), lambda c: (0, c, 0)),
            pl.BlockSpec((b, P, 128, 128), lambda c: (0, 0, 0, 0)),
            pl.BlockSpec((b, P, SUBLANES, 128), lambda c: (0, 0, 0, 0)),
            pl.BlockSpec((b, M_HEADS, SUBLANES, 128), lambda c: (0, 0, 0, 0)),
        ],
        out_shape=[
            jax.ShapeDtypeStruct((b, s, M_WIDTH), F32),
            jax.ShapeDtypeStruct((b, P, 128, 128), F32),
            jax.ShapeDtypeStruct((b, P, SUBLANES, 128), F32),
            jax.ShapeDtypeStruct((b, M_HEADS, SUBLANES, 128), F32),
        ],
        scratch_shapes=[
            pltpu.VMEM((b, P, 128, 128), F32),
            pltpu.VMEM((b, P, SUBLANES, 128), F32),
            pltpu.VMEM((b, M_HEADS, SUBLANES, 128), F32),
            pltpu.VMEM((b, M_WIDTH, L), F32),
            pltpu.VMEM((b, P, L, 2 * L), BF16),
            pltpu.VMEM((b, P, L, 128), BF16),
        ],
        compiler_params=_params("arbitrary"),
        name="mlstm_prompt",
    )(zm3, zg3, gbias, ng)


CONV_PAD = 32


def _conv_prompt_kernel(z_ref, w_ref, cb_ref, lg_ref, lb_ref, h_ref, tail_ref, buf, shifted):
    tc = z_ref.shape[0]
    t = pl.program_id(1)

    @pl.when(t == 0)
    def _():
        buf[0:CONV_PAD, :] = jnp.zeros((CONV_PAD, C_WIDTH), F32)

    @pl.when(t > 0)
    def _():
        buf[0:CONV_PAD, :] = buf[tc:tc + CONV_PAD, :]

    u = z_ref[:, 0:C_WIDTH] * _sigmoid(z_ref[:, C_WIDTH:2 * C_WIDTH])
    buf[CONV_PAD:CONV_PAD + tc, :] = u
    n_sh = shifted.shape[1]
    for r in range(1, SUBLANES):
        shifted[r - 1] = buf[r:r + n_sh, :]
    off = CONV_PAD - (C_KERNEL - 1)
    acc = jnp.broadcast_to(cb_ref[...], (tc, C_WIDTH))
    for j in range(C_KERNEL):
        r = (off + j) % SUBLANES
        base = off + j - r
        win = buf[base:base + tc, :] if r == 0 else shifted[r - 1, base:base + tc, :]
        acc = acc + w_ref[j:j + 1, :] * win
    mu = jnp.mean(acc, axis=-1, keepdims=True)
    xc = acc - mu
    y = xc * lax.rsqrt(jnp.mean(xc * xc, axis=-1, keepdims=True) + EPS) * lg_ref[...] + lb_ref[...]
    h_ref[...] = _swish(y)
    tail_ref[...] = buf[tc:tc + CONV_PAD, :]


def _conv_prompt(l, z3, w, cb, lg, lb, *, tc):
    b, s, _ = z3.shape
    return pl.pallas_call(
        _conv_prompt_kernel,
        grid=(b, s // tc),
        in_specs=[
            pl.BlockSpec((None, tc, 2 * C_WIDTH), lambda i, t: (i, t, 0)),
            _weight_spec(w, l), _weight_spec(cb, l), _weight_spec(lg, l), _weight_spec(lb, l),
        ],
        out_specs=[
            pl.BlockSpec((None, tc, C_WIDTH), lambda i, t: (i, t, 0)),
            pl.BlockSpec((None, CONV_PAD, C_WIDTH), lambda i, t: (i, 0, 0)),
        ],
        out_shape=[
            jax.ShapeDtypeStruct((b, s, C_WIDTH), F32),
            jax.ShapeDtypeStruct((b, CONV_PAD, C_WIDTH), F32),
        ],
        scratch_shapes=[pltpu.VMEM((CONV_PAD + tc, C_WIDTH), F32),
                        pltpu.VMEM((SUBLANES - 1, CONV_PAD + tc - SUBLANES, C_WIDTH), F32)],
        compiler_params=_params("arbitrary", "arbitrary"),
        name="conv_prompt",
    )(z3, w, cb, lg, lb)


def _swa_prompt_kernel(rb_ref, sink_ref, bucket_ref, q_ref, kv_ref, kvp_ref, o_ref, kt_ref, vt_ref,
                       bias_scr, s_scr, p_scr, ot_scr, *, layer):
    W = WINDOW
    H = S_Q_HEADS
    G = H // S_KV_HEADS
    i = pl.program_id(0)
    n = pl.program_id(1)
    from_prev = lax.broadcasted_iota(jnp.int32, (W, W), 0) > lax.broadcasted_iota(jnp.int32, (W, W), 1)

    @pl.when(jnp.logical_and(i == 0, n == 0))
    def _():
        bucket = bucket_ref[...]
        for h in range(H):
            acc = jnp.full((2 * W, W), NEG_INF, F32)
            for b in range(N_BUCKETS):
                acc = jnp.where(bucket == b, rb_ref[b, h], acc)
            acc = acc * LOG2E
            bias_scr[0, h] = jnp.where(from_prev, acc[0:W, :], acc[W:2 * W, :])
            bias_scr[1, h] = jnp.where(from_prev, NEG_INF, acc[W:2 * W, :])

    QB = q_ref.shape[0] // W
    NK = (QB + 1) * W
    first = jnp.where(n == 0, 1, 0)
    kk = jnp.concatenate([kvp_ref[:, 0:S_KV_WIDTH], kv_ref[:, 0:S_KV_WIDTH]], axis=0) * (QK_SCALE * LOG2E)
    vv = jnp.concatenate([kvp_ref[:, S_KV_WIDTH:2 * S_KV_WIDTH], kv_ref[:, S_KV_WIDTH:2 * S_KV_WIDTH]], axis=0)
    kk_r = pltpu.roll(kk, HEAD_DIM, axis=1)
    lo_lane = lax.broadcasted_iota(jnp.int32, (NK, S_KV_WIDTH), 1) < HEAD_DIM
    k_var = [[jnp.where(lo_lane, kk, 0.0).astype(BF16), jnp.where(lo_lane, 0.0, kk_r).astype(BF16)],
             [jnp.where(lo_lane, kk_r, 0.0).astype(BF16), jnp.where(lo_lane, 0.0, kk).astype(BF16)]]
    v_t = vv.T.astype(BF16)

    for j in range(QB):
        k0 = j * W
        masked = first if j == 0 else 0
        m_rows = [None] * H
        for hk in range(S_KV_HEADS):
            c0 = 2 * 128 * hk
            q_st = jnp.concatenate([q_ref[k0:k0 + W, c0:c0 + 128], q_ref[k0:k0 + W, c0 + 128:c0 + 256]],
                                   axis=0).astype(BF16)
            for half in range(2):
                s_t = _dot_nt(k_var[hk][half][k0:k0 + 2 * W, :], q_st)
                for slab in range(2):
                    head = G * hk + 2 * slab + half
                    lanes = slice(128 * slab, 128 * (slab + 1))
                    sb = jnp.where(from_prev, s_t[0:W, lanes], s_t[W:2 * W, lanes]) + bias_scr[masked, head]
                    s_scr[j, head] = sb
                    m_rows[head] = jnp.maximum(jnp.max(sb, axis=0, keepdims=True), sink_ref[layer, head] * LOG2E)

        inv = [None] * H
        for head in range(H):
            e = jnp.exp2(s_scr[j, head] - m_rows[head])
            den = jnp.sum(e, axis=0, keepdims=True) + jnp.exp2(sink_ref[layer, head] * LOG2E - m_rows[head])
            inv[head] = 1.0 / den
            p_scr[j, 0:W, 128 * head:128 * (head + 1)] = jnp.where(from_prev, e, 0.0).astype(BF16)
            p_scr[j, W:2 * W, 128 * head:128 * (head + 1)] = jnp.where(from_prev, 0.0, e).astype(BF16)

        for hk in range(S_KV_HEADS):
            o_t = _dot(v_t[HEAD_DIM * hk:HEAD_DIM * (hk + 1), k0:k0 + 2 * W],
                       p_scr[j, :, 128 * G * hk:128 * G * (hk + 1)])
            for g in range(G):
                head = G * hk + g
                ot_scr[j, HEAD_DIM * head:HEAD_DIM * (head + 1), :] = o_t[:, 128 * g:128 * (g + 1)] * inv[head]
        o_ref[k0:k0 + W, :] = ot_scr[j].T

    @pl.when(n == pl.num_programs(1) - 1)
    def _():
        kt_ref[...] = kv_ref[(QB - 1) * W:QB * W, 0:S_KV_WIDTH].T
        vt_ref[...] = kv_ref[(QB - 1) * W:QB * W, S_KV_WIDTH:2 * S_KV_WIDTH].T


def _swa_prompt(l, zq3, zkv3, rel_bias, sinks, bucket, *, qb):
    b, s, _ = zq3.shape
    W = WINDOW
    smem = pl.BlockSpec(memory_space=pltpu.SMEM)
    return pl.pallas_call(
        functools.partial(_swa_prompt_kernel, layer=l),
        grid=(b, s // (qb * W)),
        in_specs=[
            smem, smem, _const_spec((2 * W, W)),
            pl.BlockSpec((None, qb * W, S_WIDTH), lambda i, n: (i, n, 0)),
            pl.BlockSpec((None, qb * W, 2 * S_KV_WIDTH), lambda i, n: (i, n, 0)),
            pl.BlockSpec((None, W, 2 * S_KV_WIDTH), lambda i, n: (i, jnp.maximum(n * qb - 1, 0), 0)),
        ],
        out_specs=[
            pl.BlockSpec((None, qb * W, S_WIDTH), lambda i, n: (i, n, 0)),
            pl.BlockSpec((None, S_KV_WIDTH, W), lambda i, n: (i, 0, 0)),
            pl.BlockSpec((None, S_KV_WIDTH, W), lambda i, n: (i, 0, 0)),
        ],
        out_shape=[
            jax.ShapeDtypeStruct((b, s, S_WIDTH), F32),
            jax.ShapeDtypeStruct((b, S_KV_WIDTH, W), F32),
            jax.ShapeDtypeStruct((b, S_KV_WIDTH, W), F32),
        ],
        scratch_shapes=[
            pltpu.VMEM((2, S_Q_HEADS, W, W), F32),
            pltpu.VMEM((qb, S_Q_HEADS, W, W), F32),
            pltpu.VMEM((qb, 2 * W, S_Q_HEADS * W), BF16),
            pltpu.VMEM((qb, S_WIDTH, W), F32),
        ],
        compiler_params=_params("arbitrary", "arbitrary"),
        name="swa_prompt",
    )(rel_bias, sinks, bucket, zq3, zkv3, zkv3)


MIX_SUB = 512


def _mix_out_prompt_kernel(x_ref, hm_ref, hc_ref, hs_ref, wout_ref, g2_ref, wq_ref, mk_ref, mv_ref, wo_ref, o_ref,
                           x1_scr, qx_scr, p_scr):
    tm = x_ref.shape[0]
    mem = mk_ref.shape[1]
    mk_t = mk_ref[...]
    mv_t = mv_ref[...]
    row_head = lax.broadcasted_iota(jnp.int32, mk_t.shape, 0) // HEAD_DIM
    k_heads = [jnp.where(row_head == h, mk_t, 0.0).astype(BF16) for h in range(X_HEADS)]
    v_cat = jnp.concatenate([jnp.where(row_head == h, mv_t, 0.0).astype(BF16) for h in range(X_HEADS)], axis=1)
    subs = [slice(r0, r0 + MIX_SUB) for r0 in range(0, tm, MIX_SUB)]
    for rows in subs:
        cat = jnp.concatenate([hm_ref[rows, :].astype(BF16), hc_ref[rows, :].astype(BF16),
                               hs_ref[rows, :].astype(BF16)], axis=1)
        x1 = x_ref[rows, :] + _dot(cat, wout_ref[...])
        x1_scr[rows, :] = x1
        qx = _dot(_rms(x1, g2_ref[...]).astype(BF16), wq_ref[...])
        qx_scr[rows, :] = (qx * (QK_SCALE * LOG2E)).astype(BF16)
    for rows in subs:
        qx = qx_scr[rows, :]
        for h in range(X_HEADS):
            s = _dot(qx, k_heads[h])
            e = jnp.exp2(s - jnp.max(s, axis=1, keepdims=True))
            p_scr[rows, mem * h:mem * (h + 1)] = (e * (1.0 / jnp.sum(e, axis=1, keepdims=True))).astype(BF16)
    for rows in subs:
        o = _dot_nt(p_scr[rows, :], v_cat)
        o_ref[rows, :] = x1_scr[rows, :] + _dot(o.astype(BF16), wo_ref[...])


def _mix_out_prompt(l, x3, hm, hc, hs, wout, g2, wq, mkv, wo, *, tm):
    b, s, d = x3.shape
    tm = min(tm, s)
    mem = mkv.shape[2]
    row = lambda w: pl.BlockSpec((None, tm, w), lambda i, t: (i, t, 0))
    return pl.pallas_call(
        _mix_out_prompt_kernel,
        grid=(b, s // tm),
        in_specs=[
            row(d), row(M_WIDTH), row(C_WIDTH), row(S_WIDTH),
            _weight_spec(wout, l), _weight_spec(g2, l), _weight_spec(wq, l),
            pl.BlockSpec((None, X_WIDTH, mem), lambda i, t: (i, 0, 0)),
            pl.BlockSpec((None, X_WIDTH, mem), lambda i, t: (i, 1, 0)),
            _weight_spec(wo, l),
        ],
        out_specs=row(d),
        out_shape=jax.ShapeDtypeStruct((b, s, d), F32),
        scratch_shapes=[pltpu.VMEM((tm, d), F32), pltpu.VMEM((tm, X_WIDTH), BF16),
                        pltpu.VMEM((tm, X_HEADS * mem), BF16)],
        compiler_params=_params("arbitrary", "arbitrary"),
        name="mix_out_prompt",
    )(x3, hm, hc, hs, wout, g2, wq, mkv, mkv, wo)


FFN_PAD = 8


def _ffn_prompt_kernel(x_ref, g3_ref, wup_ref, cw_ref, cb_ref, wdn_ref, gf_ref, o_ref, tail_ref,
                       carry, act, *, final):
    tm = x_ref.shape[0]
    dff = wdn_ref.shape[0]

    @pl.when(pl.program_id(1) == 0)
    def _():
        carry[...] = jnp.zeros(carry.shape, F32)

    x = x_ref[...]
    h = _rms(x, g3_ref[...]).astype(BF16)
    w = FFN_CHUNK
    row8 = lax.broadcasted_iota(jnp.int32, (FFN_PAD, w), 0)
    for c in range(0, dff, w):
        a = _dot(h, wup_ref[:, c:c + w])
        g = _dot(h, wup_ref[:, dff + c:dff + c + w])
        prev = carry[:, c:c + w]
        carry[:, c:c + w] = g[tm - FFN_PAD:tm, :]
        taps = []
        for s in (2, 1):
            gs = pltpu.roll(g, s, axis=0)
            head = jnp.where(row8 < s, pltpu.roll(prev, s, axis=0), gs[0:FFN_PAD, :])
            taps.append(jnp.concatenate([head, gs[FFN_PAD:, :]], axis=0))
        gc = (cw_ref[0:1, c:c + w] * taps[0] + cw_ref[1:2, c:c + w] * taps[1]
              + cw_ref[2:3, c:c + w] * g + cb_ref[:, c:c + w])
        act[:, c:c + w] = (_swish(gc) * a).astype(BF16)
    y = x + _dot(act[...], wdn_ref[...])
    o_ref[...] = _rms(y, gf_ref[...]) if final else y
    tail_ref[...] = carry[...]


def _ffn_prompt(l, x3, g3, wup, cw, cb, wdn, gf, *, tm, final):
    b, s, d = x3.shape
    tm = min(tm, s)
    dff = wdn.shape[1]
    return pl.pallas_call(
        functools.partial(_ffn_prompt_kernel, final=final),
        grid=(b, s // tm),
        in_specs=[
            pl.BlockSpec((None, tm, d), lambda i, t: (i, t, 0)),
            _weight_spec(g3, l), _weight_spec(wup, l), _weight_spec(cw, l), _weight_spec(cb, l),
            _weight_spec(wdn, l), _const_spec((1, d)),
        ],
        out_specs=[
            pl.BlockSpec((None, tm, d), lambda i, t: (i, t, 0)),
            pl.BlockSpec((None, FFN_PAD, dff), lambda i, t: (i, 0, 0)),
        ],
        out_shape=[jax.ShapeDtypeStruct((b, s, d), F32), jax.ShapeDtypeStruct((b, FFN_PAD, dff), F32)],
        scratch_shapes=[
            pltpu.VMEM((FFN_PAD, dff), F32),
            pltpu.VMEM((tm, dff), BF16),
        ],
        compiler_params=_params("arbitrary", "arbitrary"),
        name="ffn_prompt",
    )(x3, g3, wup, cw, cb, wdn, gf)


SWA_SAMPLE_BLOCK = 32
XATTN_SAMPLE_BLOCK = 16
CONV_SAMPLE_BLOCK = 32


def _mlstm_sample_kernel(gb_ref, q_ref, k_ref, v_ref, o_ref, g_ref, ng_ref, c_ref, n_ref, m_ref, *rest, layer):
    h_ref, co_ref, no_ref, mo_ref, kw_scr = rest[-5:]
    _copy_earlier_layers(rest[:-5], (co_ref, no_ref, mo_ref))
    h = pl.program_id(0)
    i_pre = g_ref[pl.ds(h, 1), :] + gb_ref[layer, h]
    f_pre = g_ref[pl.ds(M_HEADS + h, 1), :] + gb_ref[layer, M_HEADS + h]
    a = _log_sigmoid(f_pre) + m_ref[pl.ds(h, 1), :]
    m_t = jnp.maximum(a, i_pre)
    w_old = jnp.exp(a - m_t)
    w_new = jnp.exp(i_pre - m_t)
    q = q_ref[...]
    k = k_ref[...] * QK_SCALE
    v = v_ref[...]
    n_old = n_ref[...]
    kw_scr[...] = k * w_new

    def body(d, acc):
        c_old = c_ref[d]
        co_ref[layer, d] = w_old * c_old + kw_scr[pl.ds(d, 1), :] * v
        return acc + q_ref[pl.ds(d, 1), :] * c_old

    qc = lax.fori_loop(0, HEAD_DIM, body, jnp.zeros(v.shape, F32), unroll=8)
    s = jnp.sum(q * k, axis=0, keepdims=True) * w_new
    num = w_old * qc + s * v
    den = w_old * jnp.sum(q * n_old, axis=0, keepdims=True) + s
    hh = num / jnp.maximum(jnp.abs(den), jnp.exp(-m_t))
    hh = hh * lax.rsqrt(jnp.mean(hh * hh, axis=0, keepdims=True) + EPS) * ng_ref[...]
    h_ref[...] = hh * _sigmoid(o_ref[...])
    no_ref[layer] = w_old * n_old + kw_scr[...]
    mo_ref[layer, pl.ds(h, 1), :] = m_t


def _mlstm_sample(l, zt, gb, ngt, c_all, n_all, m_all, prev):
    bsz = zt.shape[1]
    D = HEAD_DIM
    feat = lambda off: pl.BlockSpec((D, bsz), lambda h: (off // D + h, 0))
    c_spec = lambda n: pl.BlockSpec((n, None, D, D, bsz), lambda h: (0, h, 0, 0, 0))
    n_spec = lambda n: pl.BlockSpec((n, None, D, bsz), lambda h: (0, h, 0, 0))
    m_spec = lambda n: pl.BlockSpec((n, M_HEADS, bsz), lambda h: (0, 0, 0))
    prev_specs = [c_spec(l), n_spec(l), m_spec(l)] if prev else []
    return pl.pallas_call(
        functools.partial(_mlstm_sample_kernel, layer=l),
        grid=(M_HEADS,),
        in_specs=[
            pl.BlockSpec(memory_space=pltpu.SMEM),
            feat(Z_M), feat(Z_M + M_WIDTH), feat(Z_M + 2 * M_WIDTH), feat(Z_M + 3 * M_WIDTH),
            pl.BlockSpec((SUBLANES, bsz), lambda h: (Z_G // SUBLANES, 0)),
            pl.BlockSpec((None, D, bsz), lambda h: (l, h, 0)),
            pl.BlockSpec((None, None, D, D, bsz), lambda h: (l, h, 0, 0, 0)),
            pl.BlockSpec((None, None, D, bsz), lambda h: (l, h, 0, 0)),
            pl.BlockSpec((None, M_HEADS, bsz), lambda h: (l, 0, 0)),
        ] + prev_specs,
        out_specs=[pl.BlockSpec((D, bsz), lambda h: (h, 0)), c_spec(l + 1), n_spec(l + 1), m_spec(l + 1)],
        out_shape=[
            jax.ShapeDtypeStruct((M_WIDTH, bsz), F32),
            jax.ShapeDtypeStruct((l + 1, M_HEADS, D, D, bsz), F32),
            jax.ShapeDtypeStruct((l + 1, M_HEADS, D, bsz), F32),
            jax.ShapeDtypeStruct((l + 1, M_HEADS, bsz), F32),
        ],
        scratch_shapes=[pltpu.VMEM((D, bsz), F32)],
        compiler_params=_params("arbitrary"),
        name="mlstm_sample",
    )(gb, zt, zt, zt, zt, zt, ngt, c_all, n_all, m_all, *prev)


def _conv_sample_kernel(z_ref, hist_ref, w_ref, cb_ref, lg_ref, lb_ref, *rest):
    h_ref, hist_out_ref = rest[-2:]
    _copy_earlier_layers(rest[:-2], (hist_out_ref,))
    l = hist_out_ref.shape[0] - 1
    nh = C_KERNEL - 1
    u = z_ref[:, 0:C_WIDTH] * _sigmoid(z_ref[:, C_WIDTH:2 * C_WIDTH])
    acc = cb_ref[...] + w_ref[nh:nh + 1, :] * u
    for j in range(nh):
        acc = acc + w_ref[j:j + 1, :] * hist_ref[j]
    mu = jnp.mean(acc, axis=-1, keepdims=True)
    xc = acc - mu
    y = xc * lax.rsqrt(jnp.mean(xc * xc, axis=-1, keepdims=True) + EPS) * lg_ref[...] + lb_ref[...]
    h_ref[...] = _swish(y)
    for j in range(nh - 1):
        hist_out_ref[l, j] = hist_ref[j + 1]
    hist_out_ref[l, nh - 1] = u


def _conv_sample(l, z, hist_all, w, cb, lg, lb, prev):
    bsz = z.shape[0]
    nh = hist_all.shape[1]
    R = min(CONV_SAMPLE_BLOCK, bsz)
    hist_spec = lambda n: pl.BlockSpec((n, nh, R, C_WIDTH), lambda i: (0, 0, i, 0))
    return pl.pallas_call(
        _conv_sample_kernel,
        grid=(bsz // R,),
        in_specs=[
            pl.BlockSpec((R, 2 * C_WIDTH), lambda i: (i, Z_C // (2 * C_WIDTH))),
            pl.BlockSpec((None, nh, R, C_WIDTH), lambda i: (l, 0, i, 0)),
            _weight_spec(w, l), _weight_spec(cb, l), _weight_spec(lg, l), _weight_spec(lb, l),
        ] + [hist_spec(l)] * len(prev),
        out_specs=[pl.BlockSpec((R, C_WIDTH), lambda i: (i, 0)), hist_spec(l + 1)],
        out_shape=[jax.ShapeDtypeStruct((bsz, C_WIDTH), F32), jax.ShapeDtypeStruct((l + 1, nh, bsz, C_WIDTH), F32)],
        compiler_params=_params("arbitrary"),
        name="conv_sample",
    )(z, hist_all, w, cb, lg, lb, *prev)


def _copy_earlier_layers(prev_refs, out_refs):
    for p_ref, o_ref in zip(prev_refs, out_refs):
        o_ref[0:p_ref.shape[0]] = p_ref[...]


def _swa_sample_kernel(q_ref, kvn_ref, kv2_ref, kc_ref, vc_ref, bias_ref, aux_ref, *rest):
    o_ref, ko_ref, vo_ref = rest[-3:]
    _copy_earlier_layers(rest[:-3], (ko_ref, vo_ref))
    l = ko_ref.shape[0] - 1
    R = q_ref.shape[0]
    W = kc_ref.shape[2]
    H = S_Q_HEADS
    shape = (R, H, 128)
    row = lax.broadcasted_iota(jnp.int32, shape, 1)
    lane_half = lax.broadcasted_iota(jnp.int32, shape, 2) // HEAD_DIM
    q_half = row % 2
    kv_head = row // (H // S_KV_HEADS)
    qs = jnp.zeros(shape, F32)
    for j in range(H // 2):
        qs = jnp.where(row // 2 == j, q_ref[:, :, 128 * j:128 * (j + 1)], qs)
    q8 = jnp.where(lane_half == kv_head, jnp.where(q_half == kv_head, qs, pltpu.roll(qs, HEAD_DIM, axis=2)), 0.0)
    k_new = kvn_ref[:, :, 0:S_KV_WIDTH]
    v_new = kvn_ref[:, :, S_KV_WIDTH:2 * S_KV_WIDTH]
    s = jnp.einsum("bqd,bdk->bqk", q8.astype(BF16), kc_ref[...].astype(BF16), preferred_element_type=F32) * QK_SCALE
    s = s + bias_ref[...][None]
    s_new = jnp.sum(q8 * k_new, axis=2, keepdims=True) * QK_SCALE + aux_ref[:, 0:1][None]
    sink = aux_ref[:, 1:2][None]
    m = jnp.maximum(jnp.maximum(jnp.max(s, axis=2, keepdims=True), s_new), sink)
    e = jnp.exp(s - m)
    e_new = jnp.exp(s_new - m)
    inv = 1.0 / (jnp.sum(e, axis=2, keepdims=True) + e_new + jnp.exp(sink - m))
    o8 = jnp.einsum("bqk,bdk->bqd", e.astype(BF16), vc_ref[...].astype(BF16), preferred_element_type=F32)
    o8 = (o8 + e_new * v_new) * inv
    o8 = jnp.where(lane_half == q_half, jnp.where(q_half == kv_head, o8, pltpu.roll(o8, HEAD_DIM, axis=2)), 0.0)
    for j in range(H // 2):
        o_ref[:, :, 128 * j:128 * (j + 1)] = jnp.sum(jnp.where(row // 2 == j, o8, 0.0), axis=1, keepdims=True)
    k_cols = kv2_ref[:, 0:S_KV_WIDTH].T
    v_cols = kv2_ref[:, S_KV_WIDTH:2 * S_KV_WIDTH].T
    last = lax.broadcasted_iota(jnp.int32, (S_KV_WIDTH, W), 1) == W - 1
    for r in range(R):
        ko_ref[l, r] = jnp.where(last, k_cols[:, r:r + 1], pltpu.roll(kc_ref[r], W - 1, axis=1))
        vo_ref[l, r] = jnp.where(last, v_cols[:, r:r + 1], pltpu.roll(vc_ref[r], W - 1, axis=1))


def _swa_sample(l, z, z3, kc_all, vc_all, bias, aux, prev):
    bsz = z.shape[0]
    R = min(SWA_SAMPLE_BLOCK, bsz)
    W = kc_all.shape[3]
    cache_in = pl.BlockSpec((None, R, S_KV_WIDTH, W), lambda i: (l, i, 0, 0))
    cache_prev = pl.BlockSpec((l, R, S_KV_WIDTH, W), lambda i: (0, i, 0, 0))
    cache_out = pl.BlockSpec((l + 1, R, S_KV_WIDTH, W), lambda i: (0, i, 0, 0))
    return pl.pallas_call(
        _swa_sample_kernel,
        grid=(bsz // R,),
        in_specs=[
            pl.BlockSpec((R, 1, S_WIDTH), lambda i: (i, 0, Z_SQ // S_WIDTH)),
            pl.BlockSpec((R, 1, 2 * S_KV_WIDTH), lambda i: (i, 0, Z_SKV // (2 * S_KV_WIDTH))),
            pl.BlockSpec((R, 2 * S_KV_WIDTH), lambda i: (i, Z_SKV // (2 * S_KV_WIDTH))),
            cache_in, cache_in,
            _const_spec((S_Q_HEADS, 128)), _weight_spec(aux, l),
        ] + [cache_prev] * len(prev),
        out_specs=[pl.BlockSpec((R, 1, S_WIDTH), lambda i: (i, 0, 0)), cache_out, cache_out],
        out_shape=[
            jax.ShapeDtypeStruct((bsz, 1, S_WIDTH), F32),
            jax.ShapeDtypeStruct((l + 1, bsz, S_KV_WIDTH, W), F32),
            jax.ShapeDtypeStruct((l + 1, bsz, S_KV_WIDTH, W), F32),
        ],
        compiler_params=_params("arbitrary"),
        name="swa_sample",
    )(z3, z3, z, kc_all, vc_all, bias, aux, *prev)


def _mix_out_sample_kernel(x_ref, hmt_ref, hc_ref, hs_ref, wout_ref, g2_ref, wq_ref, x1_ref, q_ref):
    cat = jnp.concatenate([hmt_ref[...].T.astype(BF16), hc_ref[...].astype(BF16), hs_ref[...].astype(BF16)], axis=1)
    x1 = x_ref[...] + _dot(cat, wout_ref[...])
    x1_ref[...] = x1
    q_ref[...] = _dot(_rms(x1, g2_ref[...]).astype(BF16), wq_ref[...])


def _mix_out_sample(l, x, hm, hc, hs, wout, g2, wq):
    bsz, d = x.shape
    full = lambda a: pl.BlockSpec(a.shape, lambda i: (0,) * a.ndim)
    args = (x, hm, hc, hs, wout, g2, wq)
    return pl.pallas_call(
        _mix_out_sample_kernel,
        grid=(1,),
        in_specs=[full(x), full(hm), full(hc), full(hs), _weight_spec(wout, l), _weight_spec(g2, l),
                  _weight_spec(wq, l)],
        out_specs=[pl.BlockSpec((bsz, d), lambda i: (0, 0)), pl.BlockSpec((bsz, X_WIDTH), lambda i: (0, 0))],
        out_shape=[jax.ShapeDtypeStruct((bsz, d), F32), jax.ShapeDtypeStruct((bsz, X_WIDTH), F32)],
        compiler_params=_params("arbitrary"),
        name="mix_out_sample",
    )(*args)


def _xattn_sample_kernel(q_ref, k_ref, v_ref, o_ref):
    R = k_ref.shape[0]
    shape = (R, SUBLANES, X_WIDTH)
    row = lax.broadcasted_iota(jnp.int32, shape, 1)
    lane_head = lax.broadcasted_iota(jnp.int32, shape, 2) // HEAD_DIM
    own = row == lane_head
    q8 = jnp.where(own, jnp.broadcast_to(q_ref[...], shape), 0.0).astype(BF16)
    s = jnp.einsum("bqd,bdk->bqk", q8, k_ref[...].astype(BF16), preferred_element_type=F32) * QK_SCALE
    e = jnp.exp(s - jnp.max(s, axis=2, keepdims=True))
    p = (e / jnp.sum(e, axis=2, keepdims=True)).astype(BF16)
    o8 = jnp.einsum("bqk,bdk->bqd", p, v_ref[...].astype(BF16), preferred_element_type=F32)
    o_ref[...] = jnp.sum(jnp.where(own, o8, 0.0), axis=1, keepdims=True)


def _xattn_sample(l, q3, k_all, v_all):
    _, bsz, w, mem = k_all.shape
    R = min(XATTN_SAMPLE_BLOCK, bsz)
    kv = pl.BlockSpec((None, R, w, mem), lambda i: (l, i, 0, 0))
    qo = pl.BlockSpec((R, 1, w), lambda i: (i, 0, 0))
    return pl.pallas_call(
        _xattn_sample_kernel,
        grid=(bsz // R,),
        in_specs=[qo, kv, kv],
        out_specs=qo,
        out_shape=jax.ShapeDtypeStruct((bsz, 1, w), F32),
        compiler_params=_params("arbitrary"),
        name="xattn_sample",
    )(q3, k_all, v_all)


def _ffn_sample_kernel(x1_ref, ox_ref, wo_ref, g3_ref, wup_ref, cw_ref, cb_ref, wdn_ref, gf_ref, hist_ref,
                       *rest, final):
    o_ref, hist_out_ref, act = rest[-3:]
    _copy_earlier_layers(rest[:-3], (hist_out_ref,))
    l = hist_out_ref.shape[0] - 1
    dff = wdn_ref.shape[0]
    x = x1_ref[...] + _dot(ox_ref[...].astype(BF16), wo_ref[...])
    h = _rms(x, g3_ref[...]).astype(BF16)
    for c in range(0, dff, FFN_CHUNK):
        a = _dot(h, wup_ref[:, c:c + FFN_CHUNK])
        g = _dot(h, wup_ref[:, dff + c:dff + c + FFN_CHUNK])
        h1 = hist_ref[:, 1, c:c + FFN_CHUNK]
        gc = (cw_ref[0:1, c:c + FFN_CHUNK] * hist_ref[:, 0, c:c + FFN_CHUNK]
              + cw_ref[1:2, c:c + FFN_CHUNK] * h1
              + cw_ref[2:3, c:c + FFN_CHUNK] * g + cb_ref[:, c:c + FFN_CHUNK])
        act[:, c:c + FFN_CHUNK] = (_swish(gc) * a).astype(BF16)
        hist_out_ref[l, :, 0, c:c + FFN_CHUNK] = h1
        hist_out_ref[l, :, 1, c:c + FFN_CHUNK] = g
    y = x + _dot(act[...], wdn_ref[...])
    o_ref[...] = _rms(y, gf_ref[...]) if final else y


def _ffn_sample(l, x1, ox, wo, g3, wup, cw, cb, wdn, gf, hist_all, prev, *, final):
    bsz, d = x1.shape
    dff = wdn.shape[1]
    full = lambda a: pl.BlockSpec(a.shape, lambda i: (0,) * a.ndim)
    hshape = hist_all.shape[1:]
    return pl.pallas_call(
        functools.partial(_ffn_sample_kernel, final=final),
        grid=(1,),
        in_specs=[full(x1), full(ox), _weight_spec(wo, l), _weight_spec(g3, l), _weight_spec(wup, l),
                  _weight_spec(cw, l), _weight_spec(cb, l),
                  _weight_spec(wdn, l), full(gf), pl.BlockSpec((None,) + hshape, lambda i: (l, 0, 0, 0))]
                 + [full(p) for p in prev],
        out_specs=[pl.BlockSpec((bsz, d), lambda i: (0, 0)), pl.BlockSpec((l + 1,) + hshape, lambda i: (0, 0, 0, 0))],
        out_shape=[jax.ShapeDtypeStruct((bsz, d), F32), jax.ShapeDtypeStruct((l + 1,) + hshape, F32)],
        scratch_shapes=[pltpu.VMEM((bsz, dff), BF16)],
        compiler_params=_params("arbitrary"),
        name="ffn_sample",
    )(x1, ox, wo, g3, wup, cw, cb, wdn, gf, hist_all, *prev)


def _t5_buckets(dist):
    n = np.maximum(dist, 0)
    max_exact = N_BUCKETS // 2
    nf = np.maximum(n, max_exact).astype(np.float32)
    large = max_exact + (np.log(nf / np.float32(max_exact)) / np.float32(math.log(MAX_DISTANCE / max_exact))
                         * np.float32(N_BUCKETS - max_exact)).astype(np.int32)
    return np.where(n < max_exact, n, np.minimum(large, N_BUCKETS - 1))


def _prompt_buckets():
    W = WINDOW
    dist = np.arange(W)[None, :] + W - np.arange(2 * W)[:, None]
    band = (dist >= 0) & (dist < W)
    return np.where(band, _t5_buckets(dist), -1).astype(np.int32)


def _swa_tables(rel_bias):
    W = WINDOW
    dist_c = W - np.arange(W)
    tab = jnp.transpose(rel_bias[_t5_buckets(dist_c)], (1, 0))
    cache_bias = jnp.where((dist_c < W)[None], tab, NEG_INF)
    return cache_bias, rel_bias[0]


def kernel(x_prompt, x_sample, mem_prompt, state_mlstm_C, state_mlstm_n, state_mlstm_m, state_conv, cache_swa_k, cache_swa_v, cache_mem_k, cache_mem_v, state_ffn_conv, rel_bias, norm1_g, w_in, b_i, b_f, mlstm_norm_g, conv_w, conv_b, conv_ln_g, conv_ln_b, swa_sinks, w_out, norm2_g, w_xq, w_xk, w_xv, w_xo, norm3_g, w_up, ffn_conv_w, ffn_conv_b, w_down, final_norm_g):
    depth = w_in.shape[0]
    bp, seq, d = x_prompt.shape
    bs = x_sample.shape[0]
    mem = mem_prompt.shape[1]
    dff = w_down.shape[1]
    W = WINDOW
    nh = C_KERNEL - 1

    xp = x_prompt
    xs = x_sample.reshape(bs, d)
    gf = final_norm_g.reshape(1, d)
    c_all = jnp.transpose(state_mlstm_C, (0, 2, 3, 4, 1))
    n_all = jnp.transpose(state_mlstm_n, (0, 2, 3, 1))
    m_all = jnp.transpose(state_mlstm_m, (0, 2, 1))
    hist_all = jnp.transpose(state_conv, (0, 2, 1, 3))
    kc_all = jnp.transpose(cache_swa_k, (0, 1, 3, 4, 2)).reshape(depth, bs, S_KV_WIDTH, W)
    vc_all = jnp.transpose(cache_swa_v, (0, 1, 3, 4, 2)).reshape(depth, bs, S_KV_WIDTH, W)
    mk_all = jnp.transpose(cache_mem_k, (0, 1, 3, 4, 2)).reshape(depth, bs, X_WIDTH, mem)
    mv_all = jnp.transpose(cache_mem_v, (0, 1, 3, 4, 2)).reshape(depth, bs, X_WIDTH, mem)
    pm_c, pm_n, pm_m, p_conv, p_k, p_v, p_mk, p_mv, p_ffn = ([] for _ in range(9))
    s_mlstm, s_conv, s_kv, s_ffn = [], [], [], []

    win_t = jnp.swapaxes(w_in, 1, 2)
    wout = w_out.astype(BF16)
    wxq = w_xq.astype(BF16)
    wxkv = jnp.concatenate([w_xk, w_xv], axis=2).astype(BF16)
    wxo = w_xo.astype(BF16)
    wup = w_up.astype(BF16)
    wdn = w_down.astype(BF16)
    bucket = jnp.asarray(_prompt_buckets())

    rows = lambda a: a.reshape(depth, 1, -1)
    g1, g2, g3 = rows(norm1_g), rows(norm2_g), rows(norm3_g)
    gb8 = jnp.concatenate([b_i, b_f], axis=1)
    gbias = rows(jnp.concatenate([gb8, jnp.zeros((depth, 128 - 2 * M_HEADS), F32)], axis=1))
    ng = rows(mlstm_norm_g)
    ngt = jnp.broadcast_to(mlstm_norm_g[:, :, None], (depth, M_WIDTH, bs))
    cw = jnp.concatenate([conv_w, jnp.zeros((depth, CONV_PAD - C_KERNEL, C_WIDTH), F32)], axis=1)
    cb, lg, lb = rows(conv_b), rows(conv_ln_g), rows(conv_ln_b)
    fcw = jnp.concatenate([ffn_conv_w, jnp.zeros((depth, SUBLANES - FFN_KERNEL, dff), F32)], axis=1)
    fcb = rows(ffn_conv_b)
    cache_bias, bias0 = _swa_tables(rel_bias)
    aux = jnp.concatenate([jnp.broadcast_to(bias0[None, :, None], (depth, S_Q_HEADS, 1)), swa_sinks[:, :, None],
                           jnp.zeros((depth, S_Q_HEADS, 126), F32)], axis=2)

    for l in range(depth):
        last = l == depth - 1

        mkv = _mem_kv(l, mem_prompt, wxkv)
        *zs_prompt, win = _in_proj(l, xp.reshape(bp * seq, d), g1, win_t, tm=1024)
        zm, zc, zq, zkv, zg = (a.reshape(bp, seq, a.shape[1]) for a in zs_prompt)
        hm, cpair, npair, mm = _mlstm_prompt(l, zm, zg, gbias, ng)
        hc, ctail = _conv_prompt(l, zc, cw, cb, lg, lb, tc=min(1024, seq))
        hs, kt, vt = _swa_prompt(l, zq, zkv, rel_bias, swa_sinks, bucket, qb=min(8, seq // W))
        xp = _mix_out_prompt(l, xp, hm, hc, hs, wout, g2, wxq, mkv, wxo, tm=1024)
        xp, ftail = _ffn_prompt(l, xp, g3, wup, fcw, fcb, wdn, gf, tm=512, final=last)
        half = lambda h: slice(HEAD_DIM * (h % 2), HEAD_DIM * (h % 2 + 1))
        pm_c.append(jnp.stack([jnp.swapaxes(cpair[:, h // 2, half(h), half(h)], 1, 2) for h in range(M_HEADS)], axis=1))
        pm_n.append(jnp.stack([npair[:, h // 2, h % 2, half(h)] for h in range(M_HEADS)], axis=1))
        pm_m.append(mm[:, :, 0, 0])
        p_conv.append(ctail[:, CONV_PAD - nh:, :])
        p_k.append(jnp.transpose(kt.reshape(bp, S_KV_HEADS, HEAD_DIM, W), (0, 3, 1, 2)))
        p_v.append(jnp.transpose(vt.reshape(bp, S_KV_HEADS, HEAD_DIM, W), (0, 3, 1, 2)))
        p_mk.append(jnp.transpose(mkv[:, 0:X_WIDTH, :].reshape(bp, X_HEADS, HEAD_DIM, mem), (0, 3, 1, 2)))
        p_mv.append(jnp.transpose(mkv[:, X_WIDTH:, :].reshape(bp, X_HEADS, HEAD_DIM, mem), (0, 3, 1, 2)))
        p_ffn.append(ftail[:, FFN_PAD - (FFN_KERNEL - 1):, :])

        zs, zst = _sample_in(l, xs, g1, win)
        hmt_s, *s_mlstm = _mlstm_sample(l, zst, gb8, ngt, c_all, n_all, m_all, s_mlstm)
        hc_s, *s_conv = _conv_sample(l, zs, hist_all, cw, cb, lg, lb, s_conv)
        hs_s, *s_kv = _swa_sample(l, zs, zs.reshape(bs, 1, Z_WIDTH), kc_all, vc_all, cache_bias, aux, s_kv)
        x1, qx = _mix_out_sample(l, xs, hmt_s, hc_s, hs_s.reshape(bs, S_WIDTH), wout, g2, wxq)
        ox = _xattn_sample(l, qx.reshape(bs, 1, X_WIDTH), mk_all, mv_all)
        xs, *s_ffn = _ffn_sample(l, x1, ox.reshape(bs, X_WIDTH), wxo, g3, wup, fcw, fcb, wdn, gf,
                                 state_ffn_conv, s_ffn, final=last)

    st = jnp.stack
    tr = jnp.transpose
    s_c, s_n, s_m = s_mlstm
    s_k, s_v = (a.reshape(depth, bs, S_KV_HEADS, HEAD_DIM, W) for a in s_kv)
    return (xp, xs.reshape(bs, 1, d),
            st(pm_c), st(pm_n), st(pm_m), st(p_conv), st(p_k), st(p_v), st(p_mk), st(p_mv), st(p_ffn),
            tr(s_c, (0, 4, 1, 2, 3)), tr(s_n, (0, 3, 1, 2)), tr(s_m, (0, 2, 1)),
            tr(s_conv[0], (0, 2, 1, 3)), tr(s_k, (0, 1, 4, 2, 3)), tr(s_v, (0, 1, 4, 2, 3)),
            s_ffn[0])
```

```python
import functools
import math

import numpy as np
import jax
import jax.numpy as jnp
from jax import lax
from jax.experimental import pallas as pl
from jax.experimental.pallas import tpu as pltpu

F32 = jnp.float32
BF16 = jnp.bfloat16
EPS = 1e-6
NEG_INF = float("-inf")

HEAD_DIM = 64
M_HEADS = 4
M_WIDTH = M_HEADS * HEAD_DIM
C_WIDTH = 256
C_KERNEL = 31
S_Q_HEADS = 8
S_KV_HEADS = 2
S_WIDTH = S_Q_HEADS * HEAD_DIM
S_KV_WIDTH = S_KV_HEADS * HEAD_DIM
WINDOW = 128
N_BUCKETS = 32
MAX_DISTANCE = 128
X_HEADS = 4
X_WIDTH = X_HEADS * HEAD_DIM
FFN_KERNEL = 3
QK_SCALE = HEAD_DIM ** -0.5
LOG2E = math.log2(math.e)

Z_M = 0
Z_C = 4 * M_WIDTH
Z_SQ = Z_C + 2 * C_WIDTH
Z_SKV = Z_SQ + S_WIDTH
Z_G = Z_SKV + 2 * S_KV_WIDTH
LANES = 128
SUBLANES = 8
Z_WIDTH = Z_G + LANES
VMEM_LIMIT = 56 * 1024 * 1024

M_CHUNK = 128
FFN_CHUNK = 256


def _params(*sem):
    return pltpu.CompilerParams(dimension_semantics=sem, vmem_limit_bytes=VMEM_LIMIT)


def _const_spec(shape):
    nd = len(shape)
    return pl.BlockSpec(shape, lambda *_: (0,) * nd, pipeline_mode=pl.Buffered(1))


def _weight_spec(w, l):
    nd = w.ndim - 1
    return pl.BlockSpec((None,) + w.shape[1:], lambda *_: (l,) + (0,) * nd, pipeline_mode=pl.Buffered(1))


def _rms(x, g):
    return x * lax.rsqrt(jnp.mean(x * x, axis=-1, keepdims=True) + EPS) * g


def _sigmoid(x):
    return 0.5 * jnp.tanh(0.5 * x) + 0.5


def _swish(x):
    h = 0.5 * x
    return h + h * jnp.tanh(h)


def _log_sigmoid(x):
    return jnp.minimum(x, 0.0) - jnp.log1p(jnp.exp(-jnp.abs(x)))


def _dot(a, b):
    return jnp.dot(a, b, preferred_element_type=F32)


def _dot_nt(a, b):
    return lax.dot_general(a, b, (((1,), (1,)), ((), ())), preferred_element_type=F32)


def _dot_tn(a, b):
    return lax.dot_general(a, b, (((0,), (0,)), ((), ())), preferred_element_type=F32)


IN_PROJ_ROWS = 512
IN_PROJ_COLS = 512
IN_GATES = 4 * M_WIDTH
W_PREP_ROWS = 256


def _in_proj_kernel(x_ref, g_ref, wt_ref, zm_ref, zc_ref, zq_ref, zkv_ref, zg_ref, wprep_ref, w_scr):
    d = x_ref.shape[1]

    @pl.when(pl.program_id(0) == 0)
    def _():
        n_gate = 2 * M_HEADS
        for src, dst, n in ((0, 0, IN_GATES), (IN_GATES + n_gate, IN_GATES, Z_G - IN_GATES)):
            for c in range(0, n, W_PREP_ROWS):
                w_scr[:, dst + c:dst + c + W_PREP_ROWS] = wt_ref[src + c:src + c + W_PREP_ROWS, :].T.astype(BF16)
        gate_rows = jnp.concatenate([wt_ref[IN_GATES:IN_GATES + n_gate, :], jnp.zeros((LANES - n_gate, d), F32)],
                                    axis=0)
        w_scr[:, Z_G:Z_G + LANES] = gate_rows.T.astype(BF16)
        wprep_ref[...] = w_scr[...]

    tm = x_ref.shape[0]
    sub = min(IN_PROJ_ROWS, tm)
    outs = ((zm_ref, Z_M), (zc_ref, Z_C), (zq_ref, Z_SQ), (zkv_ref, Z_SKV), (zg_ref, Z_G))
    for r0 in range(0, tm, sub):
        h = _rms(x_ref[r0:r0 + sub, :], g_ref[...]).astype(BF16)
        for o_ref, off in outs:
            n = o_ref.shape[1]
            for c in range(0, n, IN_PROJ_COLS):
                w = min(IN_PROJ_COLS, n - c)
                o_ref[r0:r0 + sub, c:c + w] = _dot(h, w_scr[:, off + c:off + c + w])


IN_PROJ_WIDTHS = (4 * M_WIDTH, 2 * C_WIDTH, S_WIDTH, 2 * S_KV_WIDTH, LANES)


def _in_proj(l, x, g, wt, *, tm):
    m, d = x.shape
    tm = min(tm, m)
    return pl.pallas_call(
        _in_proj_kernel,
        grid=(m // tm,),
        in_specs=[pl.BlockSpec((tm, d), lambda i: (i, 0)), _weight_spec(g, l), _weight_spec(wt, l)],
        out_specs=[pl.BlockSpec((tm, w), lambda i: (i, 0)) for w in IN_PROJ_WIDTHS]
                  + [pl.BlockSpec((d, Z_WIDTH), lambda i: (0, 0))],
        out_shape=[jax.ShapeDtypeStruct((m, w), F32) for w in IN_PROJ_WIDTHS]
                  + [jax.ShapeDtypeStruct((d, Z_WIDTH), BF16)],
        scratch_shapes=[pltpu.VMEM((d, Z_WIDTH), BF16)],
        compiler_params=_params("arbitrary"),
        name="in_proj",
    )(x, g, wt)


def _mem_kv_kernel(x_ref, w_ref, o_ref, acc):
    acc[...] = _dot(x_ref[...].astype(BF16), w_ref[...])
    o_ref[...] = acc[...].T


def _mem_kv(l, mem3, w):
    b, mem, d = mem3.shape
    n = w.shape[2]
    return pl.pallas_call(
        _mem_kv_kernel,
        grid=(b,),
        in_specs=[pl.BlockSpec((None, mem, d), lambda i: (i, 0, 0)), _weight_spec(w, l)],
        out_specs=pl.BlockSpec((None, n, mem), lambda i: (i, 0, 0)),
        out_shape=jax.ShapeDtypeStruct((b, n, mem), F32),
        scratch_shapes=[pltpu.VMEM((mem, n), F32)],
        compiler_params=_params("arbitrary"),
        name="mem_kv",
    )(mem3, w)


def _sample_in_kernel(x_ref, g_ref, w_ref, z_ref, zt_ref):
    h = _rms(x_ref[...], g_ref[...]).astype(BF16)
    n = z_ref.shape[1]
    for c in range(0, n, IN_PROJ_COLS):
        w = min(IN_PROJ_COLS, n - c)
        zc = _dot(h, w_ref[:, c:c + w])
        z_ref[:, c:c + w] = zc
        zt_ref[c:c + w, :] = zc.T


def _sample_in(l, x, g, w):
    m, d = x.shape
    n = w.shape[1]
    full = lambda shape: pl.BlockSpec(shape, lambda i: (0,) * len(shape))
    return pl.pallas_call(
        _sample_in_kernel,
        grid=(1,),
        in_specs=[full((m, d)), _weight_spec(g, l), full((d, n))],
        out_specs=[full((m, n)), full((n, m))],
        out_shape=[jax.ShapeDtypeStruct((m, n), F32), jax.ShapeDtypeStruct((n, m), F32)],
        compiler_params=_params("arbitrary"),
        name="sample_in",
    )(x, g, w)


def _mlstm_prompt_kernel(z_ref, g_ref, gb_ref, ng_ref, h_ref, cp_ref, np_ref, m_ref,
                         cp_scr, np_scr, m_scr, ht_scr, st_scr, kw_scr):
    NB, L = z_ref.shape[0], z_ref.shape[1]
    D = HEAD_DIM
    hi = lax.Precision.HIGHEST

    @pl.when(pl.program_id(0) == 0)
    def _():
        cp_scr[...] = jnp.zeros(cp_scr.shape, F32)
        np_scr[...] = jnp.zeros(np_scr.shape, F32)
        m_scr[...] = jnp.zeros(m_scr.shape, F32)

    src = lax.broadcasted_iota(jnp.int32, (L, L), 0)
    qry = lax.broadcasted_iota(jnp.int32, (L, L), 1)
    causal_t = src <= qry
    upper = jnp.where(causal_t, 1.0, 0.0)
    lane_half = lax.broadcasted_iota(jnp.int32, (L, 128), 1) // D
    row8 = lax.broadcasted_iota(jnp.int32, (SUBLANES, 128), 0)

    rows, cols = {}, []
    for b in range(NB):
        g_t = (g_ref[b] + gb_ref[...]).T[0:SUBLANES, :]
        b_rows = jnp.dot(_log_sigmoid(g_t), upper, precision=hi, preferred_element_type=F32)
        to_cols = []
        for h in range(M_HEADS):
            b_row = b_rows[M_HEADS + h:M_HEADS + h + 1, :]
            ci_row = g_t[h:h + 1, :] - b_row
            m_prev = m_scr[b, h, 0:1, 0:1]
            b_last = b_row[:, L - 1:L]
            m_new = jnp.maximum(b_last + m_prev, jnp.max(b_last + ci_row, axis=1, keepdims=True))
            rows[b, h] = dict(b_row=b_row, m_prev=m_prev, m_new=m_new, decay=jnp.exp(b_last + m_prev - m_new))
            to_cols += [ci_row, jnp.exp(b_last + ci_row - m_new)]
        to_cols.append(jnp.zeros((128 - 2 * M_HEADS, L), F32))
        cols.append(jnp.concatenate(to_cols, axis=0).T)

    stats = {}
    for b in range(NB):
        for j in range(M_HEADS // 2):
            slab = 128 * j
            q2 = z_ref[b, :, slab:slab + 128].astype(BF16)
            k2 = z_ref[b, :, M_WIDTH + slab:M_WIDTH + slab + 128] * QK_SCALE
            for par in range(2):
                h = 2 * j + par
                r = rows[b, h]
                ci_col = cols[b][:, 2 * h:2 * h + 1]
                a_row = r["b_row"] + r["m_prev"]
                dm = jnp.where(causal_t, r["b_row"] + ci_col, NEG_INF)
                m_row = jnp.maximum(a_row, jnp.max(dm, axis=0, keepdims=True))
                k_own = jnp.where(lane_half == par, k2, 0.0).astype(BF16)
                s_t = _dot_nt(k_own, q2) * jnp.exp(dm - m_row)
                st_scr[b, j, :, L * par:L * (par + 1)] = s_t.astype(BF16)
                stats[b, h] = dict(w_inter=jnp.exp(a_row - m_row), floor=jnp.exp(-m_row),
                                   den_s=jnp.sum(s_t, axis=0, keepdims=True),
                                   decay=r["decay"], m_new=r["m_new"])
            wk = jnp.where(lane_half == 0, cols[b][:, 4 * j + 1:4 * j + 2], cols[b][:, 4 * j + 3:4 * j + 4])
            kw = k2 * wk
            kw_scr[b, j] = kw.astype(BF16)
            stats[b, j, "k_sum"] = jnp.sum(kw, axis=0, keepdims=True)

    block_diag = (lax.broadcasted_iota(jnp.int32, (128, 128), 0) // D
                  == lax.broadcasted_iota(jnp.int32, (128, 128), 1) // D)
    for b in range(NB):
        for j in range(M_HEADS // 2):
            slab = 128 * j
            q2 = z_ref[b, :, slab:slab + 128].astype(BF16)
            v2 = z_ref[b, :, 2 * M_WIDTH + slab:2 * M_WIDTH + slab + 128].astype(BF16)
            cp = cp_scr[b, j]
            npair = np_scr[b, j]
            qc = _dot_nt(cp.astype(BF16), q2)
            qn = _dot_nt(npair.astype(BF16), q2)
            pv = _dot_tn(v2, st_scr[b, j])
            upd = jnp.where(block_diag, _dot_tn(v2, kw_scr[b, j]), 0.0)
            n_new = jnp.zeros((SUBLANES, 128), F32)
            for par in range(2):
                h = 2 * j + par
                st = stats[b, h]
                hr = slice(D * par, D * (par + 1))
                num = st["w_inter"] * qc[hr, :] + pv[hr, L * par:L * (par + 1)]
                den = st["w_inter"] * qn[par:par + 1, :] + st["den_s"]
                hh = num * (1.0 / jnp.maximum(jnp.abs(den), st["floor"]))
                hh = hh * lax.rsqrt(jnp.mean(hh * hh, axis=0, keepdims=True) + EPS)
                ht_scr[b, D * h:D * (h + 1), :] = hh
                cp_scr[b, j, hr, :] = st["decay"] * cp[hr, :] + upd[hr, :]
                k_sum = jnp.where(lane_half[0:1, :] == par, stats[b, j, "k_sum"], 0.0)
                n_new = jnp.where(row8 == par, st["decay"] * npair[par:par + 1, :] + k_sum, n_new)
                m_scr[b, h] = jnp.broadcast_to(st["m_new"], m_scr.shape[2:])
            np_scr[b, j] = n_new
        o_gate = _sigmoid(z_ref[b, :, 3 * M_WIDTH:4 * M_WIDTH])
        h_ref[b] = ht_scr[b].T * ng_ref[...] * o_gate

    cp_ref[...] = cp_scr[...]
    np_ref[...] = np_scr[...]
    m_ref[...] = m_scr[...]


def _mlstm_prompt(l, zm3, zg3, gbias, ng):
    b, s, _ = zm3.shape
    L = M_CHUNK
    P = M_HEADS // 2
    return pl.pallas_call(
        _mlstm_prompt_kernel,
        grid=(s // L,),
        in_specs=[
            pl.BlockSpec((b, L, 4 * M_WIDTH), lambda c: (0, c, 0)),
            pl.BlockSpec((b, L, 128), lambda c: (0, c, 0)),
            _weight_spec(gbias, l),
            _weight_spec(ng, l),
        ],
        out_specs=[
            pl.BlockSpec((b, L, M_WIDTH), lambda c: (0, c, 0)),
            pl.BlockSpec((b, P, 128, 128), lambda c: (0, 0, 0, 0)),
            pl.BlockSpec((b, P, SUBLANES, 128), lambda c: (0, 0, 0, 0)),
            pl.BlockSpec((b, M_HEADS, SUBLANES, 128), lambda c: (0, 0, 0, 0)),
        ],
        out_shape=[
            jax.ShapeDtypeStruct((b, s, M_WIDTH), F32),
            jax.ShapeDtypeStruct((b, P, 128, 128), F32),
            jax.ShapeDtypeStruct((b, P, SUBLANES, 128), F32),
            jax.ShapeDtypeStruct((b, M_HEADS, SUBLANES, 128), F32),
        ],
        scratch_shapes=[
            pltpu.VMEM((b, P, 128, 128), F32),
            pltpu.VMEM((b, P, SUBLANES, 128), F32),
            pltpu.VMEM((b, M_HEADS, SUBLANES, 128), F32),
            pltpu.VMEM((b, M_WIDTH, L), F32),
            pltpu.VMEM((b, P, L, 2 * L), BF16),
            pltpu.VMEM((b, P, L, 128), BF16),
        ],
        compiler_params=_params("arbitrary"),
        name="mlstm_prompt",
    )(zm3, zg3, gbias, ng)


CONV_PAD = 32


def _conv_prompt_kernel(z_ref, w_ref, cb_ref, lg_ref, lb_ref, h_ref, tail_ref, buf, shifted):
    tc = z_ref.shape[0]
    t = pl.program_id(1)

    @pl.when(t == 0)
    def _():
        buf[0:CONV_PAD, :] = jnp.zeros((CONV_PAD, C_WIDTH), F32)

    @pl.when(t > 0)
    def _():
        buf[0:CONV_PAD, :] = buf[tc:tc + CONV_PAD, :]

    u = z_ref[:, 0:C_WIDTH] * _sigmoid(z_ref[:, C_WIDTH:2 * C_WIDTH])
    buf[CONV_PAD:CONV_PAD + tc, :] = u
    n_sh = shifted.shape[1]
    for r in range(1, SUBLANES):
        shifted[r - 1] = buf[r:r + n_sh, :]
    off = CONV_PAD - (C_KERNEL - 1)
    acc = jnp.broadcast_to(cb_ref[...], (tc, C_WIDTH))
    for j in range(C_KERNEL):
        r = (off + j) % SUBLANES
        base = off + j - r
        win = buf[base:base + tc, :] if r == 0 else shifted[r - 1, base:base + tc, :]
        acc = acc + w_ref[j:j + 1, :] * win
    mu = jnp.mean(acc, axis=-1, keepdims=True)
    xc = acc - mu
    y = xc * lax.rsqrt(jnp.mean(xc * xc, axis=-1, keepdims=True) + EPS) * lg_ref[...] + lb_ref[...]
    h_ref[...] = _swish(y)
    tail_ref[...] = buf[tc:tc + CONV_PAD, :]


def _conv_prompt(l, z3, w, cb, lg, lb, *, tc):
    b, s, _ = z3.shape
    return pl.pallas_call(
        _conv_prompt_kernel,
        grid=(b, s // tc),
        in_specs=[
            pl.BlockSpec((None, tc, 2 * C_WIDTH), lambda i, t: (i, t, 0)),
            _weight_spec(w, l), _weight_spec(cb, l), _weight_spec(lg, l), _weight_spec(lb, l),
        ],
        out_specs=[
            pl.BlockSpec((None, tc, C_WIDTH), lambda i, t: (i, t, 0)),
            pl.BlockSpec((None, CONV_PAD, C_WIDTH), lambda i, t: (i, 0, 0)),
        ],
        out_shape=[
            jax.ShapeDtypeStruct((b, s, C_WIDTH), F32),
            jax.ShapeDtypeStruct((b, CONV_PAD, C_WIDTH), F32),
        ],
        scratch_shapes=[pltpu.VMEM((CONV_PAD + tc, C_WIDTH), F32),
                        pltpu.VMEM((SUBLANES - 1, CONV_PAD + tc - SUBLANES, C_WIDTH), F32)],
        compiler_params=_params("arbitrary", "arbitrary"),
        name="conv_prompt",
    )(z3, w, cb, lg, lb)


def _swa_prompt_kernel(rb_ref, sink_ref, bucket_ref, q_ref, kv_ref, kvp_ref, o_ref, kt_ref, vt_ref,
                       bias_scr, s_scr, p_scr, ot_scr, *, layer):
    W = WINDOW
    H = S_Q_HEADS
    G = H // S_KV_HEADS
    i = pl.program_id(0)
    n = pl.program_id(1)
    from_prev = lax.broadcasted_iota(jnp.int32, (W, W), 0) > lax.broadcasted_iota(jnp.int32, (W, W), 1)
    eye = jnp.where(lax.broadcasted_iota(jnp.int32, (W, W), 0) == lax.broadcasted_iota(jnp.int32, (W, W), 1),
                    1.0, 0.0).astype(BF16)

    @pl.when(jnp.logical_and(i == 0, n == 0))
    def _():
        bucket = bucket_ref[...]
        for h in range(H):
            acc = jnp.full((2 * W, W), NEG_INF, F32)
            for b in range(N_BUCKETS):
                acc = jnp.where(bucket == b, rb_ref[b, h], acc)
            acc = acc * LOG2E
            bias_scr[0, h] = jnp.where(from_prev, acc[0:W, :], acc[W:2 * W, :])
            bias_scr[1, h] = jnp.where(from_prev, NEG_INF, acc[W:2 * W, :])

    QB = q_ref.shape[0] // W
    NK = (QB + 1) * W
    first = jnp.where(n == 0, 1, 0)
    kk = jnp.concatenate([kvp_ref[:, 0:S_KV_WIDTH], kv_ref[:, 0:S_KV_WIDTH]], axis=0) * (QK_SCALE * LOG2E)
    vv = jnp.concatenate([kvp_ref[:, S_KV_WIDTH:2 * S_KV_WIDTH], kv_ref[:, S_KV_WIDTH:2 * S_KV_WIDTH]], axis=0)
    kk_r = pltpu.roll(kk, HEAD_DIM, axis=1)
    lo_lane = lax.broadcasted_iota(jnp.int32, (NK, S_KV_WIDTH), 1) < HEAD_DIM
    k_var = [[jnp.where(lo_lane, kk, 0.0).astype(BF16), jnp.where(lo_lane, 0.0, kk_r).astype(BF16)],
             [jnp.where(lo_lane, kk_r, 0.0).astype(BF16), jnp.where(lo_lane, 0.0, kk).astype(BF16)]]
    v_t = vv.T.astype(BF16)

    for j in range(QB):
        k0 = j * W
        masked = first if j == 0 else 0
        m_rows = [None] * H
        for hk in range(S_KV_HEADS):
            c0 = 2 * 128 * hk
            q_st = jnp.concatenate([q_ref[k0:k0 + W, c0:c0 + 128], q_ref[k0:k0 + W, c0 + 128:c0 + 256]],
                                   axis=0).astype(BF16)
            for half in range(2):
                s_t = _dot_nt(k_var[hk][half][k0:k0 + 2 * W, :], q_st)
                for slab in range(2):
                    head = G * hk + 2 * slab + half
                    lanes = slice(128 * slab, 128 * (slab + 1))
                    sb = jnp.where(from_prev, s_t[0:W, lanes], s_t[W:2 * W, lanes]) + bias_scr[masked, head]
                    s_scr[j, head] = sb
                    m_rows[head] = jnp.maximum(jnp.max(sb, axis=0, keepdims=True), sink_ref[layer, head] * LOG2E)

        inv = [None] * H
        for head in range(H):
            e = jnp.exp2(s_scr[j, head] - m_rows[head])
            den = jnp.sum(e, axis=0, keepdims=True) + jnp.exp2(sink_ref[layer, head] * LOG2E - m_rows[head])
            inv[head] = 1.0 / den
            p_scr[j, 0:W, 128 * head:128 * (head + 1)] = jnp.where(from_prev, e, 0.0).astype(BF16)
            p_scr[j, W:2 * W, 128 * head:128 * (head + 1)] = jnp.where(from_prev, 0.0, e).astype(BF16)

        for hk in range(S_KV_HEADS):
            o_t = _dot(v_t[HEAD_DIM * hk:HEAD_DIM * (hk + 1), k0:k0 + 2 * W],
                       p_scr[j, :, 128 * G * hk:128 * G * (hk + 1)])
            for g in range(G):
                head = G * hk + g
                ot_scr[j, HEAD_DIM * head:HEAD_DIM * (head + 1), :] = (o_t[:, 128 * g:128 * (g + 1)]
                                                                       * inv[head]).astype(BF16)
        o_ref[k0:k0 + W, :] = _dot_nt(eye, ot_scr[j])

    @pl.when(n == pl.num_programs(1) - 1)
    def _():
        kt_ref[...] = kv_ref[(QB - 1) * W:QB * W, 0:S_KV_WIDTH].T
        vt_ref[...] = kv_ref[(QB - 1) * W:QB * W, S_KV_WIDTH:2 * S_KV_WIDTH].T


def _swa_prompt(l, zq3, zkv3, rel_bias, sinks, bucket, *, qb):
    b, s, _ = zq3.shape
    W = WINDOW
    smem = pl.BlockSpec(memory_space=pltpu.SMEM)
    return pl.pallas_call(
        functools.partial(_swa_prompt_kernel, layer=l),
        grid=(b, s // (qb * W)),
        in_specs=[
            smem, smem, _const_spec((2 * W, W)),
            pl.BlockSpec((None, qb * W, S_WIDTH), lambda i, n: (i, n, 0)),
            pl.BlockSpec((None, qb * W, 2 * S_KV_WIDTH), lambda i, n: (i, n, 0)),
            pl.BlockSpec((None, W, 2 * S_KV_WIDTH), lambda i, n: (i, jnp.maximum(n * qb - 1, 0), 0)),
        ],
        out_specs=[
            pl.BlockSpec((None, qb * W, S_WIDTH), lambda i, n: (i, n, 0)),
            pl.BlockSpec((None, S_KV_WIDTH, W), lambda i, n: (i, 0, 0)),
            pl.BlockSpec((None, S_KV_WIDTH, W), lambda i, n: (i, 0, 0)),
        ],
        out_shape=[
            jax.ShapeDtypeStruct((b, s, S_WIDTH), F32),
            jax.ShapeDtypeStruct((b, S_KV_WIDTH, W), F32),
            jax.ShapeDtypeStruct((b, S_KV_WIDTH, W), F32),
        ],
        scratch_shapes=[
            pltpu.VMEM((2, S_Q_HEADS, W, W), F32),
            pltpu.VMEM((qb, S_Q_HEADS, W, W), F32),
            pltpu.VMEM((qb, 2 * W, S_Q_HEADS * W), BF16),
            pltpu.VMEM((qb, S_WIDTH, W), BF16),
        ],
        compiler_params=_params("arbitrary", "arbitrary"),
        name="swa_prompt",
    )(rel_bias, sinks, bucket, zq3, zkv3, zkv3)


MIX_SUB = 512


def _mix_out_prompt_kernel(x_ref, hm_ref, hc_ref, hs_ref, wout_ref, g2_ref, wq_ref, mk_ref, mv_ref, wo_ref, o_ref,
                           x1_scr, qx_scr, p_scr):
    tm = x_ref.shape[0]
    mem = mk_ref.shape[1]
    mk_t = mk_ref[...]
    mv_t = mv_ref[...]
    row_head = lax.broadcasted_iota(jnp.int32, mk_t.shape, 0) // HEAD_DIM
    k_heads = [jnp.where(row_head == h, mk_t, 0.0).astype(BF16) for h in range(X_HEADS)]
    v_cat = jnp.concatenate([jnp.where(row_head == h, mv_t, 0.0).astype(BF16) for h in range(X_HEADS)], axis=1)
    subs = [slice(r0, r0 + MIX_SUB) for r0 in range(0, tm, MIX_SUB)]
    for rows in subs:
        cat = jnp.concatenate([hm_ref[rows, :].astype(BF16), hc_ref[rows, :].astype(BF16),
                               hs_ref[rows, :].astype(BF16)], axis=1)
        x1 = x_ref[rows, :] + _dot(cat, wout_ref[...])
        x1_scr[rows, :] = x1
        qx = _dot(_rms(x1, g2_ref[...]).astype(BF16), wq_ref[...])
        qx_scr[rows, :] = (qx * (QK_SCALE * LOG2E)).astype(BF16)
    for rows in subs:
        qx = qx_scr[rows, :]
        for h in range(X_HEADS):
            s = _dot(qx, k_heads[h])
            e = jnp.exp2(s - jnp.max(s, axis=1, keepdims=True))
            p_scr[rows, mem * h:mem * (h + 1)] = (e * (1.0 / jnp.sum(e, axis=1, keepdims=True))).astype(BF16)
    for rows in subs:
        o = _dot_nt(p_scr[rows, :], v_cat)
        o_ref[rows, :] = x1_scr[rows, :] + _dot(o.astype(BF16), wo_ref[...])


def _mix_out_prompt(l, x3, hm, hc, hs, wout, g2, wq, mkv, wo, *, tm):
    b, s, d = x3.shape
    tm = min(tm, s)
    mem = mkv.shape[2]
    row = lambda w: pl.BlockSpec((None, tm, w), lambda i, t: (i, t, 0))
    return pl.pallas_call(
        _mix_out_prompt_kernel,
        grid=(b, s // tm),
        in_specs=[
            row(d), row(M_WIDTH), row(C_WIDTH), row(S_WIDTH),
            _weight_spec(wout, l), _weight_spec(g2, l), _weight_spec(wq, l),
            pl.BlockSpec((None, X_WIDTH, mem), lambda i, t: (i, 0, 0)),
            pl.BlockSpec((None, X_WIDTH, mem), lambda i, t: (i, 1, 0)),
            _weight_spec(wo, l),
        ],
        out_specs=row(d),
        out_shape=jax.ShapeDtypeStruct((b, s, d), F32),
        scratch_shapes=[pltpu.VMEM((tm, d), F32), pltpu.VMEM((tm, X_WIDTH), BF16),
                        pltpu.VMEM((tm, X_HEADS * mem), BF16)],
        compiler_params=_params("arbitrary", "arbitrary"),
        name="mix_out_prompt",
    )(x3, hm, hc, hs, wout, g2, wq, mkv, mkv, wo)


FFN_PAD = 8


def _ffn_prompt_kernel(x_ref, g3_ref, wup_ref, cw_ref, cb_ref, wdn_ref, gf_ref, o_ref, tail_ref,
                       carry, act, *, final):
    tm = x_ref.shape[0]
    dff = wdn_ref.shape[0]

    @pl.when(pl.program_id(1) == 0)
    def _():
        carry[...] = jnp.zeros(carry.shape, F32)

    x = x_ref[...]
    h = _rms(x, g3_ref[...]).astype(BF16)
    w = FFN_CHUNK
    row8 = lax.broadcasted_iota(jnp.int32, (FFN_PAD, w), 0)
    for c in range(0, dff, w):
        a = _dot(h, wup_ref[:, c:c + w])
        g = _dot(h, wup_ref[:, dff + c:dff + c + w])
        prev = carry[:, c:c + w]
        carry[:, c:c + w] = g[tm - FFN_PAD:tm, :]
        taps = []
        for s in (2, 1):
            gs = pltpu.roll(g, s, axis=0)
            head = jnp.where(row8 < s, pltpu.roll(prev, s, axis=0), gs[0:FFN_PAD, :])
            taps.append(jnp.concatenate([head, gs[FFN_PAD:, :]], axis=0))
        gc = (cw_ref[0:1, c:c + w] * taps[0] + cw_ref[1:2, c:c + w] * taps[1]
              + cw_ref[2:3, c:c + w] * g + cb_ref[:, c:c + w])
        act[:, c:c + w] = (_swish(gc) * a).astype(BF16)
    y = x + _dot(act[...], wdn_ref[...])
    o_ref[...] = _rms(y, gf_ref[...]) if final else y
    tail_ref[...] = carry[...]


def _ffn_prompt(l, x3, g3, wup, cw, cb, wdn, gf, *, tm, final):
    b, s, d = x3.shape
    tm = min(tm, s)
    dff = wdn.shape[1]
    return pl.pallas_call(
        functools.partial(_ffn_prompt_kernel, final=final),
        grid=(b, s // tm),
        in_specs=[
            pl.BlockSpec((None, tm, d), lambda i, t: (i, t, 0)),
            _weight_spec(g3, l), _weight_spec(wup, l), _weight_spec(cw, l), _weight_spec(cb, l),
            _weight_spec(wdn, l), _const_spec((1, d)),
        ],
        out_specs=[
            pl.BlockSpec((None, tm, d), lambda i, t: (i, t, 0)),
            pl.BlockSpec((None, FFN_PAD, dff), lambda i, t: (i, 0, 0)),
        ],
        out_shape=[jax.ShapeDtypeStruct((b, s, d), F32), jax.ShapeDtypeStruct((b, FFN_PAD, dff), F32)],
        scratch_shapes=[
            pltpu.VMEM((FFN_PAD, dff), F32),
            pltpu.VMEM((tm, dff), BF16),
        ],
        compiler_params=_params("arbitrary", "arbitrary"),
        name="ffn_prompt",
    )(x3, g3, wup, cw, cb, wdn, gf)


SWA_SAMPLE_BLOCK = 32
XATTN_SAMPLE_BLOCK = 16
CONV_SAMPLE_BLOCK = 32


def _mlstm_sample_kernel(gb_ref, q_ref, k_ref, v_ref, o_ref, g_ref, ng_ref, c_ref, n_ref, m_ref, *rest, layer):
    h_ref, co_ref, no_ref, mo_ref, kw_scr = rest[-5:]
    _copy_earlier_layers(rest[:-5], (co_ref, no_ref, mo_ref))
    h = pl.program_id(0)
    i_pre = g_ref[pl.ds(h, 1), :] + gb_ref[layer, h]
    f_pre = g_ref[pl.ds(M_HEADS + h, 1), :] + gb_ref[layer, M_HEADS + h]
    a = _log_sigmoid(f_pre) + m_ref[pl.ds(h, 1), :]
    m_t = jnp.maximum(a, i_pre)
    w_old = jnp.exp(a - m_t)
    w_new = jnp.exp(i_pre - m_t)
    q = q_ref[...]
    k = k_ref[...] * QK_SCALE
    v = v_ref[...]
    n_old = n_ref[...]
    kw_scr[...] = k * w_new

    def body(d, acc):
        c_old = c_ref[d]
        co_ref[layer, d] = w_old * c_old + kw_scr[pl.ds(d, 1), :] * v
        return acc + q_ref[pl.ds(d, 1), :] * c_old

    qc = lax.fori_loop(0, HEAD_DIM, body, jnp.zeros(v.shape, F32), unroll=8)
    s = jnp.sum(q * k, axis=0, keepdims=True) * w_new
    num = w_old * qc + s * v
    den = w_old * jnp.sum(q * n_old, axis=0, keepdims=True) + s
    hh = num / jnp.maximum(jnp.abs(den), jnp.exp(-m_t))
    hh = hh * lax.rsqrt(jnp.mean(hh * hh, axis=0, keepdims=True) + EPS) * ng_ref[...]
    h_ref[...] = hh * _sigmoid(o_ref[...])
    no_ref[layer] = w_old * n_old + kw_scr[...]
    mo_ref[layer, pl.ds(h, 1), :] = m_t


def _mlstm_sample(l, zt, gb, ngt, c_all, n_all, m_all, prev):
    bsz = zt.shape[1]
    D = HEAD_DIM
    feat = lambda off: pl.BlockSpec((D, bsz), lambda h: (off // D + h, 0))
    c_spec = lambda n: pl.BlockSpec((n, None, D, D, bsz), lambda h: (0, h, 0, 0, 0))
    n_spec = lambda n: pl.BlockSpec((n, None, D, bsz), lambda h: (0, h, 0, 0))
    m_spec = lambda n: pl.BlockSpec((n, M_HEADS, bsz), lambda h: (0, 0, 0))
    prev_specs = [c_spec(l), n_spec(l), m_spec(l)] if prev else []
    return pl.pallas_call(
        functools.partial(_mlstm_sample_kernel, layer=l),
        grid=(M_HEADS,),
        in_specs=[
            pl.BlockSpec(memory_space=pltpu.SMEM),
            feat(Z_M), feat(Z_M + M_WIDTH), feat(Z_M + 2 * M_WIDTH), feat(Z_M + 3 * M_WIDTH),
            pl.BlockSpec((SUBLANES, bsz), lambda h: (Z_G // SUBLANES, 0)),
            pl.BlockSpec((None, D, bsz), lambda h: (l, h, 0)),
            pl.BlockSpec((None, None, D, D, bsz), lambda h: (l, h, 0, 0, 0)),
            pl.BlockSpec((None, None, D, bsz), lambda h: (l, h, 0, 0)),
            pl.BlockSpec((None, M_HEADS, bsz), lambda h: (l, 0, 0)),
        ] + prev_specs,
        out_specs=[pl.BlockSpec((D, bsz), lambda h: (h, 0)), c_spec(l + 1), n_spec(l + 1), m_spec(l + 1)],
        out_shape=[
            jax.ShapeDtypeStruct((M_WIDTH, bsz), F32),
            jax.ShapeDtypeStruct((l + 1, M_HEADS, D, D, bsz), F32),
            jax.ShapeDtypeStruct((l + 1, M_HEADS, D, bsz), F32),
            jax.ShapeDtypeStruct((l + 1, M_HEADS, bsz), F32),
        ],
        scratch_shapes=[pltpu.VMEM((D, bsz), F32)],
        compiler_params=_params("arbitrary"),
        name="mlstm_sample",
    )(gb, zt, zt, zt, zt, zt, ngt, c_all, n_all, m_all, *prev)


def _conv_sample_kernel(z_ref, hist_ref, w_ref, cb_ref, lg_ref, lb_ref, *rest):
    h_ref, hist_out_ref = rest[-2:]
    _copy_earlier_layers(rest[:-2], (hist_out_ref,))
    l = hist_out_ref.shape[0] - 1
    nh = C_KERNEL - 1
    u = z_ref[:, 0:C_WIDTH] * _sigmoid(z_ref[:, C_WIDTH:2 * C_WIDTH])
    acc = cb_ref[...] + w_ref[nh:nh + 1, :] * u
    for j in range(nh):
        acc = acc + w_ref[j:j + 1, :] * hist_ref[j]
    mu = jnp.mean(acc, axis=-1, keepdims=True)
    xc = acc - mu
    y = xc * lax.rsqrt(jnp.mean(xc * xc, axis=-1, keepdims=True) + EPS) * lg_ref[...] + lb_ref[...]
    h_ref[...] = _swish(y)
    for j in range(nh - 1):
        hist_out_ref[l, j] = hist_ref[j + 1]
    hist_out_ref[l, nh - 1] = u


def _conv_sample(l, z, hist_all, w, cb, lg, lb, prev):
    bsz = z.shape[0]
    nh = hist_all.shape[1]
    R = min(CONV_SAMPLE_BLOCK, bsz)
    hist_spec = lambda n: pl.BlockSpec((n, nh, R, C_WIDTH), lambda i: (0, 0, i, 0))
    return pl.pallas_call(
        _conv_sample_kernel,
        grid=(bsz // R,),
        in_specs=[
            pl.BlockSpec((R, 2 * C_WIDTH), lambda i: (i, Z_C // (2 * C_WIDTH))),
            pl.BlockSpec((None, nh, R, C_WIDTH), lambda i: (l, 0, i, 0)),
            _weight_spec(w, l), _weight_spec(cb, l), _weight_spec(lg, l), _weight_spec(lb, l),
        ] + [hist_spec(l)] * len(prev),
        out_specs=[pl.BlockSpec((R, C_WIDTH), lambda i: (i, 0)), hist_spec(l + 1)],
        out_shape=[jax.ShapeDtypeStruct((bsz, C_WIDTH), F32), jax.ShapeDtypeStruct((l + 1, nh, bsz, C_WIDTH), F32)],
        compiler_params=_params("arbitrary"),
        name="conv_sample",
    )(z, hist_all, w, cb, lg, lb, *prev)


def _copy_earlier_layers(prev_refs, out_refs):
    for p_ref, o_ref in zip(prev_refs, out_refs):
        o_ref[0:p_ref.shape[0]] = p_ref[...]


def _swa_sample_kernel(q_ref, kvn_ref, kv2_ref, kc_ref, vc_ref, bias_ref, aux_ref, *rest):
    o_ref, ko_ref, vo_ref = rest[-3:]
    _copy_earlier_layers(rest[:-3], (ko_ref, vo_ref))
    l = ko_ref.shape[0] - 1
    R = q_ref.shape[0]
    W = kc_ref.shape[2]
    H = S_Q_HEADS
    shape = (R, H, 128)
    row = lax.broadcasted_iota(jnp.int32, shape, 1)
    lane_half = lax.broadcasted_iota(jnp.int32, shape, 2) // HEAD_DIM
    q_half = row % 2
    kv_head = row // (H // S_KV_HEADS)
    qs = jnp.zeros(shape, F32)
    for j in range(H // 2):
        qs = jnp.where(row // 2 == j, q_ref[:, :, 128 * j:128 * (j + 1)], qs)
    q8 = jnp.where(lane_half == kv_head, jnp.where(q_half == kv_head, qs, pltpu.roll(qs, HEAD_DIM, axis=2)), 0.0)
    k_new = kvn_ref[:, :, 0:S_KV_WIDTH]
    v_new = kvn_ref[:, :, S_KV_WIDTH:2 * S_KV_WIDTH]
    s = jnp.einsum("bqd,bdk->bqk", q8.astype(BF16), kc_ref[...].astype(BF16), preferred_element_type=F32) * QK_SCALE
    s = s + bias_ref[...][None]
    s_new = jnp.sum(q8 * k_new, axis=2, keepdims=True) * QK_SCALE + aux_ref[:, 0:1][None]
    sink = aux_ref[:, 1:2][None]
    m = jnp.maximum(jnp.maximum(jnp.max(s, axis=2, keepdims=True), s_new), sink)
    e = jnp.exp(s - m)
    e_new = jnp.exp(s_new - m)
    inv = 1.0 / (jnp.sum(e, axis=2, keepdims=True) + e_new + jnp.exp(sink - m))
    o8 = jnp.einsum("bqk,bdk->bqd", e.astype(BF16), vc_ref[...].astype(BF16), preferred_element_type=F32)
    o8 = (o8 + e_new * v_new) * inv
    o8 = jnp.where(lane_half == q_half, jnp.where(q_half == kv_head, o8, pltpu.roll(o8, HEAD_DIM, axis=2)), 0.0)
    for j in range(H // 2):
        o_ref[:, :, 128 * j:128 * (j + 1)] = jnp.sum(jnp.where(row // 2 == j, o8, 0.0), axis=1, keepdims=True)
    k_cols = kv2_ref[:, 0:S_KV_WIDTH].T
    v_cols = kv2_ref[:, S_KV_WIDTH:2 * S_KV_WIDTH].T
    last = lax.broadcasted_iota(jnp.int32, (S_KV_WIDTH, W), 1) == W - 1
    for r in range(R):
        ko_ref[l, r] = jnp.where(last, k_cols[:, r:r + 1], pltpu.roll(kc_ref[r], W - 1, axis=1))
        vo_ref[l, r] = jnp.where(last, v_cols[:, r:r + 1], pltpu.roll(vc_ref[r], W - 1, axis=1))


def _swa_sample(l, z, z3, kc_all, vc_all, bias, aux, prev):
    bsz = z.shape[0]
    R = min(SWA_SAMPLE_BLOCK, bsz)
    W = kc_all.shape[3]
    cache_in = pl.BlockSpec((None, R, S_KV_WIDTH, W), lambda i: (l, i, 0, 0))
    cache_prev = pl.BlockSpec((l, R, S_KV_WIDTH, W), lambda i: (0, i, 0, 0))
    cache_out = pl.BlockSpec((l + 1, R, S_KV_WIDTH, W), lambda i: (0, i, 0, 0))
    return pl.pallas_call(
        _swa_sample_kernel,
        grid=(bsz // R,),
        in_specs=[
            pl.BlockSpec((R, 1, S_WIDTH), lambda i: (i, 0, Z_SQ // S_WIDTH)),
            pl.BlockSpec((R, 1, 2 * S_KV_WIDTH), lambda i: (i, 0, Z_SKV // (2 * S_KV_WIDTH))),
            pl.BlockSpec((R, 2 * S_KV_WIDTH), lambda i: (i, Z_SKV // (2 * S_KV_WIDTH))),
            cache_in, cache_in,
            _const_spec((S_Q_HEADS, 128)), _weight_spec(aux, l),
        ] + [cache_prev] * len(prev),
        out_specs=[pl.BlockSpec((R, 1, S_WIDTH), lambda i: (i, 0, 0)), cache_out, cache_out],
        out_shape=[
            jax.ShapeDtypeStruct((bsz, 1, S_WIDTH), F32),
            jax.ShapeDtypeStruct((l + 1, bsz, S_KV_WIDTH, W), F32),
            jax.ShapeDtypeStruct((l + 1, bsz, S_KV_WIDTH, W), F32),
        ],
        compiler_params=_params("arbitrary"),
        name="swa_sample",
    )(z3, z3, z, kc_all, vc_all, bias, aux, *prev)


def _mix_out_sample_kernel(x_ref, hmt_ref, hc_ref, hs_ref, wout_ref, g2_ref, wq_ref, x1_ref, q_ref):
    cat = jnp.concatenate([hmt_ref[...].T.astype(BF16), hc_ref[...].astype(BF16), hs_ref[...].astype(BF16)], axis=1)
    x1 = x_ref[...] + _dot(cat, wout_ref[...])
    x1_ref[...] = x1
    q_ref[...] = _dot(_rms(x1, g2_ref[...]).astype(BF16), wq_ref[...])


def _mix_out_sample(l, x, hm, hc, hs, wout, g2, wq):
    bsz, d = x.shape
    full = lambda a: pl.BlockSpec(a.shape, lambda i: (0,) * a.ndim)
    args = (x, hm, hc, hs, wout, g2, wq)
    return pl.pallas_call(
        _mix_out_sample_kernel,
        grid=(1,),
        in_specs=[full(x), full(hm), full(hc), full(hs), _weight_spec(wout, l), _weight_spec(g2, l),
                  _weight_spec(wq, l)],
        out_specs=[pl.BlockSpec((bsz, d), lambda i: (0, 0)), pl.BlockSpec((bsz, X_WIDTH), lambda i: (0, 0))],
        out_shape=[jax.ShapeDtypeStruct((bsz, d), F32), jax.ShapeDtypeStruct((bsz, X_WIDTH), F32)],
        compiler_params=_params("arbitrary"),
        name="mix_out_sample",
    )(*args)


def _xattn_sample_kernel(q_ref, k_ref, v_ref, o_ref):
    R = k_ref.shape[0]
    shape = (R, SUBLANES, X_WIDTH)
    row = lax.broadcasted_iota(jnp.int32, shape, 1)
    lane_head = lax.broadcasted_iota(jnp.int32, shape, 2) // HEAD_DIM
    own = row == lane_head
    q8 = jnp.where(own, jnp.broadcast_to(q_ref[...], shape), 0.0).astype(BF16)
    s = jnp.einsum("bqd,bdk->bqk", q8, k_ref[...].astype(BF16), preferred_element_type=F32) * QK_SCALE
    e = jnp.exp(s - jnp.max(s, axis=2, keepdims=True))
    p = (e / jnp.sum(e, axis=2, keepdims=True)).astype(BF16)
    o8 = jnp.einsum("bqk,bdk->bqd", p, v_ref[...].astype(BF16), preferred_element_type=F32)
    o_ref[...] = jnp.sum(jnp.where(own, o8, 0.0), axis=1, keepdims=True)


def _xattn_sample(l, q3, k_all, v_all):
    _, bsz, w, mem = k_all.shape
    R = min(XATTN_SAMPLE_BLOCK, bsz)
    kv = pl.BlockSpec((None, R, w, mem), lambda i: (l, i, 0, 0))
    qo = pl.BlockSpec((R, 1, w), lambda i: (i, 0, 0))
    return pl.pallas_call(
        _xattn_sample_kernel,
        grid=(bsz // R,),
        in_specs=[qo, kv, kv],
        out_specs=qo,
        out_shape=jax.ShapeDtypeStruct((bsz, 1, w), F32),
        compiler_params=_params("arbitrary"),
        name="xattn_sample",
    )(q3, k_all, v_all)


def _ffn_sample_kernel(x1_ref, ox_ref, wo_ref, g3_ref, wup_ref, cw_ref, cb_ref, wdn_ref, gf_ref, hist_ref,
                       *rest, final):
    o_ref, hist_out_ref, act = rest[-3:]
    _copy_earlier_layers(rest[:-3], (hist_out_ref,))
    l = hist_out_ref.shape[0] - 1
    dff = wdn_ref.shape[0]
    x = x1_ref[...] + _dot(ox_ref[...].astype(BF16), wo_ref[...])
    h = _rms(x, g3_ref[...]).astype(BF16)
    for c in range(0, dff, FFN_CHUNK):
        a = _dot(h, wup_ref[:, c:c + FFN_CHUNK])
        g = _dot(h, wup_ref[:, dff + c:dff + c + FFN_CHUNK])
        h1 = hist_ref[:, 1, c:c + FFN_CHUNK]
        gc = (cw_ref[0:1, c:c + FFN_CHUNK] * hist_ref[:, 0, c:c + FFN_CHUNK]
              + cw_ref[1:2, c:c + FFN_CHUNK] * h1
              + cw_ref[2:3, c:c + FFN_CHUNK] * g + cb_ref[:, c:c + FFN_CHUNK])
        act[:, c:c + FFN_CHUNK] = (_swish(gc) * a).astype(BF16)
        hist_out_ref[l, :, 0, c:c + FFN_CHUNK] = h1
        hist_out_ref[l, :, 1, c:c + FFN_CHUNK] = g
    y = x + _dot(act[...], wdn_ref[...])
    o_ref[...] = _rms(y, gf_ref[...]) if final else y


def _ffn_sample(l, x1, ox, wo, g3, wup, cw, cb, wdn, gf, hist_all, prev, *, final):
    bsz, d = x1.shape
    dff = wdn.shape[1]
    full = lambda a: pl.BlockSpec(a.shape, lambda i: (0,) * a.ndim)
    hshape = hist_all.shape[1:]
    return pl.pallas_call(
        functools.partial(_ffn_sample_kernel, final=final),
        grid=(1,),
        in_specs=[full(x1), full(ox), _weight_spec(wo, l), _weight_spec(g3, l), _weight_spec(wup, l),
                  _weight_spec(cw, l), _weight_spec(cb, l),
                  _weight_spec(wdn, l), full(gf), pl.BlockSpec((None,) + hshape, lambda i: (l, 0, 0, 0))]
                 + [full(p) for p in prev],
        out_specs=[pl.BlockSpec((bsz, d), lambda i: (0, 0)), pl.BlockSpec((l + 1,) + hshape, lambda i: (0, 0, 0, 0))],
        out_shape=[jax.ShapeDtypeStruct((bsz, d), F32), jax.ShapeDtypeStruct((l + 1,) + hshape, F32)],
        scratch_shapes=[pltpu.VMEM((bsz, dff), BF16)],
        compiler_params=_params("arbitrary"),
        name="ffn_sample",
    )(x1, ox, wo, g3, wup, cw, cb, wdn, gf, hist_all, *prev)


def _t5_buckets(dist):
    n = np.maximum(dist, 0)
    max_exact = N_BUCKETS // 2
    nf = np.maximum(n, max_exact).astype(np.float32)
    large = max_exact + (np.log(nf / np.float32(max_exact)) / np.float32(math.log(MAX_DISTANCE / max_exact))
                         * np.float32(N_BUCKETS - max_exact)).astype(np.int32)
    return np.where(n < max_exact, n, np.minimum(large, N_BUCKETS - 1))


def _prompt_buckets():
    W = WINDOW
    dist = np.arange(W)[None, :] + W - np.arange(2 * W)[:, None]
    band = (dist >= 0) & (dist < W)
    return np.where(band, _t5_buckets(dist), -1).astype(np.int32)


def _swa_tables(rel_bias):
    W = WINDOW
    dist_c = W - np.arange(W)
    tab = jnp.transpose(rel_bias[_t5_buckets(dist_c)], (1, 0))
    cache_bias = jnp.where((dist_c < W)[None], tab, NEG_INF)
    return cache_bias, rel_bias[0]


def kernel(x_prompt, x_sample, mem_prompt, state_mlstm_C, state_mlstm_n, state_mlstm_m, state_conv, cache_swa_k, cache_swa_v, cache_mem_k, cache_mem_v, state_ffn_conv, rel_bias, norm1_g, w_in, b_i, b_f, mlstm_norm_g, conv_w, conv_b, conv_ln_g, conv_ln_b, swa_sinks, w_out, norm2_g, w_xq, w_xk, w_xv, w_xo, norm3_g, w_up, ffn_conv_w, ffn_conv_b, w_down, final_norm_g):
    depth = w_in.shape[0]
    bp, seq, d = x_prompt.shape
    bs = x_sample.shape[0]
    mem = mem_prompt.shape[1]
    dff = w_down.shape[1]
    W = WINDOW
    nh = C_KERNEL - 1

    xp = x_prompt
    xs = x_sample.reshape(bs, d)
    gf = final_norm_g.reshape(1, d)
    c_all = jnp.transpose(state_mlstm_C, (0, 2, 3, 4, 1))
    n_all = jnp.transpose(state_mlstm_n, (0, 2, 3, 1))
    m_all = jnp.transpose(state_mlstm_m, (0, 2, 1))
    hist_all = jnp.transpose(state_conv, (0, 2, 1, 3))
    kc_all = jnp.transpose(cache_swa_k, (0, 1, 3, 4, 2)).reshape(depth, bs, S_KV_WIDTH, W)
    vc_all = jnp.transpose(cache_swa_v, (0, 1, 3, 4, 2)).reshape(depth, bs, S_KV_WIDTH, W)
    mk_all = jnp.transpose(cache_mem_k, (0, 1, 3, 4, 2)).reshape(depth, bs, X_WIDTH, mem)
    mv_all = jnp.transpose(cache_mem_v, (0, 1, 3, 4, 2)).reshape(depth, bs, X_WIDTH, mem)
    pm_c, pm_n, pm_m, p_conv, p_k, p_v, p_mk, p_mv, p_ffn = ([] for _ in range(9))
    s_mlstm, s_conv, s_kv, s_ffn = [], [], [], []

    win_t = jnp.swapaxes(w_in, 1, 2)
    wout = w_out.astype(BF16)
    wxq = w_xq.astype(BF16)
    wxkv = jnp.concatenate([w_xk, w_xv], axis=2).astype(BF16)
    wxo = w_xo.astype(BF16)
    wup = w_up.astype(BF16)
    wdn = w_down.astype(BF16)
    bucket = jnp.asarray(_prompt_buckets())

    rows = lambda a: a.reshape(depth, 1, -1)
    g1, g2, g3 = rows(norm1_g), rows(norm2_g), rows(norm3_g)
    gb8 = jnp.concatenate([b_i, b_f], axis=1)
    gbias = rows(jnp.concatenate([gb8, jnp.zeros((depth, 128 - 2 * M_HEADS), F32)], axis=1))
    ng = rows(mlstm_norm_g)
    ngt = jnp.broadcast_to(mlstm_norm_g[:, :, None], (depth, M_WIDTH, bs))
    cw = jnp.concatenate([conv_w, jnp.zeros((depth, CONV_PAD - C_KERNEL, C_WIDTH), F32)], axis=1)
    cb, lg, lb = rows(conv_b), rows(conv_ln_g), rows(conv_ln_b)
    fcw = jnp.concatenate([ffn_conv_w, jnp.zeros((depth, SUBLANES - FFN_KERNEL, dff), F32)], axis=1)
    fcb = rows(ffn_conv_b)
    cache_bias, bias0 = _swa_tables(rel_bias)
    aux = jnp.concatenate([jnp.broadcast_to(bias0[None, :, None], (depth, S_Q_HEADS, 1)), swa_sinks[:, :, None],
                           jnp.zeros((depth, S_Q_HEADS, 126), F32)], axis=2)

    for l in range(depth):
        last = l == depth - 1

        mkv = _mem_kv(l, mem_prompt, wxkv)
        *zs_prompt, win = _in_proj(l, xp.reshape(bp * seq, d), g1, win_t, tm=1024)
        zm, zc, zq, zkv, zg = (a.reshape(bp, seq, a.shape[1]) for a in zs_prompt)
        hm, cpair, npair, mm = _mlstm_prompt(l, zm, zg, gbias, ng)
        hc, ctail = _conv_prompt(l, zc, cw, cb, lg, lb, tc=min(1024, seq))
        hs, kt, vt = _swa_prompt(l, zq, zkv, rel_bias, swa_sinks, bucket, qb=min(8, seq // W))
        xp = _mix_out_prompt(l, xp, hm, hc, hs, wout, g2, wxq, mkv, wxo, tm=1024)
        xp, ftail = _ffn_prompt(l, xp, g3, wup, fcw, fcb, wdn, gf, tm=512, final=last)
        half = lambda h: slice(HEAD_DIM * (h % 2), HEAD_DIM * (h % 2 + 1))
        pm_c.append(jnp.stack([jnp.swapaxes(cpair[:, h // 2, half(h), half(h)], 1, 2) for h in range(M_HEADS)], axis=1))
        pm_n.append(jnp.stack([npair[:, h // 2, h % 2, half(h)] for h in range(M_HEADS)], axis=1))
        pm_m.append(mm[:, :, 0, 0])
        p_conv.append(ctail[:, CONV_PAD - nh:, :])
        p_k.append(jnp.transpose(kt.reshape(bp, S_KV_HEADS, HEAD_DIM, W), (0, 3, 1, 2)))
        p_v.append(jnp.transpose(vt.reshape(bp, S_KV_HEADS, HEAD_DIM, W), (0, 3, 1, 2)))
        p_mk.append(jnp.transpose(mkv[:, 0:X_WIDTH, :].reshape(bp, X_HEADS, HEAD_DIM, mem), (0, 3, 1, 2)))
        p_mv.append(jnp.transpose(mkv[:, X_WIDTH:, :].reshape(bp, X_HEADS, HEAD_DIM, mem), (0, 3, 1, 2)))
        p_ffn.append(ftail[:, FFN_PAD - (FFN_KERNEL - 1):, :])

        zs, zst = _sample_in(l, xs, g1, win)
        hmt_s, *s_mlstm = _mlstm_sample(l, zst, gb8, ngt, c_all, n_all, m_all, s_mlstm)
        hc_s, *s_conv = _conv_sample(l, zs, hist_all, cw, cb, lg, lb, s_conv)
        hs_s, *s_kv = _swa_sample(l, zs, zs.reshape(bs, 1, Z_WIDTH), kc_all, vc_all, cache_bias, aux, s_kv)
        x1, qx = _mix_out_sample(l, xs, hmt_s, hc_s, hs_s.reshape(bs, S_WIDTH), wout, g2, wxq)
        ox = _xattn_sample(l, qx.reshape(bs, 1, X_WIDTH), mk_all, mv_all)
        xs, *s_ffn = _ffn_sample(l, x1, ox.reshape(bs, X_WIDTH), wxo, g3, wup, fcw, fcb, wdn, gf,
                                 state_ffn_conv, s_ffn, final=last)

    st = jnp.stack
    tr = jnp.transpose
    s_c, s_n, s_m = s_mlstm
    s_k, s_v = (a.reshape(depth, bs, S_KV_HEADS, HEAD_DIM, W) for a in s_kv)
    return (xp, xs.reshape(bs, 1, d),
            st(pm_c), st(pm_n), st(pm_m), st(p_conv), st(p_k), st(p_v), st(p_mk), st(p_mv), st(p_ffn),
            tr(s_c, (0, 4, 1, 2, 3)), tr(s_n, (0, 3, 1, 2)), tr(s_m, (0, 2, 1)),
            tr(s_conv[0], (0, 2, 1, 3)), tr(s_k, (0, 1, 4, 2, 3)), tr(s_v, (0, 1, 4, 2, 3)),
            s_ffn[0])
```

```python
import functools
import math

import numpy as np
import jax
import jax.numpy as jnp
from jax import lax
from jax.experimental import pallas as pl
from jax.experimental.pallas import tpu as pltpu

F32 = jnp.float32
BF16 = jnp.bfloat16
EPS = 1e-6
NEG_INF = float("-inf")

HEAD_DIM = 64
M_HEADS = 4
M_WIDTH = M_HEADS * HEAD_DIM
C_WIDTH = 256
C_KERNEL = 31
S_Q_HEADS = 8
S_KV_HEADS = 2
S_WIDTH = S_Q_HEADS * HEAD_DIM
S_KV_WIDTH = S_KV_HEADS * HEAD_DIM
WINDOW = 128
N_BUCKETS = 32
MAX_DISTANCE = 128
X_HEADS = 4
X_WIDTH = X_HEADS * HEAD_DIM
FFN_KERNEL = 3
QK_SCALE = HEAD_DIM ** -0.5
LOG2E = math.log2(math.e)

Z_M = 0
Z_C = 4 * M_WIDTH
Z_SQ = Z_C + 2 * C_WIDTH
Z_SKV = Z_SQ + S_WIDTH
Z_G = Z_SKV + 2 * S_KV_WIDTH
LANES = 128
SUBLANES = 8
Z_WIDTH = Z_G + LANES
VMEM_LIMIT = 56 * 1024 * 1024

M_CHUNK = 128
FFN_CHUNK = 256


def _params(*sem):
    return pltpu.CompilerParams(dimension_semantics=sem, vmem_limit_bytes=VMEM_LIMIT)


def _const_spec(shape):
    nd = len(shape)
    return pl.BlockSpec(shape, lambda *_: (0,) * nd, pipeline_mode=pl.Buffered(1))


def _weight_spec(w, l):
    nd = w.ndim - 1
    return pl.BlockSpec((None,) + w.shape[1:], lambda *_: (l,) + (0,) * nd, pipeline_mode=pl.Buffered(1))


def _rms(x, g):
    return x * lax.rsqrt(jnp.mean(x * x, axis=-1, keepdims=True) + EPS) * g


def _sigmoid(x):
    return 0.5 * jnp.tanh(0.5 * x) + 0.5


def _swish(x):
    h = 0.5 * x
    return h + h * jnp.tanh(h)


def _log_sigmoid(x):
    return jnp.minimum(x, 0.0) - jnp.log1p(jnp.exp(-jnp.abs(x)))


def _dot(a, b):
    return jnp.dot(a, b, preferred_element_type=F32)


def _dot_nt(a, b):
    return lax.dot_general(a, b, (((1,), (1,)), ((), ())), preferred_element_type=F32)


def _dot_tn(a, b):
    return lax.dot_general(a, b, (((0,), (0,)), ((), ())), preferred_element_type=F32)


IN_PROJ_ROWS = 512
IN_PROJ_COLS = 512
IN_GATES = 4 * M_WIDTH
W_PREP_ROWS = 256


def _in_proj_kernel(x_ref, g_ref, wt_ref, o_ref, wprep_ref, w_scr):
    d = x_ref.shape[1]

    @pl.when(pl.program_id(0) == 0)
    def _():
        n_gate = 2 * M_HEADS
        for src, dst, n in ((0, 0, IN_GATES), (IN_GATES + n_gate, IN_GATES, Z_G - IN_GATES)):
            for c in range(0, n, W_PREP_ROWS):
                w_scr[:, dst + c:dst + c + W_PREP_ROWS] = wt_ref[src + c:src + c + W_PREP_ROWS, :].T.astype(BF16)
        gate_rows = jnp.concatenate([wt_ref[IN_GATES:IN_GATES + n_gate, :], jnp.zeros((LANES - n_gate, d), F32)],
                                    axis=0)
        w_scr[:, Z_G:Z_G + LANES] = gate_rows.T.astype(BF16)
        wprep_ref[...] = w_scr[...]

    tm, n = o_ref.shape
    sub = min(IN_PROJ_ROWS, tm)
    for r0 in range(0, tm, sub):
        h = _rms(x_ref[r0:r0 + sub, :], g_ref[...]).astype(BF16)
        for c in range(0, n, IN_PROJ_COLS):
            w = min(IN_PROJ_COLS, n - c)
            o_ref[r0:r0 + sub, c:c + w] = _dot(h, w_scr[:, c:c + w])


def _in_proj(l, x, g, wt, *, tm):
    m, d = x.shape
    tm = min(tm, m)
    return pl.pallas_call(
        _in_proj_kernel,
        grid=(m // tm,),
        in_specs=[pl.BlockSpec((tm, d), lambda i: (i, 0)), _weight_spec(g, l), _weight_spec(wt, l)],
        out_specs=[pl.BlockSpec((tm, Z_WIDTH), lambda i: (i, 0)), pl.BlockSpec((d, Z_WIDTH), lambda i: (0, 0))],
        out_shape=[jax.ShapeDtypeStruct((m, Z_WIDTH), F32), jax.ShapeDtypeStruct((d, Z_WIDTH), BF16)],
        scratch_shapes=[pltpu.VMEM((d, Z_WIDTH), BF16)],
        compiler_params=_params("arbitrary"),
        name="in_proj",
    )(x, g, wt)


def _mem_kv_kernel(x_ref, w_ref, o_ref, acc):
    acc[...] = _dot(x_ref[...].astype(BF16), w_ref[...])
    o_ref[...] = acc[...].T


def _mem_kv(l, mem3, w):
    b, mem, d = mem3.shape
    n = w.shape[2]
    return pl.pallas_call(
        _mem_kv_kernel,
        grid=(b,),
        in_specs=[pl.BlockSpec((None, mem, d), lambda i: (i, 0, 0)), _weight_spec(w, l)],
        out_specs=pl.BlockSpec((None, n, mem), lambda i: (i, 0, 0)),
        out_shape=jax.ShapeDtypeStruct((b, n, mem), F32),
        scratch_shapes=[pltpu.VMEM((mem, n), F32)],
        compiler_params=_params("arbitrary"),
        name="mem_kv",
    )(mem3, w)


def _sample_in_kernel(x_ref, g_ref, w_ref, z_ref, zt_ref):
    h = _rms(x_ref[...], g_ref[...]).astype(BF16)
    n = z_ref.shape[1]
    for c in range(0, n, IN_PROJ_COLS):
        w = min(IN_PROJ_COLS, n - c)
        zc = _dot(h, w_ref[:, c:c + w])
        z_ref[:, c:c + w] = zc
        zt_ref[c:c + w, :] = zc.T


def _sample_in(l, x, g, w):
    m, d = x.shape
    n = w.shape[1]
    full = lambda shape: pl.BlockSpec(shape, lambda i: (0,) * len(shape))
    return pl.pallas_call(
        _sample_in_kernel,
        grid=(1,),
        in_specs=[full((m, d)), _weight_spec(g, l), full((d, n))],
        out_specs=[full((m, n)), full((n, m))],
        out_shape=[jax.ShapeDtypeStruct((m, n), F32), jax.ShapeDtypeStruct((n, m), F32)],
        compiler_params=_params("arbitrary"),
        name="sample_in",
    )(x, g, w)


def _mlstm_prompt_kernel(z_ref, g_ref, gb_ref, ng_ref, h_ref, cp_ref, np_ref, m_ref,
                         cp_scr, np_scr, m_scr, ht_scr, st_scr, kw_scr):
    NB, L = z_ref.shape[0], z_ref.shape[1]
    D = HEAD_DIM
    hi = lax.Precision.HIGHEST

    @pl.when(pl.program_id(0) == 0)
    def _():
        cp_scr[...] = jnp.zeros(cp_scr.shape, F32)
        np_scr[...] = jnp.zeros(np_scr.shape, F32)
        m_scr[...] = jnp.zeros(m_scr.shape, F32)

    src = lax.broadcasted_iota(jnp.int32, (L, L), 0)
    qry = lax.broadcasted_iota(jnp.int32, (L, L), 1)
    causal_t = src <= qry
    upper = jnp.where(causal_t, 1.0, 0.0)
    lane_half = lax.broadcasted_iota(jnp.int32, (L, 128), 1) // D
    row8 = lax.broadcasted_iota(jnp.int32, (SUBLANES, 128), 0)

    rows, cols = {}, []
    for b in range(NB):
        g_t = (g_ref[b] + gb_ref[...]).T[0:SUBLANES, :]
        b_rows = jnp.dot(_log_sigmoid(g_t), upper, precision=hi, preferred_element_type=F32)
        to_cols = []
        for h in range(M_HEADS):
            b_row = b_rows[M_HEADS + h:M_HEADS + h + 1, :]
            ci_row = g_t[h:h + 1, :] - b_row
            m_prev = m_scr[b, h, 0:1, 0:1]
            b_last = b_row[:, L - 1:L]
            m_new = jnp.maximum(b_last + m_prev, jnp.max(b_last + ci_row, axis=1, keepdims=True))
            rows[b, h] = dict(b_row=b_row, m_prev=m_prev, m_new=m_new, decay=jnp.exp(b_last + m_prev - m_new))
            to_cols += [ci_row, jnp.exp(b_last + ci_row - m_new)]
        to_cols.append(jnp.zeros((128 - 2 * M_HEADS, L), F32))
        cols.append(jnp.concatenate(to_cols, axis=0).T)

    stats = {}
    for b in range(NB):
        for j in range(M_HEADS // 2):
            slab = 128 * j
            q2 = z_ref[b, :, slab:slab + 128].astype(BF16)
            k2 = z_ref[b, :, M_WIDTH + slab:M_WIDTH + slab + 128] * QK_SCALE
            for par in range(2):
                h = 2 * j + par
                r = rows[b, h]
                ci_col = cols[b][:, 2 * h:2 * h + 1]
                a_row = r["b_row"] + r["m_prev"]
                dm = jnp.where(causal_t, r["b_row"] + ci_col, NEG_INF)
                m_row = jnp.maximum(a_row, jnp.max(dm, axis=0, keepdims=True))
                k_own = jnp.where(lane_half == par, k2, 0.0).astype(BF16)
                s_t = _dot_nt(k_own, q2) * jnp.exp(dm - m_row)
                st_scr[b, j, :, L * par:L * (par + 1)] = s_t.astype(BF16)
                stats[b, h] = dict(w_inter=jnp.exp(a_row - m_row), floor=jnp.exp(-m_row),
                                   den_s=jnp.sum(s_t, axis=0, keepdims=True),
                                   decay=r["decay"], m_new=r["m_new"])
            wk = jnp.where(lane_half == 0, cols[b][:, 4 * j + 1:4 * j + 2], cols[b][:, 4 * j + 3:4 * j + 4])
            kw = k2 * wk
            kw_scr[b, j] = kw.astype(BF16)
            stats[b, j, "k_sum"] = jnp.sum(kw, axis=0, keepdims=True)

    block_diag = (lax.broadcasted_iota(jnp.int32, (128, 128), 0) // D
                  == lax.broadcasted_iota(jnp.int32, (128, 128), 1) // D)
    for b in range(NB):
        for j in range(M_HEADS // 2):
            slab = 128 * j
            q2 = z_ref[b, :, slab:slab + 128].astype(BF16)
            v2 = z_ref[b, :, 2 * M_WIDTH + slab:2 * M_WIDTH + slab + 128].astype(BF16)
            cp = cp_scr[b, j]
            npair = np_scr[b, j]
            qc = _dot_nt(cp.astype(BF16), q2)
            qn = _dot_nt(npair.astype(BF16), q2)
            pv = _dot_tn(v2, st_scr[b, j])
            upd = jnp.where(block_diag, _dot_tn(v2, kw_scr[b, j]), 0.0)
            n_new = jnp.zeros((SUBLANES, 128), F32)
            for par in range(2):
                h = 2 * j + par
                st = stats[b, h]
                hr = slice(D * par, D * (par + 1))
                num = st["w_inter"] * qc[hr, :] + pv[hr, L * par:L * (par + 1)]
                den = st["w_inter"] * qn[par:par + 1, :] + st["den_s"]
                hh = num * (1.0 / jnp.maximum(jnp.abs(den), st["floor"]))
                hh = hh * lax.rsqrt(jnp.mean(hh * hh, axis=0, keepdims=True) + EPS)
                ht_scr[b, D * h:D * (h + 1), :] = hh
                cp_scr[b, j, hr, :] = st["decay"] * cp[hr, :] + upd[hr, :]
                k_sum = jnp.where(lane_half[0:1, :] == par, stats[b, j, "k_sum"], 0.0)
                n_new = jnp.where(row8 == par, st["decay"] * npair[par:par + 1, :] + k_sum, n_new)
                m_scr[b, h] = jnp.broadcast_to(st["m_new"], m_scr.shape[2:])
            np_scr[b, j] = n_new
        o_gate = _sigmoid(z_ref[b, :, 3 * M_WIDTH:4 * M_WIDTH])
        h_ref[b] = ht_scr[b].T * ng_ref[...] * o_gate

    cp_ref[...] = cp_scr[...]
    np_ref[...] = np_scr[...]
    m_ref[...] = m_scr[...]


def _mlstm_prompt(l, z3, gbias, ng):
    b, s, _ = z3.shape
    L = M_CHUNK
    P = M_HEADS // 2
    return pl.pallas_call(
        _mlstm_prompt_kernel,
        grid=(s // L,),
        in_specs=[
            pl.BlockSpec((b, L, 4 * M_WIDTH), lambda c: (0, c, Z_M // (4 * M_WIDTH))),
            pl.BlockSpec((b, L, 128), lambda c: (0, c, Z_G // 128)),
            _weight_spec(gbias, l),
            _weight_spec(ng, l),
        ],
        out_specs=[
            pl.BlockSpec((b, L, M_WIDTH), lambda c: (0, c, 0)),
            pl.BlockSpec((b, P, 128, 128), lambda c: (0, 0, 0, 0)),
            pl.BlockSpec((b, P, SUBLANES, 128), lambda c: (0, 0, 0, 0)),
            pl.BlockSpec((b, M_HEADS, SUBLANES, 128), lambda c: (0, 0, 0, 0)),
        ],
        out_shape=[
            jax.ShapeDtypeStruct((b, s, M_WIDTH), F32),
            jax.ShapeDtypeStruct((b, P, 128, 128), F32),
            jax.ShapeDtypeStruct((b, P, SUBLANES, 128), F32),
            jax.ShapeDtypeStruct((b, M_HEADS, SUBLANES, 128), F32),
        ],
        scratch_shapes=[
            pltpu.VMEM((b, P, 128, 128), F32),
            pltpu.VMEM((b, P, SUBLANES, 128), F32),
            pltpu.VMEM((b, M_HEADS, SUBLANES, 128), F32),
            pltpu.VMEM((b, M_WIDTH, L), F32),
            pltpu.VMEM((b, P, L, 2 * L), BF16),
            pltpu.VMEM((b, P, L, 128), BF16),
        ],
        compiler_params=_params("arbitrary"),
        name="mlstm_prompt",
    )(z3, z3, gbias, ng)


CONV_PAD = 32


def _conv_prompt_kernel(z_ref, w_ref, cb_ref, lg_ref, lb_ref, h_ref, tail_ref, buf, shifted):
    tc = z_ref.shape[0]
    t = pl.program_id(1)

    @pl.when(t == 0)
    def _():
        buf[0:CONV_PAD, :] = jnp.zeros((CONV_PAD, C_WIDTH), F32)

    @pl.when(t > 0)
    def _():
        buf[0:CONV_PAD, :] = buf[tc:tc + CONV_PAD, :]

    u = z_ref[:, 0:C_WIDTH] * _sigmoid(z_ref[:, C_WIDTH:2 * C_WIDTH])
    buf[CONV_PAD:CONV_PAD + tc, :] = u
    n_sh = shifted.shape[1]
    for r in range(1, SUBLANES):
        shifted[r - 1] = buf[r:r + n_sh, :]
    off = CONV_PAD - (C_KERNEL - 1)
    acc = jnp.broadcast_to(cb_ref[...], (tc, C_WIDTH))
    for j in range(C_KERNEL):
        r = (off + j) % SUBLANES
        base = off + j - r
        win = buf[base:base + tc, :] if r == 0 else shifted[r - 1, base:base + tc, :]
        acc = acc + w_ref[j:j + 1, :] * win
    mu = jnp.mean(acc, axis=-1, keepdims=True)
    xc = acc - mu
    y = xc * lax.rsqrt(jnp.mean(xc * xc, axis=-1, keepdims=True) + EPS) * lg_ref[...] + lb_ref[...]
    h_ref[...] = _swish(y)
    tail_ref[...] = buf[tc:tc + CONV_PAD, :]


def _conv_prompt(l, z3, w, cb, lg, lb, *, tc):
    b, s, _ = z3.shape
    return pl.pallas_call(
        _conv_prompt_kernel,
        grid=(b, s // tc),
        in_specs=[
            pl.BlockSpec((None, tc, 2 * C_WIDTH), lambda i, t: (i, t, Z_C // (2 * C_WIDTH))),
            _weight_spec(w, l), _weight_spec(cb, l), _weight_spec(lg, l), _weight_spec(lb, l),
        ],
        out_specs=[
            pl.BlockSpec((None, tc, C_WIDTH), lambda i, t: (i, t, 0)),
            pl.BlockSpec((None, CONV_PAD, C_WIDTH), lambda i, t: (i, 0, 0)),
        ],
        out_shape=[
            jax.ShapeDtypeStruct((b, s, C_WIDTH), F32),
            jax.ShapeDtypeStruct((b, CONV_PAD, C_WIDTH), F32),
        ],
        scratch_shapes=[pltpu.VMEM((CONV_PAD + tc, C_WIDTH), F32),
                        pltpu.VMEM((SUBLANES - 1, CONV_PAD + tc - SUBLANES, C_WIDTH), F32)],
        compiler_params=_params("arbitrary", "arbitrary"),
        name="conv_prompt",
    )(z3, w, cb, lg, lb)


def _swa_prompt_kernel(rb_ref, sink_ref, bucket_ref, q_ref, kv_ref, kvp_ref, o_ref, kt_ref, vt_ref,
                       bias_scr, s_scr, p_scr, ot_scr, *, layer):
    W = WINDOW
    H = S_Q_HEADS
    G = H // S_KV_HEADS
    i = pl.program_id(0)
    n = pl.program_id(1)

    @pl.when(jnp.logical_and(i == 0, n == 0))
    def _():
        bucket = bucket_ref[...]
        prev_key = lax.broadcasted_iota(jnp.int32, (2 * W, W), 0) < W
        for h in range(H):
            acc = jnp.full((2 * W, W), NEG_INF, F32)
            for b in range(N_BUCKETS):
                acc = jnp.where(bucket == b, rb_ref[b, h], acc)
            acc = acc * LOG2E
            bias_scr[0, h] = acc
            bias_scr[1, h] = jnp.where(prev_key, NEG_INF, acc)

    QB = q_ref.shape[0] // W
    NK = (QB + 1) * W
    first = jnp.where(n == 0, 1, 0)
    kk = jnp.concatenate([kvp_ref[:, 0:S_KV_WIDTH], kv_ref[:, 0:S_KV_WIDTH]], axis=0) * (QK_SCALE * LOG2E)
    vv = jnp.concatenate([kvp_ref[:, S_KV_WIDTH:2 * S_KV_WIDTH], kv_ref[:, S_KV_WIDTH:2 * S_KV_WIDTH]], axis=0)
    kk_r = pltpu.roll(kk, HEAD_DIM, axis=1)
    lo_lane = lax.broadcasted_iota(jnp.int32, (NK, S_KV_WIDTH), 1) < HEAD_DIM
    k_var = [[jnp.where(lo_lane, kk, 0.0).astype(BF16), jnp.where(lo_lane, 0.0, kk_r).astype(BF16)],
             [jnp.where(lo_lane, kk_r, 0.0).astype(BF16), jnp.where(lo_lane, 0.0, kk).astype(BF16)]]
    v_t = vv.T.astype(BF16)

    for j in range(QB):
        k0 = j * W
        masked = first if j == 0 else 0
        m_rows = [None] * H
        for hk in range(S_KV_HEADS):
            c0 = 2 * 128 * hk
            q_st = jnp.concatenate([q_ref[k0:k0 + W, c0:c0 + 128], q_ref[k0:k0 + W, c0 + 128:c0 + 256]],
                                   axis=0).astype(BF16)
            for half in range(2):
                s_t = _dot_nt(k_var[hk][half][k0:k0 + 2 * W, :], q_st)
                for slab in range(2):
                    head = G * hk + 2 * slab + half
                    sb = s_t[:, 128 * slab:128 * (slab + 1)] + bias_scr[masked, head]
                    s_scr[j, head] = sb
                    m_rows[head] = jnp.maximum(jnp.max(sb, axis=0, keepdims=True), sink_ref[layer, head] * LOG2E)

        inv = [None] * H
        for head in range(H):
            e = jnp.exp2(s_scr[j, head] - m_rows[head])
            den = jnp.sum(e, axis=0, keepdims=True) + jnp.exp2(sink_ref[layer, head] * LOG2E - m_rows[head])
            inv[head] = 1.0 / den
            p_scr[j, :, 128 * head:128 * (head + 1)] = e.astype(BF16)

        for hk in range(S_KV_HEADS):
            o_t = _dot(v_t[HEAD_DIM * hk:HEAD_DIM * (hk + 1), k0:k0 + 2 * W],
                       p_scr[j, :, 128 * G * hk:128 * G * (hk + 1)])
            for g in range(G):
                head = G * hk + g
                ot_scr[j, HEAD_DIM * head:HEAD_DIM * (head + 1), :] = o_t[:, 128 * g:128 * (g + 1)] * inv[head]
        o_ref[k0:k0 + W, :] = ot_scr[j].T

    @pl.when(n == pl.num_programs(1) - 1)
    def _():
        kt_ref[...] = kv_ref[(QB - 1) * W:QB * W, 0:S_KV_WIDTH].T
        vt_ref[...] = kv_ref[(QB - 1) * W:QB * W, S_KV_WIDTH:2 * S_KV_WIDTH].T


def _swa_prompt(l, z3, rel_bias, sinks, bucket, *, qb):
    b, s, _ = z3.shape
    W = WINDOW
    smem = pl.BlockSpec(memory_space=pltpu.SMEM)
    return pl.pallas_call(
        functools.partial(_swa_prompt_kernel, layer=l),
        grid=(b, s // (qb * W)),
        in_specs=[
            smem, smem, _const_spec((2 * W, W)),
            pl.BlockSpec((None, qb * W, S_WIDTH), lambda i, n: (i, n, Z_SQ // S_WIDTH)),
            pl.BlockSpec((None, qb * W, 2 * S_KV_WIDTH), lambda i, n: (i, n, Z_SKV // (2 * S_KV_WIDTH))),
            pl.BlockSpec((None, W, 2 * S_KV_WIDTH),
                         lambda i, n: (i, jnp.maximum(n * qb - 1, 0), Z_SKV // (2 * S_KV_WIDTH))),
        ],
        out_specs=[
            pl.BlockSpec((None, qb * W, S_WIDTH), lambda i, n: (i, n, 0)),
            pl.BlockSpec((None, S_KV_WIDTH, W), lambda i, n: (i, 0, 0)),
            pl.BlockSpec((None, S_KV_WIDTH, W), lambda i, n: (i, 0, 0)),
        ],
        out_shape=[
            jax.ShapeDtypeStruct((b, s, S_WIDTH), F32),
            jax.ShapeDtypeStruct((b, S_KV_WIDTH, W), F32),
            jax.ShapeDtypeStruct((b, S_KV_WIDTH, W), F32),
        ],
        scratch_shapes=[
            pltpu.VMEM((2, S_Q_HEADS, 2 * W, W), F32),
            pltpu.VMEM((qb, S_Q_HEADS, 2 * W, W), F32),
            pltpu.VMEM((qb, 2 * W, S_Q_HEADS * W), BF16),
            pltpu.VMEM((qb, S_WIDTH, W), F32),
        ],
        compiler_params=_params("arbitrary", "arbitrary"),
        name="swa_prompt",
    )(rel_bias, sinks, bucket, z3, z3, z3)


MIX_SUB = 1024


def _mix_out_prompt_kernel(x_ref, hm_ref, hc_ref, hs_ref, wout_ref, g2_ref, wq_ref, mk_ref, mv_ref, wo_ref, o_ref,
                           x1_scr, qx_scr, p_scr):
    tm = x_ref.shape[0]
    mem = mk_ref.shape[1]
    mk_t = mk_ref[...]
    mv_t = mv_ref[...]
    row_head = lax.broadcasted_iota(jnp.int32, mk_t.shape, 0) // HEAD_DIM
    k_heads = [jnp.where(row_head == h, mk_t, 0.0).astype(BF16) for h in range(X_HEADS)]
    v_cat = jnp.concatenate([jnp.where(row_head == h, mv_t, 0.0).astype(BF16) for h in range(X_HEADS)], axis=1)
    subs = [slice(r0, r0 + MIX_SUB) for r0 in range(0, tm, MIX_SUB)]
    for rows in subs:
        cat = jnp.concatenate([hm_ref[rows, :].astype(BF16), hc_ref[rows, :].astype(BF16),
                               hs_ref[rows, :].astype(BF16)], axis=1)
        x1 = x_ref[rows, :] + _dot(cat, wout_ref[...])
        x1_scr[rows, :] = x1
        qx = _dot(_rms(x1, g2_ref[...]).astype(BF16), wq_ref[...])
        qx_scr[rows, :] = (qx * (QK_SCALE * LOG2E)).astype(BF16)
    for rows in subs:
        qx = qx_scr[rows, :]
        for h in range(X_HEADS):
            s = _dot(qx, k_heads[h])
            e = jnp.exp2(s - jnp.max(s, axis=1, keepdims=True))
            p_scr[rows, mem * h:mem * (h + 1)] = (e * (1.0 / jnp.sum(e, axis=1, keepdims=True))).astype(BF16)
    for rows in subs:
        o = _dot_nt(p_scr[rows, :], v_cat)
        o_ref[rows, :] = x1_scr[rows, :] + _dot(o.astype(BF16), wo_ref[...])


def _mix_out_prompt(l, x3, hm, hc, hs, wout, g2, wq, mkv, wo, *, tm):
    b, s, d = x3.shape
    tm = min(tm, s)
    mem = mkv.shape[2]
    row = lambda w: pl.BlockSpec((None, tm, w), lambda i, t: (i, t, 0))
    return pl.pallas_call(
        _mix_out_prompt_kernel,
        grid=(b, s // tm),
        in_specs=[
            row(d), row(M_WIDTH), row(C_WIDTH), row(S_WIDTH),
            _weight_spec(wout, l), _weight_spec(g2, l), _weight_spec(wq, l),
            pl.BlockSpec((None, X_WIDTH, mem), lambda i, t: (i, 0, 0)),
            pl.BlockSpec((None, X_WIDTH, mem), lambda i, t: (i, 1, 0)),
            _weight_spec(wo, l),
        ],
        out_specs=row(d),
        out_shape=jax.ShapeDtypeStruct((b, s, d), F32),
        scratch_shapes=[pltpu.VMEM((tm, d), F32), pltpu.VMEM((tm, X_WIDTH), BF16),
                        pltpu.VMEM((tm, X_HEADS * mem), BF16)],
        compiler_params=_params("arbitrary", "arbitrary"),
        name="mix_out_prompt",
    )(x3, hm, hc, hs, wout, g2, wq, mkv, mkv, wo)


FFN_PAD = 8


def _ffn_prompt_kernel(x_ref, g3_ref, wup_ref, cw_ref, cb_ref, wdn_ref, gf_ref, o_ref, tail_ref,
                       carry, act, *, final):
    tm = x_ref.shape[0]
    dff = wdn_ref.shape[0]

    @pl.when(pl.program_id(1) == 0)
    def _():
        carry[...] = jnp.zeros(carry.shape, F32)

    x = x_ref[...]
    h = _rms(x, g3_ref[...]).astype(BF16)
    w = FFN_CHUNK
    row8 = lax.broadcasted_iota(jnp.int32, (FFN_PAD, w), 0)
    for c in range(0, dff, w):
        a = _dot(h, wup_ref[:, c:c + w])
        g = _dot(h, wup_ref[:, dff + c:dff + c + w])
        prev = carry[:, c:c + w]
        carry[:, c:c + w] = g[tm - FFN_PAD:tm, :]
        taps = []
        for s in (2, 1):
            gs = pltpu.roll(g, s, axis=0)
            head = jnp.where(row8 < s, pltpu.roll(prev, s, axis=0), gs[0:FFN_PAD, :])
            taps.append(jnp.concatenate([head, gs[FFN_PAD:, :]], axis=0))
        gc = (cw_ref[0:1, c:c + w] * taps[0] + cw_ref[1:2, c:c + w] * taps[1]
              + cw_ref[2:3, c:c + w] * g + cb_ref[:, c:c + w])
        act[:, c:c + w] = (_swish(gc) * a).astype(BF16)
    y = x + _dot(act[...], wdn_ref[...])
    o_ref[...] = _rms(y, gf_ref[...]) if final else y
    tail_ref[...] = carry[...]


def _ffn_prompt(l, x3, g3, wup, cw, cb, wdn, gf, *, tm, final):
    b, s, d = x3.shape
    tm = min(tm, s)
    dff = wdn.shape[1]
    return pl.pallas_call(
        functools.partial(_ffn_prompt_kernel, final=final),
        grid=(b, s // tm),
        in_specs=[
            pl.BlockSpec((None, tm, d), lambda i, t: (i, t, 0)),
            _weight_spec(g3, l), _weight_spec(wup, l), _weight_spec(cw, l), _weight_spec(cb, l),
            _weight_spec(wdn, l), _const_spec((1, d)),
        ],
        out_specs=[
            pl.BlockSpec((None, tm, d), lambda i, t: (i, t, 0)),
            pl.BlockSpec((None, FFN_PAD, dff), lambda i, t: (i, 0, 0)),
        ],
        out_shape=[jax.ShapeDtypeStruct((b, s, d), F32), jax.ShapeDtypeStruct((b, FFN_PAD, dff), F32)],
        scratch_shapes=[
            pltpu.VMEM((FFN_PAD, dff), F32),
            pltpu.VMEM((tm, dff), BF16),
        ],
        compiler_params=_params("arbitrary", "arbitrary"),
        name="ffn_prompt",
    )(x3, g3, wup, cw, cb, wdn, gf)


SWA_SAMPLE_BLOCK = 32
XATTN_SAMPLE_BLOCK = 16
CONV_SAMPLE_BLOCK = 32


def _mlstm_sample_kernel(gb_ref, q_ref, k_ref, v_ref, o_ref, g_ref, ng_ref, c_ref, n_ref, m_ref, *rest, layer):
    h_ref, co_ref, no_ref, mo_ref, kw_scr = rest[-5:]
    _copy_earlier_layers(rest[:-5], (co_ref, no_ref, mo_ref))
    h = pl.program_id(0)
    i_pre = g_ref[pl.ds(h, 1), :] + gb_ref[layer, h]
    f_pre = g_ref[pl.ds(M_HEADS + h, 1), :] + gb_ref[layer, M_HEADS + h]
    a = _log_sigmoid(f_pre) + m_ref[pl.ds(h, 1), :]
    m_t = jnp.maximum(a, i_pre)
    w_old = jnp.exp(a - m_t)
    w_new = jnp.exp(i_pre - m_t)
    q = q_ref[...]
    k = k_ref[...] * QK_SCALE
    v = v_ref[...]
    n_old = n_ref[...]
    kw_scr[...] = k * w_new

    def body(d, acc):
        c_old = c_ref[d]
        co_ref[layer, d] = w_old * c_old + kw_scr[pl.ds(d, 1), :] * v
        return acc + q_ref[pl.ds(d, 1), :] * c_old

    qc = lax.fori_loop(0, HEAD_DIM, body, jnp.zeros(v.shape, F32), unroll=8)
    s = jnp.sum(q * k, axis=0, keepdims=True) * w_new
    num = w_old * qc + s * v
    den = w_old * jnp.sum(q * n_old, axis=0, keepdims=True) + s
    hh = num / jnp.maximum(jnp.abs(den), jnp.exp(-m_t))
    hh = hh * lax.rsqrt(jnp.mean(hh * hh, axis=0, keepdims=True) + EPS) * ng_ref[...]
    h_ref[...] = hh * _sigmoid(o_ref[...])
    no_ref[layer] = w_old * n_old + kw_scr[...]
    mo_ref[layer, pl.ds(h, 1), :] = m_t


def _mlstm_sample(l, zt, gb, ngt, c_all, n_all, m_all, prev):
    bsz = zt.shape[1]
    D = HEAD_DIM
    feat = lambda off: pl.BlockSpec((D, bsz), lambda h: (off // D + h, 0))
    c_spec = lambda n: pl.BlockSpec((n, None, D, D, bsz), lambda h: (0, h, 0, 0, 0))
    n_spec = lambda n: pl.BlockSpec((n, None, D, bsz), lambda h: (0, h, 0, 0))
    m_spec = lambda n: pl.BlockSpec((n, M_HEADS, bsz), lambda h: (0, 0, 0))
    prev_specs = [c_spec(l), n_spec(l), m_spec(l)] if prev else []
    return pl.pallas_call(
        functools.partial(_mlstm_sample_kernel, layer=l),
        grid=(M_HEADS,),
        in_specs=[
            pl.BlockSpec(memory_space=pltpu.SMEM),
            feat(Z_M), feat(Z_M + M_WIDTH), feat(Z_M + 2 * M_WIDTH), feat(Z_M + 3 * M_WIDTH),
            pl.BlockSpec((SUBLANES, bsz), lambda h: (Z_G // SUBLANES, 0)),
            pl.BlockSpec((None, D, bsz), lambda h: (l, h, 0)),
            pl.BlockSpec((None, None, D, D, bsz), lambda h: (l, h, 0, 0, 0)),
            pl.BlockSpec((None, None, D, bsz), lambda h: (l, h, 0, 0)),
            pl.BlockSpec((None, M_HEADS, bsz), lambda h: (l, 0, 0)),
        ] + prev_specs,
        out_specs=[pl.BlockSpec((D, bsz), lambda h: (h, 0)), c_spec(l + 1), n_spec(l + 1), m_spec(l + 1)],
        out_shape=[
            jax.ShapeDtypeStruct((M_WIDTH, bsz), F32),
            jax.ShapeDtypeStruct((l + 1, M_HEADS, D, D, bsz), F32),
            jax.ShapeDtypeStruct((l + 1, M_HEADS, D, bsz), F32),
            jax.ShapeDtypeStruct((l + 1, M_HEADS, bsz), F32),
        ],
        scratch_shapes=[pltpu.VMEM((D, bsz), F32)],
        compiler_params=_params("arbitrary"),
        name="mlstm_sample",
    )(gb, zt, zt, zt, zt, zt, ngt, c_all, n_all, m_all, *prev)


def _conv_sample_kernel(z_ref, hist_ref, w_ref, cb_ref, lg_ref, lb_ref, *rest):
    h_ref, hist_out_ref = rest[-2:]
    _copy_earlier_layers(rest[:-2], (hist_out_ref,))
    l = hist_out_ref.shape[0] - 1
    nh = C_KERNEL - 1
    u = z_ref[:, 0:C_WIDTH] * _sigmoid(z_ref[:, C_WIDTH:2 * C_WIDTH])
    acc = cb_ref[...] + w_ref[nh:nh + 1, :] * u
    for j in range(nh):
        acc = acc + w_ref[j:j + 1, :] * hist_ref[j]
    mu = jnp.mean(acc, axis=-1, keepdims=True)
    xc = acc - mu
    y = xc * lax.rsqrt(jnp.mean(xc * xc, axis=-1, keepdims=True) + EPS) * lg_ref[...] + lb_ref[...]
    h_ref[...] = _swish(y)
    for j in range(nh - 1):
        hist_out_ref[l, j] = hist_ref[j + 1]
    hist_out_ref[l, nh - 1] = u


def _conv_sample(l, z, hist_all, w, cb, lg, lb, prev):
    bsz = z.shape[0]
    nh = hist_all.shape[1]
    R = min(CONV_SAMPLE_BLOCK, bsz)
    hist_spec = lambda n: pl.BlockSpec((n, nh, R, C_WIDTH), lambda i: (0, 0, i, 0))
    return pl.pallas_call(
        _conv_sample_kernel,
        grid=(bsz // R,),
        in_specs=[
            pl.BlockSpec((R, 2 * C_WIDTH), lambda i: (i, Z_C // (2 * C_WIDTH))),
            pl.BlockSpec((None, nh, R, C_WIDTH), lambda i: (l, 0, i, 0)),
            _weight_spec(w, l), _weight_spec(cb, l), _weight_spec(lg, l), _weight_spec(lb, l),
        ] + [hist_spec(l)] * len(prev),
        out_specs=[pl.BlockSpec((R, C_WIDTH), lambda i: (i, 0)), hist_spec(l + 1)],
        out_shape=[jax.ShapeDtypeStruct((bsz, C_WIDTH), F32), jax.ShapeDtypeStruct((l + 1, nh, bsz, C_WIDTH), F32)],
        compiler_params=_params("arbitrary"),
        name="conv_sample",
    )(z, hist_all, w, cb, lg, lb, *prev)


def _copy_earlier_layers(prev_refs, out_refs):
    for p_ref, o_ref in zip(prev_refs, out_refs):
        o_ref[0:p_ref.shape[0]] = p_ref[...]


def _swa_sample_kernel(q_ref, kvn_ref, kv2_ref, kc_ref, vc_ref, bias_ref, aux_ref, *rest):
    o_ref, ko_ref, vo_ref = rest[-3:]
    _copy_earlier_layers(rest[:-3], (ko_ref, vo_ref))
    l = ko_ref.shape[0] - 1
    R = q_ref.shape[0]
    W = kc_ref.shape[2]
    H = S_Q_HEADS
    shape = (R, H, 128)
    row = lax.broadcasted_iota(jnp.int32, shape, 1)
    lane_half = lax.broadcasted_iota(jnp.int32, shape, 2) // HEAD_DIM
    q_half = row % 2
    kv_head = row // (H // S_KV_HEADS)
    qs = jnp.zeros(shape, F32)
    for j in range(H // 2):
        qs = jnp.where(row // 2 == j, q_ref[:, :, 128 * j:128 * (j + 1)], qs)
    q8 = jnp.where(lane_half == kv_head, jnp.where(q_half == kv_head, qs, pltpu.roll(qs, HEAD_DIM, axis=2)), 0.0)
    k_new = kvn_ref[:, :, 0:S_KV_WIDTH]
    v_new = kvn_ref[:, :, S_KV_WIDTH:2 * S_KV_WIDTH]
    s = jnp.einsum("bqd,bdk->bqk", q8.astype(BF16), kc_ref[...].astype(BF16), preferred_element_type=F32) * QK_SCALE
    s = s + bias_ref[...][None]
    s_new = jnp.sum(q8 * k_new, axis=2, keepdims=True) * QK_SCALE + aux_ref[:, 0:1][None]
    sink = aux_ref[:, 1:2][None]
    m = jnp.maximum(jnp.maximum(jnp.max(s, axis=2, keepdims=True), s_new), sink)
    e = jnp.exp(s - m)
    e_new = jnp.exp(s_new - m)
    inv = 1.0 / (jnp.sum(e, axis=2, keepdims=True) + e_new + jnp.exp(sink - m))
    o8 = jnp.einsum("bqk,bdk->bqd", e.astype(BF16), vc_ref[...].astype(BF16), preferred_element_type=F32)
    o8 = (o8 + e_new * v_new) * inv
    o8 = jnp.where(lane_half == q_half, jnp.where(q_half == kv_head, o8, pltpu.roll(o8, HEAD_DIM, axis=2)), 0.0)
    for j in range(H // 2):
        o_ref[:, :, 128 * j:128 * (j + 1)] = jnp.sum(jnp.where(row // 2 == j, o8, 0.0), axis=1, keepdims=True)
    k_cols = kv2_ref[:, 0:S_KV_WIDTH].T
    v_cols = kv2_ref[:, S_KV_WIDTH:2 * S_KV_WIDTH].T
    last = lax.broadcasted_iota(jnp.int32, (S_KV_WIDTH, W), 1) == W - 1
    for r in range(R):
        ko_ref[l, r] = jnp.where(last, k_cols[:, r:r + 1], pltpu.roll(kc_ref[r], W - 1, axis=1))
        vo_ref[l, r] = jnp.where(last, v_cols[:, r:r + 1], pltpu.roll(vc_ref[r], W - 1, axis=1))


def _swa_sample(l, z, z3, kc_all, vc_all, bias, aux, prev):
    bsz = z.shape[0]
    R = min(SWA_SAMPLE_BLOCK, bsz)
    W = kc_all.shape[3]
    cache_in = pl.BlockSpec((None, R, S_KV_WIDTH, W), lambda i: (l, i, 0, 0))
    cache_prev = pl.BlockSpec((l, R, S_KV_WIDTH, W), lambda i: (0, i, 0, 0))
    cache_out = pl.BlockSpec((l + 1, R, S_KV_WIDTH, W), lambda i: (0, i, 0, 0))
    return pl.pallas_call(
        _swa_sample_kernel,
        grid=(bsz // R,),
        in_specs=[
            pl.BlockSpec((R, 1, S_WIDTH), lambda i: (i, 0, Z_SQ // S_WIDTH)),
            pl.BlockSpec((R, 1, 2 * S_KV_WIDTH), lambda i: (i, 0, Z_SKV // (2 * S_KV_WIDTH))),
            pl.BlockSpec((R, 2 * S_KV_WIDTH), lambda i: (i, Z_SKV // (2 * S_KV_WIDTH))),
            cache_in, cache_in,
            _const_spec((S_Q_HEADS, 128)), _weight_spec(aux, l),
        ] + [cache_prev] * len(prev),
        out_specs=[pl.BlockSpec((R, 1, S_WIDTH), lambda i: (i, 0, 0)), cache_out, cache_out],
        out_shape=[
            jax.ShapeDtypeStruct((bsz, 1, S_WIDTH), F32),
            jax.ShapeDtypeStruct((l + 1, bsz, S_KV_WIDTH, W), F32),
            jax.ShapeDtypeStruct((l + 1, bsz, S_KV_WIDTH, W), F32),
        ],
        compiler_params=_params("arbitrary"),
        name="swa_sample",
    )(z3, z3, z, kc_all, vc_all, bias, aux, *prev)


def _mix_out_sample_kernel(x_ref, hmt_ref, hc_ref, hs_ref, wout_ref, g2_ref, wq_ref, x1_ref, q_ref):
    cat = jnp.concatenate([hmt_ref[...].T.astype(BF16), hc_ref[...].astype(BF16), hs_ref[...].astype(BF16)], axis=1)
    x1 = x_ref[...] + _dot(cat, wout_ref[...])
    x1_ref[...] = x1
    q_ref[...] = _dot(_rms(x1, g2_ref[...]).astype(BF16), wq_ref[...])


def _mix_out_sample(l, x, hm, hc, hs, wout, g2, wq):
    bsz, d = x.shape
    full = lambda a: pl.BlockSpec(a.shape, lambda i: (0,) * a.ndim)
    args = (x, hm, hc, hs, wout, g2, wq)
    return pl.pallas_call(
        _mix_out_sample_kernel,
        grid=(1,),
        in_specs=[full(x), full(hm), full(hc), full(hs), _weight_spec(wout, l), _weight_spec(g2, l),
                  _weight_spec(wq, l)],
        out_specs=[pl.BlockSpec((bsz, d), lambda i: (0, 0)), pl.BlockSpec((bsz, X_WIDTH), lambda i: (0, 0))],
        out_shape=[jax.ShapeDtypeStruct((bsz, d), F32), jax.ShapeDtypeStruct((bsz, X_WIDTH), F32)],
        compiler_params=_params("arbitrary"),
        name="mix_out_sample",
    )(*args)


def _xattn_sample_kernel(q_ref, k_ref, v_ref, o_ref):
    R = k_ref.shape[0]
    shape = (R, SUBLANES, X_WIDTH)
    row = lax.broadcasted_iota(jnp.int32, shape, 1)
    lane_head = lax.broadcasted_iota(jnp.int32, shape, 2) // HEAD_DIM
    own = row == lane_head
    q8 = jnp.where(own, jnp.broadcast_to(q_ref[...], shape), 0.0).astype(BF16)
    s = jnp.einsum("bqd,bdk->bqk", q8, k_ref[...].astype(BF16), preferred_element_type=F32) * QK_SCALE
    e = jnp.exp(s - jnp.max(s, axis=2, keepdims=True))
    p = (e / jnp.sum(e, axis=2, keepdims=True)).astype(BF16)
    o8 = jnp.einsum("bqk,bdk->bqd", p, v_ref[...].astype(BF16), preferred_element_type=F32)
    o_ref[...] = jnp.sum(jnp.where(own, o8, 0.0), axis=1, keepdims=True)


def _xattn_sample(l, q3, k_all, v_all):
    _, bsz, w, mem = k_all.shape
    R = min(XATTN_SAMPLE_BLOCK, bsz)
    kv = pl.BlockSpec((None, R, w, mem), lambda i: (l, i, 0, 0))
    qo = pl.BlockSpec((R, 1, w), lambda i: (i, 0, 0))
    return pl.pallas_call(
        _xattn_sample_kernel,
        grid=(bsz // R,),
        in_specs=[qo, kv, kv],
        out_specs=qo,
        out_shape=jax.ShapeDtypeStruct((bsz, 1, w), F32),
        compiler_params=_params("arbitrary"),
        name="xattn_sample",
    )(q3, k_all, v_all)


def _ffn_sample_kernel(x1_ref, ox_ref, wo_ref, g3_ref, wup_ref, cw_ref, cb_ref, wdn_ref, gf_ref, hist_ref,
                       *rest, final):
    o_ref, hist_out_ref, act = rest[-3:]
    _copy_earlier_layers(rest[:-3], (hist_out_ref,))
    l = hist_out_ref.shape[0] - 1
    dff = wdn_ref.shape[0]
    x = x1_ref[...] + _dot(ox_ref[...].astype(BF16), wo_ref[...])
    h = _rms(x, g3_ref[...]).astype(BF16)
    for c in range(0, dff, FFN_CHUNK):
        a = _dot(h, wup_ref[:, c:c + FFN_CHUNK])
        g = _dot(h, wup_ref[:, dff + c:dff + c + FFN_CHUNK])
        h1 = hist_ref[:, 1, c:c + FFN_CHUNK]
        gc = (cw_ref[0:1, c:c + FFN_CHUNK] * hist_ref[:, 0, c:c + FFN_CHUNK]
              + cw_ref[1:2, c:c + FFN_CHUNK] * h1
              + cw_ref[2:3, c:c + FFN_CHUNK] * g + cb_ref[:, c:c + FFN_CHUNK])
        act[:, c:c + FFN_CHUNK] = (_swish(gc) * a).astype(BF16)
        hist_out_ref[l, :, 0, c:c + FFN_CHUNK] = h1
        hist_out_ref[l, :, 1, c:c + FFN_CHUNK] = g
    y = x + _dot(act[...], wdn_ref[...])
    o_ref[...] = _rms(y, gf_ref[...]) if final else y


def _ffn_sample(l, x1, ox, wo, g3, wup, cw, cb, wdn, gf, hist_all, prev, *, final):
    bsz, d = x1.shape
    dff = wdn.shape[1]
    full = lambda a: pl.BlockSpec(a.shape, lambda i: (0,) * a.ndim)
    hshape = hist_all.shape[1:]
    return pl.pallas_call(
        functools.partial(_ffn_sample_kernel, final=final),
        grid=(1,),
        in_specs=[full(x1), full(ox), _weight_spec(wo, l), _weight_spec(g3, l), _weight_spec(wup, l),
                  _weight_spec(cw, l), _weight_spec(cb, l),
                  _weight_spec(wdn, l), full(gf), pl.BlockSpec((None,) + hshape, lambda i: (l, 0, 0, 0))]
                 + [full(p) for p in prev],
        out_specs=[pl.BlockSpec((bsz, d), lambda i: (0, 0)), pl.BlockSpec((l + 1,) + hshape, lambda i: (0, 0, 0, 0))],
        out_shape=[jax.ShapeDtypeStruct((bsz, d), F32), jax.ShapeDtypeStruct((l + 1,) + hshape, F32)],
        scratch_shapes=[pltpu.VMEM((bsz, dff), BF16)],
        compiler_params=_params("arbitrary"),
        name="ffn_sample",
    )(x1, ox, wo, g3, wup, cw, cb, wdn, gf, hist_all, *prev)


def _t5_buckets(dist):
    n = np.maximum(dist, 0)
    max_exact = N_BUCKETS // 2
    nf = np.maximum(n, max_exact).astype(np.float32)
    large = max_exact + (np.log(nf / np.float32(max_exact)) / np.float32(math.log(MAX_DISTANCE / max_exact))
                         * np.float32(N_BUCKETS - max_exact)).astype(np.int32)
    return np.where(n < max_exact, n, np.minimum(large, N_BUCKETS - 1))


def _prompt_buckets():
    W = WINDOW
    dist = np.arange(W)[None, :] + W - np.arange(2 * W)[:, None]
    band = (dist >= 0) & (dist < W)
    return np.where(band, _t5_buckets(dist), -1).astype(np.int32)


def _swa_tables(rel_bias):
    W = WINDOW
    dist_c = W - np.arange(W)
    tab = jnp.transpose(rel_bias[_t5_buckets(dist_c)], (1, 0))
    cache_bias = jnp.where((dist_c < W)[None], tab, NEG_INF)
    return cache_bias, rel_bias[0]


def kernel(x_prompt, x_sample, mem_prompt, state_mlstm_C, state_mlstm_n, state_mlstm_m, state_conv, cache_swa_k, cache_swa_v, cache_mem_k, cache_mem_v, state_ffn_conv, rel_bias, norm1_g, w_in, b_i, b_f, mlstm_norm_g, conv_w, conv_b, conv_ln_g, conv_ln_b, swa_sinks, w_out, norm2_g, w_xq, w_xk, w_xv, w_xo, norm3_g, w_up, ffn_conv_w, ffn_conv_b, w_down, final_norm_g):
    depth = w_in.shape[0]
    bp, seq, d = x_prompt.shape
    bs = x_sample.shape[0]
    mem = mem_prompt.shape[1]
    dff = w_down.shape[1]
    W = WINDOW
    nh = C_KERNEL - 1

    xp = x_prompt
    xs = x_sample.reshape(bs, d)
    gf = final_norm_g.reshape(1, d)
    c_all = jnp.transpose(state_mlstm_C, (0, 2, 3, 4, 1))
    n_all = jnp.transpose(state_mlstm_n, (0, 2, 3, 1))
    m_all = jnp.transpose(state_mlstm_m, (0, 2, 1))
    hist_all = jnp.transpose(state_conv, (0, 2, 1, 3))
    kc_all = jnp.transpose(cache_swa_k, (0, 1, 3, 4, 2)).reshape(depth, bs, S_KV_WIDTH, W)
    vc_all = jnp.transpose(cache_swa_v, (0, 1, 3, 4, 2)).reshape(depth, bs, S_KV_WIDTH, W)
    mk_all = jnp.transpose(cache_mem_k, (0, 1, 3, 4, 2)).reshape(depth, bs, X_WIDTH, mem)
    mv_all = jnp.transpose(cache_mem_v, (0, 1, 3, 4, 2)).reshape(depth, bs, X_WIDTH, mem)
    pm_c, pm_n, pm_m, p_conv, p_k, p_v, p_mk, p_mv, p_ffn = ([] for _ in range(9))
    s_mlstm, s_conv, s_kv, s_ffn = [], [], [], []

    win_t = jnp.swapaxes(w_in, 1, 2)
    wout = w_out.astype(BF16)
    wxq = w_xq.astype(BF16)
    wxkv = jnp.concatenate([w_xk, w_xv], axis=2).astype(BF16)
    wxo = w_xo.astype(BF16)
    wup = w_up.astype(BF16)
    wdn = w_down.astype(BF16)
    bucket = jnp.asarray(_prompt_buckets())

    rows = lambda a: a.reshape(depth, 1, -1)
    g1, g2, g3 = rows(norm1_g), rows(norm2_g), rows(norm3_g)
    gb8 = jnp.concatenate([b_i, b_f], axis=1)
    gbias = rows(jnp.concatenate([gb8, jnp.zeros((depth, 128 - 2 * M_HEADS), F32)], axis=1))
    ng = rows(mlstm_norm_g)
    ngt = jnp.broadcast_to(mlstm_norm_g[:, :, None], (depth, M_WIDTH, bs))
    cw = jnp.concatenate([conv_w, jnp.zeros((depth, CONV_PAD - C_KERNEL, C_WIDTH), F32)], axis=1)
    cb, lg, lb = rows(conv_b), rows(conv_ln_g), rows(conv_ln_b)
    fcw = jnp.concatenate([ffn_conv_w, jnp.zeros((depth, SUBLANES - FFN_KERNEL, dff), F32)], axis=1)
    fcb = rows(ffn_conv_b)
    cache_bias, bias0 = _swa_tables(rel_bias)
    aux = jnp.concatenate([jnp.broadcast_to(bias0[None, :, None], (depth, S_Q_HEADS, 1)), swa_sinks[:, :, None],
                           jnp.zeros((depth, S_Q_HEADS, 126), F32)], axis=2)

    for l in range(depth):
        last = l == depth - 1

        mkv = _mem_kv(l, mem_prompt, wxkv)
        z, win = _in_proj(l, xp.reshape(bp * seq, d), g1, win_t, tm=1024)
        z = z.reshape(bp, seq, Z_WIDTH)
        hm, cpair, npair, mm = _mlstm_prompt(l, z, gbias, ng)
        hc, ctail = _conv_prompt(l, z, cw, cb, lg, lb, tc=min(1024, seq))
        hs, kt, vt = _swa_prompt(l, z, rel_bias, swa_sinks, bucket, qb=min(8, seq // W))
        xp = _mix_out_prompt(l, xp, hm, hc, hs, wout, g2, wxq, mkv, wxo, tm=1024)
        xp, ftail = _ffn_prompt(l, xp, g3, wup, fcw, fcb, wdn, gf, tm=512, final=last)
        half = lambda h: slice(HEAD_DIM * (h % 2), HEAD_DIM * (h % 2 + 1))
        pm_c.append(jnp.stack([jnp.swapaxes(cpair[:, h // 2, half(h), half(h)], 1, 2) for h in range(M_HEADS)], axis=1))
        pm_n.append(jnp.stack([npair[:, h // 2, h % 2, half(h)] for h in range(M_HEADS)], axis=1))
        pm_m.append(mm[:, :, 0, 0])
        p_conv.append(ctail[:, CONV_PAD - nh:, :])
        p_k.append(jnp.transpose(kt.reshape(bp, S_KV_HEADS, HEAD_DIM, W), (0, 3, 1, 2)))
        p_v.append(jnp.transpose(vt.reshape(bp, S_KV_HEADS, HEAD_DIM, W), (0, 3, 1, 2)))
        p_mk.append(jnp.transpose(mkv[:, 0:X_WIDTH, :].reshape(bp, X_HEADS, HEAD_DIM, mem), (0, 3, 1, 2)))
        p_mv.append(jnp.transpose(mkv[:, X_WIDTH:, :].reshape(bp, X_HEADS, HEAD_DIM, mem), (0, 3, 1, 2)))
        p_ffn.append(ftail[:, FFN_PAD - (FFN_KERNEL - 1):, :])

        zs, zst = _sample_in(l, xs, g1, win)
        hmt_s, *s_mlstm = _mlstm_sample(l, zst, gb8, ngt, c_all, n_all, m_all, s_mlstm)
        hc_s, *s_conv = _conv_sample(l, zs, hist_all, cw, cb, lg, lb, s_conv)
        hs_s, *s_kv = _swa_sample(l, zs, zs.reshape(bs, 1, Z_WIDTH), kc_all, vc_all, cache_bias, aux, s_kv)
        x1, qx = _mix_out_sample(l, xs, hmt_s, hc_s, hs_s.reshape(bs, S_WIDTH), wout, g2, wxq)
        ox = _xattn_sample(l, qx.reshape(bs, 1, X_WIDTH), mk_all, mv_all)
        xs, *s_ffn = _ffn_sample(l, x1, ox.reshape(bs, X_WIDTH), wxo, g3, wup, fcw, fcb, wdn, gf,
                                 state_ffn_conv, s_ffn, final=last)

    st = jnp.stack
    tr = jnp.transpose
    s_c, s_n, s_m = s_mlstm
    s_k, s_v = (a.reshape(depth, bs, S_KV_HEADS, HEAD_DIM, W) for a in s_kv)
    return (xp, xs.reshape(bs, 1, d),
            st(pm_c), st(pm_n), st(pm_m), st(p_conv), st(p_k), st(p_v), st(p_mk), st(p_mv), st(p_ffn),
            tr(s_c, (0, 4, 1, 2, 3)), tr(s_n, (0, 3, 1, 2)), tr(s_m, (0, 2, 1)),
            tr(s_conv[0], (0, 2, 1, 3)), tr(s_k, (0, 1, 4, 2, 3)), tr(s_v, (0, 1, 4, 2, 3)),
            s_ffn[0])
```

```python
import functools
import math

import numpy as np
import jax
import jax.numpy as jnp
from jax import lax
from jax.experimental import pallas as pl
from jax.experimental.pallas import tpu as pltpu

F32 = jnp.float32
BF16 = jnp.bfloat16
EPS = 1e-6
NEG_INF = float("-inf")

HEAD_DIM = 64
M_HEADS = 4
M_WIDTH = M_HEADS * HEAD_DIM
C_WIDTH = 256
C_KERNEL = 31
S_Q_HEADS = 8
S_KV_HEADS = 2
S_WIDTH = S_Q_HEADS * HEAD_DIM
S_KV_WIDTH = S_KV_HEADS * HEAD_DIM
WINDOW = 128
N_BUCKETS = 32
MAX_DISTANCE = 128
X_HEADS = 4
X_WIDTH = X_HEADS * HEAD_DIM
FFN_KERNEL = 3
QK_SCALE = HEAD_DIM ** -0.5
LOG2E = math.log2(math.e)

Z_M = 0
Z_C = 4 * M_WIDTH
Z_SQ = Z_C + 2 * C_WIDTH
Z_SKV = Z_SQ + S_WIDTH
Z_G = Z_SKV + 2 * S_KV_WIDTH
LANES = 128
SUBLANES = 8
Z_WIDTH = Z_G + LANES
VMEM_LIMIT = 56 * 1024 * 1024

M_CHUNK = 128
FFN_CHUNK = 256


def _params(*sem):
    return pltpu.CompilerParams(dimension_semantics=sem, vmem_limit_bytes=VMEM_LIMIT)


def _const_spec(shape):
    nd = len(shape)
    return pl.BlockSpec(shape, lambda *_: (0,) * nd, pipeline_mode=pl.Buffered(1))


def _weight_spec(w, l):
    nd = w.ndim - 1
    return pl.BlockSpec((None,) + w.shape[1:], lambda *_: (l,) + (0,) * nd, pipeline_mode=pl.Buffered(1))


def _rms(x, g):
    return x * lax.rsqrt(jnp.mean(x * x, axis=-1, keepdims=True) + EPS) * g


def _sigmoid(x):
    return 0.5 * jnp.tanh(0.5 * x) + 0.5


def _swish(x):
    h = 0.5 * x
    return h + h * jnp.tanh(h)


def _log_sigmoid(x):
    return jnp.minimum(x, 0.0) - jnp.log1p(jnp.exp(-jnp.abs(x)))


def _dot(a, b):
    return jnp.dot(a, b, preferred_element_type=F32)


def _dot_nt(a, b):
    return lax.dot_general(a, b, (((1,), (1,)), ((), ())), preferred_element_type=F32)


def _dot_tn(a, b):
    return lax.dot_general(a, b, (((0,), (0,)), ((), ())), preferred_element_type=F32)


IN_PROJ_ROWS = 512
IN_PROJ_COLS = 512
IN_GATES = 4 * M_WIDTH
W_PREP_ROWS = 256


def _in_proj_kernel(x_ref, g_ref, wt_ref, o_ref, wprep_ref, w_scr):
    d = x_ref.shape[1]

    @pl.when(pl.program_id(0) == 0)
    def _():
        n_gate = 2 * M_HEADS
        for src, dst, n in ((0, 0, IN_GATES), (IN_GATES + n_gate, IN_GATES, Z_G - IN_GATES)):
            for c in range(0, n, W_PREP_ROWS):
                w_scr[:, dst + c:dst + c + W_PREP_ROWS] = wt_ref[src + c:src + c + W_PREP_ROWS, :].T.astype(BF16)
        gate_rows = jnp.concatenate([wt_ref[IN_GATES:IN_GATES + n_gate, :], jnp.zeros((LANES - n_gate, d), F32)],
                                    axis=0)
        w_scr[:, Z_G:Z_G + LANES] = gate_rows.T.astype(BF16)
        wprep_ref[...] = w_scr[...]

    tm, n = o_ref.shape
    sub = min(IN_PROJ_ROWS, tm)
    for r0 in range(0, tm, sub):
        h = _rms(x_ref[r0:r0 + sub, :], g_ref[...]).astype(BF16)
        for c in range(0, n, IN_PROJ_COLS):
            w = min(IN_PROJ_COLS, n - c)
            o_ref[r0:r0 + sub, c:c + w] = _dot(h, w_scr[:, c:c + w])


def _in_proj(l, x, g, wt, *, tm):
    m, d = x.shape
    tm = min(tm, m)
    return pl.pallas_call(
        _in_proj_kernel,
        grid=(m // tm,),
        in_specs=[pl.BlockSpec((tm, d), lambda i: (i, 0)), _weight_spec(g, l), _weight_spec(wt, l)],
        out_specs=[pl.BlockSpec((tm, Z_WIDTH), lambda i: (i, 0)), pl.BlockSpec((d, Z_WIDTH), lambda i: (0, 0))],
        out_shape=[jax.ShapeDtypeStruct((m, Z_WIDTH), F32), jax.ShapeDtypeStruct((d, Z_WIDTH), BF16)],
        scratch_shapes=[pltpu.VMEM((d, Z_WIDTH), BF16)],
        compiler_params=_params("arbitrary"),
        name="in_proj",
    )(x, g, wt)


def _mem_kv_kernel(x_ref, w_ref, o_ref, acc):
    acc[...] = _dot(x_ref[...].astype(BF16), w_ref[...])
    o_ref[...] = acc[...].T


def _mem_kv(l, mem3, w):
    b, mem, d = mem3.shape
    n = w.shape[2]
    return pl.pallas_call(
        _mem_kv_kernel,
        grid=(b,),
        in_specs=[pl.BlockSpec((None, mem, d), lambda i: (i, 0, 0)), _weight_spec(w, l)],
        out_specs=pl.BlockSpec((None, n, mem), lambda i: (i, 0, 0)),
        out_shape=jax.ShapeDtypeStruct((b, n, mem), F32),
        scratch_shapes=[pltpu.VMEM((mem, n), F32)],
        compiler_params=_params("arbitrary"),
        name="mem_kv",
    )(mem3, w)


def _sample_in_kernel(x_ref, g_ref, w_ref, z_ref, zt_ref):
    h = _rms(x_ref[...], g_ref[...]).astype(BF16)
    n = z_ref.shape[1]
    for c in range(0, n, IN_PROJ_COLS):
        w = min(IN_PROJ_COLS, n - c)
        zc = _dot(h, w_ref[:, c:c + w])
        z_ref[:, c:c + w] = zc
        zt_ref[c:c + w, :] = zc.T


def _sample_in(l, x, g, w):
    m, d = x.shape
    n = w.shape[1]
    full = lambda shape: pl.BlockSpec(shape, lambda i: (0,) * len(shape))
    return pl.pallas_call(
        _sample_in_kernel,
        grid=(1,),
        in_specs=[full((m, d)), _weight_spec(g, l), full((d, n))],
        out_specs=[full((m, n)), full((n, m))],
        out_shape=[jax.ShapeDtypeStruct((m, n), F32), jax.ShapeDtypeStruct((n, m), F32)],
        compiler_params=_params("arbitrary"),
        name="sample_in",
    )(x, g, w)


def _mlstm_prompt_kernel(z_ref, g_ref, gb_ref, ng_ref, h_ref, cp_ref, np_ref, m_ref,
                         cp_scr, np_scr, m_scr, ht_scr, st_scr, kw_scr):
    NB, L = z_ref.shape[0], z_ref.shape[1]
    D = HEAD_DIM
    hi = lax.Precision.HIGHEST

    @pl.when(pl.program_id(0) == 0)
    def _():
        cp_scr[...] = jnp.zeros(cp_scr.shape, F32)
        np_scr[...] = jnp.zeros(np_scr.shape, F32)
        m_scr[...] = jnp.zeros(m_scr.shape, F32)

    src = lax.broadcasted_iota(jnp.int32, (L, L), 0)
    qry = lax.broadcasted_iota(jnp.int32, (L, L), 1)
    causal_t = src <= qry
    upper = jnp.where(causal_t, 1.0, 0.0)
    lane_half = lax.broadcasted_iota(jnp.int32, (L, 128), 1) // D
    row8 = lax.broadcasted_iota(jnp.int32, (SUBLANES, 128), 0)

    rows, cols = {}, []
    for b in range(NB):
        g_t = (g_ref[b] + gb_ref[...]).T[0:SUBLANES, :]
        b_rows = jnp.dot(_log_sigmoid(g_t), upper, precision=hi, preferred_element_type=F32)
        to_cols = []
        for h in range(M_HEADS):
            b_row = b_rows[M_HEADS + h:M_HEADS + h + 1, :]
            ci_row = g_t[h:h + 1, :] - b_row
            m_prev = m_scr[b, h, 0:1, 0:1]
            b_last = b_row[:, L - 1:L]
            m_new = jnp.maximum(b_last + m_prev, jnp.max(b_last + ci_row, axis=1, keepdims=True))
            rows[b, h] = dict(b_row=b_row, m_prev=m_prev, m_new=m_new, decay=jnp.exp(b_last + m_prev - m_new))
            to_cols += [ci_row, jnp.exp(b_last + ci_row - m_new)]
        to_cols.append(jnp.zeros((128 - 2 * M_HEADS, L), F32))
        cols.append(jnp.concatenate(to_cols, axis=0).T)

    stats = {}
    for b in range(NB):
        for j in range(M_HEADS // 2):
            slab = 128 * j
            q2 = z_ref[b, :, slab:slab + 128].astype(BF16)
            k2 = z_ref[b, :, M_WIDTH + slab:M_WIDTH + slab + 128] * QK_SCALE
            for par in range(2):
                h = 2 * j + par
                r = rows[b, h]
                ci_col = cols[b][:, 2 * h:2 * h + 1]
                a_row = r["b_row"] + r["m_prev"]
                dm = jnp.where(causal_t, r["b_row"] + ci_col, NEG_INF)
                m_row = jnp.maximum(a_row, jnp.max(dm, axis=0, keepdims=True))
                k_own = jnp.where(lane_half == par, k2, 0.0).astype(BF16)
                s_t = _dot_nt(k_own, q2) * jnp.exp(dm - m_row)
                st_scr[b, j, :, L * par:L * (par + 1)] = s_t.astype(BF16)
                stats[b, h] = dict(w_inter=jnp.exp(a_row - m_row), floor=jnp.exp(-m_row),
                                   den_s=jnp.sum(s_t, axis=0, keepdims=True),
                                   decay=r["decay"], m_new=r["m_new"])
            wk = jnp.where(lane_half == 0, cols[b][:, 4 * j + 1:4 * j + 2], cols[b][:, 4 * j + 3:4 * j + 4])
            kw = k2 * wk
            kw_scr[b, j] = kw.astype(BF16)
            stats[b, j, "k_sum"] = jnp.sum(kw, axis=0, keepdims=True)

    block_diag = (lax.broadcasted_iota(jnp.int32, (128, 128), 0) // D
                  == lax.broadcasted_iota(jnp.int32, (128, 128), 1) // D)
    for b in range(NB):
        for j in range(M_HEADS // 2):
            slab = 128 * j
            q2 = z_ref[b, :, slab:slab + 128].astype(BF16)
            v2 = z_ref[b, :, 2 * M_WIDTH + slab:2 * M_WIDTH + slab + 128].astype(BF16)
            cp = cp_scr[b, j]
            npair = np_scr[b, j]
            qc = _dot_nt(cp.astype(BF16), q2)
            qn = _dot_nt(npair.astype(BF16), q2)
            pv = _dot_tn(v2, st_scr[b, j])
            upd = jnp.where(block_diag, _dot_tn(v2, kw_scr[b, j]), 0.0)
            n_new = jnp.zeros((SUBLANES, 128), F32)
            for par in range(2):
                h = 2 * j + par
                st = stats[b, h]
                hr = slice(D * par, D * (par + 1))
                num = st["w_inter"] * qc[hr, :] + pv[hr, L * par:L * (par + 1)]
                den = st["w_inter"] * qn[par:par + 1, :] + st["den_s"]
                hh = num * (1.0 / jnp.maximum(jnp.abs(den), st["floor"]))
                hh = hh * lax.rsqrt(jnp.mean(hh * hh, axis=0, keepdims=True) + EPS)
                ht_scr[b, D * h:D * (h + 1), :] = hh
                cp_scr[b, j, hr, :] = st["decay"] * cp[hr, :] + upd[hr, :]
                k_sum = jnp.where(lane_half[0:1, :] == par, stats[b, j, "k_sum"], 0.0)
                n_new = jnp.where(row8 == par, st["decay"] * npair[par:par + 1, :] + k_sum, n_new)
                m_scr[b, h] = jnp.broadcast_to(st["m_new"], m_scr.shape[2:])
            np_scr[b, j] = n_new
        o_gate = _sigmoid(z_ref[b, :, 3 * M_WIDTH:4 * M_WIDTH])
        h_ref[b] = ht_scr[b].T * ng_ref[...] * o_gate

    cp_ref[...] = cp_scr[...]
    np_ref[...] = np_scr[...]
    m_ref[...] = m_scr[...]


def _mlstm_prompt(l, z3, gbias, ng):
    b, s, _ = z3.shape
    L = M_CHUNK
    P = M_HEADS // 2
    return pl.pallas_call(
        _mlstm_prompt_kernel,
        grid=(s // L,),
        in_specs=[
            pl.BlockSpec((b, L, 4 * M_WIDTH), lambda c: (0, c, Z_M // (4 * M_WIDTH))),
            pl.BlockSpec((b, L, 128), lambda c: (0, c, Z_G // 128)),
            _weight_spec(gbias, l),
            _weight_spec(ng, l),
        ],
        out_specs=[
            pl.BlockSpec((b, L, M_WIDTH), lambda c: (0, c, 0)),
            pl.BlockSpec((b, P, 128, 128), lambda c: (0, 0, 0, 0)),
            pl.BlockSpec((b, P, SUBLANES, 128), lambda c: (0, 0, 0, 0)),
            pl.BlockSpec((b, M_HEADS, SUBLANES, 128), lambda c: (0, 0, 0, 0)),
        ],
        out_shape=[
            jax.ShapeDtypeStruct((b, s, M_WIDTH), F32),
            jax.ShapeDtypeStruct((b, P, 128, 128), F32),
            jax.ShapeDtypeStruct((b, P, SUBLANES, 128), F32),
            jax.ShapeDtypeStruct((b, M_HEADS, SUBLANES, 128), F32),
        ],
        scratch_shapes=[
            pltpu.VMEM((b, P, 128, 128), F32),
            pltpu.VMEM((b, P, SUBLANES, 128), F32),
            pltpu.VMEM((b, M_HEADS, SUBLANES, 128), F32),
            pltpu.VMEM((b, M_WIDTH, L), F32),
            pltpu.VMEM((b, P, L, 2 * L), BF16),
            pltpu.VMEM((b, P, L, 128), BF16),
        ],
        compiler_params=_params("arbitrary"),
        name="mlstm_prompt",
    )(z3, z3, gbias, ng)


CONV_PAD = 32


def _conv_prompt_kernel(z_ref, w_ref, cb_ref, lg_ref, lb_ref, h_ref, tail_ref, buf, shifted):
    tc = z_ref.shape[0]
    t = pl.program_id(1)

    @pl.when(t == 0)
    def _():
        buf[0:CONV_PAD, :] = jnp.zeros((CONV_PAD, C_WIDTH), F32)

    @pl.when(t > 0)
    def _():
        buf[0:CONV_PAD, :] = buf[tc:tc + CONV_PAD, :]

    u = z_ref[:, 0:C_WIDTH] * _sigmoid(z_ref[:, C_WIDTH:2 * C_WIDTH])
    buf[CONV_PAD:CONV_PAD + tc, :] = u
    n_sh = shifted.shape[1]
    for r in range(1, SUBLANES):
        shifted[r - 1] = buf[r:r + n_sh, :]
    off = CONV_PAD - (C_KERNEL - 1)
    acc = jnp.broadcast_to(cb_ref[...], (tc, C_WIDTH))
    for j in range(C_KERNEL):
        r = (off + j) % SUBLANES
        base = off + j - r
        win = buf[base:base + tc, :] if r == 0 else shifted[r - 1, base:base + tc, :]
        acc = acc + w_ref[j:j + 1, :] * win
    mu = jnp.mean(acc, axis=-1, keepdims=True)
    xc = acc - mu
    y = xc * lax.rsqrt(jnp.mean(xc * xc, axis=-1, keepdims=True) + EPS) * lg_ref[...] + lb_ref[...]
    h_ref[...] = _swish(y)
    tail_ref[...] = buf[tc:tc + CONV_PAD, :]


def _conv_prompt(l, z3, w, cb, lg, lb, *, tc):
    b, s, _ = z3.shape
    return pl.pallas_call(
        _conv_prompt_kernel,
        grid=(b, s // tc),
        in_specs=[
            pl.BlockSpec((None, tc, 2 * C_WIDTH), lambda i, t: (i, t, Z_C // (2 * C_WIDTH))),
            _weight_spec(w, l), _weight_spec(cb, l), _weight_spec(lg, l), _weight_spec(lb, l),
        ],
        out_specs=[
            pl.BlockSpec((None, tc, C_WIDTH), lambda i, t: (i, t, 0)),
            pl.BlockSpec((None, CONV_PAD, C_WIDTH), lambda i, t: (i, 0, 0)),
        ],
        out_shape=[
            jax.ShapeDtypeStruct((b, s, C_WIDTH), F32),
            jax.ShapeDtypeStruct((b, CONV_PAD, C_WIDTH), F32),
        ],
        scratch_shapes=[pltpu.VMEM((CONV_PAD + tc, C_WIDTH), F32),
                        pltpu.VMEM((SUBLANES - 1, CONV_PAD + tc - SUBLANES, C_WIDTH), F32)],
        compiler_params=_params("arbitrary", "arbitrary"),
        name="conv_prompt",
    )(z3, w, cb, lg, lb)


def _swa_prompt_kernel(rb_ref, sink_ref, bucket_ref, q_ref, kv_ref, kvp_ref, o_ref, kt_ref, vt_ref,
                       bias_scr, s_scr, p_scr, ot_scr, *, layer):
    W = WINDOW
    H = S_Q_HEADS
    G = H // S_KV_HEADS
    i = pl.program_id(0)
    n = pl.program_id(1)

    @pl.when(jnp.logical_and(i == 0, n == 0))
    def _():
        bucket = bucket_ref[...]
        prev_key = lax.broadcasted_iota(jnp.int32, (2 * W, W), 0) < W
        for h in range(H):
            acc = jnp.full((2 * W, W), NEG_INF, F32)
            for b in range(N_BUCKETS):
                acc = jnp.where(bucket == b, rb_ref[b, h], acc)
            acc = acc * LOG2E
            bias_scr[0, h] = acc
            bias_scr[1, h] = jnp.where(prev_key, NEG_INF, acc)

    QB = q_ref.shape[0] // W
    NK = (QB + 1) * W
    first = jnp.where(n == 0, 1, 0)
    kk = jnp.concatenate([kvp_ref[:, 0:S_KV_WIDTH], kv_ref[:, 0:S_KV_WIDTH]], axis=0) * (QK_SCALE * LOG2E)
    vv = jnp.concatenate([kvp_ref[:, S_KV_WIDTH:2 * S_KV_WIDTH], kv_ref[:, S_KV_WIDTH:2 * S_KV_WIDTH]], axis=0)
    kk_r = pltpu.roll(kk, HEAD_DIM, axis=1)
    lo_lane = lax.broadcasted_iota(jnp.int32, (NK, S_KV_WIDTH), 1) < HEAD_DIM
    k_var = [[jnp.where(lo_lane, kk, 0.0).astype(BF16), jnp.where(lo_lane, 0.0, kk_r).astype(BF16)],
             [jnp.where(lo_lane, kk_r, 0.0).astype(BF16), jnp.where(lo_lane, 0.0, kk).astype(BF16)]]
    v_t = vv.T.astype(BF16)

    for j in range(QB):
        k0 = j * W
        masked = first if j == 0 else 0
        m_rows = [None] * H
        for hk in range(S_KV_HEADS):
            c0 = 2 * 128 * hk
            q_st = jnp.concatenate([q_ref[k0:k0 + W, c0:c0 + 128], q_ref[k0:k0 + W, c0 + 128:c0 + 256]],
                                   axis=0).astype(BF16)
            for half in range(2):
                s_t = _dot_nt(k_var[hk][half][k0:k0 + 2 * W, :], q_st)
                for slab in range(2):
                    head = G * hk + 2 * slab + half
                    sb = s_t[:, 128 * slab:128 * (slab + 1)] + bias_scr[masked, head]
                    s_scr[j, head] = sb
                    m_rows[head] = jnp.maximum(jnp.max(sb, axis=0, keepdims=True), sink_ref[layer, head] * LOG2E)

        inv = [None] * H
        for head in range(H):
            e = jnp.exp2(s_scr[j, head] - m_rows[head])
            den = jnp.sum(e, axis=0, keepdims=True) + jnp.exp2(sink_ref[layer, head] * LOG2E - m_rows[head])
            inv[head] = 1.0 / den
            p_scr[j, :, 128 * head:128 * (head + 1)] = e.astype(BF16)

        for hk in range(S_KV_HEADS):
            o_t = _dot(v_t[HEAD_DIM * hk:HEAD_DIM * (hk + 1), k0:k0 + 2 * W],
                       p_scr[j, :, 128 * G * hk:128 * G * (hk + 1)])
            for g in range(G):
                head = G * hk + g
                ot_scr[j, HEAD_DIM * head:HEAD_DIM * (head + 1), :] = o_t[:, 128 * g:128 * (g + 1)] * inv[head]
        o_ref[k0:k0 + W, :] = ot_scr[j].T

    @pl.when(n == pl.num_programs(1) - 1)
    def _():
        kt_ref[...] = kv_ref[(QB - 1) * W:QB * W, 0:S_KV_WIDTH].T
        vt_ref[...] = kv_ref[(QB - 1) * W:QB * W, S_KV_WIDTH:2 * S_KV_WIDTH].T


def _swa_prompt(l, z3, rel_bias, sinks, bucket, *, qb):
    b, s, _ = z3.shape
    W = WINDOW
    smem = pl.BlockSpec(memory_space=pltpu.SMEM)
    return pl.pallas_call(
        functools.partial(_swa_prompt_kernel, layer=l),
        grid=(b, s // (qb * W)),
        in_specs=[
            smem, smem, _const_spec((2 * W, W)),
            pl.BlockSpec((None, qb * W, S_WIDTH), lambda i, n: (i, n, Z_SQ // S_WIDTH)),
            pl.BlockSpec((None, qb * W, 2 * S_KV_WIDTH), lambda i, n: (i, n, Z_SKV // (2 * S_KV_WIDTH))),
            pl.BlockSpec((None, W, 2 * S_KV_WIDTH),
                         lambda i, n: (i, jnp.maximum(n * qb - 1, 0), Z_SKV // (2 * S_KV_WIDTH))),
        ],
        out_specs=[
            pl.BlockSpec((None, qb * W, S_WIDTH), lambda i, n: (i, n, 0)),
            pl.BlockSpec((None, S_KV_WIDTH, W), lambda i, n: (i, 0, 0)),
            pl.BlockSpec((None, S_KV_WIDTH, W), lambda i, n: (i, 0, 0)),
        ],
        out_shape=[
            jax.ShapeDtypeStruct((b, s, S_WIDTH), F32),
            jax.ShapeDtypeStruct((b, S_KV_WIDTH, W), F32),
            jax.ShapeDtypeStruct((b, S_KV_WIDTH, W), F32),
        ],
        scratch_shapes=[
            pltpu.VMEM((2, S_Q_HEADS, 2 * W, W), F32),
            pltpu.VMEM((qb, S_Q_HEADS, 2 * W, W), F32),
            pltpu.VMEM((qb, 2 * W, S_Q_HEADS * W), BF16),
            pltpu.VMEM((qb, S_WIDTH, W), F32),
        ],
        compiler_params=_params("arbitrary", "arbitrary"),
        name="swa_prompt",
    )(rel_bias, sinks, bucket, z3, z3, z3)


MIX_SUB = 256


def _mix_out_prompt_kernel(x_ref, hm_ref, hc_ref, hs_ref, wout_ref, g2_ref, wq_ref, mk_ref, mv_ref, wo_ref, o_ref,
                           x1_scr, qx_scr, p_scr):
    tm = x_ref.shape[0]
    mem = mk_ref.shape[1]
    mk_t = mk_ref[...]
    mv_t = mv_ref[...]
    row_head = lax.broadcasted_iota(jnp.int32, mk_t.shape, 0) // HEAD_DIM
    k_heads = [jnp.where(row_head == h, mk_t, 0.0).astype(BF16) for h in range(X_HEADS)]
    v_cat = jnp.concatenate([jnp.where(row_head == h, mv_t, 0.0).astype(BF16) for h in range(X_HEADS)], axis=1)
    subs = [slice(r0, r0 + MIX_SUB) for r0 in range(0, tm, MIX_SUB)]
    for rows in subs:
        cat = jnp.concatenate([hm_ref[rows, :].astype(BF16), hc_ref[rows, :].astype(BF16),
                               hs_ref[rows, :].astype(BF16)], axis=1)
        x1 = x_ref[rows, :] + _dot(cat, wout_ref[...])
        x1_scr[rows, :] = x1
        qx = _dot(_rms(x1, g2_ref[...]).astype(BF16), wq_ref[...])
        qx_scr[rows, :] = (qx * (QK_SCALE * LOG2E)).astype(BF16)
    for rows in subs:
        qx = qx_scr[rows, :]
        for h in range(X_HEADS):
            s = _dot(qx, k_heads[h])
            e = jnp.exp2(s - jnp.max(s, axis=1, keepdims=True))
            p_scr[rows, mem * h:mem * (h + 1)] = (e * (1.0 / jnp.sum(e, axis=1, keepdims=True))).astype(BF16)
    for rows in subs:
        o = _dot_nt(p_scr[rows, :], v_cat)
        o_ref[rows, :] = x1_scr[rows, :] + _dot(o.astype(BF16), wo_ref[...])


def _mix_out_prompt(l, x3, hm, hc, hs, wout, g2, wq, mkv, wo, *, tm):
    b, s, d = x3.shape
    tm = min(tm, s)
    mem = mkv.shape[2]
    row = lambda w: pl.BlockSpec((None, tm, w), lambda i, t: (i, t, 0))
    return pl.pallas_call(
        _mix_out_prompt_kernel,
        grid=(b, s // tm),
        in_specs=[
            row(d), row(M_WIDTH), row(C_WIDTH), row(S_WIDTH),
            _weight_spec(wout, l), _weight_spec(g2, l), _weight_spec(wq, l),
            pl.BlockSpec((None, X_WIDTH, mem), lambda i, t: (i, 0, 0)),
            pl.BlockSpec((None, X_WIDTH, mem), lambda i, t: (i, 1, 0)),
            _weight_spec(wo, l),
        ],
        out_specs=row(d),
        out_shape=jax.ShapeDtypeStruct((b, s, d), F32),
        scratch_shapes=[pltpu.VMEM((tm, d), F32), pltpu.VMEM((tm, X_WIDTH), BF16),
                        pltpu.VMEM((tm, X_HEADS * mem), BF16)],
        compiler_params=_params("arbitrary", "arbitrary"),
        name="mix_out_prompt",
    )(x3, hm, hc, hs, wout, g2, wq, mkv, mkv, wo)


FFN_PAD = 8


def _ffn_prompt_kernel(x_ref, g3_ref, wup_ref, cw_ref, cb_ref, wdn_ref, gf_ref, o_ref, tail_ref,
                       carry, act, *, final):
    tm = x_ref.shape[0]
    dff = wdn_ref.shape[0]

    @pl.when(pl.program_id(1) == 0)
    def _():
        carry[...] = jnp.zeros(carry.shape, F32)

    x = x_ref[...]
    h = _rms(x, g3_ref[...]).astype(BF16)
    w = FFN_CHUNK
    row8 = lax.broadcasted_iota(jnp.int32, (FFN_PAD, w), 0)
    for c in range(0, dff, w):
        a = _dot(h, wup_ref[:, c:c + w])
        g = _dot(h, wup_ref[:, dff + c:dff + c + w])
        prev = carry[:, c:c + w]
        carry[:, c:c + w] = g[tm - FFN_PAD:tm, :]
        taps = []
        for s in (2, 1):
            gs = pltpu.roll(g, s, axis=0)
            head = jnp.where(row8 < s, pltpu.roll(prev, s, axis=0), gs[0:FFN_PAD, :])
            taps.append(jnp.concatenate([head, gs[FFN_PAD:, :]], axis=0))
        gc = (cw_ref[0:1, c:c + w] * taps[0] + cw_ref[1:2, c:c + w] * taps[1]
              + cw_ref[2:3, c:c + w] * g + cb_ref[:, c:c + w])
        act[:, c:c + w] = (_swish(gc) * a).astype(BF16)
    y = x + _dot(act[...], wdn_ref[...])
    o_ref[...] = _rms(y, gf_ref[...]) if final else y
    tail_ref[...] = carry[...]


def _ffn_prompt(l, x3, g3, wup, cw, cb, wdn, gf, *, tm, final):
    b, s, d = x3.shape
    tm = min(tm, s)
    dff = wdn.shape[1]
    return pl.pallas_call(
        functools.partial(_ffn_prompt_kernel, final=final),
        grid=(b, s // tm),
        in_specs=[
            pl.BlockSpec((None, tm, d), lambda i, t: (i, t, 0)),
            _weight_spec(g3, l), _weight_spec(wup, l), _weight_spec(cw, l), _weight_spec(cb, l),
            _weight_spec(wdn, l), _const_spec((1, d)),
        ],
        out_specs=[
            pl.BlockSpec((None, tm, d), lambda i, t: (i, t, 0)),
            pl.BlockSpec((None, FFN_PAD, dff), lambda i, t: (i, 0, 0)),
        ],
        out_shape=[jax.ShapeDtypeStruct((b, s, d), F32), jax.ShapeDtypeStruct((b, FFN_PAD, dff), F32)],
        scratch_shapes=[
            pltpu.VMEM((FFN_PAD, dff), F32),
            pltpu.VMEM((tm, dff), BF16),
        ],
        compiler_params=_params("arbitrary", "arbitrary"),
        name="ffn_prompt",
    )(x3, g3, wup, cw, cb, wdn, gf)


SWA_SAMPLE_BLOCK = 32
XATTN_SAMPLE_BLOCK = 16
CONV_SAMPLE_BLOCK = 32


def _mlstm_sample_kernel(gb_ref, q_ref, k_ref, v_ref, o_ref, g_ref, ng_ref, c_ref, n_ref, m_ref, *rest, layer):
    h_ref, co_ref, no_ref, mo_ref, kw_scr = rest[-5:]
    _copy_earlier_layers(rest[:-5], (co_ref, no_ref, mo_ref))
    h = pl.program_id(0)
    i_pre = g_ref[pl.ds(h, 1), :] + gb_ref[layer, h]
    f_pre = g_ref[pl.ds(M_HEADS + h, 1), :] + gb_ref[layer, M_HEADS + h]
    a = _log_sigmoid(f_pre) + m_ref[pl.ds(h, 1), :]
    m_t = jnp.maximum(a, i_pre)
    w_old = jnp.exp(a - m_t)
    w_new = jnp.exp(i_pre - m_t)
    q = q_ref[...]
    k = k_ref[...] * QK_SCALE
    v = v_ref[...]
    n_old = n_ref[...]
    kw_scr[...] = k * w_new

    def body(d, acc):
        c_old = c_ref[d]
        co_ref[layer, d] = w_old * c_old + kw_scr[pl.ds(d, 1), :] * v
        return acc + q_ref[pl.ds(d, 1), :] * c_old

    qc = lax.fori_loop(0, HEAD_DIM, body, jnp.zeros(v.shape, F32), unroll=8)
    s = jnp.sum(q * k, axis=0, keepdims=True) * w_new
    num = w_old * qc + s * v
    den = w_old * jnp.sum(q * n_old, axis=0, keepdims=True) + s
    hh = num / jnp.maximum(jnp.abs(den), jnp.exp(-m_t))
    hh = hh * lax.rsqrt(jnp.mean(hh * hh, axis=0, keepdims=True) + EPS) * ng_ref[...]
    h_ref[...] = hh * _sigmoid(o_ref[...])
    no_ref[layer] = w_old * n_old + kw_scr[...]
    mo_ref[layer, pl.ds(h, 1), :] = m_t


def _mlstm_sample(l, zt, gb, ngt, c_all, n_all, m_all, prev):
    bsz = zt.shape[1]
    D = HEAD_DIM
    feat = lambda off: pl.BlockSpec((D, bsz), lambda h: (off // D + h, 0))
    c_spec = lambda n: pl.BlockSpec((n, None, D, D, bsz), lambda h: (0, h, 0, 0, 0))
    n_spec = lambda n: pl.BlockSpec((n, None, D, bsz), lambda h: (0, h, 0, 0))
    m_spec = lambda n: pl.BlockSpec((n, M_HEADS, bsz), lambda h: (0, 0, 0))
    prev_specs = [c_spec(l), n_spec(l), m_spec(l)] if prev else []
    return pl.pallas_call(
        functools.partial(_mlstm_sample_kernel, layer=l),
        grid=(M_HEADS,),
        in_specs=[
            pl.BlockSpec(memory_space=pltpu.SMEM),
            feat(Z_M), feat(Z_M + M_WIDTH), feat(Z_M + 2 * M_WIDTH), feat(Z_M + 3 * M_WIDTH),
            pl.BlockSpec((SUBLANES, bsz), lambda h: (Z_G // SUBLANES, 0)),
            pl.BlockSpec((None, D, bsz), lambda h: (l, h, 0)),
            pl.BlockSpec((None, None, D, D, bsz), lambda h: (l, h, 0, 0, 0)),
            pl.BlockSpec((None, None, D, bsz), lambda h: (l, h, 0, 0)),
            pl.BlockSpec((None, M_HEADS, bsz), lambda h: (l, 0, 0)),
        ] + prev_specs,
        out_specs=[pl.BlockSpec((D, bsz), lambda h: (h, 0)), c_spec(l + 1), n_spec(l + 1), m_spec(l + 1)],
        out_shape=[
            jax.ShapeDtypeStruct((M_WIDTH, bsz), F32),
            jax.ShapeDtypeStruct((l + 1, M_HEADS, D, D, bsz), F32),
            jax.ShapeDtypeStruct((l + 1, M_HEADS, D, bsz), F32),
            jax.ShapeDtypeStruct((l + 1, M_HEADS, bsz), F32),
        ],
        scratch_shapes=[pltpu.VMEM((D, bsz), F32)],
        compiler_params=_params("arbitrary"),
        name="mlstm_sample",
    )(gb, zt, zt, zt, zt, zt, ngt, c_all, n_all, m_all, *prev)


def _conv_sample_kernel(z_ref, hist_ref, w_ref, cb_ref, lg_ref, lb_ref, *rest):
    h_ref, hist_out_ref = rest[-2:]
    _copy_earlier_layers(rest[:-2], (hist_out_ref,))
    l = hist_out_ref.shape[0] - 1
    nh = C_KERNEL - 1
    u = z_ref[:, 0:C_WIDTH] * _sigmoid(z_ref[:, C_WIDTH:2 * C_WIDTH])
    acc = cb_ref[...] + w_ref[nh:nh + 1, :] * u
    for j in range(nh):
        acc = acc + w_ref[j:j + 1, :] * hist_ref[j]
    mu = jnp.mean(acc, axis=-1, keepdims=True)
    xc = acc - mu
    y = xc * lax.rsqrt(jnp.mean(xc * xc, axis=-1, keepdims=True) + EPS) * lg_ref[...] + lb_ref[...]
    h_ref[...] = _swish(y)
    for j in range(nh - 1):
        hist_out_ref[l, j] = hist_ref[j + 1]
    hist_out_ref[l, nh - 1] = u


def _conv_sample(l, z, hist_all, w, cb, lg, lb, prev):
    bsz = z.shape[0]
    nh = hist_all.shape[1]
    R = min(CONV_SAMPLE_BLOCK, bsz)
    hist_spec = lambda n: pl.BlockSpec((n, nh, R, C_WIDTH), lambda i: (0, 0, i, 0))
    return pl.pallas_call(
        _conv_sample_kernel,
        grid=(bsz // R,),
        in_specs=[
            pl.BlockSpec((R, 2 * C_WIDTH), lambda i: (i, Z_C // (2 * C_WIDTH))),
            pl.BlockSpec((None, nh, R, C_WIDTH), lambda i: (l, 0, i, 0)),
            _weight_spec(w, l), _weight_spec(cb, l), _weight_spec(lg, l), _weight_spec(lb, l),
        ] + [hist_spec(l)] * len(prev),
        out_specs=[pl.BlockSpec((R, C_WIDTH), lambda i: (i, 0)), hist_spec(l + 1)],
        out_shape=[jax.ShapeDtypeStruct((bsz, C_WIDTH), F32), jax.ShapeDtypeStruct((l + 1, nh, bsz, C_WIDTH), F32)],
        compiler_params=_params("arbitrary"),
        name="conv_sample",
    )(z, hist_all, w, cb, lg, lb, *prev)


def _copy_earlier_layers(prev_refs, out_refs):
    for p_ref, o_ref in zip(prev_refs, out_refs):
        o_ref[0:p_ref.shape[0]] = p_ref[...]


def _swa_sample_kernel(q_ref, kvn_ref, kv2_ref, kc_ref, vc_ref, bias_ref, aux_ref, *rest):
    o_ref, ko_ref, vo_ref = rest[-3:]
    _copy_earlier_layers(rest[:-3], (ko_ref, vo_ref))
    l = ko_ref.shape[0] - 1
    R = q_ref.shape[0]
    W = kc_ref.shape[2]
    H = S_Q_HEADS
    shape = (R, H, 128)
    row = lax.broadcasted_iota(jnp.int32, shape, 1)
    lane_half = lax.broadcasted_iota(jnp.int32, shape, 2) // HEAD_DIM
    q_half = row % 2
    kv_head = row // (H // S_KV_HEADS)
    qs = jnp.zeros(shape, F32)
    for j in range(H // 2):
        qs = jnp.where(row // 2 == j, q_ref[:, :, 128 * j:128 * (j + 1)], qs)
    q8 = jnp.where(lane_half == kv_head, jnp.where(q_half == kv_head, qs, pltpu.roll(qs, HEAD_DIM, axis=2)), 0.0)
    k_new = kvn_ref[:, :, 0:S_KV_WIDTH]
    v_new = kvn_ref[:, :, S_KV_WIDTH:2 * S_KV_WIDTH]
    s = jnp.einsum("bqd,bdk->bqk", q8.astype(BF16), kc_ref[...].astype(BF16), preferred_element_type=F32) * QK_SCALE
    s = s + bias_ref[...][None]
    s_new = jnp.sum(q8 * k_new, axis=2, keepdims=True) * QK_SCALE + aux_ref[:, 0:1][None]
    sink = aux_ref[:, 1:2][None]
    m = jnp.maximum(jnp.maximum(jnp.max(s, axis=2, keepdims=True), s_new), sink)
    e = jnp.exp(s - m)
    e_new = jnp.exp(s_new - m)
    inv = 1.0 / (jnp.sum(e, axis=2, keepdims=True) + e_new + jnp.exp(sink - m))
    o8 = jnp.einsum("bqk,bdk->bqd", e.astype(BF16), vc_ref[...].astype(BF16), preferred_element_type=F32)
    o8 = (o8 + e_new * v_new) * inv
    o8 = jnp.where(lane_half == q_half, jnp.where(q_half == kv_head, o8, pltpu.roll(o8, HEAD_DIM, axis=2)), 0.0)
    for j in range(H // 2):
        o_ref[:, :, 128 * j:128 * (j + 1)] = jnp.sum(jnp.where(row // 2 == j, o8, 0.0), axis=1, keepdims=True)
    k_cols = kv2_ref[:, 0:S_KV_WIDTH].T
    v_cols = kv2_ref[:, S_KV_WIDTH:2 * S_KV_WIDTH].T
    last = lax.broadcasted_iota(jnp.int32, (S_KV_WIDTH, W), 1) == W - 1
    for r in range(R):
        ko_ref[l, r] = jnp.where(last, k_cols[:, r:r + 1], pltpu.roll(kc_ref[r], W - 1, axis=1))
        vo_ref[l, r] = jnp.where(last, v_cols[:, r:r + 1], pltpu.roll(vc_ref[r], W - 1, axis=1))


def _swa_sample(l, z, z3, kc_all, vc_all, bias, aux, prev):
    bsz = z.shape[0]
    R = min(SWA_SAMPLE_BLOCK, bsz)
    W = kc_all.shape[3]
    cache_in = pl.BlockSpec((None, R, S_KV_WIDTH, W), lambda i: (l, i, 0, 0))
    cache_prev = pl.BlockSpec((l, R, S_KV_WIDTH, W), lambda i: (0, i, 0, 0))
    cache_out = pl.BlockSpec((l + 1, R, S_KV_WIDTH, W), lambda i: (0, i, 0, 0))
    return pl.pallas_call(
        _swa_sample_kernel,
        grid=(bsz // R,),
        in_specs=[
            pl.BlockSpec((R, 1, S_WIDTH), lambda i: (i, 0, Z_SQ // S_WIDTH)),
            pl.BlockSpec((R, 1, 2 * S_KV_WIDTH), lambda i: (i, 0, Z_SKV // (2 * S_KV_WIDTH))),
            pl.BlockSpec((R, 2 * S_KV_WIDTH), lambda i: (i, Z_SKV // (2 * S_KV_WIDTH))),
            cache_in, cache_in,
            _const_spec((S_Q_HEADS, 128)), _weight_spec(aux, l),
        ] + [cache_prev] * len(prev),
        out_specs=[pl.BlockSpec((R, 1, S_WIDTH), lambda i: (i, 0, 0)), cache_out, cache_out],
        out_shape=[
            jax.ShapeDtypeStruct((bsz, 1, S_WIDTH), F32),
            jax.ShapeDtypeStruct((l + 1, bsz, S_KV_WIDTH, W), F32),
            jax.ShapeDtypeStruct((l + 1, bsz, S_KV_WIDTH, W), F32),
        ],
        compiler_params=_params("arbitrary"),
        name="swa_sample",
    )(z3, z3, z, kc_all, vc_all, bias, aux, *prev)


def _mix_out_sample_kernel(x_ref, hmt_ref, hc_ref, hs_ref, wout_ref, g2_ref, wq_ref, x1_ref, q_ref):
    cat = jnp.concatenate([hmt_ref[...].T.astype(BF16), hc_ref[...].astype(BF16), hs_ref[...].astype(BF16)], axis=1)
    x1 = x_ref[...] + _dot(cat, wout_ref[...])
    x1_ref[...] = x1
    q_ref[...] = _dot(_rms(x1, g2_ref[...]).astype(BF16), wq_ref[...])


def _mix_out_sample(l, x, hm, hc, hs, wout, g2, wq):
    bsz, d = x.shape
    full = lambda a: pl.BlockSpec(a.shape, lambda i: (0,) * a.ndim)
    args = (x, hm, hc, hs, wout, g2, wq)
    return pl.pallas_call(
        _mix_out_sample_kernel,
        grid=(1,),
        in_specs=[full(x), full(hm), full(hc), full(hs), _weight_spec(wout, l), _weight_spec(g2, l),
                  _weight_spec(wq, l)],
        out_specs=[pl.BlockSpec((bsz, d), lambda i: (0, 0)), pl.BlockSpec((bsz, X_WIDTH), lambda i: (0, 0))],
        out_shape=[jax.ShapeDtypeStruct((bsz, d), F32), jax.ShapeDtypeStruct((bsz, X_WIDTH), F32)],
        compiler_params=_params("arbitrary"),
        name="mix_out_sample",
    )(*args)


def _xattn_sample_kernel(q_ref, k_ref, v_ref, o_ref):
    R = k_ref.shape[0]
    shape = (R, SUBLANES, X_WIDTH)
    row = lax.broadcasted_iota(jnp.int32, shape, 1)
    lane_head = lax.broadcasted_iota(jnp.int32, shape, 2) // HEAD_DIM
    own = row == lane_head
    q8 = jnp.where(own, jnp.broadcast_to(q_ref[...], shape), 0.0).astype(BF16)
    s = jnp.einsum("bqd,bdk->bqk", q8, k_ref[...].astype(BF16), preferred_element_type=F32) * QK_SCALE
    e = jnp.exp(s - jnp.max(s, axis=2, keepdims=True))
    p = (e / jnp.sum(e, axis=2, keepdims=True)).astype(BF16)
    o8 = jnp.einsum("bqk,bdk->bqd", p, v_ref[...].astype(BF16), preferred_element_type=F32)
    o_ref[...] = jnp.sum(jnp.where(own, o8, 0.0), axis=1, keepdims=True)


def _xattn_sample(l, q3, k_all, v_all):
    _, bsz, w, mem = k_all.shape
    R = min(XATTN_SAMPLE_BLOCK, bsz)
    kv = pl.BlockSpec((None, R, w, mem), lambda i: (l, i, 0, 0))
    qo = pl.BlockSpec((R, 1, w), lambda i: (i, 0, 0))
    return pl.pallas_call(
        _xattn_sample_kernel,
        grid=(bsz // R,),
        in_specs=[qo, kv, kv],
        out_specs=qo,
        out_shape=jax.ShapeDtypeStruct((bsz, 1, w), F32),
        compiler_params=_params("arbitrary"),
        name="xattn_sample",
    )(q3, k_all, v_all)


def _ffn_sample_kernel(x1_ref, ox_ref, wo_ref, g3_ref, wup_ref, cw_ref, cb_ref, wdn_ref, gf_ref, hist_ref,
                       *rest, final):
    o_ref, hist_out_ref, act = rest[-3:]
    _copy_earlier_layers(rest[:-3], (hist_out_ref,))
    l = hist_out_ref.shape[0] - 1
    dff = wdn_ref.shape[0]
    x = x1_ref[...] + _dot(ox_ref[...].astype(BF16), wo_ref[...])
    h = _rms(x, g3_ref[...]).astype(BF16)
    for c in range(0, dff, FFN_CHUNK):
        a = _dot(h, wup_ref[:, c:c + FFN_CHUNK])
        g = _dot(h, wup_ref[:, dff + c:dff + c + FFN_CHUNK])
        h1 = hist_ref[:, 1, c:c + FFN_CHUNK]
        gc = (cw_ref[0:1, c:c + FFN_CHUNK] * hist_ref[:, 0, c:c + FFN_CHUNK]
              + cw_ref[1:2, c:c + FFN_CHUNK] * h1
              + cw_ref[2:3, c:c + FFN_CHUNK] * g + cb_ref[:, c:c + FFN_CHUNK])
        act[:, c:c + FFN_CHUNK] = (_swish(gc) * a).astype(BF16)
        hist_out_ref[l, :, 0, c:c + FFN_CHUNK] = h1
        hist_out_ref[l, :, 1, c:c + FFN_CHUNK] = g
    y = x + _dot(act[...], wdn_ref[...])
    o_ref[...] = _rms(y, gf_ref[...]) if final else y


def _ffn_sample(l, x1, ox, wo, g3, wup, cw, cb, wdn, gf, hist_all, prev, *, final):
    bsz, d = x1.shape
    dff = wdn.shape[1]
    full = lambda a: pl.BlockSpec(a.shape, lambda i: (0,) * a.ndim)
    hshape = hist_all.shape[1:]
    return pl.pallas_call(
        functools.partial(_ffn_sample_kernel, final=final),
        grid=(1,),
        in_specs=[full(x1), full(ox), _weight_spec(wo, l), _weight_spec(g3, l), _weight_spec(wup, l),
                  _weight_spec(cw, l), _weight_spec(cb, l),
                  _weight_spec(wdn, l), full(gf), pl.BlockSpec((None,) + hshape, lambda i: (l, 0, 0, 0))]
                 + [full(p) for p in prev],
        out_specs=[pl.BlockSpec((bsz, d), lambda i: (0, 0)), pl.BlockSpec((l + 1,) + hshape, lambda i: (0, 0, 0, 0))],
        out_shape=[jax.ShapeDtypeStruct((bsz, d), F32), jax.ShapeDtypeStruct((l + 1,) + hshape, F32)],
        scratch_shapes=[pltpu.VMEM((bsz, dff), BF16)],
        compiler_params=_params("arbitrary"),
        name="ffn_sample",
    )(x1, ox, wo, g3, wup, cw, cb, wdn, gf, hist_all, *prev)


def _t5_buckets(dist):
    n = np.maximum(dist, 0)
    max_exact = N_BUCKETS // 2
    nf = np.maximum(n, max_exact).astype(np.float32)
    large = max_exact + (np.log(nf / np.float32(max_exact)) / np.float32(math.log(MAX_DISTANCE / max_exact))
                         * np.float32(N_BUCKETS - max_exact)).astype(np.int32)
    return np.where(n < max_exact, n, np.minimum(large, N_BUCKETS - 1))


def _prompt_buckets():
    W = WINDOW
    dist = np.arange(W)[None, :] + W - np.arange(2 * W)[:, None]
    band = (dist >= 0) & (dist < W)
    return np.where(band, _t5_buckets(dist), -1).astype(np.int32)


def _swa_tables(rel_bias):
    W = WINDOW
    dist_c = W - np.arange(W)
    tab = jnp.transpose(rel_bias[_t5_buckets(dist_c)], (1, 0))
    cache_bias = jnp.where((dist_c < W)[None], tab, NEG_INF)
    return cache_bias, rel_bias[0]


def kernel(x_prompt, x_sample, mem_prompt, state_mlstm_C, state_mlstm_n, state_mlstm_m, state_conv, cache_swa_k, cache_swa_v, cache_mem_k, cache_mem_v, state_ffn_conv, rel_bias, norm1_g, w_in, b_i, b_f, mlstm_norm_g, conv_w, conv_b, conv_ln_g, conv_ln_b, swa_sinks, w_out, norm2_g, w_xq, w_xk, w_xv, w_xo, norm3_g, w_up, ffn_conv_w, ffn_conv_b, w_down, final_norm_g):
    depth = w_in.shape[0]
    bp, seq, d = x_prompt.shape
    bs = x_sample.shape[0]
    mem = mem_prompt.shape[1]
    dff = w_down.shape[1]
    W = WINDOW
    nh = C_KERNEL - 1

    xp = x_prompt
    xs = x_sample.reshape(bs, d)
    gf = final_norm_g.reshape(1, d)
    c_all = jnp.transpose(state_mlstm_C, (0, 2, 3, 4, 1))
    n_all = jnp.transpose(state_mlstm_n, (0, 2, 3, 1))
    m_all = jnp.transpose(state_mlstm_m, (0, 2, 1))
    hist_all = jnp.transpose(state_conv, (0, 2, 1, 3))
    kc_all = jnp.transpose(cache_swa_k, (0, 1, 3, 4, 2)).reshape(depth, bs, S_KV_WIDTH, W)
    vc_all = jnp.transpose(cache_swa_v, (0, 1, 3, 4, 2)).reshape(depth, bs, S_KV_WIDTH, W)
    mk_all = jnp.transpose(cache_mem_k, (0, 1, 3, 4, 2)).reshape(depth, bs, X_WIDTH, mem)
    mv_all = jnp.transpose(cache_mem_v, (0, 1, 3, 4, 2)).reshape(depth, bs, X_WIDTH, mem)
    pm_c, pm_n, pm_m, p_conv, p_k, p_v, p_mk, p_mv, p_ffn = ([] for _ in range(9))
    s_mlstm, s_conv, s_kv, s_ffn = [], [], [], []

    win_t = jnp.swapaxes(w_in, 1, 2)
    wout = w_out.astype(BF16)
    wxq = w_xq.astype(BF16)
    wxkv = jnp.concatenate([w_xk, w_xv], axis=2).astype(BF16)
    wxo = w_xo.astype(BF16)
    wup = w_up.astype(BF16)
    wdn = w_down.astype(BF16)
    bucket = jnp.asarray(_prompt_buckets())

    rows = lambda a: a.reshape(depth, 1, -1)
    g1, g2, g3 = rows(norm1_g), rows(norm2_g), rows(norm3_g)
    gb8 = jnp.concatenate([b_i, b_f], axis=1)
    gbias = rows(jnp.concatenate([gb8, jnp.zeros((depth, 128 - 2 * M_HEADS), F32)], axis=1))
    ng = rows(mlstm_norm_g)
    ngt = jnp.broadcast_to(mlstm_norm_g[:, :, None], (depth, M_WIDTH, bs))
    cw = jnp.concatenate([conv_w, jnp.zeros((depth, CONV_PAD - C_KERNEL, C_WIDTH), F32)], axis=1)
    cb, lg, lb = rows(conv_b), rows(conv_ln_g), rows(conv_ln_b)
    fcw = jnp.concatenate([ffn_conv_w, jnp.zeros((depth, SUBLANES - FFN_KERNEL, dff), F32)], axis=1)
    fcb = rows(ffn_conv_b)
    cache_bias, bias0 = _swa_tables(rel_bias)
    aux = jnp.concatenate([jnp.broadcast_to(bias0[None, :, None], (depth, S_Q_HEADS, 1)), swa_sinks[:, :, None],
                           jnp.zeros((depth, S_Q_HEADS, 126), F32)], axis=2)

    for l in range(depth):
        last = l == depth - 1

        mkv = _mem_kv(l, mem_prompt, wxkv)
        z, win = _in_proj(l, xp.reshape(bp * seq, d), g1, win_t, tm=1024)
        z = z.reshape(bp, seq, Z_WIDTH)
        hm, cpair, npair, mm = _mlstm_prompt(l, z, gbias, ng)
        hc, ctail = _conv_prompt(l, z, cw, cb, lg, lb, tc=min(1024, seq))
        hs, kt, vt = _swa_prompt(l, z, rel_bias, swa_sinks, bucket, qb=min(8, seq // W))
        xp = _mix_out_prompt(l, xp, hm, hc, hs, wout, g2, wxq, mkv, wxo, tm=1024)
        xp, ftail = _ffn_prompt(l, xp, g3, wup, fcw, fcb, wdn, gf, tm=512, final=last)
        half = lambda h: slice(HEAD_DIM * (h % 2), HEAD_DIM * (h % 2 + 1))
        pm_c.append(jnp.stack([jnp.swapaxes(cpair[:, h // 2, half(h), half(h)], 1, 2) for h in range(M_HEADS)], axis=1))
        pm_n.append(jnp.stack([npair[:, h // 2, h % 2, half(h)] for h in range(M_HEADS)], axis=1))
        pm_m.append(mm[:, :, 0, 0])
        p_conv.append(ctail[:, CONV_PAD - nh:, :])
        p_k.append(jnp.transpose(kt.reshape(bp, S_KV_HEADS, HEAD_DIM, W), (0, 3, 1, 2)))
        p_v.append(jnp.transpose(vt.reshape(bp, S_KV_HEADS, HEAD_DIM, W), (0, 3, 1, 2)))
        p_mk.append(jnp.transpose(mkv[:, 0:X_WIDTH, :].reshape(bp, X_HEADS, HEAD_DIM, mem), (0, 3, 1, 2)))
        p_mv.append(jnp.transpose(mkv[:, X_WIDTH:, :].reshape(bp, X_HEADS, HEAD_DIM, mem), (0, 3, 1, 2)))
        p_ffn.append(ftail[:, FFN_PAD - (FFN_KERNEL - 1):, :])

        zs, zst = _sample_in(l, xs, g1, win)
        hmt_s, *s_mlstm = _mlstm_sample(l, zst, gb8, ngt, c_all, n_all, m_all, s_mlstm)
        hc_s, *s_conv = _conv_sample(l, zs, hist_all, cw, cb, lg, lb, s_conv)
        hs_s, *s_kv = _swa_sample(l, zs, zs.reshape(bs, 1, Z_WIDTH), kc_all, vc_all, cache_bias, aux, s_kv)
        x1, qx = _mix_out_sample(l, xs, hmt_s, hc_s, hs_s.reshape(bs, S_WIDTH), wout, g2, wxq)
        ox = _xattn_sample(l, qx.reshape(bs, 1, X_WIDTH), mk_all, mv_all)
        xs, *s_ffn = _ffn_sample(l, x1, ox.reshape(bs, X_WIDTH), wxo, g3, wup, fcw, fcb, wdn, gf,
                                 state_ffn_conv, s_ffn, final=last)

    st = jnp.stack
    tr = jnp.transpose
    s_c, s_n, s_m = s_mlstm
    s_k, s_v = (a.reshape(depth, bs, S_KV_HEADS, HEAD_DIM, W) for a in s_kv)
    return (xp, xs.reshape(bs, 1, d),
            st(pm_c), st(pm_n), st(pm_m), st(p_conv), st(p_k), st(p_v), st(p_mk), st(p_mv), st(p_ffn),
            tr(s_c, (0, 4, 1, 2, 3)), tr(s_n, (0, 3, 1, 2)), tr(s_m, (0, 2, 1)),
            tr(s_conv[0], (0, 2, 1, 3)), tr(s_k, (0, 1, 4, 2, 3)), tr(s_v, (0, 1, 4, 2, 3)),
            s_ffn[0])
```

```python
import functools
import math

import numpy as np
import jax
import jax.numpy as jnp
from jax import lax
from jax.experimental import pallas as pl
from jax.experimental.pallas import tpu as pltpu

F32 = jnp.float32
BF16 = jnp.bfloat16
EPS = 1e-6
NEG_INF = float("-inf")

HEAD_DIM = 64
M_HEADS = 4
M_WIDTH = M_HEADS * HEAD_DIM
C_WIDTH = 256
C_KERNEL = 31
S_Q_HEADS = 8
S_KV_HEADS = 2
S_WIDTH = S_Q_HEADS * HEAD_DIM
S_KV_WIDTH = S_KV_HEADS * HEAD_DIM
WINDOW = 128
N_BUCKETS = 32
MAX_DISTANCE = 128
X_HEADS = 4
X_WIDTH = X_HEADS * HEAD_DIM
FFN_KERNEL = 3
QK_SCALE = HEAD_DIM ** -0.5
LOG2E = math.log2(math.e)

Z_M = 0
Z_C = 4 * M_WIDTH
Z_SQ = Z_C + 2 * C_WIDTH
Z_SKV = Z_SQ + S_WIDTH
Z_G = Z_SKV + 2 * S_KV_WIDTH
LANES = 128
SUBLANES = 8
Z_WIDTH = Z_G + LANES
VMEM_LIMIT = 56 * 1024 * 1024

M_CHUNK = 128
FFN_CHUNK = 256


def _params(*sem):
    return pltpu.CompilerParams(dimension_semantics=sem, vmem_limit_bytes=VMEM_LIMIT)


def _const_spec(shape):
    nd = len(shape)
    return pl.BlockSpec(shape, lambda *_: (0,) * nd)


def _weight_spec(w, l, single=False):
    nd = w.ndim - 1
    mode = dict(pipeline_mode=pl.Buffered(1)) if single else {}
    return pl.BlockSpec((None,) + w.shape[1:], lambda *_: (l,) + (0,) * nd, **mode)


def _rms(x, g):
    return x * lax.rsqrt(jnp.mean(x * x, axis=-1, keepdims=True) + EPS) * g


def _sigmoid(x):
    return 0.5 * jnp.tanh(0.5 * x) + 0.5


def _swish(x):
    h = 0.5 * x
    return h + h * jnp.tanh(h)


def _log_sigmoid(x):
    return jnp.minimum(x, 0.0) - jnp.log1p(jnp.exp(-jnp.abs(x)))


def _dot(a, b):
    return jnp.dot(a, b, preferred_element_type=F32)


def _dot_nt(a, b):
    return lax.dot_general(a, b, (((1,), (1,)), ((), ())), preferred_element_type=F32)


def _dot_tn(a, b):
    return lax.dot_general(a, b, (((0,), (0,)), ((), ())), preferred_element_type=F32)


IN_PROJ_ROWS = 512
IN_PROJ_COLS = 512
IN_GATES = 4 * M_WIDTH
W_PREP_ROWS = 256


def _in_proj_kernel(x_ref, g_ref, wt_ref, o_ref, wprep_ref, w_scr):
    d = x_ref.shape[1]

    @pl.when(pl.program_id(0) == 0)
    def _():
        n_gate = 2 * M_HEADS
        for src, dst, n in ((0, 0, IN_GATES), (IN_GATES + n_gate, IN_GATES, Z_G - IN_GATES)):
            for c in range(0, n, W_PREP_ROWS):
                w_scr[:, dst + c:dst + c + W_PREP_ROWS] = wt_ref[src + c:src + c + W_PREP_ROWS, :].T.astype(BF16)
        gate_rows = jnp.concatenate([wt_ref[IN_GATES:IN_GATES + n_gate, :], jnp.zeros((LANES - n_gate, d), F32)],
                                    axis=0)
        w_scr[:, Z_G:Z_G + LANES] = gate_rows.T.astype(BF16)
        wprep_ref[...] = w_scr[...]

    tm, n = o_ref.shape
    sub = min(IN_PROJ_ROWS, tm)
    for r0 in range(0, tm, sub):
        h = _rms(x_ref[r0:r0 + sub, :], g_ref[...]).astype(BF16)
        for c in range(0, n, IN_PROJ_COLS):
            w = min(IN_PROJ_COLS, n - c)
            o_ref[r0:r0 + sub, c:c + w] = _dot(h, w_scr[:, c:c + w])


def _in_proj(l, x, g, wt, *, tm):
    m, d = x.shape
    tm = min(tm, m)
    return pl.pallas_call(
        _in_proj_kernel,
        grid=(m // tm,),
        in_specs=[pl.BlockSpec((tm, d), lambda i: (i, 0)), _weight_spec(g, l), _weight_spec(wt, l, single=True)],
        out_specs=[pl.BlockSpec((tm, Z_WIDTH), lambda i: (i, 0)), pl.BlockSpec((d, Z_WIDTH), lambda i: (0, 0))],
        out_shape=[jax.ShapeDtypeStruct((m, Z_WIDTH), F32), jax.ShapeDtypeStruct((d, Z_WIDTH), BF16)],
        scratch_shapes=[pltpu.VMEM((d, Z_WIDTH), BF16)],
        compiler_params=_params("arbitrary"),
        name="in_proj",
    )(x, g, wt)


def _mem_kv_kernel(x_ref, w_ref, o_ref, acc):
    acc[...] = _dot(x_ref[...].astype(BF16), w_ref[...])
    o_ref[...] = acc[...].T


def _mem_kv(l, mem3, w):
    b, mem, d = mem3.shape
    n = w.shape[2]
    return pl.pallas_call(
        _mem_kv_kernel,
        grid=(b,),
        in_specs=[pl.BlockSpec((None, mem, d), lambda i: (i, 0, 0)), _weight_spec(w, l)],
        out_specs=pl.BlockSpec((None, n, mem), lambda i: (i, 0, 0)),
        out_shape=jax.ShapeDtypeStruct((b, n, mem), F32),
        scratch_shapes=[pltpu.VMEM((mem, n), F32)],
        compiler_params=_params("arbitrary"),
        name="mem_kv",
    )(mem3, w)


def _sample_in_kernel(x_ref, g_ref, w_ref, z_ref, zt_ref):
    h = _rms(x_ref[...], g_ref[...]).astype(BF16)
    n = z_ref.shape[1]
    for c in range(0, n, IN_PROJ_COLS):
        w = min(IN_PROJ_COLS, n - c)
        zc = _dot(h, w_ref[:, c:c + w])
        z_ref[:, c:c + w] = zc
        zt_ref[c:c + w, :] = zc.T


def _sample_in(l, x, g, w):
    m, d = x.shape
    n = w.shape[1]
    full = lambda shape: pl.BlockSpec(shape, lambda i: (0,) * len(shape))
    return pl.pallas_call(
        _sample_in_kernel,
        grid=(1,),
        in_specs=[full((m, d)), _weight_spec(g, l), full((d, n))],
        out_specs=[full((m, n)), full((n, m))],
        out_shape=[jax.ShapeDtypeStruct((m, n), F32), jax.ShapeDtypeStruct((n, m), F32)],
        compiler_params=_params("arbitrary"),
        name="sample_in",
    )(x, g, w)


def _mlstm_prompt_kernel(z_ref, g_ref, gb_ref, ng_ref, h_ref, cp_ref, np_ref, m_ref,
                         cp_scr, np_scr, m_scr, ht_scr, st_scr, kw_scr):
    NB, L = z_ref.shape[0], z_ref.shape[1]
    D = HEAD_DIM
    hi = lax.Precision.HIGHEST

    @pl.when(pl.program_id(0) == 0)
    def _():
        cp_scr[...] = jnp.zeros(cp_scr.shape, F32)
        np_scr[...] = jnp.zeros(np_scr.shape, F32)
        m_scr[...] = jnp.zeros(m_scr.shape, F32)

    src = lax.broadcasted_iota(jnp.int32, (L, L), 0)
    qry = lax.broadcasted_iota(jnp.int32, (L, L), 1)
    causal_t = src <= qry
    upper = jnp.where(causal_t, 1.0, 0.0)
    lane_half = lax.broadcasted_iota(jnp.int32, (L, 128), 1) // D
    row8 = lax.broadcasted_iota(jnp.int32, (SUBLANES, 128), 0)

    rows, cols = {}, []
    for b in range(NB):
        g_t = (g_ref[b] + gb_ref[...]).T[0:SUBLANES, :]
        b_rows = jnp.dot(_log_sigmoid(g_t), upper, precision=hi, preferred_element_type=F32)
        to_cols = []
        for h in range(M_HEADS):
            b_row = b_rows[M_HEADS + h:M_HEADS + h + 1, :]
            ci_row = g_t[h:h + 1, :] - b_row
            m_prev = m_scr[b, h, 0:1, 0:1]
            b_last = b_row[:, L - 1:L]
            m_new = jnp.maximum(b_last + m_prev, jnp.max(b_last + ci_row, axis=1, keepdims=True))
            rows[b, h] = dict(b_row=b_row, m_prev=m_prev, m_new=m_new, decay=jnp.exp(b_last + m_prev - m_new))
            to_cols += [ci_row, jnp.exp(b_last + ci_row - m_new)]
        to_cols.append(jnp.zeros((128 - 2 * M_HEADS, L), F32))
        cols.append(jnp.concatenate(to_cols, axis=0).T)

    stats = {}
    for b in range(NB):
        for j in range(M_HEADS // 2):
            slab = 128 * j
            q2 = z_ref[b, :, slab:slab + 128].astype(BF16)
            k2 = z_ref[b, :, M_WIDTH + slab:M_WIDTH + slab + 128] * QK_SCALE
            for par in range(2):
                h = 2 * j + par
                r = rows[b, h]
                ci_col = cols[b][:, 2 * h:2 * h + 1]
                a_row = r["b_row"] + r["m_prev"]
                dm = jnp.where(causal_t, r["b_row"] + ci_col, NEG_INF)
                m_row = jnp.maximum(a_row, jnp.max(dm, axis=0, keepdims=True))
                k_own = jnp.where(lane_half == par, k2, 0.0).astype(BF16)
                s_t = _dot_nt(k_own, q2) * jnp.exp(dm - m_row)
                st_scr[b, j, :, L * par:L * (par + 1)] = s_t.astype(BF16)
                stats[b, h] = dict(w_inter=jnp.exp(a_row - m_row), floor=jnp.exp(-m_row),
                                   den_s=jnp.sum(s_t, axis=0, keepdims=True),
                                   decay=r["decay"], m_new=r["m_new"])
            wk = jnp.where(lane_half == 0, cols[b][:, 4 * j + 1:4 * j + 2], cols[b][:, 4 * j + 3:4 * j + 4])
            kw = k2 * wk
            kw_scr[b, j] = kw.astype(BF16)
            stats[b, j, "k_sum"] = jnp.sum(kw, axis=0, keepdims=True)

    block_diag = (lax.broadcasted_iota(jnp.int32, (128, 128), 0) // D
                  == lax.broadcasted_iota(jnp.int32, (128, 128), 1) // D)
    for b in range(NB):
        for j in range(M_HEADS // 2):
            slab = 128 * j
            q2 = z_ref[b, :, slab:slab + 128].astype(BF16)
            v2 = z_ref[b, :, 2 * M_WIDTH + slab:2 * M_WIDTH + slab + 128].astype(BF16)
            cp = cp_scr[b, j]
            npair = np_scr[b, j]
            qc = _dot_nt(cp.astype(BF16), q2)
            qn = _dot_nt(npair.astype(BF16), q2)
            pv = _dot_tn(v2, st_scr[b, j])
            upd = jnp.where(block_diag, _dot_tn(v2, kw_scr[b, j]), 0.0)
            n_new = jnp.zeros((SUBLANES, 128), F32)
            for par in range(2):
                h = 2 * j + par
                st = stats[b, h]
                hr = slice(D * par, D * (par + 1))
                num = st["w_inter"] * qc[hr, :] + pv[hr, L * par:L * (par + 1)]
                den = st["w_inter"] * qn[par:par + 1, :] + st["den_s"]
                hh = num * (1.0 / jnp.maximum(jnp.abs(den), st["floor"]))
                hh = hh * lax.rsqrt(jnp.mean(hh * hh, axis=0, keepdims=True) + EPS)
                ht_scr[b, D * h:D * (h + 1), :] = hh
                cp_scr[b, j, hr, :] = st["decay"] * cp[hr, :] + upd[hr, :]
                k_sum = jnp.where(lane_half[0:1, :] == par, stats[b, j, "k_sum"], 0.0)
                n_new = jnp.where(row8 == par, st["decay"] * npair[par:par + 1, :] + k_sum, n_new)
                m_scr[b, h] = jnp.broadcast_to(st["m_new"], m_scr.shape[2:])
            np_scr[b, j] = n_new
        o_gate = _sigmoid(z_ref[b, :, 3 * M_WIDTH:4 * M_WIDTH])
        h_ref[b] = ht_scr[b].T * ng_ref[...] * o_gate

    cp_ref[...] = cp_scr[...]
    np_ref[...] = np_scr[...]
    m_ref[...] = m_scr[...]


def _mlstm_prompt(l, z3, gbias, ng):
    b, s, _ = z3.shape
    L = M_CHUNK
    P = M_HEADS // 2
    return pl.pallas_call(
        _mlstm_prompt_kernel,
        grid=(s // L,),
        in_specs=[
            pl.BlockSpec((b, L, 4 * M_WIDTH), lambda c: (0, c, Z_M // (4 * M_WIDTH))),
            pl.BlockSpec((b, L, 128), lambda c: (0, c, Z_G // 128)),
            _weight_spec(gbias, l),
            _weight_spec(ng, l),
        ],
        out_specs=[
            pl.BlockSpec((b, L, M_WIDTH), lambda c: (0, c, 0)),
            pl.BlockSpec((b, P, 128, 128), lambda c: (0, 0, 0, 0)),
            pl.BlockSpec((b, P, SUBLANES, 128), lambda c: (0, 0, 0, 0)),
            pl.BlockSpec((b, M_HEADS, SUBLANES, 128), lambda c: (0, 0, 0, 0)),
        ],
        out_shape=[
            jax.ShapeDtypeStruct((b, s, M_WIDTH), F32),
            jax.ShapeDtypeStruct((b, P, 128, 128), F32),
            jax.ShapeDtypeStruct((b, P, SUBLANES, 128), F32),
            jax.ShapeDtypeStruct((b, M_HEADS, SUBLANES, 128), F32),
        ],
        scratch_shapes=[
            pltpu.VMEM((b, P, 128, 128), F32),
            pltpu.VMEM((b, P, SUBLANES, 128), F32),
            pltpu.VMEM((b, M_HEADS, SUBLANES, 128), F32),
            pltpu.VMEM((b, M_WIDTH, L), F32),
            pltpu.VMEM((b, P, L, 2 * L), BF16),
            pltpu.VMEM((b, P, L, 128), BF16),
        ],
        compiler_params=_params("arbitrary"),
        name="mlstm_prompt",
    )(z3, z3, gbias, ng)


CONV_PAD = 32


def _conv_prompt_kernel(z_ref, w_ref, cb_ref, lg_ref, lb_ref, h_ref, tail_ref, buf, shifted):
    tc = z_ref.shape[0]
    t = pl.program_id(1)

    @pl.when(t == 0)
    def _():
        buf[0:CONV_PAD, :] = jnp.zeros((CONV_PAD, C_WIDTH), F32)

    @pl.when(t > 0)
    def _():
        buf[0:CONV_PAD, :] = buf[tc:tc + CONV_PAD, :]

    u = z_ref[:, 0:C_WIDTH] * _sigmoid(z_ref[:, C_WIDTH:2 * C_WIDTH])
    buf[CONV_PAD:CONV_PAD + tc, :] = u
    n_sh = shifted.shape[1]
    for r in range(1, SUBLANES):
        shifted[r - 1] = buf[r:r + n_sh, :]
    off = CONV_PAD - (C_KERNEL - 1)
    acc = jnp.broadcast_to(cb_ref[...], (tc, C_WIDTH))
    for j in range(C_KERNEL):
        r = (off + j) % SUBLANES
        base = off + j - r
        win = buf[base:base + tc, :] if r == 0 else shifted[r - 1, base:base + tc, :]
        acc = acc + w_ref[j:j + 1, :] * win
    mu = jnp.mean(acc, axis=-1, keepdims=True)
    xc = acc - mu
    y = xc * lax.rsqrt(jnp.mean(xc * xc, axis=-1, keepdims=True) + EPS) * lg_ref[...] + lb_ref[...]
    h_ref[...] = _swish(y)
    tail_ref[...] = buf[tc:tc + CONV_PAD, :]


def _conv_prompt(l, z3, w, cb, lg, lb, *, tc):
    b, s, _ = z3.shape
    return pl.pallas_call(
        _conv_prompt_kernel,
        grid=(b, s // tc),
        in_specs=[
            pl.BlockSpec((None, tc, 2 * C_WIDTH), lambda i, t: (i, t, Z_C // (2 * C_WIDTH))),
            _weight_spec(w, l), _weight_spec(cb, l), _weight_spec(lg, l), _weight_spec(lb, l),
        ],
        out_specs=[
            pl.BlockSpec((None, tc, C_WIDTH), lambda i, t: (i, t, 0)),
            pl.BlockSpec((None, CONV_PAD, C_WIDTH), lambda i, t: (i, 0, 0)),
        ],
        out_shape=[
            jax.ShapeDtypeStruct((b, s, C_WIDTH), F32),
            jax.ShapeDtypeStruct((b, CONV_PAD, C_WIDTH), F32),
        ],
        scratch_shapes=[pltpu.VMEM((CONV_PAD + tc, C_WIDTH), F32),
                        pltpu.VMEM((SUBLANES - 1, CONV_PAD + tc - SUBLANES, C_WIDTH), F32)],
        compiler_params=_params("arbitrary", "arbitrary"),
        name="conv_prompt",
    )(z3, w, cb, lg, lb)


def _swa_prompt_kernel(rb_ref, sink_ref, bucket_ref, q_ref, kv_ref, kvp_ref, o_ref, kt_ref, vt_ref,
                       bias_scr, s_scr, p_scr, ot_scr, *, layer):
    W = WINDOW
    H = S_Q_HEADS
    G = H // S_KV_HEADS
    i = pl.program_id(0)
    n = pl.program_id(1)

    @pl.when(jnp.logical_and(i == 0, n == 0))
    def _():
        bucket = bucket_ref[...]
        prev_key = lax.broadcasted_iota(jnp.int32, (2 * W, W), 0) < W
        for h in range(H):
            acc = jnp.full((2 * W, W), NEG_INF, F32)
            for b in range(N_BUCKETS):
                acc = jnp.where(bucket == b, rb_ref[b, h], acc)
            acc = acc * LOG2E
            bias_scr[0, h] = acc
            bias_scr[1, h] = jnp.where(prev_key, NEG_INF, acc)

    QB = q_ref.shape[0] // W
    NK = (QB + 1) * W
    first = jnp.where(n == 0, 1, 0)
    kk = jnp.concatenate([kvp_ref[:, 0:S_KV_WIDTH], kv_ref[:, 0:S_KV_WIDTH]], axis=0) * (QK_SCALE * LOG2E)
    vv = jnp.concatenate([kvp_ref[:, S_KV_WIDTH:2 * S_KV_WIDTH], kv_ref[:, S_KV_WIDTH:2 * S_KV_WIDTH]], axis=0)
    kk_r = pltpu.roll(kk, HEAD_DIM, axis=1)
    lo_lane = lax.broadcasted_iota(jnp.int32, (NK, S_KV_WIDTH), 1) < HEAD_DIM
    k_var = [[jnp.where(lo_lane, kk, 0.0).astype(BF16), jnp.where(lo_lane, 0.0, kk_r).astype(BF16)],
             [jnp.where(lo_lane, kk_r, 0.0).astype(BF16), jnp.where(lo_lane, 0.0, kk).astype(BF16)]]
    v_t = vv.T.astype(BF16)

    for j in range(QB):
        k0 = j * W
        masked = first if j == 0 else 0
        m_rows = [None] * H
        for hk in range(S_KV_HEADS):
            c0 = 2 * 128 * hk
            q_st = jnp.concatenate([q_ref[k0:k0 + W, c0:c0 + 128], q_ref[k0:k0 + W, c0 + 128:c0 + 256]],
                                   axis=0).astype(BF16)
            for half in range(2):
                s_t = _dot_nt(k_var[hk][half][k0:k0 + 2 * W, :], q_st)
                for slab in range(2):
                    head = G * hk + 2 * slab + half
                    sb = s_t[:, 128 * slab:128 * (slab + 1)] + bias_scr[masked, head]
                    s_scr[j, head] = sb
                    m_rows[head] = jnp.maximum(jnp.max(sb, axis=0, keepdims=True), sink_ref[layer, head] * LOG2E)

        inv = [None] * H
        for head in range(H):
            e = jnp.exp2(s_scr[j, head] - m_rows[head])
            den = jnp.sum(e, axis=0, keepdims=True) + jnp.exp2(sink_ref[layer, head] * LOG2E - m_rows[head])
            inv[head] = 1.0 / den
            p_scr[j, :, 128 * head:128 * (head + 1)] = e.astype(BF16)

        for hk in range(S_KV_HEADS):
            o_t = _dot(v_t[HEAD_DIM * hk:HEAD_DIM * (hk + 1), k0:k0 + 2 * W],
                       p_scr[j, :, 128 * G * hk:128 * G * (hk + 1)])
            for g in range(G):
                head = G * hk + g
                ot_scr[j, HEAD_DIM * head:HEAD_DIM * (head + 1), :] = o_t[:, 128 * g:128 * (g + 1)] * inv[head]
        o_ref[k0:k0 + W, :] = ot_scr[j].T

    @pl.when(n == pl.num_programs(1) - 1)
    def _():
        kt_ref[...] = kv_ref[(QB - 1) * W:QB * W, 0:S_KV_WIDTH].T
        vt_ref[...] = kv_ref[(QB - 1) * W:QB * W, S_KV_WIDTH:2 * S_KV_WIDTH].T


def _swa_prompt(l, z3, rel_bias, sinks, bucket, *, qb):
    b, s, _ = z3.shape
    W = WINDOW
    smem = pl.BlockSpec(memory_space=pltpu.SMEM)
    return pl.pallas_call(
        functools.partial(_swa_prompt_kernel, layer=l),
        grid=(b, s // (qb * W)),
        in_specs=[
            smem, smem, _const_spec((2 * W, W)),
            pl.BlockSpec((None, qb * W, S_WIDTH), lambda i, n: (i, n, Z_SQ // S_WIDTH)),
            pl.BlockSpec((None, qb * W, 2 * S_KV_WIDTH), lambda i, n: (i, n, Z_SKV // (2 * S_KV_WIDTH))),
            pl.BlockSpec((None, W, 2 * S_KV_WIDTH),
                         lambda i, n: (i, jnp.maximum(n * qb - 1, 0), Z_SKV // (2 * S_KV_WIDTH))),
        ],
        out_specs=[
            pl.BlockSpec((None, qb * W, S_WIDTH), lambda i, n: (i, n, 0)),
            pl.BlockSpec((None, S_KV_WIDTH, W), lambda i, n: (i, 0, 0)),
            pl.BlockSpec((None, S_KV_WIDTH, W), lambda i, n: (i, 0, 0)),
        ],
        out_shape=[
            jax.ShapeDtypeStruct((b, s, S_WIDTH), F32),
            jax.ShapeDtypeStruct((b, S_KV_WIDTH, W), F32),
            jax.ShapeDtypeStruct((b, S_KV_WIDTH, W), F32),
        ],
        scratch_shapes=[
            pltpu.VMEM((2, S_Q_HEADS, 2 * W, W), F32),
            pltpu.VMEM((qb, S_Q_HEADS, 2 * W, W), F32),
            pltpu.VMEM((qb, 2 * W, S_Q_HEADS * W), BF16),
            pltpu.VMEM((qb, S_WIDTH, W), F32),
        ],
        compiler_params=_params("arbitrary", "arbitrary"),
        name="swa_prompt",
    )(rel_bias, sinks, bucket, z3, z3, z3)


MIX_SUB = 512


def _mix_out_prompt_kernel(x_ref, hm_ref, hc_ref, hs_ref, wout_ref, g2_ref, wq_ref, mk_ref, mv_ref, wo_ref, o_ref,
                           x1_scr, qx_scr, p_scr):
    tm = x_ref.shape[0]
    mem = mk_ref.shape[1]
    mk_t = mk_ref[...]
    mv_t = mv_ref[...]
    row_head = lax.broadcasted_iota(jnp.int32, mk_t.shape, 0) // HEAD_DIM
    k_heads = [jnp.where(row_head == h, mk_t, 0.0).astype(BF16) for h in range(X_HEADS)]
    v_cat = jnp.concatenate([jnp.where(row_head == h, mv_t, 0.0).astype(BF16) for h in range(X_HEADS)], axis=1)
    subs = [slice(r0, r0 + MIX_SUB) for r0 in range(0, tm, MIX_SUB)]
    for rows in subs:
        cat = jnp.concatenate([hm_ref[rows, :].astype(BF16), hc_ref[rows, :].astype(BF16),
                               hs_ref[rows, :].astype(BF16)], axis=1)
        x1 = x_ref[rows, :] + _dot(cat, wout_ref[...])
        x1_scr[rows, :] = x1
        qx = _dot(_rms(x1, g2_ref[...]).astype(BF16), wq_ref[...])
        qx_scr[rows, :] = (qx * (QK_SCALE * LOG2E)).astype(BF16)
    for rows in subs:
        qx = qx_scr[rows, :]
        for h in range(X_HEADS):
            s = _dot(qx, k_heads[h])
            e = jnp.exp2(s - jnp.max(s, axis=1, keepdims=True))
            p_scr[rows, mem * h:mem * (h + 1)] = (e * (1.0 / jnp.sum(e, axis=1, keepdims=True))).astype(BF16)
    for rows in subs:
        o = _dot_nt(p_scr[rows, :], v_cat)
        o_ref[rows, :] = x1_scr[rows, :] + _dot(o.astype(BF16), wo_ref[...])


def _mix_out_prompt(l, x3, hm, hc, hs, wout, g2, wq, mkv, wo, *, tm):
    b, s, d = x3.shape
    tm = min(tm, s)
    mem = mkv.shape[2]
    row = lambda w: pl.BlockSpec((None, tm, w), lambda i, t: (i, t, 0))
    return pl.pallas_call(
        _mix_out_prompt_kernel,
        grid=(b, s // tm),
        in_specs=[
            row(d), row(M_WIDTH), row(C_WIDTH), row(S_WIDTH),
            _weight_spec(wout, l), _weight_spec(g2, l), _weight_spec(wq, l),
            pl.BlockSpec((None, X_WIDTH, mem), lambda i, t: (i, 0, 0)),
            pl.BlockSpec((None, X_WIDTH, mem), lambda i, t: (i, 1, 0)),
            _weight_spec(wo, l),
        ],
        out_specs=row(d),
        out_shape=jax.ShapeDtypeStruct((b, s, d), F32),
        scratch_shapes=[pltpu.VMEM((tm, d), F32), pltpu.VMEM((tm, X_WIDTH), BF16),
                        pltpu.VMEM((tm, X_HEADS * mem), BF16)],
        compiler_params=_params("arbitrary", "arbitrary"),
        name="mix_out_prompt",
    )(x3, hm, hc, hs, wout, g2, wq, mkv, mkv, wo)


FFN_PAD = 8


def _ffn_prompt_kernel(x_ref, g3_ref, wup_ref, cw_ref, cb_ref, wdn_ref, gf_ref, o_ref, tail_ref,
                       carry, act, *, final):
    tm = x_ref.shape[0]
    dff = wdn_ref.shape[0]

    @pl.when(pl.program_id(1) == 0)
    def _():
        carry[...] = jnp.zeros(carry.shape, F32)

    x = x_ref[...]
    h = _rms(x, g3_ref[...]).astype(BF16)
    w = FFN_CHUNK
    row8 = lax.broadcasted_iota(jnp.int32, (FFN_PAD, w), 0)
    for c in range(0, dff, w):
        a = _dot(h, wup_ref[:, c:c + w])
        g = _dot(h, wup_ref[:, dff + c:dff + c + w])
        prev = carry[:, c:c + w]
        carry[:, c:c + w] = g[tm - FFN_PAD:tm, :]
        taps = []
        for s in (2, 1):
            gs = pltpu.roll(g, s, axis=0)
            head = jnp.where(row8 < s, pltpu.roll(prev, s, axis=0), gs[0:FFN_PAD, :])
            taps.append(jnp.concatenate([head, gs[FFN_PAD:, :]], axis=0))
        gc = (cw_ref[0:1, c:c + w] * taps[0] + cw_ref[1:2, c:c + w] * taps[1]
              + cw_ref[2:3, c:c + w] * g + cb_ref[:, c:c + w])
        act[:, c:c + w] = (_swish(gc) * a).astype(BF16)
    y = x + _dot(act[...], wdn_ref[...])
    o_ref[...] = _rms(y, gf_ref[...]) if final else y
    tail_ref[...] = carry[...]


def _ffn_prompt(l, x3, g3, wup, cw, cb, wdn, gf, *, tm, final):
    b, s, d = x3.shape
    tm = min(tm, s)
    dff = wdn.shape[1]
    return pl.pallas_call(
        functools.partial(_ffn_prompt_kernel, final=final),
        grid=(b, s // tm),
        in_specs=[
            pl.BlockSpec((None, tm, d), lambda i, t: (i, t, 0)),
            _weight_spec(g3, l), _weight_spec(wup, l), _weight_spec(cw, l), _weight_spec(cb, l),
            _weight_spec(wdn, l), _const_spec((1, d)),
        ],
        out_specs=[
            pl.BlockSpec((None, tm, d), lambda i, t: (i, t, 0)),
            pl.BlockSpec((None, FFN_PAD, dff), lambda i, t: (i, 0, 0)),
        ],
        out_shape=[jax.ShapeDtypeStruct((b, s, d), F32), jax.ShapeDtypeStruct((b, FFN_PAD, dff), F32)],
        scratch_shapes=[
            pltpu.VMEM((FFN_PAD, dff), F32),
            pltpu.VMEM((tm, dff), BF16),
        ],
        compiler_params=_params("arbitrary", "arbitrary"),
        name="ffn_prompt",
    )(x3, g3, wup, cw, cb, wdn, gf)


SWA_SAMPLE_BLOCK = 32
XATTN_SAMPLE_BLOCK = 16
CONV_SAMPLE_BLOCK = 32


def _mlstm_sample_kernel(gb_ref, q_ref, k_ref, v_ref, o_ref, g_ref, ng_ref, c_ref, n_ref, m_ref, *rest, layer):
    h_ref, co_ref, no_ref, mo_ref, kw_scr = rest[-5:]
    _copy_earlier_layers(rest[:-5], (co_ref, no_ref, mo_ref))
    h = pl.program_id(0)
    i_pre = g_ref[pl.ds(h, 1), :] + gb_ref[layer, h]
    f_pre = g_ref[pl.ds(M_HEADS + h, 1), :] + gb_ref[layer, M_HEADS + h]
    a = _log_sigmoid(f_pre) + m_ref[pl.ds(h, 1), :]
    m_t = jnp.maximum(a, i_pre)
    w_old = jnp.exp(a - m_t)
    w_new = jnp.exp(i_pre - m_t)
    q = q_ref[...]
    k = k_ref[...] * QK_SCALE
    v = v_ref[...]
    n_old = n_ref[...]
    kw_scr[...] = k * w_new

    def body(d, acc):
        c_old = c_ref[d]
        co_ref[layer, d] = w_old * c_old + kw_scr[pl.ds(d, 1), :] * v
        return acc + q_ref[pl.ds(d, 1), :] * c_old

    qc = lax.fori_loop(0, HEAD_DIM, body, jnp.zeros(v.shape, F32), unroll=8)
    s = jnp.sum(q * k, axis=0, keepdims=True) * w_new
    num = w_old * qc + s * v
    den = w_old * jnp.sum(q * n_old, axis=0, keepdims=True) + s
    hh = num / jnp.maximum(jnp.abs(den), jnp.exp(-m_t))
    hh = hh * lax.rsqrt(jnp.mean(hh * hh, axis=0, keepdims=True) + EPS) * ng_ref[...]
    h_ref[...] = hh * _sigmoid(o_ref[...])
    no_ref[layer] = w_old * n_old + kw_scr[...]
    mo_ref[layer, pl.ds(h, 1), :] = m_t


def _mlstm_sample(l, zt, gb, ngt, c_all, n_all, m_all, prev):
    bsz = zt.shape[1]
    D = HEAD_DIM
    feat = lambda off: pl.BlockSpec((D, bsz), lambda h: (off // D + h, 0))
    c_spec = lambda n: pl.BlockSpec((n, None, D, D, bsz), lambda h: (0, h, 0, 0, 0))
    n_spec = lambda n: pl.BlockSpec((n, None, D, bsz), lambda h: (0, h, 0, 0))
    m_spec = lambda n: pl.BlockSpec((n, M_HEADS, bsz), lambda h: (0, 0, 0))
    prev_specs = [c_spec(l), n_spec(l), m_spec(l)] if prev else []
    return pl.pallas_call(
        functools.partial(_mlstm_sample_kernel, layer=l),
        grid=(M_HEADS,),
        in_specs=[
            pl.BlockSpec(memory_space=pltpu.SMEM),
            feat(Z_M), feat(Z_M + M_WIDTH), feat(Z_M + 2 * M_WIDTH), feat(Z_M + 3 * M_WIDTH),
            pl.BlockSpec((SUBLANES, bsz), lambda h: (Z_G // SUBLANES, 0)),
            pl.BlockSpec((None, D, bsz), lambda h: (l, h, 0)),
            pl.BlockSpec((None, None, D, D, bsz), lambda h: (l, h, 0, 0, 0)),
            pl.BlockSpec((None, None, D, bsz), lambda h: (l, h, 0, 0)),
            pl.BlockSpec((None, M_HEADS, bsz), lambda h: (l, 0, 0)),
        ] + prev_specs,
        out_specs=[pl.BlockSpec((D, bsz), lambda h: (h, 0)), c_spec(l + 1), n_spec(l + 1), m_spec(l + 1)],
        out_shape=[
            jax.ShapeDtypeStruct((M_WIDTH, bsz), F32),
            jax.ShapeDtypeStruct((l + 1, M_HEADS, D, D, bsz), F32),
            jax.ShapeDtypeStruct((l + 1, M_HEADS, D, bsz), F32),
            jax.ShapeDtypeStruct((l + 1, M_HEADS, bsz), F32),
        ],
        scratch_shapes=[pltpu.VMEM((D, bsz), F32)],
        compiler_params=_params("arbitrary"),
        name="mlstm_sample",
    )(gb, zt, zt, zt, zt, zt, ngt, c_all, n_all, m_all, *prev)


def _conv_sample_kernel(z_ref, hist_ref, w_ref, cb_ref, lg_ref, lb_ref, *rest):
    h_ref, hist_out_ref = rest[-2:]
    _copy_earlier_layers(rest[:-2], (hist_out_ref,))
    l = hist_out_ref.shape[0] - 1
    nh = C_KERNEL - 1
    u = z_ref[:, 0:C_WIDTH] * _sigmoid(z_ref[:, C_WIDTH:2 * C_WIDTH])
    acc = cb_ref[...] + w_ref[nh:nh + 1, :] * u
    for j in range(nh):
        acc = acc + w_ref[j:j + 1, :] * hist_ref[j]
    mu = jnp.mean(acc, axis=-1, keepdims=True)
    xc = acc - mu
    y = xc * lax.rsqrt(jnp.mean(xc * xc, axis=-1, keepdims=True) + EPS) * lg_ref[...] + lb_ref[...]
    h_ref[...] = _swish(y)
    for j in range(nh - 1):
        hist_out_ref[l, j] = hist_ref[j + 1]
    hist_out_ref[l, nh - 1] = u


def _conv_sample(l, z, hist_all, w, cb, lg, lb, prev):
    bsz = z.shape[0]
    nh = hist_all.shape[1]
    R = min(CONV_SAMPLE_BLOCK, bsz)
    hist_spec = lambda n: pl.BlockSpec((n, nh, R, C_WIDTH), lambda i: (0, 0, i, 0))
    return pl.pallas_call(
        _conv_sample_kernel,
        grid=(bsz // R,),
        in_specs=[
            pl.BlockSpec((R, 2 * C_WIDTH), lambda i: (i, Z_C // (2 * C_WIDTH))),
            pl.BlockSpec((None, nh, R, C_WIDTH), lambda i: (l, 0, i, 0)),
            _weight_spec(w, l), _weight_spec(cb, l), _weight_spec(lg, l), _weight_spec(lb, l),
        ] + [hist_spec(l)] * len(prev),
        out_specs=[pl.BlockSpec((R, C_WIDTH), lambda i: (i, 0)), hist_spec(l + 1)],
        out_shape=[jax.ShapeDtypeStruct((bsz, C_WIDTH), F32), jax.ShapeDtypeStruct((l + 1, nh, bsz, C_WIDTH), F32)],
        compiler_params=_params("arbitrary"),
        name="conv_sample",
    )(z, hist_all, w, cb, lg, lb, *prev)


def _copy_earlier_layers(prev_refs, out_refs):
    for p_ref, o_ref in zip(prev_refs, out_refs):
        o_ref[0:p_ref.shape[0]] = p_ref[...]


def _swa_sample_kernel(q_ref, kvn_ref, kv2_ref, kc_ref, vc_ref, bias_ref, aux_ref, *rest):
    o_ref, ko_ref, vo_ref = rest[-3:]
    _copy_earlier_layers(rest[:-3], (ko_ref, vo_ref))
    l = ko_ref.shape[0] - 1
    R = q_ref.shape[0]
    W = kc_ref.shape[2]
    H = S_Q_HEADS
    shape = (R, H, 128)
    row = lax.broadcasted_iota(jnp.int32, shape, 1)
    lane_half = lax.broadcasted_iota(jnp.int32, shape, 2) // HEAD_DIM
    q_half = row % 2
    kv_head = row // (H // S_KV_HEADS)
    qs = jnp.zeros(shape, F32)
    for j in range(H // 2):
        qs = jnp.where(row // 2 == j, q_ref[:, :, 128 * j:128 * (j + 1)], qs)
    q8 = jnp.where(lane_half == kv_head, jnp.where(q_half == kv_head, qs, pltpu.roll(qs, HEAD_DIM, axis=2)), 0.0)
    k_new = kvn_ref[:, :, 0:S_KV_WIDTH]
    v_new = kvn_ref[:, :, S_KV_WIDTH:2 * S_KV_WIDTH]
    s = jnp.einsum("bqd,bdk->bqk", q8.astype(BF16), kc_ref[...].astype(BF16), preferred_element_type=F32) * QK_SCALE
    s = s + bias_ref[...][None]
    s_new = jnp.sum(q8 * k_new, axis=2, keepdims=True) * QK_SCALE + aux_ref[:, 0:1][None]
    sink = aux_ref[:, 1:2][None]
    m = jnp.maximum(jnp.maximum(jnp.max(s, axis=2, keepdims=True), s_new), sink)
    e = jnp.exp(s - m)
    e_new = jnp.exp(s_new - m)
    inv = 1.0 / (jnp.sum(e, axis=2, keepdims=True) + e_new + jnp.exp(sink - m))
    o8 = jnp.einsum("bqk,bdk->bqd", e.astype(BF16), vc_ref[...].astype(BF16), preferred_element_type=F32)
    o8 = (o8 + e_new * v_new) * inv
    o8 = jnp.where(lane_half == q_half, jnp.where(q_half == kv_head, o8, pltpu.roll(o8, HEAD_DIM, axis=2)), 0.0)
    for j in range(H // 2):
        o_ref[:, :, 128 * j:128 * (j + 1)] = jnp.sum(jnp.where(row // 2 == j, o8, 0.0), axis=1, keepdims=True)
    k_cols = kv2_ref[:, 0:S_KV_WIDTH].T
    v_cols = kv2_ref[:, S_KV_WIDTH:2 * S_KV_WIDTH].T
    last = lax.broadcasted_iota(jnp.int32, (S_KV_WIDTH, W), 1) == W - 1
    for r in range(R):
        ko_ref[l, r] = jnp.where(last, k_cols[:, r:r + 1], pltpu.roll(kc_ref[r], W - 1, axis=1))
        vo_ref[l, r] = jnp.where(last, v_cols[:, r:r + 1], pltpu.roll(vc_ref[r], W - 1, axis=1))


def _swa_sample(l, z, z3, kc_all, vc_all, bias, aux, prev):
    bsz = z.shape[0]
    R = min(SWA_SAMPLE_BLOCK, bsz)
    W = kc_all.shape[3]
    cache_in = pl.BlockSpec((None, R, S_KV_WIDTH, W), lambda i: (l, i, 0, 0))
    cache_prev = pl.BlockSpec((l, R, S_KV_WIDTH, W), lambda i: (0, i, 0, 0))
    cache_out = pl.BlockSpec((l + 1, R, S_KV_WIDTH, W), lambda i: (0, i, 0, 0))
    return pl.pallas_call(
        _swa_sample_kernel,
        grid=(bsz // R,),
        in_specs=[
            pl.BlockSpec((R, 1, S_WIDTH), lambda i: (i, 0, Z_SQ // S_WIDTH)),
            pl.BlockSpec((R, 1, 2 * S_KV_WIDTH), lambda i: (i, 0, Z_SKV // (2 * S_KV_WIDTH))),
            pl.BlockSpec((R, 2 * S_KV_WIDTH), lambda i: (i, Z_SKV // (2 * S_KV_WIDTH))),
            cache_in, cache_in,
            _const_spec((S_Q_HEADS, 128)), _weight_spec(aux, l),
        ] + [cache_prev] * len(prev),
        out_specs=[pl.BlockSpec((R, 1, S_WIDTH), lambda i: (i, 0, 0)), cache_out, cache_out],
        out_shape=[
            jax.ShapeDtypeStruct((bsz, 1, S_WIDTH), F32),
            jax.ShapeDtypeStruct((l + 1, bsz, S_KV_WIDTH, W), F32),
            jax.ShapeDtypeStruct((l + 1, bsz, S_KV_WIDTH, W), F32),
        ],
        compiler_params=_params("arbitrary"),
        name="swa_sample",
    )(z3, z3, z, kc_all, vc_all, bias, aux, *prev)


def _mix_out_sample_kernel(x_ref, hmt_ref, hc_ref, hs_ref, wout_ref, g2_ref, wq_ref, x1_ref, q_ref):
    cat = jnp.concatenate([hmt_ref[...].T.astype(BF16), hc_ref[...].astype(BF16), hs_ref[...].astype(BF16)], axis=1)
    x1 = x_ref[...] + _dot(cat, wout_ref[...])
    x1_ref[...] = x1
    q_ref[...] = _dot(_rms(x1, g2_ref[...]).astype(BF16), wq_ref[...])


def _mix_out_sample(l, x, hm, hc, hs, wout, g2, wq):
    bsz, d = x.shape
    full = lambda a: pl.BlockSpec(a.shape, lambda i: (0,) * a.ndim)
    args = (x, hm, hc, hs, wout, g2, wq)
    return pl.pallas_call(
        _mix_out_sample_kernel,
        grid=(1,),
        in_specs=[full(x), full(hm), full(hc), full(hs), _weight_spec(wout, l), _weight_spec(g2, l),
                  _weight_spec(wq, l)],
        out_specs=[pl.BlockSpec((bsz, d), lambda i: (0, 0)), pl.BlockSpec((bsz, X_WIDTH), lambda i: (0, 0))],
        out_shape=[jax.ShapeDtypeStruct((bsz, d), F32), jax.ShapeDtypeStruct((bsz, X_WIDTH), F32)],
        compiler_params=_params("arbitrary"),
        name="mix_out_sample",
    )(*args)


def _xattn_sample_kernel(q_ref, k_ref, v_ref, o_ref):
    R = k_ref.shape[0]
    shape = (R, SUBLANES, X_WIDTH)
    row = lax.broadcasted_iota(jnp.int32, shape, 1)
    lane_head = lax.broadcasted_iota(jnp.int32, shape, 2) // HEAD_DIM
    own = row == lane_head
    q8 = jnp.where(own, jnp.broadcast_to(q_ref[...], shape), 0.0).astype(BF16)
    s = jnp.einsum("bqd,bdk->bqk", q8, k_ref[...].astype(BF16), preferred_element_type=F32) * QK_SCALE
    e = jnp.exp(s - jnp.max(s, axis=2, keepdims=True))
    p = (e / jnp.sum(e, axis=2, keepdims=True)).astype(BF16)
    o8 = jnp.einsum("bqk,bdk->bqd", p, v_ref[...].astype(BF16), preferred_element_type=F32)
    o_ref[...] = jnp.sum(jnp.where(own, o8, 0.0), axis=1, keepdims=True)


def _xattn_sample(l, q3, k_all, v_all):
    _, bsz, w, mem = k_all.shape
    R = min(XATTN_SAMPLE_BLOCK, bsz)
    kv = pl.BlockSpec((None, R, w, mem), lambda i: (l, i, 0, 0))
    qo = pl.BlockSpec((R, 1, w), lambda i: (i, 0, 0))
    return pl.pallas_call(
        _xattn_sample_kernel,
        grid=(bsz // R,),
        in_specs=[qo, kv, kv],
        out_specs=qo,
        out_shape=jax.ShapeDtypeStruct((bsz, 1, w), F32),
        compiler_params=_params("arbitrary"),
        name="xattn_sample",
    )(q3, k_all, v_all)


def _ffn_sample_kernel(x1_ref, ox_ref, wo_ref, g3_ref, wup_ref, cw_ref, cb_ref, wdn_ref, gf_ref, hist_ref,
                       *rest, final):
    o_ref, hist_out_ref, act = rest[-3:]
    _copy_earlier_layers(rest[:-3], (hist_out_ref,))
    l = hist_out_ref.shape[0] - 1
    dff = wdn_ref.shape[0]
    x = x1_ref[...] + _dot(ox_ref[...].astype(BF16), wo_ref[...])
    h = _rms(x, g3_ref[...]).astype(BF16)
    for c in range(0, dff, FFN_CHUNK):
        a = _dot(h, wup_ref[:, c:c + FFN_CHUNK])
        g = _dot(h, wup_ref[:, dff + c:dff + c + FFN_CHUNK])
        h1 = hist_ref[:, 1, c:c + FFN_CHUNK]
        gc = (cw_ref[0:1, c:c + FFN_CHUNK] * hist_ref[:, 0, c:c + FFN_CHUNK]
              + cw_ref[1:2, c:c + FFN_CHUNK] * h1
              + cw_ref[2:3, c:c + FFN_CHUNK] * g + cb_ref[:, c:c + FFN_CHUNK])
        act[:, c:c + FFN_CHUNK] = (_swish(gc) * a).astype(BF16)
        hist_out_ref[l, :, 0, c:c + FFN_CHUNK] = h1
        hist_out_ref[l, :, 1, c:c + FFN_CHUNK] = g
    y = x + _dot(act[...], wdn_ref[...])
    o_ref[...] = _rms(y, gf_ref[...]) if final else y


def _ffn_sample(l, x1, ox, wo, g3, wup, cw, cb, wdn, gf, hist_all, prev, *, final):
    bsz, d = x1.shape
    dff = wdn.shape[1]
    full = lambda a: pl.BlockSpec(a.shape, lambda i: (0,) * a.ndim)
    hshape = hist_all.shape[1:]
    return pl.pallas_call(
        functools.partial(_ffn_sample_kernel, final=final),
        grid=(1,),
        in_specs=[full(x1), full(ox), _weight_spec(wo, l), _weight_spec(g3, l), _weight_spec(wup, l),
                  _weight_spec(cw, l), _weight_spec(cb, l),
                  _weight_spec(wdn, l), full(gf), pl.BlockSpec((None,) + hshape, lambda i: (l, 0, 0, 0))]
                 + [full(p) for p in prev],
        out_specs=[pl.BlockSpec((bsz, d), lambda i: (0, 0)), pl.BlockSpec((l + 1,) + hshape, lambda i: (0, 0, 0, 0))],
        out_shape=[jax.ShapeDtypeStruct((bsz, d), F32), jax.ShapeDtypeStruct((l + 1,) + hshape, F32)],
        scratch_shapes=[pltpu.VMEM((bsz, dff), BF16)],
        compiler_params=_params("arbitrary"),
        name="ffn_sample",
    )(x1, ox, wo, g3, wup, cw, cb, wdn, gf, hist_all, *prev)


def _t5_buckets(dist):
    n = np.maximum(dist, 0)
    max_exact = N_BUCKETS // 2
    nf = np.maximum(n, max_exact).astype(np.float32)
    large = max_exact + (np.log(nf / np.float32(max_exact)) / np.float32(math.log(MAX_DISTANCE / max_exact))
                         * np.float32(N_BUCKETS - max_exact)).astype(np.int32)
    return np.where(n < max_exact, n, np.minimum(large, N_BUCKETS - 1))


def _prompt_buckets():
    W = WINDOW
    dist = np.arange(W)[None, :] + W - np.arange(2 * W)[:, None]
    band = (dist >= 0) & (dist < W)
    return np.where(band, _t5_buckets(dist), -1).astype(np.int32)


def _swa_tables(rel_bias):
    W = WINDOW
    dist_c = W - np.arange(W)
    tab = jnp.transpose(rel_bias[_t5_buckets(dist_c)], (1, 0))
    cache_bias = jnp.where((dist_c < W)[None], tab, NEG_INF)
    return cache_bias, rel_bias[0]


def kernel(x_prompt, x_sample, mem_prompt, state_mlstm_C, state_mlstm_n, state_mlstm_m, state_conv, cache_swa_k, cache_swa_v, cache_mem_k, cache_mem_v, state_ffn_conv, rel_bias, norm1_g, w_in, b_i, b_f, mlstm_norm_g, conv_w, conv_b, conv_ln_g, conv_ln_b, swa_sinks, w_out, norm2_g, w_xq, w_xk, w_xv, w_xo, norm3_g, w_up, ffn_conv_w, ffn_conv_b, w_down, final_norm_g):
    depth = w_in.shape[0]
    bp, seq, d = x_prompt.shape
    bs = x_sample.shape[0]
    mem = mem_prompt.shape[1]
    dff = w_down.shape[1]
    W = WINDOW
    nh = C_KERNEL - 1

    xp = x_prompt
    xs = x_sample.reshape(bs, d)
    gf = final_norm_g.reshape(1, d)
    c_all = jnp.transpose(state_mlstm_C, (0, 2, 3, 4, 1))
    n_all = jnp.transpose(state_mlstm_n, (0, 2, 3, 1))
    m_all = jnp.transpose(state_mlstm_m, (0, 2, 1))
    hist_all = jnp.transpose(state_conv, (0, 2, 1, 3))
    kc_all = jnp.transpose(cache_swa_k, (0, 1, 3, 4, 2)).reshape(depth, bs, S_KV_WIDTH, W)
    vc_all = jnp.transpose(cache_swa_v, (0, 1, 3, 4, 2)).reshape(depth, bs, S_KV_WIDTH, W)
    mk_all = jnp.transpose(cache_mem_k, (0, 1, 3, 4, 2)).reshape(depth, bs, X_WIDTH, mem)
    mv_all = jnp.transpose(cache_mem_v, (0, 1, 3, 4, 2)).reshape(depth, bs, X_WIDTH, mem)
    pm_c, pm_n, pm_m, p_conv, p_k, p_v, p_mk, p_mv, p_ffn = ([] for _ in range(9))
    s_mlstm, s_conv, s_kv, s_ffn = [], [], [], []

    win_t = jnp.swapaxes(w_in, 1, 2)
    wout = w_out.astype(BF16)
    wxq = w_xq.astype(BF16)
    wxkv = jnp.concatenate([w_xk, w_xv], axis=2).astype(BF16)
    wxo = w_xo.astype(BF16)
    wup = w_up.astype(BF16)
    wdn = w_down.astype(BF16)
    bucket = jnp.asarray(_prompt_buckets())

    rows = lambda a: a.reshape(depth, 1, -1)
    g1, g2, g3 = rows(norm1_g), rows(norm2_g), rows(norm3_g)
    gb8 = jnp.concatenate([b_i, b_f], axis=1)
    gbias = rows(jnp.concatenate([gb8, jnp.zeros((depth, 128 - 2 * M_HEADS), F32)], axis=1))
    ng = rows(mlstm_norm_g)
    ngt = jnp.broadcast_to(mlstm_norm_g[:, :, None], (depth, M_WIDTH, bs))
    cw = jnp.concatenate([conv_w, jnp.zeros((depth, CONV_PAD - C_KERNEL, C_WIDTH), F32)], axis=1)
    cb, lg, lb = rows(conv_b), rows(conv_ln_g), rows(conv_ln_b)
    fcw = jnp.concatenate([ffn_conv_w, jnp.zeros((depth, SUBLANES - FFN_KERNEL, dff), F32)], axis=1)
    fcb = rows(ffn_conv_b)
    cache_bias, bias0 = _swa_tables(rel_bias)
    aux = jnp.concatenate([jnp.broadcast_to(bias0[None, :, None], (depth, S_Q_HEADS, 1)), swa_sinks[:, :, None],
                           jnp.zeros((depth, S_Q_HEADS, 126), F32)], axis=2)

    for l in range(depth):
        last = l == depth - 1

        mkv = _mem_kv(l, mem_prompt, wxkv)
        z, win = _in_proj(l, xp.reshape(bp * seq, d), g1, win_t, tm=1024)
        z = z.reshape(bp, seq, Z_WIDTH)
        hm, cpair, npair, mm = _mlstm_prompt(l, z, gbias, ng)
        hc, ctail = _conv_prompt(l, z, cw, cb, lg, lb, tc=min(1024, seq))
        hs, kt, vt = _swa_prompt(l, z, rel_bias, swa_sinks, bucket, qb=min(8, seq // W))
        xp = _mix_out_prompt(l, xp, hm, hc, hs, wout, g2, wxq, mkv, wxo, tm=1024)
        xp, ftail = _ffn_prompt(l, xp, g3, wup, fcw, fcb, wdn, gf, tm=512, final=last)
        half = lambda h: slice(HEAD_DIM * (h % 2), HEAD_DIM * (h % 2 + 1))
        pm_c.append(jnp.stack([jnp.swapaxes(cpair[:, h // 2, half(h), half(h)], 1, 2) for h in range(M_HEADS)], axis=1))
        pm_n.append(jnp.stack([npair[:, h // 2, h % 2, half(h)] for h in range(M_HEADS)], axis=1))
        pm_m.append(mm[:, :, 0, 0])
        p_conv.append(ctail[:, CONV_PAD - nh:, :])
        p_k.append(jnp.transpose(kt.reshape(bp, S_KV_HEADS, HEAD_DIM, W), (0, 3, 1, 2)))
        p_v.append(jnp.transpose(vt.reshape(bp, S_KV_HEADS, HEAD_DIM, W), (0, 3, 1, 2)))
        p_mk.append(jnp.transpose(mkv[:, 0:X_WIDTH, :].reshape(bp, X_HEADS, HEAD_DIM, mem), (0, 3, 1, 2)))
        p_mv.append(jnp.transpose(mkv[:, X_WIDTH:, :].reshape(bp, X_HEADS, HEAD_DIM, mem), (0, 3, 1, 2)))
        p_ffn.append(ftail[:, FFN_PAD - (FFN_KERNEL - 1):, :])

        zs, zst = _sample_in(l, xs, g1, win)
        hmt_s, *s_mlstm = _mlstm_sample(l, zst, gb8, ngt, c_all, n_all, m_all, s_mlstm)
        hc_s, *s_conv = _conv_sample(l, zs, hist_all, cw, cb, lg, lb, s_conv)
        hs_s, *s_kv = _swa_sample(l, zs, zs.reshape(bs, 1, Z_WIDTH), kc_all, vc_all, cache_bias, aux, s_kv)
        x1, qx = _mix_out_sample(l, xs, hmt_s, hc_s, hs_s.reshape(bs, S_WIDTH), wout, g2, wxq)
        ox = _xattn_sample(l, qx.reshape(bs, 1, X_WIDTH), mk_all, mv_all)
        xs, *s_ffn = _ffn_sample(l, x1, ox.reshape(bs, X_WIDTH), wxo, g3, wup, fcw, fcb, wdn, gf,
                                 state_ffn_conv, s_ffn, final=last)

    st = jnp.stack
    tr = jnp.transpose
    s_c, s_n, s_m = s_mlstm
    s_k, s_v = (a.reshape(depth, bs, S_KV_HEADS, HEAD_DIM, W) for a in s_kv)
    return (xp, xs.reshape(bs, 1, d),
            st(pm_c), st(pm_n), st(pm_m), st(p_conv), st(p_k), st(p_v), st(p_mk), st(p_mv), st(p_ffn),
            tr(s_c, (0, 4, 1, 2, 3)), tr(s_n, (0, 3, 1, 2)), tr(s_m, (0, 2, 1)),
            tr(s_conv[0], (0, 2, 1, 3)), tr(s_k, (0, 1, 4, 2, 3)), tr(s_v, (0, 1, 4, 2, 3)),
            s_ffn[0])
```

```python
import functools
import math

import numpy as np
import jax
import jax.numpy as jnp
from jax import lax
from jax.experimental import pallas as pl
from jax.experimental.pallas import tpu as pltpu

F32 = jnp.float32
BF16 = jnp.bfloat16
EPS = 1e-6
NEG_INF = float("-inf")

HEAD_DIM = 64
M_HEADS = 4
M_WIDTH = M_HEADS * HEAD_DIM
C_WIDTH = 256
C_KERNEL = 31
S_Q_HEADS = 8
S_KV_HEADS = 2
S_WIDTH = S_Q_HEADS * HEAD_DIM
S_KV_WIDTH = S_KV_HEADS * HEAD_DIM
WINDOW = 128
N_BUCKETS = 32
MAX_DISTANCE = 128
X_HEADS = 4
X_WIDTH = X_HEADS * HEAD_DIM
FFN_KERNEL = 3
QK_SCALE = HEAD_DIM ** -0.5
LOG2E = math.log2(math.e)

Z_M = 0
Z_C = 4 * M_WIDTH
Z_SQ = Z_C + 2 * C_WIDTH
Z_SKV = Z_SQ + S_WIDTH
Z_G = Z_SKV + 2 * S_KV_WIDTH
LANES = 128
SUBLANES = 8
Z_WIDTH = Z_G + LANES
VMEM_LIMIT = 56 * 1024 * 1024

M_CHUNK = 128
FFN_CHUNK = 256


def _params(*sem):
    return pltpu.CompilerParams(dimension_semantics=sem, vmem_limit_bytes=VMEM_LIMIT)


def _const_spec(shape):
    nd = len(shape)
    return pl.BlockSpec(shape, lambda *_: (0,) * nd, pipeline_mode=pl.Buffered(1))


def _weight_spec(w, l):
    nd = w.ndim - 1
    return pl.BlockSpec((None,) + w.shape[1:], lambda *_: (l,) + (0,) * nd, pipeline_mode=pl.Buffered(1))


def _rms(x, g):
    return x * lax.rsqrt(jnp.mean(x * x, axis=-1, keepdims=True) + EPS) * g


def _sigmoid(x):
    return 0.5 * jnp.tanh(0.5 * x) + 0.5


def _swish(x):
    h = 0.5 * x
    return h + h * jnp.tanh(h)


def _log_sigmoid(x):
    return jnp.minimum(x, 0.0) - jnp.log1p(jnp.exp(-jnp.abs(x)))


def _dot(a, b):
    return jnp.dot(a, b, preferred_element_type=F32)


def _dot_nt(a, b):
    return lax.dot_general(a, b, (((1,), (1,)), ((), ())), preferred_element_type=F32)


def _dot_tn(a, b):
    return lax.dot_general(a, b, (((0,), (0,)), ((), ())), preferred_element_type=F32)


IN_PROJ_ROWS = 512
IN_PROJ_COLS = 512
IN_GATES = 4 * M_WIDTH
W_PREP_ROWS = 256


def _in_proj_kernel(x_ref, g_ref, wt_ref, o_ref, wprep_ref, w_scr):
    d = x_ref.shape[1]

    @pl.when(pl.program_id(0) == 0)
    def _():
        n_gate = 2 * M_HEADS
        for src, dst, n in ((0, 0, IN_GATES), (IN_GATES + n_gate, IN_GATES, Z_G - IN_GATES)):
            for c in range(0, n, W_PREP_ROWS):
                w_scr[:, dst + c:dst + c + W_PREP_ROWS] = wt_ref[src + c:src + c + W_PREP_ROWS, :].T.astype(BF16)
        gate_rows = jnp.concatenate([wt_ref[IN_GATES:IN_GATES + n_gate, :], jnp.zeros((LANES - n_gate, d), F32)],
                                    axis=0)
        w_scr[:, Z_G:Z_G + LANES] = gate_rows.T.astype(BF16)
        wprep_ref[...] = w_scr[...]

    tm, n = o_ref.shape
    sub = min(IN_PROJ_ROWS, tm)
    for r0 in range(0, tm, sub):
        h = _rms(x_ref[r0:r0 + sub, :], g_ref[...]).astype(BF16)
        for c in range(0, n, IN_PROJ_COLS):
            w = min(IN_PROJ_COLS, n - c)
            o_ref[r0:r0 + sub, c:c + w] = _dot(h, w_scr[:, c:c + w])


def _in_proj(l, x, g, wt, *, tm):
    m, d = x.shape
    tm = min(tm, m)
    return pl.pallas_call(
        _in_proj_kernel,
        grid=(m // tm,),
        in_specs=[pl.BlockSpec((tm, d), lambda i: (i, 0)), _weight_spec(g, l), _weight_spec(wt, l)],
        out_specs=[pl.BlockSpec((tm, Z_WIDTH), lambda i: (i, 0)), pl.BlockSpec((d, Z_WIDTH), lambda i: (0, 0))],
        out_shape=[jax.ShapeDtypeStruct((m, Z_WIDTH), F32), jax.ShapeDtypeStruct((d, Z_WIDTH), BF16)],
        scratch_shapes=[pltpu.VMEM((d, Z_WIDTH), BF16)],
        compiler_params=_params("arbitrary"),
        name="in_proj",
    )(x, g, wt)


def _mem_kv_kernel(x_ref, w_ref, o_ref, acc):
    acc[...] = _dot(x_ref[...].astype(BF16), w_ref[...])
    o_ref[...] = acc[...].T


def _mem_kv(l, mem3, w):
    b, mem, d = mem3.shape
    n = w.shape[2]
    return pl.pallas_call(
        _mem_kv_kernel,
        grid=(b,),
        in_specs=[pl.BlockSpec((None, mem, d), lambda i: (i, 0, 0)), _weight_spec(w, l)],
        out_specs=pl.BlockSpec((None, n, mem), lambda i: (i, 0, 0)),
        out_shape=jax.ShapeDtypeStruct((b, n, mem), F32),
        scratch_shapes=[pltpu.VMEM((mem, n), F32)],
        compiler_params=_params("arbitrary"),
        name="mem_kv",
    )(mem3, w)


def _sample_in_kernel(x_ref, g_ref, w_ref, z_ref, zt_ref):
    h = _rms(x_ref[...], g_ref[...]).astype(BF16)
    n = z_ref.shape[1]
    for c in range(0, n, IN_PROJ_COLS):
        w = min(IN_PROJ_COLS, n - c)
        zc = _dot(h, w_ref[:, c:c + w])
        z_ref[:, c:c + w] = zc
        zt_ref[c:c + w, :] = zc.T


def _sample_in(l, x, g, w):
    m, d = x.shape
    n = w.shape[1]
    full = lambda shape: pl.BlockSpec(shape, lambda i: (0,) * len(shape))
    return pl.pallas_call(
        _sample_in_kernel,
        grid=(1,),
        in_specs=[full((m, d)), _weight_spec(g, l), full((d, n))],
        out_specs=[full((m, n)), full((n, m))],
        out_shape=[jax.ShapeDtypeStruct((m, n), F32), jax.ShapeDtypeStruct((n, m), F32)],
        compiler_params=_params("arbitrary"),
        name="sample_in",
    )(x, g, w)


def _mlstm_prompt_kernel(z_ref, g_ref, gb_ref, ng_ref, h_ref, cp_ref, np_ref, m_ref,
                         cp_scr, np_scr, m_scr, ht_scr, st_scr, kw_scr):
    NB, L = z_ref.shape[0], z_ref.shape[1]
    D = HEAD_DIM
    hi = lax.Precision.HIGHEST

    @pl.when(pl.program_id(0) == 0)
    def _():
        cp_scr[...] = jnp.zeros(cp_scr.shape, F32)
        np_scr[...] = jnp.zeros(np_scr.shape, F32)
        m_scr[...] = jnp.zeros(m_scr.shape, F32)

    src = lax.broadcasted_iota(jnp.int32, (L, L), 0)
    qry = lax.broadcasted_iota(jnp.int32, (L, L), 1)
    causal_t = src <= qry
    upper = jnp.where(causal_t, 1.0, 0.0)
    lane_half = lax.broadcasted_iota(jnp.int32, (L, 128), 1) // D
    row8 = lax.broadcasted_iota(jnp.int32, (SUBLANES, 128), 0)

    rows, cols = {}, []
    for b in range(NB):
        g_t = (g_ref[b] + gb_ref[...]).T[0:SUBLANES, :]
        b_rows = jnp.dot(_log_sigmoid(g_t), upper, precision=hi, preferred_element_type=F32)
        to_cols = []
        for h in range(M_HEADS):
            b_row = b_rows[M_HEADS + h:M_HEADS + h + 1, :]
            ci_row = g_t[h:h + 1, :] - b_row
            m_prev = m_scr[b, h, 0:1, 0:1]
            b_last = b_row[:, L - 1:L]
            m_new = jnp.maximum(b_last + m_prev, jnp.max(b_last + ci_row, axis=1, keepdims=True))
            rows[b, h] = dict(b_row=b_row, m_prev=m_prev, m_new=m_new, decay=jnp.exp(b_last + m_prev - m_new))
            to_cols += [ci_row, jnp.exp(b_last + ci_row - m_new)]
        to_cols.append(jnp.zeros((128 - 2 * M_HEADS, L), F32))
        cols.append(jnp.concatenate(to_cols, axis=0).T)

    stats = {}
    for b in range(NB):
        for j in range(M_HEADS // 2):
            slab = 128 * j
            q2 = z_ref[b, :, slab:slab + 128].astype(BF16)
            k2 = z_ref[b, :, M_WIDTH + slab:M_WIDTH + slab + 128] * QK_SCALE
            zero = jnp.zeros_like(q2)
            q_own = jnp.concatenate([jnp.where(lane_half == 0, q2, zero), jnp.where(lane_half == 1, q2, zero)], axis=0)
            qk = _dot_nt(k2.astype(BF16), q_own)
            for par in range(2):
                h = 2 * j + par
                r = rows[b, h]
                ci_col = cols[b][:, 2 * h:2 * h + 1]
                a_row = r["b_row"] + r["m_prev"]
                dm = jnp.where(causal_t, r["b_row"] + ci_col, NEG_INF)
                m_row = jnp.maximum(a_row, jnp.max(dm, axis=0, keepdims=True))
                s_t = qk[:, L * par:L * (par + 1)] * jnp.exp(dm - m_row)
                st_scr[b, j, :, L * par:L * (par + 1)] = s_t.astype(BF16)
                stats[b, h] = dict(w_inter=jnp.exp(a_row - m_row), floor=jnp.exp(-m_row),
                                   den_s=jnp.sum(s_t, axis=0, keepdims=True),
                                   decay=r["decay"], m_new=r["m_new"])
            wk = jnp.where(lane_half == 0, cols[b][:, 4 * j + 1:4 * j + 2], cols[b][:, 4 * j + 3:4 * j + 4])
            kw = k2 * wk
            kw_scr[b, j] = kw.astype(BF16)
            stats[b, j, "k_sum"] = jnp.sum(kw, axis=0, keepdims=True)

    block_diag = (lax.broadcasted_iota(jnp.int32, (128, 128), 0) // D
                  == lax.broadcasted_iota(jnp.int32, (128, 128), 1) // D)
    for b in range(NB):
        for j in range(M_HEADS // 2):
            slab = 128 * j
            q2 = z_ref[b, :, slab:slab + 128].astype(BF16)
            v2 = z_ref[b, :, 2 * M_WIDTH + slab:2 * M_WIDTH + slab + 128].astype(BF16)
            cp = cp_scr[b, j]
            npair = np_scr[b, j]
            qc = _dot_nt(cp.astype(BF16), q2)
            qn = _dot_nt(npair.astype(BF16), q2)
            pv = _dot_tn(v2, st_scr[b, j])
            upd = jnp.where(block_diag, _dot_tn(v2, kw_scr[b, j]), 0.0)
            n_new = jnp.zeros((SUBLANES, 128), F32)
            for par in range(2):
                h = 2 * j + par
                st = stats[b, h]
                hr = slice(D * par, D * (par + 1))
                num = st["w_inter"] * qc[hr, :] + pv[hr, L * par:L * (par + 1)]
                den = st["w_inter"] * qn[par:par + 1, :] + st["den_s"]
                hh = num * (1.0 / jnp.maximum(jnp.abs(den), st["floor"]))
                hh = hh * lax.rsqrt(jnp.mean(hh * hh, axis=0, keepdims=True) + EPS)
                ht_scr[b, D * h:D * (h + 1), :] = hh
                cp_scr[b, j, hr, :] = st["decay"] * cp[hr, :] + upd[hr, :]
                k_sum = jnp.where(lane_half[0:1, :] == par, stats[b, j, "k_sum"], 0.0)
                n_new = jnp.where(row8 == par, st["decay"] * npair[par:par + 1, :] + k_sum, n_new)
                m_scr[b, h] = jnp.broadcast_to(st["m_new"], m_scr.shape[2:])
            np_scr[b, j] = n_new
        o_gate = _sigmoid(z_ref[b, :, 3 * M_WIDTH:4 * M_WIDTH])
        h_ref[b] = ht_scr[b].T * ng_ref[...] * o_gate

    cp_ref[...] = cp_scr[...]
    np_ref[...] = np_scr[...]
    m_ref[...] = m_scr[...]


def _mlstm_prompt(l, z3, gbias, ng):
    b, s, _ = z3.shape
    L = M_CHUNK
    P = M_HEADS // 2
    return pl.pallas_call(
        _mlstm_prompt_kernel,
        grid=(s // L,),
        in_specs=[
            pl.BlockSpec((b, L, 4 * M_WIDTH), lambda c: (0, c, Z_M // (4 * M_WIDTH))),
            pl.BlockSpec((b, L, 128), lambda c: (0, c, Z_G // 128)),
            _weight_spec(gbias, l),
            _weight_spec(ng, l),
        ],
        out_specs=[
            pl.BlockSpec((b, L, M_WIDTH), lambda c: (0, c, 0)),
            pl.BlockSpec((b, P, 128, 128), lambda c: (0, 0, 0, 0)),
            pl.BlockSpec((b, P, SUBLANES, 128), lambda c: (0, 0, 0, 0)),
            pl.BlockSpec((b, M_HEADS, SUBLANES, 128), lambda c: (0, 0, 0, 0)),
        ],
        out_shape=[
            jax.ShapeDtypeStruct((b, s, M_WIDTH), F32),
            jax.ShapeDtypeStruct((b, P, 128, 128), F32),
            jax.ShapeDtypeStruct((b, P, SUBLANES, 128), F32),
            jax.ShapeDtypeStruct((b, M_HEADS, SUBLANES, 128), F32),
        ],
        scratch_shapes=[
            pltpu.VMEM((b, P, 128, 128), F32),
            pltpu.VMEM((b, P, SUBLANES, 128), F32),
            pltpu.VMEM((b, M_HEADS, SUBLANES, 128), F32),
            pltpu.VMEM((b, M_WIDTH, L), F32),
            pltpu.VMEM((b, P, L, 2 * L), BF16),
            pltpu.VMEM((b, P, L, 128), BF16),
        ],
        compiler_params=_params("arbitrary"),
        name="mlstm_prompt",
    )(z3, z3, gbias, ng)


CONV_PAD = 32


def _conv_prompt_kernel(z_ref, w_ref, cb_ref, lg_ref, lb_ref, h_ref, tail_ref, buf, shifted):
    tc = z_ref.shape[0]
    t = pl.program_id(1)

    @pl.when(t == 0)
    def _():
        buf[0:CONV_PAD, :] = jnp.zeros((CONV_PAD, C_WIDTH), F32)

    @pl.when(t > 0)
    def _():
        buf[0:CONV_PAD, :] = buf[tc:tc + CONV_PAD, :]

    u = z_ref[:, 0:C_WIDTH] * _sigmoid(z_ref[:, C_WIDTH:2 * C_WIDTH])
    buf[CONV_PAD:CONV_PAD + tc, :] = u
    n_sh = shifted.shape[1]
    for r in range(1, SUBLANES):
        shifted[r - 1] = buf[r:r + n_sh, :]
    off = CONV_PAD - (C_KERNEL - 1)
    acc = jnp.broadcast_to(cb_ref[...], (tc, C_WIDTH))
    for j in range(C_KERNEL):
        r = (off + j) % SUBLANES
        base = off + j - r
        win = buf[base:base + tc, :] if r == 0 else shifted[r - 1, base:base + tc, :]
        acc = acc + w_ref[j:j + 1, :] * win
    mu = jnp.mean(acc, axis=-1, keepdims=True)
    xc = acc - mu
    y = xc * lax.rsqrt(jnp.mean(xc * xc, axis=-1, keepdims=True) + EPS) * lg_ref[...] + lb_ref[...]
    h_ref[...] = _swish(y)
    tail_ref[...] = buf[tc:tc + CONV_PAD, :]


def _conv_prompt(l, z3, w, cb, lg, lb, *, tc):
    b, s, _ = z3.shape
    return pl.pallas_call(
        _conv_prompt_kernel,
        grid=(b, s // tc),
        in_specs=[
            pl.BlockSpec((None, tc, 2 * C_WIDTH), lambda i, t: (i, t, Z_C // (2 * C_WIDTH))),
            _weight_spec(w, l), _weight_spec(cb, l), _weight_spec(lg, l), _weight_spec(lb, l),
        ],
        out_specs=[
            pl.BlockSpec((None, tc, C_WIDTH), lambda i, t: (i, t, 0)),
            pl.BlockSpec((None, CONV_PAD, C_WIDTH), lambda i, t: (i, 0, 0)),
        ],
        out_shape=[
            jax.ShapeDtypeStruct((b, s, C_WIDTH), F32),
            jax.ShapeDtypeStruct((b, CONV_PAD, C_WIDTH), F32),
        ],
        scratch_shapes=[pltpu.VMEM((CONV_PAD + tc, C_WIDTH), F32),
                        pltpu.VMEM((SUBLANES - 1, CONV_PAD + tc - SUBLANES, C_WIDTH), F32)],
        compiler_params=_params("arbitrary", "arbitrary"),
        name="conv_prompt",
    )(z3, w, cb, lg, lb)


def _swa_prompt_kernel(rb_ref, sink_ref, bucket_ref, q_ref, kv_ref, kvp_ref, o_ref, kt_ref, vt_ref,
                       bias_scr, s_scr, p_scr, ot_scr, *, layer):
    W = WINDOW
    H = S_Q_HEADS
    G = H // S_KV_HEADS
    i = pl.program_id(0)
    n = pl.program_id(1)

    @pl.when(jnp.logical_and(i == 0, n == 0))
    def _():
        bucket = bucket_ref[...]
        prev_key = lax.broadcasted_iota(jnp.int32, (2 * W, W), 0) < W
        for h in range(H):
            acc = jnp.full((2 * W, W), NEG_INF, F32)
            for b in range(N_BUCKETS):
                acc = jnp.where(bucket == b, rb_ref[b, h], acc)
            acc = acc * LOG2E
            bias_scr[0, h] = acc
            bias_scr[1, h] = jnp.where(prev_key, NEG_INF, acc)

    QB = q_ref.shape[0] // W
    NK = (QB + 1) * W
    first = jnp.where(n == 0, 1, 0)
    kk = jnp.concatenate([kvp_ref[:, 0:S_KV_WIDTH], kv_ref[:, 0:S_KV_WIDTH]], axis=0) * (QK_SCALE * LOG2E)
    vv = jnp.concatenate([kvp_ref[:, S_KV_WIDTH:2 * S_KV_WIDTH], kv_ref[:, S_KV_WIDTH:2 * S_KV_WIDTH]], axis=0)
    kk_r = pltpu.roll(kk, HEAD_DIM, axis=1)
    lo_lane = lax.broadcasted_iota(jnp.int32, (NK, S_KV_WIDTH), 1) < HEAD_DIM
    k_var = [[jnp.where(lo_lane, kk, 0.0).astype(BF16), jnp.where(lo_lane, 0.0, kk_r).astype(BF16)],
             [jnp.where(lo_lane, kk_r, 0.0).astype(BF16), jnp.where(lo_lane, 0.0, kk).astype(BF16)]]
    v_t = vv.T.astype(BF16)

    for j in range(QB):
        k0 = j * W
        masked = first if j == 0 else 0
        m_rows = [None] * H
        for hk in range(S_KV_HEADS):
            c0 = 2 * 128 * hk
            q_st = jnp.concatenate([q_ref[k0:k0 + W, c0:c0 + 128], q_ref[k0:k0 + W, c0 + 128:c0 + 256]],
                                   axis=0).astype(BF16)
            for half in range(2):
                s_t = _dot_nt(k_var[hk][half][k0:k0 + 2 * W, :], q_st)
                for slab in range(2):
                    head = G * hk + 2 * slab + half
                    sb = s_t[:, 128 * slab:128 * (slab + 1)] + bias_scr[masked, head]
                    s_scr[j, head] = sb
                    m_rows[head] = jnp.maximum(jnp.max(sb, axis=0, keepdims=True), sink_ref[layer, head] * LOG2E)

        inv = [None] * H
        for head in range(H):
            e = jnp.exp2(s_scr[j, head] - m_rows[head])
            den = jnp.sum(e, axis=0, keepdims=True) + jnp.exp2(sink_ref[layer, head] * LOG2E - m_rows[head])
            inv[head] = 1.0 / den
            p_scr[j, :, 128 * head:128 * (head + 1)] = e.astype(BF16)

        for hk in range(S_KV_HEADS):
            o_t = _dot(v_t[HEAD_DIM * hk:HEAD_DIM * (hk + 1), k0:k0 + 2 * W],
                       p_scr[j, :, 128 * G * hk:128 * G * (hk + 1)])
            for g in range(G):
                head = G * hk + g
                ot_scr[j, HEAD_DIM * head:HEAD_DIM * (head + 1), :] = o_t[:, 128 * g:128 * (g + 1)] * inv[head]
        o_ref[k0:k0 + W, :] = ot_scr[j].T

    @pl.when(n == pl.num_programs(1) - 1)
    def _():
        kt_ref[...] = kv_ref[(QB - 1) * W:QB * W, 0:S_KV_WIDTH].T
        vt_ref[...] = kv_ref[(QB - 1) * W:QB * W, S_KV_WIDTH:2 * S_KV_WIDTH].T


def _swa_prompt(l, z3, rel_bias, sinks, bucket, *, qb):
    b, s, _ = z3.shape
    W = WINDOW
    smem = pl.BlockSpec(memory_space=pltpu.SMEM)
    return pl.pallas_call(
        functools.partial(_swa_prompt_kernel, layer=l),
        grid=(b, s // (qb * W)),
        in_specs=[
            smem, smem, _const_spec((2 * W, W)),
            pl.BlockSpec((None, qb * W, S_WIDTH), lambda i, n: (i, n, Z_SQ // S_WIDTH)),
            pl.BlockSpec((None, qb * W, 2 * S_KV_WIDTH), lambda i, n: (i, n, Z_SKV // (2 * S_KV_WIDTH))),
            pl.BlockSpec((None, W, 2 * S_KV_WIDTH),
                         lambda i, n: (i, jnp.maximum(n * qb - 1, 0), Z_SKV // (2 * S_KV_WIDTH))),
        ],
        out_specs=[
            pl.BlockSpec((None, qb * W, S_WIDTH), lambda i, n: (i, n, 0)),
            pl.BlockSpec((None, S_KV_WIDTH, W), lambda i, n: (i, 0, 0)),
            pl.BlockSpec((None, S_KV_WIDTH, W), lambda i, n: (i, 0, 0)),
        ],
        out_shape=[
            jax.ShapeDtypeStruct((b, s, S_WIDTH), F32),
            jax.ShapeDtypeStruct((b, S_KV_WIDTH, W), F32),
            jax.ShapeDtypeStruct((b, S_KV_WIDTH, W), F32),
        ],
        scratch_shapes=[
            pltpu.VMEM((2, S_Q_HEADS, 2 * W, W), F32),
            pltpu.VMEM((qb, S_Q_HEADS, 2 * W, W), F32),
            pltpu.VMEM((qb, 2 * W, S_Q_HEADS * W), BF16),
            pltpu.VMEM((qb, S_WIDTH, W), F32),
        ],
        compiler_params=_params("arbitrary", "arbitrary"),
        name="swa_prompt",
    )(rel_bias, sinks, bucket, z3, z3, z3)


MIX_SUB = 512


def _mix_out_prompt_kernel(x_ref, hm_ref, hc_ref, hs_ref, wout_ref, g2_ref, wq_ref, mk_ref, mv_ref, wo_ref, o_ref,
                           x1_scr, qx_scr, p_scr):
    tm = x_ref.shape[0]
    mem = mk_ref.shape[1]
    mk_t = mk_ref[...]
    mv_t = mv_ref[...]
    row_head = lax.broadcasted_iota(jnp.int32, mk_t.shape, 0) // HEAD_DIM
    k_heads = [jnp.where(row_head == h, mk_t, 0.0).astype(BF16) for h in range(X_HEADS)]
    v_cat = jnp.concatenate([jnp.where(row_head == h, mv_t, 0.0).astype(BF16) for h in range(X_HEADS)], axis=1)
    subs = [slice(r0, r0 + MIX_SUB) for r0 in range(0, tm, MIX_SUB)]
    for rows in subs:
        cat = jnp.concatenate([hm_ref[rows, :].astype(BF16), hc_ref[rows, :].astype(BF16),
                               hs_ref[rows, :].astype(BF16)], axis=1)
        x1 = x_ref[rows, :] + _dot(cat, wout_ref[...])
        x1_scr[rows, :] = x1
        qx = _dot(_rms(x1, g2_ref[...]).astype(BF16), wq_ref[...])
        qx_scr[rows, :] = (qx * (QK_SCALE * LOG2E)).astype(BF16)
    for rows in subs:
        qx = qx_scr[rows, :]
        for h in range(X_HEADS):
            s = _dot(qx, k_heads[h])
            e = jnp.exp2(s - jnp.max(s, axis=1, keepdims=True))
            p_scr[rows, mem * h:mem * (h + 1)] = (e * (1.0 / jnp.sum(e, axis=1, keepdims=True))).astype(BF16)
    for rows in subs:
        o = _dot_nt(p_scr[rows, :], v_cat)
        o_ref[rows, :] = x1_scr[rows, :] + _dot(o.astype(BF16), wo_ref[...])


def _mix_out_prompt(l, x3, hm, hc, hs, wout, g2, wq, mkv, wo, *, tm):
    b, s, d = x3.shape
    tm = min(tm, s)
    mem = mkv.shape[2]
    row = lambda w: pl.BlockSpec((None, tm, w), lambda i, t: (i, t, 0))
    return pl.pallas_call(
        _mix_out_prompt_kernel,
        grid=(b, s // tm),
        in_specs=[
            row(d), row(M_WIDTH), row(C_WIDTH), row(S_WIDTH),
            _weight_spec(wout, l), _weight_spec(g2, l), _weight_spec(wq, l),
            pl.BlockSpec((None, X_WIDTH, mem), lambda i, t: (i, 0, 0)),
            pl.BlockSpec((None, X_WIDTH, mem), lambda i, t: (i, 1, 0)),
            _weight_spec(wo, l),
        ],
        out_specs=row(d),
        out_shape=jax.ShapeDtypeStruct((b, s, d), F32),
        scratch_shapes=[pltpu.VMEM((tm, d), F32), pltpu.VMEM((tm, X_WIDTH), BF16),
                        pltpu.VMEM((tm, X_HEADS * mem), BF16)],
        compiler_params=_params("arbitrary", "arbitrary"),
        name="mix_out_prompt",
    )(x3, hm, hc, hs, wout, g2, wq, mkv, mkv, wo)


FFN_PAD = 8


def _ffn_prompt_kernel(x_ref, g3_ref, wup_ref, cw_ref, cb_ref, wdn_ref, gf_ref, o_ref, tail_ref,
                       carry, act, *, final):
    tm = x_ref.shape[0]
    dff = wdn_ref.shape[0]

    @pl.when(pl.program_id(1) == 0)
    def _():
        carry[...] = jnp.zeros(carry.shape, F32)

    x = x_ref[...]
    h = _rms(x, g3_ref[...]).astype(BF16)
    w = FFN_CHUNK
    row8 = lax.broadcasted_iota(jnp.int32, (FFN_PAD, w), 0)
    for c in range(0, dff, w):
        a = _dot(h, wup_ref[:, c:c + w])
        g = _dot(h, wup_ref[:, dff + c:dff + c + w])
        prev = carry[:, c:c + w]
        carry[:, c:c + w] = g[tm - FFN_PAD:tm, :]
        taps = []
        for s in (2, 1):
            gs = pltpu.roll(g, s, axis=0)
            head = jnp.where(row8 < s, pltpu.roll(prev, s, axis=0), gs[0:FFN_PAD, :])
            taps.append(jnp.concatenate([head, gs[FFN_PAD:, :]], axis=0))
        gc = (cw_ref[0:1, c:c + w] * taps[0] + cw_ref[1:2, c:c + w] * taps[1]
              + cw_ref[2:3, c:c + w] * g + cb_ref[:, c:c + w])
        act[:, c:c + w] = (_swish(gc) * a).astype(BF16)
    y = x + _dot(act[...], wdn_ref[...])
    o_ref[...] = _rms(y, gf_ref[...]) if final else y
    tail_ref[...] = carry[...]


def _ffn_prompt(l, x3, g3, wup, cw, cb, wdn, gf, *, tm, final):
    b, s, d = x3.shape
    tm = min(tm, s)
    dff = wdn.shape[1]
    return pl.pallas_call(
        functools.partial(_ffn_prompt_kernel, final=final),
        grid=(b, s // tm),
        in_specs=[
            pl.BlockSpec((None, tm, d), lambda i, t: (i, t, 0)),
            _weight_spec(g3, l), _weight_spec(wup, l), _weight_spec(cw, l), _weight_spec(cb, l),
            _weight_spec(wdn, l), _const_spec((1, d)),
        ],
        out_specs=[
            pl.BlockSpec((None, tm, d), lambda i, t: (i, t, 0)),
            pl.BlockSpec((None, FFN_PAD, dff), lambda i, t: (i, 0, 0)),
        ],
        out_shape=[jax.ShapeDtypeStruct((b, s, d), F32), jax.ShapeDtypeStruct((b, FFN_PAD, dff), F32)],
        scratch_shapes=[
            pltpu.VMEM((FFN_PAD, dff), F32),
            pltpu.VMEM((tm, dff), BF16),
        ],
        compiler_params=_params("arbitrary", "arbitrary"),
        name="ffn_prompt",
    )(x3, g3, wup, cw, cb, wdn, gf)


SWA_SAMPLE_BLOCK = 32
XATTN_SAMPLE_BLOCK = 16
CONV_SAMPLE_BLOCK = 32


def _mlstm_sample_kernel(gb_ref, q_ref, k_ref, v_ref, o_ref, g_ref, ng_ref, c_ref, n_ref, m_ref, *rest, layer):
    h_ref, co_ref, no_ref, mo_ref, kw_scr = rest[-5:]
    _copy_earlier_layers(rest[:-5], (co_ref, no_ref, mo_ref))
    h = pl.program_id(0)
    i_pre = g_ref[pl.ds(h, 1), :] + gb_ref[layer, h]
    f_pre = g_ref[pl.ds(M_HEADS + h, 1), :] + gb_ref[layer, M_HEADS + h]
    a = _log_sigmoid(f_pre) + m_ref[pl.ds(h, 1), :]
    m_t = jnp.maximum(a, i_pre)
    w_old = jnp.exp(a - m_t)
    w_new = jnp.exp(i_pre - m_t)
    q = q_ref[...]
    k = k_ref[...] * QK_SCALE
    v = v_ref[...]
    n_old = n_ref[...]
    kw_scr[...] = k * w_new

    def body(d, acc):
        c_old = c_ref[d]
        co_ref[layer, d] = w_old * c_old + kw_scr[pl.ds(d, 1), :] * v
        return acc + q_ref[pl.ds(d, 1), :] * c_old

    qc = lax.fori_loop(0, HEAD_DIM, body, jnp.zeros(v.shape, F32), unroll=8)
    s = jnp.sum(q * k, axis=0, keepdims=True) * w_new
    num = w_old * qc + s * v
    den = w_old * jnp.sum(q * n_old, axis=0, keepdims=True) + s
    hh = num / jnp.maximum(jnp.abs(den), jnp.exp(-m_t))
    hh = hh * lax.rsqrt(jnp.mean(hh * hh, axis=0, keepdims=True) + EPS) * ng_ref[...]
    h_ref[...] = hh * _sigmoid(o_ref[...])
    no_ref[layer] = w_old * n_old + kw_scr[...]
    mo_ref[layer, pl.ds(h, 1), :] = m_t


def _mlstm_sample(l, zt, gb, ngt, c_all, n_all, m_all, prev):
    bsz = zt.shape[1]
    D = HEAD_DIM
    feat = lambda off: pl.BlockSpec((D, bsz), lambda h: (off // D + h, 0))
    c_spec = lambda n: pl.BlockSpec((n, None, D, D, bsz), lambda h: (0, h, 0, 0, 0))
    n_spec = lambda n: pl.BlockSpec((n, None, D, bsz), lambda h: (0, h, 0, 0))
    m_spec = lambda n: pl.BlockSpec((n, M_HEADS, bsz), lambda h: (0, 0, 0))
    prev_specs = [c_spec(l), n_spec(l), m_spec(l)] if prev else []
    return pl.pallas_call(
        functools.partial(_mlstm_sample_kernel, layer=l),
        grid=(M_HEADS,),
        in_specs=[
            pl.BlockSpec(memory_space=pltpu.SMEM),
            feat(Z_M), feat(Z_M + M_WIDTH), feat(Z_M + 2 * M_WIDTH), feat(Z_M + 3 * M_WIDTH),
            pl.BlockSpec((SUBLANES, bsz), lambda h: (Z_G // SUBLANES, 0)),
            pl.BlockSpec((None, D, bsz), lambda h: (l, h, 0)),
            pl.BlockSpec((None, None, D, D, bsz), lambda h: (l, h, 0, 0, 0)),
            pl.BlockSpec((None, None, D, bsz), lambda h: (l, h, 0, 0)),
            pl.BlockSpec((None, M_HEADS, bsz), lambda h: (l, 0, 0)),
        ] + prev_specs,
        out_specs=[pl.BlockSpec((D, bsz), lambda h: (h, 0)), c_spec(l + 1), n_spec(l + 1), m_spec(l + 1)],
        out_shape=[
            jax.ShapeDtypeStruct((M_WIDTH, bsz), F32),
            jax.ShapeDtypeStruct((l + 1, M_HEADS, D, D, bsz), F32),
            jax.ShapeDtypeStruct((l + 1, M_HEADS, D, bsz), F32),
            jax.ShapeDtypeStruct((l + 1, M_HEADS, bsz), F32),
        ],
        scratch_shapes=[pltpu.VMEM((D, bsz), F32)],
        compiler_params=_params("arbitrary"),
        name="mlstm_sample",
    )(gb, zt, zt, zt, zt, zt, ngt, c_all, n_all, m_all, *prev)


def _conv_sample_kernel(z_ref, hist_ref, w_ref, cb_ref, lg_ref, lb_ref, *rest):
    h_ref, hist_out_ref = rest[-2:]
    _copy_earlier_layers(rest[:-2], (hist_out_ref,))
    l = hist_out_ref.shape[0] - 1
    nh = C_KERNEL - 1
    u = z_ref[:, 0:C_WIDTH] * _sigmoid(z_ref[:, C_WIDTH:2 * C_WIDTH])
    acc = cb_ref[...] + w_ref[nh:nh + 1, :] * u
    for j in range(nh):
        acc = acc + w_ref[j:j + 1, :] * hist_ref[j]
    mu = jnp.mean(acc, axis=-1, keepdims=True)
    xc = acc - mu
    y = xc * lax.rsqrt(jnp.mean(xc * xc, axis=-1, keepdims=True) + EPS) * lg_ref[...] + lb_ref[...]
    h_ref[...] = _swish(y)
    for j in range(nh - 1):
        hist_out_ref[l, j] = hist_ref[j + 1]
    hist_out_ref[l, nh - 1] = u


def _conv_sample(l, z, hist_all, w, cb, lg, lb, prev):
    bsz = z.shape[0]
    nh = hist_all.shape[1]
    R = min(CONV_SAMPLE_BLOCK, bsz)
    hist_spec = lambda n: pl.BlockSpec((n, nh, R, C_WIDTH), lambda i: (0, 0, i, 0))
    return pl.pallas_call(
        _conv_sample_kernel,
        grid=(bsz // R,),
        in_specs=[
            pl.BlockSpec((R, 2 * C_WIDTH), lambda i: (i, Z_C // (2 * C_WIDTH))),
            pl.BlockSpec((None, nh, R, C_WIDTH), lambda i: (l, 0, i, 0)),
            _weight_spec(w, l), _weight_spec(cb, l), _weight_spec(lg, l), _weight_spec(lb, l),
        ] + [hist_spec(l)] * len(prev),
        out_specs=[pl.BlockSpec((R, C_WIDTH), lambda i: (i, 0)), hist_spec(l + 1)],
        out_shape=[jax.ShapeDtypeStruct((bsz, C_WIDTH), F32), jax.ShapeDtypeStruct((l + 1, nh, bsz, C_WIDTH), F32)],
        compiler_params=_params("arbitrary"),
        name="conv_sample",
    )(z, hist_all, w, cb, lg, lb, *prev)


def _copy_earlier_layers(prev_refs, out_refs):
    for p_ref, o_ref in zip(prev_refs, out_refs):
        o_ref[0:p_ref.shape[0]] = p_ref[...]


def _swa_sample_kernel(q_ref, kvn_ref, kv2_ref, kc_ref, vc_ref, bias_ref, aux_ref, *rest):
    o_ref, ko_ref, vo_ref = rest[-3:]
    _copy_earlier_layers(rest[:-3], (ko_ref, vo_ref))
    l = ko_ref.shape[0] - 1
    R = q_ref.shape[0]
    W = kc_ref.shape[2]
    H = S_Q_HEADS
    shape = (R, H, 128)
    row = lax.broadcasted_iota(jnp.int32, shape, 1)
    lane_half = lax.broadcasted_iota(jnp.int32, shape, 2) // HEAD_DIM
    q_half = row % 2
    kv_head = row // (H // S_KV_HEADS)
    qs = jnp.zeros(shape, F32)
    for j in range(H // 2):
        qs = jnp.where(row // 2 == j, q_ref[:, :, 128 * j:128 * (j + 1)], qs)
    q8 = jnp.where(lane_half == kv_head, jnp.where(q_half == kv_head, qs, pltpu.roll(qs, HEAD_DIM, axis=2)), 0.0)
    k_new = kvn_ref[:, :, 0:S_KV_WIDTH]
    v_new = kvn_ref[:, :, S_KV_WIDTH:2 * S_KV_WIDTH]
    s = jnp.einsum("bqd,bdk->bqk", q8.astype(BF16), kc_ref[...].astype(BF16), preferred_element_type=F32) * QK_SCALE
    s = s + bias_ref[...][None]
    s_new = jnp.sum(q8 * k_new, axis=2, keepdims=True) * QK_SCALE + aux_ref[:, 0:1][None]
    sink = aux_ref[:, 1:2][None]
    m = jnp.maximum(jnp.maximum(jnp.max(s, axis=2, keepdims=True), s_new), sink)
    e = jnp.exp(s - m)
    e_new = jnp.exp(s_new - m)
    inv = 1.0 / (jnp.sum(e, axis=2, keepdims=True) + e_new + jnp.exp(sink - m))
    o8 = jnp.einsum("bqk,bdk->bqd", e.astype(BF16), vc_ref[...].astype(BF16), preferred_element_type=F32)
    o8 = (o8 + e_new * v_new) * inv
    o8 = jnp.where(lane_half == q_half, jnp.where(q_half == kv_head, o8, pltpu.roll(o8, HEAD_DIM, axis=2)), 0.0)
    for j in range(H // 2):
        o_ref[:, :, 128 * j:128 * (j + 1)] = jnp.sum(jnp.where(row // 2 == j, o8, 0.0), axis=1, keepdims=True)
    k_cols = kv2_ref[:, 0:S_KV_WIDTH].T
    v_cols = kv2_ref[:, S_KV_WIDTH:2 * S_KV_WIDTH].T
    last = lax.broadcasted_iota(jnp.int32, (S_KV_WIDTH, W), 1) == W - 1
    for r in range(R):
        ko_ref[l, r] = jnp.where(last, k_cols[:, r:r + 1], pltpu.roll(kc_ref[r], W - 1, axis=1))
        vo_ref[l, r] = jnp.where(last, v_cols[:, r:r + 1], pltpu.roll(vc_ref[r], W - 1, axis=1))


def _swa_sample(l, z, z3, kc_all, vc_all, bias, aux, prev):
    bsz = z.shape[0]
    R = min(SWA_SAMPLE_BLOCK, bsz)
    W = kc_all.shape[3]
    cache_in = pl.BlockSpec((None, R, S_KV_WIDTH, W), lambda i: (l, i, 0, 0))
    cache_prev = pl.BlockSpec((l, R, S_KV_WIDTH, W), lambda i: (0, i, 0, 0))
    cache_out = pl.BlockSpec((l + 1, R, S_KV_WIDTH, W), lambda i: (0, i, 0, 0))
    return pl.pallas_call(
        _swa_sample_kernel,
        grid=(bsz // R,),
        in_specs=[
            pl.BlockSpec((R, 1, S_WIDTH), lambda i: (i, 0, Z_SQ // S_WIDTH)),
            pl.BlockSpec((R, 1, 2 * S_KV_WIDTH), lambda i: (i, 0, Z_SKV // (2 * S_KV_WIDTH))),
            pl.BlockSpec((R, 2 * S_KV_WIDTH), lambda i: (i, Z_SKV // (2 * S_KV_WIDTH))),
            cache_in, cache_in,
            _const_spec((S_Q_HEADS, 128)), _weight_spec(aux, l),
        ] + [cache_prev] * len(prev),
        out_specs=[pl.BlockSpec((R, 1, S_WIDTH), lambda i: (i, 0, 0)), cache_out, cache_out],
        out_shape=[
            jax.ShapeDtypeStruct((bsz, 1, S_WIDTH), F32),
            jax.ShapeDtypeStruct((l + 1, bsz, S_KV_WIDTH, W), F32),
            jax.ShapeDtypeStruct((l + 1, bsz, S_KV_WIDTH, W), F32),
        ],
        compiler_params=_params("arbitrary"),
        name="swa_sample",
    )(z3, z3, z, kc_all, vc_all, bias, aux, *prev)


def _mix_out_sample_kernel(x_ref, hmt_ref, hc_ref, hs_ref, wout_ref, g2_ref, wq_ref, x1_ref, q_ref):
    cat = jnp.concatenate([hmt_ref[...].T.astype(BF16), hc_ref[...].astype(BF16), hs_ref[...].astype(BF16)], axis=1)
    x1 = x_ref[...] + _dot(cat, wout_ref[...])
    x1_ref[...] = x1
    q_ref[...] = _dot(_rms(x1, g2_ref[...]).astype(BF16), wq_ref[...])


def _mix_out_sample(l, x, hm, hc, hs, wout, g2, wq):
    bsz, d = x.shape
    full = lambda a: pl.BlockSpec(a.shape, lambda i: (0,) * a.ndim)
    args = (x, hm, hc, hs, wout, g2, wq)
    return pl.pallas_call(
        _mix_out_sample_kernel,
        grid=(1,),
        in_specs=[full(x), full(hm), full(hc), full(hs), _weight_spec(wout, l), _weight_spec(g2, l),
                  _weight_spec(wq, l)],
        out_specs=[pl.BlockSpec((bsz, d), lambda i: (0, 0)), pl.BlockSpec((bsz, X_WIDTH), lambda i: (0, 0))],
        out_shape=[jax.ShapeDtypeStruct((bsz, d), F32), jax.ShapeDtypeStruct((bsz, X_WIDTH), F32)],
        compiler_params=_params("arbitrary"),
        name="mix_out_sample",
    )(*args)


def _xattn_sample_kernel(q_ref, k_ref, v_ref, o_ref):
    R = k_ref.shape[0]
    shape = (R, SUBLANES, X_WIDTH)
    row = lax.broadcasted_iota(jnp.int32, shape, 1)
    lane_head = lax.broadcasted_iota(jnp.int32, shape, 2) // HEAD_DIM
    own = row == lane_head
    q8 = jnp.where(own, jnp.broadcast_to(q_ref[...], shape), 0.0).astype(BF16)
    s = jnp.einsum("bqd,bdk->bqk", q8, k_ref[...].astype(BF16), preferred_element_type=F32) * QK_SCALE
    e = jnp.exp(s - jnp.max(s, axis=2, keepdims=True))
    p = (e / jnp.sum(e, axis=2, keepdims=True)).astype(BF16)
    o8 = jnp.einsum("bqk,bdk->bqd", p, v_ref[...].astype(BF16), preferred_element_type=F32)
    o_ref[...] = jnp.sum(jnp.where(own, o8, 0.0), axis=1, keepdims=True)


def _xattn_sample(l, q3, k_all, v_all):
    _, bsz, w, mem = k_all.shape
    R = min(XATTN_SAMPLE_BLOCK, bsz)
    kv = pl.BlockSpec((None, R, w, mem), lambda i: (l, i, 0, 0))
    qo = pl.BlockSpec((R, 1, w), lambda i: (i, 0, 0))
    return pl.pallas_call(
        _xattn_sample_kernel,
        grid=(bsz // R,),
        in_specs=[qo, kv, kv],
        out_specs=qo,
        out_shape=jax.ShapeDtypeStruct((bsz, 1, w), F32),
        compiler_params=_params("arbitrary"),
        name="xattn_sample",
    )(q3, k_all, v_all)


def _ffn_sample_kernel(x1_ref, ox_ref, wo_ref, g3_ref, wup_ref, cw_ref, cb_ref, wdn_ref, gf_ref, hist_ref,
                       *rest, final):
    o_ref, hist_out_ref, act = rest[-3:]
    _copy_earlier_layers(rest[:-3], (hist_out_ref,))
    l = hist_out_ref.shape[0] - 1
    dff = wdn_ref.shape[0]
    x = x1_ref[...] + _dot(ox_ref[...].astype(BF16), wo_ref[...])
    h = _rms(x, g3_ref[...]).astype(BF16)
    for c in range(0, dff, FFN_CHUNK):
        a = _dot(h, wup_ref[:, c:c + FFN_CHUNK])
        g = _dot(h, wup_ref[:, dff + c:dff + c + FFN_CHUNK])
        h1 = hist_ref[:, 1, c:c + FFN_CHUNK]
        gc = (cw_ref[0:1, c:c + FFN_CHUNK] * hist_ref[:, 0, c:c + FFN_CHUNK]
              + cw_ref[1:2, c:c + FFN_CHUNK] * h1
              + cw_ref[2:3, c:c + FFN_CHUNK] * g + cb_ref[:, c:c + FFN_CHUNK])
        act[:, c:c + FFN_CHUNK] = (_swish(gc) * a).astype(BF16)
        hist_out_ref[l, :, 0, c:c + FFN_CHUNK] = h1
        hist_out_ref[l, :, 1, c:c + FFN_CHUNK] = g
    y = x + _dot(act[...], wdn_ref[...])
    o_ref[...] = _rms(y, gf_ref[...]) if final else y


def _ffn_sample(l, x1, ox, wo, g3, wup, cw, cb, wdn, gf, hist_all, prev, *, final):
    bsz, d = x1.shape
    dff = wdn.shape[1]
    full = lambda a: pl.BlockSpec(a.shape, lambda i: (0,) * a.ndim)
    hshape = hist_all.shape[1:]
    return pl.pallas_call(
        functools.partial(_ffn_sample_kernel, final=final),
        grid=(1,),
        in_specs=[full(x1), full(ox), _weight_spec(wo, l), _weight_spec(g3, l), _weight_spec(wup, l),
                  _weight_spec(cw, l), _weight_spec(cb, l),
                  _weight_spec(wdn, l), full(gf), pl.BlockSpec((None,) + hshape, lambda i: (l, 0, 0, 0))]
                 + [full(p) for p in prev],
        out_specs=[pl.BlockSpec((bsz, d), lambda i: (0, 0)), pl.BlockSpec((l + 1,) + hshape, lambda i: (0, 0, 0, 0))],
        out_shape=[jax.ShapeDtypeStruct((bsz, d), F32), jax.ShapeDtypeStruct((l + 1,) + hshape, F32)],
        scratch_shapes=[pltpu.VMEM((bsz, dff), BF16)],
        compiler_params=_params("arbitrary"),
        name="ffn_sample",
    )(x1, ox, wo, g3, wup, cw, cb, wdn, gf, hist_all, *prev)


def _t5_buckets(dist):
    n = np.maximum(dist, 0)
    max_exact = N_BUCKETS // 2
    nf = np.maximum(n, max_exact).astype(np.float32)
    large = max_exact + (np.log(nf / np.float32(max_exact)) / np.float32(math.log(MAX_DISTANCE / max_exact))
                         * np.float32(N_BUCKETS - max_exact)).astype(np.int32)
    return np.where(n < max_exact, n, np.minimum(large, N_BUCKETS - 1))


def _prompt_buckets():
    W = WINDOW
    dist = np.arange(W)[None, :] + W - np.arange(2 * W)[:, None]
    band = (dist >= 0) & (dist < W)
    return np.where(band, _t5_buckets(dist), -1).astype(np.int32)


def _swa_tables(rel_bias):
    W = WINDOW
    dist_c = W - np.arange(W)
    tab = jnp.transpose(rel_bias[_t5_buckets(dist_c)], (1, 0))
    cache_bias = jnp.where((dist_c < W)[None], tab, NEG_INF)
    return cache_bias, rel_bias[0]


def kernel(x_prompt, x_sample, mem_prompt, state_mlstm_C, state_mlstm_n, state_mlstm_m, state_conv, cache_swa_k, cache_swa_v, cache_mem_k, cache_mem_v, state_ffn_conv, rel_bias, norm1_g, w_in, b_i, b_f, mlstm_norm_g, conv_w, conv_b, conv_ln_g, conv_ln_b, swa_sinks, w_out, norm2_g, w_xq, w_xk, w_xv, w_xo, norm3_g, w_up, ffn_conv_w, ffn_conv_b, w_down, final_norm_g):
    depth = w_in.shape[0]
    bp, seq, d = x_prompt.shape
    bs = x_sample.shape[0]
    mem = mem_prompt.shape[1]
    dff = w_down.shape[1]
    W = WINDOW
    nh = C_KERNEL - 1

    xp = x_prompt
    xs = x_sample.reshape(bs, d)
    gf = final_norm_g.reshape(1, d)
    c_all = jnp.transpose(state_mlstm_C, (0, 2, 3, 4, 1))
    n_all = jnp.transpose(state_mlstm_n, (0, 2, 3, 1))
    m_all = jnp.transpose(state_mlstm_m, (0, 2, 1))
    hist_all = jnp.transpose(state_conv, (0, 2, 1, 3))
    kc_all = jnp.transpose(cache_swa_k, (0, 1, 3, 4, 2)).reshape(depth, bs, S_KV_WIDTH, W)
    vc_all = jnp.transpose(cache_swa_v, (0, 1, 3, 4, 2)).reshape(depth, bs, S_KV_WIDTH, W)
    mk_all = jnp.transpose(cache_mem_k, (0, 1, 3, 4, 2)).reshape(depth, bs, X_WIDTH, mem)
    mv_all = jnp.transpose(cache_mem_v, (0, 1, 3, 4, 2)).reshape(depth, bs, X_WIDTH, mem)
    pm_c, pm_n, pm_m, p_conv, p_k, p_v, p_mk, p_mv, p_ffn = ([] for _ in range(9))
    s_mlstm, s_conv, s_kv, s_ffn = [], [], [], []

    win_t = jnp.swapaxes(w_in, 1, 2)
    wout = w_out.astype(BF16)
    wxq = w_xq.astype(BF16)
    wxkv = jnp.concatenate([w_xk, w_xv], axis=2).astype(BF16)
    wxo = w_xo.astype(BF16)
    wup = w_up.astype(BF16)
    wdn = w_down.astype(BF16)
    bucket = jnp.asarray(_prompt_buckets())

    rows = lambda a: a.reshape(depth, 1, -1)
    g1, g2, g3 = rows(norm1_g), rows(norm2_g), rows(norm3_g)
    gb8 = jnp.concatenate([b_i, b_f], axis=1)
    gbias = rows(jnp.concatenate([gb8, jnp.zeros((depth, 128 - 2 * M_HEADS), F32)], axis=1))
    ng = rows(mlstm_norm_g)
    ngt = jnp.broadcast_to(mlstm_norm_g[:, :, None], (depth, M_WIDTH, bs))
    cw = jnp.concatenate([conv_w, jnp.zeros((depth, CONV_PAD - C_KERNEL, C_WIDTH), F32)], axis=1)
    cb, lg, lb = rows(conv_b), rows(conv_ln_g), rows(conv_ln_b)
    fcw = jnp.concatenate([ffn_conv_w, jnp.zeros((depth, SUBLANES - FFN_KERNEL, dff), F32)], axis=1)
    fcb = rows(ffn_conv_b)
    cache_bias, bias0 = _swa_tables(rel_bias)
    aux = jnp.concatenate([jnp.broadcast_to(bias0[None, :, None], (depth, S_Q_HEADS, 1)), swa_sinks[:, :, None],
                           jnp.zeros((depth, S_Q_HEADS, 126), F32)], axis=2)

    for l in range(depth):
        last = l == depth - 1

        mkv = _mem_kv(l, mem_prompt, wxkv)
        z, win = _in_proj(l, xp.reshape(bp * seq, d), g1, win_t, tm=1024)
        z = z.reshape(bp, seq, Z_WIDTH)
        hm, cpair, npair, mm = _mlstm_prompt(l, z, gbias, ng)
        hc, ctail = _conv_prompt(l, z, cw, cb, lg, lb, tc=min(1024, seq))
        hs, kt, vt = _swa_prompt(l, z, rel_bias, swa_sinks, bucket, qb=min(8, seq // W))
        xp = _mix_out_prompt(l, xp, hm, hc, hs, wout, g2, wxq, mkv, wxo, tm=1024)
        xp, ftail = _ffn_prompt(l, xp, g3, wup, fcw, fcb, wdn, gf, tm=512, final=last)
        half = lambda h: slice(HEAD_DIM * (h % 2), HEAD_DIM * (h % 2 + 1))
        pm_c.append(jnp.stack([jnp.swapaxes(cpair[:, h // 2, half(h), half(h)], 1, 2) for h in range(M_HEADS)], axis=1))
        pm_n.append(jnp.stack([npair[:, h // 2, h % 2, half(h)] for h in range(M_HEADS)], axis=1))
        pm_m.append(mm[:, :, 0, 0])
        p_conv.append(ctail[:, CONV_PAD - nh:, :])
        p_k.append(jnp.transpose(kt.reshape(bp, S_KV_HEADS, HEAD_DIM, W), (0, 3, 1, 2)))
        p_v.append(jnp.transpose(vt.reshape(bp, S_KV_HEADS, HEAD_DIM, W), (0, 3, 1, 2)))
        p_mk.append(jnp.transpose(mkv[:, 0:X_WIDTH, :].reshape(bp, X_HEADS, HEAD_DIM, mem), (0, 3, 1, 2)))
        p_mv.append(jnp.transpose(mkv[:, X_WIDTH:, :].reshape(bp, X_HEADS, HEAD_DIM, mem), (0, 3, 1, 2)))
        p_ffn.append(ftail[:, FFN_PAD - (FFN_KERNEL - 1):, :])

        zs, zst = _sample_in(l, xs, g1, win)
        hmt_s, *s_mlstm = _mlstm_sample(l, zst, gb8, ngt, c_all, n_all, m_all, s_mlstm)
        hc_s, *s_conv = _conv_sample(l, zs, hist_all, cw, cb, lg, lb, s_conv)
        hs_s, *s_kv = _swa_sample(l, zs, zs.reshape(bs, 1, Z_WIDTH), kc_all, vc_all, cache_bias, aux, s_kv)
        x1, qx = _mix_out_sample(l, xs, hmt_s, hc_s, hs_s.reshape(bs, S_WIDTH), wout, g2, wxq)
        ox = _xattn_sample(l, qx.reshape(bs, 1, X_WIDTH), mk_all, mv_all)
        xs, *s_ffn = _ffn_sample(l, x1, ox.reshape(bs, X_WIDTH), wxo, g3, wup, fcw, fcb, wdn, gf,
                                 state_ffn_conv, s_ffn, final=last)

    st = jnp.stack
    tr = jnp.transpose
    s_c, s_n, s_m = s_mlstm
    s_k, s_v = (a.reshape(depth, bs, S_KV_HEADS, HEAD_DIM, W) for a in s_kv)
    return (xp, xs.reshape(bs, 1, d),
            st(pm_c), st(pm_n), st(pm_m), st(p_conv), st(p_k), st(p_v), st(p_mk), st(p_mv), st(p_ffn),
            tr(s_c, (0, 4, 1, 2, 3)), tr(s_n, (0, 3, 1, 2)), tr(s_m, (0, 2, 1)),
            tr(s_conv[0], (0, 2, 1, 3)), tr(s_k, (0, 1, 4, 2, 3)), tr(s_v, (0, 1, 4, 2, 3)),
            s_ffn[0])
```

```python
import functools
import math

import numpy as np
import jax
import jax.numpy as jnp
from jax import lax
from jax.experimental import pallas as pl
from jax.experimental.pallas import tpu as pltpu

F32 = jnp.float32
BF16 = jnp.bfloat16
EPS = 1e-6
NEG_INF = float("-inf")

HEAD_DIM = 64
M_HEADS = 4
M_WIDTH = M_HEADS * HEAD_DIM
C_WIDTH = 256
C_KERNEL = 31
S_Q_HEADS = 8
S_KV_HEADS = 2
S_WIDTH = S_Q_HEADS * HEAD_DIM
S_KV_WIDTH = S_KV_HEADS * HEAD_DIM
WINDOW = 128
N_BUCKETS = 32
MAX_DISTANCE = 128
X_HEADS = 4
X_WIDTH = X_HEADS * HEAD_DIM
FFN_KERNEL = 3
QK_SCALE = HEAD_DIM ** -0.5
LOG2E = math.log2(math.e)

Z_M = 0
Z_C = 4 * M_WIDTH
Z_SQ = Z_C + 2 * C_WIDTH
Z_SKV = Z_SQ + S_WIDTH
Z_G = Z_SKV + 2 * S_KV_WIDTH
LANES = 128
SUBLANES = 8
Z_WIDTH = Z_G + LANES
VMEM_LIMIT = 56 * 1024 * 1024

M_CHUNK = 128
FFN_CHUNK = 256


def _params(*sem):
    return pltpu.CompilerParams(dimension_semantics=sem, vmem_limit_bytes=VMEM_LIMIT)


def _const_spec(shape):
    nd = len(shape)
    return pl.BlockSpec(shape, lambda *_: (0,) * nd, pipeline_mode=pl.Buffered(1))


def _weight_spec(w, l):
    nd = w.ndim - 1
    return pl.BlockSpec((None,) + w.shape[1:], lambda *_: (l,) + (0,) * nd, pipeline_mode=pl.Buffered(1))


def _rms(x, g):
    return x * lax.rsqrt(jnp.mean(x * x, axis=-1, keepdims=True) + EPS) * g


def _sigmoid(x):
    return 0.5 * jnp.tanh(0.5 * x) + 0.5


def _swish(x):
    h = 0.5 * x
    return h + h * jnp.tanh(h)


def _log_sigmoid(x):
    return jnp.minimum(x, 0.0) - jnp.log1p(jnp.exp(-jnp.abs(x)))


def _dot(a, b):
    return jnp.dot(a, b, preferred_element_type=F32)


def _dot_nt(a, b):
    return lax.dot_general(a, b, (((1,), (1,)), ((), ())), preferred_element_type=F32)


def _dot_tn(a, b):
    return lax.dot_general(a, b, (((0,), (0,)), ((), ())), preferred_element_type=F32)


IN_PROJ_ROWS = 512
IN_PROJ_COLS = 512
IN_GATES = 4 * M_WIDTH
W_PREP_ROWS = 256


def _in_proj_kernel(x_ref, g_ref, wt_ref, o_ref, wprep_ref, w_scr):
    d = x_ref.shape[1]

    @pl.when(pl.program_id(0) == 0)
    def _():
        n_gate = 2 * M_HEADS
        for src, dst, n in ((0, 0, IN_GATES), (IN_GATES + n_gate, IN_GATES, Z_G - IN_GATES)):
            for c in range(0, n, W_PREP_ROWS):
                w_scr[:, dst + c:dst + c + W_PREP_ROWS] = wt_ref[src + c:src + c + W_PREP_ROWS, :].T.astype(BF16)
        gate_rows = jnp.concatenate([wt_ref[IN_GATES:IN_GATES + n_gate, :], jnp.zeros((LANES - n_gate, d), F32)],
                                    axis=0)
        w_scr[:, Z_G:Z_G + LANES] = gate_rows.T.astype(BF16)
        wprep_ref[...] = w_scr[...]

    tm, n = o_ref.shape
    sub = min(IN_PROJ_ROWS, tm)
    for r0 in range(0, tm, sub):
        h = _rms(x_ref[r0:r0 + sub, :], g_ref[...]).astype(BF16)
        for c in range(0, n, IN_PROJ_COLS):
            w = min(IN_PROJ_COLS, n - c)
            o_ref[r0:r0 + sub, c:c + w] = _dot(h, w_scr[:, c:c + w])


def _in_proj(l, x, g, wt, *, tm):
    m, d = x.shape
    tm = min(tm, m)
    return pl.pallas_call(
        _in_proj_kernel,
        grid=(m // tm,),
        in_specs=[pl.BlockSpec((tm, d), lambda i: (i, 0)), _weight_spec(g, l), _weight_spec(wt, l)],
        out_specs=[pl.BlockSpec((tm, Z_WIDTH), lambda i: (i, 0)), pl.BlockSpec((d, Z_WIDTH), lambda i: (0, 0))],
        out_shape=[jax.ShapeDtypeStruct((m, Z_WIDTH), F32), jax.ShapeDtypeStruct((d, Z_WIDTH), BF16)],
        scratch_shapes=[pltpu.VMEM((d, Z_WIDTH), BF16)],
        compiler_params=_params("arbitrary"),
        name="in_proj",
    )(x, g, wt)


def _mem_kv_kernel(x_ref, w_ref, o_ref, acc):
    acc[...] = _dot(x_ref[...].astype(BF16), w_ref[...])
    o_ref[...] = acc[...].T


def _mem_kv(l, mem3, w):
    b, mem, d = mem3.shape
    n = w.shape[2]
    return pl.pallas_call(
        _mem_kv_kernel,
        grid=(b,),
        in_specs=[pl.BlockSpec((None, mem, d), lambda i: (i, 0, 0)), _weight_spec(w, l)],
        out_specs=pl.BlockSpec((None, n, mem), lambda i: (i, 0, 0)),
        out_shape=jax.ShapeDtypeStruct((b, n, mem), F32),
        scratch_shapes=[pltpu.VMEM((mem, n), F32)],
        compiler_params=_params("arbitrary"),
        name="mem_kv",
    )(mem3, w)


def _sample_in_kernel(x_ref, g_ref, w_ref, z_ref, zt_ref):
    h = _rms(x_ref[...], g_ref[...]).astype(BF16)
    n = z_ref.shape[1]
    for c in range(0, n, IN_PROJ_COLS):
        w = min(IN_PROJ_COLS, n - c)
        zc = _dot(h, w_ref[:, c:c + w])
        z_ref[:, c:c + w] = zc
        zt_ref[c:c + w, :] = zc.T


def _sample_in(l, x, g, w):
    m, d = x.shape
    n = w.shape[1]
    full = lambda shape: pl.BlockSpec(shape, lambda i: (0,) * len(shape))
    return pl.pallas_call(
        _sample_in_kernel,
        grid=(1,),
        in_specs=[full((m, d)), _weight_spec(g, l), full((d, n))],
        out_specs=[full((m, n)), full((n, m))],
        out_shape=[jax.ShapeDtypeStruct((m, n), F32), jax.ShapeDtypeStruct((n, m), F32)],
        compiler_params=_params("arbitrary"),
        name="sample_in",
    )(x, g, w)


def _mlstm_prompt_kernel(z_ref, g_ref, gb_ref, ng_ref, h_ref, cp_ref, np_ref, m_ref,
                         cp_scr, np_scr, m_scr, ht_scr, st_scr, kw_scr):
    NB, L = z_ref.shape[0], z_ref.shape[1]
    D = HEAD_DIM
    hi = lax.Precision.HIGHEST

    @pl.when(pl.program_id(0) == 0)
    def _():
        cp_scr[...] = jnp.zeros(cp_scr.shape, F32)
        np_scr[...] = jnp.zeros(np_scr.shape, F32)
        m_scr[...] = jnp.zeros(m_scr.shape, F32)

    src = lax.broadcasted_iota(jnp.int32, (L, L), 0)
    qry = lax.broadcasted_iota(jnp.int32, (L, L), 1)
    causal_t = src <= qry
    upper = jnp.where(causal_t, 1.0, 0.0)
    lane_half = lax.broadcasted_iota(jnp.int32, (L, 128), 1) // D
    row8 = lax.broadcasted_iota(jnp.int32, (SUBLANES, 128), 0)

    rows, cols = {}, []
    for b in range(NB):
        g_t = (g_ref[b] + gb_ref[...]).T[0:SUBLANES, :]
        b_rows = jnp.dot(_log_sigmoid(g_t), upper, precision=hi, preferred_element_type=F32)
        to_cols = []
        for h in range(M_HEADS):
            b_row = b_rows[M_HEADS + h:M_HEADS + h + 1, :]
            ci_row = g_t[h:h + 1, :] - b_row
            m_prev = m_scr[b, h, 0:1, 0:1]
            b_last = b_row[:, L - 1:L]
            m_new = jnp.maximum(b_last + m_prev, jnp.max(b_last + ci_row, axis=1, keepdims=True))
            rows[b, h] = dict(b_row=b_row, m_prev=m_prev, m_new=m_new, decay=jnp.exp(b_last + m_prev - m_new))
            to_cols += [ci_row, jnp.exp(b_last + ci_row - m_new)]
        to_cols.append(jnp.zeros((128 - 2 * M_HEADS, L), F32))
        cols.append(jnp.concatenate(to_cols, axis=0).T)

    stats = {}
    for b in range(NB):
        for j in range(M_HEADS // 2):
            slab = 128 * j
            q2 = z_ref[b, :, slab:slab + 128].astype(BF16)
            k2 = z_ref[b, :, M_WIDTH + slab:M_WIDTH + slab + 128] * QK_SCALE
            zero = jnp.zeros_like(q2)
            q_own = jnp.concatenate([jnp.where(lane_half == 0, q2, zero), jnp.where(lane_half == 1, q2, zero)], axis=0)
            qk = _dot_nt(k2.astype(BF16), q_own)
            for par in range(2):
                h = 2 * j + par
                r = rows[b, h]
                ci_col = cols[b][:, 2 * h:2 * h + 1]
                a_row = r["b_row"] + r["m_prev"]
                dm = jnp.where(causal_t, r["b_row"] + ci_col, NEG_INF)
                m_row = jnp.maximum(a_row, jnp.max(dm, axis=0, keepdims=True))
                s_t = qk[:, L * par:L * (par + 1)] * jnp.exp(dm - m_row)
                st_scr[b, j, :, L * par:L * (par + 1)] = s_t.astype(BF16)
                stats[b, h] = dict(w_inter=jnp.exp(a_row - m_row), floor=jnp.exp(-m_row),
                                   den_s=jnp.sum(s_t, axis=0, keepdims=True),
                                   decay=r["decay"], m_new=r["m_new"])
            wk = jnp.where(lane_half == 0, cols[b][:, 4 * j + 1:4 * j + 2], cols[b][:, 4 * j + 3:4 * j + 4])
            kw = k2 * wk
            kw_scr[b, j] = kw.astype(BF16)
            stats[b, j, "k_sum"] = jnp.sum(kw, axis=0, keepdims=True)

    block_diag = (lax.broadcasted_iota(jnp.int32, (128, 128), 0) // D
                  == lax.broadcasted_iota(jnp.int32, (128, 128), 1) // D)
    for b in range(NB):
        for j in range(M_HEADS // 2):
            slab = 128 * j
            q2 = z_ref[b, :, slab:slab + 128].astype(BF16)
            v2 = z_ref[b, :, 2 * M_WIDTH + slab:2 * M_WIDTH + slab + 128].astype(BF16)
            cp = cp_scr[b, j]
            npair = np_scr[b, j]
            cq = _dot_nt(jnp.concatenate([cp, npair], axis=0).astype(BF16), q2)
            qc = cq[0:128, :]
            qn = cq[128:128 + SUBLANES, :]
            pu = _dot_tn(v2, jnp.concatenate([st_scr[b, j], kw_scr[b, j]], axis=1))
            pv = pu[:, 0:2 * L]
            upd = jnp.where(block_diag, pu[:, 2 * L:2 * L + 128], 0.0)
            n_new = jnp.zeros((SUBLANES, 128), F32)
            for par in range(2):
                h = 2 * j + par
                st = stats[b, h]
                hr = slice(D * par, D * (par + 1))
                num = st["w_inter"] * qc[hr, :] + pv[hr, L * par:L * (par + 1)]
                den = st["w_inter"] * qn[par:par + 1, :] + st["den_s"]
                hh = num * (1.0 / jnp.maximum(jnp.abs(den), st["floor"]))
                hh = hh * lax.rsqrt(jnp.mean(hh * hh, axis=0, keepdims=True) + EPS)
                ht_scr[b, D * h:D * (h + 1), :] = hh
                cp_scr[b, j, hr, :] = st["decay"] * cp[hr, :] + upd[hr, :]
                k_sum = jnp.where(lane_half[0:1, :] == par, stats[b, j, "k_sum"], 0.0)
                n_new = jnp.where(row8 == par, st["decay"] * npair[par:par + 1, :] + k_sum, n_new)
                m_scr[b, h] = jnp.broadcast_to(st["m_new"], m_scr.shape[2:])
            np_scr[b, j] = n_new
        o_gate = _sigmoid(z_ref[b, :, 3 * M_WIDTH:4 * M_WIDTH])
        h_ref[b] = ht_scr[b].T * ng_ref[...] * o_gate

    cp_ref[...] = cp_scr[...]
    np_ref[...] = np_scr[...]
    m_ref[...] = m_scr[...]


def _mlstm_prompt(l, z3, gbias, ng):
    b, s, _ = z3.shape
    L = M_CHUNK
    P = M_HEADS // 2
    return pl.pallas_call(
        _mlstm_prompt_kernel,
        grid=(s // L,),
        in_specs=[
            pl.BlockSpec((b, L, 4 * M_WIDTH), lambda c: (0, c, Z_M // (4 * M_WIDTH))),
            pl.BlockSpec((b, L, 128), lambda c: (0, c, Z_G // 128)),
            _weight_spec(gbias, l),
            _weight_spec(ng, l),
        ],
        out_specs=[
            pl.BlockSpec((b, L, M_WIDTH), lambda c: (0, c, 0)),
            pl.BlockSpec((b, P, 128, 128), lambda c: (0, 0, 0, 0)),
            pl.BlockSpec((b, P, SUBLANES, 128), lambda c: (0, 0, 0, 0)),
            pl.BlockSpec((b, M_HEADS, SUBLANES, 128), lambda c: (0, 0, 0, 0)),
        ],
        out_shape=[
            jax.ShapeDtypeStruct((b, s, M_WIDTH), F32),
            jax.ShapeDtypeStruct((b, P, 128, 128), F32),
            jax.ShapeDtypeStruct((b, P, SUBLANES, 128), F32),
            jax.ShapeDtypeStruct((b, M_HEADS, SUBLANES, 128), F32),
        ],
        scratch_shapes=[
            pltpu.VMEM((b, P, 128, 128), F32),
            pltpu.VMEM((b, P, SUBLANES, 128), F32),
            pltpu.VMEM((b, M_HEADS, SUBLANES, 128), F32),
            pltpu.VMEM((b, M_WIDTH, L), F32),
            pltpu.VMEM((b, P, L, 2 * L), BF16),
            pltpu.VMEM((b, P, L, 128), BF16),
        ],
        compiler_params=_params("arbitrary"),
        name="mlstm_prompt",
    )(z3, z3, gbias, ng)


CONV_PAD = 32


def _conv_prompt_kernel(z_ref, w_ref, cb_ref, lg_ref, lb_ref, h_ref, tail_ref, buf, shifted):
    tc = z_ref.shape[0]
    t = pl.program_id(1)

    @pl.when(t == 0)
    def _():
        buf[0:CONV_PAD, :] = jnp.zeros((CONV_PAD, C_WIDTH), F32)

    @pl.when(t > 0)
    def _():
        buf[0:CONV_PAD, :] = buf[tc:tc + CONV_PAD, :]

    u = z_ref[:, 0:C_WIDTH] * _sigmoid(z_ref[:, C_WIDTH:2 * C_WIDTH])
    buf[CONV_PAD:CONV_PAD + tc, :] = u
    n_sh = shifted.shape[1]
    for r in range(1, SUBLANES):
        shifted[r - 1] = buf[r:r + n_sh, :]
    off = CONV_PAD - (C_KERNEL - 1)
    acc = jnp.broadcast_to(cb_ref[...], (tc, C_WIDTH))
    for j in range(C_KERNEL):
        r = (off + j) % SUBLANES
        base = off + j - r
        win = buf[base:base + tc, :] if r == 0 else shifted[r - 1, base:base + tc, :]
        acc = acc + w_ref[j:j + 1, :] * win
    mu = jnp.mean(acc, axis=-1, keepdims=True)
    xc = acc - mu
    y = xc * lax.rsqrt(jnp.mean(xc * xc, axis=-1, keepdims=True) + EPS) * lg_ref[...] + lb_ref[...]
    h_ref[...] = _swish(y)
    tail_ref[...] = buf[tc:tc + CONV_PAD, :]


def _conv_prompt(l, z3, w, cb, lg, lb, *, tc):
    b, s, _ = z3.shape
    return pl.pallas_call(
        _conv_prompt_kernel,
        grid=(b, s // tc),
        in_specs=[
            pl.BlockSpec((None, tc, 2 * C_WIDTH), lambda i, t: (i, t, Z_C // (2 * C_WIDTH))),
            _weight_spec(w, l), _weight_spec(cb, l), _weight_spec(lg, l), _weight_spec(lb, l),
        ],
        out_specs=[
            pl.BlockSpec((None, tc, C_WIDTH), lambda i, t: (i, t, 0)),
            pl.BlockSpec((None, CONV_PAD, C_WIDTH), lambda i, t: (i, 0, 0)),
        ],
        out_shape=[
            jax.ShapeDtypeStruct((b, s, C_WIDTH), F32),
            jax.ShapeDtypeStruct((b, CONV_PAD, C_WIDTH), F32),
        ],
        scratch_shapes=[pltpu.VMEM((CONV_PAD + tc, C_WIDTH), F32),
                        pltpu.VMEM((SUBLANES - 1, CONV_PAD + tc - SUBLANES, C_WIDTH), F32)],
        compiler_params=_params("arbitrary", "arbitrary"),
        name="conv_prompt",
    )(z3, w, cb, lg, lb)


def _swa_prompt_kernel(rb_ref, sink_ref, bucket_ref, q_ref, kv_ref, kvp_ref, o_ref, kt_ref, vt_ref,
                       bias_scr, s_scr, p_scr, ot_scr, *, layer):
    W = WINDOW
    H = S_Q_HEADS
    G = H // S_KV_HEADS
    i = pl.program_id(0)
    n = pl.program_id(1)

    @pl.when(jnp.logical_and(i == 0, n == 0))
    def _():
        bucket = bucket_ref[...]
        prev_key = lax.broadcasted_iota(jnp.int32, (2 * W, W), 0) < W
        for h in range(H):
            acc = jnp.full((2 * W, W), NEG_INF, F32)
            for b in range(N_BUCKETS):
                acc = jnp.where(bucket == b, rb_ref[b, h], acc)
            acc = acc * LOG2E
            bias_scr[0, h] = acc
            bias_scr[1, h] = jnp.where(prev_key, NEG_INF, acc)

    QB = q_ref.shape[0] // W
    NK = (QB + 1) * W
    first = jnp.where(n == 0, 1, 0)
    kk = jnp.concatenate([kvp_ref[:, 0:S_KV_WIDTH], kv_ref[:, 0:S_KV_WIDTH]], axis=0) * (QK_SCALE * LOG2E)
    vv = jnp.concatenate([kvp_ref[:, S_KV_WIDTH:2 * S_KV_WIDTH], kv_ref[:, S_KV_WIDTH:2 * S_KV_WIDTH]], axis=0)
    kk_r = pltpu.roll(kk, HEAD_DIM, axis=1)
    lo_lane = lax.broadcasted_iota(jnp.int32, (NK, S_KV_WIDTH), 1) < HEAD_DIM
    k_var = [[jnp.where(lo_lane, kk, 0.0).astype(BF16), jnp.where(lo_lane, 0.0, kk_r).astype(BF16)],
             [jnp.where(lo_lane, kk_r, 0.0).astype(BF16), jnp.where(lo_lane, 0.0, kk).astype(BF16)]]
    v_t = vv.T.astype(BF16)

    for j in range(QB):
        k0 = j * W
        masked = first if j == 0 else 0
        m_rows = [None] * H
        for hk in range(S_KV_HEADS):
            c0 = 2 * 128 * hk
            q_st = jnp.concatenate([q_ref[k0:k0 + W, c0:c0 + 128], q_ref[k0:k0 + W, c0 + 128:c0 + 256]],
                                   axis=0).astype(BF16)
            for half in range(2):
                s_t = _dot_nt(k_var[hk][half][k0:k0 + 2 * W, :], q_st)
                for slab in range(2):
                    head = G * hk + 2 * slab + half
                    sb = s_t[:, 128 * slab:128 * (slab + 1)] + bias_scr[masked, head]
                    s_scr[j, head] = sb
                    m_rows[head] = jnp.maximum(jnp.max(sb, axis=0, keepdims=True), sink_ref[layer, head] * LOG2E)

        inv = [None] * H
        for head in range(H):
            e = jnp.exp2(s_scr[j, head] - m_rows[head])
            den = jnp.sum(e, axis=0, keepdims=True) + jnp.exp2(sink_ref[layer, head] * LOG2E - m_rows[head])
            inv[head] = 1.0 / den
            p_scr[j, :, 128 * head:128 * (head + 1)] = e.astype(BF16)

        for hk in range(S_KV_HEADS):
            o_t = _dot(v_t[HEAD_DIM * hk:HEAD_DIM * (hk + 1), k0:k0 + 2 * W],
                       p_scr[j, :, 128 * G * hk:128 * G * (hk + 1)])
            for g in range(G):
                head = G * hk + g
                ot_scr[j, HEAD_DIM * head:HEAD_DIM * (head + 1), :] = o_t[:, 128 * g:128 * (g + 1)] * inv[head]
        o_ref[k0:k0 + W, :] = ot_scr[j].T

    @pl.when(n == pl.num_programs(1) - 1)
    def _():
        kt_ref[...] = kv_ref[(QB - 1) * W:QB * W, 0:S_KV_WIDTH].T
        vt_ref[...] = kv_ref[(QB - 1) * W:QB * W, S_KV_WIDTH:2 * S_KV_WIDTH].T


def _swa_prompt(l, z3, rel_bias, sinks, bucket, *, qb):
    b, s, _ = z3.shape
    W = WINDOW
    smem = pl.BlockSpec(memory_space=pltpu.SMEM)
    return pl.pallas_call(
        functools.partial(_swa_prompt_kernel, layer=l),
        grid=(b, s // (qb * W)),
        in_specs=[
            smem, smem, _const_spec((2 * W, W)),
            pl.BlockSpec((None, qb * W, S_WIDTH), lambda i, n: (i, n, Z_SQ // S_WIDTH)),
            pl.BlockSpec((None, qb * W, 2 * S_KV_WIDTH), lambda i, n: (i, n, Z_SKV // (2 * S_KV_WIDTH))),
            pl.BlockSpec((None, W, 2 * S_KV_WIDTH),
                         lambda i, n: (i, jnp.maximum(n * qb - 1, 0), Z_SKV // (2 * S_KV_WIDTH))),
        ],
        out_specs=[
            pl.BlockSpec((None, qb * W, S_WIDTH), lambda i, n: (i, n, 0)),
            pl.BlockSpec((None, S_KV_WIDTH, W), lambda i, n: (i, 0, 0)),
            pl.BlockSpec((None, S_KV_WIDTH, W), lambda i, n: (i, 0, 0)),
        ],
        out_shape=[
            jax.ShapeDtypeStruct((b, s, S_WIDTH), F32),
            jax.ShapeDtypeStruct((b, S_KV_WIDTH, W), F32),
            jax.ShapeDtypeStruct((b, S_KV_WIDTH, W), F32),
        ],
        scratch_shapes=[
            pltpu.VMEM((2, S_Q_HEADS, 2 * W, W), F32),
            pltpu.VMEM((qb, S_Q_HEADS, 2 * W, W), F32),
            pltpu.VMEM((qb, 2 * W, S_Q_HEADS * W), BF16),
            pltpu.VMEM((qb, S_WIDTH, W), F32),
        ],
        compiler_params=_params("arbitrary", "arbitrary"),
        name="swa_prompt",
    )(rel_bias, sinks, bucket, z3, z3, z3)


MIX_SUB = 512


def _mix_out_prompt_kernel(x_ref, hm_ref, hc_ref, hs_ref, wout_ref, g2_ref, wq_ref, mk_ref, mv_ref, wo_ref, o_ref,
                           x1_scr, qx_scr, p_scr):
    tm = x_ref.shape[0]
    mem = mk_ref.shape[1]
    mk_t = mk_ref[...]
    mv_t = mv_ref[...]
    row_head = lax.broadcasted_iota(jnp.int32, mk_t.shape, 0) // HEAD_DIM
    k_heads = [jnp.where(row_head == h, mk_t, 0.0).astype(BF16) for h in range(X_HEADS)]
    v_cat = jnp.concatenate([jnp.where(row_head == h, mv_t, 0.0).astype(BF16) for h in range(X_HEADS)], axis=1)
    subs = [slice(r0, r0 + MIX_SUB) for r0 in range(0, tm, MIX_SUB)]
    for rows in subs:
        cat = jnp.concatenate([hm_ref[rows, :].astype(BF16), hc_ref[rows, :].astype(BF16),
                               hs_ref[rows, :].astype(BF16)], axis=1)
        x1 = x_ref[rows, :] + _dot(cat, wout_ref[...])
        x1_scr[rows, :] = x1
        qx = _dot(_rms(x1, g2_ref[...]).astype(BF16), wq_ref[...])
        qx_scr[rows, :] = (qx * (QK_SCALE * LOG2E)).astype(BF16)
    for rows in subs:
        qx = qx_scr[rows, :]
        for h in range(X_HEADS):
            s = _dot(qx, k_heads[h])
            e = jnp.exp2(s - jnp.max(s, axis=1, keepdims=True))
            p_scr[rows, mem * h:mem * (h + 1)] = (e * (1.0 / jnp.sum(e, axis=1, keepdims=True))).astype(BF16)
    for rows in subs:
        o = _dot_nt(p_scr[rows, :], v_cat)
        o_ref[rows, :] = x1_scr[rows, :] + _dot(o.astype(BF16), wo_ref[...])


def _mix_out_prompt(l, x3, hm, hc, hs, wout, g2, wq, mkv, wo, *, tm):
    b, s, d = x3.shape
    tm = min(tm, s)
    mem = mkv.shape[2]
    row = lambda w: pl.BlockSpec((None, tm, w), lambda i, t: (i, t, 0))
    return pl.pallas_call(
        _mix_out_prompt_kernel,
        grid=(b, s // tm),
        in_specs=[
            row(d), row(M_WIDTH), row(C_WIDTH), row(S_WIDTH),
            _weight_spec(wout, l), _weight_spec(g2, l), _weight_spec(wq, l),
            pl.BlockSpec((None, X_WIDTH, mem), lambda i, t: (i, 0, 0)),
            pl.BlockSpec((None, X_WIDTH, mem), lambda i, t: (i, 1, 0)),
            _weight_spec(wo, l),
        ],
        out_specs=row(d),
        out_shape=jax.ShapeDtypeStruct((b, s, d), F32),
        scratch_shapes=[pltpu.VMEM((tm, d), F32), pltpu.VMEM((tm, X_WIDTH), BF16),
                        pltpu.VMEM((tm, X_HEADS * mem), BF16)],
        compiler_params=_params("arbitrary", "arbitrary"),
        name="mix_out_prompt",
    )(x3, hm, hc, hs, wout, g2, wq, mkv, mkv, wo)


FFN_PAD = 8


def _ffn_prompt_kernel(x_ref, g3_ref, wup_ref, cw_ref, cb_ref, wdn_ref, gf_ref, o_ref, tail_ref,
                       carry, act, *, final):
    tm = x_ref.shape[0]
    dff = wdn_ref.shape[0]

    @pl.when(pl.program_id(1) == 0)
    def _():
        carry[...] = jnp.zeros(carry.shape, F32)

    x = x_ref[...]
    h = _rms(x, g3_ref[...]).astype(BF16)
    w = FFN_CHUNK
    row8 = lax.broadcasted_iota(jnp.int32, (FFN_PAD, w), 0)
    for c in range(0, dff, w):
        a = _dot(h, wup_ref[:, c:c + w])
        g = _dot(h, wup_ref[:, dff + c:dff + c + w])
        prev = carry[:, c:c + w]
        carry[:, c:c + w] = g[tm - FFN_PAD:tm, :]
        taps = []
        for s in (2, 1):
            gs = pltpu.roll(g, s, axis=0)
            head = jnp.where(row8 < s, pltpu.roll(prev, s, axis=0), gs[0:FFN_PAD, :])
            taps.append(jnp.concatenate([head, gs[FFN_PAD:, :]], axis=0))
        gc = (cw_ref[0:1, c:c + w] * taps[0] + cw_ref[1:2, c:c + w] * taps[1]
              + cw_ref[2:3, c:c + w] * g + cb_ref[:, c:c + w])
        act[:, c:c + w] = (_swish(gc) * a).astype(BF16)
    y = x + _dot(act[...], wdn_ref[...])
    o_ref[...] = _rms(y, gf_ref[...]) if final else y
    tail_ref[...] = carry[...]


def _ffn_prompt(l, x3, g3, wup, cw, cb, wdn, gf, *, tm, final):
    b, s, d = x3.shape
    tm = min(tm, s)
    dff = wdn.shape[1]
    return pl.pallas_call(
        functools.partial(_ffn_prompt_kernel, final=final),
        grid=(b, s // tm),
        in_specs=[
            pl.BlockSpec((None, tm, d), lambda i, t: (i, t, 0)),
            _weight_spec(g3, l), _weight_spec(wup, l), _weight_spec(cw, l), _weight_spec(cb, l),
            _weight_spec(wdn, l), _const_spec((1, d)),
        ],
        out_specs=[
            pl.BlockSpec((None, tm, d), lambda i, t: (i, t, 0)),
            pl.BlockSpec((None, FFN_PAD, dff), lambda i, t: (i, 0, 0)),
        ],
        out_shape=[jax.ShapeDtypeStruct((b, s, d), F32), jax.ShapeDtypeStruct((b, FFN_PAD, dff), F32)],
        scratch_shapes=[
            pltpu.VMEM((FFN_PAD, dff), F32),
            pltpu.VMEM((tm, dff), BF16),
        ],
        compiler_params=_params("arbitrary", "arbitrary"),
        name="ffn_prompt",
    )(x3, g3, wup, cw, cb, wdn, gf)


SWA_SAMPLE_BLOCK = 32
XATTN_SAMPLE_BLOCK = 16
CONV_SAMPLE_BLOCK = 32


def _mlstm_sample_kernel(gb_ref, q_ref, k_ref, v_ref, o_ref, g_ref, ng_ref, c_ref, n_ref, m_ref, *rest, layer):
    h_ref, co_ref, no_ref, mo_ref, kw_scr = rest[-5:]
    _copy_earlier_layers(rest[:-5], (co_ref, no_ref, mo_ref))
    h = pl.program_id(0)
    i_pre = g_ref[pl.ds(h, 1), :] + gb_ref[layer, h]
    f_pre = g_ref[pl.ds(M_HEADS + h, 1), :] + gb_ref[layer, M_HEADS + h]
    a = _log_sigmoid(f_pre) + m_ref[pl.ds(h, 1), :]
    m_t = jnp.maximum(a, i_pre)
    w_old = jnp.exp(a - m_t)
    w_new = jnp.exp(i_pre - m_t)
    q = q_ref[...]
    k = k_ref[...] * QK_SCALE
    v = v_ref[...]
    n_old = n_ref[...]
    kw_scr[...] = k * w_new

    def body(d, acc):
        c_old = c_ref[d]
        co_ref[layer, d] = w_old * c_old + kw_scr[pl.ds(d, 1), :] * v
        return acc + q_ref[pl.ds(d, 1), :] * c_old

    qc = lax.fori_loop(0, HEAD_DIM, body, jnp.zeros(v.shape, F32), unroll=8)
    s = jnp.sum(q * k, axis=0, keepdims=True) * w_new
    num = w_old * qc + s * v
    den = w_old * jnp.sum(q * n_old, axis=0, keepdims=True) + s
    hh = num / jnp.maximum(jnp.abs(den), jnp.exp(-m_t))
    hh = hh * lax.rsqrt(jnp.mean(hh * hh, axis=0, keepdims=True) + EPS) * ng_ref[...]
    h_ref[...] = hh * _sigmoid(o_ref[...])
    no_ref[layer] = w_old * n_old + kw_scr[...]
    mo_ref[layer, pl.ds(h, 1), :] = m_t


def _mlstm_sample(l, zt, gb, ngt, c_all, n_all, m_all, prev):
    bsz = zt.shape[1]
    D = HEAD_DIM
    feat = lambda off: pl.BlockSpec((D, bsz), lambda h: (off // D + h, 0))
    c_spec = lambda n: pl.BlockSpec((n, None, D, D, bsz), lambda h: (0, h, 0, 0, 0))
    n_spec = lambda n: pl.BlockSpec((n, None, D, bsz), lambda h: (0, h, 0, 0))
    m_spec = lambda n: pl.BlockSpec((n, M_HEADS, bsz), lambda h: (0, 0, 0))
    prev_specs = [c_spec(l), n_spec(l), m_spec(l)] if prev else []
    return pl.pallas_call(
        functools.partial(_mlstm_sample_kernel, layer=l),
        grid=(M_HEADS,),
        in_specs=[
            pl.BlockSpec(memory_space=pltpu.SMEM),
            feat(Z_M), feat(Z_M + M_WIDTH), feat(Z_M + 2 * M_WIDTH), feat(Z_M + 3 * M_WIDTH),
            pl.BlockSpec((SUBLANES, bsz), lambda h: (Z_G // SUBLANES, 0)),
            pl.BlockSpec((None, D, bsz), lambda h: (l, h, 0)),
            pl.BlockSpec((None, None, D, D, bsz), lambda h: (l, h, 0, 0, 0)),
            pl.BlockSpec((None, None, D, bsz), lambda h: (l, h, 0, 0)),
            pl.BlockSpec((None, M_HEADS, bsz), lambda h: (l, 0, 0)),
        ] + prev_specs,
        out_specs=[pl.BlockSpec((D, bsz), lambda h: (h, 0)), c_spec(l + 1), n_spec(l + 1), m_spec(l + 1)],
        out_shape=[
            jax.ShapeDtypeStruct((M_WIDTH, bsz), F32),
            jax.ShapeDtypeStruct((l + 1, M_HEADS, D, D, bsz), F32),
            jax.ShapeDtypeStruct((l + 1, M_HEADS, D, bsz), F32),
            jax.ShapeDtypeStruct((l + 1, M_HEADS, bsz), F32),
        ],
        scratch_shapes=[pltpu.VMEM((D, bsz), F32)],
        compiler_params=_params("arbitrary"),
        name="mlstm_sample",
    )(gb, zt, zt, zt, zt, zt, ngt, c_all, n_all, m_all, *prev)


def _conv_sample_kernel(z_ref, hist_ref, w_ref, cb_ref, lg_ref, lb_ref, *rest):
    h_ref, hist_out_ref = rest[-2:]
    _copy_earlier_layers(rest[:-2], (hist_out_ref,))
    l = hist_out_ref.shape[0] - 1
    nh = C_KERNEL - 1
    u = z_ref[:, 0:C_WIDTH] * _sigmoid(z_ref[:, C_WIDTH:2 * C_WIDTH])
    acc = cb_ref[...] + w_ref[nh:nh + 1, :] * u
    for j in range(nh):
        acc = acc + w_ref[j:j + 1, :] * hist_ref[j]
    mu = jnp.mean(acc, axis=-1, keepdims=True)
    xc = acc - mu
    y = xc * lax.rsqrt(jnp.mean(xc * xc, axis=-1, keepdims=True) + EPS) * lg_ref[...] + lb_ref[...]
    h_ref[...] = _swish(y)
    for j in range(nh - 1):
        hist_out_ref[l, j] = hist_ref[j + 1]
    hist_out_ref[l, nh - 1] = u


def _conv_sample(l, z, hist_all, w, cb, lg, lb, prev):
    bsz = z.shape[0]
    nh = hist_all.shape[1]
    R = min(CONV_SAMPLE_BLOCK, bsz)
    hist_spec = lambda n: pl.BlockSpec((n, nh, R, C_WIDTH), lambda i: (0, 0, i, 0))
    return pl.pallas_call(
        _conv_sample_kernel,
        grid=(bsz // R,),
        in_specs=[
            pl.BlockSpec((R, 2 * C_WIDTH), lambda i: (i, Z_C // (2 * C_WIDTH))),
            pl.BlockSpec((None, nh, R, C_WIDTH), lambda i: (l, 0, i, 0)),
            _weight_spec(w, l), _weight_spec(cb, l), _weight_spec(lg, l), _weight_spec(lb, l),
        ] + [hist_spec(l)] * len(prev),
        out_specs=[pl.BlockSpec((R, C_WIDTH), lambda i: (i, 0)), hist_spec(l + 1)],
        out_shape=[jax.ShapeDtypeStruct((bsz, C_WIDTH), F32), jax.ShapeDtypeStruct((l + 1, nh, bsz, C_WIDTH), F32)],
        compiler_params=_params("arbitrary"),
        name="conv_sample",
    )(z, hist_all, w, cb, lg, lb, *prev)


def _copy_earlier_layers(prev_refs, out_refs):
    for p_ref, o_ref in zip(prev_refs, out_refs):
        o_ref[0:p_ref.shape[0]] = p_ref[...]


def _swa_sample_kernel(q_ref, kvn_ref, kv2_ref, kc_ref, vc_ref, bias_ref, aux_ref, *rest):
    o_ref, ko_ref, vo_ref = rest[-3:]
    _copy_earlier_layers(rest[:-3], (ko_ref, vo_ref))
    l = ko_ref.shape[0] - 1
    R = q_ref.shape[0]
    W = kc_ref.shape[2]
    H = S_Q_HEADS
    shape = (R, H, 128)
    row = lax.broadcasted_iota(jnp.int32, shape, 1)
    lane_half = lax.broadcasted_iota(jnp.int32, shape, 2) // HEAD_DIM
    q_half = row % 2
    kv_head = row // (H // S_KV_HEADS)
    qs = jnp.zeros(shape, F32)
    for j in range(H // 2):
        qs = jnp.where(row // 2 == j, q_ref[:, :, 128 * j:128 * (j + 1)], qs)
    q8 = jnp.where(lane_half == kv_head, jnp.where(q_half == kv_head, qs, pltpu.roll(qs, HEAD_DIM, axis=2)), 0.0)
    k_new = kvn_ref[:, :, 0:S_KV_WIDTH]
    v_new = kvn_ref[:, :, S_KV_WIDTH:2 * S_KV_WIDTH]
    s = jnp.einsum("bqd,bdk->bqk", q8.astype(BF16), kc_ref[...].astype(BF16), preferred_element_type=F32) * QK_SCALE
    s = s + bias_ref[...][None]
    s_new = jnp.sum(q8 * k_new, axis=2, keepdims=True) * QK_SCALE + aux_ref[:, 0:1][None]
    sink = aux_ref[:, 1:2][None]
    m = jnp.maximum(jnp.maximum(jnp.max(s, axis=2, keepdims=True), s_new), sink)
    e = jnp.exp(s - m)
    e_new = jnp.exp(s_new - m)
    inv = 1.0 / (jnp.sum(e, axis=2, keepdims=True) + e_new + jnp.exp(sink - m))
    o8 = jnp.einsum("bqk,bdk->bqd", e.astype(BF16), vc_ref[...].astype(BF16), preferred_element_type=F32)
    o8 = (o8 + e_new * v_new) * inv
    o8 = jnp.where(lane_half == q_half, jnp.where(q_half == kv_head, o8, pltpu.roll(o8, HEAD_DIM, axis=2)), 0.0)
    for j in range(H // 2):
        o_ref[:, :, 128 * j:128 * (j + 1)] = jnp.sum(jnp.where(row // 2 == j, o8, 0.0), axis=1, keepdims=True)
    k_cols = kv2_ref[:, 0:S_KV_WIDTH].T
    v_cols = kv2_ref[:, S_KV_WIDTH:2 * S_KV_WIDTH].T
    last = lax.broadcasted_iota(jnp.int32, (S_KV_WIDTH, W), 1) == W - 1
    for r in range(R):
        ko_ref[l, r] = jnp.where(last, k_cols[:, r:r + 1], pltpu.roll(kc_ref[r], W - 1, axis=1))
        vo_ref[l, r] = jnp.where(last, v_cols[:, r:r + 1], pltpu.roll(vc_ref[r], W - 1, axis=1))


def _swa_sample(l, z, z3, kc_all, vc_all, bias, aux, prev):
    bsz = z.shape[0]
    R = min(SWA_SAMPLE_BLOCK, bsz)
    W = kc_all.shape[3]
    cache_in = pl.BlockSpec((None, R, S_KV_WIDTH, W), lambda i: (l, i, 0, 0))
    cache_prev = pl.BlockSpec((l, R, S_KV_WIDTH, W), lambda i: (0, i, 0, 0))
    cache_out = pl.BlockSpec((l + 1, R, S_KV_WIDTH, W), lambda i: (0, i, 0, 0))
    return pl.pallas_call(
        _swa_sample_kernel,
        grid=(bsz // R,),
        in_specs=[
            pl.BlockSpec((R, 1, S_WIDTH), lambda i: (i, 0, Z_SQ // S_WIDTH)),
            pl.BlockSpec((R, 1, 2 * S_KV_WIDTH), lambda i: (i, 0, Z_SKV // (2 * S_KV_WIDTH))),
            pl.BlockSpec((R, 2 * S_KV_WIDTH), lambda i: (i, Z_SKV // (2 * S_KV_WIDTH))),
            cache_in, cache_in,
            _const_spec((S_Q_HEADS, 128)), _weight_spec(aux, l),
        ] + [cache_prev] * len(prev),
        out_specs=[pl.BlockSpec((R, 1, S_WIDTH), lambda i: (i, 0, 0)), cache_out, cache_out],
        out_shape=[
            jax.ShapeDtypeStruct((bsz, 1, S_WIDTH), F32),
            jax.ShapeDtypeStruct((l + 1, bsz, S_KV_WIDTH, W), F32),
            jax.ShapeDtypeStruct((l + 1, bsz, S_KV_WIDTH, W), F32),
        ],
        compiler_params=_params("arbitrary"),
        name="swa_sample",
    )(z3, z3, z, kc_all, vc_all, bias, aux, *prev)


def _mix_out_sample_kernel(x_ref, hmt_ref, hc_ref, hs_ref, wout_ref, g2_ref, wq_ref, x1_ref, q_ref):
    cat = jnp.concatenate([hmt_ref[...].T.astype(BF16), hc_ref[...].astype(BF16), hs_ref[...].astype(BF16)], axis=1)
    x1 = x_ref[...] + _dot(cat, wout_ref[...])
    x1_ref[...] = x1
    q_ref[...] = _dot(_rms(x1, g2_ref[...]).astype(BF16), wq_ref[...])


def _mix_out_sample(l, x, hm, hc, hs, wout, g2, wq):
    bsz, d = x.shape
    full = lambda a: pl.BlockSpec(a.shape, lambda i: (0,) * a.ndim)
    args = (x, hm, hc, hs, wout, g2, wq)
    return pl.pallas_call(
        _mix_out_sample_kernel,
        grid=(1,),
        in_specs=[full(x), full(hm), full(hc), full(hs), _weight_spec(wout, l), _weight_spec(g2, l),
                  _weight_spec(wq, l)],
        out_specs=[pl.BlockSpec((bsz, d), lambda i: (0, 0)), pl.BlockSpec((bsz, X_WIDTH), lambda i: (0, 0))],
        out_shape=[jax.ShapeDtypeStruct((bsz, d), F32), jax.ShapeDtypeStruct((bsz, X_WIDTH), F32)],
        compiler_params=_params("arbitrary"),
        name="mix_out_sample",
    )(*args)


def _xattn_sample_kernel(q_ref, k_ref, v_ref, o_ref):
    R = k_ref.shape[0]
    shape = (R, SUBLANES, X_WIDTH)
    row = lax.broadcasted_iota(jnp.int32, shape, 1)
    lane_head = lax.broadcasted_iota(jnp.int32, shape, 2) // HEAD_DIM
    own = row == lane_head
    q8 = jnp.where(own, jnp.broadcast_to(q_ref[...], shape), 0.0).astype(BF16)
    s = jnp.einsum("bqd,bdk->bqk", q8, k_ref[...].astype(BF16), preferred_element_type=F32) * QK_SCALE
    e = jnp.exp(s - jnp.max(s, axis=2, keepdims=True))
    p = (e / jnp.sum(e, axis=2, keepdims=True)).astype(BF16)
    o8 = jnp.einsum("bqk,bdk->bqd", p, v_ref[...].astype(BF16), preferred_element_type=F32)
    o_ref[...] = jnp.sum(jnp.where(own, o8, 0.0), axis=1, keepdims=True)


def _xattn_sample(l, q3, k_all, v_all):
    _, bsz, w, mem = k_all.shape
    R = min(XATTN_SAMPLE_BLOCK, bsz)
    kv = pl.BlockSpec((None, R, w, mem), lambda i: (l, i, 0, 0))
    qo = pl.BlockSpec((R, 1, w), lambda i: (i, 0, 0))
    return pl.pallas_call(
        _xattn_sample_kernel,
        grid=(bsz // R,),
        in_specs=[qo, kv, kv],
        out_specs=qo,
        out_shape=jax.ShapeDtypeStruct((bsz, 1, w), F32),
        compiler_params=_params("arbitrary"),
        name="xattn_sample",
    )(q3, k_all, v_all)


def _ffn_sample_kernel(x1_ref, ox_ref, wo_ref, g3_ref, wup_ref, cw_ref, cb_ref, wdn_ref, gf_ref, hist_ref,
                       *rest, final):
    o_ref, hist_out_ref, act = rest[-3:]
    _copy_earlier_layers(rest[:-3], (hist_out_ref,))
    l = hist_out_ref.shape[0] - 1
    dff = wdn_ref.shape[0]
    x = x1_ref[...] + _dot(ox_ref[...].astype(BF16), wo_ref[...])
    h = _rms(x, g3_ref[...]).astype(BF16)
    for c in range(0, dff, FFN_CHUNK):
        a = _dot(h, wup_ref[:, c:c + FFN_CHUNK])
        g = _dot(h, wup_ref[:, dff + c:dff + c + FFN_CHUNK])
        h1 = hist_ref[:, 1, c:c + FFN_CHUNK]
        gc = (cw_ref[0:1, c:c + FFN_CHUNK] * hist_ref[:, 0, c:c + FFN_CHUNK]
              + cw_ref[1:2, c:c + FFN_CHUNK] * h1
              + cw_ref[2:3, c:c + FFN_CHUNK] * g + cb_ref[:, c:c + FFN_CHUNK])
        act[:, c:c + FFN_CHUNK] = (_swish(gc) * a).astype(BF16)
        hist_out_ref[l, :, 0, c:c + FFN_CHUNK] = h1
        hist_out_ref[l, :, 1, c:c + FFN_CHUNK] = g
    y = x + _dot(act[...], wdn_ref[...])
    o_ref[...] = _rms(y, gf_ref[...]) if final else y


def _ffn_sample(l, x1, ox, wo, g3, wup, cw, cb, wdn, gf, hist_all, prev, *, final):
    bsz, d = x1.shape
    dff = wdn.shape[1]
    full = lambda a: pl.BlockSpec(a.shape, lambda i: (0,) * a.ndim)
    hshape = hist_all.shape[1:]
    return pl.pallas_call(
        functools.partial(_ffn_sample_kernel, final=final),
        grid=(1,),
        in_specs=[full(x1), full(ox), _weight_spec(wo, l), _weight_spec(g3, l), _weight_spec(wup, l),
                  _weight_spec(cw, l), _weight_spec(cb, l),
                  _weight_spec(wdn, l), full(gf), pl.BlockSpec((None,) + hshape, lambda i: (l, 0, 0, 0))]
                 + [full(p) for p in prev],
        out_specs=[pl.BlockSpec((bsz, d), lambda i: (0, 0)), pl.BlockSpec((l + 1,) + hshape, lambda i: (0, 0, 0, 0))],
        out_shape=[jax.ShapeDtypeStruct((bsz, d), F32), jax.ShapeDtypeStruct((l + 1,) + hshape, F32)],
        scratch_shapes=[pltpu.VMEM((bsz, dff), BF16)],
        compiler_params=_params("arbitrary"),
        name="ffn_sample",
    )(x1, ox, wo, g3, wup, cw, cb, wdn, gf, hist_all, *prev)


def _t5_buckets(dist):
    n = np.maximum(dist, 0)
    max_exact = N_BUCKETS // 2
    nf = np.maximum(n, max_exact).astype(np.float32)
    large = max_exact + (np.log(nf / np.float32(max_exact)) / np.float32(math.log(MAX_DISTANCE / max_exact))
                         * np.float32(N_BUCKETS - max_exact)).astype(np.int32)
    return np.where(n < max_exact, n, np.minimum(large, N_BUCKETS - 1))


def _prompt_buckets():
    W = WINDOW
    dist = np.arange(W)[None, :] + W - np.arange(2 * W)[:, None]
    band = (dist >= 0) & (dist < W)
    return np.where(band, _t5_buckets(dist), -1).astype(np.int32)


def _swa_tables(rel_bias):
    W = WINDOW
    dist_c = W - np.arange(W)
    tab = jnp.transpose(rel_bias[_t5_buckets(dist_c)], (1, 0))
    cache_bias = jnp.where((dist_c < W)[None], tab, NEG_INF)
    return cache_bias, rel_bias[0]


def kernel(x_prompt, x_sample, mem_prompt, state_mlstm_C, state_mlstm_n, state_mlstm_m, state_conv, cache_swa_k, cache_swa_v, cache_mem_k, cache_mem_v, state_ffn_conv, rel_bias, norm1_g, w_in, b_i, b_f, mlstm_norm_g, conv_w, conv_b, conv_ln_g, conv_ln_b, swa_sinks, w_out, norm2_g, w_xq, w_xk, w_xv, w_xo, norm3_g, w_up, ffn_conv_w, ffn_conv_b, w_down, final_norm_g):
    depth = w_in.shape[0]
    bp, seq, d = x_prompt.shape
    bs = x_sample.shape[0]
    mem = mem_prompt.shape[1]
    dff = w_down.shape[1]
    W = WINDOW
    nh = C_KERNEL - 1

    xp = x_prompt
    xs = x_sample.reshape(bs, d)
    gf = final_norm_g.reshape(1, d)
    c_all = jnp.transpose(state_mlstm_C, (0, 2, 3, 4, 1))
    n_all = jnp.transpose(state_mlstm_n, (0, 2, 3, 1))
    m_all = jnp.transpose(state_mlstm_m, (0, 2, 1))
    hist_all = jnp.transpose(state_conv, (0, 2, 1, 3))
    kc_all = jnp.transpose(cache_swa_k, (0, 1, 3, 4, 2)).reshape(depth, bs, S_KV_WIDTH, W)
    vc_all = jnp.transpose(cache_swa_v, (0, 1, 3, 4, 2)).reshape(depth, bs, S_KV_WIDTH, W)
    mk_all = jnp.transpose(cache_mem_k, (0, 1, 3, 4, 2)).reshape(depth, bs, X_WIDTH, mem)
    mv_all = jnp.transpose(cache_mem_v, (0, 1, 3, 4, 2)).reshape(depth, bs, X_WIDTH, mem)
    pm_c, pm_n, pm_m, p_conv, p_k, p_v, p_mk, p_mv, p_ffn = ([] for _ in range(9))
    s_mlstm, s_conv, s_kv, s_ffn = [], [], [], []

    win_t = jnp.swapaxes(w_in, 1, 2)
    wout = w_out.astype(BF16)
    wxq = w_xq.astype(BF16)
    wxkv = jnp.concatenate([w_xk, w_xv], axis=2).astype(BF16)
    wxo = w_xo.astype(BF16)
    wup = w_up.astype(BF16)
    wdn = w_down.astype(BF16)
    bucket = jnp.asarray(_prompt_buckets())

    rows = lambda a: a.reshape(depth, 1, -1)
    g1, g2, g3 = rows(norm1_g), rows(norm2_g), rows(norm3_g)
    gb8 = jnp.concatenate([b_i, b_f], axis=1)
    gbias = rows(jnp.concatenate([gb8, jnp.zeros((depth, 128 - 2 * M_HEADS), F32)], axis=1))
    ng = rows(mlstm_norm_g)
    ngt = jnp.broadcast_to(mlstm_norm_g[:, :, None], (depth, M_WIDTH, bs))
    cw = jnp.concatenate([conv_w, jnp.zeros((depth, CONV_PAD - C_KERNEL, C_WIDTH), F32)], axis=1)
    cb, lg, lb = rows(conv_b), rows(conv_ln_g), rows(conv_ln_b)
    fcw = jnp.concatenate([ffn_conv_w, jnp.zeros((depth, SUBLANES - FFN_KERNEL, dff), F32)], axis=1)
    fcb = rows(ffn_conv_b)
    cache_bias, bias0 = _swa_tables(rel_bias)
    aux = jnp.concatenate([jnp.broadcast_to(bias0[None, :, None], (depth, S_Q_HEADS, 1)), swa_sinks[:, :, None],
                           jnp.zeros((depth, S_Q_HEADS, 126), F32)], axis=2)

    for l in range(depth):
        last = l == depth - 1

        mkv = _mem_kv(l, mem_prompt, wxkv)
        z, win = _in_proj(l, xp.reshape(bp * seq, d), g1, win_t, tm=1024)
        z = z.reshape(bp, seq, Z_WIDTH)
        hm, cpair, npair, mm = _mlstm_prompt(l, z, gbias, ng)
        hc, ctail = _conv_prompt(l, z, cw, cb, lg, lb, tc=min(1024, seq))
        hs, kt, vt = _swa_prompt(l, z, rel_bias, swa_sinks, bucket, qb=min(8, seq // W))
        xp = _mix_out_prompt(l, xp, hm, hc, hs, wout, g2, wxq, mkv, wxo, tm=1024)
        xp, ftail = _ffn_prompt(l, xp, g3, wup, fcw, fcb, wdn, gf, tm=512, final=last)
        half = lambda h: slice(HEAD_DIM * (h % 2), HEAD_DIM * (h % 2 + 1))
        pm_c.append(jnp.stack([jnp.swapaxes(cpair[:, h // 2, half(h), half(h)], 1, 2) for h in range(M_HEADS)], axis=1))
        pm_n.append(jnp.stack([npair[:, h // 2, h % 2, half(h)] for h in range(M_HEADS)], axis=1))
        pm_m.append(mm[:, :, 0, 0])
        p_conv.append(ctail[:, CONV_PAD - nh:, :])
        p_k.append(jnp.transpose(kt.reshape(bp, S_KV_HEADS, HEAD_DIM, W), (0, 3, 1, 2)))
        p_v.append(jnp.transpose(vt.reshape(bp, S_KV_HEADS, HEAD_DIM, W), (0, 3, 1, 2)))
        p_mk.append(jnp.transpose(mkv[:, 0:X_WIDTH, :].reshape(bp, X_HEADS, HEAD_DIM, mem), (0, 3, 1, 2)))
        p_mv.append(jnp.transpose(mkv[:, X_WIDTH:, :].reshape(bp, X_HEADS, HEAD_DIM, mem), (0, 3, 1, 2)))
        p_ffn.append(ftail[:, FFN_PAD - (FFN_KERNEL - 1):, :])

        zs, zst = _sample_in(l, xs, g1, win)
        hmt_s, *s_mlstm = _mlstm_sample(l, zst, gb8, ngt, c_all, n_all, m_all, s_mlstm)
        hc_s, *s_conv = _conv_sample(l, zs, hist_all, cw, cb, lg, lb, s_conv)
        hs_s, *s_kv = _swa_sample(l, zs, zs.reshape(bs, 1, Z_WIDTH), kc_all, vc_all, cache_bias, aux, s_kv)
        x1, qx = _mix_out_sample(l, xs, hmt_s, hc_s, hs_s.reshape(bs, S_WIDTH), wout, g2, wxq)
        ox = _xattn_sample(l, qx.reshape(bs, 1, X_WIDTH), mk_all, mv_all)
        xs, *s_ffn = _ffn_sample(l, x1, ox.reshape(bs, X_WIDTH), wxo, g3, wup, fcw, fcb, wdn, gf,
                                 state_ffn_conv, s_ffn, final=last)

    st = jnp.stack
    tr = jnp.transpose
    s_c, s_n, s_m = s_mlstm
    s_k, s_v = (a.reshape(depth, bs, S_KV_HEADS, HEAD_DIM, W) for a in s_kv)
    return (xp, xs.reshape(bs, 1, d),
            st(pm_c), st(pm_n), st(pm_m), st(p_conv), st(p_k), st(p_v), st(p_mk), st(p_mv), st(p_ffn),
            tr(s_c, (0, 4, 1, 2, 3)), tr(s_n, (0, 3, 1, 2)), tr(s_m, (0, 2, 1)),
            tr(s_conv[0], (0, 2, 1, 3)), tr(s_k, (0, 1, 4, 2, 3)), tr(s_v, (0, 1, 4, 2, 3)),
            s_ffn[0])
```
